```python
import jax, jax.numpy as jnp
from jax import lax
import numpy as np

D_MODEL = 1024
BATCH = 32
SEQ = 2048
DEPTH = 1

HEAD_DIM = 64
N_HEADS = D_MODEL // HEAD_DIM
N_HEADS_A = N_HEADS // 2
N_HEADS_B = N_HEADS - N_HEADS_A
WIDTH_A = N_HEADS_A * HEAD_DIM
WIDTH_B = N_HEADS_B * HEAD_DIM
DILATED_PATTERNS = ((128, 1), (512, 4), (2048, 16))
Q_BLOCK = 128
ROT_DIM = HEAD_DIM // 4
ROPE_THETA = 500000.0
D_FF = ((8 * D_MODEL // 3 + 63) // 64) * 64
N_MOD = 9
EPS = 1e-6
ATTN_SCALE = HEAD_DIM ** -0.5
NEG = -1e30
COL_SIZES = (WIDTH_A, WIDTH_A, WIDTH_A, WIDTH_B, WIDTH_B, WIDTH_B, N_HEADS_B)
COL_OFFSETS = tuple(int(o) for o in np.cumsum(COL_SIZES)[:-1])
IN_COLS = int(sum(COL_SIZES))

kernel_name = 'hybrid_dilated_fox_macaron_block'


def rmsnorm(x, g):
    xf = x.astype(jnp.float32)
    y = xf * lax.rsqrt(jnp.mean(xf * xf, axis=-1, keepdims=True) + EPS)
    return (y * g.astype(jnp.float32)).astype(x.dtype)


def partial_rotary(t, positions):
    inv_freq = ROPE_THETA ** (-jnp.arange(0, ROT_DIM, 2, dtype=jnp.float32) / ROT_DIM)
    ang = positions.astype(jnp.float32)[:, None, :, None] * inv_freq
    cos, sin = jnp.cos(ang), jnp.sin(ang)
    tf = t.astype(jnp.float32)
    x1 = tf[..., :ROT_DIM // 2]
    x2 = tf[..., ROT_DIM // 2:ROT_DIM]
    rot = jnp.concatenate([x1 * cos - x2 * sin, x2 * cos + x1 * sin, tf[..., ROT_DIM:]], axis=-1)
    return rot.astype(t.dtype)


def swiglu(h, w_gate, w_up, w_down):
    return (jax.nn.silu(h @ w_gate) * (h @ w_up)) @ w_down


def banded_causal_attention(q, k, v, w):
    L = q.shape[-2]
    lead = q.shape[:-2]
    nb = -(-L // Q_BLOCK)
    Lp = nb * Q_BLOCK
    pad = [(0, 0)] * (q.ndim - 2)
    qp = jnp.pad(q, pad + [(0, Lp - L), (0, 0)])
    kp = jnp.pad(k, pad + [(w, Lp - L), (0, 0)])
    vp = jnp.pad(v, pad + [(w, Lp - L), (0, 0)])
    span = Q_BLOCK + w
    q_blk = qp.reshape(*lead, nb, Q_BLOCK, HEAD_DIM)
    idx = jnp.arange(nb)[:, None] * Q_BLOCK + jnp.arange(span)[None, :]
    k_blk = jnp.take(kp, idx, axis=-2)
    v_blk = jnp.take(vp, idx, axis=-2)
    s = jnp.einsum('...nqd,...nkd->...nqk', q_blk, k_blk,
                   preferred_element_type=jnp.float32) * ATTN_SCALE
    dist = jnp.arange(Q_BLOCK)[:, None] + w - jnp.arange(span)[None, :]
    key_pos = idx - w
    valid = ((dist >= 0) & (dist <= w))[None] & (key_pos >= 0)[:, None, :]
    s = jnp.where(valid, s, NEG)
    lse = jax.nn.logsumexp(s, axis=-1)
    p = jnp.exp(s - lse[..., None])
    o = jnp.einsum('...nqk,...nkd->...nqd', p.astype(v.dtype), v_blk)
    o = o.reshape(*lead, Lp, HEAD_DIM)[..., :L, :]
    lse = lse.reshape(*lead, Lp)[..., :L]
    return o, lse


def dilated_mixture_attention(q, k, v):
    B, H, S, hd = q.shape
    outs, lses = [], []
    for window, d in DILATED_PATTERNS:
        w_sub = window // d
        to_cls = lambda t: t.reshape(B, H, S // d, d, hd).swapaxes(2, 3)
        o, lse = banded_causal_attention(to_cls(q), to_cls(k), to_cls(v), w_sub)
        outs.append(o.swapaxes(2, 3).reshape(B, H, S, hd))
        lses.append(lse.swapaxes(2, 3).reshape(B, H, S))
    alpha = jax.nn.softmax(jnp.stack(lses, axis=0), axis=0)
    return jnp.einsum('pbhs,pbhsd->bhsd', alpha.astype(q.dtype), jnp.stack(outs, axis=0))


def forgetting_attention(q, k, v, f_logit):
    S = q.shape[2]
    log_f = jax.nn.log_sigmoid(f_logit.astype(jnp.float32)).transpose(0, 2, 1)
    F = lax.cumsum(log_f, axis=2)
    outs = []
    for i in range(S // Q_BLOCK):
        lo, hi = i * Q_BLOCK, (i + 1) * Q_BLOCK
        s = jnp.einsum('bhqd,bhkd->bhqk', q[:, :, lo:hi], k[:, :, :hi],
                       preferred_element_type=jnp.float32) * ATTN_SCALE
        s = s + F[:, :, lo:hi, None] - F[:, :, None, :hi]
        causal = (lo + jnp.arange(Q_BLOCK))[:, None] >= jnp.arange(hi)[None, :]
        p = jax.nn.softmax(jnp.where(causal, s, NEG), axis=-1)
        outs.append(jnp.einsum('bhqk,bhkd->bhqd', p.astype(v.dtype), v[:, :, :hi]))
    return jnp.concatenate(outs, axis=2)


def hybrid_mixer(h, positions, w_in, b_forget, g_out_a, g_out_b, w_out):
    B, S, _ = h.shape
    proj = h @ w_in
    qa, ka, va, qb, kb, vb, f_logit = jnp.split(proj, COL_OFFSETS, axis=-1)
    heads = lambda t, n: t.reshape(B, S, n, HEAD_DIM).transpose(0, 2, 1, 3)
    qa = partial_rotary(heads(qa, N_HEADS_A), positions)
    ka = partial_rotary(heads(ka, N_HEADS_A), positions)
    out_a = dilated_mixture_attention(qa, ka, heads(va, N_HEADS_A))
    out_b = forgetting_attention(heads(qb, N_HEADS_B), heads(kb, N_HEADS_B), heads(vb, N_HEADS_B),
                                 f_logit + b_forget)
    flat = lambda t: t.transpose(0, 2, 1, 3).reshape(B, S, -1)
    merged = jnp.concatenate([rmsnorm(flat(out_a), g_out_a), rmsnorm(flat(out_b), g_out_b)], axis=-1)
    return merged @ w_out


def _fwd_setup_inputs(seed: int = 0) -> dict:
    key = jax.random.key(seed)
    ks = jax.random.split(key, 24)
    nrm = lambda k, shape, s: jax.random.normal(k, shape, jnp.float32) * s
    gain = lambda k, n: 1.0 + 0.05 * jax.random.normal(k, (DEPTH, n), jnp.float32)
    return {
        'x': nrm(ks[0], (BATCH, SEQ, D_MODEL), 1.0),
        'c': nrm(ks[1], (BATCH, D_MODEL), 1.0),
        'positions': jnp.broadcast_to(jnp.arange(SEQ, dtype=jnp.int32)[None, :], (BATCH, SEQ)),
        'w_ada': nrm(ks[2], (DEPTH, D_MODEL, N_MOD * D_MODEL), 0.01),
        'b_ada': nrm(ks[3], (DEPTH, N_MOD * D_MODEL), 0.02),
        'g_pre_ff1': gain(ks[4], D_MODEL),
        'g_post_ff1': gain(ks[5], D_MODEL),
        'w_ff1_gate': nrm(ks[6], (DEPTH, D_MODEL, D_FF), D_MODEL ** -0.5),
        'w_ff1_up': nrm(ks[7], (DEPTH, D_MODEL, D_FF), D_MODEL ** -0.5),
        'w_ff1_down': nrm(ks[8], (DEPTH, D_FF, D_MODEL), D_FF ** -0.5),
        'g_pre_mix': gain(ks[9], D_MODEL),
        'g_post_mix': gain(ks[10], D_MODEL),
        'w_in': nrm(ks[11], (DEPTH, D_MODEL, IN_COLS), D_MODEL ** -0.5),
        'b_forget': jax.random.uniform(ks[12], (DEPTH, N_HEADS_B), jnp.float32, 1.0, 4.0),
        'g_out_a': gain(ks[13], WIDTH_A),
        'g_out_b': gain(ks[14], WIDTH_B),
        'w_out': nrm(ks[15], (DEPTH, D_MODEL, D_MODEL), D_MODEL ** -0.5),
        'g_pre_ff2': gain(ks[16], D_MODEL),
        'g_post_ff2': gain(ks[17], D_MODEL),
        'w_ff2_gate': nrm(ks[18], (DEPTH, D_MODEL, D_FF), D_MODEL ** -0.5),
        'w_ff2_up': nrm(ks[19], (DEPTH, D_MODEL, D_FF), D_MODEL ** -0.5),
        'w_ff2_down': nrm(ks[20], (DEPTH, D_FF, D_MODEL), D_FF ** -0.5),
    }


def _fwd_reference(x, c, positions, w_ada, b_ada, g_pre_ff1, g_post_ff1, w_ff1_gate, w_ff1_up, w_ff1_down,
              g_pre_mix, g_post_mix, w_in, b_forget, g_out_a, g_out_b, w_out,
              g_pre_ff2, g_post_ff2, w_ff2_gate, w_ff2_up, w_ff2_down):
    B = x.shape[0]
    silu_c = jax.nn.silu(c)
    for l in range(DEPTH):
        mod = (silu_c @ w_ada[l] + b_ada[l]).reshape(B, N_MOD, D_MODEL)
        m = lambda i: mod[:, i][:, None, :]
        h = rmsnorm(x, g_pre_ff1[l]) * (1.0 + m(1)) + m(0)
        y = rmsnorm(swiglu(h, w_ff1_gate[l], w_ff1_up[l], w_ff1_down[l]), g_post_ff1[l])
        x = x + 0.5 * m(2) * y
        h = rmsnorm(x, g_pre_mix[l]) * (1.0 + m(4)) + m(3)
        y = rmsnorm(hybrid_mixer(h, positions, w_in[l], b_forget[l], g_out_a[l], g_out_b[l], w_out[l]),
                    g_post_mix[l])
        x = x + m(5) * y
        h = rmsnorm(x, g_pre_ff2[l]) * (1.0 + m(7)) + m(6)
        y = rmsnorm(swiglu(h, w_ff2_gate[l], w_ff2_up[l], w_ff2_down[l]), g_post_ff2[l])
        x = x + 0.5 * m(8) * y
    return x


import jax as _jax
import jax.numpy as _jnp

TWIN_FORMAT = 'train_step'
FWD_PARAMS = ['x', 'c', 'positions', 'w_ada', 'b_ada', 'g_pre_ff1', 'g_post_ff1', 'w_ff1_gate', 'w_ff1_up', 'w_ff1_down', 'g_pre_mix', 'g_post_mix', 'w_in', 'b_forget', 'g_out_a', 'g_out_b', 'w_out', 'g_pre_ff2', 'g_post_ff2', 'w_ff2_gate', 'w_ff2_up', 'w_ff2_down']
TWIN_WEIGHTS = ['w_ada', 'b_ada', 'g_pre_ff1', 'g_post_ff1', 'w_ff1_gate', 'w_ff1_up', 'w_ff1_down', 'g_pre_mix', 'g_post_mix', 'w_in', 'b_forget', 'g_out_a', 'g_out_b', 'w_out', 'g_pre_ff2', 'g_post_ff2', 'w_ff2_gate', 'w_ff2_up', 'w_ff2_down']
TWIN_DIFF_INPUT = 'x'
TWIN_INPUTS = ['x', 'c', 'positions', 'w_ada', 'b_ada', 'g_pre_ff1', 'g_post_ff1', 'w_ff1_gate', 'w_ff1_up', 'w_ff1_down', 'g_pre_mix', 'g_post_mix', 'w_in', 'b_forget', 'g_out_a', 'g_out_b', 'w_out', 'g_pre_ff2', 'g_post_ff2', 'w_ff2_gate', 'w_ff2_up', 'w_ff2_down', 'loss_target', 'm_w_ada', 'm_b_ada', 'm_g_pre_ff1', 'm_g_post_ff1', 'm_w_ff1_gate', 'm_w_ff1_up', 'm_w_ff1_down', 'm_g_pre_mix', 'm_g_post_mix', 'm_w_in', 'm_b_forget', 'm_g_out_a', 'm_g_out_b', 'm_w_out', 'm_g_pre_ff2', 'm_g_post_ff2', 'm_w_ff2_gate', 'm_w_ff2_up', 'm_w_ff2_down', 'v_w_ada', 'v_b_ada', 'v_g_pre_ff1', 'v_g_post_ff1', 'v_w_ff1_gate', 'v_w_ff1_up', 'v_w_ff1_down', 'v_g_pre_mix', 'v_g_post_mix', 'v_w_in', 'v_b_forget', 'v_g_out_a', 'v_g_out_b', 'v_w_out', 'v_g_pre_ff2', 'v_g_post_ff2', 'v_w_ff2_gate', 'v_w_ff2_up', 'v_w_ff2_down']
TWIN_OUTPUTS = ['loss', 'grad_x', 'grad_w_ada', 'grad_b_ada', 'grad_g_pre_ff1', 'grad_g_post_ff1', 'grad_w_ff1_gate', 'grad_w_ff1_up', 'grad_w_ff1_down', 'grad_g_pre_mix', 'grad_g_post_mix', 'grad_w_in', 'grad_b_forget', 'grad_g_out_a', 'grad_g_out_b', 'grad_w_out', 'grad_g_pre_ff2', 'grad_g_post_ff2', 'grad_w_ff2_gate', 'grad_w_ff2_up', 'grad_w_ff2_down', 'delta_w_ada', 'delta_b_ada', 'delta_g_pre_ff1', 'delta_g_post_ff1', 'delta_w_ff1_gate', 'delta_w_ff1_up', 'delta_w_ff1_down', 'delta_g_pre_mix', 'delta_g_post_mix', 'delta_w_in', 'delta_b_forget', 'delta_g_out_a', 'delta_g_out_b', 'delta_w_out', 'delta_g_pre_ff2', 'delta_g_post_ff2', 'delta_w_ff2_gate', 'delta_w_ff2_up', 'delta_w_ff2_down', 'new_m_w_ada', 'new_m_b_ada', 'new_m_g_pre_ff1', 'new_m_g_post_ff1', 'new_m_w_ff1_gate', 'new_m_w_ff1_up', 'new_m_w_ff1_down', 'new_m_g_pre_mix', 'new_m_g_post_mix', 'new_m_w_in', 'new_m_b_forget', 'new_m_g_out_a', 'new_m_g_out_b', 'new_m_w_out', 'new_m_g_pre_ff2', 'new_m_g_post_ff2', 'new_m_w_ff2_gate', 'new_m_w_ff2_up', 'new_m_w_ff2_down', 'new_v_w_ada', 'new_v_b_ada', 'new_v_g_pre_ff1', 'new_v_g_post_ff1', 'new_v_w_ff1_gate', 'new_v_w_ff1_up', 'new_v_w_ff1_down', 'new_v_g_pre_mix', 'new_v_g_post_mix', 'new_v_w_in', 'new_v_b_forget', 'new_v_g_out_a', 'new_v_g_out_b', 'new_v_w_out', 'new_v_g_pre_ff2', 'new_v_g_post_ff2', 'new_v_w_ff2_gate', 'new_v_w_ff2_up', 'new_v_w_ff2_down']
TWIN_LEAF_KINDS = {'loss': 'loss', 'grad_x': 'grad_x', 'grad_w_ada': 'grad_w', 'grad_b_ada': 'grad_w', 'grad_g_pre_ff1': 'grad_w', 'grad_g_post_ff1': 'grad_w', 'grad_w_ff1_gate': 'grad_w', 'grad_w_ff1_up': 'grad_w', 'grad_w_ff1_down': 'grad_w', 'grad_g_pre_mix': 'grad_w', 'grad_g_post_mix': 'grad_w', 'grad_w_in': 'grad_w', 'grad_b_forget': 'grad_w', 'grad_g_out_a': 'grad_w', 'grad_g_out_b': 'grad_w', 'grad_w_out': 'grad_w', 'grad_g_pre_ff2': 'grad_w', 'grad_g_post_ff2': 'grad_w', 'grad_w_ff2_gate': 'grad_w', 'grad_w_ff2_up': 'grad_w', 'grad_w_ff2_down': 'grad_w', 'delta_w_ada': 'delta_w', 'delta_b_ada': 'delta_w', 'delta_g_pre_ff1': 'delta_w', 'delta_g_post_ff1': 'delta_w', 'delta_w_ff1_gate': 'delta_w', 'delta_w_ff1_up': 'delta_w', 'delta_w_ff1_down': 'delta_w', 'delta_g_pre_mix': 'delta_w', 'delta_g_post_mix': 'delta_w', 'delta_w_in': 'delta_w', 'delta_b_forget': 'delta_w', 'delta_g_out_a': 'delta_w', 'delta_g_out_b': 'delta_w', 'delta_w_out': 'delta_w', 'delta_g_pre_ff2': 'delta_w', 'delta_g_post_ff2': 'delta_w', 'delta_w_ff2_gate': 'delta_w', 'delta_w_ff2_up': 'delta_w', 'delta_w_ff2_down': 'delta_w', 'new_m_w_ada': 'new_m', 'new_m_b_ada': 'new_m', 'new_m_g_pre_ff1': 'new_m', 'new_m_g_post_ff1': 'new_m', 'new_m_w_ff1_gate': 'new_m', 'new_m_w_ff1_up': 'new_m', 'new_m_w_ff1_down': 'new_m', 'new_m_g_pre_mix': 'new_m', 'new_m_g_post_mix': 'new_m', 'new_m_w_in': 'new_m', 'new_m_b_forget': 'new_m', 'new_m_g_out_a': 'new_m', 'new_m_g_out_b': 'new_m', 'new_m_w_out': 'new_m', 'new_m_g_pre_ff2': 'new_m', 'new_m_g_post_ff2': 'new_m', 'new_m_w_ff2_gate': 'new_m', 'new_m_w_ff2_up': 'new_m', 'new_m_w_ff2_down': 'new_m', 'new_v_w_ada': 'new_v', 'new_v_b_ada': 'new_v', 'new_v_g_pre_ff1': 'new_v', 'new_v_g_post_ff1': 'new_v', 'new_v_w_ff1_gate': 'new_v', 'new_v_w_ff1_up': 'new_v', 'new_v_w_ff1_down': 'new_v', 'new_v_g_pre_mix': 'new_v', 'new_v_g_post_mix': 'new_v', 'new_v_w_in': 'new_v', 'new_v_b_forget': 'new_v', 'new_v_g_out_a': 'new_v', 'new_v_g_out_b': 'new_v', 'new_v_w_out': 'new_v', 'new_v_g_pre_ff2': 'new_v', 'new_v_g_post_ff2': 'new_v', 'new_v_w_ff2_gate': 'new_v', 'new_v_w_ff2_up': 'new_v', 'new_v_w_ff2_down': 'new_v'}


def _forward(args):
    return _fwd_reference(*[args[k] for k in FWD_PARAMS])


def _output_shape():
    out = _jax.eval_shape(lambda: _forward(_fwd_setup_inputs(0)))
    return out.shape, out.dtype

N_MICROBATCH = 1
ADAM_LR = 0.001
ADAM_B1 = 0.9
ADAM_B2 = 0.999
ADAM_EPS = 1e-08
ADAM_WD = 0.01
ADAM_STEP = 10
PER_EXAMPLE_BATCH_AXIS = {'x': 0, 'c': 0, 'positions': 0, 'loss_target': 0}
SHARED_INPUTS = []
_WEIGHT_DTYPES = {'w_ada': _jnp.float32, 'b_ada': _jnp.float32, 'g_pre_ff1': _jnp.float32, 'g_post_ff1': _jnp.float32, 'w_ff1_gate': _jnp.float32, 'w_ff1_up': _jnp.float32, 'w_ff1_down': _jnp.float32, 'g_pre_mix': _jnp.float32, 'g_post_mix': _jnp.float32, 'w_in': _jnp.float32, 'b_forget': _jnp.float32, 'g_out_a': _jnp.float32, 'g_out_b': _jnp.float32, 'w_out': _jnp.float32, 'g_pre_ff2': _jnp.float32, 'g_post_ff2': _jnp.float32, 'w_ff2_gate': _jnp.float32, 'w_ff2_up': _jnp.float32, 'w_ff2_down': _jnp.float32}
MOMENT_SCALE = {'w_ada': 1.771429e+00, 'b_ada': 3.296225e+00, 'g_pre_ff1': 5.914047e-02, 'g_post_ff1': 6.766192e-01, 'w_ff1_gate': 2.478626e-02, 'w_ff1_up': 2.812611e-02, 'w_ff1_down': 4.634165e-02, 'g_pre_mix': 1.006369e-01, 'g_post_mix': 3.338251e+00, 'w_in': 3.303909e-01, 'b_forget': 8.571430e-01, 'g_out_a': 9.613925e-01, 'g_out_b': 3.761067e-01, 'w_out': 7.197352e-01, 'g_pre_ff2': 5.226930e-02, 'g_post_ff2': 7.203679e-01, 'w_ff2_gate': 2.432348e-02, 'w_ff2_up': 2.752851e-02, 'w_ff2_down': 4.515397e-02}


def _to_microbatches(a, axis):
    t = _jnp.moveaxis(a, axis, 0)
    t = t.reshape((N_MICROBATCH, t.shape[0] // N_MICROBATCH) + t.shape[1:])
    return _jnp.moveaxis(t, 1, axis + 1)


def setup_inputs(seed: int = 0) -> dict:
    inp = _fwd_setup_inputs(seed)
    key = _jax.random.fold_in(_jax.random.key(seed), 7919)
    shape, _ = _output_shape()
    out = dict(inp)
    out["loss_target"] = _jax.random.normal(_jax.random.fold_in(key, 0), shape, _jnp.float32)
    for i, name in enumerate(TWIN_WEIGHTS):
        w = inp[name].astype(_jnp.float32)
        if MOMENT_SCALE is None:
            s = _jnp.sqrt(_jnp.mean(_jnp.square(w)) + 1e-30)
        else:
            s = MOMENT_SCALE[name]
        km, kv = _jax.random.split(_jax.random.fold_in(key, i + 1))
        out[name] = w
        out["m_" + name] = s * _jax.random.normal(km, w.shape, _jnp.float32)
        out["v_" + name] = (s * s) * _jax.random.uniform(kv, w.shape, _jnp.float32, 0.5, 1.5)
    if N_MICROBATCH > 1:
        for name, axis in PER_EXAMPLE_BATCH_AXIS.items():
            out[name] = _to_microbatches(out[name], axis)
    return {'x': out['x'], 'c': out['c'], 'positions': out['positions'], 'w_ada': out['w_ada'], 'b_ada': out['b_ada'], 'g_pre_ff1': out['g_pre_ff1'], 'g_post_ff1': out['g_post_ff1'], 'w_ff1_gate': out['w_ff1_gate'], 'w_ff1_up': out['w_ff1_up'], 'w_ff1_down': out['w_ff1_down'], 'g_pre_mix': out['g_pre_mix'], 'g_post_mix': out['g_post_mix'], 'w_in': out['w_in'], 'b_forget': out['b_forget'], 'g_out_a': out['g_out_a'], 'g_out_b': out['g_out_b'], 'w_out': out['w_out'], 'g_pre_ff2': out['g_pre_ff2'], 'g_post_ff2': out['g_post_ff2'], 'w_ff2_gate': out['w_ff2_gate'], 'w_ff2_up': out['w_ff2_up'], 'w_ff2_down': out['w_ff2_down'], 'loss_target': out['loss_target'], 'm_w_ada': out['m_w_ada'], 'm_b_ada': out['m_b_ada'], 'm_g_pre_ff1': out['m_g_pre_ff1'], 'm_g_post_ff1': out['m_g_post_ff1'], 'm_w_ff1_gate': out['m_w_ff1_gate'], 'm_w_ff1_up': out['m_w_ff1_up'], 'm_w_ff1_down': out['m_w_ff1_down'], 'm_g_pre_mix': out['m_g_pre_mix'], 'm_g_post_mix': out['m_g_post_mix'], 'm_w_in': out['m_w_in'], 'm_b_forget': out['m_b_forget'], 'm_g_out_a': out['m_g_out_a'], 'm_g_out_b': out['m_g_out_b'], 'm_w_out': out['m_w_out'], 'm_g_pre_ff2': out['m_g_pre_ff2'], 'm_g_post_ff2': out['m_g_post_ff2'], 'm_w_ff2_gate': out['m_w_ff2_gate'], 'm_w_ff2_up': out['m_w_ff2_up'], 'm_w_ff2_down': out['m_w_ff2_down'], 'v_w_ada': out['v_w_ada'], 'v_b_ada': out['v_b_ada'], 'v_g_pre_ff1': out['v_g_pre_ff1'], 'v_g_post_ff1': out['v_g_post_ff1'], 'v_w_ff1_gate': out['v_w_ff1_gate'], 'v_w_ff1_up': out['v_w_ff1_up'], 'v_w_ff1_down': out['v_w_ff1_down'], 'v_g_pre_mix': out['v_g_pre_mix'], 'v_g_post_mix': out['v_g_post_mix'], 'v_w_in': out['v_w_in'], 'v_b_forget': out['v_b_forget'], 'v_g_out_a': out['v_g_out_a'], 'v_g_out_b': out['v_g_out_b'], 'v_w_out': out['v_w_out'], 'v_g_pre_ff2': out['v_g_pre_ff2'], 'v_g_post_ff2': out['v_g_post_ff2'], 'v_w_ff2_gate': out['v_w_ff2_gate'], 'v_w_ff2_up': out['v_w_ff2_up'], 'v_w_ff2_down': out['v_w_ff2_down']}


def _loss(weights, diff, rest, loss_target):
    with _jax.named_scope("forward"):
        args = {**rest, TWIN_DIFF_INPUT: diff, **{k: w.astype(_WEIGHT_DTYPES[k]) for k, w in weights.items()}}
        y = _forward(args)
    with _jax.named_scope("loss_head"):
        err = _jnp.square(y.astype(_jnp.float32) - loss_target)
        return 0.5 * _jnp.sum(_jnp.mean(err, axis=-1)) if err.ndim else 0.5 * err


def _adamw(w, g, m, v):
    m = ADAM_B1 * m + (1.0 - ADAM_B1) * g
    v = ADAM_B2 * v + (1.0 - ADAM_B2) * _jnp.square(g)
    m_hat = m / (1.0 - ADAM_B1 ** ADAM_STEP)
    v_hat = v / (1.0 - ADAM_B2 ** ADAM_STEP)
    delta = -ADAM_LR * (m_hat / (_jnp.sqrt(v_hat) + ADAM_EPS) + ADAM_WD * w)
    return delta, m, v


def reference(x, c, positions, w_ada, b_ada, g_pre_ff1, g_post_ff1, w_ff1_gate, w_ff1_up, w_ff1_down, g_pre_mix, g_post_mix, w_in, b_forget, g_out_a, g_out_b, w_out, g_pre_ff2, g_post_ff2, w_ff2_gate, w_ff2_up, w_ff2_down, loss_target, m_w_ada, m_b_ada, m_g_pre_ff1, m_g_post_ff1, m_w_ff1_gate, m_w_ff1_up, m_w_ff1_down, m_g_pre_mix, m_g_post_mix, m_w_in, m_b_forget, m_g_out_a, m_g_out_b, m_w_out, m_g_pre_ff2, m_g_post_ff2, m_w_ff2_gate, m_w_ff2_up, m_w_ff2_down, v_w_ada, v_b_ada, v_g_pre_ff1, v_g_post_ff1, v_w_ff1_gate, v_w_ff1_up, v_w_ff1_down, v_g_pre_mix, v_g_post_mix, v_w_in, v_b_forget, v_g_out_a, v_g_out_b, v_w_out, v_g_pre_ff2, v_g_post_ff2, v_w_ff2_gate, v_w_ff2_up, v_w_ff2_down):
    given = dict(x=x, c=c, positions=positions, w_ada=w_ada, b_ada=b_ada, g_pre_ff1=g_pre_ff1, g_post_ff1=g_post_ff1, w_ff1_gate=w_ff1_gate, w_ff1_up=w_ff1_up, w_ff1_down=w_ff1_down, g_pre_mix=g_pre_mix, g_post_mix=g_post_mix, w_in=w_in, b_forget=b_forget, g_out_a=g_out_a, g_out_b=g_out_b, w_out=w_out, g_pre_ff2=g_pre_ff2, g_post_ff2=g_post_ff2, w_ff2_gate=w_ff2_gate, w_ff2_up=w_ff2_up, w_ff2_down=w_ff2_down, loss_target=loss_target, m_w_ada=m_w_ada, m_b_ada=m_b_ada, m_g_pre_ff1=m_g_pre_ff1, m_g_post_ff1=m_g_post_ff1, m_w_ff1_gate=m_w_ff1_gate, m_w_ff1_up=m_w_ff1_up, m_w_ff1_down=m_w_ff1_down, m_g_pre_mix=m_g_pre_mix, m_g_post_mix=m_g_post_mix, m_w_in=m_w_in, m_b_forget=m_b_forget, m_g_out_a=m_g_out_a, m_g_out_b=m_g_out_b, m_w_out=m_w_out, m_g_pre_ff2=m_g_pre_ff2, m_g_post_ff2=m_g_post_ff2, m_w_ff2_gate=m_w_ff2_gate, m_w_ff2_up=m_w_ff2_up, m_w_ff2_down=m_w_ff2_down, v_w_ada=v_w_ada, v_b_ada=v_b_ada, v_g_pre_ff1=v_g_pre_ff1, v_g_post_ff1=v_g_post_ff1, v_w_ff1_gate=v_w_ff1_gate, v_w_ff1_up=v_w_ff1_up, v_w_ff1_down=v_w_ff1_down, v_g_pre_mix=v_g_pre_mix, v_g_post_mix=v_g_post_mix, v_w_in=v_w_in, v_b_forget=v_b_forget, v_g_out_a=v_g_out_a, v_g_out_b=v_g_out_b, v_w_out=v_w_out, v_g_pre_ff2=v_g_pre_ff2, v_g_post_ff2=v_g_post_ff2, v_w_ff2_gate=v_w_ff2_gate, v_w_ff2_up=v_w_ff2_up, v_w_ff2_down=v_w_ff2_down)
    weights = {n: given[n] for n in TWIN_WEIGHTS}
    shared = {n: given[n] for n in SHARED_INPUTS}
    per_example = {n: given[n] for n in ['x', 'c', 'positions']}
    grad_fn = _jax.value_and_grad(_loss, argnums=(0, 1))

    def one_microbatch(ex, loss_target):
        ex = dict(ex)
        diff = ex.pop(TWIN_DIFF_INPUT)
        return grad_fn(weights, diff, {**shared, **ex}, loss_target)

    if N_MICROBATCH == 1:
        loss, (grad_w, grad_x) = one_microbatch(per_example, given["loss_target"])
    else:
        def body(carry, xs):
            loss_sum, grad_sum = carry
            l_k, (gw_k, gx_k) = one_microbatch(xs[0], xs[1])
            with _jax.named_scope("update"):
                return (loss_sum + l_k, _jax.tree.map(_jnp.add, grad_sum, gw_k)), gx_k

        init = (_jnp.zeros((), _jnp.float32), _jax.tree.map(_jnp.zeros_like, weights))
        (loss, grad_w), grad_x = _jax.lax.scan(body, init, (per_example, given["loss_target"]))
    with _jax.named_scope("update"):
        delta_w, new_m, new_v = {}, {}, {}
        for n in TWIN_WEIGHTS:
            delta_w[n], new_m[n], new_v[n] = _adamw(weights[n], grad_w[n], given["m_" + n], given["v_" + n])
    return (loss, grad_x, *[grad_w[n] for n in TWIN_WEIGHTS], *[delta_w[n] for n in TWIN_WEIGHTS],
            *[new_m[n] for n in TWIN_WEIGHTS], *[new_v[n] for n in TWIN_WEIGHTS])
```

```python
import jax
import jax.numpy as jnp
from jax import lax
from jax.experimental import pallas as pl
from jax.experimental.pallas import tpu as pltpu

F32 = jnp.float32
BF16 = jnp.bfloat16

D_MODEL = 1024
HEAD_DIM = 64
N_HEADS_A = 8
N_HEADS_B = 8
WIDTH_A = N_HEADS_A * HEAD_DIM
WIDTH_B = N_HEADS_B * HEAD_DIM
DILATED_PATTERNS = ((128, 1), (512, 4), (2048, 16))
ROT_DIM = HEAD_DIM // 4
ROPE_THETA = 500000.0
D_FF = 2752
D_FF_PAD = 2816
N_MOD = 9
EPS = 1e-6
ATTN_SCALE = HEAD_DIM ** -0.5
NEG = -1e30
N_DEV = 8
LANES = 128
HEADS_PER_STEP = LANES // HEAD_DIM

ADAM_LR = 0.001
ADAM_B1 = 0.9
ADAM_B2 = 0.999
ADAM_EPS = 1e-08
ADAM_WD = 0.01
ADAM_STEP = 10

VMEM_LIMIT = 56 * 1024 * 1024
MESH = pl.DeviceIdType.MESH

NT_DIMS = (((1,), (1,)), ((), ()))
TN_DIMS = (((0,), (0,)), ((), ()))
NN_DIMS = (((1,), (0,)), ((), ()))


def _pcall(body, **kw):
    return pl.pallas_call(body, **kw)


def _params(sem=None, **kw):
    if sem is not None:
        kw["dimension_semantics"] = sem
    return pltpu.CompilerParams(vmem_limit_bytes=VMEM_LIMIT, **kw)


def mm_rows(pairs, trans_b, out_dtype, name, tm=512):
    n = len(pairs)
    m = pairs[0][0].shape[0]
    n_out = pairs[0][1].shape[0 if trans_b else 1]
    dims = NT_DIMS if trans_b else NN_DIMS

    def body(*refs):
        o_ref = refs[2 * n]
        acc = None
        for a_ref, b_ref in zip(refs[:n], refs[n:2 * n]):
            d = lax.dot_general(a_ref[...], b_ref[...], dims, preferred_element_type=F32)
            acc = d if acc is None else acc + d
        o_ref[...] = acc.astype(o_ref.dtype)

    in_specs = [pl.BlockSpec((tm, a.shape[1]), lambda i: (i, 0)) for a, _ in pairs]
    in_specs += [pl.BlockSpec(b.shape, lambda i: (0, 0)) for _, b in pairs]
    return _pcall(
        body, name=name, grid=(m // tm,), in_specs=in_specs,
        out_specs=pl.BlockSpec((tm, n_out), lambda i: (i, 0)),
        out_shape=jax.ShapeDtypeStruct((m, n_out), out_dtype),
        compiler_params=_params(("arbitrary",)),
    )(*[a for a, _ in pairs], *[b for _, b in pairs])


def mm_tn(a, b, out_dtype, name, tk=512):
    t, ka = a.shape
    n_out = b.shape[1]
    steps = t // tk

    def body(a_ref, b_ref, o_ref, acc_ref):
        k = pl.program_id(0)
        d = lax.dot_general(a_ref[...], b_ref[...], TN_DIMS, preferred_element_type=F32)

        @pl.when(k == 0)
        def _():
            acc_ref[...] = d

        @pl.when(k > 0)
        def _():
            acc_ref[...] += d

        @pl.when(k == steps - 1)
        def _():
            o_ref[...] = acc_ref[...].astype(o_ref.dtype)

    return _pcall(
        body, name=name, grid=(steps,),
        in_specs=[pl.BlockSpec((tk, ka), lambda k: (k, 0)), pl.BlockSpec((tk, n_out), lambda k: (k, 0))],
        out_specs=pl.BlockSpec((ka, n_out), lambda k: (0, 0)),
        out_shape=jax.ShapeDtypeStruct((ka, n_out), out_dtype),
        scratch_shapes=[pltpu.VMEM((ka, n_out), F32)],
        compiler_params=_params(("arbitrary",)),
    )(a, b)


def _sigmoid(x):
    return 1.0 / (1.0 + jnp.exp(-x))


def ffn_up(h, wg, wu, name, tm=512, tn=1408):
    t, d = h.shape
    fp = wg.shape[1]

    def body(h_ref, wg_ref, wu_ref, g_ref, u_ref, a_ref):
        hv = h_ref[...]
        g = jnp.dot(hv, wg_ref[...], preferred_element_type=F32)
        u = jnp.dot(hv, wu_ref[...], preferred_element_type=F32)
        g_ref[...] = g.astype(BF16)
        u_ref[...] = u.astype(BF16)
        a_ref[...] = (g * _sigmoid(g) * u).astype(BF16)

    w_spec = pl.BlockSpec((d, tn), lambda j, i: (0, j))
    o_spec = pl.BlockSpec((tm, tn), lambda j, i: (i, j))
    o_shape = jax.ShapeDtypeStruct((t, fp), BF16)
    return _pcall(
        body, name=name, grid=(fp // tn, t // tm),
        in_specs=[pl.BlockSpec((tm, d), lambda j, i: (i, 0)), w_spec, w_spec],
        out_specs=[o_spec, o_spec, o_spec], out_shape=[o_shape, o_shape, o_shape],
        compiler_params=_params(("arbitrary", "arbitrary")),
    )(h, wg, wu)


def ffn_down_bwd(dy0, wd, gate, up, name, tm=512, tn=1408):
    t, d = dy0.shape
    fp = wd.shape[0]

    def body(dy_ref, wd_ref, g_ref, u_ref, dg_ref, du_ref):
        dact = lax.dot_general(dy_ref[...], wd_ref[...], NT_DIMS, preferred_element_type=F32)
        g = g_ref[...].astype(F32)
        u = u_ref[...].astype(F32)
        sg = _sigmoid(g)
        du_ref[...] = (dact * g * sg).astype(BF16)
        dg_ref[...] = (dact * u * (sg * (1.0 + g * (1.0 - sg)))).astype(BF16)

    t_spec = pl.BlockSpec((tm, tn), lambda j, i: (i, j))
    o_shape = jax.ShapeDtypeStruct((t, fp), BF16)
    return _pcall(
        body, name=name, grid=(fp // tn, t // tm),
        in_specs=[pl.BlockSpec((tm, d), lambda j, i: (i, 0)), pl.BlockSpec((tn, d), lambda j, i: (j, 0)), t_spec, t_spec],
        out_specs=[t_spec, t_spec], out_shape=[o_shape, o_shape],
        compiler_params=_params(("arbitrary", "arbitrary")),
    )(dy0, wd, gate, up)


def _row_specs(dx, ts, ns):
    return pl.BlockSpec((ts, dx), lambda b, s: (b * ns + s, 0))


def _mod_spec():
    return pl.BlockSpec((1, N_MOD, D_MODEL), lambda b, s: (b, 0, 0))


def _vec_spec(dx):
    return pl.BlockSpec((1, dx), lambda b, s: (0, 0))


def prenorm_fwd(x, g, mod, i_shift, i_scale, nb, name, ts=512):
    t, dx = x.shape
    ns = t // nb // ts

    def body(*refs):
        if mod is None:
            x_ref, g_ref, h_ref = refs
        else:
            x_ref, g_ref, mod_ref, h_ref = refs
        xv = x_ref[...]
        r = lax.rsqrt(jnp.mean(xv * xv, axis=-1, keepdims=True) + EPS)
        h = xv * r * g_ref[...]
        if mod is not None:
            h = h * (1.0 + mod_ref[0, i_scale:i_scale + 1, :]) + mod_ref[0, i_shift:i_shift + 1, :]
        h_ref[...] = h.astype(BF16)

    in_specs = [_row_specs(dx, ts, ns), _vec_spec(dx)]
    args = [x, g]
    if mod is not None:
        in_specs.append(_mod_spec())
        args.append(mod)
    return _pcall(
        body, name=name, grid=(nb, ns), in_specs=in_specs, out_specs=_row_specs(dx, ts, ns),
        out_shape=jax.ShapeDtypeStruct((t, dx), BF16), compiler_params=_params(("arbitrary", "arbitrary")),
    )(*args)


def prenorm_bwd(dh, x, g, mod, i_scale, dres, nb, name, ts=512):
    t, dx = x.shape
    ns = t // nb // ts
    has_mod = mod is not None
    has_res = dres is not None

    def body(*refs):
        refs = list(refs)
        dh_ref, x_ref, g_ref = refs[:3]
        pos = 3
        mod_ref = dres_ref = None
        if has_mod:
            mod_ref = refs[pos]
            pos += 1
        if has_res:
            dres_ref = refs[pos]
            pos += 1
        dx_ref, dg_ref = refs[pos], refs[pos + 1]
        b, s = pl.program_id(0), pl.program_id(1)
        xv = x_ref[...]
        dhv = dh_ref[...].astype(F32)
        gv = g_ref[...]
        r = lax.rsqrt(jnp.mean(xv * xv, axis=-1, keepdims=True) + EPS)
        xhat = xv * r
        dn = dhv
        if has_mod:
            dsc_ref, dsh_ref = refs[pos + 2], refs[pos + 3]
            dn = dhv * (1.0 + mod_ref[0, i_scale:i_scale + 1, :])
            dsc = jnp.sum(dhv * xhat * gv, axis=0, keepdims=True)[None]
            dsh = jnp.sum(dhv, axis=0, keepdims=True)[None]

            @pl.when(s == 0)
            def _():
                dsc_ref[...] = dsc
                dsh_ref[...] = dsh

            @pl.when(s > 0)
            def _():
                dsc_ref[...] += dsc
                dsh_ref[...] += dsh

        dg = jnp.sum(dn * xhat, axis=0, keepdims=True)
        first = jnp.logical_and(b == 0, s == 0)

        @pl.when(first)
        def _():
            dg_ref[...] = dg

        @pl.when(jnp.logical_not(first))
        def _():
            dg_ref[...] += dg

        dxhat = dn * gv
        dxv = r * (dxhat - xhat * jnp.mean(dxhat * xhat, axis=-1, keepdims=True))
        if has_res:
            dxv = dxv + dres_ref[...]
        dx_ref[...] = dxv

    row = _row_specs(dx, ts, ns)
    in_specs = [row, row, _vec_spec(dx)]
    args = [dh, x, g]
    if has_mod:
        in_specs.append(_mod_spec())
        args.append(mod)
    if has_res:
        in_specs.append(row)
        args.append(dres)
    out_specs = [row, _vec_spec(dx)]
    out_shape = [jax.ShapeDtypeStruct((t, dx), F32), jax.ShapeDtypeStruct((1, dx), F32)]
    if has_mod:
        bspec = pl.BlockSpec((1, 1, dx), lambda b, s: (b, 0, 0))
        out_specs += [bspec, bspec]
        out_shape += [jax.ShapeDtypeStruct((nb, 1, dx), F32)] * 2
    return _pcall(
        body, name=name, grid=(nb, ns), in_specs=in_specs, out_specs=out_specs, out_shape=out_shape,
        compiler_params=_params(("arbitrary", "arbitrary")),
    )(*args)


def postnorm_fwd(x, y0, g, mod, i_gate, coef, nb, name, target=None, ts=512):
    t, dx = x.shape
    ns = t // nb // ts
    with_loss = target is not None

    def body(*refs):
        x_ref, y_ref, g_ref, mod_ref = refs[:4]
        yv = y_ref[...]
        r = lax.rsqrt(jnp.mean(yv * yv, axis=-1, keepdims=True) + EPS)
        out = x_ref[...] + (coef * mod_ref[0, i_gate:i_gate + 1, :]) * (yv * r * g_ref[...])
        if not with_loss:
            refs[4][...] = out
            return
        t_ref, dx_ref, loss_ref = refs[4:]
        b, s = pl.program_id(0), pl.program_id(1)
        err = out - t_ref[...]
        dx_ref[...] = err * (1.0 / dx)
        part = (0.5 / dx) * jnp.sum(jnp.sum(err * err, axis=1, keepdims=True), axis=0, keepdims=True)
        first = jnp.logical_and(b == 0, s == 0)

        @pl.when(first)
        def _():
            loss_ref[...] = part

        @pl.when(jnp.logical_not(first))
        def _():
            loss_ref[...] += part

    row = _row_specs(dx, ts, ns)
    in_specs = [row, row, _vec_spec(dx), _mod_spec()]
    args = [x, y0, g, mod]
    out_specs = row
    out_shape = jax.ShapeDtypeStruct((t, dx), F32)
    if with_loss:
        in_specs.append(row)
        args.append(target)
        out_specs = [row, pl.BlockSpec((1, 1), lambda b, s: (0, 0))]
        out_shape = [out_shape, jax.ShapeDtypeStruct((1, 1), F32)]
    return _pcall(
        body, name=name, grid=(nb, ns), in_specs=in_specs, out_specs=out_specs, out_shape=out_shape,
        compiler_params=_params(("arbitrary", "arbitrary")),
    )(*args)


def postnorm_bwd(dxo, y0, g, mod, i_gate, coef, nb, name, ts=512):
    t, dx = y0.shape
    ns = t // nb // ts

    def body(d_ref, y_ref, g_ref, mod_ref, dy_ref, dg_ref, dgate_ref):
        b, s = pl.program_id(0), pl.program_id(1)
        yv = y_ref[...]
        dv = d_ref[...]
        gv = g_ref[...]
        r = lax.rsqrt(jnp.mean(yv * yv, axis=-1, keepdims=True) + EPS)
        yhat = yv * r
        dgate = jnp.sum(dv * (coef * (yhat * gv)), axis=0, keepdims=True)[None]
        dyn = dv * (coef * mod_ref[0, i_gate:i_gate + 1, :])
        dg = jnp.sum(dyn * yhat, axis=0, keepdims=True)
        dyhat = dyn * gv
        dy_ref[...] = (r * (dyhat - yhat * jnp.mean(dyhat * yhat, axis=-1, keepdims=True))).astype(BF16)

        @pl.when(s == 0)
        def _():
            dgate_ref[...] = dgate

        @pl.when(s > 0)
        def _():
            dgate_ref[...] += dgate

        first = jnp.logical_and(b == 0, s == 0)

        @pl.when(first)
        def _():
            dg_ref[...] = dg

        @pl.when(jnp.logical_not(first))
        def _():
            dg_ref[...] += dg

    row = _row_specs(dx, ts, ns)
    return _pcall(
        body, name=name, grid=(nb, ns), in_specs=[row, row, _vec_spec(dx), _mod_spec()],
        out_specs=[row, _vec_spec(dx), pl.BlockSpec((1, 1, dx), lambda b, s: (b, 0, 0))],
        out_shape=[jax.ShapeDtypeStruct((t, dx), BF16), jax.ShapeDtypeStruct((1, dx), F32),
                   jax.ShapeDtypeStruct((nb, 1, dx), F32)],
        compiler_params=_params(("arbitrary", "arbitrary")),
    )(dxo, y0, g, mod)


def rope_tables(positions):
    inv_freq = ROPE_THETA ** (-jnp.arange(0, ROT_DIM, 2, dtype=F32) / ROT_DIM)
    ang = positions.astype(F32).reshape(-1, 1) * inv_freq
    cos, sin = jnp.cos(ang), jnp.sin(ang)
    half = ROT_DIM // 2
    z = lambda n: jnp.zeros((ang.shape[0], n), F32)
    c = jnp.concatenate([cos, cos, jnp.ones((ang.shape[0], HEAD_DIM - ROT_DIM), F32)], axis=1)
    sp = jnp.concatenate([z(half), sin, z(HEAD_DIM - ROT_DIM)], axis=1)
    sm = jnp.concatenate([-sin, z(HEAD_DIM - half)], axis=1)
    return tuple(jnp.tile(a, (1, HEADS_PER_STEP)) for a in (c, sp, sm))


def rope(xarr, col_block, width, tables, transpose, out_dtype, name, ts=512):
    t = xarr.shape[0]
    half = ROT_DIM // 2
    reps = width // LANES

    def body(x_ref, c_ref, sp_ref, sm_ref, o_ref):
        xv = x_ref[...].astype(F32)
        wide = lambda r: jnp.concatenate([r[...]] * reps, axis=1)
        c, sp, sm = wide(c_ref), wide(sp_ref), wide(sm_ref)
        if transpose:
            out = xv * c + pltpu.roll(xv * sp, width - half, 1) + pltpu.roll(xv * sm, half, 1)
        else:
            out = xv * c + pltpu.roll(xv, half, 1) * sp + pltpu.roll(xv, width - half, 1) * sm
        o_ref[...] = out.astype(o_ref.dtype)

    tab = pl.BlockSpec((ts, LANES), lambda i: (i, 0))
    return _pcall(
        body, name=name, grid=(t // ts,),
        in_specs=[pl.BlockSpec((ts, width), lambda i: (i, col_block)), tab, tab, tab],
        out_specs=pl.BlockSpec((ts, width), lambda i: (i, 0)),
        out_shape=jax.ShapeDtypeStruct((t, width), out_dtype), compiler_params=_params(("arbitrary",)),
    )(xarr, *tables)


def _scan_lanes(x, reverse):
    n = x.shape[-1]
    lane = lax.broadcasted_iota(jnp.int32, x.shape, x.ndim - 1)
    k = 1
    while k < n:
        if reverse:
            x = x + jnp.where(lane < n - k, pltpu.roll(x, n - k, x.ndim - 1), 0.0)
        else:
            x = x + jnp.where(lane >= k, pltpu.roll(x, k, x.ndim - 1), 0.0)
        k *= 2
    return x


def _log_sigmoid(z):
    return jnp.minimum(z, 0.0) - jnp.log(1.0 + jnp.exp(-jnp.abs(z)))


def fox_gate_fwd(ft, b_forget, name):
    nb, nh, s = ft.shape

    def body(f_ref, b_ref, o_ref):
        z = f_ref[0] + b_ref[...]
        o_ref[0] = -_scan_lanes(_log_sigmoid(z), False)

    spec = pl.BlockSpec((1, nh, s), lambda b: (b, 0, 0))
    return _pcall(
        body, name=name, grid=(nb,), in_specs=[spec, pl.BlockSpec((nh, 1), lambda b: (0, 0))], out_specs=spec,
        out_shape=jax.ShapeDtypeStruct((nb, nh, s), F32), compiler_params=_params(("arbitrary",)),
    )(ft, b_forget)


def fox_gate_bwd(dcb, drow, ft, b_forget, name):
    nb, nh, s = ft.shape

    def body(d_ref, r_ref, f_ref, b_ref, dz_ref, db_ref):
        b = pl.program_id(0)
        z = f_ref[0] + b_ref[...]
        dlf = _scan_lanes(r_ref[0] - d_ref[0], True)
        dz = dlf * _sigmoid(-z)
        dz_ref[0] = dz
        db = jnp.sum(dz, axis=1, keepdims=True)

        @pl.when(b == 0)
        def _():
            db_ref[...] = db

        @pl.when(b > 0)
        def _():
            db_ref[...] += db

    spec = pl.BlockSpec((1, nh, s), lambda b: (b, 0, 0))
    vec = pl.BlockSpec((nh, 1), lambda b: (0, 0))
    return _pcall(
        body, name=name, grid=(nb,), in_specs=[spec, spec, spec, vec], out_specs=[spec, vec],
        out_shape=[jax.ShapeDtypeStruct((nb, nh, s), F32), jax.ShapeDtypeStruct((nh, 1), F32)],
        compiler_params=_params(("arbitrary",)),
    )(dcb, drow, ft, b_forget)


def attn_block(s):
    return 256 if s % 256 == 0 else 128


def dilated_table(s, blk):
    delta = (jnp.arange(s // blk)[:, None, None] * blk + jnp.arange(blk)[None, :, None]
             - jnp.arange(blk)[None, None, :])
    count = jnp.zeros(delta.shape, F32)
    for window, dil in DILATED_PATTERNS:
        count = count + ((delta >= 0) & (delta <= window) & (delta % dil == 0)).astype(F32)
    return jnp.where(count > 0, jnp.log(jnp.maximum(count, 1.0)), NEG)


def causal_table(s, blk):
    delta = (jnp.arange(s // blk)[:, None, None] * blk + jnp.arange(blk)[None, :, None]
             - jnp.arange(blk)[None, None, :])
    return jnp.where(delta >= 0, 0.0, NEG).astype(F32)


def attn_fwd(q_arr, q_off, k_arr, k_off, v_arr, v_off, table, colbias, nb, name):
    t = q_arr.shape[0]
    s = t // nb
    blk = table.shape[1]
    nq = s // blk
    npairs = WIDTH_A // LANES
    use_cb = colbias is not None

    def body(*refs):
        if use_cb:
            q_ref, k_ref, v_ref, tab_ref, cb_ref, o_ref, lse_ref = refs
        else:
            q_ref, k_ref, v_ref, tab_ref, o_ref, lse_ref = refs
        qi = pl.program_id(2)
        outs, lses = [], []
        for h in range(HEADS_PER_STEP):
            hs = slice(h * HEAD_DIM, (h + 1) * HEAD_DIM)
            q = (q_ref[:, hs].astype(F32) * ATTN_SCALE).astype(BF16)

            def step(kb, carry, h=h, hs=hs, q=q):
                m, l, acc = carry
                ks = pl.multiple_of(kb * blk, blk)
                k = k_ref[pl.ds(ks, blk), hs]
                v = v_ref[pl.ds(ks, blk), hs]
                sc = lax.dot_general(q, k, NT_DIMS, preferred_element_type=F32) + tab_ref[qi - kb]
                if use_cb:
                    sc = sc + cb_ref[0, h, kb]
                m_new = jnp.maximum(m, jnp.max(sc, axis=1, keepdims=True))
                alpha = jnp.exp(m - m_new)
                p = jnp.exp(sc - m_new)
                l = alpha * l + jnp.sum(p, axis=1, keepdims=True)
                acc = alpha * acc + jnp.dot(p.astype(BF16), v, preferred_element_type=F32)
                return m_new, l, acc

            init = (jnp.full((blk, 1), NEG, F32), jnp.zeros((blk, 1), F32), jnp.zeros((blk, HEAD_DIM), F32))
            m, l, acc = lax.fori_loop(0, qi + 1, step, init)
            outs.append(acc / l)
            lses.append(jnp.broadcast_to(m + jnp.log(l), (blk, HEAD_DIM)))
        o_ref[...] = jnp.concatenate(outs, axis=1)
        lse_ref[...] = jnp.concatenate(lses, axis=1)

    def seq_spec(off):
        return pl.BlockSpec((s, LANES), lambda b, j, i: (b, off + j))

    in_specs = [pl.BlockSpec((blk, LANES), lambda b, j, i: (b * nq + i, q_off + j)), seq_spec(k_off), seq_spec(v_off),
                pl.BlockSpec(table.shape, lambda b, j, i: (0, 0, 0))]
    args = [q_arr, k_arr, v_arr, table]
    if use_cb:
        in_specs.append(pl.BlockSpec((1, HEADS_PER_STEP, nq, 1, blk), lambda b, j, i: (b, j, 0, 0, 0)))
        args.append(colbias)
    o_spec = pl.BlockSpec((blk, LANES), lambda b, j, i: (b * nq + i, j))
    o_shape = jax.ShapeDtypeStruct((t, npairs * LANES), F32)
    return _pcall(
        body, name=name, grid=(nb, npairs, nq), in_specs=in_specs, out_specs=[o_spec, o_spec],
        out_shape=[o_shape, o_shape], compiler_params=_params(("arbitrary", "arbitrary", "arbitrary")),
    )(*args)


def attn_bwd(q_arr, q_off, k_arr, k_off, v_arr, v_off, o_arr, lse_arr, do_arr, table, colbias, nb, qk_dtype, name):
    t = q_arr.shape[0]
    s = t // nb
    blk = table.shape[1]
    nq = s // blk
    npairs = WIDTH_A // LANES
    use_cb = colbias is not None

    def body(*refs):
        refs = list(refs)
        q_ref, k_ref, v_ref, o_ref, lse_ref, do_ref, tab_ref = refs[:7]
        pos = 7
        cb_ref = None
        if use_cb:
            cb_ref = refs[pos]
            pos += 1
        dq_ref, dk_ref, dv_ref = refs[pos:pos + 3]
        pos += 3
        dcb_ref = drow_ref = None
        if use_cb:
            dcb_ref, drow_ref = refs[pos:pos + 2]
            pos += 2
        dq_s, dk_s, dv_s = refs[pos:pos + 3]
        dcb_s = refs[pos + 3] if use_cb else None

        for h in range(HEADS_PER_STEP):
            hs = slice(h * HEAD_DIM, (h + 1) * HEAD_DIM)
            dk_s[...] = jnp.zeros_like(dk_s)
            dv_s[...] = jnp.zeros_like(dv_s)
            if use_cb:
                dcb_s[...] = jnp.zeros_like(dcb_s)

            def q_loop(qi, carry, h=h, hs=hs):
                qs = pl.multiple_of(qi * blk, blk)
                q = (q_ref[pl.ds(qs, blk), hs].astype(F32) * ATTN_SCALE).astype(BF16)
                do = do_ref[pl.ds(qs, blk), hs]
                dsum = jnp.sum(do * o_ref[pl.ds(qs, blk), hs], axis=1, keepdims=True)
                lse = lse_ref[pl.ds(qs, blk), h * HEAD_DIM:h * HEAD_DIM + 1]
                do_b = do.astype(BF16)

                def k_loop(kb, carry):
                    dq, drow = carry
                    ks = pl.multiple_of(kb * blk, blk)
                    k = k_ref[pl.ds(ks, blk), hs]
                    v = v_ref[pl.ds(ks, blk), hs]
                    sc = lax.dot_general(q, k, NT_DIMS, preferred_element_type=F32) + tab_ref[qi - kb]
                    if use_cb:
                        sc = sc + cb_ref[0, h, kb]
                    p = jnp.exp(sc - lse)
                    dp = lax.dot_general(do_b, v, NT_DIMS, preferred_element_type=F32)
                    ds = p * (dp - dsum)
                    ds_b = ds.astype(BF16)
                    dv_s[pl.ds(ks, blk), :] += lax.dot_general(p.astype(BF16), do_b, TN_DIMS, preferred_element_type=F32)
                    dk_s[pl.ds(ks, blk), :] += lax.dot_general(ds_b, q, TN_DIMS, preferred_element_type=F32)
                    if use_cb:
                        dcb_s[kb] += jnp.sum(ds, axis=0, keepdims=True)
                        drow = drow + jnp.sum(ds, axis=1, keepdims=True)
                    return dq + jnp.dot(ds_b, k, preferred_element_type=F32), drow

                dq, drow = lax.fori_loop(0, qi + 1, k_loop, (jnp.zeros((blk, HEAD_DIM), F32), jnp.zeros((blk, 1), F32)))
                dq_s[pl.ds(qs, blk), :] = dq * ATTN_SCALE
                if use_cb:
                    drow_ref[pl.ds(qs, blk), hs] = jnp.broadcast_to(drow, (blk, HEAD_DIM))
                return carry

            lax.fori_loop(0, nq, q_loop, 0)
            dq_ref[:, hs] = dq_s[...].astype(dq_ref.dtype)
            dk_ref[:, hs] = dk_s[...].astype(dk_ref.dtype)
            dv_ref[:, hs] = dv_s[...].astype(dv_ref.dtype)
            if use_cb:
                dcb_ref[0, h] = dcb_s[...]

    def seq_spec(off):
        return pl.BlockSpec((s, LANES), lambda b, j: (b, off + j))

    in_specs = [seq_spec(q_off), seq_spec(k_off), seq_spec(v_off), seq_spec(0), seq_spec(0), seq_spec(0),
                pl.BlockSpec(table.shape, lambda b, j: (0, 0, 0))]
    args = [q_arr, k_arr, v_arr, o_arr, lse_arr, do_arr, table]
    width = npairs * LANES
    out_specs = [seq_spec(0)] * 3
    out_shape = [jax.ShapeDtypeStruct((t, width), qk_dtype), jax.ShapeDtypeStruct((t, width), qk_dtype),
                 jax.ShapeDtypeStruct((t, width), BF16)]
    scratch = [pltpu.VMEM((s, HEAD_DIM), F32)] * 3
    if use_cb:
        cb_spec = pl.BlockSpec((1, HEADS_PER_STEP, nq, 1, blk), lambda b, j: (b, j, 0, 0, 0))
        in_specs.append(cb_spec)
        args.append(colbias)
        out_specs += [cb_spec, seq_spec(0)]
        out_shape += [jax.ShapeDtypeStruct(colbias.shape, F32), jax.ShapeDtypeStruct((t, width), F32)]
        scratch.append(pltpu.VMEM((nq, 1, blk), F32))
    return _pcall(
        body, name=name, grid=(nb, npairs), in_specs=in_specs, out_specs=out_specs, out_shape=out_shape,
        scratch_shapes=scratch, compiler_params=_params(("arbitrary", "arbitrary")),
    )(*args)


def ada_fwd(c_all, w_ada, b_cols, name):
    def body(c_ref, w_ref, b_ref, o_ref):
        cv = c_ref[...]
        sc = (cv * _sigmoid(cv)).astype(BF16)
        o_ref[...] = jnp.dot(sc, w_ref[...].astype(BF16), preferred_element_type=F32) + b_ref[...]

    return _pcall(body, name=name, out_shape=jax.ShapeDtypeStruct((c_all.shape[0], w_ada.shape[1]), F32),
                  compiler_params=_params())(c_all, w_ada, b_cols)


def ada_bwd(c_all, dmod_cols, name):
    def body(c_ref, d_ref, o_ref):
        cv = c_ref[...]
        sc = (cv * _sigmoid(cv)).astype(BF16)
        o_ref[...] = lax.dot_general(sc, d_ref[...].astype(BF16), TN_DIMS, preferred_element_type=F32)

    return _pcall(body, name=name, out_shape=jax.ShapeDtypeStruct((c_all.shape[1], dmod_cols.shape[1]), F32),
                  compiler_params=_params())(c_all, dmod_cols)


def adamw(parts, group, w, m, v, name, tr=None):
    n = parts.shape[0]
    r, c = w.shape
    tr = r if tr is None else tr
    c1 = 1.0 - ADAM_B1 ** ADAM_STEP
    c2 = 1.0 - ADAM_B2 ** ADAM_STEP

    def body(p_ref, w_ref, m_ref, v_ref, g_ref, d_ref, nm_ref, nv_ref):
        g = p_ref[0, 0].astype(F32)
        for i in range(1, n):
            g = g + p_ref[i, 0].astype(F32)
        wv = w_ref[...]
        nm = ADAM_B1 * m_ref[...] + (1.0 - ADAM_B1) * g
        nv = ADAM_B2 * v_ref[...] + (1.0 - ADAM_B2) * (g * g)
        g_ref[...] = g
        nm_ref[...] = nm
        nv_ref[...] = nv
        d_ref[...] = -ADAM_LR * ((nm / c1) / (jnp.sqrt(nv / c2) + ADAM_EPS) + ADAM_WD * wv)

    spec = pl.BlockSpec((tr, c), lambda i: (i, 0))
    shape = jax.ShapeDtypeStruct((r, c), F32)
    return _pcall(
        body, name=name, grid=(r // tr,),
        in_specs=[pl.BlockSpec((n, 1, tr, c), lambda i: (0, group, i, 0)), spec, spec, spec],
        out_specs=[spec] * 4, out_shape=[shape] * 4, compiler_params=_params(("arbitrary",)),
    )(parts, w, m, v)


def _place():
    return lax.axis_index("x"), lax.axis_index("y"), lax.axis_index("c")


def _slot(p):
    return 4 * p[0] + 2 * p[1] + p[2]


def all_gather(arrs, name):
    n = len(arrs)
    hbm = pl.BlockSpec(memory_space=pl.ANY)

    def body(*refs):
        ins, outs = refs[:n], refs[n:2 * n]
        send_sems, recv_sems, local_sems = refs[2 * n:]
        x, y, c = _place()
        me, sibling = (x, y, c), (x, y, 1 - c)
        chips = [(1 - x, y), (x, 1 - y), (1 - x, 1 - y)]

        def copy(a, k, block, to, src=None):
            dst = outs[a].at[_slot(block)]
            return pltpu.make_async_remote_copy(
                src_ref=dst if src is None else src, dst_ref=dst, send_sem=send_sems.at[a * 7 + k],
                recv_sem=recv_sems.at[a * 7 + k], device_id=to, device_id_type=MESH)

        mine = [pltpu.make_async_copy(ins[a], outs[a].at[_slot(me)], local_sems.at[a]) for a in range(n)]
        for cp in mine:
            cp.start()
        first = []
        for a in range(n):
            first.append(copy(a, 0, me, sibling, src=ins[a]))
            first += [copy(a, 1 + j, me, (*chip, c), src=ins[a]) for j, chip in enumerate(chips)]
        for cp in first:
            cp.start()
        passed = []
        for a in range(n):
            for j, chip in enumerate(chips):
                copy(a, 1 + j, (*chip, c), me).wait_recv()
                cp = copy(a, 4 + j, (*chip, c), sibling)
                cp.start()
                passed.append(cp)
        for a in range(n):
            copy(a, 0, sibling, me).wait_recv()
            for j, chip in enumerate(chips):
                copy(a, 4 + j, (*chip, 1 - c), me).wait_recv()
        for cp in first + passed:
            cp.wait_send()
        for cp in mine:
            cp.wait()

    return _pcall(
        body, name=name, in_specs=[hbm] * n, out_specs=[hbm] * n,
        out_shape=[jax.ShapeDtypeStruct((N_DEV,) + a.shape, a.dtype) for a in arrs],
        scratch_shapes=[pltpu.SemaphoreType.DMA((7 * n,)), pltpu.SemaphoreType.DMA((7 * n,)),
                        pltpu.SemaphoreType.DMA((n,))],
        compiler_params=pltpu.CompilerParams(has_side_effects=True),
    )(*arrs)


def exchange(arrs, name):
    n = len(arrs)
    hbm = pl.BlockSpec(memory_space=pl.ANY)

    def body(*refs):
        ins, outs = refs[:n], refs[n:2 * n]
        send_sems, recv_sems, local_sems = refs[2 * n:]
        x, y, c = _place()
        me = (x, y, c)
        flip = lambda v, bit: 1 - v if bit else v
        peers = [(flip(x, k & 4), flip(y, k & 2), flip(c, k & 1)) for k in range(1, N_DEV)]
        mine = [pltpu.make_async_copy(ins[a].at[_slot(me)], outs[a].at[_slot(me)], local_sems.at[a]) for a in range(n)]
        for cp in mine:
            cp.start()
        sends = []
        for a in range(n):
            for k, peer in enumerate(peers):
                sends.append(pltpu.make_async_remote_copy(
                    src_ref=ins[a].at[_slot(peer)], dst_ref=outs[a].at[_slot(me)], send_sem=send_sems.at[a * 7 + k],
                    recv_sem=recv_sems.at[a * 7 + k], device_id=peer, device_id_type=MESH))
        for cp in sends:
            cp.start()
        for a in range(n):
            for k, peer in enumerate(peers):
                pltpu.make_async_remote_copy(
                    src_ref=ins[a].at[_slot(me)], dst_ref=outs[a].at[_slot(peer)], send_sem=send_sems.at[a * 7 + k],
                    recv_sem=recv_sems.at[a * 7 + k], device_id=peer, device_id_type=MESH).wait_recv()
        for cp in sends:
            cp.wait_send()
        for cp in mine:
            cp.wait()

    return _pcall(
        body, name=name, in_specs=[hbm] * n, out_specs=[hbm] * n,
        out_shape=[jax.ShapeDtypeStruct(a.shape, a.dtype) for a in arrs],
        scratch_shapes=[pltpu.SemaphoreType.DMA((7 * n,)), pltpu.SemaphoreType.DMA((7 * n,)),
                        pltpu.SemaphoreType.DMA((n,))],
        compiler_params=pltpu.CompilerParams(has_side_effects=True),
    )(*arrs)


def _cols_from_blocks(blocks, pad_to=None):
    r = blocks.shape[1]
    full = blocks.transpose(1, 0, 2).reshape(r, -1)
    if pad_to is not None and pad_to > full.shape[1]:
        full = jnp.pad(full, ((0, 0), (0, pad_to - full.shape[1])))
    return full


def _cols_to_blocks(full, ncols):
    r = full.shape[0]
    return full[:, :ncols].reshape(r, N_DEV, ncols // N_DEV).transpose(1, 0, 2)


def _rows_from_blocks(blocks, pad_to=None):
    full = blocks.reshape(-1, blocks.shape[2])
    if pad_to is not None and pad_to > full.shape[0]:
        full = jnp.pad(full, ((0, pad_to - full.shape[0]), (0, 0)))
    return full


def _rows_to_blocks(full, nrows):
    return full[:nrows].reshape(N_DEV, nrows // N_DEV, full.shape[1])


SMALL_ORDER = ("g_pre_ff1", "g_post_ff1", "g_pre_mix", "g_post_mix", "g_out_a", "g_out_b", "g_pre_ff2", "g_post_ff2",
               "b_forget")


def _pack_small(vals):
    rows = []
    for name in SMALL_ORDER:
        v = vals[name].reshape(1, -1)
        if v.shape[1] % LANES:
            v = jnp.pad(v, ((0, 0), (0, LANES - v.shape[1] % LANES)))
        rows.append(v)
    return jnp.concatenate(rows, axis=1)


def _unpack_small(row, sizes):
    out, pos = {}, 0
    for name in SMALL_ORDER:
        n = sizes[name]
        out[name] = row[:, pos:pos + n]
        pos += -(-n // LANES) * LANES
    return out


def _ffn_forward(x, mod, g_pre, g_post, wg, wu, wd, i0, nb, tag, target=None):
    h = prenorm_fwd(x, g_pre, mod, i0, i0 + 1, nb, f"{tag}_prenorm")
    gate, up, act = ffn_up(h, wg, wu, f"{tag}_up")
    y0 = mm_rows([(act, wd)], False, F32, f"{tag}_down")
    out = postnorm_fwd(x, y0, g_post, mod, i0 + 2, 0.5, nb, f"{tag}_postnorm", target=target)
    return out, (x, h, gate, up, act, y0)


def _ffn_backward(dxo, saved, mod, g_pre, g_post, wg, wu, wd, i0, nb, tag):
    x, h, gate, up, act, y0 = saved
    dy0, dg_post, dgate_mod = postnorm_bwd(dxo, y0, g_post, mod, i0 + 2, 0.5, nb, f"{tag}_postnorm_bwd")
    dwd = mm_tn(act, dy0, BF16, f"{tag}_dwd")
    dgate, dup = ffn_down_bwd(dy0, wd, gate, up, f"{tag}_down_bwd")
    dwg = mm_tn(h, dgate, BF16, f"{tag}_dwg")
    dwu = mm_tn(h, dup, BF16, f"{tag}_dwu")
    dh = mm_rows([(dgate, wg), (dup, wu)], True, F32, f"{tag}_dh")
    dx, dg_pre, dsc, dsh = prenorm_bwd(dh, x, g_pre, mod, i0 + 1, dxo, nb, f"{tag}_prenorm_bwd")
    return dx, dict(g_pre=dg_pre, g_post=dg_post, wg=dwg, wu=dwu, wd=dwd, mod=(dsh, dsc, dgate_mod))


def kernel(x, c, positions, w_ada, b_ada, g_pre_ff1, g_post_ff1, w_ff1_gate, w_ff1_up, w_ff1_down, g_pre_mix, g_post_mix, w_in, b_forget, g_out_a, g_out_b, w_out, g_pre_ff2, g_post_ff2, w_ff2_gate, w_ff2_up, w_ff2_down, loss_target, m_w_ada, m_b_ada, m_g_pre_ff1, m_g_post_ff1, m_w_ff1_gate, m_w_ff1_up, m_w_ff1_down, m_g_pre_mix, m_g_post_mix, m_w_in, m_b_forget, m_g_out_a, m_g_out_b, m_w_out, m_g_pre_ff2, m_g_post_ff2, m_w_ff2_gate, m_w_ff2_up, m_w_ff2_down, v_w_ada, v_b_ada, v_g_pre_ff1, v_g_post_ff1, v_w_ff1_gate, v_w_ff1_up, v_w_ff1_down, v_g_pre_mix, v_g_post_mix, v_w_in, v_b_forget, v_g_out_a, v_g_out_b, v_w_out, v_g_pre_ff2, v_g_post_ff2, v_w_ff2_gate, v_w_ff2_up, v_w_ff2_down):
    weights = dict(w_ada=w_ada, b_ada=b_ada, g_pre_ff1=g_pre_ff1, g_post_ff1=g_post_ff1, w_ff1_gate=w_ff1_gate,
                   w_ff1_up=w_ff1_up, w_ff1_down=w_ff1_down, g_pre_mix=g_pre_mix, g_post_mix=g_post_mix, w_in=w_in,
                   b_forget=b_forget, g_out_a=g_out_a, g_out_b=g_out_b, w_out=w_out, g_pre_ff2=g_pre_ff2,
                   g_post_ff2=g_post_ff2, w_ff2_gate=w_ff2_gate, w_ff2_up=w_ff2_up, w_ff2_down=w_ff2_down)
    mom_m = dict(w_ada=m_w_ada, b_ada=m_b_ada, g_pre_ff1=m_g_pre_ff1, g_post_ff1=m_g_post_ff1, w_ff1_gate=m_w_ff1_gate,
                 w_ff1_up=m_w_ff1_up, w_ff1_down=m_w_ff1_down, g_pre_mix=m_g_pre_mix, g_post_mix=m_g_post_mix,
                 w_in=m_w_in, b_forget=m_b_forget, g_out_a=m_g_out_a, g_out_b=m_g_out_b, w_out=m_w_out,
                 g_pre_ff2=m_g_pre_ff2, g_post_ff2=m_g_post_ff2, w_ff2_gate=m_w_ff2_gate, w_ff2_up=m_w_ff2_up,
                 w_ff2_down=m_w_ff2_down)
    mom_v = dict(w_ada=v_w_ada, b_ada=v_b_ada, g_pre_ff1=v_g_pre_ff1, g_post_ff1=v_g_post_ff1, w_ff1_gate=v_w_ff1_gate,
                 w_ff1_up=v_w_ff1_up, w_ff1_down=v_w_ff1_down, g_pre_mix=v_g_pre_mix, g_post_mix=v_g_post_mix,
                 w_in=v_w_in, b_forget=v_b_forget, g_out_a=v_g_out_a, g_out_b=v_g_out_b, w_out=v_w_out,
                 g_pre_ff2=v_g_pre_ff2, g_post_ff2=v_g_post_ff2, w_ff2_gate=v_w_ff2_gate, w_ff2_up=v_w_ff2_up,
                 w_ff2_down=v_w_ff2_down)
    order = list(weights)

    nb, s, d = x.shape
    t = nb * s
    me = _slot(_place())
    nbg = nb * N_DEV
    ada_cols = w_ada.shape[2]

    ff_cols = jnp.stack([w_ff1_gate[0], w_ff1_up[0], w_ff2_gate[0], w_ff2_up[0]]).astype(BF16)
    ff_rows = jnp.stack([w_ff1_down[0], w_ff2_down[0]]).astype(BF16)
    c_all, ff_cols_all, ff_rows_all, w_in_all, w_out_all = all_gather(
        [c, ff_cols, ff_rows, w_in[0].astype(BF16), w_out[0].astype(BF16)], "gather_weights")
    c_all = c_all.reshape(nbg, d)
    wg1, wu1, wg2, wu2 = (_cols_from_blocks(ff_cols_all[:, i], D_FF_PAD) for i in range(4))
    wd1, wd2 = (_rows_from_blocks(ff_rows_all[:, i], D_FF_PAD) for i in range(2))
    w_in_full = _cols_from_blocks(w_in_all)
    n_qkv = 3 * (WIDTH_A + WIDTH_B)
    w_qkv = w_in_full[:, :n_qkv]
    w_f = jnp.pad(w_in_full[:, n_qkv:], ((0, 0), (0, LANES - N_HEADS_B)))
    w_o = _rows_from_blocks(w_out_all)
    w_o_a, w_o_b = w_o[:WIDTH_A], w_o[WIDTH_A:]

    b_cols = lax.dynamic_slice(b_ada, (0, me * ada_cols), (1, ada_cols))
    mod_cols = ada_fwd(c_all, w_ada[0], b_cols, "ada_fwd")
    (mod_all,) = all_gather([mod_cols], "gather_mod")
    mod = lax.dynamic_slice(mod_all, (0, me * nb, 0), (N_DEV, nb, ada_cols))
    mod = mod.transpose(1, 0, 2).reshape(nb, N_MOD, d)

    xf = x.reshape(t, d)
    target = loss_target.reshape(t, d)

    x1, saved1 = _ffn_forward(xf, mod, g_pre_ff1, g_post_ff1, wg1, wu1, wd1, 0, nb, "ff1")

    h2 = prenorm_fwd(x1, g_pre_mix, mod, 3, 4, nb, "mix_prenorm")
    proj = mm_rows([(h2, w_qkv)], False, BF16, "mix_proj")
    f_logit = mm_rows([(h2, w_f)], False, F32, "mix_forget")
    tables = rope_tables(positions)
    qk_rot = rope(proj, 0, 2 * WIDTH_A, tables, False, BF16, "rope")
    blk = attn_block(s)
    nq = s // blk
    tab_a = dilated_table(s, blk)
    tab_b = causal_table(s, blk)
    ft = f_logit[:, :N_HEADS_B].reshape(nb, s, N_HEADS_B).transpose(0, 2, 1)
    bf_col = b_forget.reshape(N_HEADS_B, 1)
    colbias = fox_gate_fwd(ft, bf_col, "fox_gate").reshape(nb, N_HEADS_B, nq, 1, blk)
    pa = WIDTH_A // LANES
    o_a, lse_a = attn_fwd(qk_rot, 0, qk_rot, pa, proj, 2 * pa, tab_a, None, nb, "attn_a")
    o_b, lse_b = attn_fwd(proj, 3 * pa, proj, 4 * pa, proj, 5 * pa, tab_b, colbias, nb, "attn_b")
    m_a = prenorm_fwd(o_a, g_out_a, None, None, None, nb, "out_norm_a")
    m_b = prenorm_fwd(o_b, g_out_b, None, None, None, nb, "out_norm_b")
    y0m = mm_rows([(m_a, w_o_a), (m_b, w_o_b)], False, F32, "mix_out")
    x2 = postnorm_fwd(x1, y0m, g_post_mix, mod, 5, 1.0, nb, "mix_postnorm")

    (dx3, loss_part), saved2 = _ffn_forward(x2, mod, g_pre_ff2, g_post_ff2, wg2, wu2, wd2, 6, nb, "ff2", target=target)
    loss = lax.psum(loss_part[0, 0], ("x", "y", "c"))

    dx2, gr2 = _ffn_backward(dx3, saved2, mod, g_pre_ff2, g_post_ff2, wg2, wu2, wd2, 6, nb, "ff2")

    dy0m, dg_post_mix, dgate_mix = postnorm_bwd(dx2, y0m, g_post_mix, mod, 5, 1.0, nb, "mix_postnorm_bwd")
    dw_o_a = mm_tn(m_a, dy0m, BF16, "mix_dwo_a")
    dw_o_b = mm_tn(m_b, dy0m, BF16, "mix_dwo_b")
    dm_a = mm_rows([(dy0m, w_o_a)], True, F32, "mix_dm_a")
    dm_b = mm_rows([(dy0m, w_o_b)], True, F32, "mix_dm_b")
    do_a, dg_out_a = prenorm_bwd(dm_a, o_a, g_out_a, None, None, None, nb, "out_norm_a_bwd")
    do_b, dg_out_b = prenorm_bwd(dm_b, o_b, g_out_b, None, None, None, nb, "out_norm_b_bwd")
    dq_a, dk_a, dv_a = attn_bwd(qk_rot, 0, qk_rot, pa, proj, 2 * pa, o_a, lse_a, do_a, tab_a, None, nb, F32, "attn_a_bwd")
    dq_b, dk_b, dv_b, dcb, drow = attn_bwd(proj, 3 * pa, proj, 4 * pa, proj, 5 * pa, o_b, lse_b, do_b, tab_b, colbias, nb,
                                     BF16, "attn_b_bwd")
    dq_a = rope(dq_a, 0, WIDTH_A, tables, True, BF16, "rope_bwd_q")
    dk_a = rope(dk_a, 0, WIDTH_A, tables, True, BF16, "rope_bwd_k")
    drow_t = drow[:, ::HEAD_DIM].reshape(nb, s, N_HEADS_B).transpose(0, 2, 1)
    dz_t, db_forget = fox_gate_bwd(dcb.reshape(nb, N_HEADS_B, s), drow_t, ft, bf_col, "fox_gate_bwd")
    dz = jnp.pad(dz_t.transpose(0, 2, 1).reshape(t, N_HEADS_B), ((0, 0), (0, LANES - N_HEADS_B))).astype(BF16)
    pieces = [dq_a, dk_a, dv_a, dq_b, dk_b, dv_b]
    w_pieces = [w_qkv[:, i * WIDTH_A:(i + 1) * WIDTH_A] for i in range(6)]
    dh2 = mm_rows(list(zip(pieces, w_pieces)) + [(dz, w_f)], True, F32, "mix_dh")
    dw_in = jnp.concatenate([mm_tn(h2, p, BF16, f"mix_dwin_{i}") for i, p in enumerate(pieces)]
                            + [mm_tn(h2, dz, BF16, "mix_dwin_f")[:, :N_HEADS_B]], axis=1)
    dx1, dg_pre_mix, dsc_mix, dsh_mix = prenorm_bwd(dh2, x1, g_pre_mix, mod, 4, dx2, nb, "mix_prenorm_bwd")

    dx0, gr1 = _ffn_backward(dx1, saved1, mod, g_pre_ff1, g_post_ff1, wg1, wu1, wd1, 0, nb, "ff1")
    grad_x = dx0.reshape(nb, s, d)

    dmod = jnp.concatenate(list(gr1["mod"]) + [dsh_mix, dsc_mix, dgate_mix] + list(gr2["mod"]), axis=1)
    small = _pack_small(dict(g_pre_ff1=gr1["g_pre"], g_post_ff1=gr1["g_post"], g_pre_mix=dg_pre_mix,
                             g_post_mix=dg_post_mix, g_out_a=dg_out_a, g_out_b=dg_out_b, g_pre_ff2=gr2["g_pre"],
                             g_post_ff2=gr2["g_post"], b_forget=db_forget))
    dmod_all, small_all = all_gather([dmod.reshape(nb, N_MOD * d), small], "gather_small_grads")
    dmod_all = dmod_all.reshape(nbg, N_MOD * d)
    gcols = jnp.stack([_cols_to_blocks(g, D_FF) for g in (gr1["wg"], gr1["wu"], gr2["wg"], gr2["wu"])], axis=1)
    grows = jnp.stack([_rows_to_blocks(g, D_FF) for g in (gr1["wd"], gr2["wd"])], axis=1)
    g_in = _cols_to_blocks(dw_in, dw_in.shape[1])[:, None]
    g_out = _rows_to_blocks(jnp.concatenate([dw_o_a, dw_o_b], axis=0), d)[:, None]
    gcols, grows, g_in, g_out = exchange([gcols, grows, g_in, g_out], "exchange_grads")

    res = {}
    res["w_ff1_gate"] = adamw(gcols, 0, w_ff1_gate[0], m_w_ff1_gate[0], v_w_ff1_gate[0], "adamw_ff1_gate", tr=256)
    res["w_ff1_up"] = adamw(gcols, 1, w_ff1_up[0], m_w_ff1_up[0], v_w_ff1_up[0], "adamw_ff1_up", tr=256)
    res["w_ff2_gate"] = adamw(gcols, 2, w_ff2_gate[0], m_w_ff2_gate[0], v_w_ff2_gate[0], "adamw_ff2_gate", tr=256)
    res["w_ff2_up"] = adamw(gcols, 3, w_ff2_up[0], m_w_ff2_up[0], v_w_ff2_up[0], "adamw_ff2_up", tr=256)
    res["w_ff1_down"] = adamw(grows, 0, w_ff1_down[0], m_w_ff1_down[0], v_w_ff1_down[0], "adamw_ff1_down")
    res["w_ff2_down"] = adamw(grows, 1, w_ff2_down[0], m_w_ff2_down[0], v_w_ff2_down[0], "adamw_ff2_down")
    res["w_in"] = adamw(g_in, 0, w_in[0], m_w_in[0], v_w_in[0], "adamw_in", tr=256)
    res["w_out"] = adamw(g_out, 0, w_out[0], m_w_out[0], v_w_out[0], "adamw_out")
    dmod_cols = lax.dynamic_slice(dmod_all, (0, me * ada_cols), (nbg, ada_cols))
    dw_ada = ada_bwd(c_all, dmod_cols, "ada_bwd")
    res["w_ada"] = adamw(dw_ada[None, None], 0, w_ada[0], m_w_ada[0], v_w_ada[0], "adamw_ada", tr=256)
    res["b_ada"] = adamw(dmod_all[:, None, None], 0, b_ada, m_b_ada, v_b_ada, "adamw_b_ada")
    sizes = {n: weights[n].shape[1] for n in SMALL_ORDER}
    small_res = adamw(small_all[:, None], 0, _pack_small(weights), _pack_small(mom_m), _pack_small(mom_v), "adamw_small")
    small_res = [_unpack_small(r, sizes) for r in small_res]
    for n in SMALL_ORDER:
        res[n] = tuple(r[n] for r in small_res)

    outs = [loss, grad_x]
    for kind in range(4):
        for n in order:
            a = res[n][kind]
            outs.append(a.reshape(weights[n].shape))
    return tuple(outs)
```

```python
import jax
import jax.numpy as jnp
from jax import lax
from jax.experimental import pallas as pl
from jax.experimental.pallas import tpu as pltpu

F32 = jnp.float32
BF16 = jnp.bfloat16

D_MODEL = 1024
HEAD_DIM = 64
N_HEADS_A = 8
N_HEADS_B = 8
WIDTH_A = N_HEADS_A * HEAD_DIM
WIDTH_B = N_HEADS_B * HEAD_DIM
DILATED_PATTERNS = ((128, 1), (512, 4), (2048, 16))
ROT_DIM = HEAD_DIM // 4
ROPE_THETA = 500000.0
D_FF = 2752
D_FF_PAD = 2816
N_MOD = 9
EPS = 1e-6
ATTN_SCALE = HEAD_DIM ** -0.5
NEG = -1e30
N_DEV = 8
LANES = 128
HEADS_PER_STEP = LANES // HEAD_DIM

ADAM_LR = 0.001
ADAM_B1 = 0.9
ADAM_B2 = 0.999
ADAM_EPS = 1e-08
ADAM_WD = 0.01
ADAM_STEP = 10

VMEM_LIMIT = 56 * 1024 * 1024
MESH = pl.DeviceIdType.MESH

NT_DIMS = (((1,), (1,)), ((), ()))
TN_DIMS = (((0,), (0,)), ((), ()))
NN_DIMS = (((1,), (0,)), ((), ()))


def _pcall(body, **kw):
    return pl.pallas_call(body, **kw)


def _params(sem=None, **kw):
    if sem is not None:
        kw["dimension_semantics"] = sem
    return pltpu.CompilerParams(vmem_limit_bytes=VMEM_LIMIT, **kw)


def mm_rows(pairs, trans_b, out_dtype, name, tm=512):
    n = len(pairs)
    m = pairs[0][0].shape[0]
    n_out = pairs[0][1].shape[0 if trans_b else 1]
    dims = NT_DIMS if trans_b else NN_DIMS

    def body(*refs):
        o_ref = refs[2 * n]
        acc = None
        for a_ref, b_ref in zip(refs[:n], refs[n:2 * n]):
            d = lax.dot_general(a_ref[...], b_ref[...], dims, preferred_element_type=F32)
            acc = d if acc is None else acc + d
        o_ref[...] = acc.astype(o_ref.dtype)

    in_specs = [pl.BlockSpec((tm, a.shape[1]), lambda i: (i, 0)) for a, _ in pairs]
    in_specs += [pl.BlockSpec(b.shape, lambda i: (0, 0)) for _, b in pairs]
    return _pcall(
        body, name=name, grid=(m // tm,), in_specs=in_specs,
        out_specs=pl.BlockSpec((tm, n_out), lambda i: (i, 0)),
        out_shape=jax.ShapeDtypeStruct((m, n_out), out_dtype),
        compiler_params=_params(("arbitrary",)),
    )(*[a for a, _ in pairs], *[b for _, b in pairs])


def mm_tn(a, b, out_dtype, name, tk=512):
    t, ka = a.shape
    n_out = b.shape[1]
    steps = t // tk

    def body(a_ref, b_ref, o_ref, acc_ref):
        k = pl.program_id(0)
        d = lax.dot_general(a_ref[...], b_ref[...], TN_DIMS, preferred_element_type=F32)

        @pl.when(k == 0)
        def _():
            acc_ref[...] = d

        @pl.when(k > 0)
        def _():
            acc_ref[...] += d

        @pl.when(k == steps - 1)
        def _():
            o_ref[...] = acc_ref[...].astype(o_ref.dtype)

    return _pcall(
        body, name=name, grid=(steps,),
        in_specs=[pl.BlockSpec((tk, ka), lambda k: (k, 0)), pl.BlockSpec((tk, n_out), lambda k: (k, 0))],
        out_specs=pl.BlockSpec((ka, n_out), lambda k: (0, 0)),
        out_shape=jax.ShapeDtypeStruct((ka, n_out), out_dtype),
        scratch_shapes=[pltpu.VMEM((ka, n_out), F32)],
        compiler_params=_params(("arbitrary",)),
    )(a, b)


def _sigmoid(x):
    return 1.0 / (1.0 + jnp.exp(-x))


def ffn_up(h, wg, wu, name, tm=512, tn=1408):
    t, d = h.shape
    fp = wg.shape[1]

    def body(h_ref, wg_ref, wu_ref, g_ref, u_ref, a_ref):
        hv = h_ref[...]
        g = jnp.dot(hv, wg_ref[...], preferred_element_type=F32)
        u = jnp.dot(hv, wu_ref[...], preferred_element_type=F32)
        g_ref[...] = g.astype(BF16)
        u_ref[...] = u.astype(BF16)
        a_ref[...] = (g * _sigmoid(g) * u).astype(BF16)

    w_spec = pl.BlockSpec((d, tn), lambda j, i: (0, j))
    o_spec = pl.BlockSpec((tm, tn), lambda j, i: (i, j))
    o_shape = jax.ShapeDtypeStruct((t, fp), BF16)
    return _pcall(
        body, name=name, grid=(fp // tn, t // tm),
        in_specs=[pl.BlockSpec((tm, d), lambda j, i: (i, 0)), w_spec, w_spec],
        out_specs=[o_spec, o_spec, o_spec], out_shape=[o_shape, o_shape, o_shape],
        compiler_params=_params(("arbitrary", "arbitrary")),
    )(h, wg, wu)


def ffn_down_bwd(dy0, wd, gate, up, name, tm=512, tn=1408):
    t, d = dy0.shape
    fp = wd.shape[0]

    def body(dy_ref, wd_ref, g_ref, u_ref, dg_ref, du_ref):
        dact = lax.dot_general(dy_ref[...], wd_ref[...], NT_DIMS, preferred_element_type=F32)
        g = g_ref[...].astype(F32)
        u = u_ref[...].astype(F32)
        sg = _sigmoid(g)
        du_ref[...] = (dact * g * sg).astype(BF16)
        dg_ref[...] = (dact * u * (sg * (1.0 + g * (1.0 - sg)))).astype(BF16)

    t_spec = pl.BlockSpec((tm, tn), lambda j, i: (i, j))
    o_shape = jax.ShapeDtypeStruct((t, fp), BF16)
    return _pcall(
        body, name=name, grid=(fp // tn, t // tm),
        in_specs=[pl.BlockSpec((tm, d), lambda j, i: (i, 0)), pl.BlockSpec((tn, d), lambda j, i: (j, 0)), t_spec, t_spec],
        out_specs=[t_spec, t_spec], out_shape=[o_shape, o_shape],
        compiler_params=_params(("arbitrary", "arbitrary")),
    )(dy0, wd, gate, up)


def _row_specs(dx, ts, ns):
    return pl.BlockSpec((ts, dx), lambda b, s: (b * ns + s, 0))


def _mod_spec():
    return pl.BlockSpec((1, N_MOD, D_MODEL), lambda b, s: (b, 0, 0))


def _vec_spec(dx):
    return pl.BlockSpec((1, dx), lambda b, s: (0, 0))


def prenorm_fwd(x, g, mod, i_shift, i_scale, nb, name, ts=512):
    t, dx = x.shape
    ns = t // nb // ts

    def body(*refs):
        if mod is None:
            x_ref, g_ref, h_ref = refs
        else:
            x_ref, g_ref, mod_ref, h_ref = refs
        xv = x_ref[...]
        r = lax.rsqrt(jnp.mean(xv * xv, axis=-1, keepdims=True) + EPS)
        h = xv * r * g_ref[...]
        if mod is not None:
            h = h * (1.0 + mod_ref[0, i_scale:i_scale + 1, :]) + mod_ref[0, i_shift:i_shift + 1, :]
        h_ref[...] = h.astype(BF16)

    in_specs = [_row_specs(dx, ts, ns), _vec_spec(dx)]
    args = [x, g]
    if mod is not None:
        in_specs.append(_mod_spec())
        args.append(mod)
    return _pcall(
        body, name=name, grid=(nb, ns), in_specs=in_specs, out_specs=_row_specs(dx, ts, ns),
        out_shape=jax.ShapeDtypeStruct((t, dx), BF16), compiler_params=_params(("arbitrary", "arbitrary")),
    )(*args)


def prenorm_bwd(dh, x, g, mod, i_scale, dres, nb, name, ts=512):
    t, dx = x.shape
    ns = t // nb // ts
    has_mod = mod is not None
    has_res = dres is not None

    def body(*refs):
        refs = list(refs)
        dh_ref, x_ref, g_ref = refs[:3]
        pos = 3
        mod_ref = dres_ref = None
        if has_mod:
            mod_ref = refs[pos]
            pos += 1
        if has_res:
            dres_ref = refs[pos]
            pos += 1
        dx_ref, dg_ref = refs[pos], refs[pos + 1]
        b, s = pl.program_id(0), pl.program_id(1)
        xv = x_ref[...]
        dhv = dh_ref[...].astype(F32)
        gv = g_ref[...]
        r = lax.rsqrt(jnp.mean(xv * xv, axis=-1, keepdims=True) + EPS)
        xhat = xv * r
        dn = dhv
        if has_mod:
            dsc_ref, dsh_ref = refs[pos + 2], refs[pos + 3]
            dn = dhv * (1.0 + mod_ref[0, i_scale:i_scale + 1, :])
            dsc = jnp.sum(dhv * xhat * gv, axis=0, keepdims=True)[None]
            dsh = jnp.sum(dhv, axis=0, keepdims=True)[None]

            @pl.when(s == 0)
            def _():
                dsc_ref[...] = dsc
                dsh_ref[...] = dsh

            @pl.when(s > 0)
            def _():
                dsc_ref[...] += dsc
                dsh_ref[...] += dsh

        dg = jnp.sum(dn * xhat, axis=0, keepdims=True)
        first = jnp.logical_and(b == 0, s == 0)

        @pl.when(first)
        def _():
            dg_ref[...] = dg

        @pl.when(jnp.logical_not(first))
        def _():
            dg_ref[...] += dg

        dxhat = dn * gv
        dxv = r * (dxhat - xhat * jnp.mean(dxhat * xhat, axis=-1, keepdims=True))
        if has_res:
            dxv = dxv + dres_ref[...]
        dx_ref[...] = dxv

    row = _row_specs(dx, ts, ns)
    in_specs = [row, row, _vec_spec(dx)]
    args = [dh, x, g]
    if has_mod:
        in_specs.append(_mod_spec())
        args.append(mod)
    if has_res:
        in_specs.append(row)
        args.append(dres)
    out_specs = [row, _vec_spec(dx)]
    out_shape = [jax.ShapeDtypeStruct((t, dx), F32), jax.ShapeDtypeStruct((1, dx), F32)]
    if has_mod:
        bspec = pl.BlockSpec((1, 1, dx), lambda b, s: (b, 0, 0))
        out_specs += [bspec, bspec]
        out_shape += [jax.ShapeDtypeStruct((nb, 1, dx), F32)] * 2
    return _pcall(
        body, name=name, grid=(nb, ns), in_specs=in_specs, out_specs=out_specs, out_shape=out_shape,
        compiler_params=_params(("arbitrary", "arbitrary")),
    )(*args)


def postnorm_fwd(x, y0, g, mod, i_gate, coef, nb, name, target=None, ts=512):
    t, dx = x.shape
    ns = t // nb // ts
    with_loss = target is not None

    def body(*refs):
        x_ref, y_ref, g_ref, mod_ref = refs[:4]
        yv = y_ref[...]
        r = lax.rsqrt(jnp.mean(yv * yv, axis=-1, keepdims=True) + EPS)
        out = x_ref[...] + (coef * mod_ref[0, i_gate:i_gate + 1, :]) * (yv * r * g_ref[...])
        if not with_loss:
            refs[4][...] = out
            return
        t_ref, dx_ref, loss_ref = refs[4:]
        b, s = pl.program_id(0), pl.program_id(1)
        err = out - t_ref[...]
        dx_ref[...] = err * (1.0 / dx)
        part = (0.5 / dx) * jnp.sum(jnp.sum(err * err, axis=1, keepdims=True), axis=0, keepdims=True)
        first = jnp.logical_and(b == 0, s == 0)

        @pl.when(first)
        def _():
            loss_ref[...] = part

        @pl.when(jnp.logical_not(first))
        def _():
            loss_ref[...] += part

    row = _row_specs(dx, ts, ns)
    in_specs = [row, row, _vec_spec(dx), _mod_spec()]
    args = [x, y0, g, mod]
    out_specs = row
    out_shape = jax.ShapeDtypeStruct((t, dx), F32)
    if with_loss:
        in_specs.append(row)
        args.append(target)
        out_specs = [row, pl.BlockSpec((1, 1), lambda b, s: (0, 0))]
        out_shape = [out_shape, jax.ShapeDtypeStruct((1, 1), F32)]
    return _pcall(
        body, name=name, grid=(nb, ns), in_specs=in_specs, out_specs=out_specs, out_shape=out_shape,
        compiler_params=_params(("arbitrary", "arbitrary")),
    )(*args)


def postnorm_bwd(dxo, y0, g, mod, i_gate, coef, nb, name, ts=512):
    t, dx = y0.shape
    ns = t // nb // ts

    def body(d_ref, y_ref, g_ref, mod_ref, dy_ref, dg_ref, dgate_ref):
        b, s = pl.program_id(0), pl.program_id(1)
        yv = y_ref[...]
        dv = d_ref[...]
        gv = g_ref[...]
        r = lax.rsqrt(jnp.mean(yv * yv, axis=-1, keepdims=True) + EPS)
        yhat = yv * r
        dgate = jnp.sum(dv * (coef * (yhat * gv)), axis=0, keepdims=True)[None]
        dyn = dv * (coef * mod_ref[0, i_gate:i_gate + 1, :])
        dg = jnp.sum(dyn * yhat, axis=0, keepdims=True)
        dyhat = dyn * gv
        dy_ref[...] = (r * (dyhat - yhat * jnp.mean(dyhat * yhat, axis=-1, keepdims=True))).astype(BF16)

        @pl.when(s == 0)
        def _():
            dgate_ref[...] = dgate

        @pl.when(s > 0)
        def _():
            dgate_ref[...] += dgate

        first = jnp.logical_and(b == 0, s == 0)

        @pl.when(first)
        def _():
            dg_ref[...] = dg

        @pl.when(jnp.logical_not(first))
        def _():
            dg_ref[...] += dg

    row = _row_specs(dx, ts, ns)
    return _pcall(
        body, name=name, grid=(nb, ns), in_specs=[row, row, _vec_spec(dx), _mod_spec()],
        out_specs=[row, _vec_spec(dx), pl.BlockSpec((1, 1, dx), lambda b, s: (b, 0, 0))],
        out_shape=[jax.ShapeDtypeStruct((t, dx), BF16), jax.ShapeDtypeStruct((1, dx), F32),
                   jax.ShapeDtypeStruct((nb, 1, dx), F32)],
        compiler_params=_params(("arbitrary", "arbitrary")),
    )(dxo, y0, g, mod)


def rope_tables(positions):
    inv_freq = ROPE_THETA ** (-jnp.arange(0, ROT_DIM, 2, dtype=F32) / ROT_DIM)
    ang = positions.astype(F32).reshape(-1, 1) * inv_freq
    cos, sin = jnp.cos(ang), jnp.sin(ang)
    half = ROT_DIM // 2
    z = lambda n: jnp.zeros((ang.shape[0], n), F32)
    c = jnp.concatenate([cos, cos, jnp.ones((ang.shape[0], HEAD_DIM - ROT_DIM), F32)], axis=1)
    sp = jnp.concatenate([z(half), sin, z(HEAD_DIM - ROT_DIM)], axis=1)
    sm = jnp.concatenate([-sin, z(HEAD_DIM - half)], axis=1)
    return tuple(jnp.tile(a, (1, HEADS_PER_STEP)) for a in (c, sp, sm))


def rope(xarr, col_block, width, tables, transpose, out_dtype, name, ts=512):
    t = xarr.shape[0]
    half = ROT_DIM // 2
    reps = width // LANES

    def body(x_ref, c_ref, sp_ref, sm_ref, o_ref):
        xv = x_ref[...].astype(F32)
        wide = lambda r: jnp.concatenate([r[...]] * reps, axis=1)
        c, sp, sm = wide(c_ref), wide(sp_ref), wide(sm_ref)
        if transpose:
            out = xv * c + pltpu.roll(xv * sp, width - half, 1) + pltpu.roll(xv * sm, half, 1)
        else:
            out = xv * c + pltpu.roll(xv, half, 1) * sp + pltpu.roll(xv, width - half, 1) * sm
        o_ref[...] = out.astype(o_ref.dtype)

    tab = pl.BlockSpec((ts, LANES), lambda i: (i, 0))
    return _pcall(
        body, name=name, grid=(t // ts,),
        in_specs=[pl.BlockSpec((ts, width), lambda i: (i, col_block)), tab, tab, tab],
        out_specs=pl.BlockSpec((ts, width), lambda i: (i, 0)),
        out_shape=jax.ShapeDtypeStruct((t, width), out_dtype), compiler_params=_params(("arbitrary",)),
    )(xarr, *tables)


def _scan_lanes(x, reverse):
    n = x.shape[-1]
    lane = lax.broadcasted_iota(jnp.int32, x.shape, x.ndim - 1)
    k = 1
    while k < n:
        if reverse:
            x = x + jnp.where(lane < n - k, pltpu.roll(x, n - k, x.ndim - 1), 0.0)
        else:
            x = x + jnp.where(lane >= k, pltpu.roll(x, k, x.ndim - 1), 0.0)
        k *= 2
    return x


def _log_sigmoid(z):
    return jnp.minimum(z, 0.0) - jnp.log(1.0 + jnp.exp(-jnp.abs(z)))


def fox_gate_fwd(ft, b_forget, name):
    nb, nh, s = ft.shape

    def body(f_ref, b_ref, o_ref):
        z = f_ref[0] + b_ref[...]
        o_ref[0] = -_scan_lanes(_log_sigmoid(z), False)

    spec = pl.BlockSpec((1, nh, s), lambda b: (b, 0, 0))
    return _pcall(
        body, name=name, grid=(nb,), in_specs=[spec, pl.BlockSpec((nh, 1), lambda b: (0, 0))], out_specs=spec,
        out_shape=jax.ShapeDtypeStruct((nb, nh, s), F32), compiler_params=_params(("arbitrary",)),
    )(ft, b_forget)


def fox_gate_bwd(dcb, drow, ft, b_forget, name):
    nb, nh, s = ft.shape

    def body(d_ref, r_ref, f_ref, b_ref, dz_ref, db_ref):
        b = pl.program_id(0)
        z = f_ref[0] + b_ref[...]
        dlf = _scan_lanes(r_ref[0] - d_ref[0], True)
        dz = dlf * _sigmoid(-z)
        dz_ref[0] = dz
        db = jnp.sum(dz, axis=1, keepdims=True)

        @pl.when(b == 0)
        def _():
            db_ref[...] = db

        @pl.when(b > 0)
        def _():
            db_ref[...] += db

    spec = pl.BlockSpec((1, nh, s), lambda b: (b, 0, 0))
    vec = pl.BlockSpec((nh, 1), lambda b: (0, 0))
    return _pcall(
        body, name=name, grid=(nb,), in_specs=[spec, spec, spec, vec], out_specs=[spec, vec],
        out_shape=[jax.ShapeDtypeStruct((nb, nh, s), F32), jax.ShapeDtypeStruct((nh, 1), F32)],
        compiler_params=_params(("arbitrary",)),
    )(dcb, drow, ft, b_forget)


ATTN_TQ = 512
ATTN_TK = 256


def _block_delta(s, tq, tk):
    off = jnp.arange(s // tk) - (tq // tk - 1)
    return off[:, None, None] * tk + jnp.arange(tq)[None, None, :] - jnp.arange(tk)[None, :, None]


def dilated_table(s, tq, tk):
    delta = _block_delta(s, tq, tk)
    count = jnp.zeros(delta.shape, F32)
    for window, dil in DILATED_PATTERNS:
        count = count + ((delta >= 0) & (delta <= window) & (delta % dil == 0)).astype(F32)
    return jnp.where(count > 0, jnp.log(jnp.maximum(count, 1.0)), NEG)


def causal_table(s, tq, tk):
    return jnp.where(_block_delta(s, tq, tk) >= 0, 0.0, NEG).astype(F32)


def attn_fwd(q_arr, q_off, k_arr, k_off, v_arr, v_off, table, colbias, nb, name):
    t = q_arr.shape[0]
    s = t // nb
    tk, tq = table.shape[1:]
    nq, nk, r = s // tq, s // tk, tq // tk
    npairs = WIDTH_A // LANES
    use_cb = colbias is not None

    def body(*refs):
        refs = list(refs)
        q_ref, k_ref, v_ref, tab_ref = refs[:4]
        cb_ref = refs[4] if use_cb else None
        o_ref, lse_ref, vt_s, acc_s = refs[-4:]
        qi = pl.program_id(2)

        @pl.when(qi == 0)
        def _():
            for cblk in range(nk):
                vt_s[cblk] = v_ref[cblk * tk:(cblk + 1) * tk, :].astype(F32).T.astype(BF16)

        heads = [slice(h * HEAD_DIM, (h + 1) * HEAD_DIM) for h in range(HEADS_PER_STEP)]
        qs = [(q_ref[:, hs].astype(F32) * ATTN_SCALE).astype(BF16) for hs in heads]
        acc_s[...] = jnp.zeros_like(acc_s)

        def step(kb, carry):
            ks = pl.multiple_of(kb * tk, tk)
            tab = tab_ref[qi * r + (r - 1) - kb]
            out = []
            for h, hs in enumerate(heads):
                m, l = carry[h]
                st = lax.dot_general(k_ref[pl.ds(ks, tk), hs], qs[h], NT_DIMS, preferred_element_type=F32) + tab
                if use_cb:
                    st = st + cb_ref[0, h, pl.ds(ks, tk), :]
                m_new = jnp.maximum(m, jnp.max(st, axis=0, keepdims=True))
                alpha = jnp.exp(m - m_new)
                pt = jnp.exp(st - m_new)
                l = alpha * l + jnp.sum(pt, axis=0, keepdims=True)
                acc_s[hs, :] = alpha * acc_s[hs, :] + jnp.dot(vt_s[kb, hs, :], pt.astype(BF16),
                                                              preferred_element_type=F32)
                out.append((m_new, l))
            return tuple(out)

        init = tuple((jnp.full((1, tq), NEG, F32), jnp.zeros((1, tq), F32)) for _ in heads)
        fin = lax.fori_loop(0, (qi + 1) * r, step, init)
        for h, hs in enumerate(heads):
            m, l = fin[h]
            acc_s[hs, :] = acc_s[hs, :] / l
            lse_ref[0, h, 0] = m + jnp.log(l)
        o_ref[...] = acc_s[...].T

    def seq_spec(off):
        return pl.BlockSpec((s, LANES), lambda b, j, i: (b, off + j))

    in_specs = [pl.BlockSpec((tq, LANES), lambda b, j, i: (b * nq + i, q_off + j)), seq_spec(k_off), seq_spec(v_off),
                pl.BlockSpec(table.shape, lambda b, j, i: (0, 0, 0))]
    args = [q_arr, k_arr, v_arr, table]
    if use_cb:
        in_specs.append(pl.BlockSpec((1, HEADS_PER_STEP, s, 1), lambda b, j, i: (b, j, 0, 0)))
        args.append(colbias)
    n_heads = npairs * HEADS_PER_STEP
    return _pcall(
        body, name=name, grid=(nb, npairs, nq), in_specs=in_specs,
        out_specs=[pl.BlockSpec((tq, LANES), lambda b, j, i: (b * nq + i, j)),
                   pl.BlockSpec((1, HEADS_PER_STEP, 1, 1, tq), lambda b, j, i: (b, j, i, 0, 0))],
        out_shape=[jax.ShapeDtypeStruct((t, npairs * LANES), F32), jax.ShapeDtypeStruct((nb, n_heads, nq, 1, tq), F32)],
        scratch_shapes=[pltpu.VMEM((nk, LANES, tk), BF16), pltpu.VMEM((LANES, tq), F32)],
        compiler_params=_params(("arbitrary", "arbitrary", "arbitrary")),
    )(*args)


def attn_bwd(q_arr, q_off, k_arr, k_off, v_arr, v_off, o_arr, lse_arr, do_arr, table, colbias, nb, qk_dtype, name):
    t = q_arr.shape[0]
    s = t // nb
    tk, tq = table.shape[1:]
    nq, nk, r = s // tq, s // tk, tq // tk
    npairs = WIDTH_A // LANES
    use_cb = colbias is not None

    def body(*refs):
        refs = list(refs)
        q_ref, k_ref, v_ref, o_ref, lse_ref, do_ref, tab_ref = refs[:7]
        pos = 7
        cb_ref = None
        if use_cb:
            cb_ref = refs[pos]
            pos += 1
        dq_ref, dk_ref, dv_ref = refs[pos:pos + 3]
        pos += 3
        dcb_ref = drow_ref = None
        if use_cb:
            dcb_ref, drow_ref = refs[pos:pos + 2]
            pos += 2
        kt_s, dk_s, dv_s, dqt_s = refs[pos:pos + 4]
        dcb_s = refs[pos + 4] if use_cb else None

        heads = [slice(h * HEAD_DIM, (h + 1) * HEAD_DIM) for h in range(HEADS_PER_STEP)]
        for cblk in range(nk):
            kt_s[cblk] = k_ref[cblk * tk:(cblk + 1) * tk, :].astype(F32).T.astype(BF16)
        dk_s[...] = jnp.zeros_like(dk_s)
        dv_s[...] = jnp.zeros_like(dv_s)
        if use_cb:
            dcb_s[...] = jnp.zeros_like(dcb_s)
        ones = jnp.ones((8, HEAD_DIM), BF16)

        def q_loop(qi, carry):
            qs = pl.multiple_of(qi * tq, tq)
            q, do_b, lse, dsum = [], [], [], []
            for h, hs in enumerate(heads):
                q.append((q_ref[pl.ds(qs, tq), hs].astype(F32) * ATTN_SCALE).astype(BF16))
                do = do_ref[pl.ds(qs, tq), hs]
                do_b.append(do.astype(BF16))
                lse.append(lse_ref[0, h, qi])
                prod = do * o_ref[pl.ds(qs, tq), hs]
                hi = prod.astype(BF16)
                lo = (prod - hi.astype(F32)).astype(BF16)
                dsum.append((lax.dot_general(ones, hi, NT_DIMS, preferred_element_type=F32)
                             + lax.dot_general(ones, lo, NT_DIMS, preferred_element_type=F32))[0:1, :])
            dqt_s[...] = jnp.zeros_like(dqt_s)

            def k_loop(kb, drow):
                ks = pl.multiple_of(kb * tk, tk)
                tab = tab_ref[qi * r + (r - 1) - kb]
                out = []
                for h, hs in enumerate(heads):
                    k = k_ref[pl.ds(ks, tk), hs]
                    v = v_ref[pl.ds(ks, tk), hs]
                    st = lax.dot_general(k, q[h], NT_DIMS, preferred_element_type=F32) + tab
                    if use_cb:
                        st = st + cb_ref[0, h, pl.ds(ks, tk), :]
                    pt = jnp.exp(st - lse[h])
                    dpt = lax.dot_general(v, do_b[h], NT_DIMS, preferred_element_type=F32)
                    dst = pt * (dpt - dsum[h])
                    dst_b = dst.astype(BF16)
                    dv_s[h, pl.ds(ks, tk), :] += jnp.dot(pt.astype(BF16), do_b[h], preferred_element_type=F32)
                    dk_s[h, pl.ds(ks, tk), :] += jnp.dot(dst_b, q[h], preferred_element_type=F32)
                    dqt_s[hs, :] += jnp.dot(kt_s[kb, hs, :], dst_b, preferred_element_type=F32)
                    if use_cb:
                        dcb_s[h, pl.ds(ks, tk), :] += jnp.sum(dst, axis=1, keepdims=True)
                        out.append(drow[h] + jnp.sum(dst, axis=0, keepdims=True))
                    else:
                        out.append(drow[h])
                return tuple(out)

            drow = lax.fori_loop(0, (qi + 1) * r, k_loop, tuple(jnp.zeros((1, tq), F32) for _ in heads))
            dq_ref[pl.ds(qs, tq), :] = (dqt_s[...] * ATTN_SCALE).T.astype(dq_ref.dtype)
            if use_cb:
                for h in range(HEADS_PER_STEP):
                    drow_ref[0, h, qi] = drow[h]
            return carry

        lax.fori_loop(0, nq, q_loop, 0)
        for h, hs in enumerate(heads):
            dk_ref[:, hs] = dk_s[h].astype(dk_ref.dtype)
            dv_ref[:, hs] = dv_s[h].astype(dv_ref.dtype)
            if use_cb:
                dcb_ref[0, h] = dcb_s[h]

    def seq_spec(off):
        return pl.BlockSpec((s, LANES), lambda b, j: (b, off + j))

    row_spec = pl.BlockSpec((1, HEADS_PER_STEP, nq, 1, tq), lambda b, j: (b, j, 0, 0, 0))
    in_specs = [seq_spec(q_off), seq_spec(k_off), seq_spec(v_off), seq_spec(0), row_spec, seq_spec(0),
                pl.BlockSpec(table.shape, lambda b, j: (0, 0, 0))]
    args = [q_arr, k_arr, v_arr, o_arr, lse_arr, do_arr, table]
    width = npairs * LANES
    out_specs = [seq_spec(0)] * 3
    out_shape = [jax.ShapeDtypeStruct((t, width), qk_dtype), jax.ShapeDtypeStruct((t, width), qk_dtype),
                 jax.ShapeDtypeStruct((t, width), BF16)]
    scratch = [pltpu.VMEM((nk, LANES, tk), BF16), pltpu.VMEM((HEADS_PER_STEP, s, HEAD_DIM), F32),
               pltpu.VMEM((HEADS_PER_STEP, s, HEAD_DIM), F32), pltpu.VMEM((LANES, tq), F32)]
    if use_cb:
        cb_spec = pl.BlockSpec((1, HEADS_PER_STEP, s, 1), lambda b, j: (b, j, 0, 0))
        in_specs.append(cb_spec)
        args.append(colbias)
        out_specs += [cb_spec, row_spec]
        out_shape += [jax.ShapeDtypeStruct(colbias.shape, F32), jax.ShapeDtypeStruct(lse_arr.shape, F32)]
        scratch.append(pltpu.VMEM((HEADS_PER_STEP, s, 1), F32))
    return _pcall(
        body, name=name, grid=(nb, npairs), in_specs=in_specs, out_specs=out_specs, out_shape=out_shape,
        scratch_shapes=scratch, compiler_params=_params(("arbitrary", "arbitrary")),
    )(*args)


def ada_fwd(c_all, w_ada, b_cols, name):
    def body(c_ref, w_ref, b_ref, o_ref):
        cv = c_ref[...]
        sc = (cv * _sigmoid(cv)).astype(BF16)
        o_ref[...] = jnp.dot(sc, w_ref[...].astype(BF16), preferred_element_type=F32) + b_ref[...]

    return _pcall(body, name=name, out_shape=jax.ShapeDtypeStruct((c_all.shape[0], w_ada.shape[1]), F32),
                  compiler_params=_params())(c_all, w_ada, b_cols)


def ada_bwd(c_all, dmod_cols, name):
    def body(c_ref, d_ref, o_ref):
        cv = c_ref[...]
        sc = (cv * _sigmoid(cv)).astype(BF16)
        o_ref[...] = lax.dot_general(sc, d_ref[...].astype(BF16), TN_DIMS, preferred_element_type=F32)

    return _pcall(body, name=name, out_shape=jax.ShapeDtypeStruct((c_all.shape[1], dmod_cols.shape[1]), F32),
                  compiler_params=_params())(c_all, dmod_cols)


def adamw(parts, group, w, m, v, name, tr=None):
    n = parts.shape[0]
    r, c = w.shape
    tr = r if tr is None else tr
    c1 = 1.0 - ADAM_B1 ** ADAM_STEP
    c2 = 1.0 - ADAM_B2 ** ADAM_STEP

    def body(p_ref, w_ref, m_ref, v_ref, g_ref, d_ref, nm_ref, nv_ref):
        g = p_ref[0, 0].astype(F32)
        for i in range(1, n):
            g = g + p_ref[i, 0].astype(F32)
        wv = w_ref[...]
        nm = ADAM_B1 * m_ref[...] + (1.0 - ADAM_B1) * g
        nv = ADAM_B2 * v_ref[...] + (1.0 - ADAM_B2) * (g * g)
        g_ref[...] = g
        nm_ref[...] = nm
        nv_ref[...] = nv
        d_ref[...] = -ADAM_LR * ((nm / c1) / (jnp.sqrt(nv / c2) + ADAM_EPS) + ADAM_WD * wv)

    spec = pl.BlockSpec((tr, c), lambda i: (i, 0))
    shape = jax.ShapeDtypeStruct((r, c), F32)
    return _pcall(
        body, name=name, grid=(r // tr,),
        in_specs=[pl.BlockSpec((n, 1, tr, c), lambda i: (0, group, i, 0)), spec, spec, spec],
        out_specs=[spec] * 4, out_shape=[shape] * 4, compiler_params=_params(("arbitrary",)),
    )(parts, w, m, v)


def _place():
    return lax.axis_index("x"), lax.axis_index("y"), lax.axis_index("c")


def _slot(p):
    return 4 * p[0] + 2 * p[1] + p[2]


def all_gather(arrs, name):
    n = len(arrs)
    hbm = pl.BlockSpec(memory_space=pl.ANY)

    def body(*refs):
        ins, outs = refs[:n], refs[n:2 * n]
        send_sems, recv_sems, local_sems = refs[2 * n:]
        x, y, c = _place()
        me, sibling = (x, y, c), (x, y, 1 - c)
        chips = [(1 - x, y), (x, 1 - y), (1 - x, 1 - y)]

        def copy(a, k, block, to, src=None):
            dst = outs[a].at[_slot(block)]
            return pltpu.make_async_remote_copy(
                src_ref=dst if src is None else src, dst_ref=dst, send_sem=send_sems.at[a * 7 + k],
                recv_sem=recv_sems.at[a * 7 + k], device_id=to, device_id_type=MESH)

        mine = [pltpu.make_async_copy(ins[a], outs[a].at[_slot(me)], local_sems.at[a]) for a in range(n)]
        for cp in mine:
            cp.start()
        first = []
        for a in range(n):
            first.append(copy(a, 0, me, sibling, src=ins[a]))
            first += [copy(a, 1 + j, me, (*chip, c), src=ins[a]) for j, chip in enumerate(chips)]
        for cp in first:
            cp.start()
        passed = []
        for a in range(n):
            for j, chip in enumerate(chips):
                copy(a, 1 + j, (*chip, c), me).wait_recv()
                cp = copy(a, 4 + j, (*chip, c), sibling)
                cp.start()
                passed.append(cp)
        for a in range(n):
            copy(a, 0, sibling, me).wait_recv()
            for j, chip in enumerate(chips):
                copy(a, 4 + j, (*chip, 1 - c), me).wait_recv()
        for cp in first + passed:
            cp.wait_send()
        for cp in mine:
            cp.wait()

    return _pcall(
        body, name=name, in_specs=[hbm] * n, out_specs=[hbm] * n,
        out_shape=[jax.ShapeDtypeStruct((N_DEV,) + a.shape, a.dtype) for a in arrs],
        scratch_shapes=[pltpu.SemaphoreType.DMA((7 * n,)), pltpu.SemaphoreType.DMA((7 * n,)),
                        pltpu.SemaphoreType.DMA((n,))],
        compiler_params=pltpu.CompilerParams(has_side_effects=True),
    )(*arrs)


def exchange(arrs, name):
    n = len(arrs)
    hbm = pl.BlockSpec(memory_space=pl.ANY)

    def body(*refs):
        ins, outs = refs[:n], refs[n:2 * n]
        send_sems, recv_sems, local_sems = refs[2 * n:]
        x, y, c = _place()
        me = (x, y, c)
        flip = lambda v, bit: 1 - v if bit else v
        peers = [(flip(x, k & 4), flip(y, k & 2), flip(c, k & 1)) for k in range(1, N_DEV)]
        mine = [pltpu.make_async_copy(ins[a].at[_slot(me)], outs[a].at[_slot(me)], local_sems.at[a]) for a in range(n)]
        for cp in mine:
            cp.start()
        sends = []
        for a in range(n):
            for k, peer in enumerate(peers):
                sends.append(pltpu.make_async_remote_copy(
                    src_ref=ins[a].at[_slot(peer)], dst_ref=outs[a].at[_slot(me)], send_sem=send_sems.at[a * 7 + k],
                    recv_sem=recv_sems.at[a * 7 + k], device_id=peer, device_id_type=MESH))
        for cp in sends:
            cp.start()
        for a in range(n):
            for k, peer in enumerate(peers):
                pltpu.make_async_remote_copy(
                    src_ref=ins[a].at[_slot(me)], dst_ref=outs[a].at[_slot(peer)], send_sem=send_sems.at[a * 7 + k],
                    recv_sem=recv_sems.at[a * 7 + k], device_id=peer, device_id_type=MESH).wait_recv()
        for cp in sends:
            cp.wait_send()
        for cp in mine:
            cp.wait()

    return _pcall(
        body, name=name, in_specs=[hbm] * n, out_specs=[hbm] * n,
        out_shape=[jax.ShapeDtypeStruct(a.shape, a.dtype) for a in arrs],
        scratch_shapes=[pltpu.SemaphoreType.DMA((7 * n,)), pltpu.SemaphoreType.DMA((7 * n,)),
                        pltpu.SemaphoreType.DMA((n,))],
        compiler_params=pltpu.CompilerParams(has_side_effects=True),
    )(*arrs)


def _cols_from_blocks(blocks, pad_to=None):
    r = blocks.shape[1]
    full = blocks.transpose(1, 0, 2).reshape(r, -1)
    if pad_to is not None and pad_to > full.shape[1]:
        full = jnp.pad(full, ((0, 0), (0, pad_to - full.shape[1])))
    return full


def _cols_to_blocks(full, ncols):
    r = full.shape[0]
    return full[:, :ncols].reshape(r, N_DEV, ncols // N_DEV).transpose(1, 0, 2)


def _rows_from_blocks(blocks, pad_to=None):
    full = blocks.reshape(-1, blocks.shape[2])
    if pad_to is not None and pad_to > full.shape[0]:
        full = jnp.pad(full, ((0, pad_to - full.shape[0]), (0, 0)))
    return full


def _rows_to_blocks(full, nrows):
    return full[:nrows].reshape(N_DEV, nrows // N_DEV, full.shape[1])


SMALL_ORDER = ("g_pre_ff1", "g_post_ff1", "g_pre_mix", "g_post_mix", "g_out_a", "g_out_b", "g_pre_ff2", "g_post_ff2",
               "b_forget")


def _pack_small(vals):
    rows = []
    for name in SMALL_ORDER:
        v = vals[name].reshape(1, -1)
        if v.shape[1] % LANES:
            v = jnp.pad(v, ((0, 0), (0, LANES - v.shape[1] % LANES)))
        rows.append(v)
    return jnp.concatenate(rows, axis=1)


def _unpack_small(row, sizes):
    out, pos = {}, 0
    for name in SMALL_ORDER:
        n = sizes[name]
        out[name] = row[:, pos:pos + n]
        pos += -(-n // LANES) * LANES
    return out


def _ffn_forward(x, mod, g_pre, g_post, wg, wu, wd, i0, nb, tag, target=None):
    h = prenorm_fwd(x, g_pre, mod, i0, i0 + 1, nb, f"{tag}_prenorm")
    gate, up, act = ffn_up(h, wg, wu, f"{tag}_up")
    y0 = mm_rows([(act, wd)], False, F32, f"{tag}_down")
    out = postnorm_fwd(x, y0, g_post, mod, i0 + 2, 0.5, nb, f"{tag}_postnorm", target=target)
    return out, (x, h, gate, up, act, y0)


def _ffn_backward(dxo, saved, mod, g_pre, g_post, wg, wu, wd, i0, nb, tag):
    x, h, gate, up, act, y0 = saved
    dy0, dg_post, dgate_mod = postnorm_bwd(dxo, y0, g_post, mod, i0 + 2, 0.5, nb, f"{tag}_postnorm_bwd")
    dwd = mm_tn(act, dy0, BF16, f"{tag}_dwd")
    dgate, dup = ffn_down_bwd(dy0, wd, gate, up, f"{tag}_down_bwd")
    dwg = mm_tn(h, dgate, BF16, f"{tag}_dwg")
    dwu = mm_tn(h, dup, BF16, f"{tag}_dwu")
    dh = mm_rows([(dgate, wg), (dup, wu)], True, F32, f"{tag}_dh")
    dx, dg_pre, dsc, dsh = prenorm_bwd(dh, x, g_pre, mod, i0 + 1, dxo, nb, f"{tag}_prenorm_bwd")
    return dx, dict(g_pre=dg_pre, g_post=dg_post, wg=dwg, wu=dwu, wd=dwd, mod=(dsh, dsc, dgate_mod))


def kernel(x, c, positions, w_ada, b_ada, g_pre_ff1, g_post_ff1, w_ff1_gate, w_ff1_up, w_ff1_down, g_pre_mix, g_post_mix, w_in, b_forget, g_out_a, g_out_b, w_out, g_pre_ff2, g_post_ff2, w_ff2_gate, w_ff2_up, w_ff2_down, loss_target, m_w_ada, m_b_ada, m_g_pre_ff1, m_g_post_ff1, m_w_ff1_gate, m_w_ff1_up, m_w_ff1_down, m_g_pre_mix, m_g_post_mix, m_w_in, m_b_forget, m_g_out_a, m_g_out_b, m_w_out, m_g_pre_ff2, m_g_post_ff2, m_w_ff2_gate, m_w_ff2_up, m_w_ff2_down, v_w_ada, v_b_ada, v_g_pre_ff1, v_g_post_ff1, v_w_ff1_gate, v_w_ff1_up, v_w_ff1_down, v_g_pre_mix, v_g_post_mix, v_w_in, v_b_forget, v_g_out_a, v_g_out_b, v_w_out, v_g_pre_ff2, v_g_post_ff2, v_w_ff2_gate, v_w_ff2_up, v_w_ff2_down):
    weights = dict(w_ada=w_ada, b_ada=b_ada, g_pre_ff1=g_pre_ff1, g_post_ff1=g_post_ff1, w_ff1_gate=w_ff1_gate,
                   w_ff1_up=w_ff1_up, w_ff1_down=w_ff1_down, g_pre_mix=g_pre_mix, g_post_mix=g_post_mix, w_in=w_in,
                   b_forget=b_forget, g_out_a=g_out_a, g_out_b=g_out_b, w_out=w_out, g_pre_ff2=g_pre_ff2,
                   g_post_ff2=g_post_ff2, w_ff2_gate=w_ff2_gate, w_ff2_up=w_ff2_up, w_ff2_down=w_ff2_down)
    mom_m = dict(w_ada=m_w_ada, b_ada=m_b_ada, g_pre_ff1=m_g_pre_ff1, g_post_ff1=m_g_post_ff1, w_ff1_gate=m_w_ff1_gate,
                 w_ff1_up=m_w_ff1_up, w_ff1_down=m_w_ff1_down, g_pre_mix=m_g_pre_mix, g_post_mix=m_g_post_mix,
                 w_in=m_w_in, b_forget=m_b_forget, g_out_a=m_g_out_a, g_out_b=m_g_out_b, w_out=m_w_out,
                 g_pre_ff2=m_g_pre_ff2, g_post_ff2=m_g_post_ff2, w_ff2_gate=m_w_ff2_gate, w_ff2_up=m_w_ff2_up,
                 w_ff2_down=m_w_ff2_down)
    mom_v = dict(w_ada=v_w_ada, b_ada=v_b_ada, g_pre_ff1=v_g_pre_ff1, g_post_ff1=v_g_post_ff1, w_ff1_gate=v_w_ff1_gate,
                 w_ff1_up=v_w_ff1_up, w_ff1_down=v_w_ff1_down, g_pre_mix=v_g_pre_mix, g_post_mix=v_g_post_mix,
                 w_in=v_w_in, b_forget=v_b_forget, g_out_a=v_g_out_a, g_out_b=v_g_out_b, w_out=v_w_out,
                 g_pre_ff2=v_g_pre_ff2, g_post_ff2=v_g_post_ff2, w_ff2_gate=v_w_ff2_gate, w_ff2_up=v_w_ff2_up,
                 w_ff2_down=v_w_ff2_down)
    order = list(weights)

    nb, s, d = x.shape
    t = nb * s
    me = _slot(_place())
    nbg = nb * N_DEV
    ada_cols = w_ada.shape[2]

    ff_cols = jnp.stack([w_ff1_gate[0], w_ff1_up[0], w_ff2_gate[0], w_ff2_up[0]]).astype(BF16)
    ff_rows = jnp.stack([w_ff1_down[0], w_ff2_down[0]]).astype(BF16)
    c_all, ff_cols_all, ff_rows_all, w_in_all, w_out_all = all_gather(
        [c, ff_cols, ff_rows, w_in[0].astype(BF16), w_out[0].astype(BF16)], "gather_weights")
    c_all = c_all.reshape(nbg, d)
    wg1, wu1, wg2, wu2 = (_cols_from_blocks(ff_cols_all[:, i], D_FF_PAD) for i in range(4))
    wd1, wd2 = (_rows_from_blocks(ff_rows_all[:, i], D_FF_PAD) for i in range(2))
    w_in_full = _cols_from_blocks(w_in_all)
    n_qkv = 3 * (WIDTH_A + WIDTH_B)
    w_qkv = w_in_full[:, :n_qkv]
    w_f = jnp.pad(w_in_full[:, n_qkv:], ((0, 0), (0, LANES - N_HEADS_B)))
    w_o = _rows_from_blocks(w_out_all)
    w_o_a, w_o_b = w_o[:WIDTH_A], w_o[WIDTH_A:]

    b_cols = lax.dynamic_slice(b_ada, (0, me * ada_cols), (1, ada_cols))
    mod_cols = ada_fwd(c_all, w_ada[0], b_cols, "ada_fwd")
    (mod_all,) = all_gather([mod_cols], "gather_mod")
    mod = lax.dynamic_slice(mod_all, (0, me * nb, 0), (N_DEV, nb, ada_cols))
    mod = mod.transpose(1, 0, 2).reshape(nb, N_MOD, d)

    xf = x.reshape(t, d)
    target = loss_target.reshape(t, d)

    x1, saved1 = _ffn_forward(xf, mod, g_pre_ff1, g_post_ff1, wg1, wu1, wd1, 0, nb, "ff1")

    h2 = prenorm_fwd(x1, g_pre_mix, mod, 3, 4, nb, "mix_prenorm")
    proj = mm_rows([(h2, w_qkv)], False, BF16, "mix_proj")
    f_logit = mm_rows([(h2, w_f)], False, F32, "mix_forget")
    tables = rope_tables(positions)
    qk_rot = rope(proj, 0, 2 * WIDTH_A, tables, False, BF16, "rope")
    tab_a = dilated_table(s, ATTN_TQ, ATTN_TK)
    tab_b = causal_table(s, ATTN_TQ, ATTN_TK)
    ft = f_logit[:, :N_HEADS_B].reshape(nb, s, N_HEADS_B).transpose(0, 2, 1)
    bf_col = b_forget.reshape(N_HEADS_B, 1)
    colbias = fox_gate_fwd(ft, bf_col, "fox_gate").reshape(nb, N_HEADS_B, s, 1)
    pa = WIDTH_A // LANES
    o_a, lse_a = attn_fwd(qk_rot, 0, qk_rot, pa, proj, 2 * pa, tab_a, None, nb, "attn_a")
    o_b, lse_b = attn_fwd(proj, 3 * pa, proj, 4 * pa, proj, 5 * pa, tab_b, colbias, nb, "attn_b")
    m_a = prenorm_fwd(o_a, g_out_a, None, None, None, nb, "out_norm_a")
    m_b = prenorm_fwd(o_b, g_out_b, None, None, None, nb, "out_norm_b")
    y0m = mm_rows([(m_a, w_o_a), (m_b, w_o_b)], False, F32, "mix_out")
    x2 = postnorm_fwd(x1, y0m, g_post_mix, mod, 5, 1.0, nb, "mix_postnorm")

    (dx3, loss_part), saved2 = _ffn_forward(x2, mod, g_pre_ff2, g_post_ff2, wg2, wu2, wd2, 6, nb, "ff2", target=target)
    loss = lax.psum(loss_part[0, 0], ("x", "y", "c"))

    dx2, gr2 = _ffn_backward(dx3, saved2, mod, g_pre_ff2, g_post_ff2, wg2, wu2, wd2, 6, nb, "ff2")

    dy0m, dg_post_mix, dgate_mix = postnorm_bwd(dx2, y0m, g_post_mix, mod, 5, 1.0, nb, "mix_postnorm_bwd")
    dw_o_a = mm_tn(m_a, dy0m, BF16, "mix_dwo_a")
    dw_o_b = mm_tn(m_b, dy0m, BF16, "mix_dwo_b")
    dm_a = mm_rows([(dy0m, w_o_a)], True, F32, "mix_dm_a")
    dm_b = mm_rows([(dy0m, w_o_b)], True, F32, "mix_dm_b")
    do_a, dg_out_a = prenorm_bwd(dm_a, o_a, g_out_a, None, None, None, nb, "out_norm_a_bwd")
    do_b, dg_out_b = prenorm_bwd(dm_b, o_b, g_out_b, None, None, None, nb, "out_norm_b_bwd")
    dq_a, dk_a, dv_a = attn_bwd(qk_rot, 0, qk_rot, pa, proj, 2 * pa, o_a, lse_a, do_a, tab_a, None, nb, F32, "attn_a_bwd")
    dq_b, dk_b, dv_b, dcb, drow = attn_bwd(proj, 3 * pa, proj, 4 * pa, proj, 5 * pa, o_b, lse_b, do_b, tab_b, colbias, nb,
                                     BF16, "attn_b_bwd")
    dq_a = rope(dq_a, 0, WIDTH_A, tables, True, BF16, "rope_bwd_q")
    dk_a = rope(dk_a, 0, WIDTH_A, tables, True, BF16, "rope_bwd_k")
    dz_t, db_forget = fox_gate_bwd(dcb.reshape(nb, N_HEADS_B, s), drow.reshape(nb, N_HEADS_B, s), ft, bf_col,
                                   "fox_gate_bwd")
    dz = jnp.pad(dz_t.transpose(0, 2, 1).reshape(t, N_HEADS_B), ((0, 0), (0, LANES - N_HEADS_B))).astype(BF16)
    pieces = [dq_a, dk_a, dv_a, dq_b, dk_b, dv_b]
    w_pieces = [w_qkv[:, i * WIDTH_A:(i + 1) * WIDTH_A] for i in range(6)]
    dh2 = mm_rows(list(zip(pieces, w_pieces)) + [(dz, w_f)], True, F32, "mix_dh")
    dw_in = jnp.concatenate([mm_tn(h2, p, BF16, f"mix_dwin_{i}") for i, p in enumerate(pieces)]
                            + [mm_tn(h2, dz, BF16, "mix_dwin_f")[:, :N_HEADS_B]], axis=1)
    dx1, dg_pre_mix, dsc_mix, dsh_mix = prenorm_bwd(dh2, x1, g_pre_mix, mod, 4, dx2, nb, "mix_prenorm_bwd")

    dx0, gr1 = _ffn_backward(dx1, saved1, mod, g_pre_ff1, g_post_ff1, wg1, wu1, wd1, 0, nb, "ff1")
    grad_x = dx0.reshape(nb, s, d)

    dmod = jnp.concatenate(list(gr1["mod"]) + [dsh_mix, dsc_mix, dgate_mix] + list(gr2["mod"]), axis=1)
    small = _pack_small(dict(g_pre_ff1=gr1["g_pre"], g_post_ff1=gr1["g_post"], g_pre_mix=dg_pre_mix,
                             g_post_mix=dg_post_mix, g_out_a=dg_out_a, g_out_b=dg_out_b, g_pre_ff2=gr2["g_pre"],
                             g_post_ff2=gr2["g_post"], b_forget=db_forget))
    dmod_all, small_all = all_gather([dmod.reshape(nb, N_MOD * d), small], "gather_small_grads")
    dmod_all = dmod_all.reshape(nbg, N_MOD * d)
    gcols = jnp.stack([_cols_to_blocks(g, D_FF) for g in (gr1["wg"], gr1["wu"], gr2["wg"], gr2["wu"])], axis=1)
    grows = jnp.stack([_rows_to_blocks(g, D_FF) for g in (gr1["wd"], gr2["wd"])], axis=1)
    g_in = _cols_to_blocks(dw_in, dw_in.shape[1])[:, None]
    g_out = _rows_to_blocks(jnp.concatenate([dw_o_a, dw_o_b], axis=0), d)[:, None]
    gcols, grows, g_in, g_out = exchange([gcols, grows, g_in, g_out], "exchange_grads")

    res = {}
    res["w_ff1_gate"] = adamw(gcols, 0, w_ff1_gate[0], m_w_ff1_gate[0], v_w_ff1_gate[0], "adamw_ff1_gate", tr=256)
    res["w_ff1_up"] = adamw(gcols, 1, w_ff1_up[0], m_w_ff1_up[0], v_w_ff1_up[0], "adamw_ff1_up", tr=256)
    res["w_ff2_gate"] = adamw(gcols, 2, w_ff2_gate[0], m_w_ff2_gate[0], v_w_ff2_gate[0], "adamw_ff2_gate", tr=256)
    res["w_ff2_up"] = adamw(gcols, 3, w_ff2_up[0], m_w_ff2_up[0], v_w_ff2_up[0], "adamw_ff2_up", tr=256)
    res["w_ff1_down"] = adamw(grows, 0, w_ff1_down[0], m_w_ff1_down[0], v_w_ff1_down[0], "adamw_ff1_down")
    res["w_ff2_down"] = adamw(grows, 1, w_ff2_down[0], m_w_ff2_down[0], v_w_ff2_down[0], "adamw_ff2_down")
    res["w_in"] = adamw(g_in, 0, w_in[0], m_w_in[0], v_w_in[0], "adamw_in", tr=256)
    res["w_out"] = adamw(g_out, 0, w_out[0], m_w_out[0], v_w_out[0], "adamw_out")
    dmod_cols = lax.dynamic_slice(dmod_all, (0, me * ada_cols), (nbg, ada_cols))
    dw_ada = ada_bwd(c_all, dmod_cols, "ada_bwd")
    res["w_ada"] = adamw(dw_ada[None, None], 0, w_ada[0], m_w_ada[0], v_w_ada[0], "adamw_ada", tr=256)
    res["b_ada"] = adamw(dmod_all[:, None, None], 0, b_ada, m_b_ada, v_b_ada, "adamw_b_ada")
    sizes = {n: weights[n].shape[1] for n in SMALL_ORDER}
    small_res = adamw(small_all[:, None], 0, _pack_small(weights), _pack_small(mom_m), _pack_small(mom_v), "adamw_small")
    small_res = [_unpack_small(r, sizes) for r in small_res]
    for n in SMALL_ORDER:
        res[n] = tuple(r[n] for r in small_res)

    outs = [loss, grad_x]
    for kind in range(4):
        for n in order:
            a = res[n][kind]
            outs.append(a.reshape(weights[n].shape))
    return tuple(outs)
```

```python
import functools

import jax
import jax.numpy as jnp
from jax import lax
from jax.experimental import pallas as pl
from jax.experimental.pallas import tpu as pltpu

F32 = jnp.float32
BF16 = jnp.bfloat16

D_MODEL = 1024
HEAD_DIM = 64
N_HEADS_A = 8
N_HEADS_B = 8
WIDTH_A = N_HEADS_A * HEAD_DIM
WIDTH_B = N_HEADS_B * HEAD_DIM
DILATED_PATTERNS = ((128, 1), (512, 4), (2048, 16))
ROT_DIM = HEAD_DIM // 4
ROPE_THETA = 500000.0
D_FF = 2752
D_FF_PAD = 2816
N_MOD = 9
EPS = 1e-6
ATTN_SCALE = HEAD_DIM ** -0.5
NEG = -1e30
N_DEV = 8
LANES = 128
HEADS_PER_STEP = LANES // HEAD_DIM

ADAM_LR = 0.001
ADAM_B1 = 0.9
ADAM_B2 = 0.999
ADAM_EPS = 1e-08
ADAM_WD = 0.01
ADAM_STEP = 10

VMEM_LIMIT = 56 * 1024 * 1024
MESH = pl.DeviceIdType.MESH

NT_DIMS = (((1,), (1,)), ((), ()))
TN_DIMS = (((0,), (0,)), ((), ()))
NN_DIMS = (((1,), (0,)), ((), ()))


def _place():
    return lax.axis_index("x"), lax.axis_index("y"), lax.axis_index("c")


def _slot(p):
    return 4 * p[0] + 2 * p[1] + p[2]


def _direct_copies(ins, outs, send_sems, recv_sems, local_sems, gather):
    x, y, c = _place()
    me = (x, y, c)
    flip = lambda v, bit: 1 - v if bit else v
    peers = [(flip(x, k & 4), flip(y, k & 2), flip(c, k & 1)) for k in range(1, N_DEV)]
    local, sends, recvs = [], [], []
    for a in range(len(ins)):
        mine = ins[a] if gather else ins[a].at[_slot(me)]
        local.append(pltpu.make_async_copy(mine, outs[a].at[_slot(me)], local_sems.at[a]))
        for k, peer in enumerate(peers):
            sems = dict(send_sem=send_sems.at[a * 7 + k], recv_sem=recv_sems.at[a * 7 + k], device_id=peer,
                        device_id_type=MESH)
            sends.append(pltpu.make_async_remote_copy(
                src_ref=ins[a] if gather else ins[a].at[_slot(peer)], dst_ref=outs[a].at[_slot(me)], **sems))
            recvs.append(pltpu.make_async_remote_copy(src_ref=mine, dst_ref=outs[a].at[_slot(peer)], **sems))
    return local, sends, recvs


def _comm_scratch(n):
    return [pltpu.SemaphoreType.DMA((7 * n,)), pltpu.SemaphoreType.DMA((7 * n,)), pltpu.SemaphoreType.DMA((n,))]


def _pcall(body, side=None, **kw):
    if side is None:
        return pl.pallas_call(body, **kw)
    arrs, gather = side
    n = len(arrs)
    grid = kw["grid"]
    in_specs = list(kw["in_specs"])
    single = not isinstance(kw["out_specs"], (list, tuple))
    out_specs = [kw["out_specs"]] if single else list(kw["out_specs"])
    out_shape = [kw["out_shape"]] if single else list(kw["out_shape"])
    scratch = list(kw.get("scratch_shapes", []))
    n_in, n_out, n_scr = len(in_specs), len(out_specs), len(scratch)
    hbm = pl.BlockSpec(memory_space=pl.ANY)

    def hosted(*refs):
        pos = [0]

        def take(k):
            pos[0] += k
            return refs[pos[0] - k:pos[0]]

        ins, s_ins, outs, s_outs, scr, sems = take(n_in), take(n), take(n_out), take(n), take(n_scr), take(3)
        ids = [pl.program_id(i) for i in range(len(grid))]
        first = functools.reduce(jnp.logical_and, [i == 0 for i in ids])
        last = functools.reduce(jnp.logical_and, [i == g - 1 for i, g in zip(ids, grid)])

        @pl.when(first)
        def _():
            local, sends, _ = _direct_copies(s_ins, s_outs, *sems, gather)
            for cp in local + sends:
                cp.start()

        body(*ins, *outs, *scr)

        @pl.when(last)
        def _():
            local, sends, recvs = _direct_copies(s_ins, s_outs, *sems, gather)
            for cp in recvs:
                cp.wait_recv()
            for cp in sends:
                cp.wait_send()
            for cp in local:
                cp.wait()

    kw.update(in_specs=in_specs + [hbm] * n, out_specs=out_specs + [hbm] * n,
              out_shape=out_shape + [jax.ShapeDtypeStruct(((N_DEV,) + a.shape) if gather else a.shape, a.dtype)
                                     for a in arrs],
              scratch_shapes=scratch + _comm_scratch(n))
    call = pl.pallas_call(hosted, **kw)

    def run(*args):
        res = call(*args, *arrs)
        main = res[0] if single else list(res[:n_out])
        return main, list(res[n_out:])

    return run


def _params(sem=None, **kw):
    if sem is not None:
        kw["dimension_semantics"] = sem
    return pltpu.CompilerParams(vmem_limit_bytes=VMEM_LIMIT, **kw)


def mm_rows(pairs, trans_b, out_dtype, name, tm=512, side=None):
    n = len(pairs)
    m = pairs[0][0].shape[0]
    n_out = pairs[0][1].shape[0 if trans_b else 1]
    dims = NT_DIMS if trans_b else NN_DIMS

    def body(*refs):
        o_ref = refs[2 * n]
        acc = None
        for a_ref, b_ref in zip(refs[:n], refs[n:2 * n]):
            d = lax.dot_general(a_ref[...], b_ref[...], dims, preferred_element_type=F32)
            acc = d if acc is None else acc + d
        o_ref[...] = acc.astype(o_ref.dtype)

    in_specs = [pl.BlockSpec((tm, a.shape[1]), lambda i: (i, 0)) for a, _ in pairs]
    in_specs += [pl.BlockSpec(b.shape, lambda i: (0, 0)) for _, b in pairs]
    return _pcall(
        body, side=side, name=name, grid=(m // tm,), in_specs=in_specs,
        out_specs=pl.BlockSpec((tm, n_out), lambda i: (i, 0)),
        out_shape=jax.ShapeDtypeStruct((m, n_out), out_dtype),
        compiler_params=_params(("arbitrary",)),
    )(*[a for a, _ in pairs], *[b for _, b in pairs])


def mm_tn(a, b, out_dtype, name, tk=512, side=None):
    t, ka = a.shape
    n_out = b.shape[1]
    steps = t // tk

    def body(a_ref, b_ref, o_ref, acc_ref):
        k = pl.program_id(0)
        d = lax.dot_general(a_ref[...], b_ref[...], TN_DIMS, preferred_element_type=F32)

        @pl.when(k == 0)
        def _():
            acc_ref[...] = d

        @pl.when(k > 0)
        def _():
            acc_ref[...] += d

        @pl.when(k == steps - 1)
        def _():
            o_ref[...] = acc_ref[...].astype(o_ref.dtype)

    return _pcall(
        body, side=side, name=name, grid=(steps,),
        in_specs=[pl.BlockSpec((tk, ka), lambda k: (k, 0)), pl.BlockSpec((tk, n_out), lambda k: (k, 0))],
        out_specs=pl.BlockSpec((ka, n_out), lambda k: (0, 0)),
        out_shape=jax.ShapeDtypeStruct((ka, n_out), out_dtype),
        scratch_shapes=[pltpu.VMEM((ka, n_out), F32)],
        compiler_params=_params(("arbitrary",)),
    )(a, b)


def _sigmoid(x):
    return 1.0 / (1.0 + jnp.exp(-x))


def ffn_up(h, wg, wu, name, tm=512, tn=1408, side=None):
    t, d = h.shape
    fp = wg.shape[1]

    def body(h_ref, wg_ref, wu_ref, g_ref, u_ref, a_ref):
        hv = h_ref[...]
        g = jnp.dot(hv, wg_ref[...], preferred_element_type=F32)
        u = jnp.dot(hv, wu_ref[...], preferred_element_type=F32)
        g_ref[...] = g.astype(BF16)
        u_ref[...] = u.astype(BF16)
        a_ref[...] = (g * _sigmoid(g) * u).astype(BF16)

    w_spec = pl.BlockSpec((d, tn), lambda j, i: (0, j))
    o_spec = pl.BlockSpec((tm, tn), lambda j, i: (i, j))
    o_shape = jax.ShapeDtypeStruct((t, fp), BF16)
    return _pcall(
        body, side=side, name=name, grid=(fp // tn, t // tm),
        in_specs=[pl.BlockSpec((tm, d), lambda j, i: (i, 0)), w_spec, w_spec],
        out_specs=[o_spec, o_spec, o_spec], out_shape=[o_shape, o_shape, o_shape],
        compiler_params=_params(("arbitrary", "arbitrary")),
    )(h, wg, wu)


def ffn_down_bwd(dy0, wd, gate, up, name, tm=512, tn=1408, side=None):
    t, d = dy0.shape
    fp = wd.shape[0]

    def body(dy_ref, wd_ref, g_ref, u_ref, dg_ref, du_ref):
        dact = lax.dot_general(dy_ref[...], wd_ref[...], NT_DIMS, preferred_element_type=F32)
        g = g_ref[...].astype(F32)
        u = u_ref[...].astype(F32)
        sg = _sigmoid(g)
        du_ref[...] = (dact * g * sg).astype(BF16)
        dg_ref[...] = (dact * u * (sg * (1.0 + g * (1.0 - sg)))).astype(BF16)

    t_spec = pl.BlockSpec((tm, tn), lambda j, i: (i, j))
    o_shape = jax.ShapeDtypeStruct((t, fp), BF16)
    return _pcall(
        body, side=side, name=name, grid=(fp // tn, t // tm),
        in_specs=[pl.BlockSpec((tm, d), lambda j, i: (i, 0)), pl.BlockSpec((tn, d), lambda j, i: (j, 0)), t_spec, t_spec],
        out_specs=[t_spec, t_spec], out_shape=[o_shape, o_shape],
        compiler_params=_params(("arbitrary", "arbitrary")),
    )(dy0, wd, gate, up)


def _row_specs(dx, ts, ns):
    return pl.BlockSpec((ts, dx), lambda b, s: (b * ns + s, 0))


def _mod_spec():
    return pl.BlockSpec((1, N_MOD, D_MODEL), lambda b, s: (b, 0, 0))


def _vec_spec(dx):
    return pl.BlockSpec((1, dx), lambda b, s: (0, 0))


def prenorm_fwd(x, g, mod, i_shift, i_scale, nb, name, ts=512):
    t, dx = x.shape
    ns = t // nb // ts

    def body(*refs):
        if mod is None:
            x_ref, g_ref, h_ref = refs
        else:
            x_ref, g_ref, mod_ref, h_ref = refs
        xv = x_ref[...]
        r = lax.rsqrt(jnp.mean(xv * xv, axis=-1, keepdims=True) + EPS)
        h = xv * r * g_ref[...]
        if mod is not None:
            h = h * (1.0 + mod_ref[0, i_scale:i_scale + 1, :]) + mod_ref[0, i_shift:i_shift + 1, :]
        h_ref[...] = h.astype(BF16)

    in_specs = [_row_specs(dx, ts, ns), _vec_spec(dx)]
    args = [x, g]
    if mod is not None:
        in_specs.append(_mod_spec())
        args.append(mod)
    return _pcall(
        body, name=name, grid=(nb, ns), in_specs=in_specs, out_specs=_row_specs(dx, ts, ns),
        out_shape=jax.ShapeDtypeStruct((t, dx), BF16), compiler_params=_params(("arbitrary", "arbitrary")),
    )(*args)


def prenorm_bwd(dh, x, g, mod, i_scale, dres, nb, name, ts=512):
    t, dx = x.shape
    ns = t // nb // ts
    has_mod = mod is not None
    has_res = dres is not None

    def body(*refs):
        refs = list(refs)
        dh_ref, x_ref, g_ref = refs[:3]
        pos = 3
        mod_ref = dres_ref = None
        if has_mod:
            mod_ref = refs[pos]
            pos += 1
        if has_res:
            dres_ref = refs[pos]
            pos += 1
        dx_ref, dg_ref = refs[pos], refs[pos + 1]
        b, s = pl.program_id(0), pl.program_id(1)
        xv = x_ref[...]
        dhv = dh_ref[...].astype(F32)
        gv = g_ref[...]
        r = lax.rsqrt(jnp.mean(xv * xv, axis=-1, keepdims=True) + EPS)
        xhat = xv * r
        dn = dhv
        if has_mod:
            dsc_ref, dsh_ref = refs[pos + 2], refs[pos + 3]
            dn = dhv * (1.0 + mod_ref[0, i_scale:i_scale + 1, :])
            dsc = jnp.sum(dhv * xhat * gv, axis=0, keepdims=True)[None]
            dsh = jnp.sum(dhv, axis=0, keepdims=True)[None]

            @pl.when(s == 0)
            def _():
                dsc_ref[...] = dsc
                dsh_ref[...] = dsh

            @pl.when(s > 0)
            def _():
                dsc_ref[...] += dsc
                dsh_ref[...] += dsh

        dg = jnp.sum(dn * xhat, axis=0, keepdims=True)
        first = jnp.logical_and(b == 0, s == 0)

        @pl.when(first)
        def _():
            dg_ref[...] = dg

        @pl.when(jnp.logical_not(first))
        def _():
            dg_ref[...] += dg

        dxhat = dn * gv
        dxv = r * (dxhat - xhat * jnp.mean(dxhat * xhat, axis=-1, keepdims=True))
        if has_res:
            dxv = dxv + dres_ref[...]
        dx_ref[...] = dxv

    row = _row_specs(dx, ts, ns)
    in_specs = [row, row, _vec_spec(dx)]
    args = [dh, x, g]
    if has_mod:
        in_specs.append(_mod_spec())
        args.append(mod)
    if has_res:
        in_specs.append(row)
        args.append(dres)
    out_specs = [row, _vec_spec(dx)]
    out_shape = [jax.ShapeDtypeStruct((t, dx), F32), jax.ShapeDtypeStruct((1, dx), F32)]
    if has_mod:
        bspec = pl.BlockSpec((1, 1, dx), lambda b, s: (b, 0, 0))
        out_specs += [bspec, bspec]
        out_shape += [jax.ShapeDtypeStruct((nb, 1, dx), F32)] * 2
    return _pcall(
        body, name=name, grid=(nb, ns), in_specs=in_specs, out_specs=out_specs, out_shape=out_shape,
        compiler_params=_params(("arbitrary", "arbitrary")),
    )(*args)


def postnorm_fwd(x, y0, g, mod, i_gate, coef, nb, name, target=None, ts=512):
    t, dx = x.shape
    ns = t // nb // ts
    with_loss = target is not None

    def body(*refs):
        x_ref, y_ref, g_ref, mod_ref = refs[:4]
        yv = y_ref[...]
        r = lax.rsqrt(jnp.mean(yv * yv, axis=-1, keepdims=True) + EPS)
        out = x_ref[...] + (coef * mod_ref[0, i_gate:i_gate + 1, :]) * (yv * r * g_ref[...])
        if not with_loss:
            refs[4][...] = out
            return
        t_ref, dx_ref, loss_ref = refs[4:]
        b, s = pl.program_id(0), pl.program_id(1)
        err = out - t_ref[...]
        dx_ref[...] = err * (1.0 / dx)
        part = (0.5 / dx) * jnp.sum(jnp.sum(err * err, axis=1, keepdims=True), axis=0, keepdims=True)
        first = jnp.logical_and(b == 0, s == 0)

        @pl.when(first)
        def _():
            loss_ref[...] = part

        @pl.when(jnp.logical_not(first))
        def _():
            loss_ref[...] += part

    row = _row_specs(dx, ts, ns)
    in_specs = [row, row, _vec_spec(dx), _mod_spec()]
    args = [x, y0, g, mod]
    out_specs = row
    out_shape = jax.ShapeDtypeStruct((t, dx), F32)
    if with_loss:
        in_specs.append(row)
        args.append(target)
        out_specs = [row, pl.BlockSpec((1, 1), lambda b, s: (0, 0))]
        out_shape = [out_shape, jax.ShapeDtypeStruct((1, 1), F32)]
    return _pcall(
        body, name=name, grid=(nb, ns), in_specs=in_specs, out_specs=out_specs, out_shape=out_shape,
        compiler_params=_params(("arbitrary", "arbitrary")),
    )(*args)


def postnorm_bwd(dxo, y0, g, mod, i_gate, coef, nb, name, ts=512):
    t, dx = y0.shape
    ns = t // nb // ts

    def body(d_ref, y_ref, g_ref, mod_ref, dy_ref, dg_ref, dgate_ref):
        b, s = pl.program_id(0), pl.program_id(1)
        yv = y_ref[...]
        dv = d_ref[...]
        gv = g_ref[...]
        r = lax.rsqrt(jnp.mean(yv * yv, axis=-1, keepdims=True) + EPS)
        yhat = yv * r
        dgate = jnp.sum(dv * (coef * (yhat * gv)), axis=0, keepdims=True)[None]
        dyn = dv * (coef * mod_ref[0, i_gate:i_gate + 1, :])
        dg = jnp.sum(dyn * yhat, axis=0, keepdims=True)
        dyhat = dyn * gv
        dy_ref[...] = (r * (dyhat - yhat * jnp.mean(dyhat * yhat, axis=-1, keepdims=True))).astype(BF16)

        @pl.when(s == 0)
        def _():
            dgate_ref[...] = dgate

        @pl.when(s > 0)
        def _():
            dgate_ref[...] += dgate

        first = jnp.logical_and(b == 0, s == 0)

        @pl.when(first)
        def _():
            dg_ref[...] = dg

        @pl.when(jnp.logical_not(first))
        def _():
            dg_ref[...] += dg

    row = _row_specs(dx, ts, ns)
    return _pcall(
        body, name=name, grid=(nb, ns), in_specs=[row, row, _vec_spec(dx), _mod_spec()],
        out_specs=[row, _vec_spec(dx), pl.BlockSpec((1, 1, dx), lambda b, s: (b, 0, 0))],
        out_shape=[jax.ShapeDtypeStruct((t, dx), BF16), jax.ShapeDtypeStruct((1, dx), F32),
                   jax.ShapeDtypeStruct((nb, 1, dx), F32)],
        compiler_params=_params(("arbitrary", "arbitrary")),
    )(dxo, y0, g, mod)


def rope_tables(positions):
    inv_freq = ROPE_THETA ** (-jnp.arange(0, ROT_DIM, 2, dtype=F32) / ROT_DIM)
    ang = positions.astype(F32).reshape(-1, 1) * inv_freq
    cos, sin = jnp.cos(ang), jnp.sin(ang)
    half = ROT_DIM // 2
    z = lambda n: jnp.zeros((ang.shape[0], n), F32)
    c = jnp.concatenate([cos, cos, jnp.ones((ang.shape[0], HEAD_DIM - ROT_DIM), F32)], axis=1)
    sp = jnp.concatenate([z(half), sin, z(HEAD_DIM - ROT_DIM)], axis=1)
    sm = jnp.concatenate([-sin, z(HEAD_DIM - half)], axis=1)
    return tuple(jnp.tile(a, (1, HEADS_PER_STEP)) for a in (c, sp, sm))


def rope(xarr, col_block, width, tables, transpose, out_dtype, name, ts=512):
    t = xarr.shape[0]
    half = ROT_DIM // 2
    reps = width // LANES

    def body(x_ref, c_ref, sp_ref, sm_ref, o_ref):
        xv = x_ref[...].astype(F32)
        wide = lambda r: jnp.concatenate([r[...]] * reps, axis=1)
        c, sp, sm = wide(c_ref), wide(sp_ref), wide(sm_ref)
        if transpose:
            out = xv * c + pltpu.roll(xv * sp, width - half, 1) + pltpu.roll(xv * sm, half, 1)
        else:
            out = xv * c + pltpu.roll(xv, half, 1) * sp + pltpu.roll(xv, width - half, 1) * sm
        o_ref[...] = out.astype(o_ref.dtype)

    tab = pl.BlockSpec((ts, LANES), lambda i: (i, 0))
    return _pcall(
        body, name=name, grid=(t // ts,),
        in_specs=[pl.BlockSpec((ts, width), lambda i: (i, col_block)), tab, tab, tab],
        out_specs=pl.BlockSpec((ts, width), lambda i: (i, 0)),
        out_shape=jax.ShapeDtypeStruct((t, width), out_dtype), compiler_params=_params(("arbitrary",)),
    )(xarr, *tables)


def _scan_lanes(x, reverse):
    n = x.shape[-1]
    lane = lax.broadcasted_iota(jnp.int32, x.shape, x.ndim - 1)
    k = 1
    while k < n:
        if reverse:
            x = x + jnp.where(lane < n - k, pltpu.roll(x, n - k, x.ndim - 1), 0.0)
        else:
            x = x + jnp.where(lane >= k, pltpu.roll(x, k, x.ndim - 1), 0.0)
        k *= 2
    return x


def _log_sigmoid(z):
    return jnp.minimum(z, 0.0) - jnp.log(1.0 + jnp.exp(-jnp.abs(z)))


def fox_gate_fwd(ft, b_forget, name):
    nb, nh, s = ft.shape

    def body(f_ref, b_ref, o_ref):
        z = f_ref[0] + b_ref[...]
        o_ref[0] = -_scan_lanes(_log_sigmoid(z), False)

    spec = pl.BlockSpec((1, nh, s), lambda b: (b, 0, 0))
    return _pcall(
        body, name=name, grid=(nb,), in_specs=[spec, pl.BlockSpec((nh, 1), lambda b: (0, 0))], out_specs=spec,
        out_shape=jax.ShapeDtypeStruct((nb, nh, s), F32), compiler_params=_params(("arbitrary",)),
    )(ft, b_forget)


def fox_gate_bwd(dcb, drow, ft, b_forget, name):
    nb, nh, s = ft.shape

    def body(d_ref, r_ref, f_ref, b_ref, dz_ref, db_ref):
        b = pl.program_id(0)
        z = f_ref[0] + b_ref[...]
        dlf = _scan_lanes(r_ref[0] - d_ref[0], True)
        dz = dlf * _sigmoid(-z)
        dz_ref[0] = dz
        db = jnp.sum(dz, axis=1, keepdims=True)

        @pl.when(b == 0)
        def _():
            db_ref[...] = db

        @pl.when(b > 0)
        def _():
            db_ref[...] += db

    spec = pl.BlockSpec((1, nh, s), lambda b: (b, 0, 0))
    vec = pl.BlockSpec((nh, 1), lambda b: (0, 0))
    return _pcall(
        body, name=name, grid=(nb,), in_specs=[spec, spec, spec, vec], out_specs=[spec, vec],
        out_shape=[jax.ShapeDtypeStruct((nb, nh, s), F32), jax.ShapeDtypeStruct((nh, 1), F32)],
        compiler_params=_params(("arbitrary",)),
    )(dcb, drow, ft, b_forget)


ATTN_TQ = 512
ATTN_TK = 256


def _block_delta(s, tq, tk):
    off = jnp.arange(s // tk) - (tq // tk - 1)
    return off[:, None, None] * tk + jnp.arange(tq)[None, None, :] - jnp.arange(tk)[None, :, None]


def dilated_table(s, tq, tk):
    delta = _block_delta(s, tq, tk)
    count = jnp.zeros(delta.shape, F32)
    for window, dil in DILATED_PATTERNS:
        count = count + ((delta >= 0) & (delta <= window) & (delta % dil == 0)).astype(F32)
    return jnp.where(count > 0, jnp.log(jnp.maximum(count, 1.0)), NEG)


def causal_table(s, tq, tk):
    return jnp.where(_block_delta(s, tq, tk) >= 0, 0.0, NEG).astype(F32)


def attn_fwd(q_arr, q_off, k_arr, k_off, v_arr, v_off, table, colbias, nb, name, side=None):
    t = q_arr.shape[0]
    s = t // nb
    tk, tq = table.shape[1:]
    nq, nk, r = s // tq, s // tk, tq // tk
    npairs = WIDTH_A // LANES
    use_cb = colbias is not None

    def body(*refs):
        refs = list(refs)
        q_ref, k_ref, v_ref, tab_ref = refs[:4]
        cb_ref = refs[4] if use_cb else None
        o_ref, lse_ref, vt_s, acc_s = refs[-4:]
        qi = pl.program_id(2)

        @pl.when(qi == 0)
        def _():
            for cblk in range(nk):
                vt_s[cblk] = v_ref[cblk * tk:(cblk + 1) * tk, :].astype(F32).T.astype(BF16)

        heads = [slice(h * HEAD_DIM, (h + 1) * HEAD_DIM) for h in range(HEADS_PER_STEP)]
        qs = [(q_ref[:, hs].astype(F32) * ATTN_SCALE).astype(BF16) for hs in heads]
        acc_s[...] = jnp.zeros_like(acc_s)

        def step(kb, carry):
            ks = pl.multiple_of(kb * tk, tk)
            tab = tab_ref[qi * r + (r - 1) - kb]
            out = []
            for h, hs in enumerate(heads):
                m, l = carry[h]
                st = lax.dot_general(k_ref[pl.ds(ks, tk), hs], qs[h], NT_DIMS, preferred_element_type=F32) + tab
                if use_cb:
                    st = st + cb_ref[0, h, pl.ds(ks, tk), :]
                m_new = jnp.maximum(m, jnp.max(st, axis=0, keepdims=True))
                alpha = jnp.exp(m - m_new)
                pt = jnp.exp(st - m_new)
                l = alpha * l + jnp.sum(pt, axis=0, keepdims=True)
                acc_s[hs, :] = alpha * acc_s[hs, :] + jnp.dot(vt_s[kb, hs, :], pt.astype(BF16),
                                                              preferred_element_type=F32)
                out.append((m_new, l))
            return tuple(out)

        init = tuple((jnp.full((1, tq), NEG, F32), jnp.zeros((1, tq), F32)) for _ in heads)
        fin = lax.fori_loop(0, (qi + 1) * r, step, init)
        for h, hs in enumerate(heads):
            m, l = fin[h]
            acc_s[hs, :] = acc_s[hs, :] / l
            lse_ref[0, h, 0] = m + jnp.log(l)
        o_ref[...] = acc_s[...].T

    def seq_spec(off):
        return pl.BlockSpec((s, LANES), lambda b, j, i: (b, off + j))

    in_specs = [pl.BlockSpec((tq, LANES), lambda b, j, i: (b * nq + i, q_off + j)), seq_spec(k_off), seq_spec(v_off),
                pl.BlockSpec(table.shape, lambda b, j, i: (0, 0, 0))]
    args = [q_arr, k_arr, v_arr, table]
    if use_cb:
        in_specs.append(pl.BlockSpec((1, HEADS_PER_STEP, s, 1), lambda b, j, i: (b, j, 0, 0)))
        args.append(colbias)
    n_heads = npairs * HEADS_PER_STEP
    return _pcall(
        body, side=side, name=name, grid=(nb, npairs, nq), in_specs=in_specs,
        out_specs=[pl.BlockSpec((tq, LANES), lambda b, j, i: (b * nq + i, j)),
                   pl.BlockSpec((1, HEADS_PER_STEP, 1, 1, tq), lambda b, j, i: (b, j, i, 0, 0))],
        out_shape=[jax.ShapeDtypeStruct((t, npairs * LANES), F32), jax.ShapeDtypeStruct((nb, n_heads, nq, 1, tq), F32)],
        scratch_shapes=[pltpu.VMEM((nk, LANES, tk), BF16), pltpu.VMEM((LANES, tq), F32)],
        compiler_params=_params(("arbitrary", "arbitrary", "arbitrary")),
    )(*args)


def attn_bwd(q_arr, q_off, k_arr, k_off, v_arr, v_off, o_arr, lse_arr, do_arr, table, colbias, nb, qk_dtype, name,
             side=None):
    t = q_arr.shape[0]
    s = t // nb
    tk, tq = table.shape[1:]
    nq, nk, r = s // tq, s // tk, tq // tk
    npairs = WIDTH_A // LANES
    use_cb = colbias is not None

    def body(*refs):
        refs = list(refs)
        q_ref, k_ref, v_ref, o_ref, lse_ref, do_ref, tab_ref = refs[:7]
        pos = 7
        cb_ref = None
        if use_cb:
            cb_ref = refs[pos]
            pos += 1
        dq_ref, dk_ref, dv_ref = refs[pos:pos + 3]
        pos += 3
        dcb_ref = drow_ref = None
        if use_cb:
            dcb_ref, drow_ref = refs[pos:pos + 2]
            pos += 2
        kt_s, dk_s, dv_s, dqt_s = refs[pos:pos + 4]
        dcb_s = refs[pos + 4] if use_cb else None

        heads = [slice(h * HEAD_DIM, (h + 1) * HEAD_DIM) for h in range(HEADS_PER_STEP)]
        for cblk in range(nk):
            kt_s[cblk] = k_ref[cblk * tk:(cblk + 1) * tk, :].astype(F32).T.astype(BF16)
        dk_s[...] = jnp.zeros_like(dk_s)
        dv_s[...] = jnp.zeros_like(dv_s)
        if use_cb:
            dcb_s[...] = jnp.zeros_like(dcb_s)
        ones = jnp.ones((8, HEAD_DIM), BF16)

        def q_loop(qi, carry):
            qs = pl.multiple_of(qi * tq, tq)
            q, do_b, lse, dsum = [], [], [], []
            for h, hs in enumerate(heads):
                q.append((q_ref[pl.ds(qs, tq), hs].astype(F32) * ATTN_SCALE).astype(BF16))
                do = do_ref[pl.ds(qs, tq), hs]
                do_b.append(do.astype(BF16))
                lse.append(lse_ref[0, h, qi])
                prod = do * o_ref[pl.ds(qs, tq), hs]
                hi = prod.astype(BF16)
                lo = (prod - hi.astype(F32)).astype(BF16)
                dsum.append((lax.dot_general(ones, hi, NT_DIMS, preferred_element_type=F32)
                             + lax.dot_general(ones, lo, NT_DIMS, preferred_element_type=F32))[0:1, :])
            dqt_s[...] = jnp.zeros_like(dqt_s)

            def k_loop(kb, drow):
                ks = pl.multiple_of(kb * tk, tk)
                tab = tab_ref[qi * r + (r - 1) - kb]
                out = []
                for h, hs in enumerate(heads):
                    k = k_ref[pl.ds(ks, tk), hs]
                    v = v_ref[pl.ds(ks, tk), hs]
                    st = lax.dot_general(k, q[h], NT_DIMS, preferred_element_type=F32) + tab
                    if use_cb:
                        st = st + cb_ref[0, h, pl.ds(ks, tk), :]
                    pt = jnp.exp(st - lse[h])
                    dpt = lax.dot_general(v, do_b[h], NT_DIMS, preferred_element_type=F32)
                    dst = pt * (dpt - dsum[h])
                    dst_b = dst.astype(BF16)
                    dv_s[h, pl.ds(ks, tk), :] += jnp.dot(pt.astype(BF16), do_b[h], preferred_element_type=F32)
                    dk_s[h, pl.ds(ks, tk), :] += jnp.dot(dst_b, q[h], preferred_element_type=F32)
                    dqt_s[hs, :] += jnp.dot(kt_s[kb, hs, :], dst_b, preferred_element_type=F32)
                    if use_cb:
                        dcb_s[h, pl.ds(ks, tk), :] += jnp.sum(dst, axis=1, keepdims=True)
                        out.append(drow[h] + jnp.sum(dst, axis=0, keepdims=True))
                    else:
                        out.append(drow[h])
                return tuple(out)

            drow = lax.fori_loop(0, (qi + 1) * r, k_loop, tuple(jnp.zeros((1, tq), F32) for _ in heads))
            dq_ref[pl.ds(qs, tq), :] = (dqt_s[...] * ATTN_SCALE).T.astype(dq_ref.dtype)
            if use_cb:
                for h in range(HEADS_PER_STEP):
                    drow_ref[0, h, qi] = drow[h]
            return carry

        lax.fori_loop(0, nq, q_loop, 0)
        for h, hs in enumerate(heads):
            dk_ref[:, hs] = dk_s[h].astype(dk_ref.dtype)
            dv_ref[:, hs] = dv_s[h].astype(dv_ref.dtype)
            if use_cb:
                dcb_ref[0, h] = dcb_s[h]

    def seq_spec(off):
        return pl.BlockSpec((s, LANES), lambda b, j: (b, off + j))

    row_spec = pl.BlockSpec((1, HEADS_PER_STEP, nq, 1, tq), lambda b, j: (b, j, 0, 0, 0))
    in_specs = [seq_spec(q_off), seq_spec(k_off), seq_spec(v_off), seq_spec(0), row_spec, seq_spec(0),
                pl.BlockSpec(table.shape, lambda b, j: (0, 0, 0))]
    args = [q_arr, k_arr, v_arr, o_arr, lse_arr, do_arr, table]
    width = npairs * LANES
    out_specs = [seq_spec(0)] * 3
    out_shape = [jax.ShapeDtypeStruct((t, width), qk_dtype), jax.ShapeDtypeStruct((t, width), qk_dtype),
                 jax.ShapeDtypeStruct((t, width), BF16)]
    scratch = [pltpu.VMEM((nk, LANES, tk), BF16), pltpu.VMEM((HEADS_PER_STEP, s, HEAD_DIM), F32),
               pltpu.VMEM((HEADS_PER_STEP, s, HEAD_DIM), F32), pltpu.VMEM((LANES, tq), F32)]
    if use_cb:
        cb_spec = pl.BlockSpec((1, HEADS_PER_STEP, s, 1), lambda b, j: (b, j, 0, 0))
        in_specs.append(cb_spec)
        args.append(colbias)
        out_specs += [cb_spec, row_spec]
        out_shape += [jax.ShapeDtypeStruct(colbias.shape, F32), jax.ShapeDtypeStruct(lse_arr.shape, F32)]
        scratch.append(pltpu.VMEM((HEADS_PER_STEP, s, 1), F32))
    return _pcall(
        body, side=side, name=name, grid=(nb, npairs), in_specs=in_specs, out_specs=out_specs, out_shape=out_shape,
        scratch_shapes=scratch, compiler_params=_params(("arbitrary", "arbitrary")),
    )(*args)


def ada_fwd(c_all, w_ada, b_cols, name):
    def body(c_ref, w_ref, b_ref, o_ref):
        cv = c_ref[...]
        sc = (cv * _sigmoid(cv)).astype(BF16)
        o_ref[...] = jnp.dot(sc, w_ref[...].astype(BF16), preferred_element_type=F32) + b_ref[...]

    return _pcall(body, name=name, out_shape=jax.ShapeDtypeStruct((c_all.shape[0], w_ada.shape[1]), F32),
                  compiler_params=_params())(c_all, w_ada, b_cols)


def ada_bwd(c_all, dmod_cols, name):
    def body(c_ref, d_ref, o_ref):
        cv = c_ref[...]
        sc = (cv * _sigmoid(cv)).astype(BF16)
        o_ref[...] = lax.dot_general(sc, d_ref[...].astype(BF16), TN_DIMS, preferred_element_type=F32)

    return _pcall(body, name=name, out_shape=jax.ShapeDtypeStruct((c_all.shape[1], dmod_cols.shape[1]), F32),
                  compiler_params=_params())(c_all, dmod_cols)


def adamw(parts, group, w, m, v, name, tr=None):
    n = parts.shape[0]
    r, c = w.shape
    tr = r if tr is None else tr
    c1 = 1.0 - ADAM_B1 ** ADAM_STEP
    c2 = 1.0 - ADAM_B2 ** ADAM_STEP

    def body(p_ref, w_ref, m_ref, v_ref, g_ref, d_ref, nm_ref, nv_ref):
        g = p_ref[0, 0].astype(F32)
        for i in range(1, n):
            g = g + p_ref[i, 0].astype(F32)
        wv = w_ref[...]
        nm = ADAM_B1 * m_ref[...] + (1.0 - ADAM_B1) * g
        nv = ADAM_B2 * v_ref[...] + (1.0 - ADAM_B2) * (g * g)
        g_ref[...] = g
        nm_ref[...] = nm
        nv_ref[...] = nv
        d_ref[...] = -ADAM_LR * ((nm / c1) / (jnp.sqrt(nv / c2) + ADAM_EPS) + ADAM_WD * wv)

    spec = pl.BlockSpec((tr, c), lambda i: (i, 0))
    shape = jax.ShapeDtypeStruct((r, c), F32)
    return _pcall(
        body, name=name, grid=(r // tr,),
        in_specs=[pl.BlockSpec((n, 1, tr, c), lambda i: (0, group, i, 0)), spec, spec, spec],
        out_specs=[spec] * 4, out_shape=[shape] * 4, compiler_params=_params(("arbitrary",)),
    )(parts, w, m, v)


def all_gather(arrs, name):
    n = len(arrs)
    hbm = pl.BlockSpec(memory_space=pl.ANY)

    def body(*refs):
        ins, outs = refs[:n], refs[n:2 * n]
        send_sems, recv_sems, local_sems = refs[2 * n:]
        x, y, c = _place()
        me, sibling = (x, y, c), (x, y, 1 - c)
        chips = [(1 - x, y), (x, 1 - y), (1 - x, 1 - y)]

        def copy(a, k, block, to, src=None):
            dst = outs[a].at[_slot(block)]
            return pltpu.make_async_remote_copy(
                src_ref=dst if src is None else src, dst_ref=dst, send_sem=send_sems.at[a * 7 + k],
                recv_sem=recv_sems.at[a * 7 + k], device_id=to, device_id_type=MESH)

        mine = [pltpu.make_async_copy(ins[a], outs[a].at[_slot(me)], local_sems.at[a]) for a in range(n)]
        for cp in mine:
            cp.start()
        first = []
        for a in range(n):
            first.append(copy(a, 0, me, sibling, src=ins[a]))
            first += [copy(a, 1 + j, me, (*chip, c), src=ins[a]) for j, chip in enumerate(chips)]
        for cp in first:
            cp.start()
        passed = []
        for a in range(n):
            for j, chip in enumerate(chips):
                copy(a, 1 + j, (*chip, c), me).wait_recv()
                cp = copy(a, 4 + j, (*chip, c), sibling)
                cp.start()
                passed.append(cp)
        for a in range(n):
            copy(a, 0, sibling, me).wait_recv()
            for j, chip in enumerate(chips):
                copy(a, 4 + j, (*chip, 1 - c), me).wait_recv()
        for cp in first + passed:
            cp.wait_send()
        for cp in mine:
            cp.wait()

    return _pcall(
        body, name=name, in_specs=[hbm] * n, out_specs=[hbm] * n,
        out_shape=[jax.ShapeDtypeStruct((N_DEV,) + a.shape, a.dtype) for a in arrs],
        scratch_shapes=[pltpu.SemaphoreType.DMA((7 * n,)), pltpu.SemaphoreType.DMA((7 * n,)),
                        pltpu.SemaphoreType.DMA((n,))],
        compiler_params=pltpu.CompilerParams(has_side_effects=True),
    )(*arrs)


def _cols_from_blocks(blocks, pad_to=None):
    r = blocks.shape[1]
    full = blocks.transpose(1, 0, 2).reshape(r, -1)
    if pad_to is not None and pad_to > full.shape[1]:
        full = jnp.pad(full, ((0, 0), (0, pad_to - full.shape[1])))
    return full


def _cols_to_blocks(full, ncols):
    r = full.shape[0]
    return full[:, :ncols].reshape(r, N_DEV, ncols // N_DEV).transpose(1, 0, 2)


def _rows_from_blocks(blocks, pad_to=None):
    full = blocks.reshape(-1, blocks.shape[2])
    if pad_to is not None and pad_to > full.shape[0]:
        full = jnp.pad(full, ((0, pad_to - full.shape[0]), (0, 0)))
    return full


def _rows_to_blocks(full, nrows):
    return full[:nrows].reshape(N_DEV, nrows // N_DEV, full.shape[1])


SMALL_ORDER = ("g_pre_ff1", "g_post_ff1", "g_pre_mix", "g_post_mix", "g_out_a", "g_out_b", "g_pre_ff2", "g_post_ff2",
               "b_forget")


def _pack_small(vals):
    rows = []
    for name in SMALL_ORDER:
        v = vals[name].reshape(1, -1)
        if v.shape[1] % LANES:
            v = jnp.pad(v, ((0, 0), (0, LANES - v.shape[1] % LANES)))
        rows.append(v)
    return jnp.concatenate(rows, axis=1)


def _unpack_small(row, sizes):
    out, pos = {}, 0
    for name in SMALL_ORDER:
        n = sizes[name]
        out[name] = row[:, pos:pos + n]
        pos += -(-n // LANES) * LANES
    return out


def _ffn_forward(x, mod, g_pre, g_post, wg, wu, wd, i0, nb, tag, target=None, side=None):
    h = prenorm_fwd(x, g_pre, mod, i0, i0 + 1, nb, f"{tag}_prenorm")
    res, side_out = ffn_up(h, wg, wu, f"{tag}_up", side=side), None
    if side is not None:
        res, side_out = res
    gate, up, act = res
    y0 = mm_rows([(act, wd)], False, F32, f"{tag}_down")
    out = postnorm_fwd(x, y0, g_post, mod, i0 + 2, 0.5, nb, f"{tag}_postnorm", target=target)
    return out, (x, h, gate, up, act, y0), side_out


def _ffn_backward(dxo, saved, mod, g_pre, g_post, wg, wu, wd, i0, nb, tag, side=None, chain=False):
    x, h, gate, up, act, y0 = saved
    dy0, dg_post, dgate_mod = postnorm_bwd(dxo, y0, g_post, mod, i0 + 2, 0.5, nb, f"{tag}_postnorm_bwd")
    dwd = mm_tn(act, dy0, BF16, f"{tag}_dwd")
    res, side_out = ffn_down_bwd(dy0, wd, gate, up, f"{tag}_down_bwd", side=side), None
    if side is not None:
        res, side_out = res
    dgate, dup = res
    dh_pairs = [(dgate, wg), (dup, wu)]
    if chain:
        dwg, (dwd,) = mm_tn(h, dgate, BF16, f"{tag}_dwg", side=([_rows_to_blocks(dwd, D_FF)[:, None]], False))
        dwu, (dwg,) = mm_tn(h, dup, BF16, f"{tag}_dwu", side=([_cols_to_blocks(dwg, D_FF)[:, None]], False))
        dh, (dwu,) = mm_rows(dh_pairs, True, F32, f"{tag}_dh", side=([_cols_to_blocks(dwu, D_FF)[:, None]], False))
    else:
        dwg = mm_tn(h, dgate, BF16, f"{tag}_dwg")
        dwu = mm_tn(h, dup, BF16, f"{tag}_dwu")
        dh = mm_rows(dh_pairs, True, F32, f"{tag}_dh")
    dx, dg_pre, dsc, dsh = prenorm_bwd(dh, x, g_pre, mod, i0 + 1, dxo, nb, f"{tag}_prenorm_bwd")
    return dx, dict(g_pre=dg_pre, g_post=dg_post, wg=dwg, wu=dwu, wd=dwd, mod=(dsh, dsc, dgate_mod)), side_out


def kernel(x, c, positions, w_ada, b_ada, g_pre_ff1, g_post_ff1, w_ff1_gate, w_ff1_up, w_ff1_down, g_pre_mix, g_post_mix, w_in, b_forget, g_out_a, g_out_b, w_out, g_pre_ff2, g_post_ff2, w_ff2_gate, w_ff2_up, w_ff2_down, loss_target, m_w_ada, m_b_ada, m_g_pre_ff1, m_g_post_ff1, m_w_ff1_gate, m_w_ff1_up, m_w_ff1_down, m_g_pre_mix, m_g_post_mix, m_w_in, m_b_forget, m_g_out_a, m_g_out_b, m_w_out, m_g_pre_ff2, m_g_post_ff2, m_w_ff2_gate, m_w_ff2_up, m_w_ff2_down, v_w_ada, v_b_ada, v_g_pre_ff1, v_g_post_ff1, v_w_ff1_gate, v_w_ff1_up, v_w_ff1_down, v_g_pre_mix, v_g_post_mix, v_w_in, v_b_forget, v_g_out_a, v_g_out_b, v_w_out, v_g_pre_ff2, v_g_post_ff2, v_w_ff2_gate, v_w_ff2_up, v_w_ff2_down):
    weights = dict(w_ada=w_ada, b_ada=b_ada, g_pre_ff1=g_pre_ff1, g_post_ff1=g_post_ff1, w_ff1_gate=w_ff1_gate,
                   w_ff1_up=w_ff1_up, w_ff1_down=w_ff1_down, g_pre_mix=g_pre_mix, g_post_mix=g_post_mix, w_in=w_in,
                   b_forget=b_forget, g_out_a=g_out_a, g_out_b=g_out_b, w_out=w_out, g_pre_ff2=g_pre_ff2,
                   g_post_ff2=g_post_ff2, w_ff2_gate=w_ff2_gate, w_ff2_up=w_ff2_up, w_ff2_down=w_ff2_down)
    mom_m = dict(w_ada=m_w_ada, b_ada=m_b_ada, g_pre_ff1=m_g_pre_ff1, g_post_ff1=m_g_post_ff1, w_ff1_gate=m_w_ff1_gate,
                 w_ff1_up=m_w_ff1_up, w_ff1_down=m_w_ff1_down, g_pre_mix=m_g_pre_mix, g_post_mix=m_g_post_mix,
                 w_in=m_w_in, b_forget=m_b_forget, g_out_a=m_g_out_a, g_out_b=m_g_out_b, w_out=m_w_out,
                 g_pre_ff2=m_g_pre_ff2, g_post_ff2=m_g_post_ff2, w_ff2_gate=m_w_ff2_gate, w_ff2_up=m_w_ff2_up,
                 w_ff2_down=m_w_ff2_down)
    mom_v = dict(w_ada=v_w_ada, b_ada=v_b_ada, g_pre_ff1=v_g_pre_ff1, g_post_ff1=v_g_post_ff1, w_ff1_gate=v_w_ff1_gate,
                 w_ff1_up=v_w_ff1_up, w_ff1_down=v_w_ff1_down, g_pre_mix=v_g_pre_mix, g_post_mix=v_g_post_mix,
                 w_in=v_w_in, b_forget=v_b_forget, g_out_a=v_g_out_a, g_out_b=v_g_out_b, w_out=v_w_out,
                 g_pre_ff2=v_g_pre_ff2, g_post_ff2=v_g_post_ff2, w_ff2_gate=v_w_ff2_gate, w_ff2_up=v_w_ff2_up,
                 w_ff2_down=v_w_ff2_down)
    order = list(weights)

    nb, s, d = x.shape
    t = nb * s
    me = _slot(_place())
    nbg = nb * N_DEV
    ada_cols = w_ada.shape[2]

    bf = lambda w: w[0].astype(BF16)
    c_all, ff1_cols_all, ff1_rows_all = all_gather(
        [c, jnp.stack([bf(w_ff1_gate), bf(w_ff1_up)]), bf(w_ff1_down)], "gather_ff1")
    c_all = c_all.reshape(nbg, d)
    wg1, wu1 = (_cols_from_blocks(ff1_cols_all[:, i], D_FF_PAD) for i in range(2))
    wd1 = _rows_from_blocks(ff1_rows_all, D_FF_PAD)

    b_cols = lax.dynamic_slice(b_ada, (0, me * ada_cols), (1, ada_cols))
    mod_cols = ada_fwd(c_all, w_ada[0], b_cols, "ada_fwd")
    (mod_all,) = all_gather([mod_cols], "gather_mod")
    mod = lax.dynamic_slice(mod_all, (0, me * nb, 0), (N_DEV, nb, ada_cols))
    mod = mod.transpose(1, 0, 2).reshape(nb, N_MOD, d)

    xf = x.reshape(t, d)
    target = loss_target.reshape(t, d)

    x1, saved1, (w_in_all, w_out_all) = _ffn_forward(xf, mod, g_pre_ff1, g_post_ff1, wg1, wu1, wd1, 0, nb, "ff1",
                                                     side=([bf(w_in), bf(w_out)], True))
    w_in_full = _cols_from_blocks(w_in_all)
    n_qkv = 3 * (WIDTH_A + WIDTH_B)
    w_qkv = w_in_full[:, :n_qkv]
    w_f = jnp.pad(w_in_full[:, n_qkv:], ((0, 0), (0, LANES - N_HEADS_B)))
    w_o = _rows_from_blocks(w_out_all)
    w_o_a, w_o_b = w_o[:WIDTH_A], w_o[WIDTH_A:]

    h2 = prenorm_fwd(x1, g_pre_mix, mod, 3, 4, nb, "mix_prenorm")
    proj = mm_rows([(h2, w_qkv)], False, BF16, "mix_proj")
    f_logit = mm_rows([(h2, w_f)], False, F32, "mix_forget")
    tables = rope_tables(positions)
    qk_rot = rope(proj, 0, 2 * WIDTH_A, tables, False, BF16, "rope")
    tab_a = dilated_table(s, ATTN_TQ, ATTN_TK)
    tab_b = causal_table(s, ATTN_TQ, ATTN_TK)
    ft = f_logit[:, :N_HEADS_B].reshape(nb, s, N_HEADS_B).transpose(0, 2, 1)
    bf_col = b_forget.reshape(N_HEADS_B, 1)
    colbias = fox_gate_fwd(ft, bf_col, "fox_gate").reshape(nb, N_HEADS_B, s, 1)
    pa = WIDTH_A // LANES
    (o_a, lse_a), (ff2_cols_all, ff2_rows_all) = attn_fwd(
        qk_rot, 0, qk_rot, pa, proj, 2 * pa, tab_a, None, nb, "attn_a",
        side=([jnp.stack([bf(w_ff2_gate), bf(w_ff2_up)]), bf(w_ff2_down)], True))
    wg2, wu2 = (_cols_from_blocks(ff2_cols_all[:, i], D_FF_PAD) for i in range(2))
    wd2 = _rows_from_blocks(ff2_rows_all, D_FF_PAD)
    o_b, lse_b = attn_fwd(proj, 3 * pa, proj, 4 * pa, proj, 5 * pa, tab_b, colbias, nb, "attn_b")
    m_a = prenorm_fwd(o_a, g_out_a, None, None, None, nb, "out_norm_a")
    m_b = prenorm_fwd(o_b, g_out_b, None, None, None, nb, "out_norm_b")
    y0m = mm_rows([(m_a, w_o_a), (m_b, w_o_b)], False, F32, "mix_out")
    x2 = postnorm_fwd(x1, y0m, g_post_mix, mod, 5, 1.0, nb, "mix_postnorm")

    (dx3, loss_part), saved2, _ = _ffn_forward(x2, mod, g_pre_ff2, g_post_ff2, wg2, wu2, wd2, 6, nb, "ff2", target=target)
    loss = lax.psum(loss_part[0, 0], ("x", "y", "c"))

    dx2, gr2, _ = _ffn_backward(dx3, saved2, mod, g_pre_ff2, g_post_ff2, wg2, wu2, wd2, 6, nb, "ff2")
    ff2_blocks = [jnp.stack([_cols_to_blocks(gr2["wg"], D_FF), _cols_to_blocks(gr2["wu"], D_FF)], axis=1),
                  _rows_to_blocks(gr2["wd"], D_FF)[:, None]]

    dy0m, dg_post_mix, dgate_mix = postnorm_bwd(dx2, y0m, g_post_mix, mod, 5, 1.0, nb, "mix_postnorm_bwd")
    dw_o_a = mm_tn(m_a, dy0m, BF16, "mix_dwo_a")
    dw_o_b = mm_tn(m_b, dy0m, BF16, "mix_dwo_b")
    dm_a = mm_rows([(dy0m, w_o_a)], True, F32, "mix_dm_a")
    dm_b = mm_rows([(dy0m, w_o_b)], True, F32, "mix_dm_b")
    do_a, dg_out_a = prenorm_bwd(dm_a, o_a, g_out_a, None, None, None, nb, "out_norm_a_bwd")
    do_b, dg_out_b = prenorm_bwd(dm_b, o_b, g_out_b, None, None, None, nb, "out_norm_b_bwd")
    (dq_a, dk_a, dv_a), (gcols2, grows2) = attn_bwd(qk_rot, 0, qk_rot, pa, proj, 2 * pa, o_a, lse_a, do_a, tab_a, None,
                                                    nb, F32, "attn_a_bwd", side=(ff2_blocks, False))
    dq_b, dk_b, dv_b, dcb, drow = attn_bwd(proj, 3 * pa, proj, 4 * pa, proj, 5 * pa, o_b, lse_b, do_b, tab_b, colbias, nb,
                                     BF16, "attn_b_bwd")
    dq_a = rope(dq_a, 0, WIDTH_A, tables, True, BF16, "rope_bwd_q")
    dk_a = rope(dk_a, 0, WIDTH_A, tables, True, BF16, "rope_bwd_k")
    dz_t, db_forget = fox_gate_bwd(dcb.reshape(nb, N_HEADS_B, s), drow.reshape(nb, N_HEADS_B, s), ft, bf_col,
                                   "fox_gate_bwd")
    dz = jnp.pad(dz_t.transpose(0, 2, 1).reshape(t, N_HEADS_B), ((0, 0), (0, LANES - N_HEADS_B))).astype(BF16)
    pieces = [dq_a, dk_a, dv_a, dq_b, dk_b, dv_b]
    w_pieces = [w_qkv[:, i * WIDTH_A:(i + 1) * WIDTH_A] for i in range(6)]
    dh2 = mm_rows(list(zip(pieces, w_pieces)) + [(dz, w_f)], True, F32, "mix_dh")
    dw_in = jnp.concatenate([mm_tn(h2, p, BF16, f"mix_dwin_{i}") for i, p in enumerate(pieces)]
                            + [mm_tn(h2, dz, BF16, "mix_dwin_f")[:, :N_HEADS_B]], axis=1)
    dx1, dg_pre_mix, dsc_mix, dsh_mix = prenorm_bwd(dh2, x1, g_pre_mix, mod, 4, dx2, nb, "mix_prenorm_bwd")

    g_in = _cols_to_blocks(dw_in, dw_in.shape[1])[:, None]
    g_out = _rows_to_blocks(jnp.concatenate([dw_o_a, dw_o_b], axis=0), d)[:, None]
    dx0, gr1, (g_in, g_out) = _ffn_backward(dx1, saved1, mod, g_pre_ff1, g_post_ff1, wg1, wu1, wd1, 0, nb, "ff1",
                                            side=([g_in, g_out], False), chain=True)
    grad_x = dx0.reshape(nb, s, d)

    dmod =jnp.concatenate(list(gr1["mod"]) + [dsh_mix, dsc_mix, dgate_mix] + list(gr2["mod"]), axis=1)
    small = _pack_small(dict(g_pre_ff1=gr1["g_pre"], g_post_ff1=gr1["g_post"], g_pre_mix=dg_pre_mix,
                             g_post_mix=dg_post_mix, g_out_a=dg_out_a, g_out_b=dg_out_b, g_pre_ff2=gr2["g_pre"],
                             g_post_ff2=gr2["g_post"], b_forget=db_forget))
    dmod_all, small_all = all_gather([dmod.reshape(nb, N_MOD * d), small], "gather_small_grads")
    dmod_all = dmod_all.reshape(nbg, N_MOD * d)

    res = {}
    res["w_ff1_gate"] = adamw(gr1["wg"], 0, w_ff1_gate[0], m_w_ff1_gate[0], v_w_ff1_gate[0], "adamw_ff1_gate", tr=256)
    res["w_ff1_up"] = adamw(gr1["wu"], 0, w_ff1_up[0], m_w_ff1_up[0], v_w_ff1_up[0], "adamw_ff1_up", tr=256)
    res["w_ff2_gate"] = adamw(gcols2, 0, w_ff2_gate[0], m_w_ff2_gate[0], v_w_ff2_gate[0], "adamw_ff2_gate", tr=256)
    res["w_ff2_up"] = adamw(gcols2, 1, w_ff2_up[0], m_w_ff2_up[0], v_w_ff2_up[0], "adamw_ff2_up", tr=256)
    res["w_ff1_down"] = adamw(gr1["wd"], 0, w_ff1_down[0], m_w_ff1_down[0], v_w_ff1_down[0], "adamw_ff1_down")
    res["w_ff2_down"] = adamw(grows2, 0, w_ff2_down[0], m_w_ff2_down[0], v_w_ff2_down[0], "adamw_ff2_down")
    res["w_in"] = adamw(g_in, 0, w_in[0], m_w_in[0], v_w_in[0], "adamw_in", tr=256)
    res["w_out"] = adamw(g_out, 0, w_out[0], m_w_out[0], v_w_out[0], "adamw_out")
    dmod_cols = lax.dynamic_slice(dmod_all, (0, me * ada_cols), (nbg, ada_cols))
    dw_ada = ada_bwd(c_all, dmod_cols, "ada_bwd")
    res["w_ada"] = adamw(dw_ada[None, None], 0, w_ada[0], m_w_ada[0], v_w_ada[0], "adamw_ada", tr=256)
    res["b_ada"] = adamw(dmod_all[:, None, None], 0, b_ada, m_b_ada, v_b_ada, "adamw_b_ada")
    sizes = {n: weights[n].shape[1] for n in SMALL_ORDER}
    small_res = adamw(small_all[:, None], 0, _pack_small(weights), _pack_small(mom_m), _pack_small(mom_v), "adamw_small")
    small_res = [_unpack_small(r, sizes) for r in small_res]
    for n in SMALL_ORDER:
        res[n] = tuple(r[n] for r in small_res)

    outs = [loss, grad_x]
    for kind in range(4):
        for n in order:
            a = res[n][kind]
            outs.append(a.reshape(weights[n].shape))
    return tuple(outs)
```

```python
import functools

import jax
import jax.numpy as jnp
from jax import lax
from jax.experimental import pallas as pl
from jax.experimental.pallas import tpu as pltpu

F32 = jnp.float32
BF16 = jnp.bfloat16

D_MODEL = 1024
HEAD_DIM = 64
N_HEADS_A = 8
N_HEADS_B = 8
WIDTH_A = N_HEADS_A * HEAD_DIM
WIDTH_B = N_HEADS_B * HEAD_DIM
DILATED_PATTERNS = ((128, 1), (512, 4), (2048, 16))
ROT_DIM = HEAD_DIM // 4
ROPE_THETA = 500000.0
D_FF = 2752
D_FF_PAD = 2816
N_MOD = 9
EPS = 1e-6
ATTN_SCALE = HEAD_DIM ** -0.5
NEG = -1e30
N_DEV = 8
LANES = 128
HEADS_PER_STEP = LANES // HEAD_DIM

ADAM_LR = 0.001
ADAM_B1 = 0.9
ADAM_B2 = 0.999
ADAM_EPS = 1e-08
ADAM_WD = 0.01
ADAM_STEP = 10

VMEM_LIMIT = 56 * 1024 * 1024
MESH = pl.DeviceIdType.MESH

NT_DIMS = (((1,), (1,)), ((), ()))
TN_DIMS = (((0,), (0,)), ((), ()))
NN_DIMS = (((1,), (0,)), ((), ()))


def _place():
    return lax.axis_index("x"), lax.axis_index("y"), lax.axis_index("c")


def _slot(p):
    return 4 * p[0] + 2 * p[1] + p[2]


def _direct_copies(ins, outs, send_sems, recv_sems, local_sems, gather):
    x, y, c = _place()
    me = (x, y, c)
    flip = lambda v, bit: 1 - v if bit else v
    peers = [(flip(x, k & 4), flip(y, k & 2), flip(c, k & 1)) for k in range(1, N_DEV)]
    local, sends, recvs = [], [], []
    for a in range(len(ins)):
        mine = ins[a] if gather else ins[a].at[_slot(me)]
        local.append(pltpu.make_async_copy(mine, outs[a].at[_slot(me)], local_sems.at[a]))
        for k, peer in enumerate(peers):
            sems = dict(send_sem=send_sems.at[a * 7 + k], recv_sem=recv_sems.at[a * 7 + k], device_id=peer,
                        device_id_type=MESH)
            sends.append(pltpu.make_async_remote_copy(
                src_ref=ins[a] if gather else ins[a].at[_slot(peer)], dst_ref=outs[a].at[_slot(me)], **sems))
            recvs.append(pltpu.make_async_remote_copy(src_ref=mine, dst_ref=outs[a].at[_slot(peer)], **sems))
    return local, sends, recvs


def _comm_scratch(n):
    return [pltpu.SemaphoreType.DMA((7 * n,)), pltpu.SemaphoreType.DMA((7 * n,)), pltpu.SemaphoreType.DMA((n,))]


def _pcall(body, side=None, **kw):
    if side is None:
        return pl.pallas_call(body, **kw)
    arrs, gather = side
    n = len(arrs)
    grid = kw["grid"]
    in_specs = list(kw["in_specs"])
    single = not isinstance(kw["out_specs"], (list, tuple))
    out_specs = [kw["out_specs"]] if single else list(kw["out_specs"])
    out_shape = [kw["out_shape"]] if single else list(kw["out_shape"])
    scratch = list(kw.get("scratch_shapes", []))
    n_in, n_out, n_scr = len(in_specs), len(out_specs), len(scratch)
    hbm = pl.BlockSpec(memory_space=pl.ANY)

    def hosted(*refs):
        pos = [0]

        def take(k):
            pos[0] += k
            return refs[pos[0] - k:pos[0]]

        ins, s_ins, outs, s_outs, scr, sems = take(n_in), take(n), take(n_out), take(n), take(n_scr), take(3)
        ids = [pl.program_id(i) for i in range(len(grid))]
        first = functools.reduce(jnp.logical_and, [i == 0 for i in ids])
        last = functools.reduce(jnp.logical_and, [i == g - 1 for i, g in zip(ids, grid)])

        @pl.when(first)
        def _():
            local, sends, _ = _direct_copies(s_ins, s_outs, *sems, gather)
            for cp in local + sends:
                cp.start()

        body(*ins, *outs, *scr)

        @pl.when(last)
        def _():
            local, sends, recvs = _direct_copies(s_ins, s_outs, *sems, gather)
            for cp in recvs:
                cp.wait_recv()
            for cp in sends:
                cp.wait_send()
            for cp in local:
                cp.wait()

    kw.update(in_specs=in_specs + [hbm] * n, out_specs=out_specs + [hbm] * n,
              out_shape=out_shape + [jax.ShapeDtypeStruct(((N_DEV,) + a.shape) if gather else a.shape, a.dtype)
                                     for a in arrs],
              scratch_shapes=scratch + _comm_scratch(n))
    call = pl.pallas_call(hosted, **kw)

    def run(*args):
        res = call(*args, *arrs)
        main = res[0] if single else list(res[:n_out])
        return main, list(res[n_out:])

    return run


def _params(sem=None, **kw):
    if sem is not None:
        kw["dimension_semantics"] = sem
    return pltpu.CompilerParams(vmem_limit_bytes=VMEM_LIMIT, **kw)


def mm_rows(pairs, trans_b, out_dtype, name, tm=512, side=None):
    n = len(pairs)
    m = pairs[0][0].shape[0]
    n_out = pairs[0][1].shape[0 if trans_b else 1]
    dims = NT_DIMS if trans_b else NN_DIMS

    def body(*refs):
        o_ref = refs[2 * n]
        acc = None
        for a_ref, b_ref in zip(refs[:n], refs[n:2 * n]):
            d = lax.dot_general(a_ref[...], b_ref[...], dims, preferred_element_type=F32)
            acc = d if acc is None else acc + d
        o_ref[...] = acc.astype(o_ref.dtype)

    in_specs = [pl.BlockSpec((tm, a.shape[1]), lambda i: (i, 0)) for a, _ in pairs]
    in_specs += [pl.BlockSpec(b.shape, lambda i: (0, 0)) for _, b in pairs]
    return _pcall(
        body, side=side, name=name, grid=(m // tm,), in_specs=in_specs,
        out_specs=pl.BlockSpec((tm, n_out), lambda i: (i, 0)),
        out_shape=jax.ShapeDtypeStruct((m, n_out), out_dtype),
        compiler_params=_params(("arbitrary",)),
    )(*[a for a, _ in pairs], *[b for _, b in pairs])


TN_TOKENS = 2048
TN_OUT_ELEMS = 2 * 1024 * 1024


def mm_tn(a, b, out_dtype, name, side=None):
    t, ka = a.shape
    n_out = b.shape[1]
    tk = min(TN_TOKENS, t)
    tka = ka // 2 if ka * n_out > TN_OUT_ELEMS else ka
    tn = n_out
    steps = t // tk

    def body(a_ref, b_ref, o_ref, acc_ref):
        k = pl.program_id(2)
        d = lax.dot_general(a_ref[...], b_ref[...], TN_DIMS, preferred_element_type=F32)

        @pl.when(k == 0)
        def _():
            acc_ref[...] = d

        @pl.when(k > 0)
        def _():
            acc_ref[...] += d

        @pl.when(k == steps - 1)
        def _():
            o_ref[...] = acc_ref[...].astype(o_ref.dtype)

    return _pcall(
        body, side=side, name=name, grid=(ka // tka, n_out // tn, steps),
        in_specs=[pl.BlockSpec((tk, tka), lambda i, j, k: (k, i)), pl.BlockSpec((tk, tn), lambda i, j, k: (k, j))],
        out_specs=pl.BlockSpec((tka, tn), lambda i, j, k: (i, j)),
        out_shape=jax.ShapeDtypeStruct((ka, n_out), out_dtype),
        scratch_shapes=[pltpu.VMEM((tka, tn), F32)],
        compiler_params=_params(("arbitrary", "arbitrary", "arbitrary")),
    )(a, b)


def _col_chunks(width, chunk=512):
    return [slice(c, min(c + chunk, width)) for c in range(0, width, chunk)]


def _sigmoid(x):
    return 1.0 / (1.0 + jnp.exp(-x))


def ffn_up(h, wg, wu, name, tm=512, tn=1408, side=None):
    t, d = h.shape
    fp = wg.shape[1]

    def body(h_ref, wg_ref, wu_ref, g_ref, u_ref, a_ref):
        hv = h_ref[...]

        def finish(cols, g, u):
            g_ref[:, cols] = g.astype(BF16)
            u_ref[:, cols] = u.astype(BF16)
            a_ref[:, cols] = (g * _sigmoid(g) * u).astype(BF16)

        pending = None
        for cols in _col_chunks(tn):
            g = jnp.dot(hv, wg_ref[:, cols], preferred_element_type=F32)
            u = jnp.dot(hv, wu_ref[:, cols], preferred_element_type=F32)
            if pending is not None:
                finish(*pending)
            pending = (cols, g, u)
        finish(*pending)

    w_spec = pl.BlockSpec((d, tn), lambda j, i: (0, j))
    o_spec = pl.BlockSpec((tm, tn), lambda j, i: (i, j))
    o_shape = jax.ShapeDtypeStruct((t, fp), BF16)
    return _pcall(
        body, side=side, name=name, grid=(fp // tn, t // tm),
        in_specs=[pl.BlockSpec((tm, d), lambda j, i: (i, 0)), w_spec, w_spec],
        out_specs=[o_spec, o_spec, o_spec], out_shape=[o_shape, o_shape, o_shape],
        compiler_params=_params(("arbitrary", "arbitrary")),
    )(h, wg, wu)


def ffn_down_bwd(dy0, wd, gate, up, name, tm=512, tn=1408, side=None):
    t, d = dy0.shape
    fp = wd.shape[0]

    def body(dy_ref, wd_ref, g_ref, u_ref, dg_ref, du_ref):
        dyv = dy_ref[...]

        def finish(cols, dact):
            g = g_ref[:, cols].astype(F32)
            u = u_ref[:, cols].astype(F32)
            sg = _sigmoid(g)
            du_ref[:, cols] = (dact * g * sg).astype(BF16)
            dg_ref[:, cols] = (dact * u * (sg * (1.0 + g * (1.0 - sg)))).astype(BF16)

        pending = None
        for cols in _col_chunks(tn):
            dact = lax.dot_general(dyv, wd_ref[cols, :], NT_DIMS, preferred_element_type=F32)
            if pending is not None:
                finish(*pending)
            pending = (cols, dact)
        finish(*pending)

    t_spec = pl.BlockSpec((tm, tn), lambda j, i: (i, j))
    o_shape = jax.ShapeDtypeStruct((t, fp), BF16)
    return _pcall(
        body, side=side, name=name, grid=(fp // tn, t // tm),
        in_specs=[pl.BlockSpec((tm, d), lambda j, i: (i, 0)), pl.BlockSpec((tn, d), lambda j, i: (j, 0)), t_spec, t_spec],
        out_specs=[t_spec, t_spec], out_shape=[o_shape, o_shape],
        compiler_params=_params(("arbitrary", "arbitrary")),
    )(dy0, wd, gate, up)


def _row_specs(dx, ts, ns):
    return pl.BlockSpec((ts, dx), lambda b, s: (b * ns + s, 0))


def _mod_spec():
    return pl.BlockSpec((1, N_MOD, D_MODEL), lambda b, s: (b, 0, 0))


def _vec_spec(dx):
    return pl.BlockSpec((1, dx), lambda b, s: (0, 0))


def prenorm_fwd(x, g, mod, i_shift, i_scale, nb, name, ts=512):
    t, dx = x.shape
    ns = t // nb // ts

    def body(*refs):
        if mod is None:
            x_ref, g_ref, h_ref = refs
        else:
            x_ref, g_ref, mod_ref, h_ref = refs
        xv = x_ref[...]
        r = lax.rsqrt(jnp.mean(xv * xv, axis=-1, keepdims=True) + EPS)
        h = xv * r * g_ref[...]
        if mod is not None:
            h = h * (1.0 + mod_ref[0, i_scale:i_scale + 1, :]) + mod_ref[0, i_shift:i_shift + 1, :]
        h_ref[...] = h.astype(BF16)

    in_specs = [_row_specs(dx, ts, ns), _vec_spec(dx)]
    args = [x, g]
    if mod is not None:
        in_specs.append(_mod_spec())
        args.append(mod)
    return _pcall(
        body, name=name, grid=(nb, ns), in_specs=in_specs, out_specs=_row_specs(dx, ts, ns),
        out_shape=jax.ShapeDtypeStruct((t, dx), BF16), compiler_params=_params(("arbitrary", "arbitrary")),
    )(*args)


def prenorm_bwd(dh, x, g, mod, i_scale, dres, nb, name, ts=512):
    t, dx = x.shape
    ns = t // nb // ts
    has_mod = mod is not None
    has_res = dres is not None

    def body(*refs):
        refs = list(refs)
        dh_ref, x_ref, g_ref = refs[:3]
        pos = 3
        mod_ref = dres_ref = None
        if has_mod:
            mod_ref = refs[pos]
            pos += 1
        if has_res:
            dres_ref = refs[pos]
            pos += 1
        dx_ref, dg_ref = refs[pos], refs[pos + 1]
        b, s = pl.program_id(0), pl.program_id(1)
        xv = x_ref[...]
        dhv = dh_ref[...].astype(F32)
        gv = g_ref[...]
        r = lax.rsqrt(jnp.mean(xv * xv, axis=-1, keepdims=True) + EPS)
        xhat = xv * r
        dn = dhv
        if has_mod:
            dsc_ref, dsh_ref = refs[pos + 2], refs[pos + 3]
            dn = dhv * (1.0 + mod_ref[0, i_scale:i_scale + 1, :])
            dsc = jnp.sum(dhv * xhat * gv, axis=0, keepdims=True)[None]
            dsh = jnp.sum(dhv, axis=0, keepdims=True)[None]

            @pl.when(s == 0)
            def _():
                dsc_ref[...] = dsc
                dsh_ref[...] = dsh

            @pl.when(s > 0)
            def _():
                dsc_ref[...] += dsc
                dsh_ref[...] += dsh

        dg = jnp.sum(dn * xhat, axis=0, keepdims=True)
        first = jnp.logical_and(b == 0, s == 0)

        @pl.when(first)
        def _():
            dg_ref[...] = dg

        @pl.when(jnp.logical_not(first))
        def _():
            dg_ref[...] += dg

        dxhat = dn * gv
        dxv = r * (dxhat - xhat * jnp.mean(dxhat * xhat, axis=-1, keepdims=True))
        if has_res:
            dxv = dxv + dres_ref[...]
        dx_ref[...] = dxv

    row = _row_specs(dx, ts, ns)
    in_specs = [row, row, _vec_spec(dx)]
    args = [dh, x, g]
    if has_mod:
        in_specs.append(_mod_spec())
        args.append(mod)
    if has_res:
        in_specs.append(row)
        args.append(dres)
    out_specs = [row, _vec_spec(dx)]
    out_shape = [jax.ShapeDtypeStruct((t, dx), F32), jax.ShapeDtypeStruct((1, dx), F32)]
    if has_mod:
        bspec = pl.BlockSpec((1, 1, dx), lambda b, s: (b, 0, 0))
        out_specs += [bspec, bspec]
        out_shape += [jax.ShapeDtypeStruct((nb, 1, dx), F32)] * 2
    return _pcall(
        body, name=name, grid=(nb, ns), in_specs=in_specs, out_specs=out_specs, out_shape=out_shape,
        compiler_params=_params(("arbitrary", "arbitrary")),
    )(*args)


def postnorm_fwd(x, y0, g, mod, i_gate, coef, nb, name, target=None, ts=512):
    t, dx = x.shape
    ns = t // nb // ts
    with_loss = target is not None

    def body(*refs):
        x_ref, y_ref, g_ref, mod_ref = refs[:4]
        yv = y_ref[...]
        r = lax.rsqrt(jnp.mean(yv * yv, axis=-1, keepdims=True) + EPS)
        out = x_ref[...] + (coef * mod_ref[0, i_gate:i_gate + 1, :]) * (yv * r * g_ref[...])
        if not with_loss:
            refs[4][...] = out
            return
        t_ref, dx_ref, loss_ref = refs[4:]
        b, s = pl.program_id(0), pl.program_id(1)
        err = out - t_ref[...]
        dx_ref[...] = err * (1.0 / dx)
        part = (0.5 / dx) * jnp.sum(jnp.sum(err * err, axis=1, keepdims=True), axis=0, keepdims=True)
        first = jnp.logical_and(b == 0, s == 0)

        @pl.when(first)
        def _():
            loss_ref[...] = part

        @pl.when(jnp.logical_not(first))
        def _():
            loss_ref[...] += part

    row = _row_specs(dx, ts, ns)
    in_specs = [row, row, _vec_spec(dx), _mod_spec()]
    args = [x, y0, g, mod]
    out_specs = row
    out_shape = jax.ShapeDtypeStruct((t, dx), F32)
    if with_loss:
        in_specs.append(row)
        args.append(target)
        out_specs = [row, pl.BlockSpec((1, 1), lambda b, s: (0, 0))]
        out_shape = [out_shape, jax.ShapeDtypeStruct((1, 1), F32)]
    return _pcall(
        body, name=name, grid=(nb, ns), in_specs=in_specs, out_specs=out_specs, out_shape=out_shape,
        compiler_params=_params(("arbitrary", "arbitrary")),
    )(*args)


def postnorm_bwd(dxo, y0, g, mod, i_gate, coef, nb, name, ts=512):
    t, dx = y0.shape
    ns = t // nb // ts

    def body(d_ref, y_ref, g_ref, mod_ref, dy_ref, dg_ref, dgate_ref):
        b, s = pl.program_id(0), pl.program_id(1)
        yv = y_ref[...]
        dv = d_ref[...]
        gv = g_ref[...]
        r = lax.rsqrt(jnp.mean(yv * yv, axis=-1, keepdims=True) + EPS)
        yhat = yv * r
        dgate = jnp.sum(dv * (coef * (yhat * gv)), axis=0, keepdims=True)[None]
        dyn = dv * (coef * mod_ref[0, i_gate:i_gate + 1, :])
        dg = jnp.sum(dyn * yhat, axis=0, keepdims=True)
        dyhat = dyn * gv
        dy_ref[...] = (r * (dyhat - yhat * jnp.mean(dyhat * yhat, axis=-1, keepdims=True))).astype(BF16)

        @pl.when(s == 0)
        def _():
            dgate_ref[...] = dgate

        @pl.when(s > 0)
        def _():
            dgate_ref[...] += dgate

        first = jnp.logical_and(b == 0, s == 0)

        @pl.when(first)
        def _():
            dg_ref[...] = dg

        @pl.when(jnp.logical_not(first))
        def _():
            dg_ref[...] += dg

    row = _row_specs(dx, ts, ns)
    return _pcall(
        body, name=name, grid=(nb, ns), in_specs=[row, row, _vec_spec(dx), _mod_spec()],
        out_specs=[row, _vec_spec(dx), pl.BlockSpec((1, 1, dx), lambda b, s: (b, 0, 0))],
        out_shape=[jax.ShapeDtypeStruct((t, dx), BF16), jax.ShapeDtypeStruct((1, dx), F32),
                   jax.ShapeDtypeStruct((nb, 1, dx), F32)],
        compiler_params=_params(("arbitrary", "arbitrary")),
    )(dxo, y0, g, mod)


def rope_tables(positions):
    inv_freq = ROPE_THETA ** (-jnp.arange(0, ROT_DIM, 2, dtype=F32) / ROT_DIM)
    ang = positions.astype(F32).reshape(-1, 1) * inv_freq
    cos, sin = jnp.cos(ang), jnp.sin(ang)
    half = ROT_DIM // 2
    z = lambda n: jnp.zeros((ang.shape[0], n), F32)
    c = jnp.concatenate([cos, cos, jnp.ones((ang.shape[0], HEAD_DIM - ROT_DIM), F32)], axis=1)
    sp = jnp.concatenate([z(half), sin, z(HEAD_DIM - ROT_DIM)], axis=1)
    sm = jnp.concatenate([-sin, z(HEAD_DIM - half)], axis=1)
    return tuple(jnp.tile(a, (1, HEADS_PER_STEP)) for a in (c, sp, sm))


def rope(xarr, col_block, width, tables, transpose, out_dtype, name, ts=512):
    t = xarr.shape[0]
    half = ROT_DIM // 2
    reps = width // LANES

    def body(x_ref, c_ref, sp_ref, sm_ref, o_ref):
        xv = x_ref[...].astype(F32)
        wide = lambda r: jnp.concatenate([r[...]] * reps, axis=1)
        c, sp, sm = wide(c_ref), wide(sp_ref), wide(sm_ref)
        if transpose:
            out = xv * c + pltpu.roll(xv * sp, width - half, 1) + pltpu.roll(xv * sm, half, 1)
        else:
            out = xv * c + pltpu.roll(xv, half, 1) * sp + pltpu.roll(xv, width - half, 1) * sm
        o_ref[...] = out.astype(o_ref.dtype)

    tab = pl.BlockSpec((ts, LANES), lambda i: (i, 0))
    return _pcall(
        body, name=name, grid=(t // ts,),
        in_specs=[pl.BlockSpec((ts, width), lambda i: (i, col_block)), tab, tab, tab],
        out_specs=pl.BlockSpec((ts, width), lambda i: (i, 0)),
        out_shape=jax.ShapeDtypeStruct((t, width), out_dtype), compiler_params=_params(("arbitrary",)),
    )(xarr, *tables)


def _scan_lanes(x, reverse):
    n = x.shape[-1]
    lane = lax.broadcasted_iota(jnp.int32, x.shape, x.ndim - 1)
    k = 1
    while k < n:
        if reverse:
            x = x + jnp.where(lane < n - k, pltpu.roll(x, n - k, x.ndim - 1), 0.0)
        else:
            x = x + jnp.where(lane >= k, pltpu.roll(x, k, x.ndim - 1), 0.0)
        k *= 2
    return x


def _log_sigmoid(z):
    return jnp.minimum(z, 0.0) - jnp.log(1.0 + jnp.exp(-jnp.abs(z)))


def fox_gate_fwd(ft, b_forget, name):
    nb, nh, s = ft.shape

    def body(f_ref, b_ref, o_ref):
        z = f_ref[0] + b_ref[...]
        o_ref[0] = -_scan_lanes(_log_sigmoid(z), False)

    spec = pl.BlockSpec((1, nh, s), lambda b: (b, 0, 0))
    return _pcall(
        body, name=name, grid=(nb,), in_specs=[spec, pl.BlockSpec((nh, 1), lambda b: (0, 0))], out_specs=spec,
        out_shape=jax.ShapeDtypeStruct((nb, nh, s), F32), compiler_params=_params(("arbitrary",)),
    )(ft, b_forget)


def fox_gate_bwd(dcb, drow, ft, b_forget, name):
    nb, nh, s = ft.shape

    def body(d_ref, r_ref, f_ref, b_ref, dz_ref, db_ref):
        b = pl.program_id(0)
        z = f_ref[0] + b_ref[...]
        dlf = _scan_lanes(r_ref[0] - d_ref[0], True)
        dz = dlf * _sigmoid(-z)
        dz_ref[0] = dz
        db = jnp.sum(dz, axis=1, keepdims=True)

        @pl.when(b == 0)
        def _():
            db_ref[...] = db

        @pl.when(b > 0)
        def _():
            db_ref[...] += db

    spec = pl.BlockSpec((1, nh, s), lambda b: (b, 0, 0))
    vec = pl.BlockSpec((nh, 1), lambda b: (0, 0))
    return _pcall(
        body, name=name, grid=(nb,), in_specs=[spec, spec, spec, vec], out_specs=[spec, vec],
        out_shape=[jax.ShapeDtypeStruct((nb, nh, s), F32), jax.ShapeDtypeStruct((nh, 1), F32)],
        compiler_params=_params(("arbitrary",)),
    )(dcb, drow, ft, b_forget)


ATTN_TQ = 512
ATTN_TK = 256
ONES_ROWS = 16


def _block_delta(s, tq, tk):
    off = jnp.arange(s // tk) - (tq // tk - 1)
    return off[:, None, None] * tk + jnp.arange(tq)[None, None, :] - jnp.arange(tk)[None, :, None]


def dilated_table(s, tq, tk):
    delta = _block_delta(s, tq, tk)
    count = jnp.zeros(delta.shape, F32)
    for window, dil in DILATED_PATTERNS:
        count = count + ((delta >= 0) & (delta <= window) & (delta % dil == 0)).astype(F32)
    return jnp.where(count > 0, jnp.log(jnp.maximum(count, 1.0)), NEG)


def causal_table(s, tq, tk):
    return jnp.where(_block_delta(s, tq, tk) >= 0, 0.0, NEG).astype(F32)


def attn_fwd(q_arr, q_off, k_arr, k_off, v_arr, v_off, table, colbias, nb, name, side=None):
    t = q_arr.shape[0]
    s = t // nb
    tk, tq = table.shape[1:]
    nq, nk, r = s // tq, s // tk, tq // tk
    npairs = WIDTH_A // LANES
    use_cb = colbias is not None

    def body(*refs):
        refs = list(refs)
        q_ref, k_ref, v_ref, tab_ref = refs[:4]
        cb_ref = refs[4] if use_cb else None
        o_ref, lse_ref, vt_s = refs[-3 - HEADS_PER_STEP:-HEADS_PER_STEP]
        acc_s = refs[-HEADS_PER_STEP:]
        qi = pl.program_id(2)

        heads = [slice(h * HEAD_DIM, (h + 1) * HEAD_DIM) for h in range(HEADS_PER_STEP)]

        @pl.when(qi == 0)
        def _():
            for cblk in range(nk):
                vt = v_ref[cblk * tk:(cblk + 1) * tk, :].astype(F32).T.astype(BF16)
                for h, hs in enumerate(heads):
                    vt_s[cblk, h, 0:HEAD_DIM, :] = vt[hs, :]
                    vt_s[cblk, h, HEAD_DIM:, :] = jnp.ones((ONES_ROWS, tk), BF16)

        qs = [(q_ref[:, hs].astype(F32) * ATTN_SCALE).astype(BF16) for hs in heads]
        for a in acc_s:
            a[...] = jnp.zeros_like(a)

        def step(kb, carry):
            ks = pl.multiple_of(kb * tk, tk)
            tab = tab_ref[qi * r + (r - 1) - kb]
            sts = []
            for h, hs in enumerate(heads):
                st = lax.dot_general(k_ref[pl.ds(ks, tk), hs], qs[h], NT_DIMS, preferred_element_type=F32) + tab
                if use_cb:
                    st = st + cb_ref[0, h, pl.ds(ks, tk), :]
                sts.append(st)
            m_new = [jnp.maximum(carry[h], jnp.max(sts[h], axis=0, keepdims=True)) for h in range(HEADS_PER_STEP)]
            for h in range(HEADS_PER_STEP):
                pt = jnp.exp(sts[h] - m_new[h]).astype(BF16)
                acc_s[h][...] = (jnp.exp(carry[h] - m_new[h]) * acc_s[h][...]
                                 + jnp.dot(vt_s[kb, h], pt, preferred_element_type=F32))
            return tuple(m_new)

        fin = lax.fori_loop(0, (qi + 1) * r, step, tuple(jnp.full((1, tq), NEG, F32) for _ in heads))
        outs = []
        for h in range(HEADS_PER_STEP):
            l = acc_s[h][HEAD_DIM:HEAD_DIM + 1, :]
            outs.append(acc_s[h][0:HEAD_DIM, :] / l)
            lse_ref[0, h, 0] = fin[h] + jnp.log(l)
        o_ref[...] = jnp.concatenate(outs, axis=0).T

    def seq_spec(off):
        return pl.BlockSpec((s, LANES), lambda b, j, i: (b, off + j))

    in_specs = [pl.BlockSpec((tq, LANES), lambda b, j, i: (b * nq + i, q_off + j)), seq_spec(k_off), seq_spec(v_off),
                pl.BlockSpec(table.shape, lambda b, j, i: (0, 0, 0))]
    args = [q_arr, k_arr, v_arr, table]
    if use_cb:
        in_specs.append(pl.BlockSpec((1, HEADS_PER_STEP, s, 1), lambda b, j, i: (b, j, 0, 0)))
        args.append(colbias)
    n_heads = npairs * HEADS_PER_STEP
    return _pcall(
        body, side=side, name=name, grid=(nb, npairs, nq), in_specs=in_specs,
        out_specs=[pl.BlockSpec((tq, LANES), lambda b, j, i: (b * nq + i, j)),
                   pl.BlockSpec((1, HEADS_PER_STEP, 1, 1, tq), lambda b, j, i: (b, j, i, 0, 0))],
        out_shape=[jax.ShapeDtypeStruct((t, npairs * LANES), F32), jax.ShapeDtypeStruct((nb, n_heads, nq, 1, tq), F32)],
        scratch_shapes=[pltpu.VMEM((nk, HEADS_PER_STEP, HEAD_DIM + ONES_ROWS, tk), BF16)]
        + [pltpu.VMEM((HEAD_DIM + ONES_ROWS, tq), F32)] * HEADS_PER_STEP,
        compiler_params=_params(("arbitrary", "arbitrary", "arbitrary")),
    )(*args)


def attn_bwd(q_arr, q_off, k_arr, k_off, v_arr, v_off, o_arr, lse_arr, do_arr, table, colbias, nb, qk_dtype, name,
             side=None):
    t = q_arr.shape[0]
    s = t // nb
    tk, tq = table.shape[1:]
    nq, nk, r = s // tq, s // tk, tq // tk
    npairs = WIDTH_A // LANES
    use_cb = colbias is not None

    def body(*refs):
        refs = list(refs)
        q_ref, k_ref, v_ref, o_ref, lse_ref, do_ref, tab_ref = refs[:7]
        pos = 7
        cb_ref = None
        if use_cb:
            cb_ref = refs[pos]
            pos += 1
        dq_ref, dk_ref, dv_ref = refs[pos:pos + 3]
        pos += 3
        dcb_ref = drow_ref = None
        if use_cb:
            dcb_ref, drow_ref = refs[pos:pos + 2]
            pos += 2
        kt_s, dkt_s, dvt_s = refs[pos:pos + 3]
        dqt_s = refs[pos + 3:pos + 3 + HEADS_PER_STEP]
        dcb_s = refs[pos + 3 + HEADS_PER_STEP] if use_cb else None

        heads = [slice(h * HEAD_DIM, (h + 1) * HEAD_DIM) for h in range(HEADS_PER_STEP)]
        for cblk in range(nk):
            kt_s[cblk] = k_ref[cblk * tk:(cblk + 1) * tk, :].astype(F32).T.astype(BF16)
        dkt_s[...] = jnp.zeros_like(dkt_s)
        dvt_s[...] = jnp.zeros_like(dvt_s)
        if use_cb:
            dcb_s[...] = jnp.zeros_like(dcb_s)
        ones = jnp.ones((8, HEAD_DIM), BF16)

        def q_loop(qi, carry):
            qs = pl.multiple_of(qi * tq, tq)
            q_all = (q_ref[pl.ds(qs, tq), :].astype(F32) * ATTN_SCALE)
            do_all = do_ref[pl.ds(qs, tq), :]
            qt_all = q_all.T.astype(BF16)
            dot_all = do_all.T.astype(BF16)
            q, do_b, qt, dot, lse, dsum = [], [], [], [], [], []
            for h, hs in enumerate(heads):
                q.append(q_all[:, hs].astype(BF16))
                do_b.append(do_all[:, hs].astype(BF16))
                qt.append(qt_all[hs, :])
                dot.append(dot_all[hs, :])
                lse.append(lse_ref[0, h, qi])
                prod = do_all[:, hs] * o_ref[pl.ds(qs, tq), hs]
                hi = prod.astype(BF16)
                lo = (prod - hi.astype(F32)).astype(BF16)
                dsum.append((lax.dot_general(ones, hi, NT_DIMS, preferred_element_type=F32)
                             + lax.dot_general(ones, lo, NT_DIMS, preferred_element_type=F32))[0:1, :])
            for a in dqt_s:
                a[...] = jnp.zeros_like(a)

            def k_loop(kb, drow):
                ks = pl.multiple_of(kb * tk, tk)
                tab = tab_ref[qi * r + (r - 1) - kb]
                sts, dpts, out = [], [], []
                for h, hs in enumerate(heads):
                    st = lax.dot_general(k_ref[pl.ds(ks, tk), hs], q[h], NT_DIMS, preferred_element_type=F32) + tab
                    if use_cb:
                        st = st + cb_ref[0, h, pl.ds(ks, tk), :]
                    sts.append(st)
                    dpts.append(lax.dot_general(v_ref[pl.ds(ks, tk), hs], do_b[h], NT_DIMS, preferred_element_type=F32))
                for h, hs in enumerate(heads):
                    pt = jnp.exp(sts[h] - lse[h])
                    dst = pt * (dpts[h] - dsum[h])
                    dst_b = dst.astype(BF16)
                    dvt_s[h, kb] += lax.dot_general(dot[h], pt.astype(BF16), NT_DIMS, preferred_element_type=F32)
                    dkt_s[h, kb] += lax.dot_general(qt[h], dst_b, NT_DIMS, preferred_element_type=F32)
                    dqt_s[h][...] += jnp.dot(kt_s[kb, hs, :], dst_b, preferred_element_type=F32)
                    if use_cb:
                        dcb_s[h, pl.ds(ks, tk), :] += jnp.sum(dst, axis=1, keepdims=True)
                        out.append(drow[h] + jnp.sum(dst, axis=0, keepdims=True))
                    else:
                        out.append(drow[h])
                return tuple(out)

            drow = lax.fori_loop(0, (qi + 1) * r, k_loop, tuple(jnp.zeros((1, tq), F32) for _ in heads))
            dqt = jnp.concatenate([a[...] for a in dqt_s], axis=0)
            dq_ref[pl.ds(qs, tq), :] = (dqt * ATTN_SCALE).T.astype(dq_ref.dtype)
            if use_cb:
                for h in range(HEADS_PER_STEP):
                    drow_ref[0, h, qi] = drow[h]
            return carry

        lax.fori_loop(0, nq, q_loop, 0)
        for cblk in range(nk):
            rows = slice(cblk * tk, (cblk + 1) * tk)
            dk_ref[rows, :] = jnp.concatenate([dkt_s[h, cblk] for h in range(HEADS_PER_STEP)], axis=0).T.astype(dk_ref.dtype)
            dv_ref[rows, :] = jnp.concatenate([dvt_s[h, cblk] for h in range(HEADS_PER_STEP)], axis=0).T.astype(dv_ref.dtype)
        if use_cb:
            for h in range(HEADS_PER_STEP):
                dcb_ref[0, h] = dcb_s[h]

    def seq_spec(off):
        return pl.BlockSpec((s, LANES), lambda b, j: (b, off + j))

    row_spec = pl.BlockSpec((1, HEADS_PER_STEP, nq, 1, tq), lambda b, j: (b, j, 0, 0, 0))
    in_specs = [seq_spec(q_off), seq_spec(k_off), seq_spec(v_off), seq_spec(0), row_spec, seq_spec(0),
                pl.BlockSpec(table.shape, lambda b, j: (0, 0, 0))]
    args = [q_arr, k_arr, v_arr, o_arr, lse_arr, do_arr, table]
    width = npairs * LANES
    out_specs = [seq_spec(0)] * 3
    out_shape = [jax.ShapeDtypeStruct((t, width), qk_dtype), jax.ShapeDtypeStruct((t, width), qk_dtype),
                 jax.ShapeDtypeStruct((t, width), BF16)]
    scratch = [pltpu.VMEM((nk, LANES, tk), BF16), pltpu.VMEM((HEADS_PER_STEP, nk, HEAD_DIM, tk), F32),
               pltpu.VMEM((HEADS_PER_STEP, nk, HEAD_DIM, tk), F32)] + [pltpu.VMEM((HEAD_DIM, tq), F32)] * HEADS_PER_STEP
    if use_cb:
        cb_spec = pl.BlockSpec((1, HEADS_PER_STEP, s, 1), lambda b, j: (b, j, 0, 0))
        in_specs.append(cb_spec)
        args.append(colbias)
        out_specs += [cb_spec, row_spec]
        out_shape += [jax.ShapeDtypeStruct(colbias.shape, F32), jax.ShapeDtypeStruct(lse_arr.shape, F32)]
        scratch.append(pltpu.VMEM((HEADS_PER_STEP, s, 1), F32))
    return _pcall(
        body, side=side, name=name, grid=(nb, npairs), in_specs=in_specs, out_specs=out_specs, out_shape=out_shape,
        scratch_shapes=scratch, compiler_params=_params(("arbitrary", "arbitrary")),
    )(*args)


def ada_fwd(c_all, w_ada, b_cols, name):
    def body(c_ref, w_ref, b_ref, o_ref):
        cv = c_ref[...]
        sc = (cv * _sigmoid(cv)).astype(BF16)
        o_ref[...] = jnp.dot(sc, w_ref[...].astype(BF16), preferred_element_type=F32) + b_ref[...]

    return _pcall(body, name=name, out_shape=jax.ShapeDtypeStruct((c_all.shape[0], w_ada.shape[1]), F32),
                  compiler_params=_params())(c_all, w_ada, b_cols)


def ada_bwd(c_all, dmod_cols, name):
    def body(c_ref, d_ref, o_ref):
        cv = c_ref[...]
        sc = (cv * _sigmoid(cv)).astype(BF16)
        o_ref[...] = lax.dot_general(sc, d_ref[...].astype(BF16), TN_DIMS, preferred_element_type=F32)

    return _pcall(body, name=name, out_shape=jax.ShapeDtypeStruct((c_all.shape[1], dmod_cols.shape[1]), F32),
                  compiler_params=_params())(c_all, dmod_cols)


def adamw(parts, group, w, m, v, name, tr=None):
    n = parts.shape[0]
    r, c = w.shape
    tr = r if tr is None else tr
    c1 = 1.0 - ADAM_B1 ** ADAM_STEP
    c2 = 1.0 - ADAM_B2 ** ADAM_STEP

    def body(p_ref, w_ref, m_ref, v_ref, g_ref, d_ref, nm_ref, nv_ref):
        g = p_ref[0, 0].astype(F32)
        for i in range(1, n):
            g = g + p_ref[i, 0].astype(F32)
        wv = w_ref[...]
        nm = ADAM_B1 * m_ref[...] + (1.0 - ADAM_B1) * g
        nv = ADAM_B2 * v_ref[...] + (1.0 - ADAM_B2) * (g * g)
        g_ref[...] = g
        nm_ref[...] = nm
        nv_ref[...] = nv
        d_ref[...] = -ADAM_LR * ((nm / c1) / (jnp.sqrt(nv / c2) + ADAM_EPS) + ADAM_WD * wv)

    spec = pl.BlockSpec((tr, c), lambda i: (i, 0))
    shape = jax.ShapeDtypeStruct((r, c), F32)
    return _pcall(
        body, name=name, grid=(r // tr,),
        in_specs=[pl.BlockSpec((n, 1, tr, c), lambda i: (0, group, i, 0)), spec, spec, spec],
        out_specs=[spec] * 4, out_shape=[shape] * 4, compiler_params=_params(("arbitrary",)),
    )(parts, w, m, v)


def all_gather(arrs, name):
    n = len(arrs)
    hbm = pl.BlockSpec(memory_space=pl.ANY)

    def body(*refs):
        ins, outs = refs[:n], refs[n:2 * n]
        send_sems, recv_sems, local_sems = refs[2 * n:]
        x, y, c = _place()
        me, sibling = (x, y, c), (x, y, 1 - c)
        chips = [(1 - x, y), (x, 1 - y), (1 - x, 1 - y)]

        def copy(a, k, block, to, src=None):
            dst = outs[a].at[_slot(block)]
            return pltpu.make_async_remote_copy(
                src_ref=dst if src is None else src, dst_ref=dst, send_sem=send_sems.at[a * 7 + k],
                recv_sem=recv_sems.at[a * 7 + k], device_id=to, device_id_type=MESH)

        mine = [pltpu.make_async_copy(ins[a], outs[a].at[_slot(me)], local_sems.at[a]) for a in range(n)]
        for cp in mine:
            cp.start()
        first = []
        for a in range(n):
            first.append(copy(a, 0, me, sibling, src=ins[a]))
            first += [copy(a, 1 + j, me, (*chip, c), src=ins[a]) for j, chip in enumerate(chips)]
        for cp in first:
            cp.start()
        passed = []
        for a in range(n):
            for j, chip in enumerate(chips):
                copy(a, 1 + j, (*chip, c), me).wait_recv()
                cp = copy(a, 4 + j, (*chip, c), sibling)
                cp.start()
                passed.append(cp)
        for a in range(n):
            copy(a, 0, sibling, me).wait_recv()
            for j, chip in enumerate(chips):
                copy(a, 4 + j, (*chip, 1 - c), me).wait_recv()
        for cp in first + passed:
            cp.wait_send()
        for cp in mine:
            cp.wait()

    return _pcall(
        body, name=name, in_specs=[hbm] * n, out_specs=[hbm] * n,
        out_shape=[jax.ShapeDtypeStruct((N_DEV,) + a.shape, a.dtype) for a in arrs],
        scratch_shapes=[pltpu.SemaphoreType.DMA((7 * n,)), pltpu.SemaphoreType.DMA((7 * n,)),
                        pltpu.SemaphoreType.DMA((n,))],
        compiler_params=pltpu.CompilerParams(has_side_effects=True),
    )(*arrs)


def _cols_from_blocks(blocks, pad_to=None):
    r = blocks.shape[1]
    full = blocks.transpose(1, 0, 2).reshape(r, -1)
    if pad_to is not None and pad_to > full.shape[1]:
        full = jnp.pad(full, ((0, 0), (0, pad_to - full.shape[1])))
    return full


def _cols_to_blocks(full, ncols):
    r = full.shape[0]
    return full[:, :ncols].reshape(r, N_DEV, ncols // N_DEV).transpose(1, 0, 2)


def _rows_from_blocks(blocks, pad_to=None):
    full = blocks.reshape(-1, blocks.shape[2])
    if pad_to is not None and pad_to > full.shape[0]:
        full = jnp.pad(full, ((0, pad_to - full.shape[0]), (0, 0)))
    return full


def _rows_to_blocks(full, nrows):
    return full[:nrows].reshape(N_DEV, nrows // N_DEV, full.shape[1])


SMALL_ORDER = ("g_pre_ff1", "g_post_ff1", "g_pre_mix", "g_post_mix", "g_out_a", "g_out_b", "g_pre_ff2", "g_post_ff2",
               "b_forget")


def _pack_small(vals):
    rows = []
    for name in SMALL_ORDER:
        v = vals[name].reshape(1, -1)
        if v.shape[1] % LANES:
            v = jnp.pad(v, ((0, 0), (0, LANES - v.shape[1] % LANES)))
        rows.append(v)
    return jnp.concatenate(rows, axis=1)


def _unpack_small(row, sizes):
    out, pos = {}, 0
    for name in SMALL_ORDER:
        n = sizes[name]
        out[name] = row[:, pos:pos + n]
        pos += -(-n // LANES) * LANES
    return out


def _ffn_forward(x, mod, g_pre, g_post, wg, wu, wd, i0, nb, tag, target=None, side=None):
    h = prenorm_fwd(x, g_pre, mod, i0, i0 + 1, nb, f"{tag}_prenorm")
    res, side_out = ffn_up(h, wg, wu, f"{tag}_up", side=side), None
    if side is not None:
        res, side_out = res
    gate, up, act = res
    y0 = mm_rows([(act, wd)], False, F32, f"{tag}_down")
    out = postnorm_fwd(x, y0, g_post, mod, i0 + 2, 0.5, nb, f"{tag}_postnorm", target=target)
    return out, (x, h, gate, up, act, y0), side_out


def _ffn_backward(dxo, saved, mod, g_pre, g_post, wg, wu, wd, i0, nb, tag, side=None, chain=False):
    x, h, gate, up, act, y0 = saved
    dy0, dg_post, dgate_mod = postnorm_bwd(dxo, y0, g_post, mod, i0 + 2, 0.5, nb, f"{tag}_postnorm_bwd")
    dwd = mm_tn(act, dy0, BF16, f"{tag}_dwd")
    res, side_out = ffn_down_bwd(dy0, wd, gate, up, f"{tag}_down_bwd", side=side), None
    if side is not None:
        res, side_out = res
    dgate, dup = res
    dh_pairs = [(dgate, wg), (dup, wu)]
    if chain:
        dwg, (dwd,) = mm_tn(h, dgate, BF16, f"{tag}_dwg", side=([_rows_to_blocks(dwd, D_FF)[:, None]], False))
        dwu, (dwg,) = mm_tn(h, dup, BF16, f"{tag}_dwu", side=([_cols_to_blocks(dwg, D_FF)[:, None]], False))
        dh, (dwu,) = mm_rows(dh_pairs, True, F32, f"{tag}_dh", side=([_cols_to_blocks(dwu, D_FF)[:, None]], False))
    else:
        dwg = mm_tn(h, dgate, BF16, f"{tag}_dwg")
        dwu = mm_tn(h, dup, BF16, f"{tag}_dwu")
        dh = mm_rows(dh_pairs, True, F32, f"{tag}_dh")
    dx, dg_pre, dsc, dsh = prenorm_bwd(dh, x, g_pre, mod, i0 + 1, dxo, nb, f"{tag}_prenorm_bwd")
    return dx, dict(g_pre=dg_pre, g_post=dg_post, wg=dwg, wu=dwu, wd=dwd, mod=(dsh, dsc, dgate_mod)), side_out


def kernel(x, c, positions, w_ada, b_ada, g_pre_ff1, g_post_ff1, w_ff1_gate, w_ff1_up, w_ff1_down, g_pre_mix, g_post_mix, w_in, b_forget, g_out_a, g_out_b, w_out, g_pre_ff2, g_post_ff2, w_ff2_gate, w_ff2_up, w_ff2_down, loss_target, m_w_ada, m_b_ada, m_g_pre_ff1, m_g_post_ff1, m_w_ff1_gate, m_w_ff1_up, m_w_ff1_down, m_g_pre_mix, m_g_post_mix, m_w_in, m_b_forget, m_g_out_a, m_g_out_b, m_w_out, m_g_pre_ff2, m_g_post_ff2, m_w_ff2_gate, m_w_ff2_up, m_w_ff2_down, v_w_ada, v_b_ada, v_g_pre_ff1, v_g_post_ff1, v_w_ff1_gate, v_w_ff1_up, v_w_ff1_down, v_g_pre_mix, v_g_post_mix, v_w_in, v_b_forget, v_g_out_a, v_g_out_b, v_w_out, v_g_pre_ff2, v_g_post_ff2, v_w_ff2_gate, v_w_ff2_up, v_w_ff2_down):
    weights = dict(w_ada=w_ada, b_ada=b_ada, g_pre_ff1=g_pre_ff1, g_post_ff1=g_post_ff1, w_ff1_gate=w_ff1_gate,
                   w_ff1_up=w_ff1_up, w_ff1_down=w_ff1_down, g_pre_mix=g_pre_mix, g_post_mix=g_post_mix, w_in=w_in,
                   b_forget=b_forget, g_out_a=g_out_a, g_out_b=g_out_b, w_out=w_out, g_pre_ff2=g_pre_ff2,
                   g_post_ff2=g_post_ff2, w_ff2_gate=w_ff2_gate, w_ff2_up=w_ff2_up, w_ff2_down=w_ff2_down)
    mom_m = dict(w_ada=m_w_ada, b_ada=m_b_ada, g_pre_ff1=m_g_pre_ff1, g_post_ff1=m_g_post_ff1, w_ff1_gate=m_w_ff1_gate,
                 w_ff1_up=m_w_ff1_up, w_ff1_down=m_w_ff1_down, g_pre_mix=m_g_pre_mix, g_post_mix=m_g_post_mix,
                 w_in=m_w_in, b_forget=m_b_forget, g_out_a=m_g_out_a, g_out_b=m_g_out_b, w_out=m_w_out,
                 g_pre_ff2=m_g_pre_ff2, g_post_ff2=m_g_post_ff2, w_ff2_gate=m_w_ff2_gate, w_ff2_up=m_w_ff2_up,
                 w_ff2_down=m_w_ff2_down)
    mom_v = dict(w_ada=v_w_ada, b_ada=v_b_ada, g_pre_ff1=v_g_pre_ff1, g_post_ff1=v_g_post_ff1, w_ff1_gate=v_w_ff1_gate,
                 w_ff1_up=v_w_ff1_up, w_ff1_down=v_w_ff1_down, g_pre_mix=v_g_pre_mix, g_post_mix=v_g_post_mix,
                 w_in=v_w_in, b_forget=v_b_forget, g_out_a=v_g_out_a, g_out_b=v_g_out_b, w_out=v_w_out,
                 g_pre_ff2=v_g_pre_ff2, g_post_ff2=v_g_post_ff2, w_ff2_gate=v_w_ff2_gate, w_ff2_up=v_w_ff2_up,
                 w_ff2_down=v_w_ff2_down)
    order = list(weights)

    nb, s, d = x.shape
    t = nb * s
    me = _slot(_place())
    nbg = nb * N_DEV
    ada_cols = w_ada.shape[2]

    bf = lambda w: w[0].astype(BF16)
    c_all, ff1_cols_all, ff1_rows_all = all_gather(
        [c, jnp.stack([bf(w_ff1_gate), bf(w_ff1_up)]), bf(w_ff1_down)], "gather_ff1")
    c_all = c_all.reshape(nbg, d)
    wg1, wu1 = (_cols_from_blocks(ff1_cols_all[:, i], D_FF_PAD) for i in range(2))
    wd1 = _rows_from_blocks(ff1_rows_all, D_FF_PAD)

    b_cols = lax.dynamic_slice(b_ada, (0, me * ada_cols), (1, ada_cols))
    mod_cols = ada_fwd(c_all, w_ada[0], b_cols, "ada_fwd")
    (mod_all,) = all_gather([mod_cols], "gather_mod")
    mod = lax.dynamic_slice(mod_all, (0, me * nb, 0), (N_DEV, nb, ada_cols))
    mod = mod.transpose(1, 0, 2).reshape(nb, N_MOD, d)

    xf = x.reshape(t, d)
    target = loss_target.reshape(t, d)

    x1, saved1, (w_in_all, w_out_all) = _ffn_forward(xf, mod, g_pre_ff1, g_post_ff1, wg1, wu1, wd1, 0, nb, "ff1",
                                                     side=([bf(w_in), bf(w_out)], True))
    w_in_full = _cols_from_blocks(w_in_all)
    n_qkv = 3 * (WIDTH_A + WIDTH_B)
    w_qkv = w_in_full[:, :n_qkv]
    w_f = jnp.pad(w_in_full[:, n_qkv:], ((0, 0), (0, LANES - N_HEADS_B)))
    w_o = _rows_from_blocks(w_out_all)
    w_o_a, w_o_b = w_o[:WIDTH_A], w_o[WIDTH_A:]

    h2 = prenorm_fwd(x1, g_pre_mix, mod, 3, 4, nb, "mix_prenorm")
    proj = mm_rows([(h2, w_qkv)], False, BF16, "mix_proj")
    f_logit = mm_rows([(h2, w_f)], False, F32, "mix_forget")
    tables = rope_tables(positions)
    qk_rot = rope(proj, 0, 2 * WIDTH_A, tables, False, BF16, "rope")
    tab_a = dilated_table(s, ATTN_TQ, ATTN_TK)
    tab_b = causal_table(s, ATTN_TQ, ATTN_TK)
    ft = f_logit[:, :N_HEADS_B].reshape(nb, s, N_HEADS_B).transpose(0, 2, 1)
    bf_col = b_forget.reshape(N_HEADS_B, 1)
    colbias = fox_gate_fwd(ft, bf_col, "fox_gate").reshape(nb, N_HEADS_B, s, 1)
    pa = WIDTH_A // LANES
    (o_a, lse_a), (ff2_cols_all, ff2_rows_all) = attn_fwd(
        qk_rot, 0, qk_rot, pa, proj, 2 * pa, tab_a, None, nb, "attn_a",
        side=([jnp.stack([bf(w_ff2_gate), bf(w_ff2_up)]), bf(w_ff2_down)], True))
    wg2, wu2 = (_cols_from_blocks(ff2_cols_all[:, i], D_FF_PAD) for i in range(2))
    wd2 = _rows_from_blocks(ff2_rows_all, D_FF_PAD)
    o_b, lse_b = attn_fwd(proj, 3 * pa, proj, 4 * pa, proj, 5 * pa, tab_b, colbias, nb, "attn_b")
    m_a = prenorm_fwd(o_a, g_out_a, None, None, None, nb, "out_norm_a")
    m_b = prenorm_fwd(o_b, g_out_b, None, None, None, nb, "out_norm_b")
    y0m = mm_rows([(m_a, w_o_a), (m_b, w_o_b)], False, F32, "mix_out")
    x2 = postnorm_fwd(x1, y0m, g_post_mix, mod, 5, 1.0, nb, "mix_postnorm")

    (dx3, loss_part), saved2, _ = _ffn_forward(x2, mod, g_pre_ff2, g_post_ff2, wg2, wu2, wd2, 6, nb, "ff2", target=target)
    loss = lax.psum(loss_part[0, 0], ("x", "y", "c"))

    dx2, gr2, _ = _ffn_backward(dx3, saved2, mod, g_pre_ff2, g_post_ff2, wg2, wu2, wd2, 6, nb, "ff2")
    ff2_blocks = [jnp.stack([_cols_to_blocks(gr2["wg"], D_FF), _cols_to_blocks(gr2["wu"], D_FF)], axis=1),
                  _rows_to_blocks(gr2["wd"], D_FF)[:, None]]

    dy0m, dg_post_mix, dgate_mix = postnorm_bwd(dx2, y0m, g_post_mix, mod, 5, 1.0, nb, "mix_postnorm_bwd")
    dw_o_a = mm_tn(m_a, dy0m, BF16, "mix_dwo_a")
    dw_o_b = mm_tn(m_b, dy0m, BF16, "mix_dwo_b")
    dm_a = mm_rows([(dy0m, w_o_a)], True, F32, "mix_dm_a")
    dm_b = mm_rows([(dy0m, w_o_b)], True, F32, "mix_dm_b")
    do_a, dg_out_a = prenorm_bwd(dm_a, o_a, g_out_a, None, None, None, nb, "out_norm_a_bwd")
    do_b, dg_out_b = prenorm_bwd(dm_b, o_b, g_out_b, None, None, None, nb, "out_norm_b_bwd")
    (dq_a, dk_a, dv_a), (gcols2, grows2) = attn_bwd(qk_rot, 0, qk_rot, pa, proj, 2 * pa, o_a, lse_a, do_a, tab_a, None,
                                                    nb, F32, "attn_a_bwd", side=(ff2_blocks, False))
    dq_b, dk_b, dv_b, dcb, drow = attn_bwd(proj, 3 * pa, proj, 4 * pa, proj, 5 * pa, o_b, lse_b, do_b, tab_b, colbias, nb,
                                     BF16, "attn_b_bwd")
    dq_a = rope(dq_a, 0, WIDTH_A, tables, True, BF16, "rope_bwd_q")
    dk_a = rope(dk_a, 0, WIDTH_A, tables, True, BF16, "rope_bwd_k")
    dz_t, db_forget = fox_gate_bwd(dcb.reshape(nb, N_HEADS_B, s), drow.reshape(nb, N_HEADS_B, s), ft, bf_col,
                                   "fox_gate_bwd")
    dz = jnp.pad(dz_t.transpose(0, 2, 1).reshape(t, N_HEADS_B), ((0, 0), (0, LANES - N_HEADS_B))).astype(BF16)
    pieces = [dq_a, dk_a, dv_a, dq_b, dk_b, dv_b]
    w_pieces = [w_qkv[:, i * WIDTH_A:(i + 1) * WIDTH_A] for i in range(6)]
    dh2 = mm_rows(list(zip(pieces, w_pieces)) + [(dz, w_f)], True, F32, "mix_dh")
    dw_in = jnp.concatenate([mm_tn(h2, p, BF16, f"mix_dwin_{i}") for i, p in enumerate(pieces)]
                            + [mm_tn(h2, dz, BF16, "mix_dwin_f")[:, :N_HEADS_B]], axis=1)
    dx1, dg_pre_mix, dsc_mix, dsh_mix = prenorm_bwd(dh2, x1, g_pre_mix, mod, 4, dx2, nb, "mix_prenorm_bwd")

    g_in = _cols_to_blocks(dw_in, dw_in.shape[1])[:, None]
    g_out = _rows_to_blocks(jnp.concatenate([dw_o_a, dw_o_b], axis=0), d)[:, None]
    dx0, gr1, (g_in, g_out) = _ffn_backward(dx1, saved1, mod, g_pre_ff1, g_post_ff1, wg1, wu1, wd1, 0, nb, "ff1",
                                            side=([g_in, g_out], False), chain=True)
    grad_x = dx0.reshape(nb, s, d)

    dmod =jnp.concatenate(list(gr1["mod"]) + [dsh_mix, dsc_mix, dgate_mix] + list(gr2["mod"]), axis=1)
    small = _pack_small(dict(g_pre_ff1=gr1["g_pre"], g_post_ff1=gr1["g_post"], g_pre_mix=dg_pre_mix,
                             g_post_mix=dg_post_mix, g_out_a=dg_out_a, g_out_b=dg_out_b, g_pre_ff2=gr2["g_pre"],
                             g_post_ff2=gr2["g_post"], b_forget=db_forget))
    dmod_all, small_all = all_gather([dmod.reshape(nb, N_MOD * d), small], "gather_small_grads")
    dmod_all = dmod_all.reshape(nbg, N_MOD * d)

    res = {}
    res["w_ff1_gate"] = adamw(gr1["wg"], 0, w_ff1_gate[0], m_w_ff1_gate[0], v_w_ff1_gate[0], "adamw_ff1_gate", tr=256)
    res["w_ff1_up"] = adamw(gr1["wu"], 0, w_ff1_up[0], m_w_ff1_up[0], v_w_ff1_up[0], "adamw_ff1_up", tr=256)
    res["w_ff2_gate"] = adamw(gcols2, 0, w_ff2_gate[0], m_w_ff2_gate[0], v_w_ff2_gate[0], "adamw_ff2_gate", tr=256)
    res["w_ff2_up"] = adamw(gcols2, 1, w_ff2_up[0], m_w_ff2_up[0], v_w_ff2_up[0], "adamw_ff2_up", tr=256)
    res["w_ff1_down"] = adamw(gr1["wd"], 0, w_ff1_down[0], m_w_ff1_down[0], v_w_ff1_down[0], "adamw_ff1_down")
    res["w_ff2_down"] = adamw(grows2, 0, w_ff2_down[0], m_w_ff2_down[0], v_w_ff2_down[0], "adamw_ff2_down")
    res["w_in"] = adamw(g_in, 0, w_in[0], m_w_in[0], v_w_in[0], "adamw_in", tr=256)
    res["w_out"] = adamw(g_out, 0, w_out[0], m_w_out[0], v_w_out[0], "adamw_out")
    dmod_cols = lax.dynamic_slice(dmod_all, (0, me * ada_cols), (nbg, ada_cols))
    dw_ada = ada_bwd(c_all, dmod_cols, "ada_bwd")
    res["w_ada"] = adamw(dw_ada[None, None], 0, w_ada[0], m_w_ada[0], v_w_ada[0], "adamw_ada", tr=256)
    res["b_ada"] = adamw(dmod_all[:, None, None], 0, b_ada, m_b_ada, v_b_ada, "adamw_b_ada")
    sizes = {n: weights[n].shape[1] for n in SMALL_ORDER}
    small_res = adamw(small_all[:, None], 0, _pack_small(weights), _pack_small(mom_m), _pack_small(mom_v), "adamw_small")
    small_res = [_unpack_small(r, sizes) for r in small_res]
    for n in SMALL_ORDER:
        res[n] = tuple(r[n] for r in small_res)

    outs = [loss, grad_x]
    for kind in range(4):
        for n in order:
            a = res[n][kind]
            outs.append(a.reshape(weights[n].shape))
    return tuple(outs)
```

```python
import functools

import jax
import jax.numpy as jnp
from jax import lax
from jax.experimental import pallas as pl
from jax.experimental.pallas import tpu as pltpu

F32 = jnp.float32
BF16 = jnp.bfloat16

D_MODEL = 1024
HEAD_DIM = 64
N_HEADS_A = 8
N_HEADS_B = 8
WIDTH_A = N_HEADS_A * HEAD_DIM
WIDTH_B = N_HEADS_B * HEAD_DIM
DILATED_PATTERNS = ((128, 1), (512, 4), (2048, 16))
ROT_DIM = HEAD_DIM // 4
ROPE_THETA = 500000.0
D_FF = 2752
D_FF_PAD = 2816
N_MOD = 9
EPS = 1e-6
ATTN_SCALE = HEAD_DIM ** -0.5
NEG = -1e30
N_DEV = 8
LANES = 128
HEADS_PER_STEP = LANES // HEAD_DIM

ADAM_LR = 0.001
ADAM_B1 = 0.9
ADAM_B2 = 0.999
ADAM_EPS = 1e-08
ADAM_WD = 0.01
ADAM_STEP = 10

VMEM_LIMIT = 56 * 1024 * 1024
MESH = pl.DeviceIdType.MESH

NT_DIMS = (((1,), (1,)), ((), ()))
TN_DIMS = (((0,), (0,)), ((), ()))
NN_DIMS = (((1,), (0,)), ((), ()))


def _place():
    return lax.axis_index("x"), lax.axis_index("y"), lax.axis_index("c")


def _slot(p):
    return 4 * p[0] + 2 * p[1] + p[2]


def _direct_copies(ins, outs, send_sems, recv_sems, local_sems, gather):
    x, y, c = _place()
    me = (x, y, c)
    flip = lambda v, bit: 1 - v if bit else v
    peers = [(flip(x, k & 4), flip(y, k & 2), flip(c, k & 1)) for k in range(1, N_DEV)]
    local, sends, recvs = [], [], []
    for a in range(len(ins)):
        mine = ins[a] if gather else ins[a].at[_slot(me)]
        local.append(pltpu.make_async_copy(mine, outs[a].at[_slot(me)], local_sems.at[a]))
        for k, peer in enumerate(peers):
            sems = dict(send_sem=send_sems.at[a * 7 + k], recv_sem=recv_sems.at[a * 7 + k], device_id=peer,
                        device_id_type=MESH)
            sends.append(pltpu.make_async_remote_copy(
                src_ref=ins[a] if gather else ins[a].at[_slot(peer)], dst_ref=outs[a].at[_slot(me)], **sems))
            recvs.append(pltpu.make_async_remote_copy(src_ref=mine, dst_ref=outs[a].at[_slot(peer)], **sems))
    return local, sends, recvs


def _comm_scratch(n):
    return [pltpu.SemaphoreType.DMA((7 * n,)), pltpu.SemaphoreType.DMA((7 * n,)), pltpu.SemaphoreType.DMA((n,))]


def _pcall(body, side=None, **kw):
    if side is None:
        return pl.pallas_call(body, **kw)
    arrs, gather = side
    n = len(arrs)
    grid = kw["grid"]
    in_specs = list(kw["in_specs"])
    single = not isinstance(kw["out_specs"], (list, tuple))
    out_specs = [kw["out_specs"]] if single else list(kw["out_specs"])
    out_shape = [kw["out_shape"]] if single else list(kw["out_shape"])
    scratch = list(kw.get("scratch_shapes", []))
    n_in, n_out, n_scr = len(in_specs), len(out_specs), len(scratch)
    hbm = pl.BlockSpec(memory_space=pl.ANY)

    def hosted(*refs):
        pos = [0]

        def take(k):
            pos[0] += k
            return refs[pos[0] - k:pos[0]]

        ins, s_ins, outs, s_outs, scr, sems = take(n_in), take(n), take(n_out), take(n), take(n_scr), take(3)
        ids = [pl.program_id(i) for i in range(len(grid))]
        first = functools.reduce(jnp.logical_and, [i == 0 for i in ids])
        last = functools.reduce(jnp.logical_and, [i == g - 1 for i, g in zip(ids, grid)])

        @pl.when(first)
        def _():
            local, sends, _ = _direct_copies(s_ins, s_outs, *sems, gather)
            for cp in local + sends:
                cp.start()

        body(*ins, *outs, *scr)

        @pl.when(last)
        def _():
            local, sends, recvs = _direct_copies(s_ins, s_outs, *sems, gather)
            for cp in recvs:
                cp.wait_recv()
            for cp in sends:
                cp.wait_send()
            for cp in local:
                cp.wait()

    kw.update(in_specs=in_specs + [hbm] * n, out_specs=out_specs + [hbm] * n,
              out_shape=out_shape + [jax.ShapeDtypeStruct(((N_DEV,) + a.shape) if gather else a.shape, a.dtype)
                                     for a in arrs],
              scratch_shapes=scratch + _comm_scratch(n))
    call = pl.pallas_call(hosted, **kw)

    def run(*args):
        res = call(*args, *arrs)
        main = res[0] if single else list(res[:n_out])
        return main, list(res[n_out:])

    return run


def _params(sem=None, **kw):
    if sem is not None:
        kw["dimension_semantics"] = sem
    return pltpu.CompilerParams(vmem_limit_bytes=VMEM_LIMIT, **kw)


def mm_rows(pairs, trans_b, out_dtype, name, tm=512, side=None):
    n = len(pairs)
    m = pairs[0][0].shape[0]
    n_out = pairs[0][1].shape[0 if trans_b else 1]
    dims = NT_DIMS if trans_b else NN_DIMS

    def body(*refs):
        o_ref = refs[2 * n]
        acc = None
        for a_ref, b_ref in zip(refs[:n], refs[n:2 * n]):
            d = lax.dot_general(a_ref[...], b_ref[...], dims, preferred_element_type=F32)
            acc = d if acc is None else acc + d
        o_ref[...] = acc.astype(o_ref.dtype)

    in_specs = [pl.BlockSpec((tm, a.shape[1]), lambda i: (i, 0)) for a, _ in pairs]
    in_specs += [pl.BlockSpec(b.shape, lambda i: (0, 0)) for _, b in pairs]
    return _pcall(
        body, side=side, name=name, grid=(m // tm,), in_specs=in_specs,
        out_specs=pl.BlockSpec((tm, n_out), lambda i: (i, 0)),
        out_shape=jax.ShapeDtypeStruct((m, n_out), out_dtype),
        compiler_params=_params(("arbitrary",)),
    )(*[a for a, _ in pairs], *[b for _, b in pairs])


TN_TOKENS = 2048
TN_OUT_ELEMS = 2 * 1024 * 1024


def mm_tn(a, b, out_dtype, name, side=None):
    t, ka = a.shape
    n_out = b.shape[1]
    tk = min(TN_TOKENS, t)
    tka = ka // 2 if ka * n_out > TN_OUT_ELEMS else ka
    tn = n_out
    steps = t // tk

    def body(a_ref, b_ref, o_ref, acc_ref):
        k = pl.program_id(2)
        d = lax.dot_general(a_ref[...], b_ref[...], TN_DIMS, preferred_element_type=F32)

        @pl.when(k == 0)
        def _():
            acc_ref[...] = d

        @pl.when(k > 0)
        def _():
            acc_ref[...] += d

        @pl.when(k == steps - 1)
        def _():
            o_ref[...] = acc_ref[...].astype(o_ref.dtype)

    return _pcall(
        body, side=side, name=name, grid=(ka // tka, n_out // tn, steps),
        in_specs=[pl.BlockSpec((tk, tka), lambda i, j, k: (k, i)), pl.BlockSpec((tk, tn), lambda i, j, k: (k, j))],
        out_specs=pl.BlockSpec((tka, tn), lambda i, j, k: (i, j)),
        out_shape=jax.ShapeDtypeStruct((ka, n_out), out_dtype),
        scratch_shapes=[pltpu.VMEM((tka, tn), F32)],
        compiler_params=_params(("arbitrary", "arbitrary", "arbitrary")),
    )(a, b)


def _col_chunks(width, chunk=512):
    return [slice(c, min(c + chunk, width)) for c in range(0, width, chunk)]


def _sigmoid(x):
    return 1.0 / (1.0 + jnp.exp(-x))


def ffn_up(h, wgt, wut, name, tm=512, tn=1408, side=None):
    t, d = h.shape
    fp = wgt.shape[0]

    def body(h_ref, wg_ref, wu_ref, g_ref, u_ref, a_ref):
        hv = h_ref[...]

        def finish(cols, g, u):
            g_ref[:, cols] = g.astype(BF16)
            u_ref[:, cols] = u.astype(BF16)
            a_ref[:, cols] = (g * _sigmoid(g) * u).astype(BF16)

        pending = None
        for cols in _col_chunks(tn):
            g = lax.dot_general(hv, wg_ref[cols, :], NT_DIMS, preferred_element_type=F32)
            u = lax.dot_general(hv, wu_ref[cols, :], NT_DIMS, preferred_element_type=F32)
            if pending is not None:
                finish(*pending)
            pending = (cols, g, u)
        finish(*pending)

    w_spec = pl.BlockSpec((tn, d), lambda j, i: (j, 0))
    o_spec = pl.BlockSpec((tm, tn), lambda j, i: (i, j))
    o_shape = jax.ShapeDtypeStruct((t, fp), BF16)
    return _pcall(
        body, side=side, name=name, grid=(fp // tn, t // tm),
        in_specs=[pl.BlockSpec((tm, d), lambda j, i: (i, 0)), w_spec, w_spec],
        out_specs=[o_spec, o_spec, o_spec], out_shape=[o_shape, o_shape, o_shape],
        compiler_params=_params(("arbitrary", "arbitrary")),
    )(h, wgt, wut)


def ffn_down_bwd(dy0, wd, gate, up, name, tm=512, tn=1408, side=None):
    t, d = dy0.shape
    fp = wd.shape[0]

    def body(dy_ref, wd_ref, g_ref, u_ref, dg_ref, du_ref):
        dyv = dy_ref[...]

        def finish(cols, dact):
            g = g_ref[:, cols].astype(F32)
            u = u_ref[:, cols].astype(F32)
            sg = _sigmoid(g)
            du_ref[:, cols] = (dact * g * sg).astype(BF16)
            dg_ref[:, cols] = (dact * u * (sg * (1.0 + g * (1.0 - sg)))).astype(BF16)

        pending = None
        for cols in _col_chunks(tn):
            dact = lax.dot_general(dyv, wd_ref[cols, :], NT_DIMS, preferred_element_type=F32)
            if pending is not None:
                finish(*pending)
            pending = (cols, dact)
        finish(*pending)

    t_spec = pl.BlockSpec((tm, tn), lambda j, i: (i, j))
    o_shape = jax.ShapeDtypeStruct((t, fp), BF16)
    return _pcall(
        body, side=side, name=name, grid=(fp // tn, t // tm),
        in_specs=[pl.BlockSpec((tm, d), lambda j, i: (i, 0)), pl.BlockSpec((tn, d), lambda j, i: (j, 0)), t_spec, t_spec],
        out_specs=[t_spec, t_spec], out_shape=[o_shape, o_shape],
        compiler_params=_params(("arbitrary", "arbitrary")),
    )(dy0, wd, gate, up)


def _row_specs(dx, ts, ns):
    return pl.BlockSpec((ts, dx), lambda b, s: (b * ns + s, 0))


def _mod_spec():
    return pl.BlockSpec((1, N_MOD, D_MODEL), lambda b, s: (b, 0, 0))


def _vec_spec(dx):
    return pl.BlockSpec((1, dx), lambda b, s: (0, 0))


def prenorm_fwd(x, g, mod, i_shift, i_scale, nb, name, ts=512):
    t, dx = x.shape
    ns = t // nb // ts

    def body(*refs):
        if mod is None:
            x_ref, g_ref, h_ref = refs
        else:
            x_ref, g_ref, mod_ref, h_ref = refs
        xv = x_ref[...]
        r = lax.rsqrt(jnp.mean(xv * xv, axis=-1, keepdims=True) + EPS)
        h = xv * r * g_ref[...]
        if mod is not None:
            h = h * (1.0 + mod_ref[0, i_scale:i_scale + 1, :]) + mod_ref[0, i_shift:i_shift + 1, :]
        h_ref[...] = h.astype(BF16)

    in_specs = [_row_specs(dx, ts, ns), _vec_spec(dx)]
    args = [x, g]
    if mod is not None:
        in_specs.append(_mod_spec())
        args.append(mod)
    return _pcall(
        body, name=name, grid=(nb, ns), in_specs=in_specs, out_specs=_row_specs(dx, ts, ns),
        out_shape=jax.ShapeDtypeStruct((t, dx), BF16), compiler_params=_params(("arbitrary", "arbitrary")),
    )(*args)


def prenorm_bwd(dh, x, g, mod, i_scale, dres, nb, name, ts=512):
    t, dx = x.shape
    ns = t // nb // ts
    has_mod = mod is not None
    has_res = dres is not None

    def body(*refs):
        refs = list(refs)
        dh_ref, x_ref, g_ref = refs[:3]
        pos = 3
        mod_ref = dres_ref = None
        if has_mod:
            mod_ref = refs[pos]
            pos += 1
        if has_res:
            dres_ref = refs[pos]
            pos += 1
        dx_ref, dg_ref = refs[pos], refs[pos + 1]
        b, s = pl.program_id(0), pl.program_id(1)
        xv = x_ref[...]
        dhv = dh_ref[...].astype(F32)
        gv = g_ref[...]
        r = lax.rsqrt(jnp.mean(xv * xv, axis=-1, keepdims=True) + EPS)
        xhat = xv * r
        dn = dhv
        if has_mod:
            dsc_ref, dsh_ref = refs[pos + 2], refs[pos + 3]
            dn = dhv * (1.0 + mod_ref[0, i_scale:i_scale + 1, :])
            dsc = jnp.sum(dhv * xhat * gv, axis=0, keepdims=True)[None]
            dsh = jnp.sum(dhv, axis=0, keepdims=True)[None]

            @pl.when(s == 0)
            def _():
                dsc_ref[...] = dsc
                dsh_ref[...] = dsh

            @pl.when(s > 0)
            def _():
                dsc_ref[...] += dsc
                dsh_ref[...] += dsh

        dg = jnp.sum(dn * xhat, axis=0, keepdims=True)
        first = jnp.logical_and(b == 0, s == 0)

        @pl.when(first)
        def _():
            dg_ref[...] = dg

        @pl.when(jnp.logical_not(first))
        def _():
            dg_ref[...] += dg

        dxhat = dn * gv
        dxv = r * (dxhat - xhat * jnp.mean(dxhat * xhat, axis=-1, keepdims=True))
        if has_res:
            dxv = dxv + dres_ref[...]
        dx_ref[...] = dxv

    row = _row_specs(dx, ts, ns)
    in_specs = [row, row, _vec_spec(dx)]
    args = [dh, x, g]
    if has_mod:
        in_specs.append(_mod_spec())
        args.append(mod)
    if has_res:
        in_specs.append(row)
        args.append(dres)
    out_specs = [row, _vec_spec(dx)]
    out_shape = [jax.ShapeDtypeStruct((t, dx), F32), jax.ShapeDtypeStruct((1, dx), F32)]
    if has_mod:
        bspec = pl.BlockSpec((1, 1, dx), lambda b, s: (b, 0, 0))
        out_specs += [bspec, bspec]
        out_shape += [jax.ShapeDtypeStruct((nb, 1, dx), F32)] * 2
    return _pcall(
        body, name=name, grid=(nb, ns), in_specs=in_specs, out_specs=out_specs, out_shape=out_shape,
        compiler_params=_params(("arbitrary", "arbitrary")),
    )(*args)


def postnorm_fwd(x, y0, g, mod, i_gate, coef, nb, name, target=None, ts=512):
    t, dx = x.shape
    ns = t // nb // ts
    with_loss = target is not None

    def body(*refs):
        x_ref, y_ref, g_ref, mod_ref = refs[:4]
        yv = y_ref[...]
        r = lax.rsqrt(jnp.mean(yv * yv, axis=-1, keepdims=True) + EPS)
        out = x_ref[...] + (coef * mod_ref[0, i_gate:i_gate + 1, :]) * (yv * r * g_ref[...])
        if not with_loss:
            refs[4][...] = out
            return
        t_ref, dx_ref, loss_ref = refs[4:]
        b, s = pl.program_id(0), pl.program_id(1)
        err = out - t_ref[...]
        dx_ref[...] = err * (1.0 / dx)
        part = (0.5 / dx) * jnp.sum(jnp.sum(err * err, axis=1, keepdims=True), axis=0, keepdims=True)
        first = jnp.logical_and(b == 0, s == 0)

        @pl.when(first)
        def _():
            loss_ref[...] = part

        @pl.when(jnp.logical_not(first))
        def _():
            loss_ref[...] += part

    row = _row_specs(dx, ts, ns)
    in_specs = [row, row, _vec_spec(dx), _mod_spec()]
    args = [x, y0, g, mod]
    out_specs = row
    out_shape = jax.ShapeDtypeStruct((t, dx), F32)
    if with_loss:
        in_specs.append(row)
        args.append(target)
        out_specs = [row, pl.BlockSpec((1, 1), lambda b, s: (0, 0))]
        out_shape = [out_shape, jax.ShapeDtypeStruct((1, 1), F32)]
    return _pcall(
        body, name=name, grid=(nb, ns), in_specs=in_specs, out_specs=out_specs, out_shape=out_shape,
        compiler_params=_params(("arbitrary", "arbitrary")),
    )(*args)


def postnorm_bwd(dxo, y0, g, mod, i_gate, coef, nb, name, ts=512):
    t, dx = y0.shape
    ns = t // nb // ts

    def body(d_ref, y_ref, g_ref, mod_ref, dy_ref, dg_ref, dgate_ref):
        b, s = pl.program_id(0), pl.program_id(1)
        yv = y_ref[...]
        dv = d_ref[...]
        gv = g_ref[...]
        r = lax.rsqrt(jnp.mean(yv * yv, axis=-1, keepdims=True) + EPS)
        yhat = yv * r
        dgate = jnp.sum(dv * (coef * (yhat * gv)), axis=0, keepdims=True)[None]
        dyn = dv * (coef * mod_ref[0, i_gate:i_gate + 1, :])
        dg = jnp.sum(dyn * yhat, axis=0, keepdims=True)
        dyhat = dyn * gv
        dy_ref[...] = (r * (dyhat - yhat * jnp.mean(dyhat * yhat, axis=-1, keepdims=True))).astype(BF16)

        @pl.when(s == 0)
        def _():
            dgate_ref[...] = dgate

        @pl.when(s > 0)
        def _():
            dgate_ref[...] += dgate

        first = jnp.logical_and(b == 0, s == 0)

        @pl.when(first)
        def _():
            dg_ref[...] = dg

        @pl.when(jnp.logical_not(first))
        def _():
            dg_ref[...] += dg

    row = _row_specs(dx, ts, ns)
    return _pcall(
        body, name=name, grid=(nb, ns), in_specs=[row, row, _vec_spec(dx), _mod_spec()],
        out_specs=[row, _vec_spec(dx), pl.BlockSpec((1, 1, dx), lambda b, s: (b, 0, 0))],
        out_shape=[jax.ShapeDtypeStruct((t, dx), BF16), jax.ShapeDtypeStruct((1, dx), F32),
                   jax.ShapeDtypeStruct((nb, 1, dx), F32)],
        compiler_params=_params(("arbitrary", "arbitrary")),
    )(dxo, y0, g, mod)


def rope_tables(positions):
    inv_freq = ROPE_THETA ** (-jnp.arange(0, ROT_DIM, 2, dtype=F32) / ROT_DIM)
    ang = positions.astype(F32).reshape(-1, 1) * inv_freq
    cos, sin = jnp.cos(ang), jnp.sin(ang)
    half = ROT_DIM // 2
    z = lambda n: jnp.zeros((ang.shape[0], n), F32)
    c = jnp.concatenate([cos, cos, jnp.ones((ang.shape[0], HEAD_DIM - ROT_DIM), F32)], axis=1)
    sp = jnp.concatenate([z(half), sin, z(HEAD_DIM - ROT_DIM)], axis=1)
    sm = jnp.concatenate([-sin, z(HEAD_DIM - half)], axis=1)
    return tuple(jnp.tile(a, (1, HEADS_PER_STEP)) for a in (c, sp, sm))


def rope(xarr, col_block, width, tables, transpose, out_dtype, name, ts=512):
    t = xarr.shape[0]
    half = ROT_DIM // 2
    reps = width // LANES

    def body(x_ref, c_ref, sp_ref, sm_ref, o_ref):
        xv = x_ref[...].astype(F32)
        wide = lambda r: jnp.concatenate([r[...]] * reps, axis=1)
        c, sp, sm = wide(c_ref), wide(sp_ref), wide(sm_ref)
        if transpose:
            out = xv * c + pltpu.roll(xv * sp, width - half, 1) + pltpu.roll(xv * sm, half, 1)
        else:
            out = xv * c + pltpu.roll(xv, half, 1) * sp + pltpu.roll(xv, width - half, 1) * sm
        o_ref[...] = out.astype(o_ref.dtype)

    tab = pl.BlockSpec((ts, LANES), lambda i: (i, 0))
    return _pcall(
        body, name=name, grid=(t // ts,),
        in_specs=[pl.BlockSpec((ts, width), lambda i: (i, col_block)), tab, tab, tab],
        out_specs=pl.BlockSpec((ts, width), lambda i: (i, 0)),
        out_shape=jax.ShapeDtypeStruct((t, width), out_dtype), compiler_params=_params(("arbitrary",)),
    )(xarr, *tables)


def _scan_lanes(x, reverse):
    n = x.shape[-1]
    lane = lax.broadcasted_iota(jnp.int32, x.shape, x.ndim - 1)
    k = 1
    while k < n:
        if reverse:
            x = x + jnp.where(lane < n - k, pltpu.roll(x, n - k, x.ndim - 1), 0.0)
        else:
            x = x + jnp.where(lane >= k, pltpu.roll(x, k, x.ndim - 1), 0.0)
        k *= 2
    return x


def _log_sigmoid(z):
    return jnp.minimum(z, 0.0) - jnp.log(1.0 + jnp.exp(-jnp.abs(z)))


def fox_gate_fwd(ft, b_forget, name):
    nb, nh, s = ft.shape

    def body(f_ref, b_ref, o_ref):
        z = f_ref[0] + b_ref[...]
        o_ref[0] = -_scan_lanes(_log_sigmoid(z), False)

    spec = pl.BlockSpec((1, nh, s), lambda b: (b, 0, 0))
    return _pcall(
        body, name=name, grid=(nb,), in_specs=[spec, pl.BlockSpec((nh, 1), lambda b: (0, 0))], out_specs=spec,
        out_shape=jax.ShapeDtypeStruct((nb, nh, s), F32), compiler_params=_params(("arbitrary",)),
    )(ft, b_forget)


def fox_gate_bwd(dcb, drow, ft, b_forget, name):
    nb, nh, s = ft.shape

    def body(d_ref, r_ref, f_ref, b_ref, dz_ref, db_ref):
        b = pl.program_id(0)
        z = f_ref[0] + b_ref[...]
        dlf = _scan_lanes(r_ref[0] - d_ref[0], True)
        dz = dlf * _sigmoid(-z)
        dz_ref[0] = dz
        db = jnp.sum(dz, axis=1, keepdims=True)

        @pl.when(b == 0)
        def _():
            db_ref[...] = db

        @pl.when(b > 0)
        def _():
            db_ref[...] += db

    spec = pl.BlockSpec((1, nh, s), lambda b: (b, 0, 0))
    vec = pl.BlockSpec((nh, 1), lambda b: (0, 0))
    return _pcall(
        body, name=name, grid=(nb,), in_specs=[spec, spec, spec, vec], out_specs=[spec, vec],
        out_shape=[jax.ShapeDtypeStruct((nb, nh, s), F32), jax.ShapeDtypeStruct((nh, 1), F32)],
        compiler_params=_params(("arbitrary",)),
    )(dcb, drow, ft, b_forget)


ATTN_TQ = 512
ATTN_TK = 512
ONES_ROWS = 16


def _block_delta(s, tq, tk):
    off = jnp.arange(s // tk) - (tq // tk - 1)
    return off[:, None, None] * tk + jnp.arange(tq)[None, None, :] - jnp.arange(tk)[None, :, None]


def dilated_table(s, tq, tk):
    delta = _block_delta(s, tq, tk)
    count = jnp.zeros(delta.shape, F32)
    for window, dil in DILATED_PATTERNS:
        count = count + ((delta >= 0) & (delta <= window) & (delta % dil == 0)).astype(F32)
    return jnp.where(count > 0, jnp.log(jnp.maximum(count, 1.0)), NEG)


def causal_table(s, tq, tk):
    return jnp.where(_block_delta(s, tq, tk) >= 0, 0.0, NEG).astype(F32)


def attn_fwd(q_arr, q_off, k_arr, k_off, v_arr, v_off, table, colbias, nb, name, side=None):
    t = q_arr.shape[0]
    s = t // nb
    tk, tq = table.shape[1:]
    nq, nk, r = s // tq, s // tk, tq // tk
    npairs = WIDTH_A // LANES
    use_cb = colbias is not None

    def body(*refs):
        refs = list(refs)
        q_ref, k_ref, v_ref, tab_ref = refs[:4]
        cb_ref = refs[4] if use_cb else None
        o_ref, lse_ref, vt_s = refs[-3 - HEADS_PER_STEP:-HEADS_PER_STEP]
        acc_s = refs[-HEADS_PER_STEP:]
        qi = pl.program_id(2)

        heads = [slice(h * HEAD_DIM, (h + 1) * HEAD_DIM) for h in range(HEADS_PER_STEP)]

        @pl.when(qi == 0)
        def _():
            for cblk in range(nk):
                vt = v_ref[cblk * tk:(cblk + 1) * tk, :].astype(F32).T.astype(BF16)
                for h, hs in enumerate(heads):
                    vt_s[cblk, h, 0:HEAD_DIM, :] = vt[hs, :]
                    vt_s[cblk, h, HEAD_DIM:, :] = jnp.ones((ONES_ROWS, tk), BF16)

        qs = [(q_ref[:, hs].astype(F32) * ATTN_SCALE).astype(BF16) for hs in heads]
        for a in acc_s:
            a[...] = jnp.zeros_like(a)

        def step(kb, carry):
            ks = pl.multiple_of(kb * tk, tk)
            tab = tab_ref[qi * r + (r - 1) - kb]
            sts = []
            for h, hs in enumerate(heads):
                st = lax.dot_general(k_ref[pl.ds(ks, tk), hs], qs[h], NT_DIMS, preferred_element_type=F32) + tab
                if use_cb:
                    st = st + cb_ref[0, h, pl.ds(ks, tk), :]
                sts.append(st)
            m_new = [jnp.maximum(carry[h], jnp.max(sts[h], axis=0, keepdims=True)) for h in range(HEADS_PER_STEP)]
            for h in range(HEADS_PER_STEP):
                pt = jnp.exp(sts[h] - m_new[h]).astype(BF16)
                acc_s[h][...] = (jnp.exp(carry[h] - m_new[h]) * acc_s[h][...]
                                 + jnp.dot(vt_s[kb, h], pt, preferred_element_type=F32))
            return tuple(m_new)

        fin = lax.fori_loop(0, (qi + 1) * r, step, tuple(jnp.full((1, tq), NEG, F32) for _ in heads))
        outs = []
        for h in range(HEADS_PER_STEP):
            l = acc_s[h][HEAD_DIM:HEAD_DIM + 1, :]
            outs.append(acc_s[h][0:HEAD_DIM, :] / l)
            lse_ref[0, h, 0] = fin[h] + jnp.log(l)
        o_ref[...] = jnp.concatenate(outs, axis=0).T

    def seq_spec(off):
        return pl.BlockSpec((s, LANES), lambda b, j, i: (b, off + j))

    in_specs = [pl.BlockSpec((tq, LANES), lambda b, j, i: (b * nq + i, q_off + j)), seq_spec(k_off), seq_spec(v_off),
                pl.BlockSpec(table.shape, lambda b, j, i: (0, 0, 0))]
    args = [q_arr, k_arr, v_arr, table]
    if use_cb:
        in_specs.append(pl.BlockSpec((1, HEADS_PER_STEP, s, 1), lambda b, j, i: (b, j, 0, 0)))
        args.append(colbias)
    n_heads = npairs * HEADS_PER_STEP
    return _pcall(
        body, side=side, name=name, grid=(nb, npairs, nq), in_specs=in_specs,
        out_specs=[pl.BlockSpec((tq, LANES), lambda b, j, i: (b * nq + i, j)),
                   pl.BlockSpec((1, HEADS_PER_STEP, 1, 1, tq), lambda b, j, i: (b, j, i, 0, 0))],
        out_shape=[jax.ShapeDtypeStruct((t, npairs * LANES), F32), jax.ShapeDtypeStruct((nb, n_heads, nq, 1, tq), F32)],
        scratch_shapes=[pltpu.VMEM((nk, HEADS_PER_STEP, HEAD_DIM + ONES_ROWS, tk), BF16)]
        + [pltpu.VMEM((HEAD_DIM + ONES_ROWS, tq), F32)] * HEADS_PER_STEP,
        compiler_params=_params(("arbitrary", "arbitrary", "arbitrary")),
    )(*args)


def attn_bwd(q_arr, q_off, k_arr, k_off, v_arr, v_off, o_arr, lse_arr, do_arr, table, colbias, nb, qk_dtype, name,
             side=None):
    t = q_arr.shape[0]
    s = t // nb
    tk, tq = table.shape[1:]
    nq, nk, r = s // tq, s // tk, tq // tk
    npairs = WIDTH_A // LANES
    use_cb = colbias is not None

    def body(*refs):
        refs = list(refs)
        q_ref, k_ref, v_ref, o_ref, lse_ref, do_ref, tab_ref = refs[:7]
        pos = 7
        cb_ref = None
        if use_cb:
            cb_ref = refs[pos]
            pos += 1
        dq_ref, dk_ref, dv_ref = refs[pos:pos + 3]
        pos += 3
        dcb_ref = drow_ref = None
        if use_cb:
            dcb_ref, drow_ref = refs[pos:pos + 2]
            pos += 2
        kt_s, dkt_s, dvt_s = refs[pos:pos + 3]
        dqt_s = refs[pos + 3:pos + 3 + HEADS_PER_STEP]
        dcb_s = refs[pos + 3 + HEADS_PER_STEP] if use_cb else None

        heads = [slice(h * HEAD_DIM, (h + 1) * HEAD_DIM) for h in range(HEADS_PER_STEP)]
        for cblk in range(nk):
            kt_s[cblk] = k_ref[cblk * tk:(cblk + 1) * tk, :].astype(F32).T.astype(BF16)
        dkt_s[...] = jnp.zeros_like(dkt_s)
        dvt_s[...] = jnp.zeros_like(dvt_s)
        if use_cb:
            dcb_s[...] = jnp.zeros_like(dcb_s)
        ones = jnp.ones((8, HEAD_DIM), BF16)

        def q_loop(qi, carry):
            qs = pl.multiple_of(qi * tq, tq)
            q_all = (q_ref[pl.ds(qs, tq), :].astype(F32) * ATTN_SCALE)
            do_all = do_ref[pl.ds(qs, tq), :]
            qt_all = q_all.T.astype(BF16)
            dot_all = do_all.T.astype(BF16)
            q, do_b, qt, dot, lse, dsum = [], [], [], [], [], []
            for h, hs in enumerate(heads):
                q.append(q_all[:, hs].astype(BF16))
                do_b.append(do_all[:, hs].astype(BF16))
                qt.append(qt_all[hs, :])
                dot.append(dot_all[hs, :])
                lse.append(lse_ref[0, h, qi])
                prod = do_all[:, hs] * o_ref[pl.ds(qs, tq), hs]
                hi = prod.astype(BF16)
                lo = (prod - hi.astype(F32)).astype(BF16)
                dsum.append((lax.dot_general(ones, hi, NT_DIMS, preferred_element_type=F32)
                             + lax.dot_general(ones, lo, NT_DIMS, preferred_element_type=F32))[0:1, :])
            for a in dqt_s:
                a[...] = jnp.zeros_like(a)

            def k_loop(kb, drow):
                ks = pl.multiple_of(kb * tk, tk)
                tab = tab_ref[qi * r + (r - 1) - kb]
                sts, dpts, out = [], [], []
                for h, hs in enumerate(heads):
                    st = lax.dot_general(k_ref[pl.ds(ks, tk), hs], q[h], NT_DIMS, preferred_element_type=F32) + tab
                    if use_cb:
                        st = st + cb_ref[0, h, pl.ds(ks, tk), :]
                    sts.append(st)
                    dpts.append(lax.dot_general(v_ref[pl.ds(ks, tk), hs], do_b[h], NT_DIMS, preferred_element_type=F32))
                for h, hs in enumerate(heads):
                    pt = jnp.exp(sts[h] - lse[h])
                    dst = pt * (dpts[h] - dsum[h])
                    dst_b = dst.astype(BF16)
                    dvt_s[h, kb] += lax.dot_general(dot[h], pt.astype(BF16), NT_DIMS, preferred_element_type=F32)
                    dkt_s[h, kb] += lax.dot_general(qt[h], dst_b, NT_DIMS, preferred_element_type=F32)
                    dqt_s[h][...] += jnp.dot(kt_s[kb, hs, :], dst_b, preferred_element_type=F32)
                    if use_cb:
                        dcb_s[h, pl.ds(ks, tk), :] += jnp.sum(dst, axis=1, keepdims=True)
                        out.append(drow[h] + jnp.sum(dst, axis=0, keepdims=True))
                    else:
                        out.append(drow[h])
                return tuple(out)

            drow = lax.fori_loop(0, (qi + 1) * r, k_loop, tuple(jnp.zeros((1, tq), F32) for _ in heads))
            dqt = jnp.concatenate([a[...] for a in dqt_s], axis=0)
            dq_ref[pl.ds(qs, tq), :] = (dqt * ATTN_SCALE).T.astype(dq_ref.dtype)
            if use_cb:
                for h in range(HEADS_PER_STEP):
                    drow_ref[0, h, qi] = drow[h]
            return carry

        lax.fori_loop(0, nq, q_loop, 0)
        for cblk in range(nk):
            rows = slice(cblk * tk, (cblk + 1) * tk)
            dk_ref[rows, :] = jnp.concatenate([dkt_s[h, cblk] for h in range(HEADS_PER_STEP)], axis=0).T.astype(dk_ref.dtype)
            dv_ref[rows, :] = jnp.concatenate([dvt_s[h, cblk] for h in range(HEADS_PER_STEP)], axis=0).T.astype(dv_ref.dtype)
        if use_cb:
            for h in range(HEADS_PER_STEP):
                dcb_ref[0, h] = dcb_s[h]

    def seq_spec(off):
        return pl.BlockSpec((s, LANES), lambda b, j: (b, off + j))

    row_spec = pl.BlockSpec((1, HEADS_PER_STEP, nq, 1, tq), lambda b, j: (b, j, 0, 0, 0))
    in_specs = [seq_spec(q_off), seq_spec(k_off), seq_spec(v_off), seq_spec(0), row_spec, seq_spec(0),
                pl.BlockSpec(table.shape, lambda b, j: (0, 0, 0))]
    args = [q_arr, k_arr, v_arr, o_arr, lse_arr, do_arr, table]
    width = npairs * LANES
    out_specs = [seq_spec(0)] * 3
    out_shape = [jax.ShapeDtypeStruct((t, width), qk_dtype), jax.ShapeDtypeStruct((t, width), qk_dtype),
                 jax.ShapeDtypeStruct((t, width), BF16)]
    scratch = [pltpu.VMEM((nk, LANES, tk), BF16), pltpu.VMEM((HEADS_PER_STEP, nk, HEAD_DIM, tk), F32),
               pltpu.VMEM((HEADS_PER_STEP, nk, HEAD_DIM, tk), F32)] + [pltpu.VMEM((HEAD_DIM, tq), F32)] * HEADS_PER_STEP
    if use_cb:
        cb_spec = pl.BlockSpec((1, HEADS_PER_STEP, s, 1), lambda b, j: (b, j, 0, 0))
        in_specs.append(cb_spec)
        args.append(colbias)
        out_specs += [cb_spec, row_spec]
        out_shape += [jax.ShapeDtypeStruct(colbias.shape, F32), jax.ShapeDtypeStruct(lse_arr.shape, F32)]
        scratch.append(pltpu.VMEM((HEADS_PER_STEP, s, 1), F32))
    return _pcall(
        body, side=side, name=name, grid=(nb, npairs), in_specs=in_specs, out_specs=out_specs, out_shape=out_shape,
        scratch_shapes=scratch, compiler_params=_params(("arbitrary", "arbitrary")),
    )(*args)


def ada_fwd(c_all, w_ada, b_cols, name):
    def body(c_ref, w_ref, b_ref, o_ref):
        cv = c_ref[...]
        sc = (cv * _sigmoid(cv)).astype(BF16)
        o_ref[...] = jnp.dot(sc, w_ref[...].astype(BF16), preferred_element_type=F32) + b_ref[...]

    return _pcall(body, name=name, out_shape=jax.ShapeDtypeStruct((c_all.shape[0], w_ada.shape[1]), F32),
                  compiler_params=_params())(c_all, w_ada, b_cols)


def ada_bwd(c_all, dmod_cols, name):
    def body(c_ref, d_ref, o_ref):
        cv = c_ref[...]
        sc = (cv * _sigmoid(cv)).astype(BF16)
        o_ref[...] = lax.dot_general(sc, d_ref[...].astype(BF16), TN_DIMS, preferred_element_type=F32)

    return _pcall(body, name=name, out_shape=jax.ShapeDtypeStruct((c_all.shape[1], dmod_cols.shape[1]), F32),
                  compiler_params=_params())(c_all, dmod_cols)


def adamw(parts, group, w, m, v, name, tr=None):
    n = parts.shape[0]
    r, c = w.shape
    tr = r if tr is None else tr
    c1 = 1.0 - ADAM_B1 ** ADAM_STEP
    c2 = 1.0 - ADAM_B2 ** ADAM_STEP

    def body(p_ref, w_ref, m_ref, v_ref, g_ref, d_ref, nm_ref, nv_ref):
        g = p_ref[0, 0].astype(F32)
        for i in range(1, n):
            g = g + p_ref[i, 0].astype(F32)
        wv = w_ref[...]
        nm = ADAM_B1 * m_ref[...] + (1.0 - ADAM_B1) * g
        nv = ADAM_B2 * v_ref[...] + (1.0 - ADAM_B2) * (g * g)
        g_ref[...] = g
        nm_ref[...] = nm
        nv_ref[...] = nv
        d_ref[...] = -ADAM_LR * ((nm / c1) / (jnp.sqrt(nv / c2) + ADAM_EPS) + ADAM_WD * wv)

    spec = pl.BlockSpec((tr, c), lambda i: (i, 0))
    shape = jax.ShapeDtypeStruct((r, c), F32)
    return _pcall(
        body, name=name, grid=(r // tr,),
        in_specs=[pl.BlockSpec((n, 1, tr, c), lambda i: (0, group, i, 0)), spec, spec, spec],
        out_specs=[spec] * 4, out_shape=[shape] * 4, compiler_params=_params(("arbitrary",)),
    )(parts, w, m, v)


def all_gather(arrs, name):
    n = len(arrs)
    hbm = pl.BlockSpec(memory_space=pl.ANY)

    def body(*refs):
        ins, outs = refs[:n], refs[n:2 * n]
        send_sems, recv_sems, local_sems = refs[2 * n:]
        x, y, c = _place()
        me, sibling = (x, y, c), (x, y, 1 - c)
        chips = [(1 - x, y), (x, 1 - y), (1 - x, 1 - y)]

        def copy(a, k, block, to, src=None):
            dst = outs[a].at[_slot(block)]
            return pltpu.make_async_remote_copy(
                src_ref=dst if src is None else src, dst_ref=dst, send_sem=send_sems.at[a * 7 + k],
                recv_sem=recv_sems.at[a * 7 + k], device_id=to, device_id_type=MESH)

        mine = [pltpu.make_async_copy(ins[a], outs[a].at[_slot(me)], local_sems.at[a]) for a in range(n)]
        for cp in mine:
            cp.start()
        first = []
        for a in range(n):
            first.append(copy(a, 0, me, sibling, src=ins[a]))
            first += [copy(a, 1 + j, me, (*chip, c), src=ins[a]) for j, chip in enumerate(chips)]
        for cp in first:
            cp.start()
        passed = []
        for a in range(n):
            for j, chip in enumerate(chips):
                copy(a, 1 + j, (*chip, c), me).wait_recv()
                cp = copy(a, 4 + j, (*chip, c), sibling)
                cp.start()
                passed.append(cp)
        for a in range(n):
            copy(a, 0, sibling, me).wait_recv()
            for j, chip in enumerate(chips):
                copy(a, 4 + j, (*chip, 1 - c), me).wait_recv()
        for cp in first + passed:
            cp.wait_send()
        for cp in mine:
            cp.wait()

    return _pcall(
        body, name=name, in_specs=[hbm] * n, out_specs=[hbm] * n,
        out_shape=[jax.ShapeDtypeStruct((N_DEV,) + a.shape, a.dtype) for a in arrs],
        scratch_shapes=[pltpu.SemaphoreType.DMA((7 * n,)), pltpu.SemaphoreType.DMA((7 * n,)),
                        pltpu.SemaphoreType.DMA((n,))],
        compiler_params=pltpu.CompilerParams(has_side_effects=True),
    )(*arrs)


def _t(w):
    return jnp.swapaxes(w, -1, -2)


def _rows_from_blocks(blocks, pad_to=None):
    full = blocks.reshape(-1, blocks.shape[2])
    if pad_to is not None and pad_to > full.shape[0]:
        full = jnp.pad(full, ((0, pad_to - full.shape[0]), (0, 0)))
    return full


def _rows_to_blocks(full, nrows):
    return full[:nrows].reshape(N_DEV, nrows // N_DEV, full.shape[1])


SMALL_ORDER = ("g_pre_ff1", "g_post_ff1", "g_pre_mix", "g_post_mix", "g_out_a", "g_out_b", "g_pre_ff2", "g_post_ff2",
               "b_forget")


def _pack_small(vals):
    rows = []
    for name in SMALL_ORDER:
        v = vals[name].reshape(1, -1)
        if v.shape[1] % LANES:
            v = jnp.pad(v, ((0, 0), (0, LANES - v.shape[1] % LANES)))
        rows.append(v)
    return jnp.concatenate(rows, axis=1)


def _unpack_small(row, sizes):
    out, pos = {}, 0
    for name in SMALL_ORDER:
        n = sizes[name]
        out[name] = row[:, pos:pos + n]
        pos += -(-n // LANES) * LANES
    return out


def _ffn_forward(x, mod, g_pre, g_post, wg, wu, wd, i0, nb, tag, target=None, side=None):
    h = prenorm_fwd(x, g_pre, mod, i0, i0 + 1, nb, f"{tag}_prenorm")
    res, side_out = ffn_up(h, wg, wu, f"{tag}_up", side=side), None
    if side is not None:
        res, side_out = res
    gate, up, act = res
    y0 = mm_rows([(act, wd)], False, F32, f"{tag}_down")
    out = postnorm_fwd(x, y0, g_post, mod, i0 + 2, 0.5, nb, f"{tag}_postnorm", target=target)
    return out, (x, h, gate, up, act, y0), side_out


def _ffn_backward(dxo, saved, mod, g_pre, g_post, wg, wu, wd, i0, nb, tag, side=None, chain=False):
    x, h, gate, up, act, y0 = saved
    dy0, dg_post, dgate_mod = postnorm_bwd(dxo, y0, g_post, mod, i0 + 2, 0.5, nb, f"{tag}_postnorm_bwd")
    dwd = mm_tn(act, dy0, BF16, f"{tag}_dwd")
    res, side_out = ffn_down_bwd(dy0, wd, gate, up, f"{tag}_down_bwd", side=side), None
    if side is not None:
        res, side_out = res
    dgate, dup = res
    dh_pairs = [(dgate, wg), (dup, wu)]
    if chain:
        dwg, (dwd,) = mm_tn(dgate, h, BF16, f"{tag}_dwg", side=([_rows_to_blocks(dwd, D_FF)[:, None]], False))
        dwu, (dwg,) = mm_tn(dup, h, BF16, f"{tag}_dwu", side=([_rows_to_blocks(dwg, D_FF)[:, None]], False))
        dh, (dwu,) = mm_rows(dh_pairs, False, F32, f"{tag}_dh", side=([_rows_to_blocks(dwu, D_FF)[:, None]], False))
    else:
        dwg = mm_tn(dgate, h, BF16, f"{tag}_dwg")
        dwu = mm_tn(dup, h, BF16, f"{tag}_dwu")
        dh = mm_rows(dh_pairs, False, F32, f"{tag}_dh")
    dx, dg_pre, dsc, dsh = prenorm_bwd(dh, x, g_pre, mod, i0 + 1, dxo, nb, f"{tag}_prenorm_bwd")
    return dx, dict(g_pre=dg_pre, g_post=dg_post, wg=dwg, wu=dwu, wd=dwd, mod=(dsh, dsc, dgate_mod)), side_out


def kernel(x, c, positions, w_ada, b_ada, g_pre_ff1, g_post_ff1, w_ff1_gate, w_ff1_up, w_ff1_down, g_pre_mix, g_post_mix, w_in, b_forget, g_out_a, g_out_b, w_out, g_pre_ff2, g_post_ff2, w_ff2_gate, w_ff2_up, w_ff2_down, loss_target, m_w_ada, m_b_ada, m_g_pre_ff1, m_g_post_ff1, m_w_ff1_gate, m_w_ff1_up, m_w_ff1_down, m_g_pre_mix, m_g_post_mix, m_w_in, m_b_forget, m_g_out_a, m_g_out_b, m_w_out, m_g_pre_ff2, m_g_post_ff2, m_w_ff2_gate, m_w_ff2_up, m_w_ff2_down, v_w_ada, v_b_ada, v_g_pre_ff1, v_g_post_ff1, v_w_ff1_gate, v_w_ff1_up, v_w_ff1_down, v_g_pre_mix, v_g_post_mix, v_w_in, v_b_forget, v_g_out_a, v_g_out_b, v_w_out, v_g_pre_ff2, v_g_post_ff2, v_w_ff2_gate, v_w_ff2_up, v_w_ff2_down):
    weights = dict(w_ada=w_ada, b_ada=b_ada, g_pre_ff1=g_pre_ff1, g_post_ff1=g_post_ff1, w_ff1_gate=w_ff1_gate,
                   w_ff1_up=w_ff1_up, w_ff1_down=w_ff1_down, g_pre_mix=g_pre_mix, g_post_mix=g_post_mix, w_in=w_in,
                   b_forget=b_forget, g_out_a=g_out_a, g_out_b=g_out_b, w_out=w_out, g_pre_ff2=g_pre_ff2,
                   g_post_ff2=g_post_ff2, w_ff2_gate=w_ff2_gate, w_ff2_up=w_ff2_up, w_ff2_down=w_ff2_down)
    mom_m = dict(w_ada=m_w_ada, b_ada=m_b_ada, g_pre_ff1=m_g_pre_ff1, g_post_ff1=m_g_post_ff1, w_ff1_gate=m_w_ff1_gate,
                 w_ff1_up=m_w_ff1_up, w_ff1_down=m_w_ff1_down, g_pre_mix=m_g_pre_mix, g_post_mix=m_g_post_mix,
                 w_in=m_w_in, b_forget=m_b_forget, g_out_a=m_g_out_a, g_out_b=m_g_out_b, w_out=m_w_out,
                 g_pre_ff2=m_g_pre_ff2, g_post_ff2=m_g_post_ff2, w_ff2_gate=m_w_ff2_gate, w_ff2_up=m_w_ff2_up,
                 w_ff2_down=m_w_ff2_down)
    mom_v = dict(w_ada=v_w_ada, b_ada=v_b_ada, g_pre_ff1=v_g_pre_ff1, g_post_ff1=v_g_post_ff1, w_ff1_gate=v_w_ff1_gate,
                 w_ff1_up=v_w_ff1_up, w_ff1_down=v_w_ff1_down, g_pre_mix=v_g_pre_mix, g_post_mix=v_g_post_mix,
                 w_in=v_w_in, b_forget=v_b_forget, g_out_a=v_g_out_a, g_out_b=v_g_out_b, w_out=v_w_out,
                 g_pre_ff2=v_g_pre_ff2, g_post_ff2=v_g_post_ff2, w_ff2_gate=v_w_ff2_gate, w_ff2_up=v_w_ff2_up,
                 w_ff2_down=v_w_ff2_down)
    order = list(weights)

    nb, s, d = x.shape
    t = nb * s
    me = _slot(_place())
    nbg = nb * N_DEV
    ada_cols = w_ada.shape[2]

    bf = lambda w: w[0].astype(BF16)
    bft = lambda w: _t(w)[0].astype(BF16)
    c_all, ff1_all = all_gather([c, jnp.stack([bft(w_ff1_gate), bft(w_ff1_up), bf(w_ff1_down)])], "gather_ff1")
    c_all = c_all.reshape(nbg, d)
    wg1, wu1, wd1 = (_rows_from_blocks(ff1_all[:, i], D_FF_PAD) for i in range(3))

    b_cols = lax.dynamic_slice(b_ada, (0, me * ada_cols), (1, ada_cols))
    mod_cols = ada_fwd(c_all, w_ada[0], b_cols, "ada_fwd")
    (mod_all,) = all_gather([mod_cols], "gather_mod")
    mod = lax.dynamic_slice(mod_all, (0, me * nb, 0), (N_DEV, nb, ada_cols))
    mod = mod.transpose(1, 0, 2).reshape(nb, N_MOD, d)

    xf = x.reshape(t, d)
    target = loss_target.reshape(t, d)

    x1, saved1, (w_in_all, w_out_all) = _ffn_forward(xf, mod, g_pre_ff1, g_post_ff1, wg1, wu1, wd1, 0, nb, "ff1",
                                                     side=([bft(w_in), bf(w_out)], True))
    w_in_t = _rows_from_blocks(w_in_all)
    n_qkv = 3 * (WIDTH_A + WIDTH_B)
    w_qkv_t = w_in_t[:n_qkv]
    w_f_t = jnp.pad(w_in_t[n_qkv:], ((0, LANES - N_HEADS_B), (0, 0)))
    w_o = _rows_from_blocks(w_out_all)
    w_o_a, w_o_b = w_o[:WIDTH_A], w_o[WIDTH_A:]

    h2 = prenorm_fwd(x1, g_pre_mix, mod, 3, 4, nb, "mix_prenorm")
    proj = mm_rows([(h2, w_qkv_t)], True, BF16, "mix_proj")
    f_logit = mm_rows([(h2, w_f_t)], True, F32, "mix_forget")
    tables = rope_tables(positions)
    qk_rot = rope(proj, 0, 2 * WIDTH_A, tables, False, BF16, "rope")
    tab_a = dilated_table(s, ATTN_TQ, ATTN_TK)
    tab_b = causal_table(s, ATTN_TQ, ATTN_TK)
    ft = f_logit[:, :N_HEADS_B].reshape(nb, s, N_HEADS_B).transpose(0, 2, 1)
    bf_col = b_forget.reshape(N_HEADS_B, 1)
    colbias = fox_gate_fwd(ft, bf_col, "fox_gate").reshape(nb, N_HEADS_B, s, 1)
    pa = WIDTH_A // LANES
    (o_a, lse_a), (ff2_all,) = attn_fwd(
        qk_rot, 0, qk_rot, pa, proj, 2 * pa, tab_a, None, nb, "attn_a",
        side=([jnp.stack([bft(w_ff2_gate), bft(w_ff2_up), bf(w_ff2_down)])], True))
    wg2, wu2, wd2 = (_rows_from_blocks(ff2_all[:, i], D_FF_PAD) for i in range(3))
    o_b, lse_b = attn_fwd(proj, 3 * pa, proj, 4 * pa, proj, 5 * pa, tab_b, colbias, nb, "attn_b")
    m_a = prenorm_fwd(o_a, g_out_a, None, None, None, nb, "out_norm_a")
    m_b = prenorm_fwd(o_b, g_out_b, None, None, None, nb, "out_norm_b")
    y0m = mm_rows([(m_a, w_o_a), (m_b, w_o_b)], False, F32, "mix_out")
    x2 = postnorm_fwd(x1, y0m, g_post_mix, mod, 5, 1.0, nb, "mix_postnorm")

    (dx3, loss_part), saved2, _ = _ffn_forward(x2, mod, g_pre_ff2, g_post_ff2, wg2, wu2, wd2, 6, nb, "ff2", target=target)
    loss = lax.psum(loss_part[0, 0], ("x", "y", "c"))

    dx2, gr2, _ = _ffn_backward(dx3, saved2, mod, g_pre_ff2, g_post_ff2, wg2, wu2, wd2, 6, nb, "ff2")
    ff2_blocks = [jnp.stack([_rows_to_blocks(gr2[k], D_FF) for k in ("wg", "wu", "wd")], axis=1)]

    dy0m, dg_post_mix, dgate_mix = postnorm_bwd(dx2, y0m, g_post_mix, mod, 5, 1.0, nb, "mix_postnorm_bwd")
    dw_o_a = mm_tn(m_a, dy0m, BF16, "mix_dwo_a")
    dw_o_b = mm_tn(m_b, dy0m, BF16, "mix_dwo_b")
    dm_a = mm_rows([(dy0m, w_o_a)], True, F32, "mix_dm_a")
    dm_b = mm_rows([(dy0m, w_o_b)], True, F32, "mix_dm_b")
    do_a, dg_out_a = prenorm_bwd(dm_a, o_a, g_out_a, None, None, None, nb, "out_norm_a_bwd")
    do_b, dg_out_b = prenorm_bwd(dm_b, o_b, g_out_b, None, None, None, nb, "out_norm_b_bwd")
    (dq_a, dk_a, dv_a), (g_ff2,) = attn_bwd(qk_rot, 0, qk_rot, pa, proj, 2 * pa, o_a, lse_a, do_a, tab_a, None,
                                                    nb, F32, "attn_a_bwd", side=(ff2_blocks, False))
    dq_b, dk_b, dv_b, dcb, drow = attn_bwd(proj, 3 * pa, proj, 4 * pa, proj, 5 * pa, o_b, lse_b, do_b, tab_b, colbias, nb,
                                     BF16, "attn_b_bwd")
    dq_a = rope(dq_a, 0, WIDTH_A, tables, True, BF16, "rope_bwd_q")
    dk_a = rope(dk_a, 0, WIDTH_A, tables, True, BF16, "rope_bwd_k")
    dz_t, db_forget = fox_gate_bwd(dcb.reshape(nb, N_HEADS_B, s), drow.reshape(nb, N_HEADS_B, s), ft, bf_col,
                                   "fox_gate_bwd")
    dz = jnp.pad(dz_t.transpose(0, 2, 1).reshape(t, N_HEADS_B), ((0, 0), (0, LANES - N_HEADS_B))).astype(BF16)
    pieces = [dq_a, dk_a, dv_a, dq_b, dk_b, dv_b]
    w_pieces = [w_qkv_t[i * WIDTH_A:(i + 1) * WIDTH_A] for i in range(6)]
    dh2 = mm_rows(list(zip(pieces, w_pieces)) + [(dz, w_f_t)], False, F32, "mix_dh")
    dw_in_t = jnp.concatenate([mm_tn(p, h2, BF16, f"mix_dwin_{i}") for i, p in enumerate(pieces)]
                              + [mm_tn(dz, h2, BF16, "mix_dwin_f")[:N_HEADS_B]], axis=0)
    dx1, dg_pre_mix, dsc_mix, dsh_mix = prenorm_bwd(dh2, x1, g_pre_mix, mod, 4, dx2, nb, "mix_prenorm_bwd")

    g_in = _rows_to_blocks(dw_in_t, dw_in_t.shape[0])[:, None]
    g_out = _rows_to_blocks(jnp.concatenate([dw_o_a, dw_o_b], axis=0), d)[:, None]
    dx0, gr1, (g_in, g_out) = _ffn_backward(dx1, saved1, mod, g_pre_ff1, g_post_ff1, wg1, wu1, wd1, 0, nb, "ff1",
                                            side=([g_in, g_out], False), chain=True)
    grad_x = dx0.reshape(nb, s, d)

    dmod =jnp.concatenate(list(gr1["mod"]) + [dsh_mix, dsc_mix, dgate_mix] + list(gr2["mod"]), axis=1)
    small = _pack_small(dict(g_pre_ff1=gr1["g_pre"], g_post_ff1=gr1["g_post"], g_pre_mix=dg_pre_mix,
                             g_post_mix=dg_post_mix, g_out_a=dg_out_a, g_out_b=dg_out_b, g_pre_ff2=gr2["g_pre"],
                             g_post_ff2=gr2["g_post"], b_forget=db_forget))
    dmod_all, small_all = all_gather([dmod.reshape(nb, N_MOD * d), small], "gather_small_grads")
    dmod_all = dmod_all.reshape(nbg, N_MOD * d)

    res = {}
    def adamw_t(parts, group, n):
        return tuple(_t(r) for r in adamw(parts, group, _t(weights[n])[0], _t(mom_m[n])[0], _t(mom_v[n])[0], f"adamw_{n}"))

    res["w_ff1_gate"] = adamw_t(gr1["wg"], 0, "w_ff1_gate")
    res["w_ff1_up"] = adamw_t(gr1["wu"], 0, "w_ff1_up")
    res["w_ff2_gate"] = adamw_t(g_ff2, 0, "w_ff2_gate")
    res["w_ff2_up"] = adamw_t(g_ff2, 1, "w_ff2_up")
    res["w_ff1_down"] = adamw(gr1["wd"], 0, w_ff1_down[0], m_w_ff1_down[0], v_w_ff1_down[0], "adamw_ff1_down")
    res["w_ff2_down"] = adamw(g_ff2, 2, w_ff2_down[0], m_w_ff2_down[0], v_w_ff2_down[0], "adamw_ff2_down")
    res["w_in"] = adamw_t(g_in, 0, "w_in")
    res["w_out"] = adamw(g_out, 0, w_out[0], m_w_out[0], v_w_out[0], "adamw_out")
    dmod_cols = lax.dynamic_slice(dmod_all, (0, me * ada_cols), (nbg, ada_cols))
    dw_ada = ada_bwd(c_all, dmod_cols, "ada_bwd")
    res["w_ada"] = adamw(dw_ada[None, None], 0, w_ada[0], m_w_ada[0], v_w_ada[0], "adamw_ada", tr=256)
    res["b_ada"] = adamw(dmod_all[:, None, None], 0, b_ada, m_b_ada, v_b_ada, "adamw_b_ada")
    sizes = {n: weights[n].shape[1] for n in SMALL_ORDER}
    small_res = adamw(small_all[:, None], 0, _pack_small(weights), _pack_small(mom_m), _pack_small(mom_v), "adamw_small")
    small_res = [_unpack_small(r, sizes) for r in small_res]
    for n in SMALL_ORDER:
        res[n] = tuple(r[n] for r in small_res)

    outs = [loss, grad_x]
    for kind in range(4):
        for n in order:
            a = res[n][kind]
            outs.append(a.reshape(weights[n].shape))
    return tuple(outs)
```

```python
import functools

import jax
import jax.numpy as jnp
from jax import lax
from jax.experimental import pallas as pl
from jax.experimental.pallas import tpu as pltpu

F32 = jnp.float32
BF16 = jnp.bfloat16

D_MODEL = 1024
HEAD_DIM = 64
N_HEADS_A = 8
N_HEADS_B = 8
WIDTH_A = N_HEADS_A * HEAD_DIM
WIDTH_B = N_HEADS_B * HEAD_DIM
DILATED_PATTERNS = ((128, 1), (512, 4), (2048, 16))
ROT_DIM = HEAD_DIM // 4
ROPE_THETA = 500000.0
D_FF = 2752
D_FF_PAD = 2816
N_MOD = 9
EPS = 1e-6
ATTN_SCALE = HEAD_DIM ** -0.5
NEG = -1e30
N_DEV = 8
LANES = 128
HEADS_PER_STEP = LANES // HEAD_DIM

ADAM_LR = 0.001
ADAM_B1 = 0.9
ADAM_B2 = 0.999
ADAM_EPS = 1e-08
ADAM_WD = 0.01
ADAM_STEP = 10

VMEM_LIMIT = 56 * 1024 * 1024
MESH = pl.DeviceIdType.MESH

NT_DIMS = (((1,), (1,)), ((), ()))
TN_DIMS = (((0,), (0,)), ((), ()))
NN_DIMS = (((1,), (0,)), ((), ()))


def _place():
    return lax.axis_index("x"), lax.axis_index("y"), lax.axis_index("c")


def _slot(p):
    return 4 * p[0] + 2 * p[1] + p[2]


def _direct_copies(ins, outs, send_sems, recv_sems, local_sems, gather):
    x, y, c = _place()
    me = (x, y, c)
    flip = lambda v, bit: 1 - v if bit else v
    peers = [(flip(x, k & 4), flip(y, k & 2), flip(c, k & 1)) for k in range(1, N_DEV)]
    local, sends, recvs = [], [], []
    for a in range(len(ins)):
        mine = ins[a] if gather else ins[a].at[_slot(me)]
        local.append(pltpu.make_async_copy(mine, outs[a].at[_slot(me)], local_sems.at[a]))
        for k, peer in enumerate(peers):
            sems = dict(send_sem=send_sems.at[a * 7 + k], recv_sem=recv_sems.at[a * 7 + k], device_id=peer,
                        device_id_type=MESH)
            sends.append(pltpu.make_async_remote_copy(
                src_ref=ins[a] if gather else ins[a].at[_slot(peer)], dst_ref=outs[a].at[_slot(me)], **sems))
            recvs.append(pltpu.make_async_remote_copy(src_ref=mine, dst_ref=outs[a].at[_slot(peer)], **sems))
    return local, sends, recvs


def _comm_scratch(n):
    return [pltpu.SemaphoreType.DMA((7 * n,)), pltpu.SemaphoreType.DMA((7 * n,)), pltpu.SemaphoreType.DMA((n,))]


def _pcall(body, side=None, **kw):
    if side is None:
        return pl.pallas_call(body, **kw)
    arrs, gather = side
    n = len(arrs)
    grid = kw["grid"]
    in_specs = list(kw["in_specs"])
    single = not isinstance(kw["out_specs"], (list, tuple))
    out_specs = [kw["out_specs"]] if single else list(kw["out_specs"])
    out_shape = [kw["out_shape"]] if single else list(kw["out_shape"])
    scratch = list(kw.get("scratch_shapes", []))
    n_in, n_out, n_scr = len(in_specs), len(out_specs), len(scratch)
    hbm = pl.BlockSpec(memory_space=pl.ANY)

    def hosted(*refs):
        pos = [0]

        def take(k):
            pos[0] += k
            return refs[pos[0] - k:pos[0]]

        ins, s_ins, outs, s_outs, scr, sems = take(n_in), take(n), take(n_out), take(n), take(n_scr), take(3)
        ids = [pl.program_id(i) for i in range(len(grid))]
        first = functools.reduce(jnp.logical_and, [i == 0 for i in ids])
        last = functools.reduce(jnp.logical_and, [i == g - 1 for i, g in zip(ids, grid)])

        @pl.when(first)
        def _():
            local, sends, _ = _direct_copies(s_ins, s_outs, *sems, gather)
            for cp in local + sends:
                cp.start()

        body(*ins, *outs, *scr)

        @pl.when(last)
        def _():
            local, sends, recvs = _direct_copies(s_ins, s_outs, *sems, gather)
            for cp in recvs:
                cp.wait_recv()
            for cp in sends:
                cp.wait_send()
            for cp in local:
                cp.wait()

    kw.update(in_specs=in_specs + [hbm] * n, out_specs=out_specs + [hbm] * n,
              out_shape=out_shape + [jax.ShapeDtypeStruct(((N_DEV,) + a.shape) if gather else a.shape, a.dtype)
                                     for a in arrs],
              scratch_shapes=scratch + _comm_scratch(n))
    call = pl.pallas_call(hosted, **kw)

    def run(*args):
        res = call(*args, *arrs)
        main = res[0] if single else list(res[:n_out])
        return main, list(res[n_out:])

    return run


def _params(sem=None, **kw):
    if sem is not None:
        kw["dimension_semantics"] = sem
    return pltpu.CompilerParams(vmem_limit_bytes=VMEM_LIMIT, **kw)


def mm_rows(pairs, trans_b, out_dtype, name, tm=512, side=None):
    n = len(pairs)
    m = pairs[0][0].shape[0]
    n_out = pairs[0][1].shape[0 if trans_b else 1]
    dims = NT_DIMS if trans_b else NN_DIMS

    def body(*refs):
        o_ref = refs[2 * n]
        acc = None
        for a_ref, b_ref in zip(refs[:n], refs[n:2 * n]):
            d = lax.dot_general(a_ref[...], b_ref[...], dims, preferred_element_type=F32)
            acc = d if acc is None else acc + d
        o_ref[...] = acc.astype(o_ref.dtype)

    in_specs = [pl.BlockSpec((tm, a.shape[1]), lambda i: (i, 0)) for a, _ in pairs]
    in_specs += [pl.BlockSpec(b.shape, lambda i: (0, 0)) for _, b in pairs]
    return _pcall(
        body, side=side, name=name, grid=(m // tm,), in_specs=in_specs,
        out_specs=pl.BlockSpec((tm, n_out), lambda i: (i, 0)),
        out_shape=jax.ShapeDtypeStruct((m, n_out), out_dtype),
        compiler_params=_params(("arbitrary",)),
    )(*[a for a, _ in pairs], *[b for _, b in pairs])


DH_ROWS = 256
TN_TOKENS = 2048
TN_OUT_ELEMS = 2 * 1024 * 1024


def mm_tn(a, b, out_dtype, name, side=None):
    t, ka = a.shape
    n_out = b.shape[1]
    tk = min(TN_TOKENS, t)
    tka = ka // 2 if ka * n_out > TN_OUT_ELEMS else ka
    tn = n_out
    steps = t // tk

    def body(a_ref, b_ref, o_ref, acc_ref):
        k = pl.program_id(2)
        d = lax.dot_general(a_ref[...], b_ref[...], TN_DIMS, preferred_element_type=F32)

        @pl.when(k == 0)
        def _():
            acc_ref[...] = d

        @pl.when(k > 0)
        def _():
            acc_ref[...] += d

        @pl.when(k == steps - 1)
        def _():
            o_ref[...] = acc_ref[...].astype(o_ref.dtype)

    return _pcall(
        body, side=side, name=name, grid=(ka // tka, n_out // tn, steps),
        in_specs=[pl.BlockSpec((tk, tka), lambda i, j, k: (k, i)), pl.BlockSpec((tk, tn), lambda i, j, k: (k, j))],
        out_specs=pl.BlockSpec((tka, tn), lambda i, j, k: (i, j)),
        out_shape=jax.ShapeDtypeStruct((ka, n_out), out_dtype),
        scratch_shapes=[pltpu.VMEM((tka, tn), F32)],
        compiler_params=_params(("arbitrary", "arbitrary", "arbitrary")),
    )(a, b)


def _col_chunks(width, chunk=512):
    return [slice(c, min(c + chunk, width)) for c in range(0, width, chunk)]


def _sigmoid(x):
    return 1.0 / (1.0 + jnp.exp(-x))


def ffn_up(h, wgt, wut, name, tm=512, tn=1408, side=None):
    t, d = h.shape
    fp = wgt.shape[0]

    def body(h_ref, wg_ref, wu_ref, g_ref, u_ref, a_ref):
        hv = h_ref[...]

        def finish(cols, g, u):
            g_ref[:, cols] = g.astype(BF16)
            u_ref[:, cols] = u.astype(BF16)
            a_ref[:, cols] = (g * _sigmoid(g) * u).astype(BF16)

        pending = None
        for cols in _col_chunks(tn):
            g = lax.dot_general(hv, wg_ref[cols, :], NT_DIMS, preferred_element_type=F32)
            u = lax.dot_general(hv, wu_ref[cols, :], NT_DIMS, preferred_element_type=F32)
            if pending is not None:
                finish(*pending)
            pending = (cols, g, u)
        finish(*pending)

    w_spec = pl.BlockSpec((tn, d), lambda j, i: (j, 0))
    o_spec = pl.BlockSpec((tm, tn), lambda j, i: (i, j))
    o_shape = jax.ShapeDtypeStruct((t, fp), BF16)
    return _pcall(
        body, side=side, name=name, grid=(fp // tn, t // tm),
        in_specs=[pl.BlockSpec((tm, d), lambda j, i: (i, 0)), w_spec, w_spec],
        out_specs=[o_spec, o_spec, o_spec], out_shape=[o_shape, o_shape, o_shape],
        compiler_params=_params(("arbitrary", "arbitrary")),
    )(h, wgt, wut)


def ffn_down_bwd(dy0, wd, gate, up, name, tm=512, tn=1408, side=None):
    t, d = dy0.shape
    fp = wd.shape[0]

    def body(dy_ref, wd_ref, g_ref, u_ref, dg_ref, du_ref):
        dyv = dy_ref[...]

        def finish(cols, dact):
            g = g_ref[:, cols].astype(F32)
            u = u_ref[:, cols].astype(F32)
            sg = _sigmoid(g)
            du_ref[:, cols] = (dact * g * sg).astype(BF16)
            dg_ref[:, cols] = (dact * u * (sg * (1.0 + g * (1.0 - sg)))).astype(BF16)

        pending = None
        for cols in _col_chunks(tn):
            dact = lax.dot_general(dyv, wd_ref[cols, :], NT_DIMS, preferred_element_type=F32)
            if pending is not None:
                finish(*pending)
            pending = (cols, dact)
        finish(*pending)

    t_spec = pl.BlockSpec((tm, tn), lambda j, i: (i, j))
    o_shape = jax.ShapeDtypeStruct((t, fp), BF16)
    return _pcall(
        body, side=side, name=name, grid=(fp // tn, t // tm),
        in_specs=[pl.BlockSpec((tm, d), lambda j, i: (i, 0)), pl.BlockSpec((tn, d), lambda j, i: (j, 0)), t_spec, t_spec],
        out_specs=[t_spec, t_spec], out_shape=[o_shape, o_shape],
        compiler_params=_params(("arbitrary", "arbitrary")),
    )(dy0, wd, gate, up)


def _row_specs(dx, ts, ns):
    return pl.BlockSpec((ts, dx), lambda b, s: (b * ns + s, 0))


def _mod_spec():
    return pl.BlockSpec((1, N_MOD, D_MODEL), lambda b, s: (b, 0, 0))


def _vec_spec(dx):
    return pl.BlockSpec((1, dx), lambda b, s: (0, 0))


def prenorm_fwd(x, g, mod, i_shift, i_scale, nb, name, ts=1024):
    t, dx = x.shape
    ts = min(ts, t // nb)
    ns = t // nb // ts

    def body(*refs):
        if mod is None:
            x_ref, g_ref, h_ref = refs
        else:
            x_ref, g_ref, mod_ref, h_ref = refs
        xv = x_ref[...]
        r = lax.rsqrt(jnp.mean(xv * xv, axis=-1, keepdims=True) + EPS)
        h = xv * r * g_ref[...]
        if mod is not None:
            h = h * (1.0 + mod_ref[0, i_scale:i_scale + 1, :]) + mod_ref[0, i_shift:i_shift + 1, :]
        h_ref[...] = h.astype(BF16)

    in_specs = [_row_specs(dx, ts, ns), _vec_spec(dx)]
    args = [x, g]
    if mod is not None:
        in_specs.append(_mod_spec())
        args.append(mod)
    return _pcall(
        body, name=name, grid=(nb, ns), in_specs=in_specs, out_specs=_row_specs(dx, ts, ns),
        out_shape=jax.ShapeDtypeStruct((t, dx), BF16), compiler_params=_params(("arbitrary", "arbitrary")),
    )(*args)


def prenorm_bwd(dh, x, g, mod, i_scale, dres, nb, name, ts=512, side=None):
    t, dx = x.shape
    ts = min(ts, t // nb)
    ns = t // nb // ts
    has_mod = mod is not None
    has_res = dres is not None
    pairs = dh if isinstance(dh, list) else None
    n_mm = 0 if pairs is None else len(pairs)

    def body(*refs):
        refs = list(refs)
        if pairs is None:
            dhv = refs[0][...].astype(F32)
            refs = refs[1:]
        else:
            dhv = None
            for a_ref, b_ref in zip(refs[:n_mm], refs[n_mm:2 * n_mm]):
                d = jnp.dot(a_ref[...], b_ref[...], preferred_element_type=F32)
                dhv = d if dhv is None else dhv + d
            refs = refs[2 * n_mm:]
        x_ref, g_ref = refs[:2]
        pos = 2
        mod_ref = dres_ref = None
        if has_mod:
            mod_ref = refs[pos]
            pos += 1
        if has_res:
            dres_ref = refs[pos]
            pos += 1
        dx_ref, dg_ref = refs[pos], refs[pos + 1]
        b, s = pl.program_id(0), pl.program_id(1)
        xv = x_ref[...]
        gv = g_ref[...]
        r = lax.rsqrt(jnp.mean(xv * xv, axis=-1, keepdims=True) + EPS)
        xhat = xv * r
        dn = dhv
        if has_mod:
            dsc_ref, dsh_ref = refs[pos + 2], refs[pos + 3]
            dn = dhv * (1.0 + mod_ref[0, i_scale:i_scale + 1, :])
            dsc = jnp.sum(dhv * xhat * gv, axis=0, keepdims=True)[None]
            dsh = jnp.sum(dhv, axis=0, keepdims=True)[None]

            @pl.when(s == 0)
            def _():
                dsc_ref[...] = dsc
                dsh_ref[...] = dsh

            @pl.when(s > 0)
            def _():
                dsc_ref[...] += dsc
                dsh_ref[...] += dsh

        dg = jnp.sum(dn * xhat, axis=0, keepdims=True)
        first = jnp.logical_and(b == 0, s == 0)

        @pl.when(first)
        def _():
            dg_ref[...] = dg

        @pl.when(jnp.logical_not(first))
        def _():
            dg_ref[...] += dg

        dxhat = dn * gv
        dxv = r * (dxhat - xhat * jnp.mean(dxhat * xhat, axis=-1, keepdims=True))
        if has_res:
            dxv = dxv + dres_ref[...]
        dx_ref[...] = dxv

    row = _row_specs(dx, ts, ns)
    if pairs is None:
        in_specs, args = [row], [dh]
    else:
        in_specs = [_row_specs(a.shape[1], ts, ns) for a, _ in pairs]
        in_specs += [pl.BlockSpec(b.shape, lambda b_, s_: (0, 0)) for _, b in pairs]
        args = [a for a, _ in pairs] + [b for _, b in pairs]
    in_specs += [row, _vec_spec(dx)]
    args += [x, g]
    if has_mod:
        in_specs.append(_mod_spec())
        args.append(mod)
    if has_res:
        in_specs.append(row)
        args.append(dres)
    out_specs = [row, _vec_spec(dx)]
    out_shape = [jax.ShapeDtypeStruct((t, dx), F32), jax.ShapeDtypeStruct((1, dx), F32)]
    if has_mod:
        bspec = pl.BlockSpec((1, 1, dx), lambda b, s: (b, 0, 0))
        out_specs += [bspec, bspec]
        out_shape += [jax.ShapeDtypeStruct((nb, 1, dx), F32)] * 2
    return _pcall(
        body, side=side, name=name, grid=(nb, ns), in_specs=in_specs, out_specs=out_specs, out_shape=out_shape,
        compiler_params=_params(("arbitrary", "arbitrary")),
    )(*args)


def postnorm_fwd(x, y0, g, mod, i_gate, coef, nb, name, target=None, ts=512):
    t, dx = x.shape
    with_loss = target is not None
    ts = min(ts if with_loss else 2 * ts, t // nb)
    ns = t // nb // ts

    def body(*refs):
        x_ref, y_ref, g_ref, mod_ref = refs[:4]
        yv = y_ref[...]
        r = lax.rsqrt(jnp.mean(yv * yv, axis=-1, keepdims=True) + EPS)
        out = x_ref[...] + (coef * mod_ref[0, i_gate:i_gate + 1, :]) * (yv * r * g_ref[...])
        if not with_loss:
            refs[4][...] = out
            return
        t_ref, dx_ref, loss_ref = refs[4:]
        b, s = pl.program_id(0), pl.program_id(1)
        err = out - t_ref[...]
        dx_ref[...] = err * (1.0 / dx)
        part = (0.5 / dx) * jnp.sum(jnp.sum(err * err, axis=1, keepdims=True), axis=0, keepdims=True)
        first = jnp.logical_and(b == 0, s == 0)

        @pl.when(first)
        def _():
            loss_ref[...] = part

        @pl.when(jnp.logical_not(first))
        def _():
            loss_ref[...] += part

    row = _row_specs(dx, ts, ns)
    in_specs = [row, row, _vec_spec(dx), _mod_spec()]
    args = [x, y0, g, mod]
    out_specs = row
    out_shape = jax.ShapeDtypeStruct((t, dx), F32)
    if with_loss:
        in_specs.append(row)
        args.append(target)
        out_specs = [row, pl.BlockSpec((1, 1), lambda b, s: (0, 0))]
        out_shape = [out_shape, jax.ShapeDtypeStruct((1, 1), F32)]
    return _pcall(
        body, name=name, grid=(nb, ns), in_specs=in_specs, out_specs=out_specs, out_shape=out_shape,
        compiler_params=_params(("arbitrary", "arbitrary")),
    )(*args)


def postnorm_bwd(dxo, y0, g, mod, i_gate, coef, nb, name, ts=1024):
    t, dx = y0.shape
    ts = min(ts, t // nb)
    ns = t // nb // ts

    def body(d_ref, y_ref, g_ref, mod_ref, dy_ref, dg_ref, dgate_ref):
        b, s = pl.program_id(0), pl.program_id(1)
        yv = y_ref[...]
        dv = d_ref[...]
        gv = g_ref[...]
        r = lax.rsqrt(jnp.mean(yv * yv, axis=-1, keepdims=True) + EPS)
        yhat = yv * r
        dgate = jnp.sum(dv * (coef * (yhat * gv)), axis=0, keepdims=True)[None]
        dyn = dv * (coef * mod_ref[0, i_gate:i_gate + 1, :])
        dg = jnp.sum(dyn * yhat, axis=0, keepdims=True)
        dyhat = dyn * gv
        dy_ref[...] = (r * (dyhat - yhat * jnp.mean(dyhat * yhat, axis=-1, keepdims=True))).astype(BF16)

        @pl.when(s == 0)
        def _():
            dgate_ref[...] = dgate

        @pl.when(s > 0)
        def _():
            dgate_ref[...] += dgate

        first = jnp.logical_and(b == 0, s == 0)

        @pl.when(first)
        def _():
            dg_ref[...] = dg

        @pl.when(jnp.logical_not(first))
        def _():
            dg_ref[...] += dg

    row = _row_specs(dx, ts, ns)
    return _pcall(
        body, name=name, grid=(nb, ns), in_specs=[row, row, _vec_spec(dx), _mod_spec()],
        out_specs=[row, _vec_spec(dx), pl.BlockSpec((1, 1, dx), lambda b, s: (b, 0, 0))],
        out_shape=[jax.ShapeDtypeStruct((t, dx), BF16), jax.ShapeDtypeStruct((1, dx), F32),
                   jax.ShapeDtypeStruct((nb, 1, dx), F32)],
        compiler_params=_params(("arbitrary", "arbitrary")),
    )(dxo, y0, g, mod)


def rope_tables(positions):
    inv_freq = ROPE_THETA ** (-jnp.arange(0, ROT_DIM, 2, dtype=F32) / ROT_DIM)
    ang = positions.astype(F32).reshape(-1, 1) * inv_freq
    cos, sin = jnp.cos(ang), jnp.sin(ang)
    half = ROT_DIM // 2
    z = lambda n: jnp.zeros((ang.shape[0], n), F32)
    c = jnp.concatenate([cos, cos, jnp.ones((ang.shape[0], HEAD_DIM - ROT_DIM), F32)], axis=1)
    sp = jnp.concatenate([z(half), sin, z(HEAD_DIM - ROT_DIM)], axis=1)
    sm = jnp.concatenate([-sin, z(HEAD_DIM - half)], axis=1)
    return tuple(jnp.tile(a, (1, HEADS_PER_STEP)) for a in (c, sp, sm))


def rope(xarr, col_block, width, tables, transpose, out_dtype, name, ts=1024):
    t = xarr.shape[0]
    ts = min(ts, t)
    half = ROT_DIM // 2
    reps = width // LANES

    def body(x_ref, c_ref, sp_ref, sm_ref, o_ref):
        xv = x_ref[...].astype(F32)
        wide = lambda r: jnp.concatenate([r[...]] * reps, axis=1)
        c, sp, sm = wide(c_ref), wide(sp_ref), wide(sm_ref)
        if transpose:
            out = xv * c + pltpu.roll(xv * sp, width - half, 1) + pltpu.roll(xv * sm, half, 1)
        else:
            out = xv * c + pltpu.roll(xv, half, 1) * sp + pltpu.roll(xv, width - half, 1) * sm
        o_ref[...] = out.astype(o_ref.dtype)

    tab = pl.BlockSpec((ts, LANES), lambda i: (i, 0))
    return _pcall(
        body, name=name, grid=(t // ts,),
        in_specs=[pl.BlockSpec((ts, width), lambda i: (i, col_block)), tab, tab, tab],
        out_specs=pl.BlockSpec((ts, width), lambda i: (i, 0)),
        out_shape=jax.ShapeDtypeStruct((t, width), out_dtype), compiler_params=_params(("arbitrary",)),
    )(xarr, *tables)


def _scan_lanes(x, reverse):
    n = x.shape[-1]
    lane = lax.broadcasted_iota(jnp.int32, x.shape, x.ndim - 1)
    k = 1
    while k < n:
        if reverse:
            x = x + jnp.where(lane < n - k, pltpu.roll(x, n - k, x.ndim - 1), 0.0)
        else:
            x = x + jnp.where(lane >= k, pltpu.roll(x, k, x.ndim - 1), 0.0)
        k *= 2
    return x


def _log_sigmoid(z):
    return jnp.minimum(z, 0.0) - jnp.log(1.0 + jnp.exp(-jnp.abs(z)))


def fox_gate_fwd(ft, b_forget, name):
    nb, nh, s = ft.shape

    def body(f_ref, b_ref, o_ref):
        z = f_ref[0] + b_ref[...]
        o_ref[0] = -_scan_lanes(_log_sigmoid(z), False)

    spec = pl.BlockSpec((1, nh, s), lambda b: (b, 0, 0))
    return _pcall(
        body, name=name, grid=(nb,), in_specs=[spec, pl.BlockSpec((nh, 1), lambda b: (0, 0))], out_specs=spec,
        out_shape=jax.ShapeDtypeStruct((nb, nh, s), F32), compiler_params=_params(("arbitrary",)),
    )(ft, b_forget)


def fox_gate_bwd(dcb, drow, ft, b_forget, name):
    nb, nh, s = ft.shape

    def body(d_ref, r_ref, f_ref, b_ref, dz_ref, db_ref):
        b = pl.program_id(0)
        z = f_ref[0] + b_ref[...]
        dlf = _scan_lanes(r_ref[0] - d_ref[0], True)
        dz = dlf * _sigmoid(-z)
        dz_ref[0] = dz
        db = jnp.sum(dz, axis=1, keepdims=True)

        @pl.when(b == 0)
        def _():
            db_ref[...] = db

        @pl.when(b > 0)
        def _():
            db_ref[...] += db

    spec = pl.BlockSpec((1, nh, s), lambda b: (b, 0, 0))
    vec = pl.BlockSpec((nh, 1), lambda b: (0, 0))
    return _pcall(
        body, name=name, grid=(nb,), in_specs=[spec, spec, spec, vec], out_specs=[spec, vec],
        out_shape=[jax.ShapeDtypeStruct((nb, nh, s), F32), jax.ShapeDtypeStruct((nh, 1), F32)],
        compiler_params=_params(("arbitrary",)),
    )(dcb, drow, ft, b_forget)


ATTN_TQ = 512
ATTN_TK = 512
ONES_ROWS = 16


def _block_delta(s, tq, tk):
    off = jnp.arange(s // tk) - (tq // tk - 1)
    return off[:, None, None] * tk + jnp.arange(tq)[None, None, :] - jnp.arange(tk)[None, :, None]


def dilated_table(s, tq, tk):
    delta = _block_delta(s, tq, tk)
    count = jnp.zeros(delta.shape, F32)
    for window, dil in DILATED_PATTERNS:
        count = count + ((delta >= 0) & (delta <= window) & (delta % dil == 0)).astype(F32)
    return jnp.where(count > 0, jnp.log(jnp.maximum(count, 1.0)), NEG)


def causal_table(s, tq, tk):
    return jnp.where(_block_delta(s, tq, tk) >= 0, 0.0, NEG).astype(F32)


def attn_fwd(q_arr, q_off, k_arr, k_off, v_arr, v_off, table, colbias, nb, name, side=None):
    t = q_arr.shape[0]
    s = t // nb
    tk, tq = table.shape[1:]
    nq, nk, r = s // tq, s // tk, tq // tk
    npairs = WIDTH_A // LANES
    use_cb = colbias is not None

    def body(*refs):
        refs = list(refs)
        q_ref, k_ref, v_ref, tab_ref = refs[:4]
        cb_ref = refs[4] if use_cb else None
        o_ref, lse_ref, vt_s = refs[-3 - HEADS_PER_STEP:-HEADS_PER_STEP]
        acc_s = refs[-HEADS_PER_STEP:]
        qi = pl.program_id(2)

        heads = [slice(h * HEAD_DIM, (h + 1) * HEAD_DIM) for h in range(HEADS_PER_STEP)]

        @pl.when(qi == 0)
        def _():
            for cblk in range(nk):
                vt = v_ref[cblk * tk:(cblk + 1) * tk, :].astype(F32).T.astype(BF16)
                for h, hs in enumerate(heads):
                    vt_s[cblk, h, 0:HEAD_DIM, :] = vt[hs, :]
                    vt_s[cblk, h, HEAD_DIM:, :] = jnp.ones((ONES_ROWS, tk), BF16)

        qt_all = (q_ref[...].astype(F32) * ATTN_SCALE).T.astype(BF16)
        qts = [qt_all[hs, :] for hs in heads]
        for a in acc_s:
            a[...] = jnp.zeros_like(a)

        def step(kb, carry):
            ks = pl.multiple_of(kb * tk, tk)
            tab = tab_ref[qi * r + (r - 1) - kb]
            sts = []
            for h, hs in enumerate(heads):
                st = jnp.dot(k_ref[pl.ds(ks, tk), hs], qts[h], preferred_element_type=F32) + tab
                if use_cb:
                    st = st + cb_ref[0, h, pl.ds(ks, tk), :]
                sts.append(st)
            m_new = [jnp.maximum(carry[h], jnp.max(sts[h], axis=0, keepdims=True)) for h in range(HEADS_PER_STEP)]
            for h in range(HEADS_PER_STEP):
                pt = jnp.exp(sts[h] - m_new[h]).astype(BF16)
                acc_s[h][...] = (jnp.exp(carry[h] - m_new[h]) * acc_s[h][...]
                                 + jnp.dot(vt_s[kb, h], pt, preferred_element_type=F32))
            return tuple(m_new)

        fin = lax.fori_loop(0, (qi + 1) * r, step, tuple(jnp.full((1, tq), NEG, F32) for _ in heads))
        outs = []
        for h in range(HEADS_PER_STEP):
            l = acc_s[h][HEAD_DIM:HEAD_DIM + 1, :]
            outs.append(acc_s[h][0:HEAD_DIM, :] / l)
            lse_ref[0, h, 0] = fin[h] + jnp.log(l)
        o_ref[...] = jnp.concatenate(outs, axis=0).T

    def seq_spec(off):
        return pl.BlockSpec((s, LANES), lambda b, j, i: (b, off + j))

    in_specs = [pl.BlockSpec((tq, LANES), lambda b, j, i: (b * nq + i, q_off + j)), seq_spec(k_off), seq_spec(v_off),
                pl.BlockSpec(table.shape, lambda b, j, i: (0, 0, 0))]
    args = [q_arr, k_arr, v_arr, table]
    if use_cb:
        in_specs.append(pl.BlockSpec((1, HEADS_PER_STEP, s, 1), lambda b, j, i: (b, j, 0, 0)))
        args.append(colbias)
    n_heads = npairs * HEADS_PER_STEP
    return _pcall(
        body, side=side, name=name, grid=(nb, npairs, nq), in_specs=in_specs,
        out_specs=[pl.BlockSpec((tq, LANES), lambda b, j, i: (b * nq + i, j)),
                   pl.BlockSpec((1, HEADS_PER_STEP, 1, 1, tq), lambda b, j, i: (b, j, i, 0, 0))],
        out_shape=[jax.ShapeDtypeStruct((t, npairs * LANES), F32), jax.ShapeDtypeStruct((nb, n_heads, nq, 1, tq), F32)],
        scratch_shapes=[pltpu.VMEM((nk, HEADS_PER_STEP, HEAD_DIM + ONES_ROWS, tk), BF16)]
        + [pltpu.VMEM((HEAD_DIM + ONES_ROWS, tq), F32)] * HEADS_PER_STEP,
        compiler_params=_params(("arbitrary", "arbitrary", "arbitrary")),
    )(*args)


def attn_bwd(q_arr, q_off, k_arr, k_off, v_arr, v_off, o_arr, lse_arr, do_arr, table, colbias, nb, qk_dtype, name,
             side=None):
    t = q_arr.shape[0]
    s = t // nb
    tk, tq = table.shape[1:]
    nq, nk, r = s // tq, s // tk, tq // tk
    npairs = WIDTH_A // LANES
    use_cb = colbias is not None

    def body(*refs):
        refs = list(refs)
        q_ref, k_ref, v_ref, o_ref, lse_ref, do_ref, tab_ref = refs[:7]
        pos = 7
        cb_ref = None
        if use_cb:
            cb_ref = refs[pos]
            pos += 1
        dq_ref, dk_ref, dv_ref = refs[pos:pos + 3]
        pos += 3
        dcb_ref = drow_ref = None
        if use_cb:
            dcb_ref, drow_ref = refs[pos:pos + 2]
            pos += 2
        kt_s, dkt_s, dvt_s = refs[pos:pos + 3]
        dqt_s = refs[pos + 3:pos + 3 + HEADS_PER_STEP]
        dcb_s = refs[pos + 3 + HEADS_PER_STEP] if use_cb else None

        heads = [slice(h * HEAD_DIM, (h + 1) * HEAD_DIM) for h in range(HEADS_PER_STEP)]
        for cblk in range(nk):
            kt_s[cblk] = k_ref[cblk * tk:(cblk + 1) * tk, :].astype(F32).T.astype(BF16)
        dkt_s[...] = jnp.zeros_like(dkt_s)
        dvt_s[...] = jnp.zeros_like(dvt_s)
        if use_cb:
            dcb_s[...] = jnp.zeros_like(dcb_s)
        ones = jnp.ones((8, HEAD_DIM), BF16)

        def q_loop(qi, carry):
            qs = pl.multiple_of(qi * tq, tq)
            q_all = (q_ref[pl.ds(qs, tq), :].astype(F32) * ATTN_SCALE)
            do_all = do_ref[pl.ds(qs, tq), :]
            qt_all = q_all.T.astype(BF16)
            dot_all = do_all.T.astype(BF16)
            qt, dot, lse, dsum = [], [], [], []
            for h, hs in enumerate(heads):
                qt.append(qt_all[hs, :])
                dot.append(dot_all[hs, :])
                lse.append(lse_ref[0, h, qi])
                prod = do_all[:, hs] * o_ref[pl.ds(qs, tq), hs]
                hi = prod.astype(BF16)
                lo = (prod - hi.astype(F32)).astype(BF16)
                dsum.append((lax.dot_general(ones, hi, NT_DIMS, preferred_element_type=F32)
                             + lax.dot_general(ones, lo, NT_DIMS, preferred_element_type=F32))[0:1, :])
            for a in dqt_s:
                a[...] = jnp.zeros_like(a)

            def k_loop(kb, drow):
                ks = pl.multiple_of(kb * tk, tk)
                tab = tab_ref[qi * r + (r - 1) - kb]
                sts, dpts, out = [], [], []
                for h, hs in enumerate(heads):
                    st = jnp.dot(k_ref[pl.ds(ks, tk), hs], qt[h], preferred_element_type=F32) + tab
                    if use_cb:
                        st = st + cb_ref[0, h, pl.ds(ks, tk), :]
                    sts.append(st)
                    dpts.append(jnp.dot(v_ref[pl.ds(ks, tk), hs], dot[h], preferred_element_type=F32))
                for h, hs in enumerate(heads):
                    pt = jnp.exp(sts[h] - lse[h])
                    dst = pt * (dpts[h] - dsum[h])
                    dst_b = dst.astype(BF16)
                    dvt_s[h, kb] += lax.dot_general(dot[h], pt.astype(BF16), NT_DIMS, preferred_element_type=F32)
                    dkt_s[h, kb] += lax.dot_general(qt[h], dst_b, NT_DIMS, preferred_element_type=F32)
                    dqt_s[h][...] += jnp.dot(kt_s[kb, hs, :], dst_b, preferred_element_type=F32)
                    if use_cb:
                        dcb_s[h, pl.ds(ks, tk), :] += jnp.sum(dst, axis=1, keepdims=True)
                        out.append(drow[h] + jnp.sum(dst, axis=0, keepdims=True))
                    else:
                        out.append(drow[h])
                return tuple(out)

            drow = lax.fori_loop(0, (qi + 1) * r, k_loop, tuple(jnp.zeros((1, tq), F32) for _ in heads))
            dqt = jnp.concatenate([a[...] for a in dqt_s], axis=0)
            dq_ref[pl.ds(qs, tq), :] = (dqt * ATTN_SCALE).T.astype(dq_ref.dtype)
            if use_cb:
                for h in range(HEADS_PER_STEP):
                    drow_ref[0, h, qi] = drow[h]
            return carry

        lax.fori_loop(0, nq, q_loop, 0)
        for cblk in range(nk):
            rows = slice(cblk * tk, (cblk + 1) * tk)
            dk_ref[rows, :] = jnp.concatenate([dkt_s[h, cblk] for h in range(HEADS_PER_STEP)], axis=0).T.astype(dk_ref.dtype)
            dv_ref[rows, :] = jnp.concatenate([dvt_s[h, cblk] for h in range(HEADS_PER_STEP)], axis=0).T.astype(dv_ref.dtype)
        if use_cb:
            for h in range(HEADS_PER_STEP):
                dcb_ref[0, h] = dcb_s[h]

    def seq_spec(off):
        return pl.BlockSpec((s, LANES), lambda b, j: (b, off + j))

    row_spec = pl.BlockSpec((1, HEADS_PER_STEP, nq, 1, tq), lambda b, j: (b, j, 0, 0, 0))
    in_specs = [seq_spec(q_off), seq_spec(k_off), seq_spec(v_off), seq_spec(0), row_spec, seq_spec(0),
                pl.BlockSpec(table.shape, lambda b, j: (0, 0, 0))]
    args = [q_arr, k_arr, v_arr, o_arr, lse_arr, do_arr, table]
    width = npairs * LANES
    out_specs = [seq_spec(0)] * 3
    out_shape = [jax.ShapeDtypeStruct((t, width), qk_dtype), jax.ShapeDtypeStruct((t, width), qk_dtype),
                 jax.ShapeDtypeStruct((t, width), BF16)]
    scratch = [pltpu.VMEM((nk, LANES, tk), BF16), pltpu.VMEM((HEADS_PER_STEP, nk, HEAD_DIM, tk), F32),
               pltpu.VMEM((HEADS_PER_STEP, nk, HEAD_DIM, tk), F32)] + [pltpu.VMEM((HEAD_DIM, tq), F32)] * HEADS_PER_STEP
    if use_cb:
        cb_spec = pl.BlockSpec((1, HEADS_PER_STEP, s, 1), lambda b, j: (b, j, 0, 0))
        in_specs.append(cb_spec)
        args.append(colbias)
        out_specs += [cb_spec, row_spec]
        out_shape += [jax.ShapeDtypeStruct(colbias.shape, F32), jax.ShapeDtypeStruct(lse_arr.shape, F32)]
        scratch.append(pltpu.VMEM((HEADS_PER_STEP, s, 1), F32))
    return _pcall(
        body, side=side, name=name, grid=(nb, npairs), in_specs=in_specs, out_specs=out_specs, out_shape=out_shape,
        scratch_shapes=scratch, compiler_params=_params(("arbitrary", "arbitrary")),
    )(*args)


def ada_fwd(c_all, w_ada, b_cols, name):
    def body(c_ref, w_ref, b_ref, o_ref):
        cv = c_ref[...]
        sc = (cv * _sigmoid(cv)).astype(BF16)
        o_ref[...] = jnp.dot(sc, w_ref[...].astype(BF16), preferred_element_type=F32) + b_ref[...]

    return _pcall(body, name=name, out_shape=jax.ShapeDtypeStruct((c_all.shape[0], w_ada.shape[1]), F32),
                  compiler_params=_params())(c_all, w_ada, b_cols)


def ada_bwd(c_all, dmod_cols, name):
    def body(c_ref, d_ref, o_ref):
        cv = c_ref[...]
        sc = (cv * _sigmoid(cv)).astype(BF16)
        o_ref[...] = lax.dot_general(sc, d_ref[...].astype(BF16), TN_DIMS, preferred_element_type=F32)

    return _pcall(body, name=name, out_shape=jax.ShapeDtypeStruct((c_all.shape[1], dmod_cols.shape[1]), F32),
                  compiler_params=_params())(c_all, dmod_cols)


def adamw(parts, group, w, m, v, name, tr=None):
    n = parts.shape[0]
    r, c = w.shape
    tr = r if tr is None else tr
    c1 = 1.0 - ADAM_B1 ** ADAM_STEP
    c2 = 1.0 - ADAM_B2 ** ADAM_STEP

    def body(p_ref, w_ref, m_ref, v_ref, g_ref, d_ref, nm_ref, nv_ref):
        g = p_ref[0, 0].astype(F32)
        for i in range(1, n):
            g = g + p_ref[i, 0].astype(F32)
        wv = w_ref[...]
        nm = ADAM_B1 * m_ref[...] + (1.0 - ADAM_B1) * g
        nv = ADAM_B2 * v_ref[...] + (1.0 - ADAM_B2) * (g * g)
        g_ref[...] = g
        nm_ref[...] = nm
        nv_ref[...] = nv
        d_ref[...] = -ADAM_LR * ((nm / c1) / (jnp.sqrt(nv / c2) + ADAM_EPS) + ADAM_WD * wv)

    spec = pl.BlockSpec((tr, c), lambda i: (i, 0))
    shape = jax.ShapeDtypeStruct((r, c), F32)
    return _pcall(
        body, name=name, grid=(r // tr,),
        in_specs=[pl.BlockSpec((n, 1, tr, c), lambda i: (0, group, i, 0)), spec, spec, spec],
        out_specs=[spec] * 4, out_shape=[shape] * 4, compiler_params=_params(("arbitrary",)),
    )(parts, w, m, v)


def all_gather(arrs, name):
    n = len(arrs)
    hbm = pl.BlockSpec(memory_space=pl.ANY)

    def body(*refs):
        ins, outs = refs[:n], refs[n:2 * n]
        send_sems, recv_sems, local_sems = refs[2 * n:]
        x, y, c = _place()
        me, sibling = (x, y, c), (x, y, 1 - c)
        chips = [(1 - x, y), (x, 1 - y), (1 - x, 1 - y)]

        def copy(a, k, block, to, src=None):
            dst = outs[a].at[_slot(block)]
            return pltpu.make_async_remote_copy(
                src_ref=dst if src is None else src, dst_ref=dst, send_sem=send_sems.at[a * 7 + k],
                recv_sem=recv_sems.at[a * 7 + k], device_id=to, device_id_type=MESH)

        mine = [pltpu.make_async_copy(ins[a], outs[a].at[_slot(me)], local_sems.at[a]) for a in range(n)]
        for cp in mine:
            cp.start()
        first = []
        for a in range(n):
            first.append(copy(a, 0, me, sibling, src=ins[a]))
            first += [copy(a, 1 + j, me, (*chip, c), src=ins[a]) for j, chip in enumerate(chips)]
        for cp in first:
            cp.start()
        passed = []
        for a in range(n):
            for j, chip in enumerate(chips):
                copy(a, 1 + j, (*chip, c), me).wait_recv()
                cp = copy(a, 4 + j, (*chip, c), sibling)
                cp.start()
                passed.append(cp)
        for a in range(n):
            copy(a, 0, sibling, me).wait_recv()
            for j, chip in enumerate(chips):
                copy(a, 4 + j, (*chip, 1 - c), me).wait_recv()
        for cp in first + passed:
            cp.wait_send()
        for cp in mine:
            cp.wait()

    return _pcall(
        body, name=name, in_specs=[hbm] * n, out_specs=[hbm] * n,
        out_shape=[jax.ShapeDtypeStruct((N_DEV,) + a.shape, a.dtype) for a in arrs],
        scratch_shapes=[pltpu.SemaphoreType.DMA((7 * n,)), pltpu.SemaphoreType.DMA((7 * n,)),
                        pltpu.SemaphoreType.DMA((n,))],
        compiler_params=pltpu.CompilerParams(has_side_effects=True),
    )(*arrs)


def _t(w):
    return jnp.swapaxes(w, -1, -2)


def _rows_from_blocks(blocks, pad_to=None):
    full = blocks.reshape(-1, blocks.shape[2])
    if pad_to is not None and pad_to > full.shape[0]:
        full = jnp.pad(full, ((0, pad_to - full.shape[0]), (0, 0)))
    return full


def _rows_to_blocks(full, nrows):
    return full[:nrows].reshape(N_DEV, nrows // N_DEV, full.shape[1])


SMALL_ORDER = ("g_pre_ff1", "g_post_ff1", "g_pre_mix", "g_post_mix", "g_out_a", "g_out_b", "g_pre_ff2", "g_post_ff2",
               "b_forget")


def _pack_small(vals):
    rows = []
    for name in SMALL_ORDER:
        v = vals[name].reshape(1, -1)
        if v.shape[1] % LANES:
            v = jnp.pad(v, ((0, 0), (0, LANES - v.shape[1] % LANES)))
        rows.append(v)
    return jnp.concatenate(rows, axis=1)


def _unpack_small(row, sizes):
    out, pos = {}, 0
    for name in SMALL_ORDER:
        n = sizes[name]
        out[name] = row[:, pos:pos + n]
        pos += -(-n // LANES) * LANES
    return out


def _ffn_forward(x, mod, g_pre, g_post, wg, wu, wd, i0, nb, tag, target=None, side=None):
    h = prenorm_fwd(x, g_pre, mod, i0, i0 + 1, nb, f"{tag}_prenorm")
    res, side_out = ffn_up(h, wg, wu, f"{tag}_up", side=side), None
    if side is not None:
        res, side_out = res
    gate, up, act = res
    y0 = mm_rows([(act, wd)], False, F32, f"{tag}_down")
    out = postnorm_fwd(x, y0, g_post, mod, i0 + 2, 0.5, nb, f"{tag}_postnorm", target=target)
    return out, (x, h, gate, up, act, y0), side_out


def _ffn_backward(dxo, saved, mod, g_pre, g_post, wg, wu, wd, i0, nb, tag, side=None, chain=False):
    x, h, gate, up, act, y0 = saved
    dy0, dg_post, dgate_mod = postnorm_bwd(dxo, y0, g_post, mod, i0 + 2, 0.5, nb, f"{tag}_postnorm_bwd")
    dwd = mm_tn(act, dy0, BF16, f"{tag}_dwd")
    res, side_out = ffn_down_bwd(dy0, wd, gate, up, f"{tag}_down_bwd", side=side), None
    if side is not None:
        res, side_out = res
    dgate, dup = res
    dh_pairs = [(dgate, wg), (dup, wu)]
    if chain:
        dwg, (dwd,) = mm_tn(dgate, h, BF16, f"{tag}_dwg", side=([_rows_to_blocks(dwd, D_FF)[:, None]], False))
        dwu, (dwg,) = mm_tn(dup, h, BF16, f"{tag}_dwu", side=([_rows_to_blocks(dwg, D_FF)[:, None]], False))
        (dx, dg_pre, dsc, dsh), (dwu,) = prenorm_bwd(dh_pairs, x, g_pre, mod, i0 + 1, dxo, nb, f"{tag}_dh_prenorm_bwd",
                                                     ts=DH_ROWS, side=([_rows_to_blocks(dwu, D_FF)[:, None]], False))
    else:
        dwg = mm_tn(dgate, h, BF16, f"{tag}_dwg")
        dwu = mm_tn(dup, h, BF16, f"{tag}_dwu")
        dx, dg_pre, dsc, dsh = prenorm_bwd(dh_pairs, x, g_pre, mod, i0 + 1, dxo, nb, f"{tag}_dh_prenorm_bwd", ts=DH_ROWS)
    return dx, dict(g_pre=dg_pre, g_post=dg_post, wg=dwg, wu=dwu, wd=dwd, mod=(dsh, dsc, dgate_mod)), side_out


def kernel(x, c, positions, w_ada, b_ada, g_pre_ff1, g_post_ff1, w_ff1_gate, w_ff1_up, w_ff1_down, g_pre_mix, g_post_mix, w_in, b_forget, g_out_a, g_out_b, w_out, g_pre_ff2, g_post_ff2, w_ff2_gate, w_ff2_up, w_ff2_down, loss_target, m_w_ada, m_b_ada, m_g_pre_ff1, m_g_post_ff1, m_w_ff1_gate, m_w_ff1_up, m_w_ff1_down, m_g_pre_mix, m_g_post_mix, m_w_in, m_b_forget, m_g_out_a, m_g_out_b, m_w_out, m_g_pre_ff2, m_g_post_ff2, m_w_ff2_gate, m_w_ff2_up, m_w_ff2_down, v_w_ada, v_b_ada, v_g_pre_ff1, v_g_post_ff1, v_w_ff1_gate, v_w_ff1_up, v_w_ff1_down, v_g_pre_mix, v_g_post_mix, v_w_in, v_b_forget, v_g_out_a, v_g_out_b, v_w_out, v_g_pre_ff2, v_g_post_ff2, v_w_ff2_gate, v_w_ff2_up, v_w_ff2_down):
    weights = dict(w_ada=w_ada, b_ada=b_ada, g_pre_ff1=g_pre_ff1, g_post_ff1=g_post_ff1, w_ff1_gate=w_ff1_gate,
                   w_ff1_up=w_ff1_up, w_ff1_down=w_ff1_down, g_pre_mix=g_pre_mix, g_post_mix=g_post_mix, w_in=w_in,
                   b_forget=b_forget, g_out_a=g_out_a, g_out_b=g_out_b, w_out=w_out, g_pre_ff2=g_pre_ff2,
                   g_post_ff2=g_post_ff2, w_ff2_gate=w_ff2_gate, w_ff2_up=w_ff2_up, w_ff2_down=w_ff2_down)
    mom_m = dict(w_ada=m_w_ada, b_ada=m_b_ada, g_pre_ff1=m_g_pre_ff1, g_post_ff1=m_g_post_ff1, w_ff1_gate=m_w_ff1_gate,
                 w_ff1_up=m_w_ff1_up, w_ff1_down=m_w_ff1_down, g_pre_mix=m_g_pre_mix, g_post_mix=m_g_post_mix,
                 w_in=m_w_in, b_forget=m_b_forget, g_out_a=m_g_out_a, g_out_b=m_g_out_b, w_out=m_w_out,
                 g_pre_ff2=m_g_pre_ff2, g_post_ff2=m_g_post_ff2, w_ff2_gate=m_w_ff2_gate, w_ff2_up=m_w_ff2_up,
                 w_ff2_down=m_w_ff2_down)
    mom_v = dict(w_ada=v_w_ada, b_ada=v_b_ada, g_pre_ff1=v_g_pre_ff1, g_post_ff1=v_g_post_ff1, w_ff1_gate=v_w_ff1_gate,
                 w_ff1_up=v_w_ff1_up, w_ff1_down=v_w_ff1_down, g_pre_mix=v_g_pre_mix, g_post_mix=v_g_post_mix,
                 w_in=v_w_in, b_forget=v_b_forget, g_out_a=v_g_out_a, g_out_b=v_g_out_b, w_out=v_w_out,
                 g_pre_ff2=v_g_pre_ff2, g_post_ff2=v_g_post_ff2, w_ff2_gate=v_w_ff2_gate, w_ff2_up=v_w_ff2_up,
                 w_ff2_down=v_w_ff2_down)
    order = list(weights)

    nb, s, d = x.shape
    t = nb * s
    me = _slot(_place())
    nbg = nb * N_DEV
    ada_cols = w_ada.shape[2]

    bf = lambda w: w[0].astype(BF16)
    bft = lambda w: _t(w)[0].astype(BF16)
    c_all, ff1_all = all_gather([c, jnp.stack([bft(w_ff1_gate), bft(w_ff1_up), bf(w_ff1_down)])], "gather_ff1")
    c_all = c_all.reshape(nbg, d)
    wg1, wu1, wd1 = (_rows_from_blocks(ff1_all[:, i], D_FF_PAD) for i in range(3))

    b_cols = lax.dynamic_slice(b_ada, (0, me * ada_cols), (1, ada_cols))
    mod_cols = ada_fwd(c_all, w_ada[0], b_cols, "ada_fwd")
    (mod_all,) = all_gather([mod_cols], "gather_mod")
    mod = lax.dynamic_slice(mod_all, (0, me * nb, 0), (N_DEV, nb, ada_cols))
    mod = mod.transpose(1, 0, 2).reshape(nb, N_MOD, d)

    xf = x.reshape(t, d)
    target = loss_target.reshape(t, d)

    x1, saved1, (w_in_all, w_out_all) = _ffn_forward(xf, mod, g_pre_ff1, g_post_ff1, wg1, wu1, wd1, 0, nb, "ff1",
                                                     side=([bft(w_in), bf(w_out)], True))
    w_in_t = _rows_from_blocks(w_in_all)
    n_qkv = 3 * (WIDTH_A + WIDTH_B)
    w_qkv_t = w_in_t[:n_qkv]
    w_f_t = jnp.pad(w_in_t[n_qkv:], ((0, LANES - N_HEADS_B), (0, 0)))
    w_o = _rows_from_blocks(w_out_all)
    w_o_a, w_o_b = w_o[:WIDTH_A], w_o[WIDTH_A:]

    h2 = prenorm_fwd(x1, g_pre_mix, mod, 3, 4, nb, "mix_prenorm")
    proj = mm_rows([(h2, w_qkv_t)], True, BF16, "mix_proj")
    f_logit = mm_rows([(h2, w_f_t)], True, F32, "mix_forget")
    tables = rope_tables(positions)
    qk_rot = rope(proj, 0, 2 * WIDTH_A, tables, False, BF16, "rope")
    tab_a = dilated_table(s, ATTN_TQ, ATTN_TK)
    tab_b = causal_table(s, ATTN_TQ, ATTN_TK)
    ft = f_logit[:, :N_HEADS_B].reshape(nb, s, N_HEADS_B).transpose(0, 2, 1)
    bf_col = b_forget.reshape(N_HEADS_B, 1)
    colbias = fox_gate_fwd(ft, bf_col, "fox_gate").reshape(nb, N_HEADS_B, s, 1)
    pa = WIDTH_A // LANES
    (o_a, lse_a), (ff2_all,) = attn_fwd(
        qk_rot, 0, qk_rot, pa, proj, 2 * pa, tab_a, None, nb, "attn_a",
        side=([jnp.stack([bft(w_ff2_gate), bft(w_ff2_up), bf(w_ff2_down)])], True))
    wg2, wu2, wd2 = (_rows_from_blocks(ff2_all[:, i], D_FF_PAD) for i in range(3))
    o_b, lse_b = attn_fwd(proj, 3 * pa, proj, 4 * pa, proj, 5 * pa, tab_b, colbias, nb, "attn_b")
    m_a = prenorm_fwd(o_a, g_out_a, None, None, None, nb, "out_norm_a")
    m_b = prenorm_fwd(o_b, g_out_b, None, None, None, nb, "out_norm_b")
    y0m = mm_rows([(m_a, w_o_a), (m_b, w_o_b)], False, F32, "mix_out")
    x2 = postnorm_fwd(x1, y0m, g_post_mix, mod, 5, 1.0, nb, "mix_postnorm")

    (dx3, loss_part), saved2, _ = _ffn_forward(x2, mod, g_pre_ff2, g_post_ff2, wg2, wu2, wd2, 6, nb, "ff2", target=target)
    loss = lax.psum(loss_part[0, 0], ("x", "y", "c"))

    dx2, gr2, _ = _ffn_backward(dx3, saved2, mod, g_pre_ff2, g_post_ff2, wg2, wu2, wd2, 6, nb, "ff2")
    ff2_blocks = [jnp.stack([_rows_to_blocks(gr2[k], D_FF) for k in ("wg", "wu", "wd")], axis=1)]

    dy0m, dg_post_mix, dgate_mix = postnorm_bwd(dx2, y0m, g_post_mix, mod, 5, 1.0, nb, "mix_postnorm_bwd")
    dw_o_a = mm_tn(m_a, dy0m, BF16, "mix_dwo_a")
    dw_o_b = mm_tn(m_b, dy0m, BF16, "mix_dwo_b")
    dm_a = mm_rows([(dy0m, w_o_a)], True, F32, "mix_dm_a")
    dm_b = mm_rows([(dy0m, w_o_b)], True, F32, "mix_dm_b")
    do_a, dg_out_a = prenorm_bwd(dm_a, o_a, g_out_a, None, None, None, nb, "out_norm_a_bwd")
    do_b, dg_out_b = prenorm_bwd(dm_b, o_b, g_out_b, None, None, None, nb, "out_norm_b_bwd")
    (dq_a, dk_a, dv_a), (g_ff2,) = attn_bwd(qk_rot, 0, qk_rot, pa, proj, 2 * pa, o_a, lse_a, do_a, tab_a, None,
                                                    nb, F32, "attn_a_bwd", side=(ff2_blocks, False))
    dq_b, dk_b, dv_b, dcb, drow = attn_bwd(proj, 3 * pa, proj, 4 * pa, proj, 5 * pa, o_b, lse_b, do_b, tab_b, colbias, nb,
                                     BF16, "attn_b_bwd")
    dq_a = rope(dq_a, 0, WIDTH_A, tables, True, BF16, "rope_bwd_q")
    dk_a = rope(dk_a, 0, WIDTH_A, tables, True, BF16, "rope_bwd_k")
    dz_t, db_forget = fox_gate_bwd(dcb.reshape(nb, N_HEADS_B, s), drow.reshape(nb, N_HEADS_B, s), ft, bf_col,
                                   "fox_gate_bwd")
    dz = jnp.pad(dz_t.transpose(0, 2, 1).reshape(t, N_HEADS_B), ((0, 0), (0, LANES - N_HEADS_B))).astype(BF16)
    pieces = [dq_a, dk_a, dv_a, dq_b, dk_b, dv_b]
    w_pieces = [w_qkv_t[i * WIDTH_A:(i + 1) * WIDTH_A] for i in range(6)]
    dh2_pairs = list(zip(pieces, w_pieces)) + [(dz, w_f_t)]
    dw_in_t = jnp.concatenate([mm_tn(p, h2, BF16, f"mix_dwin_{i}") for i, p in enumerate(pieces)]
                              + [mm_tn(dz, h2, BF16, "mix_dwin_f")[:N_HEADS_B]], axis=0)
    dx1, dg_pre_mix, dsc_mix, dsh_mix = prenorm_bwd(dh2_pairs, x1, g_pre_mix, mod, 4, dx2, nb, "mix_dh_prenorm_bwd",
                                                    ts=DH_ROWS)

    g_in = _rows_to_blocks(dw_in_t, dw_in_t.shape[0])[:, None]
    g_out = _rows_to_blocks(jnp.concatenate([dw_o_a, dw_o_b], axis=0), d)[:, None]
    dx0, gr1, (g_in, g_out) = _ffn_backward(dx1, saved1, mod, g_pre_ff1, g_post_ff1, wg1, wu1, wd1, 0, nb, "ff1",
                                            side=([g_in, g_out], False), chain=True)
    grad_x = dx0.reshape(nb, s, d)

    dmod =jnp.concatenate(list(gr1["mod"]) + [dsh_mix, dsc_mix, dgate_mix] + list(gr2["mod"]), axis=1)
    small = _pack_small(dict(g_pre_ff1=gr1["g_pre"], g_post_ff1=gr1["g_post"], g_pre_mix=dg_pre_mix,
                             g_post_mix=dg_post_mix, g_out_a=dg_out_a, g_out_b=dg_out_b, g_pre_ff2=gr2["g_pre"],
                             g_post_ff2=gr2["g_post"], b_forget=db_forget))
    dmod_all, small_all = all_gather([dmod.reshape(nb, N_MOD * d), small], "gather_small_grads")
    dmod_all = dmod_all.reshape(nbg, N_MOD * d)

    res = {}
    def adamw_t(parts, group, n):
        return tuple(_t(r) for r in adamw(parts, group, _t(weights[n])[0], _t(mom_m[n])[0], _t(mom_v[n])[0], f"adamw_{n}"))

    res["w_ff1_gate"] = adamw_t(gr1["wg"], 0, "w_ff1_gate")
    res["w_ff1_up"] = adamw_t(gr1["wu"], 0, "w_ff1_up")
    res["w_ff2_gate"] = adamw_t(g_ff2, 0, "w_ff2_gate")
    res["w_ff2_up"] = adamw_t(g_ff2, 1, "w_ff2_up")
    res["w_ff1_down"] = adamw(gr1["wd"], 0, w_ff1_down[0], m_w_ff1_down[0], v_w_ff1_down[0], "adamw_ff1_down")
    res["w_ff2_down"] = adamw(g_ff2, 2, w_ff2_down[0], m_w_ff2_down[0], v_w_ff2_down[0], "adamw_ff2_down")
    res["w_in"] = adamw_t(g_in, 0, "w_in")
    res["w_out"] = adamw(g_out, 0, w_out[0], m_w_out[0], v_w_out[0], "adamw_out")
    dmod_cols = lax.dynamic_slice(dmod_all, (0, me * ada_cols), (nbg, ada_cols))
    dw_ada = ada_bwd(c_all, dmod_cols, "ada_bwd")
    res["w_ada"] = adamw(dw_ada[None, None], 0, w_ada[0], m_w_ada[0], v_w_ada[0], "adamw_ada", tr=256)
    res["b_ada"] = adamw(dmod_all[:, None, None], 0, b_ada, m_b_ada, v_b_ada, "adamw_b_ada")
    sizes = {n: weights[n].shape[1] for n in SMALL_ORDER}
    small_res = adamw(small_all[:, None], 0, _pack_small(weights), _pack_small(mom_m), _pack_small(mom_v), "adamw_small")
    small_res = [_unpack_small(r, sizes) for r in small_res]
    for n in SMALL_ORDER:
        res[n] = tuple(r[n] for r in small_res)

    outs = [loss, grad_x]
    for kind in range(4):
        for n in order:
            a = res[n][kind]
            outs.append(a.reshape(weights[n].shape))
    return tuple(outs)
```

```python
import functools

import jax
import jax.numpy as jnp
from jax import lax
from jax.experimental import pallas as pl
from jax.experimental.pallas import tpu as pltpu

F32 = jnp.float32
BF16 = jnp.bfloat16

D_MODEL = 1024
HEAD_DIM = 64
N_HEADS_A = 8
N_HEADS_B = 8
WIDTH_A = N_HEADS_A * HEAD_DIM
WIDTH_B = N_HEADS_B * HEAD_DIM
DILATED_PATTERNS = ((128, 1), (512, 4), (2048, 16))
ROT_DIM = HEAD_DIM // 4
ROPE_THETA = 500000.0
D_FF = 2752
D_FF_PAD = 2816
N_MOD = 9
EPS = 1e-6
ATTN_SCALE = HEAD_DIM ** -0.5
NEG = -1e30
N_DEV = 8
LANES = 128
HEADS_PER_STEP = LANES // HEAD_DIM

ADAM_LR = 0.001
ADAM_B1 = 0.9
ADAM_B2 = 0.999
ADAM_EPS = 1e-08
ADAM_WD = 0.01
ADAM_STEP = 10

VMEM_LIMIT = 56 * 1024 * 1024
MESH = pl.DeviceIdType.MESH

NT_DIMS = (((1,), (1,)), ((), ()))
TN_DIMS = (((0,), (0,)), ((), ()))
NN_DIMS = (((1,), (0,)), ((), ()))


def _place():
    return lax.axis_index("x"), lax.axis_index("y"), lax.axis_index("c")


def _slot(p):
    return 4 * p[0] + 2 * p[1] + p[2]


def _direct_copies(ins, outs, send_sems, recv_sems, local_sems, gather):
    x, y, c = _place()
    me = (x, y, c)
    flip = lambda v, bit: 1 - v if bit else v
    peers = [(flip(x, k & 4), flip(y, k & 2), flip(c, k & 1)) for k in range(1, N_DEV)]
    local, sends, recvs = [], [], []
    for a in range(len(ins)):
        mine = ins[a] if gather else ins[a].at[_slot(me)]
        local.append(pltpu.make_async_copy(mine, outs[a].at[_slot(me)], local_sems.at[a]))
        for k, peer in enumerate(peers):
            sems = dict(send_sem=send_sems.at[a * 7 + k], recv_sem=recv_sems.at[a * 7 + k], device_id=peer,
                        device_id_type=MESH)
            sends.append(pltpu.make_async_remote_copy(
                src_ref=ins[a] if gather else ins[a].at[_slot(peer)], dst_ref=outs[a].at[_slot(me)], **sems))
            recvs.append(pltpu.make_async_remote_copy(src_ref=mine, dst_ref=outs[a].at[_slot(peer)], **sems))
    return local, sends, recvs


def _comm_scratch(n):
    return [pltpu.SemaphoreType.DMA((7 * n,)), pltpu.SemaphoreType.DMA((7 * n,)), pltpu.SemaphoreType.DMA((n,))]


def _pcall(body, side=None, **kw):
    if side is None:
        return pl.pallas_call(body, **kw)
    arrs, gather = side
    n = len(arrs)
    grid = kw["grid"]
    in_specs = list(kw["in_specs"])
    single = not isinstance(kw["out_specs"], (list, tuple))
    out_specs = [kw["out_specs"]] if single else list(kw["out_specs"])
    out_shape = [kw["out_shape"]] if single else list(kw["out_shape"])
    scratch = list(kw.get("scratch_shapes", []))
    n_in, n_out, n_scr = len(in_specs), len(out_specs), len(scratch)
    hbm = pl.BlockSpec(memory_space=pl.ANY)

    def hosted(*refs):
        pos = [0]

        def take(k):
            pos[0] += k
            return refs[pos[0] - k:pos[0]]

        ins, s_ins, outs, s_outs, scr, sems = take(n_in), take(n), take(n_out), take(n), take(n_scr), take(3)
        ids = [pl.program_id(i) for i in range(len(grid))]
        first = functools.reduce(jnp.logical_and, [i == 0 for i in ids])
        last = functools.reduce(jnp.logical_and, [i == g - 1 for i, g in zip(ids, grid)])

        @pl.when(first)
        def _():
            local, sends, _ = _direct_copies(s_ins, s_outs, *sems, gather)
            for cp in local + sends:
                cp.start()

        body(*ins, *outs, *scr)

        @pl.when(last)
        def _():
            local, sends, recvs = _direct_copies(s_ins, s_outs, *sems, gather)
            for cp in recvs:
                cp.wait_recv()
            for cp in sends:
                cp.wait_send()
            for cp in local:
                cp.wait()

    kw.update(in_specs=in_specs + [hbm] * n, out_specs=out_specs + [hbm] * n,
              out_shape=out_shape + [jax.ShapeDtypeStruct(((N_DEV,) + a.shape) if gather else a.shape, a.dtype)
                                     for a in arrs],
              scratch_shapes=scratch + _comm_scratch(n))
    call = pl.pallas_call(hosted, **kw)

    def run(*args):
        res = call(*args, *arrs)
        main = res[0] if single else list(res[:n_out])
        return main, list(res[n_out:])

    return run


def _params(sem=None, **kw):
    if sem is not None:
        kw["dimension_semantics"] = sem
    return pltpu.CompilerParams(vmem_limit_bytes=VMEM_LIMIT, **kw)


def _rotate(xv, c, sp, sm, transpose):
    width = xv.shape[1]
    half = ROT_DIM // 2
    if transpose:
        return xv * c + pltpu.roll(xv * sp, width - half, 1) + pltpu.roll(xv * sm, half, 1)
    return xv * c + pltpu.roll(xv, half, 1) * sp + pltpu.roll(xv, width - half, 1) * sm


def mm_rows(pairs, trans_b, out_dtype, name, tm=512, side=None, rope=None):
    n = len(pairs)
    m = pairs[0][0].shape[0]
    n_out = pairs[0][1].shape[0 if trans_b else 1]
    dims = NT_DIMS if trans_b else NN_DIMS

    def body(*refs):
        o_ref = refs[-1]
        acc = None
        for a_ref, b_ref in zip(refs[:n], refs[n:2 * n]):
            d = lax.dot_general(a_ref[...], b_ref[...], dims, preferred_element_type=F32)
            acc = d if acc is None else acc + d
        if rope is None:
            o_ref[...] = acc.astype(o_ref.dtype)
        else:
            width = rope[1]
            c, sp, sm = (jnp.concatenate([r[...]] * (width // LANES), axis=1) for r in refs[2 * n:2 * n + 3])
            o_ref[:, :width] = _rotate(acc[:, :width], c, sp, sm, False).astype(o_ref.dtype)
            o_ref[:, width:] = acc[:, width:].astype(o_ref.dtype)

    in_specs = [pl.BlockSpec((tm, a.shape[1]), lambda i: (i, 0)) for a, _ in pairs]
    in_specs += [pl.BlockSpec(b.shape, lambda i: (0, 0)) for _, b in pairs]
    args = [a for a, _ in pairs] + [b for _, b in pairs]
    if rope is not None:
        in_specs += [pl.BlockSpec((tm, LANES), lambda i: (i, 0))] * 3
        args += list(rope[0])
    return _pcall(
        body, side=side, name=name, grid=(m // tm,), in_specs=in_specs,
        out_specs=pl.BlockSpec((tm, n_out), lambda i: (i, 0)),
        out_shape=jax.ShapeDtypeStruct((m, n_out), out_dtype),
        compiler_params=_params(("arbitrary",)),
    )(*args)


DH_ROWS = 256
TN_TOKENS = 2048
TN_OUT_ELEMS = 2 * 1024 * 1024


def mm_tn(a, b, out_dtype, name, side=None):
    t, ka = a.shape
    n_out = b.shape[1]
    tk = min(TN_TOKENS, t)
    tka = ka // 2 if ka * n_out > TN_OUT_ELEMS else ka
    tn = n_out
    steps = t // tk

    def body(a_ref, b_ref, o_ref, acc_ref):
        k = pl.program_id(2)
        d = lax.dot_general(a_ref[...], b_ref[...], TN_DIMS, preferred_element_type=F32)

        @pl.when(k == 0)
        def _():
            acc_ref[...] = d

        @pl.when(k > 0)
        def _():
            acc_ref[...] += d

        @pl.when(k == steps - 1)
        def _():
            o_ref[...] = acc_ref[...].astype(o_ref.dtype)

    return _pcall(
        body, side=side, name=name, grid=(ka // tka, n_out // tn, steps),
        in_specs=[pl.BlockSpec((tk, tka), lambda i, j, k: (k, i)), pl.BlockSpec((tk, tn), lambda i, j, k: (k, j))],
        out_specs=pl.BlockSpec((tka, tn), lambda i, j, k: (i, j)),
        out_shape=jax.ShapeDtypeStruct((ka, n_out), out_dtype),
        scratch_shapes=[pltpu.VMEM((tka, tn), F32)],
        compiler_params=_params(("arbitrary", "arbitrary", "arbitrary")),
    )(a, b)


def _col_chunks(width, chunk=512):
    return [slice(c, min(c + chunk, width)) for c in range(0, width, chunk)]


def _sigmoid(x):
    return 1.0 / (1.0 + jnp.exp(-x))


def ffn_up(h, wgt, wut, name, tm=256, tn=D_FF_PAD, side=None):
    t, d = h.shape
    fp = wgt.shape[0]

    def body(h_ref, wg_ref, wu_ref, g_ref, u_ref, a_ref):
        hv = h_ref[...]

        def finish(cols, g, u):
            g_ref[:, cols] = g.astype(BF16)
            u_ref[:, cols] = u.astype(BF16)
            a_ref[:, cols] = (g * _sigmoid(g) * u).astype(BF16)

        pending = None
        for cols in _col_chunks(tn):
            g = lax.dot_general(hv, wg_ref[cols, :], NT_DIMS, preferred_element_type=F32)
            u = lax.dot_general(hv, wu_ref[cols, :], NT_DIMS, preferred_element_type=F32)
            if pending is not None:
                finish(*pending)
            pending = (cols, g, u)
        finish(*pending)

    w_spec = pl.BlockSpec((tn, d), lambda j, i: (j, 0))
    o_spec = pl.BlockSpec((tm, tn), lambda j, i: (i, j))
    o_shape = jax.ShapeDtypeStruct((t, fp), BF16)
    return _pcall(
        body, side=side, name=name, grid=(fp // tn, t // tm),
        in_specs=[pl.BlockSpec((tm, d), lambda j, i: (i, 0)), w_spec, w_spec],
        out_specs=[o_spec, o_spec, o_spec], out_shape=[o_shape, o_shape, o_shape],
        compiler_params=_params(("arbitrary", "arbitrary")),
    )(h, wgt, wut)


def ffn_down_bwd(dy0, wd, gate, up, name, tm=256, tn=D_FF_PAD, side=None):
    t, d = dy0.shape
    fp = wd.shape[0]

    def body(dy_ref, wd_ref, g_ref, u_ref, dg_ref, du_ref):
        dyv = dy_ref[...]

        def finish(cols, dact):
            g = g_ref[:, cols].astype(F32)
            u = u_ref[:, cols].astype(F32)
            sg = _sigmoid(g)
            du_ref[:, cols] = (dact * g * sg).astype(BF16)
            dg_ref[:, cols] = (dact * u * (sg * (1.0 + g * (1.0 - sg)))).astype(BF16)

        pending = None
        for cols in _col_chunks(tn):
            dact = lax.dot_general(dyv, wd_ref[cols, :], NT_DIMS, preferred_element_type=F32)
            if pending is not None:
                finish(*pending)
            pending = (cols, dact)
        finish(*pending)

    t_spec = pl.BlockSpec((tm, tn), lambda j, i: (i, j))
    o_shape = jax.ShapeDtypeStruct((t, fp), BF16)
    return _pcall(
        body, side=side, name=name, grid=(fp // tn, t // tm),
        in_specs=[pl.BlockSpec((tm, d), lambda j, i: (i, 0)), pl.BlockSpec((tn, d), lambda j, i: (j, 0)), t_spec, t_spec],
        out_specs=[t_spec, t_spec], out_shape=[o_shape, o_shape],
        compiler_params=_params(("arbitrary", "arbitrary")),
    )(dy0, wd, gate, up)


def _row_specs(dx, ts, ns):
    return pl.BlockSpec((ts, dx), lambda b, s: (b * ns + s, 0))


def _mod_spec():
    return pl.BlockSpec((1, N_MOD, D_MODEL), lambda b, s: (b, 0, 0))


def _vec_spec(dx):
    return pl.BlockSpec((1, dx), lambda b, s: (0, 0))


def prenorm_fwd(x, g, mod, i_shift, i_scale, nb, name, ts=1024):
    t, dx = x.shape
    ts = min(ts, t // nb)
    ns = t // nb // ts

    def body(*refs):
        if mod is None:
            x_ref, g_ref, h_ref = refs
        else:
            x_ref, g_ref, mod_ref, h_ref = refs
        xv = x_ref[...]
        r = lax.rsqrt(jnp.mean(xv * xv, axis=-1, keepdims=True) + EPS)
        h = xv * r * g_ref[...]
        if mod is not None:
            h = h * (1.0 + mod_ref[0, i_scale:i_scale + 1, :]) + mod_ref[0, i_shift:i_shift + 1, :]
        h_ref[...] = h.astype(BF16)

    in_specs = [_row_specs(dx, ts, ns), _vec_spec(dx)]
    args = [x, g]
    if mod is not None:
        in_specs.append(_mod_spec())
        args.append(mod)
    return _pcall(
        body, name=name, grid=(nb, ns), in_specs=in_specs, out_specs=_row_specs(dx, ts, ns),
        out_shape=jax.ShapeDtypeStruct((t, dx), BF16), compiler_params=_params(("arbitrary", "arbitrary")),
    )(*args)


def prenorm_bwd(dh, x, g, mod, i_scale, dres, nb, name, ts=512, side=None):
    t, dx = x.shape
    ts = min(ts, t // nb)
    ns = t // nb // ts
    has_mod = mod is not None
    has_res = dres is not None
    pairs = dh if isinstance(dh, list) else None
    n_mm = 0 if pairs is None else len(pairs)

    def body(*refs):
        refs = list(refs)
        if pairs is None:
            dhv = refs[0][...].astype(F32)
            refs = refs[1:]
        else:
            dhv = None
            for a_ref, b_ref in zip(refs[:n_mm], refs[n_mm:2 * n_mm]):
                d = jnp.dot(a_ref[...], b_ref[...], preferred_element_type=F32)
                dhv = d if dhv is None else dhv + d
            refs = refs[2 * n_mm:]
        x_ref, g_ref = refs[:2]
        pos = 2
        mod_ref = dres_ref = None
        if has_mod:
            mod_ref = refs[pos]
            pos += 1
        if has_res:
            dres_ref = refs[pos]
            pos += 1
        dx_ref, dg_ref = refs[pos], refs[pos + 1]
        b, s = pl.program_id(0), pl.program_id(1)
        xv = x_ref[...]
        gv = g_ref[...]
        r = lax.rsqrt(jnp.mean(xv * xv, axis=-1, keepdims=True) + EPS)
        xhat = xv * r
        dn = dhv
        if has_mod:
            dsc_ref, dsh_ref = refs[pos + 2], refs[pos + 3]
            dn = dhv * (1.0 + mod_ref[0, i_scale:i_scale + 1, :])
            dsc = jnp.sum(dhv * xhat * gv, axis=0, keepdims=True)[None]
            dsh = jnp.sum(dhv, axis=0, keepdims=True)[None]

            @pl.when(s == 0)
            def _():
                dsc_ref[...] = dsc
                dsh_ref[...] = dsh

            @pl.when(s > 0)
            def _():
                dsc_ref[...] += dsc
                dsh_ref[...] += dsh

        dg = jnp.sum(dn * xhat, axis=0, keepdims=True)
        first = jnp.logical_and(b == 0, s == 0)

        @pl.when(first)
        def _():
            dg_ref[...] = dg

        @pl.when(jnp.logical_not(first))
        def _():
            dg_ref[...] += dg

        dxhat = dn * gv
        dxv = r * (dxhat - xhat * jnp.mean(dxhat * xhat, axis=-1, keepdims=True))
        if has_res:
            dxv = dxv + dres_ref[...]
        dx_ref[...] = dxv

    row = _row_specs(dx, ts, ns)
    if pairs is None:
        in_specs, args = [row], [dh]
    else:
        in_specs = [_row_specs(a.shape[1], ts, ns) for a, _ in pairs]
        in_specs += [pl.BlockSpec(b.shape, lambda b_, s_: (0, 0)) for _, b in pairs]
        args = [a for a, _ in pairs] + [b for _, b in pairs]
    in_specs += [row, _vec_spec(dx)]
    args += [x, g]
    if has_mod:
        in_specs.append(_mod_spec())
        args.append(mod)
    if has_res:
        in_specs.append(row)
        args.append(dres)
    out_specs = [row, _vec_spec(dx)]
    out_shape = [jax.ShapeDtypeStruct((t, dx), F32), jax.ShapeDtypeStruct((1, dx), F32)]
    if has_mod:
        bspec = pl.BlockSpec((1, 1, dx), lambda b, s: (b, 0, 0))
        out_specs += [bspec, bspec]
        out_shape += [jax.ShapeDtypeStruct((nb, 1, dx), F32)] * 2
    return _pcall(
        body, side=side, name=name, grid=(nb, ns), in_specs=in_specs, out_specs=out_specs, out_shape=out_shape,
        compiler_params=_params(("arbitrary", "arbitrary")),
    )(*args)


def postnorm_fwd(x, pairs, g, mod, i_gate, coef, nb, name, target=None, ts=512):
    t, dx = x.shape
    with_loss = target is not None
    ts = min(ts, t // nb)
    ns = t // nb // ts
    n_mm = len(pairs)

    def body(*refs):
        yv = None
        for a_ref, b_ref in zip(refs[:n_mm], refs[n_mm:2 * n_mm]):
            d = jnp.dot(a_ref[...], b_ref[...], preferred_element_type=F32)
            yv = d if yv is None else yv + d
        refs = refs[2 * n_mm:]
        x_ref, g_ref, mod_ref = refs[:3]
        refs[-1][...] = yv
        r = lax.rsqrt(jnp.mean(yv * yv, axis=-1, keepdims=True) + EPS)
        out = x_ref[...] + (coef * mod_ref[0, i_gate:i_gate + 1, :]) * (yv * r * g_ref[...])
        if not with_loss:
            refs[3][...] = out
            return
        t_ref, dx_ref, loss_ref = refs[3:6]
        b, s = pl.program_id(0), pl.program_id(1)
        err = out - t_ref[...]
        dx_ref[...] = err * (1.0 / dx)
        part = (0.5 / dx) * jnp.sum(jnp.sum(err * err, axis=1, keepdims=True), axis=0, keepdims=True)
        first = jnp.logical_and(b == 0, s == 0)

        @pl.when(first)
        def _():
            loss_ref[...] = part

        @pl.when(jnp.logical_not(first))
        def _():
            loss_ref[...] += part

    row = _row_specs(dx, ts, ns)
    in_specs = [_row_specs(a.shape[1], ts, ns) for a, _ in pairs]
    in_specs += [pl.BlockSpec(b.shape, lambda b_, s_: (0, 0)) for _, b in pairs]
    in_specs += [row, _vec_spec(dx), _mod_spec()]
    args = [a for a, _ in pairs] + [b for _, b in pairs] + [x, g, mod]
    row_shape = jax.ShapeDtypeStruct((t, dx), F32)
    out_specs, out_shape = [row, row], [row_shape, row_shape]
    if with_loss:
        in_specs.append(row)
        args.append(target)
        out_specs = [row, pl.BlockSpec((1, 1), lambda b, s: (0, 0)), row]
        out_shape = [row_shape, jax.ShapeDtypeStruct((1, 1), F32), row_shape]
    return _pcall(
        body, name=name, grid=(nb, ns), in_specs=in_specs, out_specs=out_specs, out_shape=out_shape,
        compiler_params=_params(("arbitrary", "arbitrary")),
    )(*args)


def postnorm_bwd(dxo, y0, g, mod, i_gate, coef, nb, name, ts=1024):
    t, dx = y0.shape
    ts = min(ts, t // nb)
    ns = t // nb // ts

    def body(d_ref, y_ref, g_ref, mod_ref, dy_ref, dg_ref, dgate_ref):
        b, s = pl.program_id(0), pl.program_id(1)
        yv = y_ref[...]
        dv = d_ref[...]
        gv = g_ref[...]
        r = lax.rsqrt(jnp.mean(yv * yv, axis=-1, keepdims=True) + EPS)
        yhat = yv * r
        dgate = jnp.sum(dv * (coef * (yhat * gv)), axis=0, keepdims=True)[None]
        dyn = dv * (coef * mod_ref[0, i_gate:i_gate + 1, :])
        dg = jnp.sum(dyn * yhat, axis=0, keepdims=True)
        dyhat = dyn * gv
        dy_ref[...] = (r * (dyhat - yhat * jnp.mean(dyhat * yhat, axis=-1, keepdims=True))).astype(BF16)

        @pl.when(s == 0)
        def _():
            dgate_ref[...] = dgate

        @pl.when(s > 0)
        def _():
            dgate_ref[...] += dgate

        first = jnp.logical_and(b == 0, s == 0)

        @pl.when(first)
        def _():
            dg_ref[...] = dg

        @pl.when(jnp.logical_not(first))
        def _():
            dg_ref[...] += dg

    row = _row_specs(dx, ts, ns)
    return _pcall(
        body, name=name, grid=(nb, ns), in_specs=[row, row, _vec_spec(dx), _mod_spec()],
        out_specs=[row, _vec_spec(dx), pl.BlockSpec((1, 1, dx), lambda b, s: (b, 0, 0))],
        out_shape=[jax.ShapeDtypeStruct((t, dx), BF16), jax.ShapeDtypeStruct((1, dx), F32),
                   jax.ShapeDtypeStruct((nb, 1, dx), F32)],
        compiler_params=_params(("arbitrary", "arbitrary")),
    )(dxo, y0, g, mod)


def rope_tables(positions):
    inv_freq = ROPE_THETA ** (-jnp.arange(0, ROT_DIM, 2, dtype=F32) / ROT_DIM)
    ang = positions.astype(F32).reshape(-1, 1) * inv_freq
    cos, sin = jnp.cos(ang), jnp.sin(ang)
    half = ROT_DIM // 2
    z = lambda n: jnp.zeros((ang.shape[0], n), F32)
    c = jnp.concatenate([cos, cos, jnp.ones((ang.shape[0], HEAD_DIM - ROT_DIM), F32)], axis=1)
    sp = jnp.concatenate([z(half), sin, z(HEAD_DIM - ROT_DIM)], axis=1)
    sm = jnp.concatenate([-sin, z(HEAD_DIM - half)], axis=1)
    return tuple(jnp.tile(a, (1, HEADS_PER_STEP)) for a in (c, sp, sm))


def _scan_lanes(x, reverse):
    n = x.shape[-1]
    lane = lax.broadcasted_iota(jnp.int32, x.shape, x.ndim - 1)
    k = 1
    while k < n:
        if reverse:
            x = x + jnp.where(lane < n - k, pltpu.roll(x, n - k, x.ndim - 1), 0.0)
        else:
            x = x + jnp.where(lane >= k, pltpu.roll(x, k, x.ndim - 1), 0.0)
        k *= 2
    return x


def _log_sigmoid(z):
    return jnp.minimum(z, 0.0) - jnp.log(1.0 + jnp.exp(-jnp.abs(z)))


def fox_gate_fwd(ft, b_forget, name):
    nb, nh, s = ft.shape

    def body(f_ref, b_ref, o_ref):
        z = f_ref[0] + b_ref[...]
        o_ref[0] = -_scan_lanes(_log_sigmoid(z), False)

    spec = pl.BlockSpec((1, nh, s), lambda b: (b, 0, 0))
    return _pcall(
        body, name=name, grid=(nb,), in_specs=[spec, pl.BlockSpec((nh, 1), lambda b: (0, 0))], out_specs=spec,
        out_shape=jax.ShapeDtypeStruct((nb, nh, s), F32), compiler_params=_params(("arbitrary",)),
    )(ft, b_forget)


def fox_gate_bwd(dcb, drow, ft, b_forget, name):
    nb, nh, s = ft.shape

    def body(d_ref, r_ref, f_ref, b_ref, dz_ref, db_ref):
        b = pl.program_id(0)
        z = f_ref[0] + b_ref[...]
        dlf = _scan_lanes(r_ref[0] - d_ref[0], True)
        dz = dlf * _sigmoid(-z)
        dz_ref[0] = dz
        db = jnp.sum(dz, axis=1, keepdims=True)

        @pl.when(b == 0)
        def _():
            db_ref[...] = db

        @pl.when(b > 0)
        def _():
            db_ref[...] += db

    spec = pl.BlockSpec((1, nh, s), lambda b: (b, 0, 0))
    vec = pl.BlockSpec((nh, 1), lambda b: (0, 0))
    return _pcall(
        body, name=name, grid=(nb,), in_specs=[spec, spec, spec, vec], out_specs=[spec, vec],
        out_shape=[jax.ShapeDtypeStruct((nb, nh, s), F32), jax.ShapeDtypeStruct((nh, 1), F32)],
        compiler_params=_params(("arbitrary",)),
    )(dcb, drow, ft, b_forget)


ATTN_TQ = 512
ATTN_TK = 512
ONES_ROWS = 16


def _block_delta(s, tq, tk):
    off = jnp.arange(s // tk) - (tq // tk - 1)
    return off[:, None, None] * tk + jnp.arange(tq)[None, None, :] - jnp.arange(tk)[None, :, None]


def dilated_table(s, tq, tk):
    delta = _block_delta(s, tq, tk)
    count = jnp.zeros(delta.shape, F32)
    for window, dil in DILATED_PATTERNS:
        count = count + ((delta >= 0) & (delta <= window) & (delta % dil == 0)).astype(F32)
    return jnp.where(count > 0, jnp.log(jnp.maximum(count, 1.0)), NEG)


def causal_table(s, tq, tk):
    return jnp.where(_block_delta(s, tq, tk) >= 0, 0.0, NEG).astype(F32)


def attn_fwd(q_arr, q_off, k_arr, k_off, v_arr, v_off, table, colbias, nb, name, side=None):
    t = q_arr.shape[0]
    s = t // nb
    tk, tq = table.shape[1:]
    nq, nk, r = s // tq, s // tk, tq // tk
    npairs = WIDTH_A // LANES
    use_cb = colbias is not None

    def body(*refs):
        refs = list(refs)
        q_ref, k_ref, v_ref, tab_ref = refs[:4]
        cb_ref = refs[4] if use_cb else None
        o_ref, lse_ref, vt_s = refs[-3 - HEADS_PER_STEP:-HEADS_PER_STEP]
        acc_s = refs[-HEADS_PER_STEP:]
        qi = pl.program_id(2)

        heads = [slice(h * HEAD_DIM, (h + 1) * HEAD_DIM) for h in range(HEADS_PER_STEP)]

        @pl.when(qi == 0)
        def _():
            for cblk in range(nk):
                vt = v_ref[cblk * tk:(cblk + 1) * tk, :].astype(F32).T.astype(BF16)
                for h, hs in enumerate(heads):
                    vt_s[cblk, h, 0:HEAD_DIM, :] = vt[hs, :]
                    vt_s[cblk, h, HEAD_DIM:, :] = jnp.ones((ONES_ROWS, tk), BF16)

        qt_all = (q_ref[...].astype(F32) * ATTN_SCALE).T.astype(BF16)
        qts = [qt_all[hs, :] for hs in heads]
        for a in acc_s:
            a[...] = jnp.zeros_like(a)

        def step(kb, carry):
            ks = pl.multiple_of(kb * tk, tk)
            tab = tab_ref[qi * r + (r - 1) - kb]
            sts = []
            for h, hs in enumerate(heads):
                st = jnp.dot(k_ref[pl.ds(ks, tk), hs], qts[h], preferred_element_type=F32) + tab
                if use_cb:
                    st = st + cb_ref[0, h, pl.ds(ks, tk), :]
                sts.append(st)
            m_new = [jnp.maximum(carry[h], jnp.max(sts[h], axis=0, keepdims=True)) for h in range(HEADS_PER_STEP)]
            for h in range(HEADS_PER_STEP):
                pt = jnp.exp(sts[h] - m_new[h]).astype(BF16)
                acc_s[h][...] = (jnp.exp(carry[h] - m_new[h]) * acc_s[h][...]
                                 + jnp.dot(vt_s[kb, h], pt, preferred_element_type=F32))
            return tuple(m_new)

        fin = lax.fori_loop(0, (qi + 1) * r, step, tuple(jnp.full((1, tq), NEG, F32) for _ in heads))
        outs = []
        for h in range(HEADS_PER_STEP):
            l = acc_s[h][HEAD_DIM:HEAD_DIM + 1, :]
            outs.append(acc_s[h][0:HEAD_DIM, :] / l)
            lse_ref[0, h, 0] = fin[h] + jnp.log(l)
        o_ref[...] = jnp.concatenate(outs, axis=0).T

    def seq_spec(off):
        return pl.BlockSpec((s, LANES), lambda b, j, i: (b, off + j))

    in_specs = [pl.BlockSpec((tq, LANES), lambda b, j, i: (b * nq + i, q_off + j)), seq_spec(k_off), seq_spec(v_off),
                pl.BlockSpec(table.shape, lambda b, j, i: (0, 0, 0))]
    args = [q_arr, k_arr, v_arr, table]
    if use_cb:
        in_specs.append(pl.BlockSpec((1, HEADS_PER_STEP, s, 1), lambda b, j, i: (b, j, 0, 0)))
        args.append(colbias)
    n_heads = npairs * HEADS_PER_STEP
    return _pcall(
        body, side=side, name=name, grid=(nb, npairs, nq), in_specs=in_specs,
        out_specs=[pl.BlockSpec((tq, LANES), lambda b, j, i: (b * nq + i, j)),
                   pl.BlockSpec((1, HEADS_PER_STEP, 1, 1, tq), lambda b, j, i: (b, j, i, 0, 0))],
        out_shape=[jax.ShapeDtypeStruct((t, npairs * LANES), F32), jax.ShapeDtypeStruct((nb, n_heads, nq, 1, tq), F32)],
        scratch_shapes=[pltpu.VMEM((nk, HEADS_PER_STEP, HEAD_DIM + ONES_ROWS, tk), BF16)]
        + [pltpu.VMEM((HEAD_DIM + ONES_ROWS, tq), F32)] * HEADS_PER_STEP,
        compiler_params=_params(("arbitrary", "arbitrary", "arbitrary")),
    )(*args)


def attn_bwd(q_arr, q_off, k_arr, k_off, v_arr, v_off, o_arr, lse_arr, do_arr, table, colbias, nb, name, side=None,
             rope_tabs=None):
    t = q_arr.shape[0]
    s = t // nb
    tk, tq = table.shape[1:]
    nq, nk, r = s // tq, s // tk, tq // tk
    npairs = WIDTH_A // LANES
    use_cb = colbias is not None

    def body(*refs):
        refs = list(refs)
        q_ref, k_ref, v_ref, o_ref, lse_ref, do_ref, tab_ref = refs[:7]
        pos = 7
        cb_ref = None
        if use_cb:
            cb_ref = refs[pos]
            pos += 1
        rope_refs = None
        if rope_tabs is not None:
            rope_refs = refs[pos:pos + 3]
            pos += 3
        dq_ref, dk_ref, dv_ref = refs[pos:pos + 3]
        pos += 3
        dcb_ref = drow_ref = None
        if use_cb:
            dcb_ref, drow_ref = refs[pos:pos + 2]
            pos += 2
        kt_s, dkt_s, dvt_s = refs[pos:pos + 3]
        dqt_s = refs[pos + 3:pos + 3 + HEADS_PER_STEP]
        dcb_s = refs[pos + 3 + HEADS_PER_STEP] if use_cb else None

        heads = [slice(h * HEAD_DIM, (h + 1) * HEAD_DIM) for h in range(HEADS_PER_STEP)]
        for cblk in range(nk):
            kt_s[cblk] = k_ref[cblk * tk:(cblk + 1) * tk, :].astype(F32).T.astype(BF16)
        dkt_s[...] = jnp.zeros_like(dkt_s)
        dvt_s[...] = jnp.zeros_like(dvt_s)
        if use_cb:
            dcb_s[...] = jnp.zeros_like(dcb_s)
        ones = jnp.ones((8, HEAD_DIM), BF16)

        def q_loop(qi, carry):
            qs = pl.multiple_of(qi * tq, tq)
            q_all = (q_ref[pl.ds(qs, tq), :].astype(F32) * ATTN_SCALE)
            do_all = do_ref[pl.ds(qs, tq), :]
            qt_all = q_all.T.astype(BF16)
            dot_all = do_all.T.astype(BF16)
            qt, dot, lse, dsum = [], [], [], []
            for h, hs in enumerate(heads):
                qt.append(qt_all[hs, :])
                dot.append(dot_all[hs, :])
                lse.append(lse_ref[0, h, qi])
                prod = do_all[:, hs] * o_ref[pl.ds(qs, tq), hs]
                hi = prod.astype(BF16)
                lo = (prod - hi.astype(F32)).astype(BF16)
                dsum.append((lax.dot_general(ones, hi, NT_DIMS, preferred_element_type=F32)
                             + lax.dot_general(ones, lo, NT_DIMS, preferred_element_type=F32))[0:1, :])
            for a in dqt_s:
                a[...] = jnp.zeros_like(a)

            def k_loop(kb, drow):
                ks = pl.multiple_of(kb * tk, tk)
                tab = tab_ref[qi * r + (r - 1) - kb]
                sts, dpts, out = [], [], []
                for h, hs in enumerate(heads):
                    st = jnp.dot(k_ref[pl.ds(ks, tk), hs], qt[h], preferred_element_type=F32) + tab
                    if use_cb:
                        st = st + cb_ref[0, h, pl.ds(ks, tk), :]
                    sts.append(st)
                    dpts.append(jnp.dot(v_ref[pl.ds(ks, tk), hs], dot[h], preferred_element_type=F32))
                for h, hs in enumerate(heads):
                    pt = jnp.exp(sts[h] - lse[h])
                    dst = pt * (dpts[h] - dsum[h])
                    dst_b = dst.astype(BF16)
                    dvt_s[h, kb] += lax.dot_general(dot[h], pt.astype(BF16), NT_DIMS, preferred_element_type=F32)
                    dkt_s[h, kb] += lax.dot_general(qt[h], dst_b, NT_DIMS, preferred_element_type=F32)
                    dqt_s[h][...] += jnp.dot(kt_s[kb, hs, :], dst_b, preferred_element_type=F32)
                    if use_cb:
                        dcb_s[h, pl.ds(ks, tk), :] += jnp.sum(dst, axis=1, keepdims=True)
                        out.append(drow[h] + jnp.sum(dst, axis=0, keepdims=True))
                    else:
                        out.append(drow[h])
                return tuple(out)

            drow = lax.fori_loop(0, (qi + 1) * r, k_loop, tuple(jnp.zeros((1, tq), F32) for _ in heads))
            dq = (jnp.concatenate([a[...] for a in dqt_s], axis=0) * ATTN_SCALE).T
            if rope_refs is not None:
                dq = _rotate(dq, *[coef[pl.ds(qs, tq), :] for coef in rope_refs], True)
            dq_ref[pl.ds(qs, tq), :] = dq.astype(dq_ref.dtype)
            if use_cb:
                for h in range(HEADS_PER_STEP):
                    drow_ref[0, h, qi] = drow[h]
            return carry

        lax.fori_loop(0, nq, q_loop, 0)
        for cblk in range(nk):
            rows = slice(cblk * tk, (cblk + 1) * tk)
            dk = jnp.concatenate([dkt_s[h, cblk] for h in range(HEADS_PER_STEP)], axis=0).T
            if rope_refs is not None:
                dk = _rotate(dk, *[coef[rows, :] for coef in rope_refs], True)
            dk_ref[rows, :] = dk.astype(dk_ref.dtype)
            dv_ref[rows, :] = jnp.concatenate([dvt_s[h, cblk] for h in range(HEADS_PER_STEP)], axis=0).T.astype(dv_ref.dtype)
        if use_cb:
            for h in range(HEADS_PER_STEP):
                dcb_ref[0, h] = dcb_s[h]

    def seq_spec(off):
        return pl.BlockSpec((s, LANES), lambda b, j: (b, off + j))

    row_spec = pl.BlockSpec((1, HEADS_PER_STEP, nq, 1, tq), lambda b, j: (b, j, 0, 0, 0))
    in_specs = [seq_spec(q_off), seq_spec(k_off), seq_spec(v_off), seq_spec(0), row_spec, seq_spec(0),
                pl.BlockSpec(table.shape, lambda b, j: (0, 0, 0))]
    args = [q_arr, k_arr, v_arr, o_arr, lse_arr, do_arr, table]
    width = npairs * LANES
    out_specs = [seq_spec(0)] * 3
    out_shape = [jax.ShapeDtypeStruct((t, width), BF16)] * 3
    scratch = [pltpu.VMEM((nk, LANES, tk), BF16), pltpu.VMEM((HEADS_PER_STEP, nk, HEAD_DIM, tk), F32),
               pltpu.VMEM((HEADS_PER_STEP, nk, HEAD_DIM, tk), F32)] + [pltpu.VMEM((HEAD_DIM, tq), F32)] * HEADS_PER_STEP
    if use_cb:
        cb_spec = pl.BlockSpec((1, HEADS_PER_STEP, s, 1), lambda b, j: (b, j, 0, 0))
        in_specs.append(cb_spec)
        args.append(colbias)
    if rope_tabs is not None:
        in_specs += [pl.BlockSpec((s, LANES), lambda b, j: (b, 0))] * 3
        args += list(rope_tabs)
    if use_cb:
        out_specs += [cb_spec, row_spec]
        out_shape += [jax.ShapeDtypeStruct(colbias.shape, F32), jax.ShapeDtypeStruct(lse_arr.shape, F32)]
        scratch.append(pltpu.VMEM((HEADS_PER_STEP, s, 1), F32))
    return _pcall(
        body, side=side, name=name, grid=(nb, npairs), in_specs=in_specs, out_specs=out_specs, out_shape=out_shape,
        scratch_shapes=scratch, compiler_params=_params(("arbitrary", "arbitrary")),
    )(*args)


def ada_fwd(c_all, w_ada, b_cols, name):
    def body(c_ref, w_ref, b_ref, o_ref):
        cv = c_ref[...]
        sc = (cv * _sigmoid(cv)).astype(BF16)
        o_ref[...] = jnp.dot(sc, w_ref[...].astype(BF16), preferred_element_type=F32) + b_ref[...]

    return _pcall(body, name=name, out_shape=jax.ShapeDtypeStruct((c_all.shape[0], w_ada.shape[1]), F32),
                  compiler_params=_params())(c_all, w_ada, b_cols)


def ada_bwd(c_all, dmod_cols, name):
    def body(c_ref, d_ref, o_ref):
        cv = c_ref[...]
        sc = (cv * _sigmoid(cv)).astype(BF16)
        o_ref[...] = lax.dot_general(sc, d_ref[...].astype(BF16), TN_DIMS, preferred_element_type=F32)

    return _pcall(body, name=name, out_shape=jax.ShapeDtypeStruct((c_all.shape[1], dmod_cols.shape[1]), F32),
                  compiler_params=_params())(c_all, dmod_cols)


def adamw(parts, group, w, m, v, name, tr=None):
    n = parts.shape[0]
    r, c = w.shape
    tr = r if tr is None else tr
    c1 = 1.0 - ADAM_B1 ** ADAM_STEP
    c2 = 1.0 - ADAM_B2 ** ADAM_STEP

    def body(p_ref, w_ref, m_ref, v_ref, g_ref, d_ref, nm_ref, nv_ref):
        g = p_ref[0, 0].astype(F32)
        for i in range(1, n):
            g = g + p_ref[i, 0].astype(F32)
        wv = w_ref[...]
        nm = ADAM_B1 * m_ref[...] + (1.0 - ADAM_B1) * g
        nv = ADAM_B2 * v_ref[...] + (1.0 - ADAM_B2) * (g * g)
        g_ref[...] = g
        nm_ref[...] = nm
        nv_ref[...] = nv
        d_ref[...] = -ADAM_LR * ((nm / c1) / (jnp.sqrt(nv / c2) + ADAM_EPS) + ADAM_WD * wv)

    spec = pl.BlockSpec((tr, c), lambda i: (i, 0))
    shape = jax.ShapeDtypeStruct((r, c), F32)
    return _pcall(
        body, name=name, grid=(r // tr,),
        in_specs=[pl.BlockSpec((n, 1, tr, c), lambda i: (0, group, i, 0)), spec, spec, spec],
        out_specs=[spec] * 4, out_shape=[shape] * 4, compiler_params=_params(("arbitrary",)),
    )(parts, w, m, v)


def all_gather(arrs, name):
    n = len(arrs)
    hbm = pl.BlockSpec(memory_space=pl.ANY)

    def body(*refs):
        ins, outs = refs[:n], refs[n:2 * n]
        send_sems, recv_sems, local_sems = refs[2 * n:]
        x, y, c = _place()
        me, sibling = (x, y, c), (x, y, 1 - c)
        chips = [(1 - x, y), (x, 1 - y), (1 - x, 1 - y)]

        def copy(a, k, block, to, src=None):
            dst = outs[a].at[_slot(block)]
            return pltpu.make_async_remote_copy(
                src_ref=dst if src is None else src, dst_ref=dst, send_sem=send_sems.at[a * 7 + k],
                recv_sem=recv_sems.at[a * 7 + k], device_id=to, device_id_type=MESH)

        mine = [pltpu.make_async_copy(ins[a], outs[a].at[_slot(me)], local_sems.at[a]) for a in range(n)]
        for cp in mine:
            cp.start()
        first = []
        for a in range(n):
            first.append(copy(a, 0, me, sibling, src=ins[a]))
            first += [copy(a, 1 + j, me, (*chip, c), src=ins[a]) for j, chip in enumerate(chips)]
        for cp in first:
            cp.start()
        passed = []
        for a in range(n):
            for j, chip in enumerate(chips):
                copy(a, 1 + j, (*chip, c), me).wait_recv()
                cp = copy(a, 4 + j, (*chip, c), sibling)
                cp.start()
                passed.append(cp)
        for a in range(n):
            copy(a, 0, sibling, me).wait_recv()
            for j, chip in enumerate(chips):
                copy(a, 4 + j, (*chip, 1 - c), me).wait_recv()
        for cp in first + passed:
            cp.wait_send()
        for cp in mine:
            cp.wait()

    return _pcall(
        body, name=name, in_specs=[hbm] * n, out_specs=[hbm] * n,
        out_shape=[jax.ShapeDtypeStruct((N_DEV,) + a.shape, a.dtype) for a in arrs],
        scratch_shapes=[pltpu.SemaphoreType.DMA((7 * n,)), pltpu.SemaphoreType.DMA((7 * n,)),
                        pltpu.SemaphoreType.DMA((n,))],
        compiler_params=pltpu.CompilerParams(has_side_effects=True),
    )(*arrs)


def _t(w):
    return jnp.swapaxes(w, -1, -2)


def _rows_from_blocks(blocks, pad_to=None):
    full = blocks.reshape(-1, blocks.shape[2])
    if pad_to is not None and pad_to > full.shape[0]:
        full = jnp.pad(full, ((0, pad_to - full.shape[0]), (0, 0)))
    return full


def _rows_to_blocks(full, nrows):
    return full[:nrows].reshape(N_DEV, nrows // N_DEV, full.shape[1])


SMALL_ORDER = ("g_pre_ff1", "g_post_ff1", "g_pre_mix", "g_post_mix", "g_out_a", "g_out_b", "g_pre_ff2", "g_post_ff2",
               "b_forget")


def _pack_small(vals):
    rows = []
    for name in SMALL_ORDER:
        v = vals[name].reshape(1, -1)
        if v.shape[1] % LANES:
            v = jnp.pad(v, ((0, 0), (0, LANES - v.shape[1] % LANES)))
        rows.append(v)
    return jnp.concatenate(rows, axis=1)


def _unpack_small(row, sizes):
    out, pos = {}, 0
    for name in SMALL_ORDER:
        n = sizes[name]
        out[name] = row[:, pos:pos + n]
        pos += -(-n // LANES) * LANES
    return out


def _ffn_forward(x, mod, g_pre, g_post, wg, wu, wd, i0, nb, tag, target=None, side=None):
    h = prenorm_fwd(x, g_pre, mod, i0, i0 + 1, nb, f"{tag}_prenorm")
    res, side_out = ffn_up(h, wg, wu, f"{tag}_up", side=side), None
    if side is not None:
        res, side_out = res
    gate, up, act = res
    res = postnorm_fwd(x, [(act, wd)], g_post, mod, i0 + 2, 0.5, nb, f"{tag}_down_postnorm", target=target)
    out, y0 = (res[0] if target is None else tuple(res[:2])), res[-1]
    return out, (x, h, gate, up, act, y0), side_out


def _ffn_backward(dxo, saved, mod, g_pre, g_post, wg, wu, wd, i0, nb, tag, side=None, chain=False):
    x, h, gate, up, act, y0 = saved
    dy0, dg_post, dgate_mod = postnorm_bwd(dxo, y0, g_post, mod, i0 + 2, 0.5, nb, f"{tag}_postnorm_bwd")
    dwd = mm_tn(act, dy0, BF16, f"{tag}_dwd")
    res, side_out = ffn_down_bwd(dy0, wd, gate, up, f"{tag}_down_bwd", side=side), None
    if side is not None:
        res, side_out = res
    dgate, dup = res
    dh_pairs = [(dgate, wg), (dup, wu)]
    if chain:
        dwg, (dwd,) = mm_tn(dgate, h, BF16, f"{tag}_dwg", side=([_rows_to_blocks(dwd, D_FF)[:, None]], False))
        dwu, (dwg,) = mm_tn(dup, h, BF16, f"{tag}_dwu", side=([_rows_to_blocks(dwg, D_FF)[:, None]], False))
        (dx, dg_pre, dsc, dsh), (dwu,) = prenorm_bwd(dh_pairs, x, g_pre, mod, i0 + 1, dxo, nb, f"{tag}_dh_prenorm_bwd",
                                                     ts=DH_ROWS, side=([_rows_to_blocks(dwu, D_FF)[:, None]], False))
    else:
        dwg = mm_tn(dgate, h, BF16, f"{tag}_dwg")
        dwu = mm_tn(dup, h, BF16, f"{tag}_dwu")
        dx, dg_pre, dsc, dsh = prenorm_bwd(dh_pairs, x, g_pre, mod, i0 + 1, dxo, nb, f"{tag}_dh_prenorm_bwd", ts=DH_ROWS)
    return dx, dict(g_pre=dg_pre, g_post=dg_post, wg=dwg, wu=dwu, wd=dwd, mod=(dsh, dsc, dgate_mod)), side_out


def kernel(x, c, positions, w_ada, b_ada, g_pre_ff1, g_post_ff1, w_ff1_gate, w_ff1_up, w_ff1_down, g_pre_mix, g_post_mix, w_in, b_forget, g_out_a, g_out_b, w_out, g_pre_ff2, g_post_ff2, w_ff2_gate, w_ff2_up, w_ff2_down, loss_target, m_w_ada, m_b_ada, m_g_pre_ff1, m_g_post_ff1, m_w_ff1_gate, m_w_ff1_up, m_w_ff1_down, m_g_pre_mix, m_g_post_mix, m_w_in, m_b_forget, m_g_out_a, m_g_out_b, m_w_out, m_g_pre_ff2, m_g_post_ff2, m_w_ff2_gate, m_w_ff2_up, m_w_ff2_down, v_w_ada, v_b_ada, v_g_pre_ff1, v_g_post_ff1, v_w_ff1_gate, v_w_ff1_up, v_w_ff1_down, v_g_pre_mix, v_g_post_mix, v_w_in, v_b_forget, v_g_out_a, v_g_out_b, v_w_out, v_g_pre_ff2, v_g_post_ff2, v_w_ff2_gate, v_w_ff2_up, v_w_ff2_down):
    weights = dict(w_ada=w_ada, b_ada=b_ada, g_pre_ff1=g_pre_ff1, g_post_ff1=g_post_ff1, w_ff1_gate=w_ff1_gate,
                   w_ff1_up=w_ff1_up, w_ff1_down=w_ff1_down, g_pre_mix=g_pre_mix, g_post_mix=g_post_mix, w_in=w_in,
                   b_forget=b_forget, g_out_a=g_out_a, g_out_b=g_out_b, w_out=w_out, g_pre_ff2=g_pre_ff2,
                   g_post_ff2=g_post_ff2, w_ff2_gate=w_ff2_gate, w_ff2_up=w_ff2_up, w_ff2_down=w_ff2_down)
    mom_m = dict(w_ada=m_w_ada, b_ada=m_b_ada, g_pre_ff1=m_g_pre_ff1, g_post_ff1=m_g_post_ff1, w_ff1_gate=m_w_ff1_gate,
                 w_ff1_up=m_w_ff1_up, w_ff1_down=m_w_ff1_down, g_pre_mix=m_g_pre_mix, g_post_mix=m_g_post_mix,
                 w_in=m_w_in, b_forget=m_b_forget, g_out_a=m_g_out_a, g_out_b=m_g_out_b, w_out=m_w_out,
                 g_pre_ff2=m_g_pre_ff2, g_post_ff2=m_g_post_ff2, w_ff2_gate=m_w_ff2_gate, w_ff2_up=m_w_ff2_up,
                 w_ff2_down=m_w_ff2_down)
    mom_v = dict(w_ada=v_w_ada, b_ada=v_b_ada, g_pre_ff1=v_g_pre_ff1, g_post_ff1=v_g_post_ff1, w_ff1_gate=v_w_ff1_gate,
                 w_ff1_up=v_w_ff1_up, w_ff1_down=v_w_ff1_down, g_pre_mix=v_g_pre_mix, g_post_mix=v_g_post_mix,
                 w_in=v_w_in, b_forget=v_b_forget, g_out_a=v_g_out_a, g_out_b=v_g_out_b, w_out=v_w_out,
                 g_pre_ff2=v_g_pre_ff2, g_post_ff2=v_g_post_ff2, w_ff2_gate=v_w_ff2_gate, w_ff2_up=v_w_ff2_up,
                 w_ff2_down=v_w_ff2_down)
    order = list(weights)

    nb, s, d = x.shape
    t = nb * s
    me = _slot(_place())
    nbg = nb * N_DEV
    ada_cols = w_ada.shape[2]

    bf = lambda w: w[0].astype(BF16)
    bft = lambda w: _t(w)[0].astype(BF16)
    c_all, ff1_all = all_gather([c, jnp.stack([bft(w_ff1_gate), bft(w_ff1_up), bf(w_ff1_down)])], "gather_ff1")
    c_all = c_all.reshape(nbg, d)
    wg1, wu1, wd1 = (_rows_from_blocks(ff1_all[:, i], D_FF_PAD) for i in range(3))

    b_cols = lax.dynamic_slice(b_ada, (0, me * ada_cols), (1, ada_cols))
    mod_cols = ada_fwd(c_all, w_ada[0], b_cols, "ada_fwd")
    (mod_all,) = all_gather([mod_cols], "gather_mod")
    mod = lax.dynamic_slice(mod_all, (0, me * nb, 0), (N_DEV, nb, ada_cols))
    mod = mod.transpose(1, 0, 2).reshape(nb, N_MOD, d)

    xf = x.reshape(t, d)
    target = loss_target.reshape(t, d)

    x1, saved1, (w_in_all, w_out_all) = _ffn_forward(xf, mod, g_pre_ff1, g_post_ff1, wg1, wu1, wd1, 0, nb, "ff1",
                                                     side=([bft(w_in), bf(w_out)], True))
    w_in_t = _rows_from_blocks(w_in_all)
    n_qkv = 3 * (WIDTH_A + WIDTH_B)
    w_qkv_t = w_in_t[:n_qkv]
    w_f_t = jnp.pad(w_in_t[n_qkv:], ((0, LANES - N_HEADS_B), (0, 0)))
    w_o = _rows_from_blocks(w_out_all)
    w_o_a, w_o_b = w_o[:WIDTH_A], w_o[WIDTH_A:]

    h2 = prenorm_fwd(x1, g_pre_mix, mod, 3, 4, nb, "mix_prenorm")
    tables = rope_tables(positions)
    proj = mm_rows([(h2, w_qkv_t)], True, BF16, "mix_proj", rope=(tables, 2 * WIDTH_A))
    f_logit = mm_rows([(h2, w_f_t)], True, F32, "mix_forget")
    tab_a = dilated_table(s, ATTN_TQ, ATTN_TK)
    tab_b = causal_table(s, ATTN_TQ, ATTN_TK)
    ft = f_logit[:, :N_HEADS_B].reshape(nb, s, N_HEADS_B).transpose(0, 2, 1)
    bf_col = b_forget.reshape(N_HEADS_B, 1)
    colbias = fox_gate_fwd(ft, bf_col, "fox_gate").reshape(nb, N_HEADS_B, s, 1)
    pa = WIDTH_A // LANES
    (o_a, lse_a), (ff2_all,) = attn_fwd(
        proj, 0, proj, pa, proj, 2 * pa, tab_a, None, nb, "attn_a",
        side=([jnp.stack([bft(w_ff2_gate), bft(w_ff2_up), bf(w_ff2_down)])], True))
    wg2, wu2, wd2 = (_rows_from_blocks(ff2_all[:, i], D_FF_PAD) for i in range(3))
    o_b, lse_b = attn_fwd(proj, 3 * pa, proj, 4 * pa, proj, 5 * pa, tab_b, colbias, nb, "attn_b")
    m_a = prenorm_fwd(o_a, g_out_a, None, None, None, nb, "out_norm_a")
    m_b = prenorm_fwd(o_b, g_out_b, None, None, None, nb, "out_norm_b")
    x2, y0m = postnorm_fwd(x1, [(m_a, w_o_a), (m_b, w_o_b)], g_post_mix, mod, 5, 1.0, nb, "mix_out_postnorm")

    (dx3, loss_part), saved2, _ = _ffn_forward(x2, mod, g_pre_ff2, g_post_ff2, wg2, wu2, wd2, 6, nb, "ff2", target=target)
    loss = lax.psum(loss_part[0, 0], ("x", "y", "c"))

    dx2, gr2, _ = _ffn_backward(dx3, saved2, mod, g_pre_ff2, g_post_ff2, wg2, wu2, wd2, 6, nb, "ff2")
    ff2_blocks = [jnp.stack([_rows_to_blocks(gr2[k], D_FF) for k in ("wg", "wu", "wd")], axis=1)]

    dy0m, dg_post_mix, dgate_mix = postnorm_bwd(dx2, y0m, g_post_mix, mod, 5, 1.0, nb, "mix_postnorm_bwd")
    dw_o_a = mm_tn(m_a, dy0m, BF16, "mix_dwo_a")
    dw_o_b = mm_tn(m_b, dy0m, BF16, "mix_dwo_b")
    dm_a = mm_rows([(dy0m, w_o_a)], True, F32, "mix_dm_a")
    dm_b = mm_rows([(dy0m, w_o_b)], True, F32, "mix_dm_b")
    do_a, dg_out_a = prenorm_bwd(dm_a, o_a, g_out_a, None, None, None, nb, "out_norm_a_bwd")
    do_b, dg_out_b = prenorm_bwd(dm_b, o_b, g_out_b, None, None, None, nb, "out_norm_b_bwd")
    (dq_a, dk_a, dv_a), (g_ff2,) = attn_bwd(proj, 0, proj, pa, proj, 2 * pa, o_a, lse_a, do_a, tab_a, None, nb,
                                            "attn_a_bwd", side=(ff2_blocks, False), rope_tabs=tables)
    dq_b, dk_b, dv_b, dcb, drow = attn_bwd(proj, 3 * pa, proj, 4 * pa, proj, 5 * pa, o_b, lse_b, do_b, tab_b, colbias, nb,
                                           "attn_b_bwd")
    dz_t, db_forget = fox_gate_bwd(dcb.reshape(nb, N_HEADS_B, s), drow.reshape(nb, N_HEADS_B, s), ft, bf_col,
                                   "fox_gate_bwd")
    dz = jnp.pad(dz_t.transpose(0, 2, 1).reshape(t, N_HEADS_B), ((0, 0), (0, LANES - N_HEADS_B))).astype(BF16)
    pieces = [dq_a, dk_a, dv_a, dq_b, dk_b, dv_b]
    w_pieces = [w_qkv_t[i * WIDTH_A:(i + 1) * WIDTH_A] for i in range(6)]
    dh2_pairs = list(zip(pieces, w_pieces)) + [(dz, w_f_t)]
    dw_in_t = jnp.concatenate([mm_tn(p, h2, BF16, f"mix_dwin_{i}") for i, p in enumerate(pieces)]
                              + [mm_tn(dz, h2, BF16, "mix_dwin_f")[:N_HEADS_B]], axis=0)
    dx1, dg_pre_mix, dsc_mix, dsh_mix = prenorm_bwd(dh2_pairs, x1, g_pre_mix, mod, 4, dx2, nb, "mix_dh_prenorm_bwd",
                                                    ts=DH_ROWS)

    g_in = _rows_to_blocks(dw_in_t, dw_in_t.shape[0])[:, None]
    g_out = _rows_to_blocks(jnp.concatenate([dw_o_a, dw_o_b], axis=0), d)[:, None]
    dx0, gr1, (g_in, g_out) = _ffn_backward(dx1, saved1, mod, g_pre_ff1, g_post_ff1, wg1, wu1, wd1, 0, nb, "ff1",
                                            side=([g_in, g_out], False), chain=True)
    grad_x = dx0.reshape(nb, s, d)

    dmod =jnp.concatenate(list(gr1["mod"]) + [dsh_mix, dsc_mix, dgate_mix] + list(gr2["mod"]), axis=1)
    small = _pack_small(dict(g_pre_ff1=gr1["g_pre"], g_post_ff1=gr1["g_post"], g_pre_mix=dg_pre_mix,
                             g_post_mix=dg_post_mix, g_out_a=dg_out_a, g_out_b=dg_out_b, g_pre_ff2=gr2["g_pre"],
                             g_post_ff2=gr2["g_post"], b_forget=db_forget))
    dmod_all, small_all = all_gather([dmod.reshape(nb, N_MOD * d), small], "gather_small_grads")
    dmod_all = dmod_all.reshape(nbg, N_MOD * d)

    res = {}
    def adamw_t(parts, group, n):
        return tuple(_t(r) for r in adamw(parts, group, _t(weights[n])[0], _t(mom_m[n])[0], _t(mom_v[n])[0], f"adamw_{n}"))

    res["w_ff1_gate"] = adamw_t(gr1["wg"], 0, "w_ff1_gate")
    res["w_ff1_up"] = adamw_t(gr1["wu"], 0, "w_ff1_up")
    res["w_ff2_gate"] = adamw_t(g_ff2, 0, "w_ff2_gate")
    res["w_ff2_up"] = adamw_t(g_ff2, 1, "w_ff2_up")
    res["w_ff1_down"] = adamw(gr1["wd"], 0, w_ff1_down[0], m_w_ff1_down[0], v_w_ff1_down[0], "adamw_ff1_down")
    res["w_ff2_down"] = adamw(g_ff2, 2, w_ff2_down[0], m_w_ff2_down[0], v_w_ff2_down[0], "adamw_ff2_down")
    res["w_in"] = adamw_t(g_in, 0, "w_in")
    res["w_out"] = adamw(g_out, 0, w_out[0], m_w_out[0], v_w_out[0], "adamw_out")
    dmod_cols = lax.dynamic_slice(dmod_all, (0, me * ada_cols), (nbg, ada_cols))
    dw_ada = ada_bwd(c_all, dmod_cols, "ada_bwd")
    res["w_ada"] = adamw(dw_ada[None, None], 0, w_ada[0], m_w_ada[0], v_w_ada[0], "adamw_ada", tr=256)
    res["b_ada"] = adamw(dmod_all[:, None, None], 0, b_ada, m_b_ada, v_b_ada, "adamw_b_ada")
    sizes = {n: weights[n].shape[1] for n in SMALL_ORDER}
    small_res = adamw(small_all[:, None], 0, _pack_small(weights), _pack_small(mom_m), _pack_small(mom_v), "adamw_small")
    small_res = [_unpack_small(r, sizes) for r in small_res]
    for n in SMALL_ORDER:
        res[n] = tuple(r[n] for r in small_res)

    outs = [loss, grad_x]
    for kind in range(4):
        for n in order:
            a = res[n][kind]
            outs.append(a.reshape(weights[n].shape))
    return tuple(outs)
```

```python
import functools

import jax
import jax.numpy as jnp
from jax import lax
from jax.experimental import pallas as pl
from jax.experimental.pallas import tpu as pltpu

F32 = jnp.float32
BF16 = jnp.bfloat16

D_MODEL = 1024
HEAD_DIM = 64
N_HEADS_A = 8
N_HEADS_B = 8
WIDTH_A = N_HEADS_A * HEAD_DIM
WIDTH_B = N_HEADS_B * HEAD_DIM
DILATED_PATTERNS = ((128, 1), (512, 4), (2048, 16))
ROT_DIM = HEAD_DIM // 4
ROPE_THETA = 500000.0
D_FF = 2752
D_FF_PAD = 2816
N_MOD = 9
EPS = 1e-6
ATTN_SCALE = HEAD_DIM ** -0.5
NEG = -1e30
N_DEV = 8
LANES = 128
HEADS_PER_STEP = LANES // HEAD_DIM

ADAM_LR = 0.001
ADAM_B1 = 0.9
ADAM_B2 = 0.999
ADAM_EPS = 1e-08
ADAM_WD = 0.01
ADAM_STEP = 10

VMEM_LIMIT = 56 * 1024 * 1024
MESH = pl.DeviceIdType.MESH

NT_DIMS = (((1,), (1,)), ((), ()))
TN_DIMS = (((0,), (0,)), ((), ()))
NN_DIMS = (((1,), (0,)), ((), ()))


def _place():
    return lax.axis_index("x"), lax.axis_index("y"), lax.axis_index("c")


def _slot(p):
    return 4 * p[0] + 2 * p[1] + p[2]


def _direct_copies(ins, outs, send_sems, recv_sems, local_sems, gather):
    x, y, c = _place()
    me = (x, y, c)
    flip = lambda v, bit: 1 - v if bit else v
    peers = [(flip(x, k & 4), flip(y, k & 2), flip(c, k & 1)) for k in range(1, N_DEV)]
    local, sends, recvs = [], [], []
    for a in range(len(ins)):
        mine = ins[a] if gather else ins[a].at[_slot(me)]
        local.append(pltpu.make_async_copy(mine, outs[a].at[_slot(me)], local_sems.at[a]))
        for k, peer in enumerate(peers):
            sems = dict(send_sem=send_sems.at[a * 7 + k], recv_sem=recv_sems.at[a * 7 + k], device_id=peer,
                        device_id_type=MESH)
            sends.append(pltpu.make_async_remote_copy(
                src_ref=ins[a] if gather else ins[a].at[_slot(peer)], dst_ref=outs[a].at[_slot(me)], **sems))
            recvs.append(pltpu.make_async_remote_copy(src_ref=mine, dst_ref=outs[a].at[_slot(peer)], **sems))
    return local, sends, recvs


def _comm_scratch(n):
    return [pltpu.SemaphoreType.DMA((7 * n,)), pltpu.SemaphoreType.DMA((7 * n,)), pltpu.SemaphoreType.DMA((n,))]


def _pcall(body, side=None, **kw):
    if side is None:
        return pl.pallas_call(body, **kw)
    arrs, gather = side
    n = len(arrs)
    grid = kw["grid"]
    in_specs = list(kw["in_specs"])
    single = not isinstance(kw["out_specs"], (list, tuple))
    out_specs = [kw["out_specs"]] if single else list(kw["out_specs"])
    out_shape = [kw["out_shape"]] if single else list(kw["out_shape"])
    scratch = list(kw.get("scratch_shapes", []))
    n_in, n_out, n_scr = len(in_specs), len(out_specs), len(scratch)
    hbm = pl.BlockSpec(memory_space=pl.ANY)

    def hosted(*refs):
        pos = [0]

        def take(k):
            pos[0] += k
            return refs[pos[0] - k:pos[0]]

        ins, s_ins, outs, s_outs, scr, sems = take(n_in), take(n), take(n_out), take(n), take(n_scr), take(3)
        ids = [pl.program_id(i) for i in range(len(grid))]
        first = functools.reduce(jnp.logical_and, [i == 0 for i in ids])
        last = functools.reduce(jnp.logical_and, [i == g - 1 for i, g in zip(ids, grid)])

        @pl.when(first)
        def _():
            local, sends, _ = _direct_copies(s_ins, s_outs, *sems, gather)
            for cp in local + sends:
                cp.start()

        body(*ins, *outs, *scr)

        @pl.when(last)
        def _():
            local, sends, recvs = _direct_copies(s_ins, s_outs, *sems, gather)
            for cp in recvs:
                cp.wait_recv()
            for cp in sends:
                cp.wait_send()
            for cp in local:
                cp.wait()

    kw.update(in_specs=in_specs + [hbm] * n, out_specs=out_specs + [hbm] * n,
              out_shape=out_shape + [jax.ShapeDtypeStruct(((N_DEV,) + a.shape) if gather else a.shape, a.dtype)
                                     for a in arrs],
              scratch_shapes=scratch + _comm_scratch(n))
    call = pl.pallas_call(hosted, **kw)

    def run(*args):
        res = call(*args, *arrs)
        main = res[0] if single else list(res[:n_out])
        return main, list(res[n_out:])

    return run


def _params(sem=None, **kw):
    if sem is not None:
        kw["dimension_semantics"] = sem
    return pltpu.CompilerParams(vmem_limit_bytes=VMEM_LIMIT, **kw)


def _rotate(xv, c, sp, sm, transpose):
    width = xv.shape[1]
    half = ROT_DIM // 2
    if transpose:
        return xv * c + pltpu.roll(xv * sp, width - half, 1) + pltpu.roll(xv * sm, half, 1)
    return xv * c + pltpu.roll(xv, half, 1) * sp + pltpu.roll(xv, width - half, 1) * sm


def mm_rows(pairs, trans_b, out_dtype, name, tm=512, side=None, rope=None):
    n = len(pairs)
    m = pairs[0][0].shape[0]
    n_out = pairs[0][1].shape[0 if trans_b else 1]
    dims = NT_DIMS if trans_b else NN_DIMS

    def body(*refs):
        o_ref = refs[-1]
        acc = None
        for a_ref, b_ref in zip(refs[:n], refs[n:2 * n]):
            d = lax.dot_general(a_ref[...], b_ref[...], dims, preferred_element_type=F32)
            acc = d if acc is None else acc + d
        if rope is None:
            o_ref[...] = acc.astype(o_ref.dtype)
        else:
            width = rope[1]
            c, sp, sm = (jnp.concatenate([r[...]] * (width // LANES), axis=1) for r in refs[2 * n:2 * n + 3])
            o_ref[:, :width] = _rotate(acc[:, :width], c, sp, sm, False).astype(o_ref.dtype)
            o_ref[:, width:] = acc[:, width:].astype(o_ref.dtype)

    in_specs = [pl.BlockSpec((tm, a.shape[1]), lambda i: (i, 0)) for a, _ in pairs]
    in_specs += [pl.BlockSpec(b.shape, lambda i: (0, 0)) for _, b in pairs]
    args = [a for a, _ in pairs] + [b for _, b in pairs]
    if rope is not None:
        in_specs += [pl.BlockSpec((tm, LANES), lambda i: (i, 0))] * 3
        args += list(rope[0])
    return _pcall(
        body, side=side, name=name, grid=(m // tm,), in_specs=in_specs,
        out_specs=pl.BlockSpec((tm, n_out), lambda i: (i, 0)),
        out_shape=jax.ShapeDtypeStruct((m, n_out), out_dtype),
        compiler_params=_params(("arbitrary",)),
    )(*args)


DH_ROWS = 256
TN_TOKENS = 2048
TN_OUT_ELEMS = 2 * 1024 * 1024


def mm_tn(a, b, out_dtype, name, side=None):
    t, ka = a.shape
    n_out = b.shape[1]
    tk = min(TN_TOKENS, t)
    tka = ka // 2 if ka * n_out > TN_OUT_ELEMS else ka
    tn = n_out
    steps = t // tk

    def body(a_ref, b_ref, o_ref, acc_ref):
        k = pl.program_id(2)
        d = lax.dot_general(a_ref[...], b_ref[...], TN_DIMS, preferred_element_type=F32)

        @pl.when(k == 0)
        def _():
            acc_ref[...] = d

        @pl.when(k > 0)
        def _():
            acc_ref[...] += d

        @pl.when(k == steps - 1)
        def _():
            o_ref[...] = acc_ref[...].astype(o_ref.dtype)

    return _pcall(
        body, side=side, name=name, grid=(ka // tka, n_out // tn, steps),
        in_specs=[pl.BlockSpec((tk, tka), lambda i, j, k: (k, i)), pl.BlockSpec((tk, tn), lambda i, j, k: (k, j))],
        out_specs=pl.BlockSpec((tka, tn), lambda i, j, k: (i, j)),
        out_shape=jax.ShapeDtypeStruct((ka, n_out), out_dtype),
        scratch_shapes=[pltpu.VMEM((tka, tn), F32)],
        compiler_params=_params(("arbitrary", "arbitrary", "arbitrary")),
    )(a, b)


def _col_chunks(width, chunk=512):
    return [slice(c, min(c + chunk, width)) for c in range(0, width, chunk)]


def _sigmoid(x):
    return 1.0 / (1.0 + jnp.exp(-x))


def ffn_up(h, wgt, wut, name, tm=256, tn=D_FF_PAD, side=None):
    t, d = h.shape
    fp = wgt.shape[0]

    def body(h_ref, wg_ref, wu_ref, g_ref, u_ref, a_ref):
        hv = h_ref[...]

        def finish(cols, g, u):
            g_ref[:, cols] = g.astype(BF16)
            u_ref[:, cols] = u.astype(BF16)
            a_ref[:, cols] = (g * _sigmoid(g) * u).astype(BF16)

        pending = None
        for cols in _col_chunks(tn):
            g = lax.dot_general(hv, wg_ref[cols, :], NT_DIMS, preferred_element_type=F32)
            u = lax.dot_general(hv, wu_ref[cols, :], NT_DIMS, preferred_element_type=F32)
            if pending is not None:
                finish(*pending)
            pending = (cols, g, u)
        finish(*pending)

    w_spec = pl.BlockSpec((tn, d), lambda j, i: (j, 0))
    o_spec = pl.BlockSpec((tm, tn), lambda j, i: (i, j))
    o_shape = jax.ShapeDtypeStruct((t, fp), BF16)
    return _pcall(
        body, side=side, name=name, grid=(fp // tn, t // tm),
        in_specs=[pl.BlockSpec((tm, d), lambda j, i: (i, 0)), w_spec, w_spec],
        out_specs=[o_spec, o_spec, o_spec], out_shape=[o_shape, o_shape, o_shape],
        compiler_params=_params(("arbitrary", "arbitrary")),
    )(h, wgt, wut)


def ffn_down_bwd(dy0, wd, gate, up, name, tm=256, tn=D_FF_PAD, side=None):
    t, d = dy0.shape
    fp = wd.shape[0]

    def body(dy_ref, wd_ref, g_ref, u_ref, dg_ref, du_ref):
        dyv = dy_ref[...]

        def finish(cols, dact):
            g = g_ref[:, cols].astype(F32)
            u = u_ref[:, cols].astype(F32)
            sg = _sigmoid(g)
            du_ref[:, cols] = (dact * g * sg).astype(BF16)
            dg_ref[:, cols] = (dact * u * (sg * (1.0 + g * (1.0 - sg)))).astype(BF16)

        pending = None
        for cols in _col_chunks(tn):
            dact = lax.dot_general(dyv, wd_ref[cols, :], NT_DIMS, preferred_element_type=F32)
            if pending is not None:
                finish(*pending)
            pending = (cols, dact)
        finish(*pending)

    t_spec = pl.BlockSpec((tm, tn), lambda j, i: (i, j))
    o_shape = jax.ShapeDtypeStruct((t, fp), BF16)
    return _pcall(
        body, side=side, name=name, grid=(fp // tn, t // tm),
        in_specs=[pl.BlockSpec((tm, d), lambda j, i: (i, 0)), pl.BlockSpec((tn, d), lambda j, i: (j, 0)), t_spec, t_spec],
        out_specs=[t_spec, t_spec], out_shape=[o_shape, o_shape],
        compiler_params=_params(("arbitrary", "arbitrary")),
    )(dy0, wd, gate, up)


def _row_specs(dx, ts, ns):
    return pl.BlockSpec((ts, dx), lambda b, s: (b * ns + s, 0))


def _mod_spec():
    return pl.BlockSpec((1, N_MOD, D_MODEL), lambda b, s: (b, 0, 0))


def _vec_spec(dx):
    return pl.BlockSpec((1, dx), lambda b, s: (0, 0))


def prenorm_fwd(x, g, mod, i_shift, i_scale, nb, name, ts=1024):
    t, dx = x.shape
    ts = min(ts, t // nb)
    ns = t // nb // ts

    def body(*refs):
        if mod is None:
            x_ref, g_ref, h_ref = refs
        else:
            x_ref, g_ref, mod_ref, h_ref = refs
        xv = x_ref[...]
        r = lax.rsqrt(jnp.mean(xv * xv, axis=-1, keepdims=True) + EPS)
        h = xv * r * g_ref[...]
        if mod is not None:
            h = h * (1.0 + mod_ref[0, i_scale:i_scale + 1, :]) + mod_ref[0, i_shift:i_shift + 1, :]
        h_ref[...] = h.astype(BF16)

    in_specs = [_row_specs(dx, ts, ns), _vec_spec(dx)]
    args = [x, g]
    if mod is not None:
        in_specs.append(_mod_spec())
        args.append(mod)
    return _pcall(
        body, name=name, grid=(nb, ns), in_specs=in_specs, out_specs=_row_specs(dx, ts, ns),
        out_shape=jax.ShapeDtypeStruct((t, dx), BF16), compiler_params=_params(("arbitrary", "arbitrary")),
    )(*args)


def prenorm_bwd(dh, x, g, mod, i_scale, dres, nb, name, ts=512, side=None):
    t, dx = x.shape
    ts = min(ts, t // nb)
    ns = t // nb // ts
    has_mod = mod is not None
    has_res = dres is not None
    pairs = dh if isinstance(dh, list) else None
    n_mm = 0 if pairs is None else len(pairs)

    def body(*refs):
        refs = list(refs)
        if pairs is None:
            dhv = refs[0][...].astype(F32)
            refs = refs[1:]
        else:
            dhv = None
            for a_ref, b_ref in zip(refs[:n_mm], refs[n_mm:2 * n_mm]):
                d = jnp.dot(a_ref[...], b_ref[...], preferred_element_type=F32)
                dhv = d if dhv is None else dhv + d
            refs = refs[2 * n_mm:]
        x_ref, g_ref = refs[:2]
        pos = 2
        mod_ref = dres_ref = None
        if has_mod:
            mod_ref = refs[pos]
            pos += 1
        if has_res:
            dres_ref = refs[pos]
            pos += 1
        dx_ref, dg_ref = refs[pos], refs[pos + 1]
        b, s = pl.program_id(0), pl.program_id(1)
        xv = x_ref[...]
        gv = g_ref[...]
        r = lax.rsqrt(jnp.mean(xv * xv, axis=-1, keepdims=True) + EPS)
        xhat = xv * r
        dn = dhv
        if has_mod:
            dsc_ref, dsh_ref = refs[pos + 2], refs[pos + 3]
            dn = dhv * (1.0 + mod_ref[0, i_scale:i_scale + 1, :])
            dsc = jnp.sum(dhv * xhat * gv, axis=0, keepdims=True)[None]
            dsh = jnp.sum(dhv, axis=0, keepdims=True)[None]

            @pl.when(s == 0)
            def _():
                dsc_ref[...] = dsc
                dsh_ref[...] = dsh

            @pl.when(s > 0)
            def _():
                dsc_ref[...] += dsc
                dsh_ref[...] += dsh

        dg = jnp.sum(dn * xhat, axis=0, keepdims=True)
        first = jnp.logical_and(b == 0, s == 0)

        @pl.when(first)
        def _():
            dg_ref[...] = dg

        @pl.when(jnp.logical_not(first))
        def _():
            dg_ref[...] += dg

        dxhat = dn * gv
        dxv = r * (dxhat - xhat * jnp.mean(dxhat * xhat, axis=-1, keepdims=True))
        if has_res:
            dxv = dxv + dres_ref[...]
        dx_ref[...] = dxv

    row = _row_specs(dx, ts, ns)
    if pairs is None:
        in_specs, args = [row], [dh]
    else:
        in_specs = [_row_specs(a.shape[1], ts, ns) for a, _ in pairs]
        in_specs += [pl.BlockSpec(b.shape, lambda b_, s_: (0, 0)) for _, b in pairs]
        args = [a for a, _ in pairs] + [b for _, b in pairs]
    in_specs += [row, _vec_spec(dx)]
    args += [x, g]
    if has_mod:
        in_specs.append(_mod_spec())
        args.append(mod)
    if has_res:
        in_specs.append(row)
        args.append(dres)
    out_specs = [row, _vec_spec(dx)]
    out_shape = [jax.ShapeDtypeStruct((t, dx), F32), jax.ShapeDtypeStruct((1, dx), F32)]
    if has_mod:
        bspec = pl.BlockSpec((1, 1, dx), lambda b, s: (b, 0, 0))
        out_specs += [bspec, bspec]
        out_shape += [jax.ShapeDtypeStruct((nb, 1, dx), F32)] * 2
    return _pcall(
        body, side=side, name=name, grid=(nb, ns), in_specs=in_specs, out_specs=out_specs, out_shape=out_shape,
        compiler_params=_params(("arbitrary", "arbitrary")),
    )(*args)


def postnorm_fwd(x, pairs, g, mod, i_gate, coef, nb, name, target=None, ts=512):
    t, dx = x.shape
    with_loss = target is not None
    ts = min(ts, t // nb)
    ns = t // nb // ts
    n_mm = len(pairs)

    def body(*refs):
        yv = None
        for a_ref, b_ref in zip(refs[:n_mm], refs[n_mm:2 * n_mm]):
            d = jnp.dot(a_ref[...], b_ref[...], preferred_element_type=F32)
            yv = d if yv is None else yv + d
        refs = refs[2 * n_mm:]
        x_ref, g_ref, mod_ref = refs[:3]
        refs[-1][...] = yv
        r = lax.rsqrt(jnp.mean(yv * yv, axis=-1, keepdims=True) + EPS)
        out = x_ref[...] + (coef * mod_ref[0, i_gate:i_gate + 1, :]) * (yv * r * g_ref[...])
        if not with_loss:
            refs[3][...] = out
            return
        t_ref, dx_ref, loss_ref = refs[3:6]
        b, s = pl.program_id(0), pl.program_id(1)
        err = out - t_ref[...]
        dx_ref[...] = err * (1.0 / dx)
        part = (0.5 / dx) * jnp.sum(jnp.sum(err * err, axis=1, keepdims=True), axis=0, keepdims=True)
        first = jnp.logical_and(b == 0, s == 0)

        @pl.when(first)
        def _():
            loss_ref[...] = part

        @pl.when(jnp.logical_not(first))
        def _():
            loss_ref[...] += part

    row = _row_specs(dx, ts, ns)
    in_specs = [_row_specs(a.shape[1], ts, ns) for a, _ in pairs]
    in_specs += [pl.BlockSpec(b.shape, lambda b_, s_: (0, 0)) for _, b in pairs]
    in_specs += [row, _vec_spec(dx), _mod_spec()]
    args = [a for a, _ in pairs] + [b for _, b in pairs] + [x, g, mod]
    row_shape = jax.ShapeDtypeStruct((t, dx), F32)
    out_specs, out_shape = [row, row], [row_shape, row_shape]
    if with_loss:
        in_specs.append(row)
        args.append(target)
        out_specs = [row, pl.BlockSpec((1, 1), lambda b, s: (0, 0)), row]
        out_shape = [row_shape, jax.ShapeDtypeStruct((1, 1), F32), row_shape]
    return _pcall(
        body, name=name, grid=(nb, ns), in_specs=in_specs, out_specs=out_specs, out_shape=out_shape,
        compiler_params=_params(("arbitrary", "arbitrary")),
    )(*args)


def postnorm_bwd(dxo, y0, g, mod, i_gate, coef, nb, name, ts=1024):
    t, dx = y0.shape
    ts = min(ts, t // nb)
    ns = t // nb // ts

    def body(d_ref, y_ref, g_ref, mod_ref, dy_ref, dg_ref, dgate_ref):
        b, s = pl.program_id(0), pl.program_id(1)
        yv = y_ref[...]
        dv = d_ref[...]
        gv = g_ref[...]
        r = lax.rsqrt(jnp.mean(yv * yv, axis=-1, keepdims=True) + EPS)
        yhat = yv * r
        dgate = jnp.sum(dv * (coef * (yhat * gv)), axis=0, keepdims=True)[None]
        dyn = dv * (coef * mod_ref[0, i_gate:i_gate + 1, :])
        dg = jnp.sum(dyn * yhat, axis=0, keepdims=True)
        dyhat = dyn * gv
        dy_ref[...] = (r * (dyhat - yhat * jnp.mean(dyhat * yhat, axis=-1, keepdims=True))).astype(BF16)

        @pl.when(s == 0)
        def _():
            dgate_ref[...] = dgate

        @pl.when(s > 0)
        def _():
            dgate_ref[...] += dgate

        first = jnp.logical_and(b == 0, s == 0)

        @pl.when(first)
        def _():
            dg_ref[...] = dg

        @pl.when(jnp.logical_not(first))
        def _():
            dg_ref[...] += dg

    row = _row_specs(dx, ts, ns)
    return _pcall(
        body, name=name, grid=(nb, ns), in_specs=[row, row, _vec_spec(dx), _mod_spec()],
        out_specs=[row, _vec_spec(dx), pl.BlockSpec((1, 1, dx), lambda b, s: (b, 0, 0))],
        out_shape=[jax.ShapeDtypeStruct((t, dx), BF16), jax.ShapeDtypeStruct((1, dx), F32),
                   jax.ShapeDtypeStruct((nb, 1, dx), F32)],
        compiler_params=_params(("arbitrary", "arbitrary")),
    )(dxo, y0, g, mod)


def rope_tables(positions):
    inv_freq = ROPE_THETA ** (-jnp.arange(0, ROT_DIM, 2, dtype=F32) / ROT_DIM)
    ang = positions.astype(F32).reshape(-1, 1) * inv_freq
    cos, sin = jnp.cos(ang), jnp.sin(ang)
    half = ROT_DIM // 2
    z = lambda n: jnp.zeros((ang.shape[0], n), F32)
    c = jnp.concatenate([cos, cos, jnp.ones((ang.shape[0], HEAD_DIM - ROT_DIM), F32)], axis=1)
    sp = jnp.concatenate([z(half), sin, z(HEAD_DIM - ROT_DIM)], axis=1)
    sm = jnp.concatenate([-sin, z(HEAD_DIM - half)], axis=1)
    return tuple(jnp.tile(a, (1, HEADS_PER_STEP)) for a in (c, sp, sm))


def _scan_lanes(x, reverse):
    n = x.shape[-1]
    lane = lax.broadcasted_iota(jnp.int32, x.shape, x.ndim - 1)
    k = 1
    while k < n:
        if reverse:
            x = x + jnp.where(lane < n - k, pltpu.roll(x, n - k, x.ndim - 1), 0.0)
        else:
            x = x + jnp.where(lane >= k, pltpu.roll(x, k, x.ndim - 1), 0.0)
        k *= 2
    return x


def _log_sigmoid(z):
    return jnp.minimum(z, 0.0) - jnp.log(1.0 + jnp.exp(-jnp.abs(z)))


def fox_gate_fwd(ft, b_forget, name):
    nb, nh, s = ft.shape

    def body(f_ref, b_ref, o_ref):
        z = f_ref[0] + b_ref[...]
        o_ref[0] = -_scan_lanes(_log_sigmoid(z), False)

    spec = pl.BlockSpec((1, nh, s), lambda b: (b, 0, 0))
    return _pcall(
        body, name=name, grid=(nb,), in_specs=[spec, pl.BlockSpec((nh, 1), lambda b: (0, 0))], out_specs=spec,
        out_shape=jax.ShapeDtypeStruct((nb, nh, s), F32), compiler_params=_params(("arbitrary",)),
    )(ft, b_forget)


def fox_gate_bwd(dcb, drow, ft, b_forget, name):
    nb, nh, s = ft.shape

    def body(d_ref, r_ref, f_ref, b_ref, dz_ref, db_ref):
        b = pl.program_id(0)
        z = f_ref[0] + b_ref[...]
        dlf = _scan_lanes(r_ref[0] - d_ref[0], True)
        dz = dlf * _sigmoid(-z)
        dz_ref[0] = dz
        db = jnp.sum(dz, axis=1, keepdims=True)

        @pl.when(b == 0)
        def _():
            db_ref[...] = db

        @pl.when(b > 0)
        def _():
            db_ref[...] += db

    spec = pl.BlockSpec((1, nh, s), lambda b: (b, 0, 0))
    vec = pl.BlockSpec((nh, 1), lambda b: (0, 0))
    return _pcall(
        body, name=name, grid=(nb,), in_specs=[spec, spec, spec, vec], out_specs=[spec, vec],
        out_shape=[jax.ShapeDtypeStruct((nb, nh, s), F32), jax.ShapeDtypeStruct((nh, 1), F32)],
        compiler_params=_params(("arbitrary",)),
    )(dcb, drow, ft, b_forget)


ATTN_TQ = 512
ATTN_TK = 512
ONES_ROWS = 16


def _rows_to_cols(rows):
    tile = jnp.concatenate([jnp.broadcast_to(rw, (HEAD_DIM, rw.shape[1])) for rw in rows], axis=0)
    return tile.T


def _block_delta(s, tq, tk):
    off = jnp.arange(s // tk) - (tq // tk - 1)
    return off[:, None, None] * tk + jnp.arange(tq)[None, None, :] - jnp.arange(tk)[None, :, None]


def dilated_table(s, tq, tk):
    delta = _block_delta(s, tq, tk)
    count = jnp.zeros(delta.shape, F32)
    for window, dil in DILATED_PATTERNS:
        count = count + ((delta >= 0) & (delta <= window) & (delta % dil == 0)).astype(F32)
    return jnp.where(count > 0, jnp.log(jnp.maximum(count, 1.0)), NEG)


def causal_table(s, tq, tk):
    return jnp.where(_block_delta(s, tq, tk) >= 0, 0.0, NEG).astype(F32)


def attn_fwd(q_arr, q_off, k_arr, k_off, v_arr, v_off, table, colbias, nb, name, side=None):
    t = q_arr.shape[0]
    s = t // nb
    tk, tq = table.shape[1:]
    nq, nk, r = s // tq, s // tk, tq // tk
    npairs = WIDTH_A // LANES
    use_cb = colbias is not None

    def body(*refs):
        refs = list(refs)
        q_ref, k_ref, v_ref, tab_ref = refs[:4]
        cb_ref = refs[4] if use_cb else None
        tail = refs[-(HEADS_PER_STEP + int(use_cb)):]
        acc_s = tail[:HEADS_PER_STEP]
        cbc_s = tail[-1] if use_cb else None
        o_ref, lse_ref, vt_s = refs[-3 - len(tail):-len(tail)]
        qi = pl.program_id(2)

        heads = [slice(h * HEAD_DIM, (h + 1) * HEAD_DIM) for h in range(HEADS_PER_STEP)]

        @pl.when(qi == 0)
        def _():
            for cblk in range(nk):
                vt = v_ref[cblk * tk:(cblk + 1) * tk, :].astype(F32).T.astype(BF16)
                for h, hs in enumerate(heads):
                    vt_s[cblk, h, 0:HEAD_DIM, :] = vt[hs, :]
                    vt_s[cblk, h, HEAD_DIM:, :] = jnp.ones((ONES_ROWS, tk), BF16)
                if use_cb:
                    cbc_s[cblk] = _rows_to_cols([cb_ref[0, h, cblk] for h in range(HEADS_PER_STEP)])

        qt_all = (q_ref[...].astype(F32) * ATTN_SCALE).T.astype(BF16)
        qts = [qt_all[hs, :] for hs in heads]
        for a in acc_s:
            a[...] = jnp.zeros_like(a)

        def step(kb, carry):
            ks = pl.multiple_of(kb * tk, tk)
            tab = tab_ref[qi * r + (r - 1) - kb]
            sts = []
            for h, hs in enumerate(heads):
                st = jnp.dot(k_ref[pl.ds(ks, tk), hs], qts[h], preferred_element_type=F32) + tab
                if use_cb:
                    st = st + cbc_s[kb, :, h * HEAD_DIM:h * HEAD_DIM + 1]
                sts.append(st)
            m_new = [jnp.maximum(carry[h], jnp.max(sts[h], axis=0, keepdims=True)) for h in range(HEADS_PER_STEP)]
            for h in range(HEADS_PER_STEP):
                pt = jnp.exp(sts[h] - m_new[h]).astype(BF16)
                acc_s[h][...] = (jnp.exp(carry[h] - m_new[h]) * acc_s[h][...]
                                 + jnp.dot(vt_s[kb, h], pt, preferred_element_type=F32))
            return tuple(m_new)

        fin = lax.fori_loop(0, (qi + 1) * r, step, tuple(jnp.full((1, tq), NEG, F32) for _ in heads))
        outs = []
        for h in range(HEADS_PER_STEP):
            l = acc_s[h][HEAD_DIM:HEAD_DIM + 1, :]
            outs.append(acc_s[h][0:HEAD_DIM, :] / l)
            lse_ref[0, h, 0] = fin[h] + jnp.log(l)
        o_ref[...] = jnp.concatenate(outs, axis=0).T

    def seq_spec(off):
        return pl.BlockSpec((s, LANES), lambda b, j, i: (b, off + j))

    in_specs = [pl.BlockSpec((tq, LANES), lambda b, j, i: (b * nq + i, q_off + j)), seq_spec(k_off), seq_spec(v_off),
                pl.BlockSpec(table.shape, lambda b, j, i: (0, 0, 0))]
    args = [q_arr, k_arr, v_arr, table]
    if use_cb:
        in_specs.append(pl.BlockSpec((1, HEADS_PER_STEP, nk, 1, tk), lambda b, j, i: (b, j, 0, 0, 0)))
        args.append(colbias)
    n_heads = npairs * HEADS_PER_STEP
    return _pcall(
        body, side=side, name=name, grid=(nb, npairs, nq), in_specs=in_specs,
        out_specs=[pl.BlockSpec((tq, LANES), lambda b, j, i: (b * nq + i, j)),
                   pl.BlockSpec((1, HEADS_PER_STEP, 1, 1, tq), lambda b, j, i: (b, j, i, 0, 0))],
        out_shape=[jax.ShapeDtypeStruct((t, npairs * LANES), F32), jax.ShapeDtypeStruct((nb, n_heads, nq, 1, tq), F32)],
        scratch_shapes=[pltpu.VMEM((nk, HEADS_PER_STEP, HEAD_DIM + ONES_ROWS, tk), BF16)]
        + [pltpu.VMEM((HEAD_DIM + ONES_ROWS, tq), F32)] * HEADS_PER_STEP
        + ([pltpu.VMEM((nk, tk, LANES), F32)] if use_cb else []),
        compiler_params=_params(("arbitrary", "arbitrary", "arbitrary")),
    )(*args)


def attn_bwd(q_arr, q_off, k_arr, k_off, v_arr, v_off, o_arr, lse_arr, do_arr, table, colbias, nb, name, side=None,
             rope_tabs=None):
    t = q_arr.shape[0]
    s = t // nb
    tk, tq = table.shape[1:]
    nq, nk, r = s // tq, s // tk, tq // tk
    npairs = WIDTH_A // LANES
    use_cb = colbias is not None

    def body(*refs):
        refs = list(refs)
        q_ref, k_ref, v_ref, o_ref, lse_ref, do_ref, tab_ref = refs[:7]
        pos = 7
        cb_ref = None
        if use_cb:
            cb_ref = refs[pos]
            pos += 1
        rope_refs = None
        if rope_tabs is not None:
            rope_refs = refs[pos:pos + 3]
            pos += 3
        dq_ref, dk_ref, dv_ref = refs[pos:pos + 3]
        pos += 3
        dcb_ref = drow_ref = None
        if use_cb:
            dcb_ref, drow_ref = refs[pos:pos + 2]
            pos += 2
        kt_s, dkt_s, dvt_s = refs[pos:pos + 3]
        dqt_s = refs[pos + 3:pos + 3 + HEADS_PER_STEP]
        dcb_s, cbc_s = refs[pos + 3 + HEADS_PER_STEP:pos + 5 + HEADS_PER_STEP] if use_cb else (None, None)

        heads = [slice(h * HEAD_DIM, (h + 1) * HEAD_DIM) for h in range(HEADS_PER_STEP)]
        for cblk in range(nk):
            kt_s[cblk] = k_ref[cblk * tk:(cblk + 1) * tk, :].astype(F32).T.astype(BF16)
        dkt_s[...] = jnp.zeros_like(dkt_s)
        dvt_s[...] = jnp.zeros_like(dvt_s)
        if use_cb:
            dcb_s[...] = jnp.zeros_like(dcb_s)
            for cblk in range(nk):
                cbc_s[cblk] = _rows_to_cols([cb_ref[0, h, cblk] for h in range(HEADS_PER_STEP)])
        ones = jnp.ones((8, HEAD_DIM), BF16)

        def q_loop(qi, carry):
            qs = pl.multiple_of(qi * tq, tq)
            q_all = (q_ref[pl.ds(qs, tq), :].astype(F32) * ATTN_SCALE)
            do_all = do_ref[pl.ds(qs, tq), :]
            qt_all = q_all.T.astype(BF16)
            dot_all = do_all.T.astype(BF16)
            qt, dot, lse, dsum = [], [], [], []
            for h, hs in enumerate(heads):
                qt.append(qt_all[hs, :])
                dot.append(dot_all[hs, :])
                lse.append(lse_ref[0, h, qi])
                prod = do_all[:, hs] * o_ref[pl.ds(qs, tq), hs]
                hi = prod.astype(BF16)
                lo = (prod - hi.astype(F32)).astype(BF16)
                dsum.append((lax.dot_general(ones, hi, NT_DIMS, preferred_element_type=F32)
                             + lax.dot_general(ones, lo, NT_DIMS, preferred_element_type=F32))[0:1, :])
            for a in dqt_s:
                a[...] = jnp.zeros_like(a)

            def k_loop(kb, drow):
                ks = pl.multiple_of(kb * tk, tk)
                tab = tab_ref[qi * r + (r - 1) - kb]
                sts, dpts, out = [], [], []
                for h, hs in enumerate(heads):
                    st = jnp.dot(k_ref[pl.ds(ks, tk), hs], qt[h], preferred_element_type=F32) + tab
                    if use_cb:
                        st = st + cbc_s[kb, :, h * HEAD_DIM:h * HEAD_DIM + 1]
                    sts.append(st)
                    dpts.append(jnp.dot(v_ref[pl.ds(ks, tk), hs], dot[h], preferred_element_type=F32))
                for h, hs in enumerate(heads):
                    pt = jnp.exp(sts[h] - lse[h])
                    dst = pt * (dpts[h] - dsum[h])
                    dst_b = dst.astype(BF16)
                    dvt_s[h, kb] += lax.dot_general(dot[h], pt.astype(BF16), NT_DIMS, preferred_element_type=F32)
                    dkt_s[h, kb] += lax.dot_general(qt[h], dst_b, NT_DIMS, preferred_element_type=F32)
                    dqt_s[h][...] += jnp.dot(kt_s[kb, hs, :], dst_b, preferred_element_type=F32)
                    if use_cb:
                        dcb_s[h, pl.ds(ks, tk), :] += jnp.sum(dst, axis=1, keepdims=True)
                        out.append(drow[h] + jnp.sum(dst, axis=0, keepdims=True))
                    else:
                        out.append(drow[h])
                return tuple(out)

            drow = lax.fori_loop(0, (qi + 1) * r, k_loop, tuple(jnp.zeros((1, tq), F32) for _ in heads))
            dq = (jnp.concatenate([a[...] for a in dqt_s], axis=0) * ATTN_SCALE).T
            if rope_refs is not None:
                dq = _rotate(dq, *[coef[pl.ds(qs, tq), :] for coef in rope_refs], True)
            dq_ref[pl.ds(qs, tq), :] = dq.astype(dq_ref.dtype)
            if use_cb:
                for h in range(HEADS_PER_STEP):
                    drow_ref[0, h, qi] = drow[h]
            return carry

        lax.fori_loop(0, nq, q_loop, 0)
        for cblk in range(nk):
            rows = slice(cblk * tk, (cblk + 1) * tk)
            dk = jnp.concatenate([dkt_s[h, cblk] for h in range(HEADS_PER_STEP)], axis=0).T
            if rope_refs is not None:
                dk = _rotate(dk, *[coef[rows, :] for coef in rope_refs], True)
            dk_ref[rows, :] = dk.astype(dk_ref.dtype)
            dv_ref[rows, :] = jnp.concatenate([dvt_s[h, cblk] for h in range(HEADS_PER_STEP)], axis=0).T.astype(dv_ref.dtype)
            if use_cb:
                for h in range(HEADS_PER_STEP):
                    dcb_ref[0, h, cblk] = jnp.broadcast_to(dcb_s[h, rows, :], (tk, LANES)).T[0:1, :]

    def seq_spec(off):
        return pl.BlockSpec((s, LANES), lambda b, j: (b, off + j))

    row_spec = pl.BlockSpec((1, HEADS_PER_STEP, nq, 1, tq), lambda b, j: (b, j, 0, 0, 0))
    in_specs = [seq_spec(q_off), seq_spec(k_off), seq_spec(v_off), seq_spec(0), row_spec, seq_spec(0),
                pl.BlockSpec(table.shape, lambda b, j: (0, 0, 0))]
    args = [q_arr, k_arr, v_arr, o_arr, lse_arr, do_arr, table]
    width = npairs * LANES
    out_specs = [seq_spec(0)] * 3
    out_shape = [jax.ShapeDtypeStruct((t, width), BF16)] * 3
    scratch = [pltpu.VMEM((nk, LANES, tk), BF16), pltpu.VMEM((HEADS_PER_STEP, nk, HEAD_DIM, tk), F32),
               pltpu.VMEM((HEADS_PER_STEP, nk, HEAD_DIM, tk), F32)] + [pltpu.VMEM((HEAD_DIM, tq), F32)] * HEADS_PER_STEP
    if use_cb:
        cb_spec = pl.BlockSpec((1, HEADS_PER_STEP, nk, 1, tk), lambda b, j: (b, j, 0, 0, 0))
        in_specs.append(cb_spec)
        args.append(colbias)
    if rope_tabs is not None:
        in_specs += [pl.BlockSpec((s, LANES), lambda b, j: (b, 0))] * 3
        args += list(rope_tabs)
    if use_cb:
        out_specs += [cb_spec, row_spec]
        out_shape += [jax.ShapeDtypeStruct(colbias.shape, F32), jax.ShapeDtypeStruct(lse_arr.shape, F32)]
        scratch += [pltpu.VMEM((HEADS_PER_STEP, s, 1), F32), pltpu.VMEM((nk, tk, LANES), F32)]
    return _pcall(
        body, side=side, name=name, grid=(nb, npairs), in_specs=in_specs, out_specs=out_specs, out_shape=out_shape,
        scratch_shapes=scratch, compiler_params=_params(("arbitrary", "arbitrary")),
    )(*args)


def ada_fwd(c_all, w_ada, b_cols, name):
    def body(c_ref, w_ref, b_ref, o_ref):
        cv = c_ref[...]
        sc = (cv * _sigmoid(cv)).astype(BF16)
        o_ref[...] = jnp.dot(sc, w_ref[...].astype(BF16), preferred_element_type=F32) + b_ref[...]

    return _pcall(body, name=name, out_shape=jax.ShapeDtypeStruct((c_all.shape[0], w_ada.shape[1]), F32),
                  compiler_params=_params())(c_all, w_ada, b_cols)


def ada_bwd(c_all, dmod_cols, name):
    def body(c_ref, d_ref, o_ref):
        cv = c_ref[...]
        sc = (cv * _sigmoid(cv)).astype(BF16)
        o_ref[...] = lax.dot_general(sc, d_ref[...].astype(BF16), TN_DIMS, preferred_element_type=F32)

    return _pcall(body, name=name, out_shape=jax.ShapeDtypeStruct((c_all.shape[1], dmod_cols.shape[1]), F32),
                  compiler_params=_params())(c_all, dmod_cols)


def adamw(parts, group, w, m, v, name, tr=None):
    n = parts.shape[0]
    r, c = w.shape
    tr = r if tr is None else tr
    c1 = 1.0 - ADAM_B1 ** ADAM_STEP
    c2 = 1.0 - ADAM_B2 ** ADAM_STEP

    def body(p_ref, w_ref, m_ref, v_ref, g_ref, d_ref, nm_ref, nv_ref):
        g = p_ref[0, 0].astype(F32)
        for i in range(1, n):
            g = g + p_ref[i, 0].astype(F32)
        wv = w_ref[...]
        nm = ADAM_B1 * m_ref[...] + (1.0 - ADAM_B1) * g
        nv = ADAM_B2 * v_ref[...] + (1.0 - ADAM_B2) * (g * g)
        g_ref[...] = g
        nm_ref[...] = nm
        nv_ref[...] = nv
        d_ref[...] = -ADAM_LR * ((nm / c1) / (jnp.sqrt(nv / c2) + ADAM_EPS) + ADAM_WD * wv)

    spec = pl.BlockSpec((tr, c), lambda i: (i, 0))
    shape = jax.ShapeDtypeStruct((r, c), F32)
    return _pcall(
        body, name=name, grid=(r // tr,),
        in_specs=[pl.BlockSpec((n, 1, tr, c), lambda i: (0, group, i, 0)), spec, spec, spec],
        out_specs=[spec] * 4, out_shape=[shape] * 4, compiler_params=_params(("arbitrary",)),
    )(parts, w, m, v)


def all_gather(arrs, name):
    n = len(arrs)
    hbm = pl.BlockSpec(memory_space=pl.ANY)

    def body(*refs):
        ins, outs = refs[:n], refs[n:2 * n]
        send_sems, recv_sems, local_sems = refs[2 * n:]
        x, y, c = _place()
        me, sibling = (x, y, c), (x, y, 1 - c)
        chips = [(1 - x, y), (x, 1 - y), (1 - x, 1 - y)]

        def copy(a, k, block, to, src=None):
            dst = outs[a].at[_slot(block)]
            return pltpu.make_async_remote_copy(
                src_ref=dst if src is None else src, dst_ref=dst, send_sem=send_sems.at[a * 7 + k],
                recv_sem=recv_sems.at[a * 7 + k], device_id=to, device_id_type=MESH)

        mine = [pltpu.make_async_copy(ins[a], outs[a].at[_slot(me)], local_sems.at[a]) for a in range(n)]
        for cp in mine:
            cp.start()
        first = []
        for a in range(n):
            first.append(copy(a, 0, me, sibling, src=ins[a]))
            first += [copy(a, 1 + j, me, (*chip, c), src=ins[a]) for j, chip in enumerate(chips)]
        for cp in first:
            cp.start()
        passed = []
        for a in range(n):
            for j, chip in enumerate(chips):
                copy(a, 1 + j, (*chip, c), me).wait_recv()
                cp = copy(a, 4 + j, (*chip, c), sibling)
                cp.start()
                passed.append(cp)
        for a in range(n):
            copy(a, 0, sibling, me).wait_recv()
            for j, chip in enumerate(chips):
                copy(a, 4 + j, (*chip, 1 - c), me).wait_recv()
        for cp in first + passed:
            cp.wait_send()
        for cp in mine:
            cp.wait()

    return _pcall(
        body, name=name, in_specs=[hbm] * n, out_specs=[hbm] * n,
        out_shape=[jax.ShapeDtypeStruct((N_DEV,) + a.shape, a.dtype) for a in arrs],
        scratch_shapes=[pltpu.SemaphoreType.DMA((7 * n,)), pltpu.SemaphoreType.DMA((7 * n,)),
                        pltpu.SemaphoreType.DMA((n,))],
        compiler_params=pltpu.CompilerParams(has_side_effects=True),
    )(*arrs)


def _t(w):
    return jnp.swapaxes(w, -1, -2)


def _rows_from_blocks(blocks, pad_to=None):
    full = blocks.reshape(-1, blocks.shape[2])
    if pad_to is not None and pad_to > full.shape[0]:
        full = jnp.pad(full, ((0, pad_to - full.shape[0]), (0, 0)))
    return full


def _rows_to_blocks(full, nrows):
    return full[:nrows].reshape(N_DEV, nrows // N_DEV, full.shape[1])


SMALL_ORDER = ("g_pre_ff1", "g_post_ff1", "g_pre_mix", "g_post_mix", "g_out_a", "g_out_b", "g_pre_ff2", "g_post_ff2",
               "b_forget")


def _pack_small(vals):
    rows = []
    for name in SMALL_ORDER:
        v = vals[name].reshape(1, -1)
        if v.shape[1] % LANES:
            v = jnp.pad(v, ((0, 0), (0, LANES - v.shape[1] % LANES)))
        rows.append(v)
    return jnp.concatenate(rows, axis=1)


def _unpack_small(row, sizes):
    out, pos = {}, 0
    for name in SMALL_ORDER:
        n = sizes[name]
        out[name] = row[:, pos:pos + n]
        pos += -(-n // LANES) * LANES
    return out


def _ffn_forward(x, mod, g_pre, g_post, wg, wu, wd, i0, nb, tag, target=None, side=None):
    h = prenorm_fwd(x, g_pre, mod, i0, i0 + 1, nb, f"{tag}_prenorm")
    res, side_out = ffn_up(h, wg, wu, f"{tag}_up", side=side), None
    if side is not None:
        res, side_out = res
    gate, up, act = res
    res = postnorm_fwd(x, [(act, wd)], g_post, mod, i0 + 2, 0.5, nb, f"{tag}_down_postnorm", target=target)
    out, y0 = (res[0] if target is None else tuple(res[:2])), res[-1]
    return out, (x, h, gate, up, act, y0), side_out


def _ffn_backward(dxo, saved, mod, g_pre, g_post, wg, wu, wd, i0, nb, tag, side=None, chain=False):
    x, h, gate, up, act, y0 = saved
    dy0, dg_post, dgate_mod = postnorm_bwd(dxo, y0, g_post, mod, i0 + 2, 0.5, nb, f"{tag}_postnorm_bwd")
    dwd = mm_tn(act, dy0, BF16, f"{tag}_dwd")
    res, side_out = ffn_down_bwd(dy0, wd, gate, up, f"{tag}_down_bwd", side=side), None
    if side is not None:
        res, side_out = res
    dgate, dup = res
    dh_pairs = [(dgate, wg), (dup, wu)]
    if chain:
        dwg, (dwd,) = mm_tn(dgate, h, BF16, f"{tag}_dwg", side=([_rows_to_blocks(dwd, D_FF)[:, None]], False))
        dwu, (dwg,) = mm_tn(dup, h, BF16, f"{tag}_dwu", side=([_rows_to_blocks(dwg, D_FF)[:, None]], False))
        (dx, dg_pre, dsc, dsh), (dwu,) = prenorm_bwd(dh_pairs, x, g_pre, mod, i0 + 1, dxo, nb, f"{tag}_dh_prenorm_bwd",
                                                     ts=DH_ROWS, side=([_rows_to_blocks(dwu, D_FF)[:, None]], False))
    else:
        dwg = mm_tn(dgate, h, BF16, f"{tag}_dwg")
        dwu = mm_tn(dup, h, BF16, f"{tag}_dwu")
        dx, dg_pre, dsc, dsh = prenorm_bwd(dh_pairs, x, g_pre, mod, i0 + 1, dxo, nb, f"{tag}_dh_prenorm_bwd", ts=DH_ROWS)
    return dx, dict(g_pre=dg_pre, g_post=dg_post, wg=dwg, wu=dwu, wd=dwd, mod=(dsh, dsc, dgate_mod)), side_out


def kernel(x, c, positions, w_ada, b_ada, g_pre_ff1, g_post_ff1, w_ff1_gate, w_ff1_up, w_ff1_down, g_pre_mix, g_post_mix, w_in, b_forget, g_out_a, g_out_b, w_out, g_pre_ff2, g_post_ff2, w_ff2_gate, w_ff2_up, w_ff2_down, loss_target, m_w_ada, m_b_ada, m_g_pre_ff1, m_g_post_ff1, m_w_ff1_gate, m_w_ff1_up, m_w_ff1_down, m_g_pre_mix, m_g_post_mix, m_w_in, m_b_forget, m_g_out_a, m_g_out_b, m_w_out, m_g_pre_ff2, m_g_post_ff2, m_w_ff2_gate, m_w_ff2_up, m_w_ff2_down, v_w_ada, v_b_ada, v_g_pre_ff1, v_g_post_ff1, v_w_ff1_gate, v_w_ff1_up, v_w_ff1_down, v_g_pre_mix, v_g_post_mix, v_w_in, v_b_forget, v_g_out_a, v_g_out_b, v_w_out, v_g_pre_ff2, v_g_post_ff2, v_w_ff2_gate, v_w_ff2_up, v_w_ff2_down):
    weights = dict(w_ada=w_ada, b_ada=b_ada, g_pre_ff1=g_pre_ff1, g_post_ff1=g_post_ff1, w_ff1_gate=w_ff1_gate,
                   w_ff1_up=w_ff1_up, w_ff1_down=w_ff1_down, g_pre_mix=g_pre_mix, g_post_mix=g_post_mix, w_in=w_in,
                   b_forget=b_forget, g_out_a=g_out_a, g_out_b=g_out_b, w_out=w_out, g_pre_ff2=g_pre_ff2,
                   g_post_ff2=g_post_ff2, w_ff2_gate=w_ff2_gate, w_ff2_up=w_ff2_up, w_ff2_down=w_ff2_down)
    mom_m = dict(w_ada=m_w_ada, b_ada=m_b_ada, g_pre_ff1=m_g_pre_ff1, g_post_ff1=m_g_post_ff1, w_ff1_gate=m_w_ff1_gate,
                 w_ff1_up=m_w_ff1_up, w_ff1_down=m_w_ff1_down, g_pre_mix=m_g_pre_mix, g_post_mix=m_g_post_mix,
                 w_in=m_w_in, b_forget=m_b_forget, g_out_a=m_g_out_a, g_out_b=m_g_out_b, w_out=m_w_out,
                 g_pre_ff2=m_g_pre_ff2, g_post_ff2=m_g_post_ff2, w_ff2_gate=m_w_ff2_gate, w_ff2_up=m_w_ff2_up,
                 w_ff2_down=m_w_ff2_down)
    mom_v = dict(w_ada=v_w_ada, b_ada=v_b_ada, g_pre_ff1=v_g_pre_ff1, g_post_ff1=v_g_post_ff1, w_ff1_gate=v_w_ff1_gate,
                 w_ff1_up=v_w_ff1_up, w_ff1_down=v_w_ff1_down, g_pre_mix=v_g_pre_mix, g_post_mix=v_g_post_mix,
                 w_in=v_w_in, b_forget=v_b_forget, g_out_a=v_g_out_a, g_out_b=v_g_out_b, w_out=v_w_out,
                 g_pre_ff2=v_g_pre_ff2, g_post_ff2=v_g_post_ff2, w_ff2_gate=v_w_ff2_gate, w_ff2_up=v_w_ff2_up,
                 w_ff2_down=v_w_ff2_down)
    order = list(weights)

    nb, s, d = x.shape
    t = nb * s
    me = _slot(_place())
    nbg = nb * N_DEV
    ada_cols = w_ada.shape[2]

    bf = lambda w: w[0].astype(BF16)
    bft = lambda w: _t(w)[0].astype(BF16)
    c_all, ff1_all = all_gather([c, jnp.stack([bft(w_ff1_gate), bft(w_ff1_up), bf(w_ff1_down)])], "gather_ff1")
    c_all = c_all.reshape(nbg, d)
    wg1, wu1, wd1 = (_rows_from_blocks(ff1_all[:, i], D_FF_PAD) for i in range(3))

    b_cols = lax.dynamic_slice(b_ada, (0, me * ada_cols), (1, ada_cols))
    mod_cols = ada_fwd(c_all, w_ada[0], b_cols, "ada_fwd")
    (mod_all,) = all_gather([mod_cols], "gather_mod")
    mod = lax.dynamic_slice(mod_all, (0, me * nb, 0), (N_DEV, nb, ada_cols))
    mod = mod.transpose(1, 0, 2).reshape(nb, N_MOD, d)

    xf = x.reshape(t, d)
    target = loss_target.reshape(t, d)

    x1, saved1, (w_in_all, w_out_all) = _ffn_forward(xf, mod, g_pre_ff1, g_post_ff1, wg1, wu1, wd1, 0, nb, "ff1",
                                                     side=([bft(w_in), bf(w_out)], True))
    w_in_t = _rows_from_blocks(w_in_all)
    n_qkv = 3 * (WIDTH_A + WIDTH_B)
    w_qkv_t = w_in_t[:n_qkv]
    w_f_t = jnp.pad(w_in_t[n_qkv:], ((0, LANES - N_HEADS_B), (0, 0)))
    w_o = _rows_from_blocks(w_out_all)
    w_o_a, w_o_b = w_o[:WIDTH_A], w_o[WIDTH_A:]

    h2 = prenorm_fwd(x1, g_pre_mix, mod, 3, 4, nb, "mix_prenorm")
    tables = rope_tables(positions)
    proj = mm_rows([(h2, w_qkv_t)], True, BF16, "mix_proj", rope=(tables, 2 * WIDTH_A))
    f_logit = mm_rows([(h2, w_f_t)], True, F32, "mix_forget")
    tab_a = dilated_table(s, ATTN_TQ, ATTN_TK)
    tab_b = causal_table(s, ATTN_TQ, ATTN_TK)
    ft = f_logit[:, :N_HEADS_B].reshape(nb, s, N_HEADS_B).transpose(0, 2, 1)
    bf_col = b_forget.reshape(N_HEADS_B, 1)
    colbias = fox_gate_fwd(ft, bf_col, "fox_gate").reshape(nb, N_HEADS_B, s // ATTN_TK, 1, ATTN_TK)
    pa = WIDTH_A // LANES
    (o_a, lse_a), (ff2_all,) = attn_fwd(
        proj, 0, proj, pa, proj, 2 * pa, tab_a, None, nb, "attn_a",
        side=([jnp.stack([bft(w_ff2_gate), bft(w_ff2_up), bf(w_ff2_down)])], True))
    wg2, wu2, wd2 = (_rows_from_blocks(ff2_all[:, i], D_FF_PAD) for i in range(3))
    o_b, lse_b = attn_fwd(proj, 3 * pa, proj, 4 * pa, proj, 5 * pa, tab_b, colbias, nb, "attn_b")
    m_a = prenorm_fwd(o_a, g_out_a, None, None, None, nb, "out_norm_a")
    m_b = prenorm_fwd(o_b, g_out_b, None, None, None, nb, "out_norm_b")
    x2, y0m = postnorm_fwd(x1, [(m_a, w_o_a), (m_b, w_o_b)], g_post_mix, mod, 5, 1.0, nb, "mix_out_postnorm")

    (dx3, loss_part), saved2, _ = _ffn_forward(x2, mod, g_pre_ff2, g_post_ff2, wg2, wu2, wd2, 6, nb, "ff2", target=target)
    loss = lax.psum(loss_part[0, 0], ("x", "y", "c"))

    dx2, gr2, _ = _ffn_backward(dx3, saved2, mod, g_pre_ff2, g_post_ff2, wg2, wu2, wd2, 6, nb, "ff2")
    ff2_blocks = [jnp.stack([_rows_to_blocks(gr2[k], D_FF) for k in ("wg", "wu", "wd")], axis=1)]

    dy0m, dg_post_mix, dgate_mix = postnorm_bwd(dx2, y0m, g_post_mix, mod, 5, 1.0, nb, "mix_postnorm_bwd")
    dw_o_a = mm_tn(m_a, dy0m, BF16, "mix_dwo_a")
    dw_o_b = mm_tn(m_b, dy0m, BF16, "mix_dwo_b")
    do_a, dg_out_a = prenorm_bwd([(dy0m, w_o_a.T)], o_a, g_out_a, None, None, None, nb, "out_norm_a_bwd")
    do_b, dg_out_b = prenorm_bwd([(dy0m, w_o_b.T)], o_b, g_out_b, None, None, None, nb, "out_norm_b_bwd")
    (dq_a, dk_a, dv_a), (g_ff2,) = attn_bwd(proj, 0, proj, pa, proj, 2 * pa, o_a, lse_a, do_a, tab_a, None, nb,
                                            "attn_a_bwd", side=(ff2_blocks, False), rope_tabs=tables)
    dq_b, dk_b, dv_b, dcb, drow = attn_bwd(proj, 3 * pa, proj, 4 * pa, proj, 5 * pa, o_b, lse_b, do_b, tab_b, colbias, nb,
                                           "attn_b_bwd")
    dz_t, db_forget = fox_gate_bwd(dcb.reshape(nb, N_HEADS_B, s), drow.reshape(nb, N_HEADS_B, s), ft, bf_col,
                                   "fox_gate_bwd")
    dz = jnp.pad(dz_t.transpose(0, 2, 1).reshape(t, N_HEADS_B), ((0, 0), (0, LANES - N_HEADS_B))).astype(BF16)
    pieces = [dq_a, dk_a, dv_a, dq_b, dk_b, dv_b]
    w_pieces = [w_qkv_t[i * WIDTH_A:(i + 1) * WIDTH_A] for i in range(6)]
    dh2_pairs = list(zip(pieces, w_pieces)) + [(dz, w_f_t)]
    dw_in_t = jnp.concatenate([mm_tn(p, h2, BF16, f"mix_dwin_{i}") for i, p in enumerate(pieces)]
                              + [mm_tn(dz, h2, BF16, "mix_dwin_f")[:N_HEADS_B]], axis=0)
    dx1, dg_pre_mix, dsc_mix, dsh_mix = prenorm_bwd(dh2_pairs, x1, g_pre_mix, mod, 4, dx2, nb, "mix_dh_prenorm_bwd",
                                                    ts=DH_ROWS)

    g_in = _rows_to_blocks(dw_in_t, dw_in_t.shape[0])[:, None]
    g_out = _rows_to_blocks(jnp.concatenate([dw_o_a, dw_o_b], axis=0), d)[:, None]
    dx0, gr1, (g_in, g_out) = _ffn_backward(dx1, saved1, mod, g_pre_ff1, g_post_ff1, wg1, wu1, wd1, 0, nb, "ff1",
                                            side=([g_in, g_out], False), chain=True)
    grad_x = dx0.reshape(nb, s, d)

    dmod =jnp.concatenate(list(gr1["mod"]) + [dsh_mix, dsc_mix, dgate_mix] + list(gr2["mod"]), axis=1)
    small = _pack_small(dict(g_pre_ff1=gr1["g_pre"], g_post_ff1=gr1["g_post"], g_pre_mix=dg_pre_mix,
                             g_post_mix=dg_post_mix, g_out_a=dg_out_a, g_out_b=dg_out_b, g_pre_ff2=gr2["g_pre"],
                             g_post_ff2=gr2["g_post"], b_forget=db_forget))
    dmod_all, small_all = all_gather([dmod.reshape(nb, N_MOD * d), small], "gather_small_grads")
    dmod_all = dmod_all.reshape(nbg, N_MOD * d)

    res = {}
    def adamw_t(parts, group, n):
        return tuple(_t(r) for r in adamw(parts, group, _t(weights[n])[0], _t(mom_m[n])[0], _t(mom_v[n])[0], f"adamw_{n}"))

    res["w_ff1_gate"] = adamw_t(gr1["wg"], 0, "w_ff1_gate")
    res["w_ff1_up"] = adamw_t(gr1["wu"], 0, "w_ff1_up")
    res["w_ff2_gate"] = adamw_t(g_ff2, 0, "w_ff2_gate")
    res["w_ff2_up"] = adamw_t(g_ff2, 1, "w_ff2_up")
    res["w_ff1_down"] = adamw(gr1["wd"], 0, w_ff1_down[0], m_w_ff1_down[0], v_w_ff1_down[0], "adamw_ff1_down")
    res["w_ff2_down"] = adamw(g_ff2, 2, w_ff2_down[0], m_w_ff2_down[0], v_w_ff2_down[0], "adamw_ff2_down")
    res["w_in"] = adamw_t(g_in, 0, "w_in")
    res["w_out"] = adamw(g_out, 0, w_out[0], m_w_out[0], v_w_out[0], "adamw_out")
    dmod_cols = lax.dynamic_slice(dmod_all, (0, me * ada_cols), (nbg, ada_cols))
    dw_ada = ada_bwd(c_all, dmod_cols, "ada_bwd")
    res["w_ada"] = adamw(dw_ada[None, None], 0, w_ada[0], m_w_ada[0], v_w_ada[0], "adamw_ada", tr=256)
    res["b_ada"] = adamw(dmod_all[:, None, None], 0, b_ada, m_b_ada, v_b_ada, "adamw_b_ada")
    sizes = {n: weights[n].shape[1] for n in SMALL_ORDER}
    small_res = adamw(small_all[:, None], 0, _pack_small(weights), _pack_small(mom_m), _pack_small(mom_v), "adamw_small")
    small_res = [_unpack_small(r, sizes) for r in small_res]
    for n in SMALL_ORDER:
        res[n] = tuple(r[n] for r in small_res)

    outs = [loss, grad_x]
    for kind in range(4):
        for n in order:
            a = res[n][kind]
            outs.append(a.reshape(weights[n].shape))
    return tuple(outs)
```

```python
import functools

import jax
import jax.numpy as jnp
from jax import lax
from jax.experimental import pallas as pl
from jax.experimental.pallas import tpu as pltpu

F32 = jnp.float32
BF16 = jnp.bfloat16

D_MODEL = 1024
HEAD_DIM = 64
N_HEADS_A = 8
N_HEADS_B = 8
WIDTH_A = N_HEADS_A * HEAD_DIM
WIDTH_B = N_HEADS_B * HEAD_DIM
DILATED_PATTERNS = ((128, 1), (512, 4), (2048, 16))
ROT_DIM = HEAD_DIM // 4
ROPE_THETA = 500000.0
D_FF = 2752
D_FF_PAD = 2816
N_MOD = 9
EPS = 1e-6
ATTN_SCALE = HEAD_DIM ** -0.5
NEG = -1e30
N_DEV = 8
LANES = 128
HEADS_PER_STEP = LANES // HEAD_DIM

ADAM_LR = 0.001
ADAM_B1 = 0.9
ADAM_B2 = 0.999
ADAM_EPS = 1e-08
ADAM_WD = 0.01
ADAM_STEP = 10

VMEM_LIMIT = 56 * 1024 * 1024
MESH = pl.DeviceIdType.MESH

NT_DIMS = (((1,), (1,)), ((), ()))
TN_DIMS = (((0,), (0,)), ((), ()))
NN_DIMS = (((1,), (0,)), ((), ()))


def _place():
    return lax.axis_index("x"), lax.axis_index("y"), lax.axis_index("c")


def _slot(p):
    return 4 * p[0] + 2 * p[1] + p[2]


def _direct_copies(ins, outs, send_sems, recv_sems, local_sems, gather):
    x, y, c = _place()
    me = (x, y, c)
    flip = lambda v, bit: 1 - v if bit else v
    peers = [(flip(x, k & 4), flip(y, k & 2), flip(c, k & 1)) for k in range(1, N_DEV)]
    local, sends, recvs = [], [], []
    for a in range(len(ins)):
        mine = ins[a] if gather else ins[a].at[_slot(me)]
        local.append(pltpu.make_async_copy(mine, outs[a].at[_slot(me)], local_sems.at[a]))
        for k, peer in enumerate(peers):
            sems = dict(send_sem=send_sems.at[a * 7 + k], recv_sem=recv_sems.at[a * 7 + k], device_id=peer,
                        device_id_type=MESH)
            sends.append(pltpu.make_async_remote_copy(
                src_ref=ins[a] if gather else ins[a].at[_slot(peer)], dst_ref=outs[a].at[_slot(me)], **sems))
            recvs.append(pltpu.make_async_remote_copy(src_ref=mine, dst_ref=outs[a].at[_slot(peer)], **sems))
    return local, sends, recvs


def _comm_scratch(n):
    return [pltpu.SemaphoreType.DMA((7 * n,)), pltpu.SemaphoreType.DMA((7 * n,)), pltpu.SemaphoreType.DMA((n,))]


def _pcall(body, side=None, **kw):
    if side is None:
        return pl.pallas_call(body, **kw)
    arrs, gather = side
    n = len(arrs)
    grid = kw["grid"]
    in_specs = list(kw["in_specs"])
    single = not isinstance(kw["out_specs"], (list, tuple))
    out_specs = [kw["out_specs"]] if single else list(kw["out_specs"])
    out_shape = [kw["out_shape"]] if single else list(kw["out_shape"])
    scratch = list(kw.get("scratch_shapes", []))
    n_in, n_out, n_scr = len(in_specs), len(out_specs), len(scratch)
    hbm = pl.BlockSpec(memory_space=pl.ANY)

    def hosted(*refs):
        pos = [0]

        def take(k):
            pos[0] += k
            return refs[pos[0] - k:pos[0]]

        ins, s_ins, outs, s_outs, scr, sems = take(n_in), take(n), take(n_out), take(n), take(n_scr), take(3)
        ids = [pl.program_id(i) for i in range(len(grid))]
        first = functools.reduce(jnp.logical_and, [i == 0 for i in ids])
        last = functools.reduce(jnp.logical_and, [i == g - 1 for i, g in zip(ids, grid)])

        @pl.when(first)
        def _():
            local, sends, _ = _direct_copies(s_ins, s_outs, *sems, gather)
            for cp in local + sends:
                cp.start()

        body(*ins, *outs, *scr)

        @pl.when(last)
        def _():
            local, sends, recvs = _direct_copies(s_ins, s_outs, *sems, gather)
            for cp in recvs:
                cp.wait_recv()
            for cp in sends:
                cp.wait_send()
            for cp in local:
                cp.wait()

    kw.update(in_specs=in_specs + [hbm] * n, out_specs=out_specs + [hbm] * n,
              out_shape=out_shape + [jax.ShapeDtypeStruct(((N_DEV,) + a.shape) if gather else a.shape, a.dtype)
                                     for a in arrs],
              scratch_shapes=scratch + _comm_scratch(n))
    call = pl.pallas_call(hosted, **kw)

    def run(*args):
        res = call(*args, *arrs)
        main = res[0] if single else list(res[:n_out])
        return main, list(res[n_out:])

    return run


def _params(sem=None, **kw):
    if sem is not None:
        kw["dimension_semantics"] = sem
    return pltpu.CompilerParams(vmem_limit_bytes=VMEM_LIMIT, **kw)


def _rotate(xv, c, sp, sm, transpose):
    width = xv.shape[1]
    half = ROT_DIM // 2
    if transpose:
        return xv * c + pltpu.roll(xv * sp, width - half, 1) + pltpu.roll(xv * sm, half, 1)
    return xv * c + pltpu.roll(xv, half, 1) * sp + pltpu.roll(xv, width - half, 1) * sm


def mm_rows(pairs, trans_b, out_dtype, name, tm=512, side=None, rope=None):
    n = len(pairs)
    m = pairs[0][0].shape[0]
    n_out = pairs[0][1].shape[0 if trans_b else 1]
    dims = NT_DIMS if trans_b else NN_DIMS

    def body(*refs):
        o_ref = refs[-1]
        acc = None
        for a_ref, b_ref in zip(refs[:n], refs[n:2 * n]):
            d = lax.dot_general(a_ref[...], b_ref[...], dims, preferred_element_type=F32)
            acc = d if acc is None else acc + d
        if rope is None:
            o_ref[...] = acc.astype(o_ref.dtype)
        else:
            width = rope[1]
            c, sp, sm = (jnp.concatenate([r[...]] * (width // LANES), axis=1) for r in refs[2 * n:2 * n + 3])
            o_ref[:, :width] = _rotate(acc[:, :width], c, sp, sm, False).astype(o_ref.dtype)
            o_ref[:, width:] = acc[:, width:].astype(o_ref.dtype)

    in_specs = [pl.BlockSpec((tm, a.shape[1]), lambda i: (i, 0)) for a, _ in pairs]
    in_specs += [pl.BlockSpec(b.shape, lambda i: (0, 0)) for _, b in pairs]
    args = [a for a, _ in pairs] + [b for _, b in pairs]
    if rope is not None:
        in_specs += [pl.BlockSpec((tm, LANES), lambda i: (i, 0))] * 3
        args += list(rope[0])
    return _pcall(
        body, side=side, name=name, grid=(m // tm,), in_specs=in_specs,
        out_specs=pl.BlockSpec((tm, n_out), lambda i: (i, 0)),
        out_shape=jax.ShapeDtypeStruct((m, n_out), out_dtype),
        compiler_params=_params(("arbitrary",)),
    )(*args)


DH_ROWS = 256
TN_TOKENS = 2048
TN_OUT_ELEMS = 2 * 1024 * 1024


def mm_tn(a, b, out_dtype, name, side=None):
    t, ka = a.shape
    n_out = b.shape[1]
    tk = min(TN_TOKENS, t)
    tka = ka // 2 if ka * n_out > TN_OUT_ELEMS else ka
    tn = n_out
    steps = t // tk

    def body(a_ref, b_ref, o_ref, acc_ref):
        k = pl.program_id(2)
        d = lax.dot_general(a_ref[...], b_ref[...], TN_DIMS, preferred_element_type=F32)

        @pl.when(k == 0)
        def _():
            acc_ref[...] = d

        @pl.when(k > 0)
        def _():
            acc_ref[...] += d

        @pl.when(k == steps - 1)
        def _():
            o_ref[...] = acc_ref[...].astype(o_ref.dtype)

    return _pcall(
        body, side=side, name=name, grid=(ka // tka, n_out // tn, steps),
        in_specs=[pl.BlockSpec((tk, tka), lambda i, j, k: (k, i)), pl.BlockSpec((tk, tn), lambda i, j, k: (k, j))],
        out_specs=pl.BlockSpec((tka, tn), lambda i, j, k: (i, j)),
        out_shape=jax.ShapeDtypeStruct((ka, n_out), out_dtype),
        scratch_shapes=[pltpu.VMEM((tka, tn), F32)],
        compiler_params=_params(("arbitrary", "arbitrary", "arbitrary")),
    )(a, b)


def _col_chunks(width, chunk=512):
    return [slice(c, min(c + chunk, width)) for c in range(0, width, chunk)]


def _sigmoid(x):
    return 1.0 / (1.0 + jnp.exp(-x))


def ffn_up(h, wgt, wut, name, tm=256, tn=D_FF_PAD, side=None):
    t, d = h.shape
    fp = wgt.shape[0]

    def body(h_ref, wg_ref, wu_ref, g_ref, u_ref, a_ref):
        hv = h_ref[...]

        def finish(cols, g, u):
            g_ref[:, cols] = g.astype(BF16)
            u_ref[:, cols] = u.astype(BF16)
            a_ref[:, cols] = (g * _sigmoid(g) * u).astype(BF16)

        pending = None
        for cols in _col_chunks(tn):
            g = lax.dot_general(hv, wg_ref[cols, :], NT_DIMS, preferred_element_type=F32)
            u = lax.dot_general(hv, wu_ref[cols, :], NT_DIMS, preferred_element_type=F32)
            if pending is not None:
                finish(*pending)
            pending = (cols, g, u)
        finish(*pending)

    w_spec = pl.BlockSpec((tn, d), lambda j, i: (j, 0))
    o_spec = pl.BlockSpec((tm, tn), lambda j, i: (i, j))
    o_shape = jax.ShapeDtypeStruct((t, fp), BF16)
    return _pcall(
        body, side=side, name=name, grid=(fp // tn, t // tm),
        in_specs=[pl.BlockSpec((tm, d), lambda j, i: (i, 0)), w_spec, w_spec],
        out_specs=[o_spec, o_spec, o_spec], out_shape=[o_shape, o_shape, o_shape],
        compiler_params=_params(("arbitrary", "arbitrary")),
    )(h, wgt, wut)


def ffn_down_bwd(dy0, wd, gate, up, name, tm=256, tn=D_FF_PAD, side=None):
    t, d = dy0.shape
    fp = wd.shape[0]

    def body(dy_ref, wd_ref, g_ref, u_ref, dg_ref, du_ref):
        dyv = dy_ref[...]

        def finish(cols, dact):
            g = g_ref[:, cols].astype(F32)
            u = u_ref[:, cols].astype(F32)
            sg = _sigmoid(g)
            silu = g * sg
            du_ref[:, cols] = (dact * silu).astype(BF16)
            dg_ref[:, cols] = ((dact * u) * (sg + silu * (1.0 - sg))).astype(BF16)

        pending = None
        for cols in _col_chunks(tn):
            dact = lax.dot_general(dyv, wd_ref[cols, :], NT_DIMS, preferred_element_type=F32)
            if pending is not None:
                finish(*pending)
            pending = (cols, dact)
        finish(*pending)

    t_spec = pl.BlockSpec((tm, tn), lambda j, i: (i, j))
    o_shape = jax.ShapeDtypeStruct((t, fp), BF16)
    return _pcall(
        body, side=side, name=name, grid=(fp // tn, t // tm),
        in_specs=[pl.BlockSpec((tm, d), lambda j, i: (i, 0)), pl.BlockSpec((tn, d), lambda j, i: (j, 0)), t_spec, t_spec],
        out_specs=[t_spec, t_spec], out_shape=[o_shape, o_shape],
        compiler_params=_params(("arbitrary", "arbitrary")),
    )(dy0, wd, gate, up)


def _row_specs(dx, ts, ns):
    return pl.BlockSpec((ts, dx), lambda b, s: (b * ns + s, 0))


def _mod_spec():
    return pl.BlockSpec((1, N_MOD, D_MODEL), lambda b, s: (b, 0, 0))


def _vec_spec(dx):
    return pl.BlockSpec((1, dx), lambda b, s: (0, 0))


def prenorm_fwd(x, g, mod, i_shift, i_scale, nb, name, ts=1024):
    t, dx = x.shape
    ts = min(ts, t // nb)
    ns = t // nb // ts

    def body(*refs):
        if mod is None:
            x_ref, g_ref, h_ref = refs
        else:
            x_ref, g_ref, mod_ref, h_ref = refs
        xv = x_ref[...]
        r = lax.rsqrt(jnp.mean(xv * xv, axis=-1, keepdims=True) + EPS)
        h = xv * r * g_ref[...]
        if mod is not None:
            h = h * (1.0 + mod_ref[0, i_scale:i_scale + 1, :]) + mod_ref[0, i_shift:i_shift + 1, :]
        h_ref[...] = h.astype(BF16)

    in_specs = [_row_specs(dx, ts, ns), _vec_spec(dx)]
    args = [x, g]
    if mod is not None:
        in_specs.append(_mod_spec())
        args.append(mod)
    return _pcall(
        body, name=name, grid=(nb, ns), in_specs=in_specs, out_specs=_row_specs(dx, ts, ns),
        out_shape=jax.ShapeDtypeStruct((t, dx), BF16), compiler_params=_params(("arbitrary", "arbitrary")),
    )(*args)


def prenorm_bwd(dh, x, g, mod, i_scale, dres, nb, name, ts=512, side=None):
    t, dx = x.shape
    ts = min(ts, t // nb)
    ns = t // nb // ts
    has_mod = mod is not None
    has_res = dres is not None
    pairs = dh if isinstance(dh, list) else None
    n_mm = 0 if pairs is None else len(pairs)

    def body(*refs):
        refs = list(refs)
        if pairs is None:
            dhv = refs[0][...].astype(F32)
            refs = refs[1:]
        else:
            dhv = None
            for a_ref, b_ref in zip(refs[:n_mm], refs[n_mm:2 * n_mm]):
                d = jnp.dot(a_ref[...], b_ref[...], preferred_element_type=F32)
                dhv = d if dhv is None else dhv + d
            refs = refs[2 * n_mm:]
        x_ref, g_ref = refs[:2]
        pos = 2
        mod_ref = dres_ref = None
        if has_mod:
            mod_ref = refs[pos]
            pos += 1
        if has_res:
            dres_ref = refs[pos]
            pos += 1
        dx_ref, dg_ref = refs[pos], refs[pos + 1]
        b, s = pl.program_id(0), pl.program_id(1)
        xv = x_ref[...]
        gv = g_ref[...]
        r = lax.rsqrt(jnp.mean(xv * xv, axis=-1, keepdims=True) + EPS)
        xhat = xv * r
        dn = dhv
        if has_mod:
            dsc_ref, dsh_ref = refs[pos + 2], refs[pos + 3]
            dn = dhv * (1.0 + mod_ref[0, i_scale:i_scale + 1, :])
            dsc = jnp.sum(dhv * xhat * gv, axis=0, keepdims=True)[None]
            dsh = jnp.sum(dhv, axis=0, keepdims=True)[None]

            @pl.when(s == 0)
            def _():
                dsc_ref[...] = dsc
                dsh_ref[...] = dsh

            @pl.when(s > 0)
            def _():
                dsc_ref[...] += dsc
                dsh_ref[...] += dsh

        dg = jnp.sum(dn * xhat, axis=0, keepdims=True)
        first = jnp.logical_and(b == 0, s == 0)

        @pl.when(first)
        def _():
            dg_ref[...] = dg

        @pl.when(jnp.logical_not(first))
        def _():
            dg_ref[...] += dg

        dxhat = dn * gv
        dxv = r * (dxhat - xhat * jnp.mean(dxhat * xhat, axis=-1, keepdims=True))
        if has_res:
            dxv = dxv + dres_ref[...]
        dx_ref[...] = dxv

    row = _row_specs(dx, ts, ns)
    if pairs is None:
        in_specs, args = [row], [dh]
    else:
        in_specs = [_row_specs(a.shape[1], ts, ns) for a, _ in pairs]
        in_specs += [pl.BlockSpec(b.shape, lambda b_, s_: (0, 0)) for _, b in pairs]
        args = [a for a, _ in pairs] + [b for _, b in pairs]
    in_specs += [row, _vec_spec(dx)]
    args += [x, g]
    if has_mod:
        in_specs.append(_mod_spec())
        args.append(mod)
    if has_res:
        in_specs.append(row)
        args.append(dres)
    out_specs = [row, _vec_spec(dx)]
    out_shape = [jax.ShapeDtypeStruct((t, dx), F32), jax.ShapeDtypeStruct((1, dx), F32)]
    if has_mod:
        bspec = pl.BlockSpec((1, 1, dx), lambda b, s: (b, 0, 0))
        out_specs += [bspec, bspec]
        out_shape += [jax.ShapeDtypeStruct((nb, 1, dx), F32)] * 2
    return _pcall(
        body, side=side, name=name, grid=(nb, ns), in_specs=in_specs, out_specs=out_specs, out_shape=out_shape,
        compiler_params=_params(("arbitrary", "arbitrary")),
    )(*args)


def postnorm_fwd(x, pairs, g, mod, i_gate, coef, nb, name, target=None, ts=512):
    t, dx = x.shape
    with_loss = target is not None
    ts = min(ts, t // nb)
    ns = t // nb // ts
    n_mm = len(pairs)

    def body(*refs):
        yv = None
        for a_ref, b_ref in zip(refs[:n_mm], refs[n_mm:2 * n_mm]):
            d = jnp.dot(a_ref[...], b_ref[...], preferred_element_type=F32)
            yv = d if yv is None else yv + d
        refs = refs[2 * n_mm:]
        x_ref, g_ref, mod_ref = refs[:3]
        refs[-1][...] = yv
        r = lax.rsqrt(jnp.mean(yv * yv, axis=-1, keepdims=True) + EPS)
        out = x_ref[...] + (coef * mod_ref[0, i_gate:i_gate + 1, :]) * (yv * r * g_ref[...])
        if not with_loss:
            refs[3][...] = out
            return
        t_ref, dx_ref, loss_ref = refs[3:6]
        b, s = pl.program_id(0), pl.program_id(1)
        err = out - t_ref[...]
        dx_ref[...] = err * (1.0 / dx)
        part = (0.5 / dx) * jnp.sum(jnp.sum(err * err, axis=1, keepdims=True), axis=0, keepdims=True)
        first = jnp.logical_and(b == 0, s == 0)

        @pl.when(first)
        def _():
            loss_ref[...] = part

        @pl.when(jnp.logical_not(first))
        def _():
            loss_ref[...] += part

    row = _row_specs(dx, ts, ns)
    in_specs = [_row_specs(a.shape[1], ts, ns) for a, _ in pairs]
    in_specs += [pl.BlockSpec(b.shape, lambda b_, s_: (0, 0)) for _, b in pairs]
    in_specs += [row, _vec_spec(dx), _mod_spec()]
    args = [a for a, _ in pairs] + [b for _, b in pairs] + [x, g, mod]
    row_shape = jax.ShapeDtypeStruct((t, dx), F32)
    out_specs, out_shape = [row, row], [row_shape, row_shape]
    if with_loss:
        in_specs.append(row)
        args.append(target)
        out_specs = [row, pl.BlockSpec((1, 1), lambda b, s: (0, 0)), row]
        out_shape = [row_shape, jax.ShapeDtypeStruct((1, 1), F32), row_shape]
    return _pcall(
        body, name=name, grid=(nb, ns), in_specs=in_specs, out_specs=out_specs, out_shape=out_shape,
        compiler_params=_params(("arbitrary", "arbitrary")),
    )(*args)


def postnorm_bwd(dxo, y0, g, mod, i_gate, coef, nb, name, ts=1024):
    t, dx = y0.shape
    ts = min(ts, t // nb)
    ns = t // nb // ts

    def body(d_ref, y_ref, g_ref, mod_ref, dy_ref, dg_ref, dgate_ref):
        b, s = pl.program_id(0), pl.program_id(1)
        yv = y_ref[...]
        dv = d_ref[...]
        gv = g_ref[...]
        r = lax.rsqrt(jnp.mean(yv * yv, axis=-1, keepdims=True) + EPS)
        yhat = yv * r
        dgate = jnp.sum(dv * (coef * (yhat * gv)), axis=0, keepdims=True)[None]
        dyn = dv * (coef * mod_ref[0, i_gate:i_gate + 1, :])
        dg = jnp.sum(dyn * yhat, axis=0, keepdims=True)
        dyhat = dyn * gv
        dy_ref[...] = (r * (dyhat - yhat * jnp.mean(dyhat * yhat, axis=-1, keepdims=True))).astype(BF16)

        @pl.when(s == 0)
        def _():
            dgate_ref[...] = dgate

        @pl.when(s > 0)
        def _():
            dgate_ref[...] += dgate

        first = jnp.logical_and(b == 0, s == 0)

        @pl.when(first)
        def _():
            dg_ref[...] = dg

        @pl.when(jnp.logical_not(first))
        def _():
            dg_ref[...] += dg

    row = _row_specs(dx, ts, ns)
    return _pcall(
        body, name=name, grid=(nb, ns), in_specs=[row, row, _vec_spec(dx), _mod_spec()],
        out_specs=[row, _vec_spec(dx), pl.BlockSpec((1, 1, dx), lambda b, s: (b, 0, 0))],
        out_shape=[jax.ShapeDtypeStruct((t, dx), BF16), jax.ShapeDtypeStruct((1, dx), F32),
                   jax.ShapeDtypeStruct((nb, 1, dx), F32)],
        compiler_params=_params(("arbitrary", "arbitrary")),
    )(dxo, y0, g, mod)


def rope_tables(positions):
    inv_freq = ROPE_THETA ** (-jnp.arange(0, ROT_DIM, 2, dtype=F32) / ROT_DIM)
    ang = positions.astype(F32).reshape(-1, 1) * inv_freq
    cos, sin = jnp.cos(ang), jnp.sin(ang)
    half = ROT_DIM // 2
    z = lambda n: jnp.zeros((ang.shape[0], n), F32)
    c = jnp.concatenate([cos, cos, jnp.ones((ang.shape[0], HEAD_DIM - ROT_DIM), F32)], axis=1)
    sp = jnp.concatenate([z(half), sin, z(HEAD_DIM - ROT_DIM)], axis=1)
    sm = jnp.concatenate([-sin, z(HEAD_DIM - half)], axis=1)
    return tuple(jnp.tile(a, (1, HEADS_PER_STEP)) for a in (c, sp, sm))


def _scan_lanes(x, reverse):
    n = x.shape[-1]
    lane = lax.broadcasted_iota(jnp.int32, x.shape, x.ndim - 1)
    k = 1
    while k < n:
        if reverse:
            x = x + jnp.where(lane < n - k, pltpu.roll(x, n - k, x.ndim - 1), 0.0)
        else:
            x = x + jnp.where(lane >= k, pltpu.roll(x, k, x.ndim - 1), 0.0)
        k *= 2
    return x


def _log_sigmoid(z):
    return jnp.minimum(z, 0.0) - jnp.log(1.0 + jnp.exp(-jnp.abs(z)))


def fox_gate_fwd(ft, b_forget, name):
    nb, nh, s = ft.shape

    def body(f_ref, b_ref, o_ref):
        z = f_ref[0] + b_ref[...]
        o_ref[0] = -_scan_lanes(_log_sigmoid(z), False)

    spec = pl.BlockSpec((1, nh, s), lambda b: (b, 0, 0))
    return _pcall(
        body, name=name, grid=(nb,), in_specs=[spec, pl.BlockSpec((nh, 1), lambda b: (0, 0))], out_specs=spec,
        out_shape=jax.ShapeDtypeStruct((nb, nh, s), F32), compiler_params=_params(("arbitrary",)),
    )(ft, b_forget)


def fox_gate_bwd(dcb, drow, ft, b_forget, name):
    nb, nh, s = ft.shape

    def body(d_ref, r_ref, f_ref, b_ref, dz_ref, db_ref):
        b = pl.program_id(0)
        z = f_ref[0] + b_ref[...]
        dlf = _scan_lanes(r_ref[0] - d_ref[0], True)
        dz = dlf * _sigmoid(-z)
        dz_ref[0] = dz
        db = jnp.sum(dz, axis=1, keepdims=True)

        @pl.when(b == 0)
        def _():
            db_ref[...] = db

        @pl.when(b > 0)
        def _():
            db_ref[...] += db

    spec = pl.BlockSpec((1, nh, s), lambda b: (b, 0, 0))
    vec = pl.BlockSpec((nh, 1), lambda b: (0, 0))
    return _pcall(
        body, name=name, grid=(nb,), in_specs=[spec, spec, spec, vec], out_specs=[spec, vec],
        out_shape=[jax.ShapeDtypeStruct((nb, nh, s), F32), jax.ShapeDtypeStruct((nh, 1), F32)],
        compiler_params=_params(("arbitrary",)),
    )(dcb, drow, ft, b_forget)


ATTN_TQ = 512
ATTN_TK = 512
ONES_ROWS = 16


def _rows_to_cols(rows):
    tile = jnp.concatenate([jnp.broadcast_to(rw, (HEAD_DIM, rw.shape[1])) for rw in rows], axis=0)
    return tile.T


def _block_delta(s, tq, tk):
    off = jnp.arange(s // tk) - (tq // tk - 1)
    return off[:, None, None] * tk + jnp.arange(tq)[None, None, :] - jnp.arange(tk)[None, :, None]


def dilated_table(s, tq, tk):
    delta = _block_delta(s, tq, tk)
    count = jnp.zeros(delta.shape, F32)
    for window, dil in DILATED_PATTERNS:
        count = count + ((delta >= 0) & (delta <= window) & (delta % dil == 0)).astype(F32)
    return jnp.where(count > 0, jnp.log(jnp.maximum(count, 1.0)), NEG)


def causal_table(s, tq, tk):
    return jnp.where(_block_delta(s, tq, tk) >= 0, 0.0, NEG).astype(F32)


def attn_fwd(q_arr, q_off, k_arr, k_off, v_arr, v_off, table, colbias, nb, name, side=None, off_diag_bias=True):
    t = q_arr.shape[0]
    s = t // nb
    tk, tq = table.shape[1:]
    assert tq == tk, "the diagonal handling below is written for square tiles"
    nq, nk = s // tq, s // tk
    npairs = WIDTH_A // LANES
    use_cb = colbias is not None

    def body(*refs):
        refs = list(refs)
        q_ref, k_ref, v_ref, tab_ref = refs[:4]
        cb_ref = refs[4] if use_cb else None
        tail = refs[-(HEADS_PER_STEP + int(use_cb)):]
        acc_s = tail[:HEADS_PER_STEP]
        cbc_s = tail[-1] if use_cb else None
        o_ref, lse_ref, vt_s = refs[-3 - len(tail):-len(tail)]
        qi = pl.program_id(2)

        heads = [slice(h * HEAD_DIM, (h + 1) * HEAD_DIM) for h in range(HEADS_PER_STEP)]

        @pl.when(qi == 0)
        def _():
            for cblk in range(nk):
                vt = v_ref[cblk * tk:(cblk + 1) * tk, :].astype(F32).T.astype(BF16)
                for h, hs in enumerate(heads):
                    vt_s[cblk, h, 0:HEAD_DIM, :] = vt[hs, :]
                    vt_s[cblk, h, HEAD_DIM:, :] = jnp.ones((ONES_ROWS, tk), BF16)
                if use_cb:
                    cbc_s[cblk] = _rows_to_cols([cb_ref[0, h, cblk] for h in range(HEADS_PER_STEP)])

        qt_all = (q_ref[...].astype(F32) * ATTN_SCALE).T.astype(BF16)
        qts = [qt_all[hs, :] for hs in heads]
        for a in acc_s:
            a[...] = jnp.zeros_like(a)

        def tile(kb, tab, k0, klen, q0, carry):
            ks = pl.multiple_of(kb * tk + k0, klen)
            sts, out = [], []
            for h, hs in enumerate(heads):
                st = jnp.dot(k_ref[pl.ds(ks, klen), hs], qts[h][:, q0:], preferred_element_type=F32)
                if tab is not None:
                    st = st + tab
                if use_cb:
                    st = st + cbc_s[kb, k0:k0 + klen, h * HEAD_DIM:h * HEAD_DIM + 1]
                sts.append(st)
            m_old = [carry[h][:, q0:] for h in range(HEADS_PER_STEP)]
            m_new = [jnp.maximum(m_old[h], jnp.max(sts[h], axis=0, keepdims=True)) for h in range(HEADS_PER_STEP)]
            for h in range(HEADS_PER_STEP):
                pt = jnp.exp(sts[h] - m_new[h]).astype(BF16)
                acc_s[h][:, q0:] = (jnp.exp(m_old[h] - m_new[h]) * acc_s[h][:, q0:]
                                    + jnp.dot(vt_s[kb, h, :, k0:k0 + klen], pt, preferred_element_type=F32))
                out.append(m_new[h] if q0 == 0 else jnp.concatenate([carry[h][:, :q0], m_new[h]], axis=1))
            return tuple(out)

        fin = lax.fori_loop(0, qi, lambda kb, c: tile(kb, tab_ref[qi - kb] if off_diag_bias else None, 0, tk, 0, c),
                            tuple(jnp.full((1, tq), NEG, F32) for _ in heads))
        half = tk // 2
        fin = tile(qi, tab_ref[0, 0:half, :], 0, half, 0, fin)
        fin = tile(qi, tab_ref[0, half:, half:], half, half, half, fin)
        outs = []
        for h in range(HEADS_PER_STEP):
            l = acc_s[h][HEAD_DIM:HEAD_DIM + 1, :]
            outs.append(acc_s[h][0:HEAD_DIM, :] / l)
            lse_ref[0, h, 0] = fin[h] + jnp.log(l)
        o_ref[...] = jnp.concatenate(outs, axis=0).T

    def seq_spec(off):
        return pl.BlockSpec((s, LANES), lambda b, j, i: (b, off + j))

    in_specs = [pl.BlockSpec((tq, LANES), lambda b, j, i: (b * nq + i, q_off + j)), seq_spec(k_off), seq_spec(v_off),
                pl.BlockSpec(table.shape, lambda b, j, i: (0, 0, 0))]
    args = [q_arr, k_arr, v_arr, table]
    if use_cb:
        in_specs.append(pl.BlockSpec((1, HEADS_PER_STEP, nk, 1, tk), lambda b, j, i: (b, j, 0, 0, 0)))
        args.append(colbias)
    n_heads = npairs * HEADS_PER_STEP
    return _pcall(
        body, side=side, name=name, grid=(nb, npairs, nq), in_specs=in_specs,
        out_specs=[pl.BlockSpec((tq, LANES), lambda b, j, i: (b * nq + i, j)),
                   pl.BlockSpec((1, HEADS_PER_STEP, 1, 1, tq), lambda b, j, i: (b, j, i, 0, 0))],
        out_shape=[jax.ShapeDtypeStruct((t, npairs * LANES), F32), jax.ShapeDtypeStruct((nb, n_heads, nq, 1, tq), F32)],
        scratch_shapes=[pltpu.VMEM((nk, HEADS_PER_STEP, HEAD_DIM + ONES_ROWS, tk), BF16)]
        + [pltpu.VMEM((HEAD_DIM + ONES_ROWS, tq), F32)] * HEADS_PER_STEP
        + ([pltpu.VMEM((nk, tk, LANES), F32)] if use_cb else []),
        compiler_params=_params(("arbitrary", "arbitrary", "arbitrary")),
    )(*args)


def attn_bwd(q_arr, q_off, k_arr, k_off, v_arr, v_off, o_arr, lse_arr, do_arr, table, colbias, nb, name, side=None,
             rope_tabs=None, off_diag_bias=True):
    t = q_arr.shape[0]
    s = t // nb
    tk, tq = table.shape[1:]
    assert tq == tk, "the diagonal handling below is written for square tiles"
    nq, nk = s // tq, s // tk
    npairs = WIDTH_A // LANES
    use_cb = colbias is not None

    def body(*refs):
        refs = list(refs)
        q_ref, k_ref, v_ref, o_ref, lse_ref, do_ref, tab_ref = refs[:7]
        pos = 7
        cb_ref = None
        if use_cb:
            cb_ref = refs[pos]
            pos += 1
        rope_refs = None
        if rope_tabs is not None:
            rope_refs = refs[pos:pos + 3]
            pos += 3
        dq_ref, dk_ref, dv_ref = refs[pos:pos + 3]
        pos += 3
        dcb_ref = drow_ref = None
        if use_cb:
            dcb_ref, drow_ref = refs[pos:pos + 2]
            pos += 2
        kt_s, dkt_s, dvt_s = refs[pos:pos + 3]
        dqt_s = refs[pos + 3:pos + 3 + HEADS_PER_STEP]
        dcb_s, cbc_s = refs[pos + 3 + HEADS_PER_STEP:pos + 5 + HEADS_PER_STEP] if use_cb else (None, None)

        heads = [slice(h * HEAD_DIM, (h + 1) * HEAD_DIM) for h in range(HEADS_PER_STEP)]
        for cblk in range(nk):
            kt_s[cblk] = k_ref[cblk * tk:(cblk + 1) * tk, :].astype(F32).T.astype(BF16)
        dkt_s[...] = jnp.zeros_like(dkt_s)
        dvt_s[...] = jnp.zeros_like(dvt_s)
        if use_cb:
            dcb_s[...] = jnp.zeros_like(dcb_s)
            for cblk in range(nk):
                cbc_s[cblk] = _rows_to_cols([cb_ref[0, h, cblk] for h in range(HEADS_PER_STEP)])
        ones = jnp.ones((8, HEAD_DIM), BF16)

        def q_loop(qi, carry):
            qs = pl.multiple_of(qi * tq, tq)
            q_all = (q_ref[pl.ds(qs, tq), :].astype(F32) * ATTN_SCALE)
            do_all = do_ref[pl.ds(qs, tq), :]
            qt_all = q_all.T.astype(BF16)
            dot_all = do_all.T.astype(BF16)
            qt, dot, lse, dsum = [], [], [], []
            for h, hs in enumerate(heads):
                qt.append(qt_all[hs, :])
                dot.append(dot_all[hs, :])
                lse.append(lse_ref[0, h, qi])
                prod = do_all[:, hs] * o_ref[pl.ds(qs, tq), hs]
                hi = prod.astype(BF16)
                lo = (prod - hi.astype(F32)).astype(BF16)
                dsum.append((lax.dot_general(ones, hi, NT_DIMS, preferred_element_type=F32)
                             + lax.dot_general(ones, lo, NT_DIMS, preferred_element_type=F32))[0:1, :])
            for a in dqt_s:
                a[...] = jnp.zeros_like(a)

            def tile(kb, tab, k0, klen, q0, drow):
                ks = pl.multiple_of(kb * tk + k0, klen)
                keys = slice(k0, k0 + klen)
                sts, dpts, out = [], [], []
                for h, hs in enumerate(heads):
                    st = jnp.dot(k_ref[pl.ds(ks, klen), hs], qt[h][:, q0:], preferred_element_type=F32)
                    if tab is not None:
                        st = st + tab
                    if use_cb:
                        st = st + cbc_s[kb, keys, h * HEAD_DIM:h * HEAD_DIM + 1]
                    sts.append(st)
                    dpts.append(jnp.dot(v_ref[pl.ds(ks, klen), hs], dot[h][:, q0:], preferred_element_type=F32))
                for h, hs in enumerate(heads):
                    pt = jnp.exp(sts[h] - lse[h][:, q0:])
                    dst = pt * (dpts[h] - dsum[h][:, q0:])
                    dst_b = dst.astype(BF16)
                    dvt_s[h, kb, :, keys] += lax.dot_general(dot[h][:, q0:], pt.astype(BF16), NT_DIMS,
                                                             preferred_element_type=F32)
                    dkt_s[h, kb, :, keys] += lax.dot_general(qt[h][:, q0:], dst_b, NT_DIMS, preferred_element_type=F32)
                    dqt_s[h][:, q0:] += jnp.dot(kt_s[kb, hs, keys], dst_b, preferred_element_type=F32)
                    if use_cb:
                        dcb_s[h, pl.ds(ks, klen), :] += jnp.sum(dst, axis=1, keepdims=True)
                        dr = drow[h][:, q0:] + jnp.sum(dst, axis=0, keepdims=True)
                        out.append(dr if q0 == 0 else jnp.concatenate([drow[h][:, :q0], dr], axis=1))
                    else:
                        out.append(drow[h])
                return tuple(out)

            drow = lax.fori_loop(0, qi, lambda kb, c: tile(kb, tab_ref[qi - kb] if off_diag_bias else None, 0, tk, 0, c),
                                 tuple(jnp.zeros((1, tq), F32) for _ in heads))
            half = tk // 2
            drow = tile(qi, tab_ref[0, 0:half, :], 0, half, 0, drow)
            drow = tile(qi, tab_ref[0, half:, half:], half, half, half, drow)
            dq = (jnp.concatenate([a[...] for a in dqt_s], axis=0) * ATTN_SCALE).T
            if rope_refs is not None:
                dq = _rotate(dq, *[coef[pl.ds(qs, tq), :] for coef in rope_refs], True)
            dq_ref[pl.ds(qs, tq), :] = dq.astype(dq_ref.dtype)
            if use_cb:
                for h in range(HEADS_PER_STEP):
                    drow_ref[0, h, qi] = drow[h]
            return carry

        lax.fori_loop(0, nq, q_loop, 0)
        for cblk in range(nk):
            rows = slice(cblk * tk, (cblk + 1) * tk)
            dk = jnp.concatenate([dkt_s[h, cblk] for h in range(HEADS_PER_STEP)], axis=0).T
            if rope_refs is not None:
                dk = _rotate(dk, *[coef[rows, :] for coef in rope_refs], True)
            dk_ref[rows, :] = dk.astype(dk_ref.dtype)
            dv_ref[rows, :] = jnp.concatenate([dvt_s[h, cblk] for h in range(HEADS_PER_STEP)], axis=0).T.astype(dv_ref.dtype)
            if use_cb:
                for h in range(HEADS_PER_STEP):
                    dcb_ref[0, h, cblk] = jnp.broadcast_to(dcb_s[h, rows, :], (tk, LANES)).T[0:1, :]

    def seq_spec(off):
        return pl.BlockSpec((s, LANES), lambda b, j: (b, off + j))

    row_spec = pl.BlockSpec((1, HEADS_PER_STEP, nq, 1, tq), lambda b, j: (b, j, 0, 0, 0))
    in_specs = [seq_spec(q_off), seq_spec(k_off), seq_spec(v_off), seq_spec(0), row_spec, seq_spec(0),
                pl.BlockSpec(table.shape, lambda b, j: (0, 0, 0))]
    args = [q_arr, k_arr, v_arr, o_arr, lse_arr, do_arr, table]
    width = npairs * LANES
    out_specs = [seq_spec(0)] * 3
    out_shape = [jax.ShapeDtypeStruct((t, width), BF16)] * 3
    scratch = [pltpu.VMEM((nk, LANES, tk), BF16), pltpu.VMEM((HEADS_PER_STEP, nk, HEAD_DIM, tk), F32),
               pltpu.VMEM((HEADS_PER_STEP, nk, HEAD_DIM, tk), F32)] + [pltpu.VMEM((HEAD_DIM, tq), F32)] * HEADS_PER_STEP
    if use_cb:
        cb_spec = pl.BlockSpec((1, HEADS_PER_STEP, nk, 1, tk), lambda b, j: (b, j, 0, 0, 0))
        in_specs.append(cb_spec)
        args.append(colbias)
    if rope_tabs is not None:
        in_specs += [pl.BlockSpec((s, LANES), lambda b, j: (b, 0))] * 3
        args += list(rope_tabs)
    if use_cb:
        out_specs += [cb_spec, row_spec]
        out_shape += [jax.ShapeDtypeStruct(colbias.shape, F32), jax.ShapeDtypeStruct(lse_arr.shape, F32)]
        scratch += [pltpu.VMEM((HEADS_PER_STEP, s, 1), F32), pltpu.VMEM((nk, tk, LANES), F32)]
    return _pcall(
        body, side=side, name=name, grid=(nb, npairs), in_specs=in_specs, out_specs=out_specs, out_shape=out_shape,
        scratch_shapes=scratch, compiler_params=_params(("arbitrary", "arbitrary")),
    )(*args)


def ada_fwd(c_all, w_ada, b_cols, name):
    def body(c_ref, w_ref, b_ref, o_ref):
        cv = c_ref[...]
        sc = (cv * _sigmoid(cv)).astype(BF16)
        o_ref[...] = jnp.dot(sc, w_ref[...].astype(BF16), preferred_element_type=F32) + b_ref[...]

    return _pcall(body, name=name, out_shape=jax.ShapeDtypeStruct((c_all.shape[0], w_ada.shape[1]), F32),
                  compiler_params=_params())(c_all, w_ada, b_cols)


def ada_bwd(c_all, dmod_cols, name):
    def body(c_ref, d_ref, o_ref):
        cv = c_ref[...]
        sc = (cv * _sigmoid(cv)).astype(BF16)
        o_ref[...] = lax.dot_general(sc, d_ref[...].astype(BF16), TN_DIMS, preferred_element_type=F32)

    return _pcall(body, name=name, out_shape=jax.ShapeDtypeStruct((c_all.shape[1], dmod_cols.shape[1]), F32),
                  compiler_params=_params())(c_all, dmod_cols)


def adamw(parts, group, w, m, v, name, tr=None):
    n = parts.shape[0]
    r, c = w.shape
    tr = r if tr is None else tr
    c1 = 1.0 - ADAM_B1 ** ADAM_STEP
    c2 = 1.0 - ADAM_B2 ** ADAM_STEP

    def body(p_ref, w_ref, m_ref, v_ref, g_ref, d_ref, nm_ref, nv_ref):
        g = p_ref[0, 0].astype(F32)
        for i in range(1, n):
            g = g + p_ref[i, 0].astype(F32)
        wv = w_ref[...]
        nm = ADAM_B1 * m_ref[...] + (1.0 - ADAM_B1) * g
        nv = ADAM_B2 * v_ref[...] + (1.0 - ADAM_B2) * (g * g)
        g_ref[...] = g
        nm_ref[...] = nm
        nv_ref[...] = nv
        d_ref[...] = -ADAM_LR * ((nm / c1) / (jnp.sqrt(nv / c2) + ADAM_EPS) + ADAM_WD * wv)

    spec = pl.BlockSpec((tr, c), lambda i: (i, 0))
    shape = jax.ShapeDtypeStruct((r, c), F32)
    return _pcall(
        body, name=name, grid=(r // tr,),
        in_specs=[pl.BlockSpec((n, 1, tr, c), lambda i: (0, group, i, 0)), spec, spec, spec],
        out_specs=[spec] * 4, out_shape=[shape] * 4, compiler_params=_params(("arbitrary",)),
    )(parts, w, m, v)


def all_gather(arrs, name):
    n = len(arrs)
    hbm = pl.BlockSpec(memory_space=pl.ANY)

    def body(*refs):
        ins, outs = refs[:n], refs[n:2 * n]
        send_sems, recv_sems, local_sems = refs[2 * n:]
        x, y, c = _place()
        me, sibling = (x, y, c), (x, y, 1 - c)
        chips = [(1 - x, y), (x, 1 - y), (1 - x, 1 - y)]

        def copy(a, k, block, to, src=None):
            dst = outs[a].at[_slot(block)]
            return pltpu.make_async_remote_copy(
                src_ref=dst if src is None else src, dst_ref=dst, send_sem=send_sems.at[a * 7 + k],
                recv_sem=recv_sems.at[a * 7 + k], device_id=to, device_id_type=MESH)

        mine = [pltpu.make_async_copy(ins[a], outs[a].at[_slot(me)], local_sems.at[a]) for a in range(n)]
        for cp in mine:
            cp.start()
        first = []
        for a in range(n):
            first.append(copy(a, 0, me, sibling, src=ins[a]))
            first += [copy(a, 1 + j, me, (*chip, c), src=ins[a]) for j, chip in enumerate(chips)]
        for cp in first:
            cp.start()
        passed = []
        for a in range(n):
            for j, chip in enumerate(chips):
                copy(a, 1 + j, (*chip, c), me).wait_recv()
                cp = copy(a, 4 + j, (*chip, c), sibling)
                cp.start()
                passed.append(cp)
        for a in range(n):
            copy(a, 0, sibling, me).wait_recv()
            for j, chip in enumerate(chips):
                copy(a, 4 + j, (*chip, 1 - c), me).wait_recv()
        for cp in first + passed:
            cp.wait_send()
        for cp in mine:
            cp.wait()

    return _pcall(
        body, name=name, in_specs=[hbm] * n, out_specs=[hbm] * n,
        out_shape=[jax.ShapeDtypeStruct((N_DEV,) + a.shape, a.dtype) for a in arrs],
        scratch_shapes=[pltpu.SemaphoreType.DMA((7 * n,)), pltpu.SemaphoreType.DMA((7 * n,)),
                        pltpu.SemaphoreType.DMA((n,))],
        compiler_params=pltpu.CompilerParams(has_side_effects=True),
    )(*arrs)


def _t(w):
    return jnp.swapaxes(w, -1, -2)


def _rows_from_blocks(blocks, pad_to=None):
    full = blocks.reshape(-1, blocks.shape[2])
    if pad_to is not None and pad_to > full.shape[0]:
        full = jnp.pad(full, ((0, pad_to - full.shape[0]), (0, 0)))
    return full


def _rows_to_blocks(full, nrows):
    return full[:nrows].reshape(N_DEV, nrows // N_DEV, full.shape[1])


SMALL_ORDER = ("g_pre_ff1", "g_post_ff1", "g_pre_mix", "g_post_mix", "g_out_a", "g_out_b", "g_pre_ff2", "g_post_ff2",
               "b_forget")


def _pack_small(vals):
    rows = []
    for name in SMALL_ORDER:
        v = vals[name].reshape(1, -1)
        if v.shape[1] % LANES:
            v = jnp.pad(v, ((0, 0), (0, LANES - v.shape[1] % LANES)))
        rows.append(v)
    return jnp.concatenate(rows, axis=1)


def _unpack_small(row, sizes):
    out, pos = {}, 0
    for name in SMALL_ORDER:
        n = sizes[name]
        out[name] = row[:, pos:pos + n]
        pos += -(-n // LANES) * LANES
    return out


def _ffn_forward(x, mod, g_pre, g_post, wg, wu, wd, i0, nb, tag, target=None, side=None):
    h = prenorm_fwd(x, g_pre, mod, i0, i0 + 1, nb, f"{tag}_prenorm")
    res, side_out = ffn_up(h, wg, wu, f"{tag}_up", side=side), None
    if side is not None:
        res, side_out = res
    gate, up, act = res
    res = postnorm_fwd(x, [(act, wd)], g_post, mod, i0 + 2, 0.5, nb, f"{tag}_down_postnorm", target=target)
    out, y0 = (res[0] if target is None else tuple(res[:2])), res[-1]
    return out, (x, h, gate, up, act, y0), side_out


def _ffn_backward(dxo, saved, mod, g_pre, g_post, wg, wu, wd, i0, nb, tag, side=None, chain=False):
    x, h, gate, up, act, y0 = saved
    dy0, dg_post, dgate_mod = postnorm_bwd(dxo, y0, g_post, mod, i0 + 2, 0.5, nb, f"{tag}_postnorm_bwd")
    dwd = mm_tn(act, dy0, BF16, f"{tag}_dwd")
    res, side_out = ffn_down_bwd(dy0, wd, gate, up, f"{tag}_down_bwd", side=side), None
    if side is not None:
        res, side_out = res
    dgate, dup = res
    dh_pairs = [(dgate, wg), (dup, wu)]
    if chain:
        dwg, (dwd,) = mm_tn(dgate, h, BF16, f"{tag}_dwg", side=([_rows_to_blocks(dwd, D_FF)[:, None]], False))
        dwu, (dwg,) = mm_tn(dup, h, BF16, f"{tag}_dwu", side=([_rows_to_blocks(dwg, D_FF)[:, None]], False))
        (dx, dg_pre, dsc, dsh), (dwu,) = prenorm_bwd(dh_pairs, x, g_pre, mod, i0 + 1, dxo, nb, f"{tag}_dh_prenorm_bwd",
                                                     ts=DH_ROWS, side=([_rows_to_blocks(dwu, D_FF)[:, None]], False))
    else:
        dwg = mm_tn(dgate, h, BF16, f"{tag}_dwg")
        dwu = mm_tn(dup, h, BF16, f"{tag}_dwu")
        dx, dg_pre, dsc, dsh = prenorm_bwd(dh_pairs, x, g_pre, mod, i0 + 1, dxo, nb, f"{tag}_dh_prenorm_bwd", ts=DH_ROWS)
    return dx, dict(g_pre=dg_pre, g_post=dg_post, wg=dwg, wu=dwu, wd=dwd, mod=(dsh, dsc, dgate_mod)), side_out


def kernel(x, c, positions, w_ada, b_ada, g_pre_ff1, g_post_ff1, w_ff1_gate, w_ff1_up, w_ff1_down, g_pre_mix, g_post_mix, w_in, b_forget, g_out_a, g_out_b, w_out, g_pre_ff2, g_post_ff2, w_ff2_gate, w_ff2_up, w_ff2_down, loss_target, m_w_ada, m_b_ada, m_g_pre_ff1, m_g_post_ff1, m_w_ff1_gate, m_w_ff1_up, m_w_ff1_down, m_g_pre_mix, m_g_post_mix, m_w_in, m_b_forget, m_g_out_a, m_g_out_b, m_w_out, m_g_pre_ff2, m_g_post_ff2, m_w_ff2_gate, m_w_ff2_up, m_w_ff2_down, v_w_ada, v_b_ada, v_g_pre_ff1, v_g_post_ff1, v_w_ff1_gate, v_w_ff1_up, v_w_ff1_down, v_g_pre_mix, v_g_post_mix, v_w_in, v_b_forget, v_g_out_a, v_g_out_b, v_w_out, v_g_pre_ff2, v_g_post_ff2, v_w_ff2_gate, v_w_ff2_up, v_w_ff2_down):
    weights = dict(w_ada=w_ada, b_ada=b_ada, g_pre_ff1=g_pre_ff1, g_post_ff1=g_post_ff1, w_ff1_gate=w_ff1_gate,
                   w_ff1_up=w_ff1_up, w_ff1_down=w_ff1_down, g_pre_mix=g_pre_mix, g_post_mix=g_post_mix, w_in=w_in,
                   b_forget=b_forget, g_out_a=g_out_a, g_out_b=g_out_b, w_out=w_out, g_pre_ff2=g_pre_ff2,
                   g_post_ff2=g_post_ff2, w_ff2_gate=w_ff2_gate, w_ff2_up=w_ff2_up, w_ff2_down=w_ff2_down)
    mom_m = dict(w_ada=m_w_ada, b_ada=m_b_ada, g_pre_ff1=m_g_pre_ff1, g_post_ff1=m_g_post_ff1, w_ff1_gate=m_w_ff1_gate,
                 w_ff1_up=m_w_ff1_up, w_ff1_down=m_w_ff1_down, g_pre_mix=m_g_pre_mix, g_post_mix=m_g_post_mix,
                 w_in=m_w_in, b_forget=m_b_forget, g_out_a=m_g_out_a, g_out_b=m_g_out_b, w_out=m_w_out,
                 g_pre_ff2=m_g_pre_ff2, g_post_ff2=m_g_post_ff2, w_ff2_gate=m_w_ff2_gate, w_ff2_up=m_w_ff2_up,
                 w_ff2_down=m_w_ff2_down)
    mom_v = dict(w_ada=v_w_ada, b_ada=v_b_ada, g_pre_ff1=v_g_pre_ff1, g_post_ff1=v_g_post_ff1, w_ff1_gate=v_w_ff1_gate,
                 w_ff1_up=v_w_ff1_up, w_ff1_down=v_w_ff1_down, g_pre_mix=v_g_pre_mix, g_post_mix=v_g_post_mix,
                 w_in=v_w_in, b_forget=v_b_forget, g_out_a=v_g_out_a, g_out_b=v_g_out_b, w_out=v_w_out,
                 g_pre_ff2=v_g_pre_ff2, g_post_ff2=v_g_post_ff2, w_ff2_gate=v_w_ff2_gate, w_ff2_up=v_w_ff2_up,
                 w_ff2_down=v_w_ff2_down)
    order = list(weights)

    nb, s, d = x.shape
    t = nb * s
    me = _slot(_place())
    nbg = nb * N_DEV
    ada_cols = w_ada.shape[2]

    bf = lambda w: w[0].astype(BF16)
    bft = lambda w: _t(w)[0].astype(BF16)
    c_all, ff1_all = all_gather([c, jnp.stack([bft(w_ff1_gate), bft(w_ff1_up), bf(w_ff1_down)])], "gather_ff1")
    c_all = c_all.reshape(nbg, d)
    wg1, wu1, wd1 = (_rows_from_blocks(ff1_all[:, i], D_FF_PAD) for i in range(3))

    b_cols = lax.dynamic_slice(b_ada, (0, me * ada_cols), (1, ada_cols))
    mod_cols = ada_fwd(c_all, w_ada[0], b_cols, "ada_fwd")
    (mod_all,) = all_gather([mod_cols], "gather_mod")
    mod = lax.dynamic_slice(mod_all, (0, me * nb, 0), (N_DEV, nb, ada_cols))
    mod = mod.transpose(1, 0, 2).reshape(nb, N_MOD, d)

    xf = x.reshape(t, d)
    target = loss_target.reshape(t, d)

    x1, saved1, (w_in_all, w_out_all) = _ffn_forward(xf, mod, g_pre_ff1, g_post_ff1, wg1, wu1, wd1, 0, nb, "ff1",
                                                     side=([bft(w_in), bf(w_out)], True))
    w_in_t = _rows_from_blocks(w_in_all)
    n_qkv = 3 * (WIDTH_A + WIDTH_B)
    w_qkv_t = w_in_t[:n_qkv]
    w_f_t = jnp.pad(w_in_t[n_qkv:], ((0, LANES - N_HEADS_B), (0, 0)))
    w_o = _rows_from_blocks(w_out_all)
    w_o_a, w_o_b = w_o[:WIDTH_A], w_o[WIDTH_A:]

    h2 = prenorm_fwd(x1, g_pre_mix, mod, 3, 4, nb, "mix_prenorm")
    tables = rope_tables(positions)
    proj = mm_rows([(h2, w_qkv_t)], True, BF16, "mix_proj", rope=(tables, 2 * WIDTH_A))
    f_logit = mm_rows([(h2, w_f_t)], True, F32, "mix_forget")
    tab_a = dilated_table(s, ATTN_TQ, ATTN_TK)
    tab_b = causal_table(s, ATTN_TQ, ATTN_TK)
    ft = f_logit[:, :N_HEADS_B].reshape(nb, s, N_HEADS_B).transpose(0, 2, 1)
    bf_col = b_forget.reshape(N_HEADS_B, 1)
    colbias = fox_gate_fwd(ft, bf_col, "fox_gate").reshape(nb, N_HEADS_B, s // ATTN_TK, 1, ATTN_TK)
    pa = WIDTH_A // LANES
    (o_a, lse_a), (ff2_all,) = attn_fwd(
        proj, 0, proj, pa, proj, 2 * pa, tab_a, None, nb, "attn_a",
        side=([jnp.stack([bft(w_ff2_gate), bft(w_ff2_up), bf(w_ff2_down)])], True))
    wg2, wu2, wd2 = (_rows_from_blocks(ff2_all[:, i], D_FF_PAD) for i in range(3))
    o_b, lse_b = attn_fwd(proj, 3 * pa, proj, 4 * pa, proj, 5 * pa, tab_b, colbias, nb, "attn_b", off_diag_bias=False)
    m_a = prenorm_fwd(o_a, g_out_a, None, None, None, nb, "out_norm_a")
    m_b = prenorm_fwd(o_b, g_out_b, None, None, None, nb, "out_norm_b")
    x2, y0m = postnorm_fwd(x1, [(m_a, w_o_a), (m_b, w_o_b)], g_post_mix, mod, 5, 1.0, nb, "mix_out_postnorm")

    (dx3, loss_part), saved2, _ = _ffn_forward(x2, mod, g_pre_ff2, g_post_ff2, wg2, wu2, wd2, 6, nb, "ff2", target=target)
    loss = lax.psum(loss_part[0, 0], ("x", "y", "c"))

    dx2, gr2, _ = _ffn_backward(dx3, saved2, mod, g_pre_ff2, g_post_ff2, wg2, wu2, wd2, 6, nb, "ff2")
    ff2_blocks = [jnp.stack([_rows_to_blocks(gr2[k], D_FF) for k in ("wg", "wu", "wd")], axis=1)]

    dy0m, dg_post_mix, dgate_mix = postnorm_bwd(dx2, y0m, g_post_mix, mod, 5, 1.0, nb, "mix_postnorm_bwd")
    dw_o_a = mm_tn(m_a, dy0m, BF16, "mix_dwo_a")
    dw_o_b = mm_tn(m_b, dy0m, BF16, "mix_dwo_b")
    do_a, dg_out_a = prenorm_bwd([(dy0m, w_o_a.T)], o_a, g_out_a, None, None, None, nb, "out_norm_a_bwd")
    do_b, dg_out_b = prenorm_bwd([(dy0m, w_o_b.T)], o_b, g_out_b, None, None, None, nb, "out_norm_b_bwd")
    (dq_a, dk_a, dv_a), (g_ff2,) = attn_bwd(proj, 0, proj, pa, proj, 2 * pa, o_a, lse_a, do_a, tab_a, None, nb,
                                            "attn_a_bwd", side=(ff2_blocks, False), rope_tabs=tables)
    dq_b, dk_b, dv_b, dcb, drow = attn_bwd(proj, 3 * pa, proj, 4 * pa, proj, 5 * pa, o_b, lse_b, do_b, tab_b, colbias, nb,
                                           "attn_b_bwd", off_diag_bias=False)
    dz_t, db_forget = fox_gate_bwd(dcb.reshape(nb, N_HEADS_B, s), drow.reshape(nb, N_HEADS_B, s), ft, bf_col,
                                   "fox_gate_bwd")
    dz = jnp.pad(dz_t.transpose(0, 2, 1).reshape(t, N_HEADS_B), ((0, 0), (0, LANES - N_HEADS_B))).astype(BF16)
    pieces = [dq_a, dk_a, dv_a, dq_b, dk_b, dv_b]
    w_pieces = [w_qkv_t[i * WIDTH_A:(i + 1) * WIDTH_A] for i in range(6)]
    dh2_pairs = list(zip(pieces, w_pieces)) + [(dz, w_f_t)]
    dw_in_t = jnp.concatenate([mm_tn(p, h2, BF16, f"mix_dwin_{i}") for i, p in enumerate(pieces)]
                              + [mm_tn(dz, h2, BF16, "mix_dwin_f")[:N_HEADS_B]], axis=0)
    dx1, dg_pre_mix, dsc_mix, dsh_mix = prenorm_bwd(dh2_pairs, x1, g_pre_mix, mod, 4, dx2, nb, "mix_dh_prenorm_bwd",
                                                    ts=DH_ROWS)

    g_in = _rows_to_blocks(dw_in_t, dw_in_t.shape[0])[:, None]
    g_out = _rows_to_blocks(jnp.concatenate([dw_o_a, dw_o_b], axis=0), d)[:, None]
    dx0, gr1, (g_in, g_out) = _ffn_backward(dx1, saved1, mod, g_pre_ff1, g_post_ff1, wg1, wu1, wd1, 0, nb, "ff1",
                                            side=([g_in, g_out], False), chain=True)
    grad_x = dx0.reshape(nb, s, d)

    dmod =jnp.concatenate(list(gr1["mod"]) + [dsh_mix, dsc_mix, dgate_mix] + list(gr2["mod"]), axis=1)
    small = _pack_small(dict(g_pre_ff1=gr1["g_pre"], g_post_ff1=gr1["g_post"], g_pre_mix=dg_pre_mix,
                             g_post_mix=dg_post_mix, g_out_a=dg_out_a, g_out_b=dg_out_b, g_pre_ff2=gr2["g_pre"],
                             g_post_ff2=gr2["g_post"], b_forget=db_forget))
    dmod_all, small_all = all_gather([dmod.reshape(nb, N_MOD * d), small], "gather_small_grads")
    dmod_all = dmod_all.reshape(nbg, N_MOD * d)

    res = {}
    def adamw_t(parts, group, n):
        return tuple(_t(r) for r in adamw(parts, group, _t(weights[n])[0], _t(mom_m[n])[0], _t(mom_v[n])[0], f"adamw_{n}"))

    res["w_ff1_gate"] = adamw_t(gr1["wg"], 0, "w_ff1_gate")
    res["w_ff1_up"] = adamw_t(gr1["wu"], 0, "w_ff1_up")
    res["w_ff2_gate"] = adamw_t(g_ff2, 0, "w_ff2_gate")
    res["w_ff2_up"] = adamw_t(g_ff2, 1, "w_ff2_up")
    res["w_ff1_down"] = adamw(gr1["wd"], 0, w_ff1_down[0], m_w_ff1_down[0], v_w_ff1_down[0], "adamw_ff1_down")
    res["w_ff2_down"] = adamw(g_ff2, 2, w_ff2_down[0], m_w_ff2_down[0], v_w_ff2_down[0], "adamw_ff2_down")
    res["w_in"] = adamw_t(g_in, 0, "w_in")
    res["w_out"] = adamw(g_out, 0, w_out[0], m_w_out[0], v_w_out[0], "adamw_out")
    dmod_cols = lax.dynamic_slice(dmod_all, (0, me * ada_cols), (nbg, ada_cols))
    dw_ada = ada_bwd(c_all, dmod_cols, "ada_bwd")
    res["w_ada"] = adamw(dw_ada[None, None], 0, w_ada[0], m_w_ada[0], v_w_ada[0], "adamw_ada", tr=256)
    res["b_ada"] = adamw(dmod_all[:, None, None], 0, b_ada, m_b_ada, v_b_ada, "adamw_b_ada")
    sizes = {n: weights[n].shape[1] for n in SMALL_ORDER}
    small_res = adamw(small_all[:, None], 0, _pack_small(weights), _pack_small(mom_m), _pack_small(mom_v), "adamw_small")
    small_res = [_unpack_small(r, sizes) for r in small_res]
    for n in SMALL_ORDER:
        res[n] = tuple(r[n] for r in small_res)

    outs = [loss, grad_x]
    for kind in range(4):
        for n in order:
            a = res[n][kind]
            outs.append(a.reshape(weights[n].shape))
    return tuple(outs)
```

```python
import functools

import jax
import jax.numpy as jnp
from jax import lax
from jax.experimental import pallas as pl
from jax.experimental.pallas import tpu as pltpu

F32 = jnp.float32
BF16 = jnp.bfloat16

D_MODEL = 1024
HEAD_DIM = 64
N_HEADS_A = 8
N_HEADS_B = 8
WIDTH_A = N_HEADS_A * HEAD_DIM
WIDTH_B = N_HEADS_B * HEAD_DIM
DILATED_PATTERNS = ((128, 1), (512, 4), (2048, 16))
ROT_DIM = HEAD_DIM // 4
ROPE_THETA = 500000.0
D_FF = 2752
D_FF_PAD = 2816
N_MOD = 9
EPS = 1e-6
ATTN_SCALE = HEAD_DIM ** -0.5
NEG = -1e30
N_DEV = 8
LANES = 128
HEADS_PER_STEP = LANES // HEAD_DIM

ADAM_LR = 0.001
ADAM_B1 = 0.9
ADAM_B2 = 0.999
ADAM_EPS = 1e-08
ADAM_WD = 0.01
ADAM_STEP = 10

VMEM_LIMIT = 56 * 1024 * 1024
MESH = pl.DeviceIdType.MESH

NT_DIMS = (((1,), (1,)), ((), ()))
TN_DIMS = (((0,), (0,)), ((), ()))
NN_DIMS = (((1,), (0,)), ((), ()))


def _place():
    return lax.axis_index("x"), lax.axis_index("y"), lax.axis_index("c")


def _slot(p):
    return 4 * p[0] + 2 * p[1] + p[2]


def _direct_copies(ins, outs, send_sems, recv_sems, local_sems, gather):
    x, y, c = _place()
    me = (x, y, c)
    flip = lambda v, bit: 1 - v if bit else v
    peers = [(flip(x, k & 4), flip(y, k & 2), flip(c, k & 1)) for k in range(1, N_DEV)]
    local, sends, recvs = [], [], []
    for a in range(len(ins)):
        mine = ins[a] if gather else ins[a].at[_slot(me)]
        local.append(pltpu.make_async_copy(mine, outs[a].at[_slot(me)], local_sems.at[a]))
        for k, peer in enumerate(peers):
            sems = dict(send_sem=send_sems.at[a * 7 + k], recv_sem=recv_sems.at[a * 7 + k], device_id=peer,
                        device_id_type=MESH)
            sends.append(pltpu.make_async_remote_copy(
                src_ref=ins[a] if gather else ins[a].at[_slot(peer)], dst_ref=outs[a].at[_slot(me)], **sems))
            recvs.append(pltpu.make_async_remote_copy(src_ref=mine, dst_ref=outs[a].at[_slot(peer)], **sems))
    return local, sends, recvs


def _comm_scratch(n):
    return [pltpu.SemaphoreType.DMA((7 * n,)), pltpu.SemaphoreType.DMA((7 * n,)), pltpu.SemaphoreType.DMA((n,))]


def _pcall(body, side=None, **kw):
    if side is None:
        return pl.pallas_call(body, **kw)
    arrs, gather = side
    n = len(arrs)
    grid = kw["grid"]
    in_specs = list(kw["in_specs"])
    single = not isinstance(kw["out_specs"], (list, tuple))
    out_specs = [kw["out_specs"]] if single else list(kw["out_specs"])
    out_shape = [kw["out_shape"]] if single else list(kw["out_shape"])
    scratch = list(kw.get("scratch_shapes", []))
    n_in, n_out, n_scr = len(in_specs), len(out_specs), len(scratch)
    hbm = pl.BlockSpec(memory_space=pl.ANY)

    def hosted(*refs):
        pos = [0]

        def take(k):
            pos[0] += k
            return refs[pos[0] - k:pos[0]]

        ins, s_ins, outs, s_outs, scr, sems = take(n_in), take(n), take(n_out), take(n), take(n_scr), take(3)
        ids = [pl.program_id(i) for i in range(len(grid))]
        first = functools.reduce(jnp.logical_and, [i == 0 for i in ids])
        last = functools.reduce(jnp.logical_and, [i == g - 1 for i, g in zip(ids, grid)])

        @pl.when(first)
        def _():
            local, sends, _ = _direct_copies(s_ins, s_outs, *sems, gather)
            for cp in local + sends:
                cp.start()

        body(*ins, *outs, *scr)

        @pl.when(last)
        def _():
            local, sends, recvs = _direct_copies(s_ins, s_outs, *sems, gather)
            for cp in recvs:
                cp.wait_recv()
            for cp in sends:
                cp.wait_send()
            for cp in local:
                cp.wait()

    kw.update(in_specs=in_specs + [hbm] * n, out_specs=out_specs + [hbm] * n,
              out_shape=out_shape + [jax.ShapeDtypeStruct(((N_DEV,) + a.shape) if gather else a.shape, a.dtype)
                                     for a in arrs],
              scratch_shapes=scratch + _comm_scratch(n))
    call = pl.pallas_call(hosted, **kw)

    def run(*args):
        res = call(*args, *arrs)
        main = res[0] if single else list(res[:n_out])
        return main, list(res[n_out:])

    return run


def _params(sem=None, **kw):
    if sem is not None:
        kw["dimension_semantics"] = sem
    return pltpu.CompilerParams(vmem_limit_bytes=VMEM_LIMIT, **kw)


def _rotate(xv, c, sp, sm, transpose):
    width = xv.shape[1]
    half = ROT_DIM // 2
    if transpose:
        return xv * c + pltpu.roll(xv * sp, width - half, 1) + pltpu.roll(xv * sm, half, 1)
    return xv * c + pltpu.roll(xv, half, 1) * sp + pltpu.roll(xv, width - half, 1) * sm


def mm_rows(pairs, trans_b, out_dtype, name, tm=512, side=None, rope=None):
    n = len(pairs)
    m = pairs[0][0].shape[0]
    n_out = pairs[0][1].shape[0 if trans_b else 1]
    dims = NT_DIMS if trans_b else NN_DIMS

    def body(*refs):
        o_ref = refs[-1]
        acc = None
        for a_ref, b_ref in zip(refs[:n], refs[n:2 * n]):
            d = lax.dot_general(a_ref[...], b_ref[...], dims, preferred_element_type=F32)
            acc = d if acc is None else acc + d
        if rope is None:
            o_ref[...] = acc.astype(o_ref.dtype)
        else:
            width = rope[1]
            c, sp, sm = (jnp.concatenate([r[...]] * (width // LANES), axis=1) for r in refs[2 * n:2 * n + 3])
            o_ref[:, :width] = _rotate(acc[:, :width], c, sp, sm, False).astype(o_ref.dtype)
            o_ref[:, width:] = acc[:, width:].astype(o_ref.dtype)

    in_specs = [pl.BlockSpec((tm, a.shape[1]), lambda i: (i, 0)) for a, _ in pairs]
    in_specs += [pl.BlockSpec(b.shape, lambda i: (0, 0)) for _, b in pairs]
    args = [a for a, _ in pairs] + [b for _, b in pairs]
    if rope is not None:
        in_specs += [pl.BlockSpec((tm, LANES), lambda i: (i, 0))] * 3
        args += list(rope[0])
    return _pcall(
        body, side=side, name=name, grid=(m // tm,), in_specs=in_specs,
        out_specs=pl.BlockSpec((tm, n_out), lambda i: (i, 0)),
        out_shape=jax.ShapeDtypeStruct((m, n_out), out_dtype),
        compiler_params=_params(("arbitrary",)),
    )(*args)


DH_ROWS = 256
TN_TOKENS = 2048
TN_OUT_ELEMS = 2 * 1024 * 1024


def mm_tn(a, b, out_dtype, name, side=None):
    t, ka = a.shape
    n_out = b.shape[1]
    tk = min(TN_TOKENS, t)
    tka = ka // 2 if ka * n_out > TN_OUT_ELEMS else ka
    tn = n_out
    steps = t // tk

    def body(a_ref, b_ref, o_ref, acc_ref):
        k = pl.program_id(2)
        d = lax.dot_general(a_ref[...], b_ref[...], TN_DIMS, preferred_element_type=F32)

        @pl.when(k == 0)
        def _():
            acc_ref[...] = d

        @pl.when(k > 0)
        def _():
            acc_ref[...] += d

        @pl.when(k == steps - 1)
        def _():
            o_ref[...] = acc_ref[...].astype(o_ref.dtype)

    return _pcall(
        body, side=side, name=name, grid=(ka // tka, n_out // tn, steps),
        in_specs=[pl.BlockSpec((tk, tka), lambda i, j, k: (k, i)), pl.BlockSpec((tk, tn), lambda i, j, k: (k, j))],
        out_specs=pl.BlockSpec((tka, tn), lambda i, j, k: (i, j)),
        out_shape=jax.ShapeDtypeStruct((ka, n_out), out_dtype),
        scratch_shapes=[pltpu.VMEM((tka, tn), F32)],
        compiler_params=_params(("arbitrary", "arbitrary", "arbitrary")),
    )(a, b)


def _col_chunks(width, chunk=512):
    return [slice(c, min(c + chunk, width)) for c in range(0, width, chunk)]


def _sigmoid(x):
    return 1.0 / (1.0 + jnp.exp(-x))


def ffn_up(h, wgt, wut, name, tm=256, tn=D_FF_PAD, side=None):
    t, d = h.shape
    fp = wgt.shape[0]

    def body(h_ref, wg_ref, wu_ref, g_ref, u_ref, a_ref):
        hv = h_ref[...]

        def finish(cols, g, u):
            g_ref[:, cols] = g.astype(BF16)
            u_ref[:, cols] = u.astype(BF16)
            a_ref[:, cols] = (g * _sigmoid(g) * u).astype(BF16)

        pending = None
        for cols in _col_chunks(tn):
            g = lax.dot_general(hv, wg_ref[cols, :], NT_DIMS, preferred_element_type=F32)
            u = lax.dot_general(hv, wu_ref[cols, :], NT_DIMS, preferred_element_type=F32)
            if pending is not None:
                finish(*pending)
            pending = (cols, g, u)
        finish(*pending)

    w_spec = pl.BlockSpec((tn, d), lambda j, i: (j, 0))
    o_spec = pl.BlockSpec((tm, tn), lambda j, i: (i, j))
    o_shape = jax.ShapeDtypeStruct((t, fp), BF16)
    return _pcall(
        body, side=side, name=name, grid=(fp // tn, t // tm),
        in_specs=[pl.BlockSpec((tm, d), lambda j, i: (i, 0)), w_spec, w_spec],
        out_specs=[o_spec, o_spec, o_spec], out_shape=[o_shape, o_shape, o_shape],
        compiler_params=_params(("arbitrary", "arbitrary")),
    )(h, wgt, wut)


def ffn_down_bwd(dy0, wd, gate, up, name, tm=256, tn=D_FF_PAD, side=None):
    t, d = dy0.shape
    fp = wd.shape[0]

    def body(dy_ref, wd_ref, g_ref, u_ref, dg_ref, du_ref):
        dyv = dy_ref[...]

        def finish(cols, dact):
            g = g_ref[:, cols].astype(F32)
            u = u_ref[:, cols].astype(F32)
            sg = _sigmoid(g)
            silu = g * sg
            du_ref[:, cols] = (dact * silu).astype(BF16)
            dg_ref[:, cols] = ((dact * u) * (sg + silu * (1.0 - sg))).astype(BF16)

        pending = None
        for cols in _col_chunks(tn):
            dact = lax.dot_general(dyv, wd_ref[cols, :], NT_DIMS, preferred_element_type=F32)
            if pending is not None:
                finish(*pending)
            pending = (cols, dact)
        finish(*pending)

    t_spec = pl.BlockSpec((tm, tn), lambda j, i: (i, j))
    o_shape = jax.ShapeDtypeStruct((t, fp), BF16)
    return _pcall(
        body, side=side, name=name, grid=(fp // tn, t // tm),
        in_specs=[pl.BlockSpec((tm, d), lambda j, i: (i, 0)), pl.BlockSpec((tn, d), lambda j, i: (j, 0)), t_spec, t_spec],
        out_specs=[t_spec, t_spec], out_shape=[o_shape, o_shape],
        compiler_params=_params(("arbitrary", "arbitrary")),
    )(dy0, wd, gate, up)


def _row_specs(dx, ts, ns, b0=0):
    return pl.BlockSpec((ts, dx), lambda b, s: ((b + b0) * ns + s, 0))


def _mod_spec(b0=0):
    return pl.BlockSpec((1, N_MOD, D_MODEL), lambda b, s: (b + b0, 0, 0))


def _vec_spec(dx):
    return pl.BlockSpec((1, dx), lambda b, s: (0, 0))


def prenorm_fwd(x, g, mod, i_shift, i_scale, nb, name, ts=1024):
    t, dx = x.shape
    ts = min(ts, t // nb)
    ns = t // nb // ts

    def body(*refs):
        if mod is None:
            x_ref, g_ref, h_ref = refs
        else:
            x_ref, g_ref, mod_ref, h_ref = refs
        xv = x_ref[...]
        r = lax.rsqrt(jnp.mean(xv * xv, axis=-1, keepdims=True) + EPS)
        h = xv * r * g_ref[...]
        if mod is not None:
            h = h * (1.0 + mod_ref[0, i_scale:i_scale + 1, :]) + mod_ref[0, i_shift:i_shift + 1, :]
        h_ref[...] = h.astype(BF16)

    in_specs = [_row_specs(dx, ts, ns), _vec_spec(dx)]
    args = [x, g]
    if mod is not None:
        in_specs.append(_mod_spec())
        args.append(mod)
    return _pcall(
        body, name=name, grid=(nb, ns), in_specs=in_specs, out_specs=_row_specs(dx, ts, ns),
        out_shape=jax.ShapeDtypeStruct((t, dx), BF16), compiler_params=_params(("arbitrary", "arbitrary")),
    )(*args)


def prenorm_bwd(dh, x, g, mod, i_scale, dres, nb, name, ts=512, side=None, seqs=None, carry=None):
    t, dx = x.shape
    ts = min(ts, t // nb)
    ns = t // nb // ts
    b0, nbc = (0, nb) if seqs is None else seqs
    has_mod = mod is not None
    has_res = dres is not None
    pairs = dh if isinstance(dh, list) else None
    n_mm = 0 if pairs is None else len(pairs)

    def body(*refs):
        refs = list(refs)
        if pairs is None:
            dhv = refs[0][...].astype(F32)
            refs = refs[1:]
        else:
            dhv = None
            for a_ref, b_ref in zip(refs[:n_mm], refs[n_mm:2 * n_mm]):
                d = jnp.dot(a_ref[...], b_ref[...], preferred_element_type=F32)
                dhv = d if dhv is None else dhv + d
            refs = refs[2 * n_mm:]
        x_ref, g_ref = refs[:2]
        pos = 2
        mod_ref = dres_ref = None
        if has_mod:
            mod_ref = refs[pos]
            pos += 1
        if has_res:
            dres_ref = refs[pos]
            pos += 1
        dg0_ref = None
        if carry is not None:
            dg0_ref = refs[pos + 1]
            pos += 2
        dx_ref, dg_ref = refs[pos], refs[pos + 1]
        b, s = pl.program_id(0), pl.program_id(1)
        xv = x_ref[...]
        gv = g_ref[...]
        r = lax.rsqrt(jnp.mean(xv * xv, axis=-1, keepdims=True) + EPS)
        xhat = xv * r
        dn = dhv
        if has_mod:
            dsc_ref, dsh_ref = refs[pos + 2], refs[pos + 3]
            dn = dhv * (1.0 + mod_ref[0, i_scale:i_scale + 1, :])
            dsc = jnp.sum(dhv * xhat * gv, axis=0, keepdims=True)[None]
            dsh = jnp.sum(dhv, axis=0, keepdims=True)[None]

            @pl.when(s == 0)
            def _():
                dsc_ref[...] = dsc
                dsh_ref[...] = dsh

            @pl.when(s > 0)
            def _():
                dsc_ref[...] += dsc
                dsh_ref[...] += dsh

        dg = jnp.sum(dn * xhat, axis=0, keepdims=True)
        first = jnp.logical_and(b == 0, s == 0)

        @pl.when(first)
        def _():
            dg_ref[...] = dg if dg0_ref is None else dg0_ref[...] + dg

        @pl.when(jnp.logical_not(first))
        def _():
            dg_ref[...] += dg

        dxhat = dn * gv
        dxv = r * (dxhat - xhat * jnp.mean(dxhat * xhat, axis=-1, keepdims=True))
        if has_res:
            dxv = dxv + dres_ref[...]
        dx_ref[...] = dxv

    row = _row_specs(dx, ts, ns, b0)
    if pairs is None:
        in_specs, args = [row], [dh]
    else:
        in_specs = [_row_specs(a.shape[1], ts, ns, b0) for a, _ in pairs]
        in_specs += [pl.BlockSpec(b.shape, lambda b_, s_: (0, 0)) for _, b in pairs]
        args = [a for a, _ in pairs] + [b for _, b in pairs]
    in_specs += [row, _vec_spec(dx)]
    args += [x, g]
    if has_mod:
        in_specs.append(_mod_spec(b0))
        args.append(mod)
    if has_res:
        in_specs.append(row)
        args.append(dres)
    aliases = {}
    if carry is not None:
        aliases = {len(in_specs): 0}
        in_specs += [pl.BlockSpec(memory_space=pl.ANY), _vec_spec(dx)]
        args += list(carry)
    out_specs = [row, _vec_spec(dx)]
    out_shape = [jax.ShapeDtypeStruct((t, dx), F32), jax.ShapeDtypeStruct((1, dx), F32)]
    if has_mod:
        bspec = pl.BlockSpec((1, 1, dx), lambda b, s: (b, 0, 0))
        out_specs += [bspec, bspec]
        out_shape += [jax.ShapeDtypeStruct((nbc, 1, dx), F32)] * 2
    return _pcall(
        body, side=side, name=name, grid=(nbc, ns), in_specs=in_specs, out_specs=out_specs, out_shape=out_shape,
        input_output_aliases=aliases, compiler_params=_params(("arbitrary", "arbitrary")),
    )(*args)


def postnorm_fwd(x, pairs, g, mod, i_gate, coef, nb, name, target=None, ts=512):
    t, dx = x.shape
    with_loss = target is not None
    ts = min(ts, t // nb)
    ns = t // nb // ts
    n_mm = len(pairs)

    def body(*refs):
        yv = None
        for a_ref, b_ref in zip(refs[:n_mm], refs[n_mm:2 * n_mm]):
            d = jnp.dot(a_ref[...], b_ref[...], preferred_element_type=F32)
            yv = d if yv is None else yv + d
        refs = refs[2 * n_mm:]
        x_ref, g_ref, mod_ref = refs[:3]
        refs[-1][...] = yv
        r = lax.rsqrt(jnp.mean(yv * yv, axis=-1, keepdims=True) + EPS)
        out = x_ref[...] + (coef * mod_ref[0, i_gate:i_gate + 1, :]) * (yv * r * g_ref[...])
        if not with_loss:
            refs[3][...] = out
            return
        t_ref, dx_ref, loss_ref = refs[3:6]
        b, s = pl.program_id(0), pl.program_id(1)
        err = out - t_ref[...]
        dx_ref[...] = err * (1.0 / dx)
        part = (0.5 / dx) * jnp.sum(jnp.sum(err * err, axis=1, keepdims=True), axis=0, keepdims=True)
        first = jnp.logical_and(b == 0, s == 0)

        @pl.when(first)
        def _():
            loss_ref[...] = part

        @pl.when(jnp.logical_not(first))
        def _():
            loss_ref[...] += part

    row = _row_specs(dx, ts, ns)
    in_specs = [_row_specs(a.shape[1], ts, ns) for a, _ in pairs]
    in_specs += [pl.BlockSpec(b.shape, lambda b_, s_: (0, 0)) for _, b in pairs]
    in_specs += [row, _vec_spec(dx), _mod_spec()]
    args = [a for a, _ in pairs] + [b for _, b in pairs] + [x, g, mod]
    row_shape = jax.ShapeDtypeStruct((t, dx), F32)
    out_specs, out_shape = [row, row], [row_shape, row_shape]
    if with_loss:
        in_specs.append(row)
        args.append(target)
        out_specs = [row, pl.BlockSpec((1, 1), lambda b, s: (0, 0)), row]
        out_shape = [row_shape, jax.ShapeDtypeStruct((1, 1), F32), row_shape]
    return _pcall(
        body, name=name, grid=(nb, ns), in_specs=in_specs, out_specs=out_specs, out_shape=out_shape,
        compiler_params=_params(("arbitrary", "arbitrary")),
    )(*args)


def postnorm_bwd(dxo, y0, g, mod, i_gate, coef, nb, name, ts=1024):
    t, dx = y0.shape
    ts = min(ts, t // nb)
    ns = t // nb // ts

    def body(d_ref, y_ref, g_ref, mod_ref, dy_ref, dg_ref, dgate_ref):
        b, s = pl.program_id(0), pl.program_id(1)
        yv = y_ref[...]
        dv = d_ref[...]
        gv = g_ref[...]
        r = lax.rsqrt(jnp.mean(yv * yv, axis=-1, keepdims=True) + EPS)
        yhat = yv * r
        dgate = jnp.sum(dv * (coef * (yhat * gv)), axis=0, keepdims=True)[None]
        dyn = dv * (coef * mod_ref[0, i_gate:i_gate + 1, :])
        dg = jnp.sum(dyn * yhat, axis=0, keepdims=True)
        dyhat = dyn * gv
        dy_ref[...] = (r * (dyhat - yhat * jnp.mean(dyhat * yhat, axis=-1, keepdims=True))).astype(BF16)

        @pl.when(s == 0)
        def _():
            dgate_ref[...] = dgate

        @pl.when(s > 0)
        def _():
            dgate_ref[...] += dgate

        first = jnp.logical_and(b == 0, s == 0)

        @pl.when(first)
        def _():
            dg_ref[...] = dg

        @pl.when(jnp.logical_not(first))
        def _():
            dg_ref[...] += dg

    row = _row_specs(dx, ts, ns)
    return _pcall(
        body, name=name, grid=(nb, ns), in_specs=[row, row, _vec_spec(dx), _mod_spec()],
        out_specs=[row, _vec_spec(dx), pl.BlockSpec((1, 1, dx), lambda b, s: (b, 0, 0))],
        out_shape=[jax.ShapeDtypeStruct((t, dx), BF16), jax.ShapeDtypeStruct((1, dx), F32),
                   jax.ShapeDtypeStruct((nb, 1, dx), F32)],
        compiler_params=_params(("arbitrary", "arbitrary")),
    )(dxo, y0, g, mod)


def rope_tables(positions):
    inv_freq = ROPE_THETA ** (-jnp.arange(0, ROT_DIM, 2, dtype=F32) / ROT_DIM)
    ang = positions.astype(F32).reshape(-1, 1) * inv_freq
    cos, sin = jnp.cos(ang), jnp.sin(ang)
    half = ROT_DIM // 2
    z = lambda n: jnp.zeros((ang.shape[0], n), F32)
    c = jnp.concatenate([cos, cos, jnp.ones((ang.shape[0], HEAD_DIM - ROT_DIM), F32)], axis=1)
    sp = jnp.concatenate([z(half), sin, z(HEAD_DIM - ROT_DIM)], axis=1)
    sm = jnp.concatenate([-sin, z(HEAD_DIM - half)], axis=1)
    return tuple(jnp.tile(a, (1, HEADS_PER_STEP)) for a in (c, sp, sm))


def _scan_lanes(x, reverse):
    n = x.shape[-1]
    lane = lax.broadcasted_iota(jnp.int32, x.shape, x.ndim - 1)
    k = 1
    while k < n:
        if reverse:
            x = x + jnp.where(lane < n - k, pltpu.roll(x, n - k, x.ndim - 1), 0.0)
        else:
            x = x + jnp.where(lane >= k, pltpu.roll(x, k, x.ndim - 1), 0.0)
        k *= 2
    return x


def _log_sigmoid(z):
    return jnp.minimum(z, 0.0) - jnp.log(1.0 + jnp.exp(-jnp.abs(z)))


def fox_gate_fwd(ft, b_forget, name):
    nb, nh, s = ft.shape

    def body(f_ref, b_ref, o_ref):
        z = f_ref[0] + b_ref[...]
        o_ref[0] = -_scan_lanes(_log_sigmoid(z), False)

    spec = pl.BlockSpec((1, nh, s), lambda b: (b, 0, 0))
    return _pcall(
        body, name=name, grid=(nb,), in_specs=[spec, pl.BlockSpec((nh, 1), lambda b: (0, 0))], out_specs=spec,
        out_shape=jax.ShapeDtypeStruct((nb, nh, s), F32), compiler_params=_params(("arbitrary",)),
    )(ft, b_forget)


def fox_gate_bwd(dcb, drow, ft, b_forget, name):
    nb, nh, s = ft.shape

    def body(d_ref, r_ref, f_ref, b_ref, dz_ref, db_ref):
        b = pl.program_id(0)
        z = f_ref[0] + b_ref[...]
        dlf = _scan_lanes(r_ref[0] - d_ref[0], True)
        dz = dlf * _sigmoid(-z)
        dz_ref[0] = dz
        db = jnp.sum(dz, axis=1, keepdims=True)

        @pl.when(b == 0)
        def _():
            db_ref[...] = db

        @pl.when(b > 0)
        def _():
            db_ref[...] += db

    spec = pl.BlockSpec((1, nh, s), lambda b: (b, 0, 0))
    vec = pl.BlockSpec((nh, 1), lambda b: (0, 0))
    return _pcall(
        body, name=name, grid=(nb,), in_specs=[spec, spec, spec, vec], out_specs=[spec, vec],
        out_shape=[jax.ShapeDtypeStruct((nb, nh, s), F32), jax.ShapeDtypeStruct((nh, 1), F32)],
        compiler_params=_params(("arbitrary",)),
    )(dcb, drow, ft, b_forget)


ATTN_TQ = 512
ATTN_TK = 512
ONES_ROWS = 16


def _rows_to_cols(rows):
    tile = jnp.concatenate([jnp.broadcast_to(rw, (HEAD_DIM, rw.shape[1])) for rw in rows], axis=0)
    return tile.T


def _block_delta(s, tq, tk):
    off = jnp.arange(s // tk) - (tq // tk - 1)
    return off[:, None, None] * tk + jnp.arange(tq)[None, None, :] - jnp.arange(tk)[None, :, None]


def dilated_table(s, tq, tk):
    delta = _block_delta(s, tq, tk)
    count = jnp.zeros(delta.shape, F32)
    for window, dil in DILATED_PATTERNS:
        count = count + ((delta >= 0) & (delta <= window) & (delta % dil == 0)).astype(F32)
    return jnp.where(count > 0, jnp.log(jnp.maximum(count, 1.0)), NEG)


def causal_table(s, tq, tk):
    return jnp.where(_block_delta(s, tq, tk) >= 0, 0.0, NEG).astype(F32)


def attn_fwd(q_arr, q_off, k_arr, k_off, v_arr, v_off, table, colbias, nb, name, side=None, off_diag_bias=True):
    t = q_arr.shape[0]
    s = t // nb
    tk, tq = table.shape[1:]
    assert tq == tk, "the diagonal handling below is written for square tiles"
    nq, nk = s // tq, s // tk
    npairs = WIDTH_A // LANES
    use_cb = colbias is not None

    def body(*refs):
        refs = list(refs)
        q_ref, k_ref, v_ref, tab_ref = refs[:4]
        cb_ref = refs[4] if use_cb else None
        tail = refs[-(HEADS_PER_STEP + int(use_cb)):]
        acc_s = tail[:HEADS_PER_STEP]
        cbc_s = tail[-1] if use_cb else None
        o_ref, lse_ref, vt_s = refs[-3 - len(tail):-len(tail)]
        qi = pl.program_id(2)

        heads = [slice(h * HEAD_DIM, (h + 1) * HEAD_DIM) for h in range(HEADS_PER_STEP)]

        @pl.when(qi == 0)
        def _():
            for cblk in range(nk):
                vt = v_ref[cblk * tk:(cblk + 1) * tk, :].astype(F32).T.astype(BF16)
                for h, hs in enumerate(heads):
                    vt_s[cblk, h, 0:HEAD_DIM, :] = vt[hs, :]
                    vt_s[cblk, h, HEAD_DIM:, :] = jnp.ones((ONES_ROWS, tk), BF16)
                if use_cb:
                    cbc_s[cblk] = _rows_to_cols([cb_ref[0, h, cblk] for h in range(HEADS_PER_STEP)])

        qt_all = (q_ref[...].astype(F32) * ATTN_SCALE).T.astype(BF16)
        qts = [qt_all[hs, :] for hs in heads]
        for a in acc_s:
            a[...] = jnp.zeros_like(a)

        def tile(kb, tab, k0, klen, q0, carry):
            ks = pl.multiple_of(kb * tk + k0, klen)
            sts, out = [], []
            for h, hs in enumerate(heads):
                st = jnp.dot(k_ref[pl.ds(ks, klen), hs], qts[h][:, q0:], preferred_element_type=F32)
                if tab is not None:
                    st = st + tab
                if use_cb:
                    st = st + cbc_s[kb, k0:k0 + klen, h * HEAD_DIM:h * HEAD_DIM + 1]
                sts.append(st)
            m_old = [carry[h][:, q0:] for h in range(HEADS_PER_STEP)]
            m_new = [jnp.maximum(m_old[h], jnp.max(sts[h], axis=0, keepdims=True)) for h in range(HEADS_PER_STEP)]
            for h in range(HEADS_PER_STEP):
                pt = jnp.exp(sts[h] - m_new[h]).astype(BF16)
                acc_s[h][:, q0:] = (jnp.exp(m_old[h] - m_new[h]) * acc_s[h][:, q0:]
                                    + jnp.dot(vt_s[kb, h, :, k0:k0 + klen], pt, preferred_element_type=F32))
                out.append(m_new[h] if q0 == 0 else jnp.concatenate([carry[h][:, :q0], m_new[h]], axis=1))
            return tuple(out)

        fin = lax.fori_loop(0, qi, lambda kb, c: tile(kb, tab_ref[qi - kb] if off_diag_bias else None, 0, tk, 0, c),
                            tuple(jnp.full((1, tq), NEG, F32) for _ in heads))
        half = tk // 2
        fin = tile(qi, tab_ref[0, 0:half, :], 0, half, 0, fin)
        fin = tile(qi, tab_ref[0, half:, half:], half, half, half, fin)
        outs = []
        for h in range(HEADS_PER_STEP):
            l = acc_s[h][HEAD_DIM:HEAD_DIM + 1, :]
            outs.append(acc_s[h][0:HEAD_DIM, :] / l)
            lse_ref[0, h, 0] = fin[h] + jnp.log(l)
        o_ref[...] = jnp.concatenate(outs, axis=0).T

    def seq_spec(off):
        return pl.BlockSpec((s, LANES), lambda b, j, i: (b, off + j))

    in_specs = [pl.BlockSpec((tq, LANES), lambda b, j, i: (b * nq + i, q_off + j)), seq_spec(k_off), seq_spec(v_off),
                pl.BlockSpec(table.shape, lambda b, j, i: (0, 0, 0))]
    args = [q_arr, k_arr, v_arr, table]
    if use_cb:
        in_specs.append(pl.BlockSpec((1, HEADS_PER_STEP, nk, 1, tk), lambda b, j, i: (b, j, 0, 0, 0)))
        args.append(colbias)
    n_heads = npairs * HEADS_PER_STEP
    return _pcall(
        body, side=side, name=name, grid=(nb, npairs, nq), in_specs=in_specs,
        out_specs=[pl.BlockSpec((tq, LANES), lambda b, j, i: (b * nq + i, j)),
                   pl.BlockSpec((1, HEADS_PER_STEP, 1, 1, tq), lambda b, j, i: (b, j, i, 0, 0))],
        out_shape=[jax.ShapeDtypeStruct((t, npairs * LANES), F32), jax.ShapeDtypeStruct((nb, n_heads, nq, 1, tq), F32)],
        scratch_shapes=[pltpu.VMEM((nk, HEADS_PER_STEP, HEAD_DIM + ONES_ROWS, tk), BF16)]
        + [pltpu.VMEM((HEAD_DIM + ONES_ROWS, tq), F32)] * HEADS_PER_STEP
        + ([pltpu.VMEM((nk, tk, LANES), F32)] if use_cb else []),
        compiler_params=_params(("arbitrary", "arbitrary", "arbitrary")),
    )(*args)


def attn_bwd(q_arr, q_off, k_arr, k_off, v_arr, v_off, o_arr, lse_arr, do_arr, table, colbias, nb, name, side=None,
             rope_tabs=None, off_diag_bias=True):
    t = q_arr.shape[0]
    s = t // nb
    tk, tq = table.shape[1:]
    assert tq == tk, "the diagonal handling below is written for square tiles"
    nq, nk = s // tq, s // tk
    npairs = WIDTH_A // LANES
    use_cb = colbias is not None

    def body(*refs):
        refs = list(refs)
        q_ref, k_ref, v_ref, o_ref, lse_ref, do_ref, tab_ref = refs[:7]
        pos = 7
        cb_ref = None
        if use_cb:
            cb_ref = refs[pos]
            pos += 1
        rope_refs = None
        if rope_tabs is not None:
            rope_refs = refs[pos:pos + 3]
            pos += 3
        dq_ref, dk_ref, dv_ref = refs[pos:pos + 3]
        pos += 3
        dcb_ref = drow_ref = None
        if use_cb:
            dcb_ref, drow_ref = refs[pos:pos + 2]
            pos += 2
        kt_s, dkt_s, dvt_s = refs[pos:pos + 3]
        dqt_s = refs[pos + 3:pos + 3 + HEADS_PER_STEP]
        dcb_s, cbc_s = refs[pos + 3 + HEADS_PER_STEP:pos + 5 + HEADS_PER_STEP] if use_cb else (None, None)

        heads = [slice(h * HEAD_DIM, (h + 1) * HEAD_DIM) for h in range(HEADS_PER_STEP)]
        for cblk in range(nk):
            kt_s[cblk] = k_ref[cblk * tk:(cblk + 1) * tk, :].astype(F32).T.astype(BF16)
        dkt_s[...] = jnp.zeros_like(dkt_s)
        dvt_s[...] = jnp.zeros_like(dvt_s)
        if use_cb:
            dcb_s[...] = jnp.zeros_like(dcb_s)
            for cblk in range(nk):
                cbc_s[cblk] = _rows_to_cols([cb_ref[0, h, cblk] for h in range(HEADS_PER_STEP)])
        ones = jnp.ones((8, HEAD_DIM), BF16)

        def q_loop(qi, carry):
            qs = pl.multiple_of(qi * tq, tq)
            q_all = (q_ref[pl.ds(qs, tq), :].astype(F32) * ATTN_SCALE)
            do_all = do_ref[pl.ds(qs, tq), :]
            qt_all = q_all.T.astype(BF16)
            dot_all = do_all.T.astype(BF16)
            qt, dot, lse, dsum = [], [], [], []
            for h, hs in enumerate(heads):
                qt.append(qt_all[hs, :])
                dot.append(dot_all[hs, :])
                lse.append(lse_ref[0, h, qi])
                prod = do_all[:, hs] * o_ref[pl.ds(qs, tq), hs]
                hi = prod.astype(BF16)
                lo = (prod - hi.astype(F32)).astype(BF16)
                dsum.append((lax.dot_general(ones, hi, NT_DIMS, preferred_element_type=F32)
                             + lax.dot_general(ones, lo, NT_DIMS, preferred_element_type=F32))[0:1, :])
            for a in dqt_s:
                a[...] = jnp.zeros_like(a)

            def tile(kb, tab, k0, klen, q0, drow):
                ks = pl.multiple_of(kb * tk + k0, klen)
                keys = slice(k0, k0 + klen)
                sts, dpts, out = [], [], []
                for h, hs in enumerate(heads):
                    st = jnp.dot(k_ref[pl.ds(ks, klen), hs], qt[h][:, q0:], preferred_element_type=F32)
                    if tab is not None:
                        st = st + tab
                    if use_cb:
                        st = st + cbc_s[kb, keys, h * HEAD_DIM:h * HEAD_DIM + 1]
                    sts.append(st)
                    dpts.append(jnp.dot(v_ref[pl.ds(ks, klen), hs], dot[h][:, q0:], preferred_element_type=F32))
                for h, hs in enumerate(heads):
                    pt = jnp.exp(sts[h] - lse[h][:, q0:])
                    dst = pt * (dpts[h] - dsum[h][:, q0:])
                    dst_b = dst.astype(BF16)
                    dvt_s[h, kb, :, keys] += lax.dot_general(dot[h][:, q0:], pt.astype(BF16), NT_DIMS,
                                                             preferred_element_type=F32)
                    dkt_s[h, kb, :, keys] += lax.dot_general(qt[h][:, q0:], dst_b, NT_DIMS, preferred_element_type=F32)
                    dqt_s[h][:, q0:] += jnp.dot(kt_s[kb, hs, keys], dst_b, preferred_element_type=F32)
                    if use_cb:
                        dcb_s[h, pl.ds(ks, klen), :] += jnp.sum(dst, axis=1, keepdims=True)
                        dr = drow[h][:, q0:] + jnp.sum(dst, axis=0, keepdims=True)
                        out.append(dr if q0 == 0 else jnp.concatenate([drow[h][:, :q0], dr], axis=1))
                    else:
                        out.append(drow[h])
                return tuple(out)

            drow = lax.fori_loop(0, qi, lambda kb, c: tile(kb, tab_ref[qi - kb] if off_diag_bias else None, 0, tk, 0, c),
                                 tuple(jnp.zeros((1, tq), F32) for _ in heads))
            half = tk // 2
            drow = tile(qi, tab_ref[0, 0:half, :], 0, half, 0, drow)
            drow = tile(qi, tab_ref[0, half:, half:], half, half, half, drow)
            dq = (jnp.concatenate([a[...] for a in dqt_s], axis=0) * ATTN_SCALE).T
            if rope_refs is not None:
                dq = _rotate(dq, *[coef[pl.ds(qs, tq), :] for coef in rope_refs], True)
            dq_ref[pl.ds(qs, tq), :] = dq.astype(dq_ref.dtype)
            if use_cb:
                for h in range(HEADS_PER_STEP):
                    drow_ref[0, h, qi] = drow[h]
            return carry

        lax.fori_loop(0, nq, q_loop, 0)
        for cblk in range(nk):
            rows = slice(cblk * tk, (cblk + 1) * tk)
            dk = jnp.concatenate([dkt_s[h, cblk] for h in range(HEADS_PER_STEP)], axis=0).T
            if rope_refs is not None:
                dk = _rotate(dk, *[coef[rows, :] for coef in rope_refs], True)
            dk_ref[rows, :] = dk.astype(dk_ref.dtype)
            dv_ref[rows, :] = jnp.concatenate([dvt_s[h, cblk] for h in range(HEADS_PER_STEP)], axis=0).T.astype(dv_ref.dtype)
            if use_cb:
                for h in range(HEADS_PER_STEP):
                    dcb_ref[0, h, cblk] = jnp.broadcast_to(dcb_s[h, rows, :], (tk, LANES)).T[0:1, :]

    def seq_spec(off):
        return pl.BlockSpec((s, LANES), lambda b, j: (b, off + j))

    row_spec = pl.BlockSpec((1, HEADS_PER_STEP, nq, 1, tq), lambda b, j: (b, j, 0, 0, 0))
    in_specs = [seq_spec(q_off), seq_spec(k_off), seq_spec(v_off), seq_spec(0), row_spec, seq_spec(0),
                pl.BlockSpec(table.shape, lambda b, j: (0, 0, 0))]
    args = [q_arr, k_arr, v_arr, o_arr, lse_arr, do_arr, table]
    width = npairs * LANES
    out_specs = [seq_spec(0)] * 3
    out_shape = [jax.ShapeDtypeStruct((t, width), BF16)] * 3
    scratch = [pltpu.VMEM((nk, LANES, tk), BF16), pltpu.VMEM((HEADS_PER_STEP, nk, HEAD_DIM, tk), F32),
               pltpu.VMEM((HEADS_PER_STEP, nk, HEAD_DIM, tk), F32)] + [pltpu.VMEM((HEAD_DIM, tq), F32)] * HEADS_PER_STEP
    if use_cb:
        cb_spec = pl.BlockSpec((1, HEADS_PER_STEP, nk, 1, tk), lambda b, j: (b, j, 0, 0, 0))
        in_specs.append(cb_spec)
        args.append(colbias)
    if rope_tabs is not None:
        in_specs += [pl.BlockSpec((s, LANES), lambda b, j: (b, 0))] * 3
        args += list(rope_tabs)
    if use_cb:
        out_specs += [cb_spec, row_spec]
        out_shape += [jax.ShapeDtypeStruct(colbias.shape, F32), jax.ShapeDtypeStruct(lse_arr.shape, F32)]
        scratch += [pltpu.VMEM((HEADS_PER_STEP, s, 1), F32), pltpu.VMEM((nk, tk, LANES), F32)]
    return _pcall(
        body, side=side, name=name, grid=(nb, npairs), in_specs=in_specs, out_specs=out_specs, out_shape=out_shape,
        scratch_shapes=scratch, compiler_params=_params(("arbitrary", "arbitrary")),
    )(*args)


def ada_fwd(c_all, w_ada, b_cols, name):
    def body(c_ref, w_ref, b_ref, o_ref):
        cv = c_ref[...]
        sc = (cv * _sigmoid(cv)).astype(BF16)
        o_ref[...] = jnp.dot(sc, w_ref[...].astype(BF16), preferred_element_type=F32) + b_ref[...]

    return _pcall(body, name=name, out_shape=jax.ShapeDtypeStruct((c_all.shape[0], w_ada.shape[1]), F32),
                  compiler_params=_params())(c_all, w_ada, b_cols)


def ada_bwd(c_all, dmod_cols, name):
    def body(c_ref, d_ref, o_ref):
        cv = c_ref[...]
        sc = (cv * _sigmoid(cv)).astype(BF16)
        o_ref[...] = lax.dot_general(sc, d_ref[...].astype(BF16), TN_DIMS, preferred_element_type=F32)

    return _pcall(body, name=name, out_shape=jax.ShapeDtypeStruct((c_all.shape[1], dmod_cols.shape[1]), F32),
                  compiler_params=_params())(c_all, dmod_cols)


def adamw(parts, group, w, m, v, name, tr=None):
    n = parts.shape[0]
    r, c = w.shape
    tr = r if tr is None else tr
    c1 = 1.0 - ADAM_B1 ** ADAM_STEP
    c2 = 1.0 - ADAM_B2 ** ADAM_STEP

    def body(p_ref, w_ref, m_ref, v_ref, g_ref, d_ref, nm_ref, nv_ref):
        g = p_ref[0, 0].astype(F32)
        for i in range(1, n):
            g = g + p_ref[i, 0].astype(F32)
        wv = w_ref[...]
        nm = ADAM_B1 * m_ref[...] + (1.0 - ADAM_B1) * g
        nv = ADAM_B2 * v_ref[...] + (1.0 - ADAM_B2) * (g * g)
        g_ref[...] = g
        nm_ref[...] = nm
        nv_ref[...] = nv
        d_ref[...] = -ADAM_LR * ((nm / c1) / (jnp.sqrt(nv / c2) + ADAM_EPS) + ADAM_WD * wv)

    spec = pl.BlockSpec((tr, c), lambda i: (i, 0))
    shape = jax.ShapeDtypeStruct((r, c), F32)
    return _pcall(
        body, name=name, grid=(r // tr,),
        in_specs=[pl.BlockSpec((n, 1, tr, c), lambda i: (0, group, i, 0)), spec, spec, spec],
        out_specs=[spec] * 4, out_shape=[shape] * 4, compiler_params=_params(("arbitrary",)),
    )(parts, w, m, v)


def direct_gather(arrs, name):
    n = len(arrs)
    hbm = pl.BlockSpec(memory_space=pl.ANY)

    def body(*refs):
        local, sends, recvs = _direct_copies(refs[:n], refs[n:2 * n], *refs[2 * n:], True)
        for cp in local + sends:
            cp.start()
        for cp in recvs:
            cp.wait_recv()
        for cp in sends:
            cp.wait_send()
        for cp in local:
            cp.wait()

    return _pcall(
        body, name=name, in_specs=[hbm] * n, out_specs=[hbm] * n,
        out_shape=[jax.ShapeDtypeStruct((N_DEV,) + a.shape, a.dtype) for a in arrs], scratch_shapes=_comm_scratch(n),
    )(*arrs)


def all_gather(arrs, name):
    n = len(arrs)
    hbm = pl.BlockSpec(memory_space=pl.ANY)

    def body(*refs):
        ins, outs = refs[:n], refs[n:2 * n]
        send_sems, recv_sems, local_sems = refs[2 * n:]
        x, y, c = _place()
        me, sibling = (x, y, c), (x, y, 1 - c)
        chips = [(1 - x, y), (x, 1 - y), (1 - x, 1 - y)]

        def copy(a, k, block, to, src=None):
            dst = outs[a].at[_slot(block)]
            return pltpu.make_async_remote_copy(
                src_ref=dst if src is None else src, dst_ref=dst, send_sem=send_sems.at[a * 7 + k],
                recv_sem=recv_sems.at[a * 7 + k], device_id=to, device_id_type=MESH)

        mine = [pltpu.make_async_copy(ins[a], outs[a].at[_slot(me)], local_sems.at[a]) for a in range(n)]
        for cp in mine:
            cp.start()
        first = []
        for a in range(n):
            first.append(copy(a, 0, me, sibling, src=ins[a]))
            first += [copy(a, 1 + j, me, (*chip, c), src=ins[a]) for j, chip in enumerate(chips)]
        for cp in first:
            cp.start()
        passed = []
        for a in range(n):
            for j, chip in enumerate(chips):
                copy(a, 1 + j, (*chip, c), me).wait_recv()
                cp = copy(a, 4 + j, (*chip, c), sibling)
                cp.start()
                passed.append(cp)
        for a in range(n):
            copy(a, 0, sibling, me).wait_recv()
            for j, chip in enumerate(chips):
                copy(a, 4 + j, (*chip, 1 - c), me).wait_recv()
        for cp in first + passed:
            cp.wait_send()
        for cp in mine:
            cp.wait()

    return _pcall(
        body, name=name, in_specs=[hbm] * n, out_specs=[hbm] * n,
        out_shape=[jax.ShapeDtypeStruct((N_DEV,) + a.shape, a.dtype) for a in arrs],
        scratch_shapes=[pltpu.SemaphoreType.DMA((7 * n,)), pltpu.SemaphoreType.DMA((7 * n,)),
                        pltpu.SemaphoreType.DMA((n,))],
        compiler_params=pltpu.CompilerParams(has_side_effects=True),
    )(*arrs)


def _t(w):
    return jnp.swapaxes(w, -1, -2)


def _rows_from_blocks(blocks, pad_to=None):
    full = blocks.reshape(-1, blocks.shape[2])
    if pad_to is not None and pad_to > full.shape[0]:
        full = jnp.pad(full, ((0, pad_to - full.shape[0]), (0, 0)))
    return full


def _rows_to_blocks(full, nrows):
    return full[:nrows].reshape(N_DEV, nrows // N_DEV, full.shape[1])


SMALL_ORDER = ("g_pre_ff1", "g_post_ff1", "g_pre_mix", "g_post_mix", "g_out_a", "g_out_b", "g_pre_ff2", "g_post_ff2",
               "b_forget")


def _pack_small(vals):
    rows = []
    for name in SMALL_ORDER:
        v = vals[name].reshape(1, -1)
        if v.shape[1] % LANES:
            v = jnp.pad(v, ((0, 0), (0, LANES - v.shape[1] % LANES)))
        rows.append(v)
    return jnp.concatenate(rows, axis=1)


def _unpack_small(row, sizes):
    out, pos = {}, 0
    for name in SMALL_ORDER:
        n = sizes[name]
        out[name] = row[:, pos:pos + n]
        pos += -(-n // LANES) * LANES
    return out


def _ffn_forward(x, mod, g_pre, g_post, wg, wu, wd, i0, nb, tag, target=None, side=None):
    h = prenorm_fwd(x, g_pre, mod, i0, i0 + 1, nb, f"{tag}_prenorm")
    res, side_out = ffn_up(h, wg, wu, f"{tag}_up", side=side), None
    if side is not None:
        res, side_out = res
    gate, up, act = res
    res = postnorm_fwd(x, [(act, wd)], g_post, mod, i0 + 2, 0.5, nb, f"{tag}_down_postnorm", target=target)
    out, y0 = (res[0] if target is None else tuple(res[:2])), res[-1]
    return out, (x, h, gate, up, act, y0), side_out


def _ffn_backward(dxo, saved, mod, g_pre, g_post, wg, wu, wd, i0, nb, tag, side=None, chain=False):
    x, h, gate, up, act, y0 = saved
    dy0, dg_post, dgate_mod = postnorm_bwd(dxo, y0, g_post, mod, i0 + 2, 0.5, nb, f"{tag}_postnorm_bwd")
    dwd = mm_tn(act, dy0, BF16, f"{tag}_dwd")
    res, side_out = ffn_down_bwd(dy0, wd, gate, up, f"{tag}_down_bwd", side=side), None
    if side is not None:
        res, side_out = res
    dgate, dup = res
    dh_pairs = [(dgate, wg), (dup, wu)]
    if chain:
        dwg, (dwd,) = mm_tn(dgate, h, BF16, f"{tag}_dwg", side=([_rows_to_blocks(dwd, D_FF)[:, None]], False))
        dwu, (dwg,) = mm_tn(dup, h, BF16, f"{tag}_dwu", side=([_rows_to_blocks(dwg, D_FF)[:, None]], False))
        first = max(nb // 2, 1)
        (dx, dg_pre, dsc, dsh), (dwu,) = prenorm_bwd(
            dh_pairs, x, g_pre, mod, i0 + 1, dxo, nb, f"{tag}_dh_prenorm_bwd", ts=DH_ROWS, seqs=(0, first),
            side=([_rows_to_blocks(dwu, D_FF)[:, None]], False))
        if first < nb:
            dx, dg_pre, dsc2, dsh2 = prenorm_bwd(dh_pairs, x, g_pre, mod, i0 + 1, dxo, nb, f"{tag}_dh_prenorm_bwd_rest",
                                                 ts=DH_ROWS, seqs=(first, nb - first), carry=(dx, dg_pre))
            dsc, dsh = jnp.concatenate([dsc, dsc2]), jnp.concatenate([dsh, dsh2])
    else:
        dwg = mm_tn(dgate, h, BF16, f"{tag}_dwg")
        dwu = mm_tn(dup, h, BF16, f"{tag}_dwu")
        dx, dg_pre, dsc, dsh = prenorm_bwd(dh_pairs, x, g_pre, mod, i0 + 1, dxo, nb, f"{tag}_dh_prenorm_bwd", ts=DH_ROWS)
    return dx, dict(g_pre=dg_pre, g_post=dg_post, wg=dwg, wu=dwu, wd=dwd, mod=(dsh, dsc, dgate_mod)), side_out


def kernel(x, c, positions, w_ada, b_ada, g_pre_ff1, g_post_ff1, w_ff1_gate, w_ff1_up, w_ff1_down, g_pre_mix, g_post_mix, w_in, b_forget, g_out_a, g_out_b, w_out, g_pre_ff2, g_post_ff2, w_ff2_gate, w_ff2_up, w_ff2_down, loss_target, m_w_ada, m_b_ada, m_g_pre_ff1, m_g_post_ff1, m_w_ff1_gate, m_w_ff1_up, m_w_ff1_down, m_g_pre_mix, m_g_post_mix, m_w_in, m_b_forget, m_g_out_a, m_g_out_b, m_w_out, m_g_pre_ff2, m_g_post_ff2, m_w_ff2_gate, m_w_ff2_up, m_w_ff2_down, v_w_ada, v_b_ada, v_g_pre_ff1, v_g_post_ff1, v_w_ff1_gate, v_w_ff1_up, v_w_ff1_down, v_g_pre_mix, v_g_post_mix, v_w_in, v_b_forget, v_g_out_a, v_g_out_b, v_w_out, v_g_pre_ff2, v_g_post_ff2, v_w_ff2_gate, v_w_ff2_up, v_w_ff2_down):
    weights = dict(w_ada=w_ada, b_ada=b_ada, g_pre_ff1=g_pre_ff1, g_post_ff1=g_post_ff1, w_ff1_gate=w_ff1_gate,
                   w_ff1_up=w_ff1_up, w_ff1_down=w_ff1_down, g_pre_mix=g_pre_mix, g_post_mix=g_post_mix, w_in=w_in,
                   b_forget=b_forget, g_out_a=g_out_a, g_out_b=g_out_b, w_out=w_out, g_pre_ff2=g_pre_ff2,
                   g_post_ff2=g_post_ff2, w_ff2_gate=w_ff2_gate, w_ff2_up=w_ff2_up, w_ff2_down=w_ff2_down)
    mom_m = dict(w_ada=m_w_ada, b_ada=m_b_ada, g_pre_ff1=m_g_pre_ff1, g_post_ff1=m_g_post_ff1, w_ff1_gate=m_w_ff1_gate,
                 w_ff1_up=m_w_ff1_up, w_ff1_down=m_w_ff1_down, g_pre_mix=m_g_pre_mix, g_post_mix=m_g_post_mix,
                 w_in=m_w_in, b_forget=m_b_forget, g_out_a=m_g_out_a, g_out_b=m_g_out_b, w_out=m_w_out,
                 g_pre_ff2=m_g_pre_ff2, g_post_ff2=m_g_post_ff2, w_ff2_gate=m_w_ff2_gate, w_ff2_up=m_w_ff2_up,
                 w_ff2_down=m_w_ff2_down)
    mom_v = dict(w_ada=v_w_ada, b_ada=v_b_ada, g_pre_ff1=v_g_pre_ff1, g_post_ff1=v_g_post_ff1, w_ff1_gate=v_w_ff1_gate,
                 w_ff1_up=v_w_ff1_up, w_ff1_down=v_w_ff1_down, g_pre_mix=v_g_pre_mix, g_post_mix=v_g_post_mix,
                 w_in=v_w_in, b_forget=v_b_forget, g_out_a=v_g_out_a, g_out_b=v_g_out_b, w_out=v_w_out,
                 g_pre_ff2=v_g_pre_ff2, g_post_ff2=v_g_post_ff2, w_ff2_gate=v_w_ff2_gate, w_ff2_up=v_w_ff2_up,
                 w_ff2_down=v_w_ff2_down)
    order = list(weights)

    nb, s, d = x.shape
    t = nb * s
    me = _slot(_place())
    nbg = nb * N_DEV
    ada_cols = w_ada.shape[2]

    bf = lambda w: w[0].astype(BF16)
    bft = lambda w: _t(w)[0].astype(BF16)
    c_all, ff1_all = all_gather([c, jnp.stack([bft(w_ff1_gate), bft(w_ff1_up), bf(w_ff1_down)])], "gather_ff1")
    c_all = c_all.reshape(nbg, d)
    wg1, wu1, wd1 = (_rows_from_blocks(ff1_all[:, i], D_FF_PAD) for i in range(3))

    b_cols = lax.dynamic_slice(b_ada, (0, me * ada_cols), (1, ada_cols))
    mod_cols = ada_fwd(c_all, w_ada[0], b_cols, "ada_fwd")
    (mod_all,) = direct_gather([mod_cols], "gather_mod")
    mod = lax.dynamic_slice(mod_all, (0, me * nb, 0), (N_DEV, nb, ada_cols))
    mod = mod.transpose(1, 0, 2).reshape(nb, N_MOD, d)

    xf = x.reshape(t, d)
    target = loss_target.reshape(t, d)

    x1, saved1, (w_in_all, w_out_all) = _ffn_forward(xf, mod, g_pre_ff1, g_post_ff1, wg1, wu1, wd1, 0, nb, "ff1",
                                                     side=([bft(w_in), bf(w_out)], True))
    w_in_t = _rows_from_blocks(w_in_all)
    n_qkv = 3 * (WIDTH_A + WIDTH_B)
    w_qkv_t = w_in_t[:n_qkv]
    w_f_t = jnp.pad(w_in_t[n_qkv:], ((0, LANES - N_HEADS_B), (0, 0)))
    w_o = _rows_from_blocks(w_out_all)
    w_o_a, w_o_b = w_o[:WIDTH_A], w_o[WIDTH_A:]

    h2 = prenorm_fwd(x1, g_pre_mix, mod, 3, 4, nb, "mix_prenorm")
    tables = rope_tables(positions)
    proj = mm_rows([(h2, w_qkv_t)], True, BF16, "mix_proj", rope=(tables, 2 * WIDTH_A))
    f_logit = mm_rows([(h2, w_f_t)], True, F32, "mix_forget")
    tab_a = dilated_table(s, ATTN_TQ, ATTN_TK)
    tab_b = causal_table(s, ATTN_TQ, ATTN_TK)
    ft = f_logit[:, :N_HEADS_B].reshape(nb, s, N_HEADS_B).transpose(0, 2, 1)
    bf_col = b_forget.reshape(N_HEADS_B, 1)
    colbias = fox_gate_fwd(ft, bf_col, "fox_gate").reshape(nb, N_HEADS_B, s // ATTN_TK, 1, ATTN_TK)
    pa = WIDTH_A // LANES
    (o_a, lse_a), (ff2_all,) = attn_fwd(
        proj, 0, proj, pa, proj, 2 * pa, tab_a, None, nb, "attn_a",
        side=([jnp.stack([bft(w_ff2_gate), bft(w_ff2_up), bf(w_ff2_down)])], True))
    wg2, wu2, wd2 = (_rows_from_blocks(ff2_all[:, i], D_FF_PAD) for i in range(3))
    o_b, lse_b = attn_fwd(proj, 3 * pa, proj, 4 * pa, proj, 5 * pa, tab_b, colbias, nb, "attn_b", off_diag_bias=False)
    m_a = prenorm_fwd(o_a, g_out_a, None, None, None, nb, "out_norm_a")
    m_b = prenorm_fwd(o_b, g_out_b, None, None, None, nb, "out_norm_b")
    x2, y0m = postnorm_fwd(x1, [(m_a, w_o_a), (m_b, w_o_b)], g_post_mix, mod, 5, 1.0, nb, "mix_out_postnorm")

    (dx3, loss_part), saved2, _ = _ffn_forward(x2, mod, g_pre_ff2, g_post_ff2, wg2, wu2, wd2, 6, nb, "ff2", target=target)
    loss = lax.psum(loss_part[0, 0], ("x", "y", "c"))

    dx2, gr2, _ = _ffn_backward(dx3, saved2, mod, g_pre_ff2, g_post_ff2, wg2, wu2, wd2, 6, nb, "ff2")
    ff2_blocks = [jnp.stack([_rows_to_blocks(gr2[k], D_FF) for k in ("wg", "wu", "wd")], axis=1)]

    dy0m, dg_post_mix, dgate_mix = postnorm_bwd(dx2, y0m, g_post_mix, mod, 5, 1.0, nb, "mix_postnorm_bwd")
    dw_o_a = mm_tn(m_a, dy0m, BF16, "mix_dwo_a")
    dw_o_b = mm_tn(m_b, dy0m, BF16, "mix_dwo_b")
    do_a, dg_out_a = prenorm_bwd([(dy0m, w_o_a.T)], o_a, g_out_a, None, None, None, nb, "out_norm_a_bwd")
    do_b, dg_out_b = prenorm_bwd([(dy0m, w_o_b.T)], o_b, g_out_b, None, None, None, nb, "out_norm_b_bwd")
    (dq_a, dk_a, dv_a), (g_ff2,) = attn_bwd(proj, 0, proj, pa, proj, 2 * pa, o_a, lse_a, do_a, tab_a, None, nb,
                                            "attn_a_bwd", side=(ff2_blocks, False), rope_tabs=tables)
    dq_b, dk_b, dv_b, dcb, drow = attn_bwd(proj, 3 * pa, proj, 4 * pa, proj, 5 * pa, o_b, lse_b, do_b, tab_b, colbias, nb,
                                           "attn_b_bwd", off_diag_bias=False)
    dz_t, db_forget = fox_gate_bwd(dcb.reshape(nb, N_HEADS_B, s), drow.reshape(nb, N_HEADS_B, s), ft, bf_col,
                                   "fox_gate_bwd")
    dz = jnp.pad(dz_t.transpose(0, 2, 1).reshape(t, N_HEADS_B), ((0, 0), (0, LANES - N_HEADS_B))).astype(BF16)
    pieces = [dq_a, dk_a, dv_a, dq_b, dk_b, dv_b]
    w_pieces = [w_qkv_t[i * WIDTH_A:(i + 1) * WIDTH_A] for i in range(6)]
    dh2_pairs = list(zip(pieces, w_pieces)) + [(dz, w_f_t)]
    dw_in_t = jnp.concatenate([mm_tn(p, h2, BF16, f"mix_dwin_{i}") for i, p in enumerate(pieces)]
                              + [mm_tn(dz, h2, BF16, "mix_dwin_f")[:N_HEADS_B]], axis=0)
    dx1, dg_pre_mix, dsc_mix, dsh_mix = prenorm_bwd(dh2_pairs, x1, g_pre_mix, mod, 4, dx2, nb, "mix_dh_prenorm_bwd",
                                                    ts=DH_ROWS)

    g_in = _rows_to_blocks(dw_in_t, dw_in_t.shape[0])[:, None]
    g_out = _rows_to_blocks(jnp.concatenate([dw_o_a, dw_o_b], axis=0), d)[:, None]
    dx0, gr1, (g_in, g_out) = _ffn_backward(dx1, saved1, mod, g_pre_ff1, g_post_ff1, wg1, wu1, wd1, 0, nb, "ff1",
                                            side=([g_in, g_out], False), chain=True)
    grad_x = dx0.reshape(nb, s, d)

    dmod =jnp.concatenate(list(gr1["mod"]) + [dsh_mix, dsc_mix, dgate_mix] + list(gr2["mod"]), axis=1)
    small = _pack_small(dict(g_pre_ff1=gr1["g_pre"], g_post_ff1=gr1["g_post"], g_pre_mix=dg_pre_mix,
                             g_post_mix=dg_post_mix, g_out_a=dg_out_a, g_out_b=dg_out_b, g_pre_ff2=gr2["g_pre"],
                             g_post_ff2=gr2["g_post"], b_forget=db_forget))
    dmod_all, small_all = direct_gather([dmod.reshape(nb, N_MOD * d), small], "gather_small_grads")
    dmod_all = dmod_all.reshape(nbg, N_MOD * d)

    res = {}
    def adamw_t(parts, group, n):
        return tuple(_t(r) for r in adamw(parts, group, _t(weights[n])[0], _t(mom_m[n])[0], _t(mom_v[n])[0], f"adamw_{n}"))

    res["w_ff1_gate"] = adamw_t(gr1["wg"], 0, "w_ff1_gate")
    res["w_ff1_up"] = adamw_t(gr1["wu"], 0, "w_ff1_up")
    res["w_ff2_gate"] = adamw_t(g_ff2, 0, "w_ff2_gate")
    res["w_ff2_up"] = adamw_t(g_ff2, 1, "w_ff2_up")
    res["w_ff1_down"] = adamw(gr1["wd"], 0, w_ff1_down[0], m_w_ff1_down[0], v_w_ff1_down[0], "adamw_ff1_down")
    res["w_ff2_down"] = adamw(g_ff2, 2, w_ff2_down[0], m_w_ff2_down[0], v_w_ff2_down[0], "adamw_ff2_down")
    res["w_in"] = adamw_t(g_in, 0, "w_in")
    res["w_out"] = adamw(g_out, 0, w_out[0], m_w_out[0], v_w_out[0], "adamw_out")
    dmod_cols = lax.dynamic_slice(dmod_all, (0, me * ada_cols), (nbg, ada_cols))
    dw_ada = ada_bwd(c_all, dmod_cols, "ada_bwd")
    res["w_ada"] = adamw(dw_ada[None, None], 0, w_ada[0], m_w_ada[0], v_w_ada[0], "adamw_ada", tr=256)
    res["b_ada"] = adamw(dmod_all[:, None, None], 0, b_ada, m_b_ada, v_b_ada, "adamw_b_ada")
    sizes = {n: weights[n].shape[1] for n in SMALL_ORDER}
    small_res = adamw(small_all[:, None], 0, _pack_small(weights), _pack_small(mom_m), _pack_small(mom_v), "adamw_small")
    small_res = [_unpack_small(r, sizes) for r in small_res]
    for n in SMALL_ORDER:
        res[n] = tuple(r[n] for r in small_res)

    outs = [loss, grad_x]
    for kind in range(4):
        for n in order:
            a = res[n][kind]
            outs.append(a.reshape(weights[n].shape))
    return tuple(outs)
```

```python
import functools

import jax
import jax.numpy as jnp
from jax import lax
from jax.experimental import pallas as pl
from jax.experimental.pallas import tpu as pltpu

F32 = jnp.float32
BF16 = jnp.bfloat16

D_MODEL = 1024
HEAD_DIM = 64
N_HEADS_A = 8
N_HEADS_B = 8
WIDTH_A = N_HEADS_A * HEAD_DIM
WIDTH_B = N_HEADS_B * HEAD_DIM
DILATED_PATTERNS = ((128, 1), (512, 4), (2048, 16))
ROT_DIM = HEAD_DIM // 4
ROPE_THETA = 500000.0
D_FF = 2752
D_FF_PAD = 2816
N_MOD = 9
EPS = 1e-6
ATTN_SCALE = HEAD_DIM ** -0.5
NEG = -1e30
N_DEV = 8
LANES = 128
HEADS_PER_STEP = LANES // HEAD_DIM

ADAM_LR = 0.001
ADAM_B1 = 0.9
ADAM_B2 = 0.999
ADAM_EPS = 1e-08
ADAM_WD = 0.01
ADAM_STEP = 10

VMEM_LIMIT = 56 * 1024 * 1024
MESH = pl.DeviceIdType.MESH

NT_DIMS = (((1,), (1,)), ((), ()))
TN_DIMS = (((0,), (0,)), ((), ()))
NN_DIMS = (((1,), (0,)), ((), ()))


def _place():
    return lax.axis_index("x"), lax.axis_index("y"), lax.axis_index("c")


def _slot(p):
    return 4 * p[0] + 2 * p[1] + p[2]


def _direct_copies(ins, outs, send_sems, recv_sems, local_sems, gather):
    x, y, c = _place()
    me = (x, y, c)
    flip = lambda v, bit: 1 - v if bit else v
    peers = [(flip(x, k & 4), flip(y, k & 2), flip(c, k & 1)) for k in range(1, N_DEV)]
    local, sends, recvs = [], [], []
    for a in range(len(ins)):
        mine = ins[a] if gather else ins[a].at[_slot(me)]
        local.append(pltpu.make_async_copy(mine, outs[a].at[_slot(me)], local_sems.at[a]))
        for k, peer in enumerate(peers):
            sems = dict(send_sem=send_sems.at[a * 7 + k], recv_sem=recv_sems.at[a * 7 + k], device_id=peer,
                        device_id_type=MESH)
            sends.append(pltpu.make_async_remote_copy(
                src_ref=ins[a] if gather else ins[a].at[_slot(peer)], dst_ref=outs[a].at[_slot(me)], **sems))
            recvs.append(pltpu.make_async_remote_copy(src_ref=mine, dst_ref=outs[a].at[_slot(peer)], **sems))
    return local, sends, recvs


def _comm_scratch(n):
    return [pltpu.SemaphoreType.DMA((7 * n,)), pltpu.SemaphoreType.DMA((7 * n,)), pltpu.SemaphoreType.DMA((n,))]


def _pcall(body, side=None, **kw):
    if side is None:
        return pl.pallas_call(body, **kw)
    arrs, gather = side
    n = len(arrs)
    grid = kw["grid"]
    in_specs = list(kw["in_specs"])
    single = not isinstance(kw["out_specs"], (list, tuple))
    out_specs = [kw["out_specs"]] if single else list(kw["out_specs"])
    out_shape = [kw["out_shape"]] if single else list(kw["out_shape"])
    scratch = list(kw.get("scratch_shapes", []))
    n_in, n_out, n_scr = len(in_specs), len(out_specs), len(scratch)
    hbm = pl.BlockSpec(memory_space=pl.ANY)

    def hosted(*refs):
        pos = [0]

        def take(k):
            pos[0] += k
            return refs[pos[0] - k:pos[0]]

        ins, s_ins, outs, s_outs, scr, sems = take(n_in), take(n), take(n_out), take(n), take(n_scr), take(3)
        ids = [pl.program_id(i) for i in range(len(grid))]
        first = functools.reduce(jnp.logical_and, [i == 0 for i in ids])
        last = functools.reduce(jnp.logical_and, [i == g - 1 for i, g in zip(ids, grid)])

        @pl.when(first)
        def _():
            local, sends, _ = _direct_copies(s_ins, s_outs, *sems, gather)
            for cp in local + sends:
                cp.start()

        body(*ins, *outs, *scr)

        @pl.when(last)
        def _():
            local, sends, recvs = _direct_copies(s_ins, s_outs, *sems, gather)
            for cp in recvs:
                cp.wait_recv()
            for cp in sends:
                cp.wait_send()
            for cp in local:
                cp.wait()

    kw.update(in_specs=in_specs + [hbm] * n, out_specs=out_specs + [hbm] * n,
              out_shape=out_shape + [jax.ShapeDtypeStruct(((N_DEV,) + a.shape) if gather else a.shape, a.dtype)
                                     for a in arrs],
              scratch_shapes=scratch + _comm_scratch(n))
    call = pl.pallas_call(hosted, **kw)

    def run(*args):
        res = call(*args, *arrs)
        main = res[0] if single else list(res[:n_out])
        return main, list(res[n_out:])

    return run


def _params(sem=None, **kw):
    if sem is not None:
        kw["dimension_semantics"] = sem
    return pltpu.CompilerParams(vmem_limit_bytes=VMEM_LIMIT, **kw)


def _rotate(xv, c, sp, sm, transpose):
    width = xv.shape[1]
    half = ROT_DIM // 2
    if transpose:
        return xv * c + pltpu.roll(xv * sp, width - half, 1) + pltpu.roll(xv * sm, half, 1)
    return xv * c + pltpu.roll(xv, half, 1) * sp + pltpu.roll(xv, width - half, 1) * sm


def mm_rows(pairs, trans_b, out_dtype, name, tm=512, side=None, rope=None):
    n = len(pairs)
    m = pairs[0][0].shape[0]
    n_out = pairs[0][1].shape[0 if trans_b else 1]
    dims = NT_DIMS if trans_b else NN_DIMS

    def body(*refs):
        o_ref = refs[-1]
        acc = None
        for a_ref, b_ref in zip(refs[:n], refs[n:2 * n]):
            d = lax.dot_general(a_ref[...], b_ref[...], dims, preferred_element_type=F32)
            acc = d if acc is None else acc + d
        if rope is None:
            o_ref[...] = acc.astype(o_ref.dtype)
        else:
            width = rope[1]
            c, sp, sm = (jnp.concatenate([r[...]] * (width // LANES), axis=1) for r in refs[2 * n:2 * n + 3])
            o_ref[:, :width] = _rotate(acc[:, :width], c, sp, sm, False).astype(o_ref.dtype)
            o_ref[:, width:] = acc[:, width:].astype(o_ref.dtype)

    in_specs = [pl.BlockSpec((tm, a.shape[1]), lambda i: (i, 0)) for a, _ in pairs]
    in_specs += [pl.BlockSpec(b.shape, lambda i: (0, 0)) for _, b in pairs]
    args = [a for a, _ in pairs] + [b for _, b in pairs]
    if rope is not None:
        in_specs += [pl.BlockSpec((tm, LANES), lambda i: (i, 0))] * 3
        args += list(rope[0])
    return _pcall(
        body, side=side, name=name, grid=(m // tm,), in_specs=in_specs,
        out_specs=pl.BlockSpec((tm, n_out), lambda i: (i, 0)),
        out_shape=jax.ShapeDtypeStruct((m, n_out), out_dtype),
        compiler_params=_params(("arbitrary",)),
    )(*args)


DH_ROWS = 256
TN_TOKENS = 2048
TN_OUT_ELEMS = 2 * 1024 * 1024


def mm_tn(a, b, out_dtype, name, side=None):
    t, ka = a.shape
    n_out = b.shape[1]
    tk = min(TN_TOKENS, t)
    tka = ka // 2 if ka * n_out > TN_OUT_ELEMS else ka
    tn = n_out
    steps = t // tk

    def body(a_ref, b_ref, o_ref, acc_ref):
        k = pl.program_id(2)
        d = lax.dot_general(a_ref[...], b_ref[...], TN_DIMS, preferred_element_type=F32)

        @pl.when(k == 0)
        def _():
            acc_ref[...] = d

        @pl.when(k > 0)
        def _():
            acc_ref[...] += d

        @pl.when(k == steps - 1)
        def _():
            o_ref[...] = acc_ref[...].astype(o_ref.dtype)

    return _pcall(
        body, side=side, name=name, grid=(ka // tka, n_out // tn, steps),
        in_specs=[pl.BlockSpec((tk, tka), lambda i, j, k: (k, i)), pl.BlockSpec((tk, tn), lambda i, j, k: (k, j))],
        out_specs=pl.BlockSpec((tka, tn), lambda i, j, k: (i, j)),
        out_shape=jax.ShapeDtypeStruct((ka, n_out), out_dtype),
        scratch_shapes=[pltpu.VMEM((tka, tn), F32)],
        compiler_params=_params(("arbitrary", "arbitrary", "arbitrary")),
    )(a, b)


def _col_chunks(width, chunk=512):
    return [slice(c, min(c + chunk, width)) for c in range(0, width, chunk)]


def _sigmoid(x):
    return 1.0 / (1.0 + jnp.exp(-x))


def ffn_up(h, wgt, wut, name, tm=256, tn=D_FF_PAD, side=None):
    t, d = h.shape
    fp = wgt.shape[0]

    def body(h_ref, wg_ref, wu_ref, g_ref, u_ref, a_ref):
        hv = h_ref[...]

        def finish(cols, g, u):
            g_ref[:, cols] = g.astype(BF16)
            u_ref[:, cols] = u.astype(BF16)
            a_ref[:, cols] = (g * _sigmoid(g) * u).astype(BF16)

        pending = None
        for cols in _col_chunks(tn):
            g = lax.dot_general(hv, wg_ref[cols, :], NT_DIMS, preferred_element_type=F32)
            u = lax.dot_general(hv, wu_ref[cols, :], NT_DIMS, preferred_element_type=F32)
            if pending is not None:
                finish(*pending)
            pending = (cols, g, u)
        finish(*pending)

    w_spec = pl.BlockSpec((tn, d), lambda j, i: (j, 0))
    o_spec = pl.BlockSpec((tm, tn), lambda j, i: (i, j))
    o_shape = jax.ShapeDtypeStruct((t, fp), BF16)
    return _pcall(
        body, side=side, name=name, grid=(fp // tn, t // tm),
        in_specs=[pl.BlockSpec((tm, d), lambda j, i: (i, 0)), w_spec, w_spec],
        out_specs=[o_spec, o_spec, o_spec], out_shape=[o_shape, o_shape, o_shape],
        compiler_params=_params(("arbitrary", "arbitrary")),
    )(h, wgt, wut)


def ffn_down_bwd(dy0, wd, gate, up, name, tm=256, tn=D_FF_PAD, side=None):
    t, d = dy0.shape
    fp = wd.shape[0]

    def body(dy_ref, wd_ref, g_ref, u_ref, dg_ref, du_ref):
        dyv = dy_ref[...]

        def finish(cols, dact):
            g = g_ref[:, cols].astype(F32)
            u = u_ref[:, cols].astype(F32)
            sg = _sigmoid(g)
            silu = g * sg
            du_ref[:, cols] = (dact * silu).astype(BF16)
            dg_ref[:, cols] = ((dact * u) * (sg + silu * (1.0 - sg))).astype(BF16)

        pending = None
        for cols in _col_chunks(tn):
            dact = lax.dot_general(dyv, wd_ref[cols, :], NT_DIMS, preferred_element_type=F32)
            if pending is not None:
                finish(*pending)
            pending = (cols, dact)
        finish(*pending)

    t_spec = pl.BlockSpec((tm, tn), lambda j, i: (i, j))
    o_shape = jax.ShapeDtypeStruct((t, fp), BF16)
    return _pcall(
        body, side=side, name=name, grid=(fp // tn, t // tm),
        in_specs=[pl.BlockSpec((tm, d), lambda j, i: (i, 0)), pl.BlockSpec((tn, d), lambda j, i: (j, 0)), t_spec, t_spec],
        out_specs=[t_spec, t_spec], out_shape=[o_shape, o_shape],
        compiler_params=_params(("arbitrary", "arbitrary")),
    )(dy0, wd, gate, up)


def _row_specs(dx, ts, ns):
    return pl.BlockSpec((ts, dx), lambda b, s: (b * ns + s, 0))


def _mod_spec():
    return pl.BlockSpec((1, N_MOD, D_MODEL), lambda b, s: (b, 0, 0))


def _vec_spec(dx):
    return pl.BlockSpec((1, dx), lambda b, s: (0, 0))


def prenorm_fwd(x, g, mod, i_shift, i_scale, nb, name, ts=1024):
    t, dx = x.shape
    ts = min(ts, t // nb)
    ns = t // nb // ts

    def body(*refs):
        if mod is None:
            x_ref, g_ref, h_ref = refs
        else:
            x_ref, g_ref, mod_ref, h_ref = refs
        xv = x_ref[...]
        r = lax.rsqrt(jnp.mean(xv * xv, axis=-1, keepdims=True) + EPS)
        h = xv * r * g_ref[...]
        if mod is not None:
            h = h * (1.0 + mod_ref[0, i_scale:i_scale + 1, :]) + mod_ref[0, i_shift:i_shift + 1, :]
        h_ref[...] = h.astype(BF16)

    in_specs = [_row_specs(dx, ts, ns), _vec_spec(dx)]
    args = [x, g]
    if mod is not None:
        in_specs.append(_mod_spec())
        args.append(mod)
    return _pcall(
        body, name=name, grid=(nb, ns), in_specs=in_specs, out_specs=_row_specs(dx, ts, ns),
        out_shape=jax.ShapeDtypeStruct((t, dx), BF16), compiler_params=_params(("arbitrary", "arbitrary")),
    )(*args)


def prenorm_bwd(dh, x, g, mod, i_scale, dres, nb, name, ts=512, side=None):
    t, dx = x.shape
    ts = min(ts, t // nb)
    ns = t // nb // ts
    has_mod = mod is not None
    has_res = dres is not None
    pairs = dh if isinstance(dh, list) else None
    n_mm = 0 if pairs is None else len(pairs)

    def body(*refs):
        refs = list(refs)
        if pairs is None:
            dhv = refs[0][...].astype(F32)
            refs = refs[1:]
        else:
            dhv = None
            for a_ref, b_ref in zip(refs[:n_mm], refs[n_mm:2 * n_mm]):
                d = jnp.dot(a_ref[...], b_ref[...], preferred_element_type=F32)
                dhv = d if dhv is None else dhv + d
            refs = refs[2 * n_mm:]
        x_ref, g_ref = refs[:2]
        pos = 2
        mod_ref = dres_ref = None
        if has_mod:
            mod_ref = refs[pos]
            pos += 1
        if has_res:
            dres_ref = refs[pos]
            pos += 1
        dx_ref, dg_ref = refs[pos], refs[pos + 1]
        b, s = pl.program_id(0), pl.program_id(1)
        xv = x_ref[...]
        gv = g_ref[...]
        r = lax.rsqrt(jnp.mean(xv * xv, axis=-1, keepdims=True) + EPS)
        xhat = xv * r
        dn = dhv
        if has_mod:
            dsc_ref, dsh_ref = refs[pos + 2], refs[pos + 3]
            dn = dhv * (1.0 + mod_ref[0, i_scale:i_scale + 1, :])
            dsc = jnp.sum(dhv * xhat * gv, axis=0, keepdims=True)[None]
            dsh = jnp.sum(dhv, axis=0, keepdims=True)[None]

            @pl.when(s == 0)
            def _():
                dsc_ref[...] = dsc
                dsh_ref[...] = dsh

            @pl.when(s > 0)
            def _():
                dsc_ref[...] += dsc
                dsh_ref[...] += dsh

        dg = jnp.sum(dn * xhat, axis=0, keepdims=True)
        first = jnp.logical_and(b == 0, s == 0)

        @pl.when(first)
        def _():
            dg_ref[...] = dg

        @pl.when(jnp.logical_not(first))
        def _():
            dg_ref[...] += dg

        dxhat = dn * gv
        dxv = r * (dxhat - xhat * jnp.mean(dxhat * xhat, axis=-1, keepdims=True))
        if has_res:
            dxv = dxv + dres_ref[...]
        dx_ref[...] = dxv

    row = _row_specs(dx, ts, ns)
    if pairs is None:
        in_specs, args = [row], [dh]
    else:
        in_specs = [_row_specs(a.shape[1], ts, ns) for a, _ in pairs]
        in_specs += [pl.BlockSpec(b.shape, lambda b_, s_: (0, 0)) for _, b in pairs]
        args = [a for a, _ in pairs] + [b for _, b in pairs]
    in_specs += [row, _vec_spec(dx)]
    args += [x, g]
    if has_mod:
        in_specs.append(_mod_spec())
        args.append(mod)
    if has_res:
        in_specs.append(row)
        args.append(dres)
    out_specs = [row, _vec_spec(dx)]
    out_shape = [jax.ShapeDtypeStruct((t, dx), F32), jax.ShapeDtypeStruct((1, dx), F32)]
    if has_mod:
        bspec = pl.BlockSpec((1, 1, dx), lambda b, s: (b, 0, 0))
        out_specs += [bspec, bspec]
        out_shape += [jax.ShapeDtypeStruct((nb, 1, dx), F32)] * 2
    return _pcall(
        body, side=side, name=name, grid=(nb, ns), in_specs=in_specs, out_specs=out_specs, out_shape=out_shape,
        compiler_params=_params(("arbitrary", "arbitrary")),
    )(*args)


def postnorm_fwd(x, pairs, g, mod, i_gate, coef, nb, name, target=None, ts=512, side=None):
    t, dx = x.shape
    with_loss = target is not None
    ts = min(ts, t // nb)
    ns = t // nb // ts
    n_mm = len(pairs)

    def body(*refs):
        yv = None
        for a_ref, b_ref in zip(refs[:n_mm], refs[n_mm:2 * n_mm]):
            d = jnp.dot(a_ref[...], b_ref[...], preferred_element_type=F32)
            yv = d if yv is None else yv + d
        refs = refs[2 * n_mm:]
        x_ref, g_ref, mod_ref = refs[:3]
        refs[-1][...] = yv
        r = lax.rsqrt(jnp.mean(yv * yv, axis=-1, keepdims=True) + EPS)
        out = x_ref[...] + (coef * mod_ref[0, i_gate:i_gate + 1, :]) * (yv * r * g_ref[...])
        if not with_loss:
            refs[3][...] = out
            return
        t_ref, dx_ref, loss_ref = refs[3:6]
        b, s = pl.program_id(0), pl.program_id(1)
        err = out - t_ref[...]
        dx_ref[...] = err * (1.0 / dx)
        part = (0.5 / dx) * jnp.sum(jnp.sum(err * err, axis=1, keepdims=True), axis=0, keepdims=True)
        first = jnp.logical_and(b == 0, s == 0)

        @pl.when(first)
        def _():
            loss_ref[...] = part

        @pl.when(jnp.logical_not(first))
        def _():
            loss_ref[...] += part

    row = _row_specs(dx, ts, ns)
    in_specs = [_row_specs(a.shape[1], ts, ns) for a, _ in pairs]
    in_specs += [pl.BlockSpec(b.shape, lambda b_, s_: (0, 0)) for _, b in pairs]
    in_specs += [row, _vec_spec(dx), _mod_spec()]
    args = [a for a, _ in pairs] + [b for _, b in pairs] + [x, g, mod]
    row_shape = jax.ShapeDtypeStruct((t, dx), F32)
    out_specs, out_shape = [row, row], [row_shape, row_shape]
    if with_loss:
        in_specs.append(row)
        args.append(target)
        out_specs = [row, pl.BlockSpec((1, 1), lambda b, s: (0, 0)), row]
        out_shape = [row_shape, jax.ShapeDtypeStruct((1, 1), F32), row_shape]
    return _pcall(
        body, side=side, name=name, grid=(nb, ns), in_specs=in_specs, out_specs=out_specs, out_shape=out_shape,
        compiler_params=_params(("arbitrary", "arbitrary")),
    )(*args)


def postnorm_bwd(dxo, y0, g, mod, i_gate, coef, nb, name, ts=1024):
    t, dx = y0.shape
    ts = min(ts, t // nb)
    ns = t // nb // ts

    def body(d_ref, y_ref, g_ref, mod_ref, dy_ref, dg_ref, dgate_ref):
        b, s = pl.program_id(0), pl.program_id(1)
        yv = y_ref[...]
        dv = d_ref[...]
        gv = g_ref[...]
        r = lax.rsqrt(jnp.mean(yv * yv, axis=-1, keepdims=True) + EPS)
        yhat = yv * r
        dgate = jnp.sum(dv * (coef * (yhat * gv)), axis=0, keepdims=True)[None]
        dyn = dv * (coef * mod_ref[0, i_gate:i_gate + 1, :])
        dg = jnp.sum(dyn * yhat, axis=0, keepdims=True)
        dyhat = dyn * gv
        dy_ref[...] = (r * (dyhat - yhat * jnp.mean(dyhat * yhat, axis=-1, keepdims=True))).astype(BF16)

        @pl.when(s == 0)
        def _():
            dgate_ref[...] = dgate

        @pl.when(s > 0)
        def _():
            dgate_ref[...] += dgate

        first = jnp.logical_and(b == 0, s == 0)

        @pl.when(first)
        def _():
            dg_ref[...] = dg

        @pl.when(jnp.logical_not(first))
        def _():
            dg_ref[...] += dg

    row = _row_specs(dx, ts, ns)
    return _pcall(
        body, name=name, grid=(nb, ns), in_specs=[row, row, _vec_spec(dx), _mod_spec()],
        out_specs=[row, _vec_spec(dx), pl.BlockSpec((1, 1, dx), lambda b, s: (b, 0, 0))],
        out_shape=[jax.ShapeDtypeStruct((t, dx), BF16), jax.ShapeDtypeStruct((1, dx), F32),
                   jax.ShapeDtypeStruct((nb, 1, dx), F32)],
        compiler_params=_params(("arbitrary", "arbitrary")),
    )(dxo, y0, g, mod)


def rope_tables(positions):
    inv_freq = ROPE_THETA ** (-jnp.arange(0, ROT_DIM, 2, dtype=F32) / ROT_DIM)
    ang = positions.astype(F32).reshape(-1, 1) * inv_freq
    cos, sin = jnp.cos(ang), jnp.sin(ang)
    half = ROT_DIM // 2
    z = lambda n: jnp.zeros((ang.shape[0], n), F32)
    c = jnp.concatenate([cos, cos, jnp.ones((ang.shape[0], HEAD_DIM - ROT_DIM), F32)], axis=1)
    sp = jnp.concatenate([z(half), sin, z(HEAD_DIM - ROT_DIM)], axis=1)
    sm = jnp.concatenate([-sin, z(HEAD_DIM - half)], axis=1)
    return tuple(jnp.tile(a, (1, HEADS_PER_STEP)) for a in (c, sp, sm))


def _scan_lanes(x, reverse):
    n = x.shape[-1]
    lane = lax.broadcasted_iota(jnp.int32, x.shape, x.ndim - 1)
    k = 1
    while k < n:
        if reverse:
            x = x + jnp.where(lane < n - k, pltpu.roll(x, n - k, x.ndim - 1), 0.0)
        else:
            x = x + jnp.where(lane >= k, pltpu.roll(x, k, x.ndim - 1), 0.0)
        k *= 2
    return x


def _log_sigmoid(z):
    return jnp.minimum(z, 0.0) - jnp.log(1.0 + jnp.exp(-jnp.abs(z)))


def fox_gate_fwd(ft, b_forget, name):
    nb, nh, s = ft.shape

    def body(f_ref, b_ref, o_ref):
        z = f_ref[0] + b_ref[...]
        o_ref[0] = -_scan_lanes(_log_sigmoid(z), False)

    spec = pl.BlockSpec((1, nh, s), lambda b: (b, 0, 0))
    return _pcall(
        body, name=name, grid=(nb,), in_specs=[spec, pl.BlockSpec((nh, 1), lambda b: (0, 0))], out_specs=spec,
        out_shape=jax.ShapeDtypeStruct((nb, nh, s), F32), compiler_params=_params(("arbitrary",)),
    )(ft, b_forget)


def fox_gate_bwd(dcb, drow, ft, b_forget, name):
    nb, nh, s = ft.shape

    def body(d_ref, r_ref, f_ref, b_ref, dz_ref, db_ref):
        b = pl.program_id(0)
        z = f_ref[0] + b_ref[...]
        dlf = _scan_lanes(r_ref[0] - d_ref[0], True)
        dz = dlf * _sigmoid(-z)
        dz_ref[0] = dz
        db = jnp.sum(dz, axis=1, keepdims=True)

        @pl.when(b == 0)
        def _():
            db_ref[...] = db

        @pl.when(b > 0)
        def _():
            db_ref[...] += db

    spec = pl.BlockSpec((1, nh, s), lambda b: (b, 0, 0))
    vec = pl.BlockSpec((nh, 1), lambda b: (0, 0))
    return _pcall(
        body, name=name, grid=(nb,), in_specs=[spec, spec, spec, vec], out_specs=[spec, vec],
        out_shape=[jax.ShapeDtypeStruct((nb, nh, s), F32), jax.ShapeDtypeStruct((nh, 1), F32)],
        compiler_params=_params(("arbitrary",)),
    )(dcb, drow, ft, b_forget)


ATTN_TQ = 512
ATTN_TK = 512
ONES_ROWS = 16


def _rows_to_cols(rows):
    tile = jnp.concatenate([jnp.broadcast_to(rw, (HEAD_DIM, rw.shape[1])) for rw in rows], axis=0)
    return tile.T


def _block_delta(s, tq, tk):
    off = jnp.arange(s // tk) - (tq // tk - 1)
    return off[:, None, None] * tk + jnp.arange(tq)[None, None, :] - jnp.arange(tk)[None, :, None]


def dilated_table(s, tq, tk):
    delta = _block_delta(s, tq, tk)
    count = jnp.zeros(delta.shape, F32)
    for window, dil in DILATED_PATTERNS:
        count = count + ((delta >= 0) & (delta <= window) & (delta % dil == 0)).astype(F32)
    return jnp.where(count > 0, jnp.log(jnp.maximum(count, 1.0)), NEG)


def causal_table(s, tq, tk):
    return jnp.where(_block_delta(s, tq, tk) >= 0, 0.0, NEG).astype(F32)


def attn_fwd(q_arr, q_off, k_arr, k_off, v_arr, v_off, table, colbias, nb, name, side=None, off_diag_bias=True):
    t = q_arr.shape[0]
    s = t // nb
    tk, tq = table.shape[1:]
    assert tq == tk, "the diagonal handling below is written for square tiles"
    nq, nk = s // tq, s // tk
    npairs = WIDTH_A // LANES
    use_cb = colbias is not None

    def body(*refs):
        refs = list(refs)
        q_ref, k_ref, v_ref, tab_ref = refs[:4]
        cb_ref = refs[4] if use_cb else None
        tail = refs[-(HEADS_PER_STEP + int(use_cb)):]
        acc_s = tail[:HEADS_PER_STEP]
        cbc_s = tail[-1] if use_cb else None
        o_ref, lse_ref, vt_s = refs[-3 - len(tail):-len(tail)]
        qi = pl.program_id(2)

        heads = [slice(h * HEAD_DIM, (h + 1) * HEAD_DIM) for h in range(HEADS_PER_STEP)]

        @pl.when(qi == 0)
        def _():
            for cblk in range(nk):
                vt = v_ref[cblk * tk:(cblk + 1) * tk, :].astype(F32).T.astype(BF16)
                for h, hs in enumerate(heads):
                    vt_s[cblk, h, 0:HEAD_DIM, :] = vt[hs, :]
                    vt_s[cblk, h, HEAD_DIM:, :] = jnp.ones((ONES_ROWS, tk), BF16)
                if use_cb:
                    cbc_s[cblk] = _rows_to_cols([cb_ref[0, h, cblk] for h in range(HEADS_PER_STEP)])

        qt_all = (q_ref[...].astype(F32) * ATTN_SCALE).T.astype(BF16)
        qts = [qt_all[hs, :] for hs in heads]
        for a in acc_s:
            a[...] = jnp.zeros_like(a)

        def tile(kb, tab, k0, klen, q0, carry):
            ks = pl.multiple_of(kb * tk + k0, klen)
            sts, out = [], []
            for h, hs in enumerate(heads):
                st = jnp.dot(k_ref[pl.ds(ks, klen), hs], qts[h][:, q0:], preferred_element_type=F32)
                if tab is not None:
                    st = st + tab
                if use_cb:
                    st = st + cbc_s[kb, k0:k0 + klen, h * HEAD_DIM:h * HEAD_DIM + 1]
                sts.append(st)
            m_old = [carry[h][:, q0:] for h in range(HEADS_PER_STEP)]
            m_new = [jnp.maximum(m_old[h], jnp.max(sts[h], axis=0, keepdims=True)) for h in range(HEADS_PER_STEP)]
            for h in range(HEADS_PER_STEP):
                pt = jnp.exp(sts[h] - m_new[h]).astype(BF16)
                acc_s[h][:, q0:] = (jnp.exp(m_old[h] - m_new[h]) * acc_s[h][:, q0:]
                                    + jnp.dot(vt_s[kb, h, :, k0:k0 + klen], pt, preferred_element_type=F32))
                out.append(m_new[h] if q0 == 0 else jnp.concatenate([carry[h][:, :q0], m_new[h]], axis=1))
            return tuple(out)

        fin = lax.fori_loop(0, qi, lambda kb, c: tile(kb, tab_ref[qi - kb] if off_diag_bias else None, 0, tk, 0, c),
                            tuple(jnp.full((1, tq), NEG, F32) for _ in heads))
        half = tk // 2
        fin = tile(qi, tab_ref[0, 0:half, :], 0, half, 0, fin)
        fin = tile(qi, tab_ref[0, half:, half:], half, half, half, fin)
        outs = []
        for h in range(HEADS_PER_STEP):
            l = acc_s[h][HEAD_DIM:HEAD_DIM + 1, :]
            outs.append(acc_s[h][0:HEAD_DIM, :] / l)
            lse_ref[0, h, 0] = fin[h] + jnp.log(l)
        o_ref[...] = jnp.concatenate(outs, axis=0).T

    def seq_spec(off):
        return pl.BlockSpec((s, LANES), lambda b, j, i: (b, off + j))

    in_specs = [pl.BlockSpec((tq, LANES), lambda b, j, i: (b * nq + i, q_off + j)), seq_spec(k_off), seq_spec(v_off),
                pl.BlockSpec(table.shape, lambda b, j, i: (0, 0, 0))]
    args = [q_arr, k_arr, v_arr, table]
    if use_cb:
        in_specs.append(pl.BlockSpec((1, HEADS_PER_STEP, nk, 1, tk), lambda b, j, i: (b, j, 0, 0, 0)))
        args.append(colbias)
    n_heads = npairs * HEADS_PER_STEP
    return _pcall(
        body, side=side, name=name, grid=(nb, npairs, nq), in_specs=in_specs,
        out_specs=[pl.BlockSpec((tq, LANES), lambda b, j, i: (b * nq + i, j)),
                   pl.BlockSpec((1, HEADS_PER_STEP, 1, 1, tq), lambda b, j, i: (b, j, i, 0, 0))],
        out_shape=[jax.ShapeDtypeStruct((t, npairs * LANES), F32), jax.ShapeDtypeStruct((nb, n_heads, nq, 1, tq), F32)],
        scratch_shapes=[pltpu.VMEM((nk, HEADS_PER_STEP, HEAD_DIM + ONES_ROWS, tk), BF16)]
        + [pltpu.VMEM((HEAD_DIM + ONES_ROWS, tq), F32)] * HEADS_PER_STEP
        + ([pltpu.VMEM((nk, tk, LANES), F32)] if use_cb else []),
        compiler_params=_params(("arbitrary", "arbitrary", "arbitrary")),
    )(*args)


def attn_bwd(q_arr, q_off, k_arr, k_off, v_arr, v_off, o_arr, lse_arr, do_arr, table, colbias, nb, name, side=None,
             rope_tabs=None, off_diag_bias=True):
    t = q_arr.shape[0]
    s = t // nb
    tk, tq = table.shape[1:]
    assert tq == tk, "the diagonal handling below is written for square tiles"
    nq, nk = s // tq, s // tk
    npairs = WIDTH_A // LANES
    use_cb = colbias is not None

    def body(*refs):
        refs = list(refs)
        q_ref, k_ref, v_ref, o_ref, lse_ref, do_ref, tab_ref = refs[:7]
        pos = 7
        cb_ref = None
        if use_cb:
            cb_ref = refs[pos]
            pos += 1
        rope_refs = None
        if rope_tabs is not None:
            rope_refs = refs[pos:pos + 3]
            pos += 3
        dq_ref, dk_ref, dv_ref = refs[pos:pos + 3]
        pos += 3
        dcb_ref = drow_ref = None
        if use_cb:
            dcb_ref, drow_ref = refs[pos:pos + 2]
            pos += 2
        kt_s, dkt_s, dvt_s = refs[pos:pos + 3]
        dqt_s = refs[pos + 3:pos + 3 + HEADS_PER_STEP]
        dcb_s, cbc_s = refs[pos + 3 + HEADS_PER_STEP:pos + 5 + HEADS_PER_STEP] if use_cb else (None, None)

        heads = [slice(h * HEAD_DIM, (h + 1) * HEAD_DIM) for h in range(HEADS_PER_STEP)]
        for cblk in range(nk):
            kt_s[cblk] = k_ref[cblk * tk:(cblk + 1) * tk, :].astype(F32).T.astype(BF16)
        dkt_s[...] = jnp.zeros_like(dkt_s)
        dvt_s[...] = jnp.zeros_like(dvt_s)
        if use_cb:
            dcb_s[...] = jnp.zeros_like(dcb_s)
            for cblk in range(nk):
                cbc_s[cblk] = _rows_to_cols([cb_ref[0, h, cblk] for h in range(HEADS_PER_STEP)])
        ones = jnp.ones((8, HEAD_DIM), BF16)

        def q_loop(qi, carry):
            qs = pl.multiple_of(qi * tq, tq)
            q_all = (q_ref[pl.ds(qs, tq), :].astype(F32) * ATTN_SCALE)
            do_all = do_ref[pl.ds(qs, tq), :]
            qt_all = q_all.T.astype(BF16)
            dot_all = do_all.T.astype(BF16)
            qt, dot, lse, dsum = [], [], [], []
            for h, hs in enumerate(heads):
                qt.append(qt_all[hs, :])
                dot.append(dot_all[hs, :])
                lse.append(lse_ref[0, h, qi])
                prod = do_all[:, hs] * o_ref[pl.ds(qs, tq), hs]
                hi = prod.astype(BF16)
                lo = (prod - hi.astype(F32)).astype(BF16)
                dsum.append((lax.dot_general(ones, hi, NT_DIMS, preferred_element_type=F32)
                             + lax.dot_general(ones, lo, NT_DIMS, preferred_element_type=F32))[0:1, :])
            for a in dqt_s:
                a[...] = jnp.zeros_like(a)

            def tile(kb, tab, k0, klen, q0, drow):
                ks = pl.multiple_of(kb * tk + k0, klen)
                keys = slice(k0, k0 + klen)
                sts, dpts, out = [], [], []
                for h, hs in enumerate(heads):
                    st = jnp.dot(k_ref[pl.ds(ks, klen), hs], qt[h][:, q0:], preferred_element_type=F32)
                    if tab is not None:
                        st = st + tab
                    if use_cb:
                        st = st + cbc_s[kb, keys, h * HEAD_DIM:h * HEAD_DIM + 1]
                    sts.append(st)
                    dpts.append(jnp.dot(v_ref[pl.ds(ks, klen), hs], dot[h][:, q0:], preferred_element_type=F32))
                for h, hs in enumerate(heads):
                    pt = jnp.exp(sts[h] - lse[h][:, q0:])
                    dst = pt * (dpts[h] - dsum[h][:, q0:])
                    dst_b = dst.astype(BF16)
                    dvt_s[h, kb, :, keys] += lax.dot_general(dot[h][:, q0:], pt.astype(BF16), NT_DIMS,
                                                             preferred_element_type=F32)
                    dkt_s[h, kb, :, keys] += lax.dot_general(qt[h][:, q0:], dst_b, NT_DIMS, preferred_element_type=F32)
                    dqt_s[h][:, q0:] += jnp.dot(kt_s[kb, hs, keys], dst_b, preferred_element_type=F32)
                    if use_cb:
                        dcb_s[h, pl.ds(ks, klen), :] += jnp.sum(dst, axis=1, keepdims=True)
                        dr = drow[h][:, q0:] + jnp.sum(dst, axis=0, keepdims=True)
                        out.append(dr if q0 == 0 else jnp.concatenate([drow[h][:, :q0], dr], axis=1))
                    else:
                        out.append(drow[h])
                return tuple(out)

            drow = lax.fori_loop(0, qi, lambda kb, c: tile(kb, tab_ref[qi - kb] if off_diag_bias else None, 0, tk, 0, c),
                                 tuple(jnp.zeros((1, tq), F32) for _ in heads))
            half = tk // 2
            drow = tile(qi, tab_ref[0, 0:half, :], 0, half, 0, drow)
            drow = tile(qi, tab_ref[0, half:, half:], half, half, half, drow)
            dq = (jnp.concatenate([a[...] for a in dqt_s], axis=0) * ATTN_SCALE).T
            if rope_refs is not None:
                dq = _rotate(dq, *[coef[pl.ds(qs, tq), :] for coef in rope_refs], True)
            dq_ref[pl.ds(qs, tq), :] = dq.astype(dq_ref.dtype)
            if use_cb:
                for h in range(HEADS_PER_STEP):
                    drow_ref[0, h, qi] = drow[h]
            return carry

        lax.fori_loop(0, nq, q_loop, 0)
        for cblk in range(nk):
            rows = slice(cblk * tk, (cblk + 1) * tk)
            dk = jnp.concatenate([dkt_s[h, cblk] for h in range(HEADS_PER_STEP)], axis=0).T
            if rope_refs is not None:
                dk = _rotate(dk, *[coef[rows, :] for coef in rope_refs], True)
            dk_ref[rows, :] = dk.astype(dk_ref.dtype)
            dv_ref[rows, :] = jnp.concatenate([dvt_s[h, cblk] for h in range(HEADS_PER_STEP)], axis=0).T.astype(dv_ref.dtype)
            if use_cb:
                for h in range(HEADS_PER_STEP):
                    dcb_ref[0, h, cblk] = jnp.broadcast_to(dcb_s[h, rows, :], (tk, LANES)).T[0:1, :]

    def seq_spec(off):
        return pl.BlockSpec((s, LANES), lambda b, j: (b, off + j))

    row_spec = pl.BlockSpec((1, HEADS_PER_STEP, nq, 1, tq), lambda b, j: (b, j, 0, 0, 0))
    in_specs = [seq_spec(q_off), seq_spec(k_off), seq_spec(v_off), seq_spec(0), row_spec, seq_spec(0),
                pl.BlockSpec(table.shape, lambda b, j: (0, 0, 0))]
    args = [q_arr, k_arr, v_arr, o_arr, lse_arr, do_arr, table]
    width = npairs * LANES
    out_specs = [seq_spec(0)] * 3
    out_shape = [jax.ShapeDtypeStruct((t, width), BF16)] * 3
    scratch = [pltpu.VMEM((nk, LANES, tk), BF16), pltpu.VMEM((HEADS_PER_STEP, nk, HEAD_DIM, tk), F32),
               pltpu.VMEM((HEADS_PER_STEP, nk, HEAD_DIM, tk), F32)] + [pltpu.VMEM((HEAD_DIM, tq), F32)] * HEADS_PER_STEP
    if use_cb:
        cb_spec = pl.BlockSpec((1, HEADS_PER_STEP, nk, 1, tk), lambda b, j: (b, j, 0, 0, 0))
        in_specs.append(cb_spec)
        args.append(colbias)
    if rope_tabs is not None:
        in_specs += [pl.BlockSpec((s, LANES), lambda b, j: (b, 0))] * 3
        args += list(rope_tabs)
    if use_cb:
        out_specs += [cb_spec, row_spec]
        out_shape += [jax.ShapeDtypeStruct(colbias.shape, F32), jax.ShapeDtypeStruct(lse_arr.shape, F32)]
        scratch += [pltpu.VMEM((HEADS_PER_STEP, s, 1), F32), pltpu.VMEM((nk, tk, LANES), F32)]
    return _pcall(
        body, side=side, name=name, grid=(nb, npairs), in_specs=in_specs, out_specs=out_specs, out_shape=out_shape,
        scratch_shapes=scratch, compiler_params=_params(("arbitrary", "arbitrary")),
    )(*args)


def ada_fwd(c_all, w_ada, b_cols, name):
    def body(c_ref, w_ref, b_ref, o_ref):
        cv = c_ref[...]
        sc = (cv * _sigmoid(cv)).astype(BF16)
        o_ref[...] = jnp.dot(sc, w_ref[...].astype(BF16), preferred_element_type=F32) + b_ref[...]

    return _pcall(body, name=name, out_shape=jax.ShapeDtypeStruct((c_all.shape[0], w_ada.shape[1]), F32),
                  compiler_params=_params())(c_all, w_ada, b_cols)


def ada_bwd(c_all, dmod_cols, name):
    def body(c_ref, d_ref, o_ref):
        cv = c_ref[...]
        sc = (cv * _sigmoid(cv)).astype(BF16)
        o_ref[...] = lax.dot_general(sc, d_ref[...].astype(BF16), TN_DIMS, preferred_element_type=F32)

    return _pcall(body, name=name, out_shape=jax.ShapeDtypeStruct((c_all.shape[1], dmod_cols.shape[1]), F32),
                  compiler_params=_params())(c_all, dmod_cols)


def adamw(parts, group, w, m, v, name, tr=None):
    n = parts.shape[0]
    r, c = w.shape
    tr = r if tr is None else tr
    c1 = 1.0 - ADAM_B1 ** ADAM_STEP
    c2 = 1.0 - ADAM_B2 ** ADAM_STEP

    def body(p_ref, w_ref, m_ref, v_ref, g_ref, d_ref, nm_ref, nv_ref):
        g = p_ref[0, 0].astype(F32)
        for i in range(1, n):
            g = g + p_ref[i, 0].astype(F32)
        wv = w_ref[...]
        nm = ADAM_B1 * m_ref[...] + (1.0 - ADAM_B1) * g
        nv = ADAM_B2 * v_ref[...] + (1.0 - ADAM_B2) * (g * g)
        g_ref[...] = g
        nm_ref[...] = nm
        nv_ref[...] = nv
        d_ref[...] = -ADAM_LR * ((nm / c1) / (jnp.sqrt(nv / c2) + ADAM_EPS) + ADAM_WD * wv)

    spec = pl.BlockSpec((tr, c), lambda i: (i, 0))
    shape = jax.ShapeDtypeStruct((r, c), F32)
    return _pcall(
        body, name=name, grid=(r // tr,),
        in_specs=[pl.BlockSpec((n, 1, tr, c), lambda i: (0, group, i, 0)), spec, spec, spec],
        out_specs=[spec] * 4, out_shape=[shape] * 4, compiler_params=_params(("arbitrary",)),
    )(parts, w, m, v)


def all_gather(arrs, name):
    n = len(arrs)
    hbm = pl.BlockSpec(memory_space=pl.ANY)

    def body(*refs):
        ins, outs = refs[:n], refs[n:2 * n]
        send_sems, recv_sems, local_sems = refs[2 * n:]
        x, y, c = _place()
        me, sibling = (x, y, c), (x, y, 1 - c)
        chips = [(1 - x, y), (x, 1 - y), (1 - x, 1 - y)]

        def copy(a, k, block, to, src=None):
            dst = outs[a].at[_slot(block)]
            return pltpu.make_async_remote_copy(
                src_ref=dst if src is None else src, dst_ref=dst, send_sem=send_sems.at[a * 7 + k],
                recv_sem=recv_sems.at[a * 7 + k], device_id=to, device_id_type=MESH)

        mine = [pltpu.make_async_copy(ins[a], outs[a].at[_slot(me)], local_sems.at[a]) for a in range(n)]
        for cp in mine:
            cp.start()
        first = []
        for a in range(n):
            first.append(copy(a, 0, me, sibling, src=ins[a]))
            first += [copy(a, 1 + j, me, (*chip, c), src=ins[a]) for j, chip in enumerate(chips)]
        for cp in first:
            cp.start()
        passed = []
        for a in range(n):
            for j, chip in enumerate(chips):
                copy(a, 1 + j, (*chip, c), me).wait_recv()
                cp = copy(a, 4 + j, (*chip, c), sibling)
                cp.start()
                passed.append(cp)
        for a in range(n):
            copy(a, 0, sibling, me).wait_recv()
            for j, chip in enumerate(chips):
                copy(a, 4 + j, (*chip, 1 - c), me).wait_recv()
        for cp in first + passed:
            cp.wait_send()
        for cp in mine:
            cp.wait()

    return _pcall(
        body, name=name, in_specs=[hbm] * n, out_specs=[hbm] * n,
        out_shape=[jax.ShapeDtypeStruct((N_DEV,) + a.shape, a.dtype) for a in arrs],
        scratch_shapes=[pltpu.SemaphoreType.DMA((7 * n,)), pltpu.SemaphoreType.DMA((7 * n,)),
                        pltpu.SemaphoreType.DMA((n,))],
        compiler_params=pltpu.CompilerParams(has_side_effects=True),
    )(*arrs)


def _t(w):
    return jnp.swapaxes(w, -1, -2)


def _rows_from_blocks(blocks, pad_to=None):
    full = blocks.reshape(-1, blocks.shape[2])
    if pad_to is not None and pad_to > full.shape[0]:
        full = jnp.pad(full, ((0, pad_to - full.shape[0]), (0, 0)))
    return full


def _rows_to_blocks(full, nrows):
    return full[:nrows].reshape(N_DEV, nrows // N_DEV, full.shape[1])


SMALL_ORDER = ("g_pre_ff1", "g_post_ff1", "g_pre_mix", "g_post_mix", "g_out_a", "g_out_b", "g_pre_ff2", "g_post_ff2",
               "b_forget")


def _pack_small(vals):
    rows = []
    for name in SMALL_ORDER:
        v = vals[name].reshape(1, -1)
        if v.shape[1] % LANES:
            v = jnp.pad(v, ((0, 0), (0, LANES - v.shape[1] % LANES)))
        rows.append(v)
    return jnp.concatenate(rows, axis=1)


def _unpack_small(row, sizes):
    out, pos = {}, 0
    for name in SMALL_ORDER:
        n = sizes[name]
        out[name] = row[:, pos:pos + n]
        pos += -(-n // LANES) * LANES
    return out


def _ffn_forward(x, mod, g_pre, g_post, wg, wu, wd, i0, nb, tag, target=None, side=None, side_down=None):
    h = prenorm_fwd(x, g_pre, mod, i0, i0 + 1, nb, f"{tag}_prenorm")
    res, side_out = ffn_up(h, wg, wu, f"{tag}_up", side=side), None
    if side is not None:
        res, side_out = res
    gate, up, act = res
    if callable(wd):
        wd = wd(side_out)
    res, side_down_out = postnorm_fwd(x, [(act, wd)], g_post, mod, i0 + 2, 0.5, nb, f"{tag}_down_postnorm",
                                      target=target, side=side_down), None
    if side_down is not None:
        res, side_down_out = res
    out, y0 = (res[0] if target is None else tuple(res[:2])), res[-1]
    return out, (x, h, gate, up, act, y0), wd, side_out, side_down_out


def _ffn_backward(dxo, saved, mod, g_pre, g_post, wg, wu, wd, i0, nb, tag, side=None, chain=False):
    x, h, gate, up, act, y0 = saved
    dy0, dg_post, dgate_mod = postnorm_bwd(dxo, y0, g_post, mod, i0 + 2, 0.5, nb, f"{tag}_postnorm_bwd")
    dwd = mm_tn(act, dy0, BF16, f"{tag}_dwd")
    res, side_out = ffn_down_bwd(dy0, wd, gate, up, f"{tag}_down_bwd", side=side), None
    if side is not None:
        res, side_out = res
    dgate, dup = res
    dh_pairs = [(dgate, wg), (dup, wu)]
    if chain:
        dwg, (dwd,) = mm_tn(dgate, h, BF16, f"{tag}_dwg", side=([_rows_to_blocks(dwd, D_FF)[:, None]], False))
        dwu, (dwg,) = mm_tn(dup, h, BF16, f"{tag}_dwu", side=([_rows_to_blocks(dwg, D_FF)[:, None]], False))
        (dx, dg_pre, dsc, dsh), (dwu,) = prenorm_bwd(dh_pairs, x, g_pre, mod, i0 + 1, dxo, nb, f"{tag}_dh_prenorm_bwd",
                                                     ts=DH_ROWS, side=([_rows_to_blocks(dwu, D_FF)[:, None]], False))
    else:
        dwg = mm_tn(dgate, h, BF16, f"{tag}_dwg")
        dwu = mm_tn(dup, h, BF16, f"{tag}_dwu")
        dx, dg_pre, dsc, dsh = prenorm_bwd(dh_pairs, x, g_pre, mod, i0 + 1, dxo, nb, f"{tag}_dh_prenorm_bwd", ts=DH_ROWS)
    return dx, dict(g_pre=dg_pre, g_post=dg_post, wg=dwg, wu=dwu, wd=dwd, mod=(dsh, dsc, dgate_mod)), side_out


def kernel(x, c, positions, w_ada, b_ada, g_pre_ff1, g_post_ff1, w_ff1_gate, w_ff1_up, w_ff1_down, g_pre_mix, g_post_mix, w_in, b_forget, g_out_a, g_out_b, w_out, g_pre_ff2, g_post_ff2, w_ff2_gate, w_ff2_up, w_ff2_down, loss_target, m_w_ada, m_b_ada, m_g_pre_ff1, m_g_post_ff1, m_w_ff1_gate, m_w_ff1_up, m_w_ff1_down, m_g_pre_mix, m_g_post_mix, m_w_in, m_b_forget, m_g_out_a, m_g_out_b, m_w_out, m_g_pre_ff2, m_g_post_ff2, m_w_ff2_gate, m_w_ff2_up, m_w_ff2_down, v_w_ada, v_b_ada, v_g_pre_ff1, v_g_post_ff1, v_w_ff1_gate, v_w_ff1_up, v_w_ff1_down, v_g_pre_mix, v_g_post_mix, v_w_in, v_b_forget, v_g_out_a, v_g_out_b, v_w_out, v_g_pre_ff2, v_g_post_ff2, v_w_ff2_gate, v_w_ff2_up, v_w_ff2_down):
    weights = dict(w_ada=w_ada, b_ada=b_ada, g_pre_ff1=g_pre_ff1, g_post_ff1=g_post_ff1, w_ff1_gate=w_ff1_gate,
                   w_ff1_up=w_ff1_up, w_ff1_down=w_ff1_down, g_pre_mix=g_pre_mix, g_post_mix=g_post_mix, w_in=w_in,
                   b_forget=b_forget, g_out_a=g_out_a, g_out_b=g_out_b, w_out=w_out, g_pre_ff2=g_pre_ff2,
                   g_post_ff2=g_post_ff2, w_ff2_gate=w_ff2_gate, w_ff2_up=w_ff2_up, w_ff2_down=w_ff2_down)
    mom_m = dict(w_ada=m_w_ada, b_ada=m_b_ada, g_pre_ff1=m_g_pre_ff1, g_post_ff1=m_g_post_ff1, w_ff1_gate=m_w_ff1_gate,
                 w_ff1_up=m_w_ff1_up, w_ff1_down=m_w_ff1_down, g_pre_mix=m_g_pre_mix, g_post_mix=m_g_post_mix,
                 w_in=m_w_in, b_forget=m_b_forget, g_out_a=m_g_out_a, g_out_b=m_g_out_b, w_out=m_w_out,
                 g_pre_ff2=m_g_pre_ff2, g_post_ff2=m_g_post_ff2, w_ff2_gate=m_w_ff2_gate, w_ff2_up=m_w_ff2_up,
                 w_ff2_down=m_w_ff2_down)
    mom_v = dict(w_ada=v_w_ada, b_ada=v_b_ada, g_pre_ff1=v_g_pre_ff1, g_post_ff1=v_g_post_ff1, w_ff1_gate=v_w_ff1_gate,
                 w_ff1_up=v_w_ff1_up, w_ff1_down=v_w_ff1_down, g_pre_mix=v_g_pre_mix, g_post_mix=v_g_post_mix,
                 w_in=v_w_in, b_forget=v_b_forget, g_out_a=v_g_out_a, g_out_b=v_g_out_b, w_out=v_w_out,
                 g_pre_ff2=v_g_pre_ff2, g_post_ff2=v_g_post_ff2, w_ff2_gate=v_w_ff2_gate, w_ff2_up=v_w_ff2_up,
                 w_ff2_down=v_w_ff2_down)
    order = list(weights)

    nb, s, d = x.shape
    t = nb * s
    me = _slot(_place())
    nbg = nb * N_DEV
    ada_cols = w_ada.shape[2]

    bf = lambda w: w[0].astype(BF16)
    bft = lambda w: _t(w)[0].astype(BF16)
    c_all, ff1_all = all_gather([c, jnp.stack([bft(w_ff1_gate), bft(w_ff1_up)])], "gather_ff1")
    c_all = c_all.reshape(nbg, d)
    wg1, wu1 = (_rows_from_blocks(ff1_all[:, i], D_FF_PAD) for i in range(2))

    b_cols = lax.dynamic_slice(b_ada, (0, me * ada_cols), (1, ada_cols))
    mod_cols = ada_fwd(c_all, w_ada[0], b_cols, "ada_fwd")
    (mod_all,) = all_gather([mod_cols], "gather_mod")
    mod = lax.dynamic_slice(mod_all, (0, me * nb, 0), (N_DEV, nb, ada_cols))
    mod = mod.transpose(1, 0, 2).reshape(nb, N_MOD, d)

    xf = x.reshape(t, d)
    target = loss_target.reshape(t, d)

    x1, saved1, wd1, (_, w_out_all), (w_in_all,) = _ffn_forward(
        xf, mod, g_pre_ff1, g_post_ff1, wg1, wu1, lambda got: _rows_from_blocks(got[0], D_FF_PAD), 0, nb, "ff1",
        side=([bf(w_ff1_down), bf(w_out)], True), side_down=([bft(w_in)], True))
    w_in_t = _rows_from_blocks(w_in_all)
    n_qkv = 3 * (WIDTH_A + WIDTH_B)
    w_qkv_t = w_in_t[:n_qkv]
    w_f_t = jnp.pad(w_in_t[n_qkv:], ((0, LANES - N_HEADS_B), (0, 0)))
    w_o = _rows_from_blocks(w_out_all)
    w_o_a, w_o_b = w_o[:WIDTH_A], w_o[WIDTH_A:]

    h2 = prenorm_fwd(x1, g_pre_mix, mod, 3, 4, nb, "mix_prenorm")
    tables = rope_tables(positions)
    proj = mm_rows([(h2, w_qkv_t)], True, BF16, "mix_proj", rope=(tables, 2 * WIDTH_A))
    f_logit = mm_rows([(h2, w_f_t)], True, F32, "mix_forget")
    tab_a = dilated_table(s, ATTN_TQ, ATTN_TK)
    tab_b = causal_table(s, ATTN_TQ, ATTN_TK)
    ft = f_logit[:, :N_HEADS_B].reshape(nb, s, N_HEADS_B).transpose(0, 2, 1)
    bf_col = b_forget.reshape(N_HEADS_B, 1)
    colbias = fox_gate_fwd(ft, bf_col, "fox_gate").reshape(nb, N_HEADS_B, s // ATTN_TK, 1, ATTN_TK)
    pa = WIDTH_A // LANES
    (o_a, lse_a), (ff2_all,) = attn_fwd(
        proj, 0, proj, pa, proj, 2 * pa, tab_a, None, nb, "attn_a",
        side=([jnp.stack([bft(w_ff2_gate), bft(w_ff2_up), bf(w_ff2_down)])], True))
    wg2, wu2, wd2 = (_rows_from_blocks(ff2_all[:, i], D_FF_PAD) for i in range(3))
    o_b, lse_b = attn_fwd(proj, 3 * pa, proj, 4 * pa, proj, 5 * pa, tab_b, colbias, nb, "attn_b", off_diag_bias=False)
    m_a = prenorm_fwd(o_a, g_out_a, None, None, None, nb, "out_norm_a")
    m_b = prenorm_fwd(o_b, g_out_b, None, None, None, nb, "out_norm_b")
    x2, y0m = postnorm_fwd(x1, [(m_a, w_o_a), (m_b, w_o_b)], g_post_mix, mod, 5, 1.0, nb, "mix_out_postnorm")

    (dx3, loss_part), saved2 = _ffn_forward(x2, mod, g_pre_ff2, g_post_ff2, wg2, wu2, wd2, 6, nb, "ff2", target=target)[:2]
    loss = lax.psum(loss_part[0, 0], ("x", "y", "c"))

    dx2, gr2, _ = _ffn_backward(dx3, saved2, mod, g_pre_ff2, g_post_ff2, wg2, wu2, wd2, 6, nb, "ff2")
    ff2_blocks = [jnp.stack([_rows_to_blocks(gr2[k], D_FF) for k in ("wg", "wu", "wd")], axis=1)]

    dy0m, dg_post_mix, dgate_mix = postnorm_bwd(dx2, y0m, g_post_mix, mod, 5, 1.0, nb, "mix_postnorm_bwd")
    dw_o_a = mm_tn(m_a, dy0m, BF16, "mix_dwo_a")
    dw_o_b = mm_tn(m_b, dy0m, BF16, "mix_dwo_b")
    do_a, dg_out_a = prenorm_bwd([(dy0m, w_o_a.T)], o_a, g_out_a, None, None, None, nb, "out_norm_a_bwd")
    do_b, dg_out_b = prenorm_bwd([(dy0m, w_o_b.T)], o_b, g_out_b, None, None, None, nb, "out_norm_b_bwd")
    (dq_a, dk_a, dv_a), (g_ff2,) = attn_bwd(proj, 0, proj, pa, proj, 2 * pa, o_a, lse_a, do_a, tab_a, None, nb,
                                            "attn_a_bwd", side=(ff2_blocks, False), rope_tabs=tables)
    dq_b, dk_b, dv_b, dcb, drow = attn_bwd(proj, 3 * pa, proj, 4 * pa, proj, 5 * pa, o_b, lse_b, do_b, tab_b, colbias, nb,
                                           "attn_b_bwd", off_diag_bias=False)
    dz_t, db_forget = fox_gate_bwd(dcb.reshape(nb, N_HEADS_B, s), drow.reshape(nb, N_HEADS_B, s), ft, bf_col,
                                   "fox_gate_bwd")
    dz = jnp.pad(dz_t.transpose(0, 2, 1).reshape(t, N_HEADS_B), ((0, 0), (0, LANES - N_HEADS_B))).astype(BF16)
    pieces = [dq_a, dk_a, dv_a, dq_b, dk_b, dv_b]
    w_pieces = [w_qkv_t[i * WIDTH_A:(i + 1) * WIDTH_A] for i in range(6)]
    dh2_pairs = list(zip(pieces, w_pieces)) + [(dz, w_f_t)]
    dw_in_t = jnp.concatenate([mm_tn(p, h2, BF16, f"mix_dwin_{i}") for i, p in enumerate(pieces)]
                              + [mm_tn(dz, h2, BF16, "mix_dwin_f")[:N_HEADS_B]], axis=0)
    dx1, dg_pre_mix, dsc_mix, dsh_mix = prenorm_bwd(dh2_pairs, x1, g_pre_mix, mod, 4, dx2, nb, "mix_dh_prenorm_bwd",
                                                    ts=DH_ROWS)

    g_in = _rows_to_blocks(dw_in_t, dw_in_t.shape[0])[:, None]
    g_out = _rows_to_blocks(jnp.concatenate([dw_o_a, dw_o_b], axis=0), d)[:, None]
    dx0, gr1, (g_in, g_out) = _ffn_backward(dx1, saved1, mod, g_pre_ff1, g_post_ff1, wg1, wu1, wd1, 0, nb, "ff1",
                                            side=([g_in, g_out], False), chain=True)
    grad_x = dx0.reshape(nb, s, d)

    dmod =jnp.concatenate(list(gr1["mod"]) + [dsh_mix, dsc_mix, dgate_mix] + list(gr2["mod"]), axis=1)
    small = _pack_small(dict(g_pre_ff1=gr1["g_pre"], g_post_ff1=gr1["g_post"], g_pre_mix=dg_pre_mix,
                             g_post_mix=dg_post_mix, g_out_a=dg_out_a, g_out_b=dg_out_b, g_pre_ff2=gr2["g_pre"],
                             g_post_ff2=gr2["g_post"], b_forget=db_forget))
    dmod_all, small_all = all_gather([dmod.reshape(nb, N_MOD * d), small], "gather_small_grads")
    dmod_all = dmod_all.reshape(nbg, N_MOD * d)

    res = {}
    def adamw_t(parts, group, n):
        return tuple(_t(r) for r in adamw(parts, group, _t(weights[n])[0], _t(mom_m[n])[0], _t(mom_v[n])[0], f"adamw_{n}"))

    res["w_ff1_gate"] = adamw_t(gr1["wg"], 0, "w_ff1_gate")
    res["w_ff1_up"] = adamw_t(gr1["wu"], 0, "w_ff1_up")
    res["w_ff2_gate"] = adamw_t(g_ff2, 0, "w_ff2_gate")
    res["w_ff2_up"] = adamw_t(g_ff2, 1, "w_ff2_up")
    res["w_ff1_down"] = adamw(gr1["wd"], 0, w_ff1_down[0], m_w_ff1_down[0], v_w_ff1_down[0], "adamw_ff1_down")
    res["w_ff2_down"] = adamw(g_ff2, 2, w_ff2_down[0], m_w_ff2_down[0], v_w_ff2_down[0], "adamw_ff2_down")
    res["w_in"] = adamw_t(g_in, 0, "w_in")
    res["w_out"] = adamw(g_out, 0, w_out[0], m_w_out[0], v_w_out[0], "adamw_out")
    dmod_cols = lax.dynamic_slice(dmod_all, (0, me * ada_cols), (nbg, ada_cols))
    dw_ada = ada_bwd(c_all, dmod_cols, "ada_bwd")
    res["w_ada"] = adamw(dw_ada[None, None], 0, w_ada[0], m_w_ada[0], v_w_ada[0], "adamw_ada", tr=256)
    res["b_ada"] = adamw(dmod_all[:, None, None], 0, b_ada, m_b_ada, v_b_ada, "adamw_b_ada")
    sizes = {n: weights[n].shape[1] for n in SMALL_ORDER}
    small_res = adamw(small_all[:, None], 0, _pack_small(weights), _pack_small(mom_m), _pack_small(mom_v), "adamw_small")
    small_res = [_unpack_small(r, sizes) for r in small_res]
    for n in SMALL_ORDER:
        res[n] = tuple(r[n] for r in small_res)

    outs = [loss, grad_x]
    for kind in range(4):
        for n in order:
            a = res[n][kind]
            outs.append(a.reshape(weights[n].shape))
    return tuple(outs)
```

```python
import functools

import jax
import jax.numpy as jnp
from jax import lax
from jax.experimental import pallas as pl
from jax.experimental.pallas import tpu as pltpu

F32 = jnp.float32
BF16 = jnp.bfloat16

D_MODEL = 1024
HEAD_DIM = 64
N_HEADS_A = 8
N_HEADS_B = 8
WIDTH_A = N_HEADS_A * HEAD_DIM
WIDTH_B = N_HEADS_B * HEAD_DIM
DILATED_PATTERNS = ((128, 1), (512, 4), (2048, 16))
ROT_DIM = HEAD_DIM // 4
ROPE_THETA = 500000.0
D_FF = 2752
D_FF_PAD = 2816
N_MOD = 9
EPS = 1e-6
ATTN_SCALE = HEAD_DIM ** -0.5
NEG = -1e30
N_DEV = 8
LANES = 128
HEADS_PER_STEP = LANES // HEAD_DIM

ADAM_LR = 0.001
ADAM_B1 = 0.9
ADAM_B2 = 0.999
ADAM_EPS = 1e-08
ADAM_WD = 0.01
ADAM_STEP = 10

VMEM_LIMIT = 56 * 1024 * 1024
MESH = pl.DeviceIdType.MESH

NT_DIMS = (((1,), (1,)), ((), ()))
TN_DIMS = (((0,), (0,)), ((), ()))
NN_DIMS = (((1,), (0,)), ((), ()))


def _place():
    return lax.axis_index("x"), lax.axis_index("y"), lax.axis_index("c")


def _slot(p):
    return 4 * p[0] + 2 * p[1] + p[2]


def _direct_copies(ins, outs, send_sems, recv_sems, local_sems, gather):
    x, y, c = _place()
    me = (x, y, c)
    flip = lambda v, bit: 1 - v if bit else v
    peers = [(flip(x, k & 4), flip(y, k & 2), flip(c, k & 1)) for k in range(1, N_DEV)]
    local, sends, recvs = [], [], []
    for a in range(len(ins)):
        mine = ins[a] if gather else ins[a].at[_slot(me)]
        local.append(pltpu.make_async_copy(mine, outs[a].at[_slot(me)], local_sems.at[a]))
        for k, peer in enumerate(peers):
            sems = dict(send_sem=send_sems.at[a * 7 + k], recv_sem=recv_sems.at[a * 7 + k], device_id=peer,
                        device_id_type=MESH)
            sends.append(pltpu.make_async_remote_copy(
                src_ref=ins[a] if gather else ins[a].at[_slot(peer)], dst_ref=outs[a].at[_slot(me)], **sems))
            recvs.append(pltpu.make_async_remote_copy(src_ref=mine, dst_ref=outs[a].at[_slot(peer)], **sems))
    return local, sends, recvs


def _comm_scratch(n):
    return [pltpu.SemaphoreType.DMA((7 * n,)), pltpu.SemaphoreType.DMA((7 * n,)), pltpu.SemaphoreType.DMA((n,))]


def _pcall(body, side=None, **kw):
    if side is None:
        return pl.pallas_call(body, **kw)
    arrs, gather = side
    n = len(arrs)
    grid = kw["grid"]
    in_specs = list(kw["in_specs"])
    single = not isinstance(kw["out_specs"], (list, tuple))
    out_specs = [kw["out_specs"]] if single else list(kw["out_specs"])
    out_shape = [kw["out_shape"]] if single else list(kw["out_shape"])
    scratch = list(kw.get("scratch_shapes", []))
    n_in, n_out, n_scr = len(in_specs), len(out_specs), len(scratch)
    hbm = pl.BlockSpec(memory_space=pl.ANY)

    def hosted(*refs):
        pos = [0]

        def take(k):
            pos[0] += k
            return refs[pos[0] - k:pos[0]]

        ins, s_ins, outs, s_outs, scr, sems = take(n_in), take(n), take(n_out), take(n), take(n_scr), take(3)
        ids = [pl.program_id(i) for i in range(len(grid))]
        first = functools.reduce(jnp.logical_and, [i == 0 for i in ids])
        last = functools.reduce(jnp.logical_and, [i == g - 1 for i, g in zip(ids, grid)])

        @pl.when(first)
        def _():
            local, sends, _ = _direct_copies(s_ins, s_outs, *sems, gather)
            for cp in local + sends:
                cp.start()

        body(*ins, *outs, *scr)

        @pl.when(last)
        def _():
            local, sends, recvs = _direct_copies(s_ins, s_outs, *sems, gather)
            for cp in recvs:
                cp.wait_recv()
            for cp in sends:
                cp.wait_send()
            for cp in local:
                cp.wait()

    kw.update(in_specs=in_specs + [hbm] * n, out_specs=out_specs + [hbm] * n,
              out_shape=out_shape + [jax.ShapeDtypeStruct(((N_DEV,) + a.shape) if gather else a.shape, a.dtype)
                                     for a in arrs],
              scratch_shapes=scratch + _comm_scratch(n))
    call = pl.pallas_call(hosted, **kw)

    def run(*args):
        res = call(*args, *arrs)
        main = res[0] if single else list(res[:n_out])
        return main, list(res[n_out:])

    return run


def _params(sem=None, **kw):
    if sem is not None:
        kw["dimension_semantics"] = sem
    return pltpu.CompilerParams(vmem_limit_bytes=VMEM_LIMIT, **kw)


def _rotate(xv, c, sp, sm, transpose):
    width = xv.shape[1]
    half = ROT_DIM // 2
    if transpose:
        return xv * c + pltpu.roll(xv * sp, width - half, 1) + pltpu.roll(xv * sm, half, 1)
    return xv * c + pltpu.roll(xv, half, 1) * sp + pltpu.roll(xv, width - half, 1) * sm


def mm_rows(pairs, trans_b, out_dtype, name, tm=512, side=None, rope=None):
    n = len(pairs)
    m = pairs[0][0].shape[0]
    n_out = pairs[0][1].shape[0 if trans_b else 1]
    dims = NT_DIMS if trans_b else NN_DIMS

    def body(*refs):
        o_ref = refs[-1]
        acc = None
        for a_ref, b_ref in zip(refs[:n], refs[n:2 * n]):
            d = lax.dot_general(a_ref[...], b_ref[...], dims, preferred_element_type=F32)
            acc = d if acc is None else acc + d
        if rope is None:
            o_ref[...] = acc.astype(o_ref.dtype)
        else:
            width = rope[1]
            c, sp, sm = (jnp.concatenate([r[...]] * (width // LANES), axis=1) for r in refs[2 * n:2 * n + 3])
            o_ref[:, :width] = _rotate(acc[:, :width], c, sp, sm, False).astype(o_ref.dtype)
            o_ref[:, width:] = acc[:, width:].astype(o_ref.dtype)

    in_specs = [pl.BlockSpec((tm, a.shape[1]), lambda i: (i, 0)) for a, _ in pairs]
    in_specs += [pl.BlockSpec(b.shape, lambda i: (0, 0)) for _, b in pairs]
    args = [a for a, _ in pairs] + [b for _, b in pairs]
    if rope is not None:
        in_specs += [pl.BlockSpec((tm, LANES), lambda i: (i, 0))] * 3
        args += list(rope[0])
    return _pcall(
        body, side=side, name=name, grid=(m // tm,), in_specs=in_specs,
        out_specs=pl.BlockSpec((tm, n_out), lambda i: (i, 0)),
        out_shape=jax.ShapeDtypeStruct((m, n_out), out_dtype),
        compiler_params=_params(("arbitrary",)),
    )(*args)


DH_ROWS = 256
TN_TOKENS = 2048
TN_OUT_ELEMS = 2 * 1024 * 1024


def mm_tn(a, b, out_dtype, name, side=None, rows=None):
    t, ka = a.shape
    n_out = b.shape[1]
    tk = min(TN_TOKENS, t)
    tka = ka // 2 if ka * n_out > TN_OUT_ELEMS else ka
    tn = n_out
    steps = t // tk

    def body(a_ref, b_ref, o_ref, acc_ref):
        k = pl.program_id(2)
        d = lax.dot_general(a_ref[...], b_ref[...], TN_DIMS, preferred_element_type=F32)

        @pl.when(k == 0)
        def _():
            acc_ref[...] = d

        @pl.when(k > 0)
        def _():
            acc_ref[...] += d

        @pl.when(k == steps - 1)
        def _():
            o_ref[...] = acc_ref[...].astype(o_ref.dtype)

    return _pcall(
        body, side=side, name=name, grid=(ka // tka, n_out // tn, steps),
        in_specs=[pl.BlockSpec((tk, tka), lambda i, j, k: (k, i)), pl.BlockSpec((tk, tn), lambda i, j, k: (k, j))],
        out_specs=pl.BlockSpec((tka, tn), lambda i, j, k: (i, j)),
        out_shape=jax.ShapeDtypeStruct((ka if rows is None else rows, n_out), out_dtype),
        scratch_shapes=[pltpu.VMEM((tka, tn), F32)],
        compiler_params=_params(("arbitrary", "arbitrary", "arbitrary")),
    )(a, b)


def _col_chunks(width, chunk=512):
    return [slice(c, min(c + chunk, width)) for c in range(0, width, chunk)]


def _sigmoid(x):
    return 1.0 / (1.0 + jnp.exp(-x))


def ffn_up(h, wgt, wut, name, tm=256, tn=D_FF_PAD, side=None):
    t, d = h.shape
    fp = wgt.shape[0]

    def body(h_ref, wg_ref, wu_ref, g_ref, u_ref, a_ref):
        hv = h_ref[...]

        def finish(cols, g, u):
            g_ref[:, cols] = g.astype(BF16)
            u_ref[:, cols] = u.astype(BF16)
            a_ref[:, cols] = (g * _sigmoid(g) * u).astype(BF16)

        pending = None
        for cols in _col_chunks(tn):
            g = lax.dot_general(hv, wg_ref[cols, :], NT_DIMS, preferred_element_type=F32)
            u = lax.dot_general(hv, wu_ref[cols, :], NT_DIMS, preferred_element_type=F32)
            if pending is not None:
                finish(*pending)
            pending = (cols, g, u)
        finish(*pending)

    w_spec = pl.BlockSpec((tn, d), lambda j, i: (j, 0))
    o_spec = pl.BlockSpec((tm, tn), lambda j, i: (i, j))
    o_shape = jax.ShapeDtypeStruct((t, fp), BF16)
    return _pcall(
        body, side=side, name=name, grid=(fp // tn, t // tm),
        in_specs=[pl.BlockSpec((tm, d), lambda j, i: (i, 0)), w_spec, w_spec],
        out_specs=[o_spec, o_spec, o_spec], out_shape=[o_shape, o_shape, o_shape],
        compiler_params=_params(("arbitrary", "arbitrary")),
    )(h, wgt, wut)


def ffn_down_bwd(dy0, wd, gate, up, name, tm=256, tn=D_FF_PAD, side=None):
    t, d = dy0.shape
    fp = wd.shape[0]

    def body(dy_ref, wd_ref, g_ref, u_ref, dg_ref, du_ref):
        dyv = dy_ref[...]

        def finish(cols, dact):
            g = g_ref[:, cols].astype(F32)
            u = u_ref[:, cols].astype(F32)
            sg = _sigmoid(g)
            silu = g * sg
            du_ref[:, cols] = (dact * silu).astype(BF16)
            dg_ref[:, cols] = ((dact * u) * (sg + silu * (1.0 - sg))).astype(BF16)

        pending = None
        for cols in _col_chunks(tn):
            dact = lax.dot_general(dyv, wd_ref[cols, :], NT_DIMS, preferred_element_type=F32)
            if pending is not None:
                finish(*pending)
            pending = (cols, dact)
        finish(*pending)

    t_spec = pl.BlockSpec((tm, tn), lambda j, i: (i, j))
    o_shape = jax.ShapeDtypeStruct((t, fp), BF16)
    return _pcall(
        body, side=side, name=name, grid=(fp // tn, t // tm),
        in_specs=[pl.BlockSpec((tm, d), lambda j, i: (i, 0)), pl.BlockSpec((tn, d), lambda j, i: (j, 0)), t_spec, t_spec],
        out_specs=[t_spec, t_spec], out_shape=[o_shape, o_shape],
        compiler_params=_params(("arbitrary", "arbitrary")),
    )(dy0, wd, gate, up)


def _row_specs(dx, ts, ns):
    return pl.BlockSpec((ts, dx), lambda b, s: (b * ns + s, 0))


def _mod_spec():
    return pl.BlockSpec((1, N_MOD, D_MODEL), lambda b, s: (b, 0, 0))


def _vec_spec(dx):
    return pl.BlockSpec((1, dx), lambda b, s: (0, 0))


def prenorm_fwd(x, g, mod, i_shift, i_scale, nb, name, ts=1024):
    t, dx = x.shape
    ts = min(ts, t // nb)
    ns = t // nb // ts

    def body(*refs):
        if mod is None:
            x_ref, g_ref, h_ref = refs
        else:
            x_ref, g_ref, mod_ref, h_ref = refs
        xv = x_ref[...]
        r = lax.rsqrt(jnp.mean(xv * xv, axis=-1, keepdims=True) + EPS)
        h = xv * r * g_ref[...]
        if mod is not None:
            h = h * (1.0 + mod_ref[0, i_scale:i_scale + 1, :]) + mod_ref[0, i_shift:i_shift + 1, :]
        h_ref[...] = h.astype(BF16)

    in_specs = [_row_specs(dx, ts, ns), _vec_spec(dx)]
    args = [x, g]
    if mod is not None:
        in_specs.append(_mod_spec())
        args.append(mod)
    return _pcall(
        body, name=name, grid=(nb, ns), in_specs=in_specs, out_specs=_row_specs(dx, ts, ns),
        out_shape=jax.ShapeDtypeStruct((t, dx), BF16), compiler_params=_params(("arbitrary", "arbitrary")),
    )(*args)


def prenorm_bwd(dh, x, g, mod, i_scale, dres, nb, name, ts=512, side=None):
    t, dx = x.shape
    ts = min(ts, t // nb)
    ns = t // nb // ts
    has_mod = mod is not None
    has_res = dres is not None
    pairs = dh if isinstance(dh, list) else None
    n_mm = 0 if pairs is None else len(pairs)

    def body(*refs):
        refs = list(refs)
        if pairs is None:
            dhv = refs[0][...].astype(F32)
            refs = refs[1:]
        else:
            dhv = None
            for a_ref, b_ref in zip(refs[:n_mm], refs[n_mm:2 * n_mm]):
                d = jnp.dot(a_ref[...], b_ref[...], preferred_element_type=F32)
                dhv = d if dhv is None else dhv + d
            refs = refs[2 * n_mm:]
        x_ref, g_ref = refs[:2]
        pos = 2
        mod_ref = dres_ref = None
        if has_mod:
            mod_ref = refs[pos]
            pos += 1
        if has_res:
            dres_ref = refs[pos]
            pos += 1
        dx_ref, dg_ref = refs[pos], refs[pos + 1]
        b, s = pl.program_id(0), pl.program_id(1)
        xv = x_ref[...]
        gv = g_ref[...]
        r = lax.rsqrt(jnp.mean(xv * xv, axis=-1, keepdims=True) + EPS)
        xhat = xv * r
        dn = dhv
        if has_mod:
            dsc_ref, dsh_ref = refs[pos + 2], refs[pos + 3]
            dn = dhv * (1.0 + mod_ref[0, i_scale:i_scale + 1, :])
            dsc = jnp.sum(dhv * xhat * gv, axis=0, keepdims=True)[None]
            dsh = jnp.sum(dhv, axis=0, keepdims=True)[None]

            @pl.when(s == 0)
            def _():
                dsc_ref[...] = dsc
                dsh_ref[...] = dsh

            @pl.when(s > 0)
            def _():
                dsc_ref[...] += dsc
                dsh_ref[...] += dsh

        dg = jnp.sum(dn * xhat, axis=0, keepdims=True)
        first = jnp.logical_and(b == 0, s == 0)

        @pl.when(first)
        def _():
            dg_ref[...] = dg

        @pl.when(jnp.logical_not(first))
        def _():
            dg_ref[...] += dg

        dxhat = dn * gv
        dxv = r * (dxhat - xhat * jnp.mean(dxhat * xhat, axis=-1, keepdims=True))
        if has_res:
            dxv = dxv + dres_ref[...]
        dx_ref[...] = dxv

    row = _row_specs(dx, ts, ns)
    if pairs is None:
        in_specs, args = [row], [dh]
    else:
        in_specs = [_row_specs(a.shape[1], ts, ns) for a, _ in pairs]
        in_specs += [pl.BlockSpec(b.shape, lambda b_, s_: (0, 0)) for _, b in pairs]
        args = [a for a, _ in pairs] + [b for _, b in pairs]
    in_specs += [row, _vec_spec(dx)]
    args += [x, g]
    if has_mod:
        in_specs.append(_mod_spec())
        args.append(mod)
    if has_res:
        in_specs.append(row)
        args.append(dres)
    out_specs = [row, _vec_spec(dx)]
    out_shape = [jax.ShapeDtypeStruct((t, dx), F32), jax.ShapeDtypeStruct((1, dx), F32)]
    if has_mod:
        bspec = pl.BlockSpec((1, 1, dx), lambda b, s: (b, 0, 0))
        out_specs += [bspec, bspec]
        out_shape += [jax.ShapeDtypeStruct((nb, 1, dx), F32)] * 2
    return _pcall(
        body, side=side, name=name, grid=(nb, ns), in_specs=in_specs, out_specs=out_specs, out_shape=out_shape,
        compiler_params=_params(("arbitrary", "arbitrary")),
    )(*args)


def postnorm_fwd(x, pairs, g, mod, i_gate, coef, nb, name, target=None, ts=512, side=None):
    t, dx = x.shape
    with_loss = target is not None
    ts = min(ts, t // nb)
    ns = t // nb // ts
    n_mm = len(pairs)

    def body(*refs):
        yv = None
        for a_ref, b_ref in zip(refs[:n_mm], refs[n_mm:2 * n_mm]):
            d = jnp.dot(a_ref[...], b_ref[...], preferred_element_type=F32)
            yv = d if yv is None else yv + d
        refs = refs[2 * n_mm:]
        x_ref, g_ref, mod_ref = refs[:3]
        refs[-1][...] = yv
        r = lax.rsqrt(jnp.mean(yv * yv, axis=-1, keepdims=True) + EPS)
        out = x_ref[...] + (coef * mod_ref[0, i_gate:i_gate + 1, :]) * (yv * r * g_ref[...])
        if not with_loss:
            refs[3][...] = out
            return
        t_ref, dx_ref, loss_ref = refs[3:6]
        b, s = pl.program_id(0), pl.program_id(1)
        err = out - t_ref[...]
        dx_ref[...] = err * (1.0 / dx)
        part = (0.5 / dx) * jnp.sum(jnp.sum(err * err, axis=1, keepdims=True), axis=0, keepdims=True)
        first = jnp.logical_and(b == 0, s == 0)

        @pl.when(first)
        def _():
            loss_ref[...] = part

        @pl.when(jnp.logical_not(first))
        def _():
            loss_ref[...] += part

    row = _row_specs(dx, ts, ns)
    in_specs = [_row_specs(a.shape[1], ts, ns) for a, _ in pairs]
    in_specs += [pl.BlockSpec(b.shape, lambda b_, s_: (0, 0)) for _, b in pairs]
    in_specs += [row, _vec_spec(dx), _mod_spec()]
    args = [a for a, _ in pairs] + [b for _, b in pairs] + [x, g, mod]
    row_shape = jax.ShapeDtypeStruct((t, dx), F32)
    out_specs, out_shape = [row, row], [row_shape, row_shape]
    if with_loss:
        in_specs.append(row)
        args.append(target)
        out_specs = [row, pl.BlockSpec((1, 1), lambda b, s: (0, 0)), row]
        out_shape = [row_shape, jax.ShapeDtypeStruct((1, 1), F32), row_shape]
    return _pcall(
        body, side=side, name=name, grid=(nb, ns), in_specs=in_specs, out_specs=out_specs, out_shape=out_shape,
        compiler_params=_params(("arbitrary", "arbitrary")),
    )(*args)


def postnorm_bwd(dxo, y0, g, mod, i_gate, coef, nb, name, ts=1024):
    t, dx = y0.shape
    ts = min(ts, t // nb)
    ns = t // nb // ts

    def body(d_ref, y_ref, g_ref, mod_ref, dy_ref, dg_ref, dgate_ref):
        b, s = pl.program_id(0), pl.program_id(1)
        yv = y_ref[...]
        dv = d_ref[...]
        gv = g_ref[...]
        r = lax.rsqrt(jnp.mean(yv * yv, axis=-1, keepdims=True) + EPS)
        yhat = yv * r
        dgate = jnp.sum(dv * (coef * (yhat * gv)), axis=0, keepdims=True)[None]
        dyn = dv * (coef * mod_ref[0, i_gate:i_gate + 1, :])
        dg = jnp.sum(dyn * yhat, axis=0, keepdims=True)
        dyhat = dyn * gv
        dy_ref[...] = (r * (dyhat - yhat * jnp.mean(dyhat * yhat, axis=-1, keepdims=True))).astype(BF16)

        @pl.when(s == 0)
        def _():
            dgate_ref[...] = dgate

        @pl.when(s > 0)
        def _():
            dgate_ref[...] += dgate

        first = jnp.logical_and(b == 0, s == 0)

        @pl.when(first)
        def _():
            dg_ref[...] = dg

        @pl.when(jnp.logical_not(first))
        def _():
            dg_ref[...] += dg

    row = _row_specs(dx, ts, ns)
    return _pcall(
        body, name=name, grid=(nb, ns), in_specs=[row, row, _vec_spec(dx), _mod_spec()],
        out_specs=[row, _vec_spec(dx), pl.BlockSpec((1, 1, dx), lambda b, s: (b, 0, 0))],
        out_shape=[jax.ShapeDtypeStruct((t, dx), BF16), jax.ShapeDtypeStruct((1, dx), F32),
                   jax.ShapeDtypeStruct((nb, 1, dx), F32)],
        compiler_params=_params(("arbitrary", "arbitrary")),
    )(dxo, y0, g, mod)


def rope_tables(positions):
    inv_freq = ROPE_THETA ** (-jnp.arange(0, ROT_DIM, 2, dtype=F32) / ROT_DIM)
    ang = positions.astype(F32).reshape(-1, 1) * inv_freq
    cos, sin = jnp.cos(ang), jnp.sin(ang)
    half = ROT_DIM // 2
    z = lambda n: jnp.zeros((ang.shape[0], n), F32)
    c = jnp.concatenate([cos, cos, jnp.ones((ang.shape[0], HEAD_DIM - ROT_DIM), F32)], axis=1)
    sp = jnp.concatenate([z(half), sin, z(HEAD_DIM - ROT_DIM)], axis=1)
    sm = jnp.concatenate([-sin, z(HEAD_DIM - half)], axis=1)
    return tuple(jnp.tile(a, (1, HEADS_PER_STEP)) for a in (c, sp, sm))


def _scan_lanes(x, reverse):
    n = x.shape[-1]
    lane = lax.broadcasted_iota(jnp.int32, x.shape, x.ndim - 1)
    k = 1
    while k < n:
        if reverse:
            x = x + jnp.where(lane < n - k, pltpu.roll(x, n - k, x.ndim - 1), 0.0)
        else:
            x = x + jnp.where(lane >= k, pltpu.roll(x, k, x.ndim - 1), 0.0)
        k *= 2
    return x


def _log_sigmoid(z):
    return jnp.minimum(z, 0.0) - jnp.log(1.0 + jnp.exp(-jnp.abs(z)))


def fox_gate_fwd(ft, b_forget, name):
    nb, nh, s = ft.shape

    def body(f_ref, b_ref, o_ref):
        z = f_ref[0] + b_ref[...]
        o_ref[0] = -_scan_lanes(_log_sigmoid(z), False)

    spec = pl.BlockSpec((1, nh, s), lambda b: (b, 0, 0))
    return _pcall(
        body, name=name, grid=(nb,), in_specs=[spec, pl.BlockSpec((nh, 1), lambda b: (0, 0))], out_specs=spec,
        out_shape=jax.ShapeDtypeStruct((nb, nh, s), F32), compiler_params=_params(("arbitrary",)),
    )(ft, b_forget)


def fox_gate_bwd(dcb, drow, ft, b_forget, name):
    nb, nh, s = ft.shape

    def body(d_ref, r_ref, f_ref, b_ref, dz_ref, db_ref):
        b = pl.program_id(0)
        z = f_ref[0] + b_ref[...]
        dlf = _scan_lanes(r_ref[0] - d_ref[0], True)
        dz = dlf * _sigmoid(-z)
        dz_ref[0] = dz
        db = jnp.sum(dz, axis=1, keepdims=True)

        @pl.when(b == 0)
        def _():
            db_ref[...] = db

        @pl.when(b > 0)
        def _():
            db_ref[...] += db

    spec = pl.BlockSpec((1, nh, s), lambda b: (b, 0, 0))
    vec = pl.BlockSpec((nh, 1), lambda b: (0, 0))
    return _pcall(
        body, name=name, grid=(nb,), in_specs=[spec, spec, spec, vec], out_specs=[spec, vec],
        out_shape=[jax.ShapeDtypeStruct((nb, nh, s), F32), jax.ShapeDtypeStruct((nh, 1), F32)],
        compiler_params=_params(("arbitrary",)),
    )(dcb, drow, ft, b_forget)


ATTN_TQ = 512
ATTN_TK = 512
ONES_ROWS = 16


def _rows_to_cols(rows):
    tile = jnp.concatenate([jnp.broadcast_to(rw, (HEAD_DIM, rw.shape[1])) for rw in rows], axis=0)
    return tile.T


def _block_delta(s, tq, tk):
    off = jnp.arange(s // tk) - (tq // tk - 1)
    return off[:, None, None] * tk + jnp.arange(tq)[None, None, :] - jnp.arange(tk)[None, :, None]


def dilated_table(s, tq, tk):
    delta = _block_delta(s, tq, tk)
    count = jnp.zeros(delta.shape, F32)
    for window, dil in DILATED_PATTERNS:
        count = count + ((delta >= 0) & (delta <= window) & (delta % dil == 0)).astype(F32)
    return jnp.where(count > 0, jnp.log(jnp.maximum(count, 1.0)), NEG)


def causal_table(s, tq, tk):
    return jnp.where(_block_delta(s, tq, tk) >= 0, 0.0, NEG).astype(F32)


def attn_fwd(q_arr, q_off, k_arr, k_off, v_arr, v_off, table, colbias, nb, name, side=None, off_diag_bias=True):
    t = q_arr.shape[0]
    s = t // nb
    tk, tq = table.shape[1:]
    assert tq == tk, "the diagonal handling below is written for square tiles"
    nq, nk = s // tq, s // tk
    npairs = WIDTH_A // LANES
    use_cb = colbias is not None

    def body(*refs):
        refs = list(refs)
        q_ref, k_ref, v_ref, tab_ref = refs[:4]
        cb_ref = refs[4] if use_cb else None
        tail = refs[-(HEADS_PER_STEP + int(use_cb)):]
        acc_s = tail[:HEADS_PER_STEP]
        cbc_s = tail[-1] if use_cb else None
        o_ref, lse_ref, vt_s = refs[-3 - len(tail):-len(tail)]
        qi = pl.program_id(2)

        heads = [slice(h * HEAD_DIM, (h + 1) * HEAD_DIM) for h in range(HEADS_PER_STEP)]

        @pl.when(qi == 0)
        def _():
            for cblk in range(nk):
                vt = v_ref[cblk * tk:(cblk + 1) * tk, :].astype(F32).T.astype(BF16)
                for h, hs in enumerate(heads):
                    vt_s[cblk, h, 0:HEAD_DIM, :] = vt[hs, :]
                    vt_s[cblk, h, HEAD_DIM:, :] = jnp.ones((ONES_ROWS, tk), BF16)
                if use_cb:
                    cbc_s[cblk] = _rows_to_cols([cb_ref[0, h, cblk] for h in range(HEADS_PER_STEP)])

        qt_all = (q_ref[...].astype(F32) * ATTN_SCALE).T.astype(BF16)
        qts = [qt_all[hs, :] for hs in heads]
        for a in acc_s:
            a[...] = jnp.zeros_like(a)

        def tile(kb, tab, k0, klen, q0, carry):
            ks = pl.multiple_of(kb * tk + k0, klen)
            sts, out = [], []
            for h, hs in enumerate(heads):
                st = jnp.dot(k_ref[pl.ds(ks, klen), hs], qts[h][:, q0:], preferred_element_type=F32)
                if tab is not None:
                    st = st + tab
                if use_cb:
                    st = st + cbc_s[kb, k0:k0 + klen, h * HEAD_DIM:h * HEAD_DIM + 1]
                sts.append(st)
            m_old = [carry[h][:, q0:] for h in range(HEADS_PER_STEP)]
            m_new = [jnp.maximum(m_old[h], jnp.max(sts[h], axis=0, keepdims=True)) for h in range(HEADS_PER_STEP)]
            for h in range(HEADS_PER_STEP):
                pt = jnp.exp(sts[h] - m_new[h]).astype(BF16)
                acc_s[h][:, q0:] = (jnp.exp(m_old[h] - m_new[h]) * acc_s[h][:, q0:]
                                    + jnp.dot(vt_s[kb, h, :, k0:k0 + klen], pt, preferred_element_type=F32))
                out.append(m_new[h] if q0 == 0 else jnp.concatenate([carry[h][:, :q0], m_new[h]], axis=1))
            return tuple(out)

        fin = lax.fori_loop(0, qi, lambda kb, c: tile(kb, tab_ref[qi - kb] if off_diag_bias else None, 0, tk, 0, c),
                            tuple(jnp.full((1, tq), NEG, F32) for _ in heads))
        half = tk // 2
        fin = tile(qi, tab_ref[0, 0:half, :], 0, half, 0, fin)
        fin = tile(qi, tab_ref[0, half:, half:], half, half, half, fin)
        outs = []
        for h in range(HEADS_PER_STEP):
            l = acc_s[h][HEAD_DIM:HEAD_DIM + 1, :]
            outs.append(acc_s[h][0:HEAD_DIM, :] / l)
            lse_ref[0, h, 0] = fin[h] + jnp.log(l)
        o_ref[...] = jnp.concatenate(outs, axis=0).T

    def seq_spec(off):
        return pl.BlockSpec((s, LANES), lambda b, j, i: (b, off + j))

    in_specs = [pl.BlockSpec((tq, LANES), lambda b, j, i: (b * nq + i, q_off + j)), seq_spec(k_off), seq_spec(v_off),
                pl.BlockSpec(table.shape, lambda b, j, i: (0, 0, 0))]
    args = [q_arr, k_arr, v_arr, table]
    if use_cb:
        in_specs.append(pl.BlockSpec((1, HEADS_PER_STEP, nk, 1, tk), lambda b, j, i: (b, j, 0, 0, 0)))
        args.append(colbias)
    n_heads = npairs * HEADS_PER_STEP
    return _pcall(
        body, side=side, name=name, grid=(nb, npairs, nq), in_specs=in_specs,
        out_specs=[pl.BlockSpec((tq, LANES), lambda b, j, i: (b * nq + i, j)),
                   pl.BlockSpec((1, HEADS_PER_STEP, 1, 1, tq), lambda b, j, i: (b, j, i, 0, 0))],
        out_shape=[jax.ShapeDtypeStruct((t, npairs * LANES), F32), jax.ShapeDtypeStruct((nb, n_heads, nq, 1, tq), F32)],
        scratch_shapes=[pltpu.VMEM((nk, HEADS_PER_STEP, HEAD_DIM + ONES_ROWS, tk), BF16)]
        + [pltpu.VMEM((HEAD_DIM + ONES_ROWS, tq), F32)] * HEADS_PER_STEP
        + ([pltpu.VMEM((nk, tk, LANES), F32)] if use_cb else []),
        compiler_params=_params(("arbitrary", "arbitrary", "arbitrary")),
    )(*args)


def attn_bwd(q_arr, q_off, k_arr, k_off, v_arr, v_off, o_arr, lse_arr, do_arr, table, colbias, nb, name, side=None,
             rope_tabs=None, off_diag_bias=True):
    t = q_arr.shape[0]
    s = t // nb
    tk, tq = table.shape[1:]
    assert tq == tk, "the diagonal handling below is written for square tiles"
    nq, nk = s // tq, s // tk
    npairs = WIDTH_A // LANES
    use_cb = colbias is not None

    def body(*refs):
        refs = list(refs)
        q_ref, k_ref, v_ref, o_ref, lse_ref, do_ref, tab_ref = refs[:7]
        pos = 7
        cb_ref = None
        if use_cb:
            cb_ref = refs[pos]
            pos += 1
        rope_refs = None
        if rope_tabs is not None:
            rope_refs = refs[pos:pos + 3]
            pos += 3
        dq_ref, dk_ref, dv_ref = refs[pos:pos + 3]
        pos += 3
        dcb_ref = drow_ref = None
        if use_cb:
            dcb_ref, drow_ref = refs[pos:pos + 2]
            pos += 2
        kt_s, dkt_s, dvt_s = refs[pos:pos + 3]
        dqt_s = refs[pos + 3:pos + 3 + HEADS_PER_STEP]
        dcb_s, cbc_s = refs[pos + 3 + HEADS_PER_STEP:pos + 5 + HEADS_PER_STEP] if use_cb else (None, None)

        heads = [slice(h * HEAD_DIM, (h + 1) * HEAD_DIM) for h in range(HEADS_PER_STEP)]
        for cblk in range(nk):
            kt_s[cblk] = k_ref[cblk * tk:(cblk + 1) * tk, :].astype(F32).T.astype(BF16)
        dkt_s[...] = jnp.zeros_like(dkt_s)
        dvt_s[...] = jnp.zeros_like(dvt_s)
        if use_cb:
            dcb_s[...] = jnp.zeros_like(dcb_s)
            for cblk in range(nk):
                cbc_s[cblk] = _rows_to_cols([cb_ref[0, h, cblk] for h in range(HEADS_PER_STEP)])
        ones = jnp.ones((8, HEAD_DIM), BF16)

        def q_loop(qi, carry):
            qs = pl.multiple_of(qi * tq, tq)
            q_all = (q_ref[pl.ds(qs, tq), :].astype(F32) * ATTN_SCALE)
            do_all = do_ref[pl.ds(qs, tq), :]
            qt_all = q_all.T.astype(BF16)
            dot_all = do_all.T.astype(BF16)
            qt, dot, lse, dsum = [], [], [], []
            for h, hs in enumerate(heads):
                qt.append(qt_all[hs, :])
                dot.append(dot_all[hs, :])
                lse.append(lse_ref[0, h, qi])
                prod = do_all[:, hs] * o_ref[pl.ds(qs, tq), hs]
                hi = prod.astype(BF16)
                lo = (prod - hi.astype(F32)).astype(BF16)
                dsum.append((lax.dot_general(ones, hi, NT_DIMS, preferred_element_type=F32)
                             + lax.dot_general(ones, lo, NT_DIMS, preferred_element_type=F32))[0:1, :])
            for a in dqt_s:
                a[...] = jnp.zeros_like(a)

            def tile(kb, tab, k0, klen, q0, drow):
                ks = pl.multiple_of(kb * tk + k0, klen)
                keys = slice(k0, k0 + klen)
                sts, dpts, out = [], [], []
                for h, hs in enumerate(heads):
                    st = jnp.dot(k_ref[pl.ds(ks, klen), hs], qt[h][:, q0:], preferred_element_type=F32)
                    if tab is not None:
                        st = st + tab
                    if use_cb:
                        st = st + cbc_s[kb, keys, h * HEAD_DIM:h * HEAD_DIM + 1]
                    sts.append(st)
                    dpts.append(jnp.dot(v_ref[pl.ds(ks, klen), hs], dot[h][:, q0:], preferred_element_type=F32))
                for h, hs in enumerate(heads):
                    pt = jnp.exp(sts[h] - lse[h][:, q0:])
                    dst = pt * (dpts[h] - dsum[h][:, q0:])
                    dst_b = dst.astype(BF16)
                    dvt_s[h, kb, :, keys] += lax.dot_general(dot[h][:, q0:], pt.astype(BF16), NT_DIMS,
                                                             preferred_element_type=F32)
                    dkt_s[h, kb, :, keys] += lax.dot_general(qt[h][:, q0:], dst_b, NT_DIMS, preferred_element_type=F32)
                    dqt_s[h][:, q0:] += jnp.dot(kt_s[kb, hs, keys], dst_b, preferred_element_type=F32)
                    if use_cb:
                        dcb_s[h, pl.ds(ks, klen), :] += jnp.sum(dst, axis=1, keepdims=True)
                        dr = drow[h][:, q0:] + jnp.sum(dst, axis=0, keepdims=True)
                        out.append(dr if q0 == 0 else jnp.concatenate([drow[h][:, :q0], dr], axis=1))
                    else:
                        out.append(drow[h])
                return tuple(out)

            drow = lax.fori_loop(0, qi, lambda kb, c: tile(kb, tab_ref[qi - kb] if off_diag_bias else None, 0, tk, 0, c),
                                 tuple(jnp.zeros((1, tq), F32) for _ in heads))
            half = tk // 2
            drow = tile(qi, tab_ref[0, 0:half, :], 0, half, 0, drow)
            drow = tile(qi, tab_ref[0, half:, half:], half, half, half, drow)
            dq = (jnp.concatenate([a[...] for a in dqt_s], axis=0) * ATTN_SCALE).T
            if rope_refs is not None:
                dq = _rotate(dq, *[coef[pl.ds(qs, tq), :] for coef in rope_refs], True)
            dq_ref[pl.ds(qs, tq), :] = dq.astype(dq_ref.dtype)
            if use_cb:
                for h in range(HEADS_PER_STEP):
                    drow_ref[0, h, qi] = drow[h]
            return carry

        lax.fori_loop(0, nq, q_loop, 0)
        for cblk in range(nk):
            rows = slice(cblk * tk, (cblk + 1) * tk)
            dk = jnp.concatenate([dkt_s[h, cblk] for h in range(HEADS_PER_STEP)], axis=0).T
            if rope_refs is not None:
                dk = _rotate(dk, *[coef[rows, :] for coef in rope_refs], True)
            dk_ref[rows, :] = dk.astype(dk_ref.dtype)
            dv_ref[rows, :] = jnp.concatenate([dvt_s[h, cblk] for h in range(HEADS_PER_STEP)], axis=0).T.astype(dv_ref.dtype)
            if use_cb:
                for h in range(HEADS_PER_STEP):
                    dcb_ref[0, h, cblk] = jnp.broadcast_to(dcb_s[h, rows, :], (tk, LANES)).T[0:1, :]

    def seq_spec(off):
        return pl.BlockSpec((s, LANES), lambda b, j: (b, off + j))

    row_spec = pl.BlockSpec((1, HEADS_PER_STEP, nq, 1, tq), lambda b, j: (b, j, 0, 0, 0))
    in_specs = [seq_spec(q_off), seq_spec(k_off), seq_spec(v_off), seq_spec(0), row_spec, seq_spec(0),
                pl.BlockSpec(table.shape, lambda b, j: (0, 0, 0))]
    args = [q_arr, k_arr, v_arr, o_arr, lse_arr, do_arr, table]
    width = npairs * LANES
    out_specs = [seq_spec(0)] * 3
    out_shape = [jax.ShapeDtypeStruct((t, width), BF16)] * 3
    scratch = [pltpu.VMEM((nk, LANES, tk), BF16), pltpu.VMEM((HEADS_PER_STEP, nk, HEAD_DIM, tk), F32),
               pltpu.VMEM((HEADS_PER_STEP, nk, HEAD_DIM, tk), F32)] + [pltpu.VMEM((HEAD_DIM, tq), F32)] * HEADS_PER_STEP
    if use_cb:
        cb_spec = pl.BlockSpec((1, HEADS_PER_STEP, nk, 1, tk), lambda b, j: (b, j, 0, 0, 0))
        in_specs.append(cb_spec)
        args.append(colbias)
    if rope_tabs is not None:
        in_specs += [pl.BlockSpec((s, LANES), lambda b, j: (b, 0))] * 3
        args += list(rope_tabs)
    if use_cb:
        out_specs += [cb_spec, row_spec]
        out_shape += [jax.ShapeDtypeStruct(colbias.shape, F32), jax.ShapeDtypeStruct(lse_arr.shape, F32)]
        scratch += [pltpu.VMEM((HEADS_PER_STEP, s, 1), F32), pltpu.VMEM((nk, tk, LANES), F32)]
    return _pcall(
        body, side=side, name=name, grid=(nb, npairs), in_specs=in_specs, out_specs=out_specs, out_shape=out_shape,
        scratch_shapes=scratch, compiler_params=_params(("arbitrary", "arbitrary")),
    )(*args)


def ada_fwd(c_all, w_ada, b_cols, name):
    def body(c_ref, w_ref, b_ref, o_ref):
        cv = c_ref[...]
        sc = (cv * _sigmoid(cv)).astype(BF16)
        o_ref[...] = jnp.dot(sc, w_ref[...].astype(BF16), preferred_element_type=F32) + b_ref[...]

    return _pcall(body, name=name, out_shape=jax.ShapeDtypeStruct((c_all.shape[0], w_ada.shape[1]), F32),
                  compiler_params=_params())(c_all, w_ada, b_cols)


def ada_bwd(c_all, dmod_cols, name):
    def body(c_ref, d_ref, o_ref):
        cv = c_ref[...]
        sc = (cv * _sigmoid(cv)).astype(BF16)
        o_ref[...] = lax.dot_general(sc, d_ref[...].astype(BF16), TN_DIMS, preferred_element_type=F32)

    return _pcall(body, name=name, out_shape=jax.ShapeDtypeStruct((c_all.shape[1], dmod_cols.shape[1]), F32),
                  compiler_params=_params())(c_all, dmod_cols)


def adamw(parts, group, w, m, v, name, tr=None):
    n = parts.shape[0]
    r, c = w.shape
    tr = r if tr is None else tr
    c1 = 1.0 - ADAM_B1 ** ADAM_STEP
    c2 = 1.0 - ADAM_B2 ** ADAM_STEP

    def body(p_ref, w_ref, m_ref, v_ref, g_ref, d_ref, nm_ref, nv_ref):
        g = p_ref[0, 0].astype(F32)
        for i in range(1, n):
            g = g + p_ref[i, 0].astype(F32)
        wv = w_ref[...]
        nm = ADAM_B1 * m_ref[...] + (1.0 - ADAM_B1) * g
        nv = ADAM_B2 * v_ref[...] + (1.0 - ADAM_B2) * (g * g)
        g_ref[...] = g
        nm_ref[...] = nm
        nv_ref[...] = nv
        d_ref[...] = -ADAM_LR * ((nm / c1) / (jnp.sqrt(nv / c2) + ADAM_EPS) + ADAM_WD * wv)

    spec = pl.BlockSpec((tr, c), lambda i: (i, 0))
    shape = jax.ShapeDtypeStruct((r, c), F32)
    return _pcall(
        body, name=name, grid=(r // tr,),
        in_specs=[pl.BlockSpec((n, 1, tr, c), lambda i: (0, group, i, 0)), spec, spec, spec],
        out_specs=[spec] * 4, out_shape=[shape] * 4, compiler_params=_params(("arbitrary",)),
    )(parts, w, m, v)


def all_gather(arrs, name):
    n = len(arrs)
    hbm = pl.BlockSpec(memory_space=pl.ANY)

    def body(*refs):
        ins, outs = refs[:n], refs[n:2 * n]
        send_sems, recv_sems, local_sems = refs[2 * n:]
        x, y, c = _place()
        me, sibling = (x, y, c), (x, y, 1 - c)
        chips = [(1 - x, y), (x, 1 - y), (1 - x, 1 - y)]

        def copy(a, k, block, to, src=None):
            dst = outs[a].at[_slot(block)]
            return pltpu.make_async_remote_copy(
                src_ref=dst if src is None else src, dst_ref=dst, send_sem=send_sems.at[a * 7 + k],
                recv_sem=recv_sems.at[a * 7 + k], device_id=to, device_id_type=MESH)

        mine = [pltpu.make_async_copy(ins[a], outs[a].at[_slot(me)], local_sems.at[a]) for a in range(n)]
        for cp in mine:
            cp.start()
        first = []
        for a in range(n):
            first.append(copy(a, 0, me, sibling, src=ins[a]))
            first += [copy(a, 1 + j, me, (*chip, c), src=ins[a]) for j, chip in enumerate(chips)]
        for cp in first:
            cp.start()
        passed = []
        for a in range(n):
            for j, chip in enumerate(chips):
                copy(a, 1 + j, (*chip, c), me).wait_recv()
                cp = copy(a, 4 + j, (*chip, c), sibling)
                cp.start()
                passed.append(cp)
        for a in range(n):
            copy(a, 0, sibling, me).wait_recv()
            for j, chip in enumerate(chips):
                copy(a, 4 + j, (*chip, 1 - c), me).wait_recv()
        for cp in first + passed:
            cp.wait_send()
        for cp in mine:
            cp.wait()

    return _pcall(
        body, name=name, in_specs=[hbm] * n, out_specs=[hbm] * n,
        out_shape=[jax.ShapeDtypeStruct((N_DEV,) + a.shape, a.dtype) for a in arrs],
        scratch_shapes=[pltpu.SemaphoreType.DMA((7 * n,)), pltpu.SemaphoreType.DMA((7 * n,)),
                        pltpu.SemaphoreType.DMA((n,))],
        compiler_params=pltpu.CompilerParams(has_side_effects=True),
    )(*arrs)


def _t(w):
    return jnp.swapaxes(w, -1, -2)


def _rows_from_blocks(blocks, pad_to=None):
    full = blocks.reshape(-1, blocks.shape[2])
    if pad_to is not None and pad_to > full.shape[0]:
        full = jnp.pad(full, ((0, pad_to - full.shape[0]), (0, 0)))
    return full


def _rows_to_blocks(full, nrows):
    return full[:nrows].reshape(N_DEV, nrows // N_DEV, full.shape[1])


SMALL_ORDER = ("g_pre_ff1", "g_post_ff1", "g_pre_mix", "g_post_mix", "g_out_a", "g_out_b", "g_pre_ff2", "g_post_ff2",
               "b_forget")


def _pack_small(vals):
    rows = []
    for name in SMALL_ORDER:
        v = vals[name].reshape(1, -1)
        if v.shape[1] % LANES:
            v = jnp.pad(v, ((0, 0), (0, LANES - v.shape[1] % LANES)))
        rows.append(v)
    return jnp.concatenate(rows, axis=1)


def _unpack_small(row, sizes):
    out, pos = {}, 0
    for name in SMALL_ORDER:
        n = sizes[name]
        out[name] = row[:, pos:pos + n]
        pos += -(-n // LANES) * LANES
    return out


def _ffn_forward(x, mod, g_pre, g_post, wg, wu, wd, i0, nb, tag, target=None, side=None, side_down=None):
    h = prenorm_fwd(x, g_pre, mod, i0, i0 + 1, nb, f"{tag}_prenorm")
    res, side_out = ffn_up(h, wg, wu, f"{tag}_up", side=side), None
    if side is not None:
        res, side_out = res
    gate, up, act = res
    if callable(wd):
        wd = wd(side_out)
    res, side_down_out = postnorm_fwd(x, [(act, wd)], g_post, mod, i0 + 2, 0.5, nb, f"{tag}_down_postnorm",
                                      target=target, side=side_down), None
    if side_down is not None:
        res, side_down_out = res
    out, y0 = (res[0] if target is None else tuple(res[:2])), res[-1]
    return out, (x, h, gate, up, act, y0), wd, side_out, side_down_out


def _ffn_backward(dxo, saved, mod, g_pre, g_post, wg, wu, wd, i0, nb, tag, side=None, chain=False):
    x, h, gate, up, act, y0 = saved
    dy0, dg_post, dgate_mod = postnorm_bwd(dxo, y0, g_post, mod, i0 + 2, 0.5, nb, f"{tag}_postnorm_bwd")
    dwd = mm_tn(act, dy0, BF16, f"{tag}_dwd", rows=D_FF)
    res, side_out = ffn_down_bwd(dy0, wd, gate, up, f"{tag}_down_bwd", side=side), None
    if side is not None:
        res, side_out = res
    dgate, dup = res
    dh_pairs = [(dgate, wg), (dup, wu)]
    if chain:
        dwg, (dwd,) = mm_tn(dgate, h, BF16, f"{tag}_dwg", rows=D_FF, side=([_rows_to_blocks(dwd, D_FF)[:, None]], False))
        dwu, (dwg,) = mm_tn(dup, h, BF16, f"{tag}_dwu", rows=D_FF, side=([_rows_to_blocks(dwg, D_FF)[:, None]], False))
        (dx, dg_pre, dsc, dsh), (dwu,) = prenorm_bwd(dh_pairs, x, g_pre, mod, i0 + 1, dxo, nb, f"{tag}_dh_prenorm_bwd",
                                                     ts=DH_ROWS, side=([_rows_to_blocks(dwu, D_FF)[:, None]], False))
    else:
        dwg = mm_tn(dgate, h, BF16, f"{tag}_dwg", rows=D_FF)
        dwu = mm_tn(dup, h, BF16, f"{tag}_dwu", rows=D_FF)
        dx, dg_pre, dsc, dsh = prenorm_bwd(dh_pairs, x, g_pre, mod, i0 + 1, dxo, nb, f"{tag}_dh_prenorm_bwd", ts=DH_ROWS)
    return dx, dict(g_pre=dg_pre, g_post=dg_post, wg=dwg, wu=dwu, wd=dwd, mod=(dsh, dsc, dgate_mod)), side_out


def kernel(x, c, positions, w_ada, b_ada, g_pre_ff1, g_post_ff1, w_ff1_gate, w_ff1_up, w_ff1_down, g_pre_mix, g_post_mix, w_in, b_forget, g_out_a, g_out_b, w_out, g_pre_ff2, g_post_ff2, w_ff2_gate, w_ff2_up, w_ff2_down, loss_target, m_w_ada, m_b_ada, m_g_pre_ff1, m_g_post_ff1, m_w_ff1_gate, m_w_ff1_up, m_w_ff1_down, m_g_pre_mix, m_g_post_mix, m_w_in, m_b_forget, m_g_out_a, m_g_out_b, m_w_out, m_g_pre_ff2, m_g_post_ff2, m_w_ff2_gate, m_w_ff2_up, m_w_ff2_down, v_w_ada, v_b_ada, v_g_pre_ff1, v_g_post_ff1, v_w_ff1_gate, v_w_ff1_up, v_w_ff1_down, v_g_pre_mix, v_g_post_mix, v_w_in, v_b_forget, v_g_out_a, v_g_out_b, v_w_out, v_g_pre_ff2, v_g_post_ff2, v_w_ff2_gate, v_w_ff2_up, v_w_ff2_down):
    weights = dict(w_ada=w_ada, b_ada=b_ada, g_pre_ff1=g_pre_ff1, g_post_ff1=g_post_ff1, w_ff1_gate=w_ff1_gate,
                   w_ff1_up=w_ff1_up, w_ff1_down=w_ff1_down, g_pre_mix=g_pre_mix, g_post_mix=g_post_mix, w_in=w_in,
                   b_forget=b_forget, g_out_a=g_out_a, g_out_b=g_out_b, w_out=w_out, g_pre_ff2=g_pre_ff2,
                   g_post_ff2=g_post_ff2, w_ff2_gate=w_ff2_gate, w_ff2_up=w_ff2_up, w_ff2_down=w_ff2_down)
    mom_m = dict(w_ada=m_w_ada, b_ada=m_b_ada, g_pre_ff1=m_g_pre_ff1, g_post_ff1=m_g_post_ff1, w_ff1_gate=m_w_ff1_gate,
                 w_ff1_up=m_w_ff1_up, w_ff1_down=m_w_ff1_down, g_pre_mix=m_g_pre_mix, g_post_mix=m_g_post_mix,
                 w_in=m_w_in, b_forget=m_b_forget, g_out_a=m_g_out_a, g_out_b=m_g_out_b, w_out=m_w_out,
                 g_pre_ff2=m_g_pre_ff2, g_post_ff2=m_g_post_ff2, w_ff2_gate=m_w_ff2_gate, w_ff2_up=m_w_ff2_up,
                 w_ff2_down=m_w_ff2_down)
    mom_v = dict(w_ada=v_w_ada, b_ada=v_b_ada, g_pre_ff1=v_g_pre_ff1, g_post_ff1=v_g_post_ff1, w_ff1_gate=v_w_ff1_gate,
                 w_ff1_up=v_w_ff1_up, w_ff1_down=v_w_ff1_down, g_pre_mix=v_g_pre_mix, g_post_mix=v_g_post_mix,
                 w_in=v_w_in, b_forget=v_b_forget, g_out_a=v_g_out_a, g_out_b=v_g_out_b, w_out=v_w_out,
                 g_pre_ff2=v_g_pre_ff2, g_post_ff2=v_g_post_ff2, w_ff2_gate=v_w_ff2_gate, w_ff2_up=v_w_ff2_up,
                 w_ff2_down=v_w_ff2_down)
    order = list(weights)

    nb, s, d = x.shape
    t = nb * s
    me = _slot(_place())
    nbg = nb * N_DEV
    ada_cols = w_ada.shape[2]

    bf = lambda w: w[0].astype(BF16)
    bft = lambda w: _t(w)[0].astype(BF16)
    c_all, wg1, wu1 = all_gather([c, bft(w_ff1_gate), bft(w_ff1_up)], "gather_ff1")
    c_all = c_all.reshape(nbg, d)
    wg1, wu1 = (_rows_from_blocks(w, D_FF_PAD) for w in (wg1, wu1))

    b_cols = lax.dynamic_slice(b_ada, (0, me * ada_cols), (1, ada_cols))
    mod_cols = ada_fwd(c_all, w_ada[0], b_cols, "ada_fwd")
    (mod_all,) = all_gather([mod_cols], "gather_mod")
    mod = lax.dynamic_slice(mod_all, (0, me * nb, 0), (N_DEV, nb, ada_cols))
    mod = mod.transpose(1, 0, 2).reshape(nb, N_MOD, d)

    xf = x.reshape(t, d)
    target = loss_target.reshape(t, d)

    x1, saved1, wd1, (_, w_out_all), (w_in_all,) = _ffn_forward(
        xf, mod, g_pre_ff1, g_post_ff1, wg1, wu1, lambda got: _rows_from_blocks(got[0], D_FF_PAD), 0, nb, "ff1",
        side=([bf(w_ff1_down), bf(w_out)], True), side_down=([bft(w_in)], True))
    w_in_t = _rows_from_blocks(w_in_all)
    n_qkv = 3 * (WIDTH_A + WIDTH_B)
    w_qkv_t = w_in_t[:n_qkv]
    w_f_t = jnp.pad(w_in_t[n_qkv:], ((0, LANES - N_HEADS_B), (0, 0)))
    w_o = _rows_from_blocks(w_out_all)
    w_o_a, w_o_b = w_o[:WIDTH_A], w_o[WIDTH_A:]

    h2 = prenorm_fwd(x1, g_pre_mix, mod, 3, 4, nb, "mix_prenorm")
    tables = rope_tables(positions)
    proj = mm_rows([(h2, w_qkv_t)], True, BF16, "mix_proj", rope=(tables, 2 * WIDTH_A))
    f_logit = mm_rows([(h2, w_f_t)], True, F32, "mix_forget")
    tab_a = dilated_table(s, ATTN_TQ, ATTN_TK)
    tab_b = causal_table(s, ATTN_TQ, ATTN_TK)
    ft = f_logit[:, :N_HEADS_B].reshape(nb, s, N_HEADS_B).transpose(0, 2, 1)
    bf_col = b_forget.reshape(N_HEADS_B, 1)
    colbias = fox_gate_fwd(ft, bf_col, "fox_gate").reshape(nb, N_HEADS_B, s // ATTN_TK, 1, ATTN_TK)
    pa = WIDTH_A // LANES
    (o_a, lse_a), ff2_all = attn_fwd(
        proj, 0, proj, pa, proj, 2 * pa, tab_a, None, nb, "attn_a",
        side=([bft(w_ff2_gate), bft(w_ff2_up), bf(w_ff2_down)], True))
    wg2, wu2, wd2 = (_rows_from_blocks(w, D_FF_PAD) for w in ff2_all)
    o_b, lse_b = attn_fwd(proj, 3 * pa, proj, 4 * pa, proj, 5 * pa, tab_b, colbias, nb, "attn_b", off_diag_bias=False)
    m_a = prenorm_fwd(o_a, g_out_a, None, None, None, nb, "out_norm_a")
    m_b = prenorm_fwd(o_b, g_out_b, None, None, None, nb, "out_norm_b")
    x2, y0m = postnorm_fwd(x1, [(m_a, w_o_a), (m_b, w_o_b)], g_post_mix, mod, 5, 1.0, nb, "mix_out_postnorm")

    (dx3, loss_part), saved2 = _ffn_forward(x2, mod, g_pre_ff2, g_post_ff2, wg2, wu2, wd2, 6, nb, "ff2", target=target)[:2]
    loss = lax.psum(loss_part[0, 0], ("x", "y", "c"))

    dx2, gr2, _ = _ffn_backward(dx3, saved2, mod, g_pre_ff2, g_post_ff2, wg2, wu2, wd2, 6, nb, "ff2")
    ff2_blocks = [_rows_to_blocks(gr2[k], D_FF)[:, None] for k in ("wg", "wu", "wd")]

    dy0m, dg_post_mix, dgate_mix = postnorm_bwd(dx2, y0m, g_post_mix, mod, 5, 1.0, nb, "mix_postnorm_bwd")
    dw_o_a = mm_tn(m_a, dy0m, BF16, "mix_dwo_a")
    dw_o_b = mm_tn(m_b, dy0m, BF16, "mix_dwo_b")
    do_a, dg_out_a = prenorm_bwd([(dy0m, w_o_a.T)], o_a, g_out_a, None, None, None, nb, "out_norm_a_bwd")
    do_b, dg_out_b = prenorm_bwd([(dy0m, w_o_b.T)], o_b, g_out_b, None, None, None, nb, "out_norm_b_bwd")
    (dq_a, dk_a, dv_a), g_ff2 = attn_bwd(proj, 0, proj, pa, proj, 2 * pa, o_a, lse_a, do_a, tab_a, None, nb,
                                            "attn_a_bwd", side=(ff2_blocks, False), rope_tabs=tables)
    dq_b, dk_b, dv_b, dcb, drow = attn_bwd(proj, 3 * pa, proj, 4 * pa, proj, 5 * pa, o_b, lse_b, do_b, tab_b, colbias, nb,
                                           "attn_b_bwd", off_diag_bias=False)
    dz_t, db_forget = fox_gate_bwd(dcb.reshape(nb, N_HEADS_B, s), drow.reshape(nb, N_HEADS_B, s), ft, bf_col,
                                   "fox_gate_bwd")
    dz = jnp.pad(dz_t.transpose(0, 2, 1).reshape(t, N_HEADS_B), ((0, 0), (0, LANES - N_HEADS_B))).astype(BF16)
    pieces = [dq_a, dk_a, dv_a, dq_b, dk_b, dv_b]
    w_pieces = [w_qkv_t[i * WIDTH_A:(i + 1) * WIDTH_A] for i in range(6)]
    dh2_pairs = list(zip(pieces, w_pieces)) + [(dz, w_f_t)]
    dw_in_t = jnp.concatenate([mm_tn(p, h2, BF16, f"mix_dwin_{i}") for i, p in enumerate(pieces)]
                              + [mm_tn(dz, h2, BF16, "mix_dwin_f")[:N_HEADS_B]], axis=0)
    dx1, dg_pre_mix, dsc_mix, dsh_mix = prenorm_bwd(dh2_pairs, x1, g_pre_mix, mod, 4, dx2, nb, "mix_dh_prenorm_bwd",
                                                    ts=DH_ROWS)

    g_in = _rows_to_blocks(dw_in_t, dw_in_t.shape[0])[:, None]
    g_out = _rows_to_blocks(jnp.concatenate([dw_o_a, dw_o_b], axis=0), d)[:, None]
    dx0, gr1, (g_in, g_out) = _ffn_backward(dx1, saved1, mod, g_pre_ff1, g_post_ff1, wg1, wu1, wd1, 0, nb, "ff1",
                                            side=([g_in, g_out], False), chain=True)
    grad_x = dx0.reshape(nb, s, d)

    dmod =jnp.concatenate(list(gr1["mod"]) + [dsh_mix, dsc_mix, dgate_mix] + list(gr2["mod"]), axis=1)
    small = _pack_small(dict(g_pre_ff1=gr1["g_pre"], g_post_ff1=gr1["g_post"], g_pre_mix=dg_pre_mix,
                             g_post_mix=dg_post_mix, g_out_a=dg_out_a, g_out_b=dg_out_b, g_pre_ff2=gr2["g_pre"],
                             g_post_ff2=gr2["g_post"], b_forget=db_forget))
    dmod_all, small_all = all_gather([dmod.reshape(nb, N_MOD * d), small], "gather_small_grads")
    dmod_all = dmod_all.reshape(nbg, N_MOD * d)

    res = {}
    def adamw_t(parts, group, n):
        return tuple(_t(r) for r in adamw(parts, group, _t(weights[n])[0], _t(mom_m[n])[0], _t(mom_v[n])[0], f"adamw_{n}"))

    res["w_ff1_gate"] = adamw_t(gr1["wg"], 0, "w_ff1_gate")
    res["w_ff1_up"] = adamw_t(gr1["wu"], 0, "w_ff1_up")
    res["w_ff2_gate"] = adamw_t(g_ff2[0], 0, "w_ff2_gate")
    res["w_ff2_up"] = adamw_t(g_ff2[1], 0, "w_ff2_up")
    res["w_ff1_down"] = adamw(gr1["wd"], 0, w_ff1_down[0], m_w_ff1_down[0], v_w_ff1_down[0], "adamw_ff1_down")
    res["w_ff2_down"] = adamw(g_ff2[2], 0, w_ff2_down[0], m_w_ff2_down[0], v_w_ff2_down[0], "adamw_ff2_down")
    res["w_in"] = adamw_t(g_in, 0, "w_in")
    res["w_out"] = adamw(g_out, 0, w_out[0], m_w_out[0], v_w_out[0], "adamw_out")
    dmod_cols = lax.dynamic_slice(dmod_all, (0, me * ada_cols), (nbg, ada_cols))
    dw_ada = ada_bwd(c_all, dmod_cols, "ada_bwd")
    res["w_ada"] = adamw(dw_ada[None, None], 0, w_ada[0], m_w_ada[0], v_w_ada[0], "adamw_ada", tr=256)
    res["b_ada"] = adamw(dmod_all[:, None, None], 0, b_ada, m_b_ada, v_b_ada, "adamw_b_ada")
    sizes = {n: weights[n].shape[1] for n in SMALL_ORDER}
    small_res = adamw(small_all[:, None], 0, _pack_small(weights), _pack_small(mom_m), _pack_small(mom_v), "adamw_small")
    small_res = [_unpack_small(r, sizes) for r in small_res]
    for n in SMALL_ORDER:
        res[n] = tuple(r[n] for r in small_res)

    outs = [loss, grad_x]
    for kind in range(4):
        for n in order:
            a = res[n][kind]
            outs.append(a.reshape(weights[n].shape))
    return tuple(outs)
```

```python
import functools

import jax
import jax.numpy as jnp
from jax import lax
from jax.experimental import pallas as pl
from jax.experimental.pallas import tpu as pltpu

F32 = jnp.float32
BF16 = jnp.bfloat16

D_MODEL = 1024
HEAD_DIM = 64
N_HEADS_A = 8
N_HEADS_B = 8
WIDTH_A = N_HEADS_A * HEAD_DIM
WIDTH_B = N_HEADS_B * HEAD_DIM
DILATED_PATTERNS = ((128, 1), (512, 4), (2048, 16))
ROT_DIM = HEAD_DIM // 4
ROPE_THETA = 500000.0
D_FF = 2752
D_FF_PAD = 2816
N_MOD = 9
EPS = 1e-6
ATTN_SCALE = HEAD_DIM ** -0.5
NEG = -1e30
N_DEV = 8
LANES = 128
HEADS_PER_STEP = LANES // HEAD_DIM

ADAM_LR = 0.001
ADAM_B1 = 0.9
ADAM_B2 = 0.999
ADAM_EPS = 1e-08
ADAM_WD = 0.01
ADAM_STEP = 10

VMEM_LIMIT = 56 * 1024 * 1024
MESH = pl.DeviceIdType.MESH

NT_DIMS = (((1,), (1,)), ((), ()))
TN_DIMS = (((0,), (0,)), ((), ()))
NN_DIMS = (((1,), (0,)), ((), ()))


def _place():
    return lax.axis_index("x"), lax.axis_index("y"), lax.axis_index("c")


def _slot(p):
    return 4 * p[0] + 2 * p[1] + p[2]


def _direct_copies(ins, outs, send_sems, recv_sems, local_sems, gather):
    x, y, c = _place()
    me = (x, y, c)
    flip = lambda v, bit: 1 - v if bit else v
    peers = [(flip(x, k & 4), flip(y, k & 2), flip(c, k & 1)) for k in range(1, N_DEV)]
    local, sends, recvs = [], [], []
    for a in range(len(ins)):
        mine = ins[a] if gather else ins[a].at[_slot(me)]
        local.append(pltpu.make_async_copy(mine, outs[a].at[_slot(me)], local_sems.at[a]))
        for k, peer in enumerate(peers):
            sems = dict(send_sem=send_sems.at[a * 7 + k], recv_sem=recv_sems.at[a * 7 + k], device_id=peer,
                        device_id_type=MESH)
            sends.append(pltpu.make_async_remote_copy(
                src_ref=ins[a] if gather else ins[a].at[_slot(peer)], dst_ref=outs[a].at[_slot(me)], **sems))
            recvs.append(pltpu.make_async_remote_copy(src_ref=mine, dst_ref=outs[a].at[_slot(peer)], **sems))
    return local, sends, recvs


def _comm_scratch(n):
    return [pltpu.SemaphoreType.DMA((7 * n,)), pltpu.SemaphoreType.DMA((7 * n,)), pltpu.SemaphoreType.DMA((n,))]


def _pcall(body, side=None, **kw):
    if side is None:
        return pl.pallas_call(body, **kw)
    arrs, gather = side
    n = len(arrs)
    grid = kw["grid"]
    in_specs = list(kw["in_specs"])
    single = not isinstance(kw["out_specs"], (list, tuple))
    out_specs = [kw["out_specs"]] if single else list(kw["out_specs"])
    out_shape = [kw["out_shape"]] if single else list(kw["out_shape"])
    scratch = list(kw.get("scratch_shapes", []))
    n_in, n_out, n_scr = len(in_specs), len(out_specs), len(scratch)
    hbm = pl.BlockSpec(memory_space=pl.ANY)

    def hosted(*refs):
        pos = [0]

        def take(k):
            pos[0] += k
            return refs[pos[0] - k:pos[0]]

        ins, s_ins, outs, s_outs, scr, sems = take(n_in), take(n), take(n_out), take(n), take(n_scr), take(3)
        ids = [pl.program_id(i) for i in range(len(grid))]
        first = functools.reduce(jnp.logical_and, [i == 0 for i in ids])
        last = functools.reduce(jnp.logical_and, [i == g - 1 for i, g in zip(ids, grid)])

        @pl.when(first)
        def _():
            local, sends, _ = _direct_copies(s_ins, s_outs, *sems, gather)
            for cp in local + sends:
                cp.start()

        body(*ins, *outs, *scr)

        @pl.when(last)
        def _():
            local, sends, recvs = _direct_copies(s_ins, s_outs, *sems, gather)
            for cp in recvs:
                cp.wait_recv()
            for cp in sends:
                cp.wait_send()
            for cp in local:
                cp.wait()

    kw.update(in_specs=in_specs + [hbm] * n, out_specs=out_specs + [hbm] * n,
              out_shape=out_shape + [jax.ShapeDtypeStruct(((N_DEV,) + a.shape) if gather else a.shape, a.dtype)
                                     for a in arrs],
              scratch_shapes=scratch + _comm_scratch(n))
    call = pl.pallas_call(hosted, **kw)

    def run(*args):
        res = call(*args, *arrs)
        main = res[0] if single else list(res[:n_out])
        return main, list(res[n_out:])

    return run


def _params(sem=None, **kw):
    if sem is not None:
        kw["dimension_semantics"] = sem
    return pltpu.CompilerParams(vmem_limit_bytes=VMEM_LIMIT, **kw)


def _rotate(xv, c, sp, sm, transpose):
    width = xv.shape[1]
    half = ROT_DIM // 2
    if transpose:
        return xv * c + pltpu.roll(xv * sp, width - half, 1) + pltpu.roll(xv * sm, half, 1)
    return xv * c + pltpu.roll(xv, half, 1) * sp + pltpu.roll(xv, width - half, 1) * sm


def mm_rows(pairs, trans_b, out_dtype, name, tm=512, side=None, rope=None):
    n = len(pairs)
    m = pairs[0][0].shape[0]
    n_out = pairs[0][1].shape[0 if trans_b else 1]
    dims = NT_DIMS if trans_b else NN_DIMS

    def body(*refs):
        o_ref = refs[-1]
        acc = None
        for a_ref, b_ref in zip(refs[:n], refs[n:2 * n]):
            d = lax.dot_general(a_ref[...], b_ref[...], dims, preferred_element_type=F32)
            acc = d if acc is None else acc + d
        if rope is None:
            o_ref[...] = acc.astype(o_ref.dtype)
        else:
            width = rope[1]
            c, sp, sm = (jnp.concatenate([r[...]] * (width // LANES), axis=1) for r in refs[2 * n:2 * n + 3])
            o_ref[:, :width] = _rotate(acc[:, :width], c, sp, sm, False).astype(o_ref.dtype)
            o_ref[:, width:] = acc[:, width:].astype(o_ref.dtype)

    in_specs = [pl.BlockSpec((tm, a.shape[1]), lambda i: (i, 0)) for a, _ in pairs]
    in_specs += [pl.BlockSpec(b.shape, lambda i: (0, 0)) for _, b in pairs]
    args = [a for a, _ in pairs] + [b for _, b in pairs]
    if rope is not None:
        in_specs += [pl.BlockSpec((tm, LANES), lambda i: (i, 0))] * 3
        args += list(rope[0])
    return _pcall(
        body, side=side, name=name, grid=(m // tm,), in_specs=in_specs,
        out_specs=pl.BlockSpec((tm, n_out), lambda i: (i, 0)),
        out_shape=jax.ShapeDtypeStruct((m, n_out), out_dtype),
        compiler_params=_params(("arbitrary",)),
    )(*args)


DH_ROWS = 256
TN_TOKENS = 2048
TN_OUT_ELEMS = 2 * 1024 * 1024


def mm_tn(a, b, out_dtype, name, side=None, rows=None):
    t, ka = a.shape
    n_out = b.shape[1]
    tk = min(TN_TOKENS, t)
    tka = ka // 2 if ka * n_out > TN_OUT_ELEMS else ka
    tn = n_out
    steps = t // tk

    def body(a_ref, b_ref, o_ref, acc_ref):
        k = pl.program_id(2)
        d = lax.dot_general(a_ref[...], b_ref[...], TN_DIMS, preferred_element_type=F32)

        @pl.when(k == 0)
        def _():
            acc_ref[...] = d

        @pl.when(k > 0)
        def _():
            acc_ref[...] += d

        @pl.when(k == steps - 1)
        def _():
            o_ref[...] = acc_ref[...].astype(o_ref.dtype)

    return _pcall(
        body, side=side, name=name, grid=(ka // tka, n_out // tn, steps),
        in_specs=[pl.BlockSpec((tk, tka), lambda i, j, k: (k, i)), pl.BlockSpec((tk, tn), lambda i, j, k: (k, j))],
        out_specs=pl.BlockSpec((tka, tn), lambda i, j, k: (i, j)),
        out_shape=jax.ShapeDtypeStruct((ka if rows is None else rows, n_out), out_dtype),
        scratch_shapes=[pltpu.VMEM((tka, tn), F32)],
        compiler_params=_params(("arbitrary", "arbitrary", "arbitrary")),
    )(a, b)


def mm_tn_stack(a_list, b, rows, out_dtype, name, tk=1024):
    t, n_out = b.shape
    tk = min(tk, t)
    steps = t // tk
    n = len(a_list)
    offs = [sum(rows[:i]) for i in range(n)]

    def body(*refs):
        a_refs, b_ref, o_ref, acc_refs = refs[:n], refs[n], refs[n + 1], refs[n + 2:]
        k = pl.program_id(0)
        bv = b_ref[...]
        for a_ref, acc_ref in zip(a_refs, acc_refs):
            d = lax.dot_general(a_ref[...], bv, TN_DIMS, preferred_element_type=F32)

            @pl.when(k == 0)
            def _(acc_ref=acc_ref, d=d):
                acc_ref[...] = d

            @pl.when(k > 0)
            def _(acc_ref=acc_ref, d=d):
                acc_ref[...] += d

        @pl.when(k == steps - 1)
        def _():
            for acc_ref, off, r in zip(acc_refs, offs, rows):
                o_ref[off:off + r, :] = acc_ref[0:r, :].astype(o_ref.dtype)

    return _pcall(
        body, name=name, grid=(steps,),
        in_specs=[pl.BlockSpec((tk, a.shape[1]), lambda k: (k, 0)) for a in a_list]
        + [pl.BlockSpec((tk, n_out), lambda k: (k, 0))],
        out_specs=pl.BlockSpec((sum(rows), n_out), lambda k: (0, 0)),
        out_shape=jax.ShapeDtypeStruct((sum(rows), n_out), out_dtype),
        scratch_shapes=[pltpu.VMEM((a.shape[1], n_out), F32) for a in a_list],
        compiler_params=_params(("arbitrary",)),
    )(*a_list, b)


def _col_chunks(width, chunk=512):
    return [slice(c, min(c + chunk, width)) for c in range(0, width, chunk)]


def _sigmoid(x):
    return 1.0 / (1.0 + jnp.exp(-x))


def ffn_up(h, wgt, wut, name, tm=256, tn=D_FF_PAD, side=None):
    t, d = h.shape
    fp = wgt.shape[0]

    def body(h_ref, wg_ref, wu_ref, g_ref, u_ref, a_ref):
        hv = h_ref[...]

        def finish(cols, g, u):
            g_ref[:, cols] = g.astype(BF16)
            u_ref[:, cols] = u.astype(BF16)
            a_ref[:, cols] = (g * _sigmoid(g) * u).astype(BF16)

        pending = None
        for cols in _col_chunks(tn):
            g = lax.dot_general(hv, wg_ref[cols, :], NT_DIMS, preferred_element_type=F32)
            u = lax.dot_general(hv, wu_ref[cols, :], NT_DIMS, preferred_element_type=F32)
            if pending is not None:
                finish(*pending)
            pending = (cols, g, u)
        finish(*pending)

    w_spec = pl.BlockSpec((tn, d), lambda j, i: (j, 0))
    o_spec = pl.BlockSpec((tm, tn), lambda j, i: (i, j))
    o_shape = jax.ShapeDtypeStruct((t, fp), BF16)
    return _pcall(
        body, side=side, name=name, grid=(fp // tn, t // tm),
        in_specs=[pl.BlockSpec((tm, d), lambda j, i: (i, 0)), w_spec, w_spec],
        out_specs=[o_spec, o_spec, o_spec], out_shape=[o_shape, o_shape, o_shape],
        compiler_params=_params(("arbitrary", "arbitrary")),
    )(h, wgt, wut)


def ffn_down_bwd(dy0, wd, gate, up, name, tm=256, tn=D_FF_PAD, side=None):
    t, d = dy0.shape
    fp = wd.shape[0]

    def body(dy_ref, wd_ref, g_ref, u_ref, dg_ref, du_ref):
        dyv = dy_ref[...]

        def finish(cols, dact):
            g = g_ref[:, cols].astype(F32)
            u = u_ref[:, cols].astype(F32)
            sg = _sigmoid(g)
            silu = g * sg
            du_ref[:, cols] = (dact * silu).astype(BF16)
            dg_ref[:, cols] = ((dact * u) * (sg + silu * (1.0 - sg))).astype(BF16)

        pending = None
        for cols in _col_chunks(tn):
            dact = lax.dot_general(dyv, wd_ref[cols, :], NT_DIMS, preferred_element_type=F32)
            if pending is not None:
                finish(*pending)
            pending = (cols, dact)
        finish(*pending)

    t_spec = pl.BlockSpec((tm, tn), lambda j, i: (i, j))
    o_shape = jax.ShapeDtypeStruct((t, fp), BF16)
    return _pcall(
        body, side=side, name=name, grid=(fp // tn, t // tm),
        in_specs=[pl.BlockSpec((tm, d), lambda j, i: (i, 0)), pl.BlockSpec((tn, d), lambda j, i: (j, 0)), t_spec, t_spec],
        out_specs=[t_spec, t_spec], out_shape=[o_shape, o_shape],
        compiler_params=_params(("arbitrary", "arbitrary")),
    )(dy0, wd, gate, up)


def _row_specs(dx, ts, ns):
    return pl.BlockSpec((ts, dx), lambda b, s: (b * ns + s, 0))


def _mod_spec():
    return pl.BlockSpec((1, N_MOD, D_MODEL), lambda b, s: (b, 0, 0))


def _vec_spec(dx):
    return pl.BlockSpec((1, dx), lambda b, s: (0, 0))


def prenorm_fwd(x, g, mod, i_shift, i_scale, nb, name, ts=1024):
    t, dx = x.shape
    ts = min(ts, t // nb)
    ns = t // nb // ts

    def body(*refs):
        if mod is None:
            x_ref, g_ref, h_ref = refs
        else:
            x_ref, g_ref, mod_ref, h_ref = refs
        xv = x_ref[...]
        r = lax.rsqrt(jnp.mean(xv * xv, axis=-1, keepdims=True) + EPS)
        h = xv * r * g_ref[...]
        if mod is not None:
            h = h * (1.0 + mod_ref[0, i_scale:i_scale + 1, :]) + mod_ref[0, i_shift:i_shift + 1, :]
        h_ref[...] = h.astype(BF16)

    in_specs = [_row_specs(dx, ts, ns), _vec_spec(dx)]
    args = [x, g]
    if mod is not None:
        in_specs.append(_mod_spec())
        args.append(mod)
    return _pcall(
        body, name=name, grid=(nb, ns), in_specs=in_specs, out_specs=_row_specs(dx, ts, ns),
        out_shape=jax.ShapeDtypeStruct((t, dx), BF16), compiler_params=_params(("arbitrary", "arbitrary")),
    )(*args)


def prenorm_bwd(dh, x, g, mod, i_scale, dres, nb, name, ts=512, side=None):
    t, dx = x.shape
    ts = min(ts, t // nb)
    ns = t // nb // ts
    has_mod = mod is not None
    has_res = dres is not None
    pairs = dh if isinstance(dh, list) else None
    n_mm = 0 if pairs is None else len(pairs)

    def body(*refs):
        refs = list(refs)
        if pairs is None:
            dhv = refs[0][...].astype(F32)
            refs = refs[1:]
        else:
            dhv = None
            for a_ref, b_ref in zip(refs[:n_mm], refs[n_mm:2 * n_mm]):
                d = jnp.dot(a_ref[...], b_ref[...], preferred_element_type=F32)
                dhv = d if dhv is None else dhv + d
            refs = refs[2 * n_mm:]
        x_ref, g_ref = refs[:2]
        pos = 2
        mod_ref = dres_ref = None
        if has_mod:
            mod_ref = refs[pos]
            pos += 1
        if has_res:
            dres_ref = refs[pos]
            pos += 1
        dx_ref, dg_ref = refs[pos], refs[pos + 1]
        b, s = pl.program_id(0), pl.program_id(1)
        xv = x_ref[...]
        gv = g_ref[...]
        r = lax.rsqrt(jnp.mean(xv * xv, axis=-1, keepdims=True) + EPS)
        xhat = xv * r
        dn = dhv
        if has_mod:
            dsc_ref, dsh_ref = refs[pos + 2], refs[pos + 3]
            dn = dhv * (1.0 + mod_ref[0, i_scale:i_scale + 1, :])
            dsc = jnp.sum(dhv * xhat * gv, axis=0, keepdims=True)[None]
            dsh = jnp.sum(dhv, axis=0, keepdims=True)[None]

            @pl.when(s == 0)
            def _():
                dsc_ref[...] = dsc
                dsh_ref[...] = dsh

            @pl.when(s > 0)
            def _():
                dsc_ref[...] += dsc
                dsh_ref[...] += dsh

        dg = jnp.sum(dn * xhat, axis=0, keepdims=True)
        first = jnp.logical_and(b == 0, s == 0)

        @pl.when(first)
        def _():
            dg_ref[...] = dg

        @pl.when(jnp.logical_not(first))
        def _():
            dg_ref[...] += dg

        dxhat = dn * gv
        dxv = r * (dxhat - xhat * jnp.mean(dxhat * xhat, axis=-1, keepdims=True))
        if has_res:
            dxv = dxv + dres_ref[...]
        dx_ref[...] = dxv

    row = _row_specs(dx, ts, ns)
    if pairs is None:
        in_specs, args = [row], [dh]
    else:
        in_specs = [_row_specs(a.shape[1], ts, ns) for a, _ in pairs]
        in_specs += [pl.BlockSpec(b.shape, lambda b_, s_: (0, 0)) for _, b in pairs]
        args = [a for a, _ in pairs] + [b for _, b in pairs]
    in_specs += [row, _vec_spec(dx)]
    args += [x, g]
    if has_mod:
        in_specs.append(_mod_spec())
        args.append(mod)
    if has_res:
        in_specs.append(row)
        args.append(dres)
    out_specs = [row, _vec_spec(dx)]
    out_shape = [jax.ShapeDtypeStruct((t, dx), F32), jax.ShapeDtypeStruct((1, dx), F32)]
    if has_mod:
        bspec = pl.BlockSpec((1, 1, dx), lambda b, s: (b, 0, 0))
        out_specs += [bspec, bspec]
        out_shape += [jax.ShapeDtypeStruct((nb, 1, dx), F32)] * 2
    return _pcall(
        body, side=side, name=name, grid=(nb, ns), in_specs=in_specs, out_specs=out_specs, out_shape=out_shape,
        compiler_params=_params(("arbitrary", "arbitrary")),
    )(*args)


def postnorm_fwd(x, pairs, g, mod, i_gate, coef, nb, name, target=None, ts=512, side=None):
    t, dx = x.shape
    with_loss = target is not None
    ts = min(ts, t // nb)
    ns = t // nb // ts
    n_mm = len(pairs)

    def body(*refs):
        yv = None
        for a_ref, b_ref in zip(refs[:n_mm], refs[n_mm:2 * n_mm]):
            d = jnp.dot(a_ref[...], b_ref[...], preferred_element_type=F32)
            yv = d if yv is None else yv + d
        refs = refs[2 * n_mm:]
        x_ref, g_ref, mod_ref = refs[:3]
        refs[-1][...] = yv
        r = lax.rsqrt(jnp.mean(yv * yv, axis=-1, keepdims=True) + EPS)
        out = x_ref[...] + (coef * mod_ref[0, i_gate:i_gate + 1, :]) * (yv * r * g_ref[...])
        if not with_loss:
            refs[3][...] = out
            return
        t_ref, dx_ref, loss_ref = refs[3:6]
        b, s = pl.program_id(0), pl.program_id(1)
        err = out - t_ref[...]
        dx_ref[...] = err * (1.0 / dx)
        part = (0.5 / dx) * jnp.sum(jnp.sum(err * err, axis=1, keepdims=True), axis=0, keepdims=True)
        first = jnp.logical_and(b == 0, s == 0)

        @pl.when(first)
        def _():
            loss_ref[...] = part

        @pl.when(jnp.logical_not(first))
        def _():
            loss_ref[...] += part

    row = _row_specs(dx, ts, ns)
    in_specs = [_row_specs(a.shape[1], ts, ns) for a, _ in pairs]
    in_specs += [pl.BlockSpec(b.shape, lambda b_, s_: (0, 0)) for _, b in pairs]
    in_specs += [row, _vec_spec(dx), _mod_spec()]
    args = [a for a, _ in pairs] + [b for _, b in pairs] + [x, g, mod]
    row_shape = jax.ShapeDtypeStruct((t, dx), F32)
    out_specs, out_shape = [row, row], [row_shape, row_shape]
    if with_loss:
        in_specs.append(row)
        args.append(target)
        out_specs = [row, pl.BlockSpec((1, 1), lambda b, s: (0, 0)), row]
        out_shape = [row_shape, jax.ShapeDtypeStruct((1, 1), F32), row_shape]
    return _pcall(
        body, side=side, name=name, grid=(nb, ns), in_specs=in_specs, out_specs=out_specs, out_shape=out_shape,
        compiler_params=_params(("arbitrary", "arbitrary")),
    )(*args)


def postnorm_bwd(dxo, y0, g, mod, i_gate, coef, nb, name, ts=1024):
    t, dx = y0.shape
    ts = min(ts, t // nb)
    ns = t // nb // ts

    def body(d_ref, y_ref, g_ref, mod_ref, dy_ref, dg_ref, dgate_ref):
        b, s = pl.program_id(0), pl.program_id(1)
        yv = y_ref[...]
        dv = d_ref[...]
        gv = g_ref[...]
        r = lax.rsqrt(jnp.mean(yv * yv, axis=-1, keepdims=True) + EPS)
        yhat = yv * r
        dgate = jnp.sum(dv * (coef * (yhat * gv)), axis=0, keepdims=True)[None]
        dyn = dv * (coef * mod_ref[0, i_gate:i_gate + 1, :])
        dg = jnp.sum(dyn * yhat, axis=0, keepdims=True)
        dyhat = dyn * gv
        dy_ref[...] = (r * (dyhat - yhat * jnp.mean(dyhat * yhat, axis=-1, keepdims=True))).astype(BF16)

        @pl.when(s == 0)
        def _():
            dgate_ref[...] = dgate

        @pl.when(s > 0)
        def _():
            dgate_ref[...] += dgate

        first = jnp.logical_and(b == 0, s == 0)

        @pl.when(first)
        def _():
            dg_ref[...] = dg

        @pl.when(jnp.logical_not(first))
        def _():
            dg_ref[...] += dg

    row = _row_specs(dx, ts, ns)
    return _pcall(
        body, name=name, grid=(nb, ns), in_specs=[row, row, _vec_spec(dx), _mod_spec()],
        out_specs=[row, _vec_spec(dx), pl.BlockSpec((1, 1, dx), lambda b, s: (b, 0, 0))],
        out_shape=[jax.ShapeDtypeStruct((t, dx), BF16), jax.ShapeDtypeStruct((1, dx), F32),
                   jax.ShapeDtypeStruct((nb, 1, dx), F32)],
        compiler_params=_params(("arbitrary", "arbitrary")),
    )(dxo, y0, g, mod)


def rope_tables(positions):
    inv_freq = ROPE_THETA ** (-jnp.arange(0, ROT_DIM, 2, dtype=F32) / ROT_DIM)
    ang = positions.astype(F32).reshape(-1, 1) * inv_freq
    cos, sin = jnp.cos(ang), jnp.sin(ang)
    half = ROT_DIM // 2
    z = lambda n: jnp.zeros((ang.shape[0], n), F32)
    c = jnp.concatenate([cos, cos, jnp.ones((ang.shape[0], HEAD_DIM - ROT_DIM), F32)], axis=1)
    sp = jnp.concatenate([z(half), sin, z(HEAD_DIM - ROT_DIM)], axis=1)
    sm = jnp.concatenate([-sin, z(HEAD_DIM - half)], axis=1)
    return tuple(jnp.tile(a, (1, HEADS_PER_STEP)) for a in (c, sp, sm))


def _scan_lanes(x, reverse):
    n = x.shape[-1]
    lane = lax.broadcasted_iota(jnp.int32, x.shape, x.ndim - 1)
    k = 1
    while k < n:
        if reverse:
            x = x + jnp.where(lane < n - k, pltpu.roll(x, n - k, x.ndim - 1), 0.0)
        else:
            x = x + jnp.where(lane >= k, pltpu.roll(x, k, x.ndim - 1), 0.0)
        k *= 2
    return x


def _log_sigmoid(z):
    return jnp.minimum(z, 0.0) - jnp.log(1.0 + jnp.exp(-jnp.abs(z)))


def fox_gate_fwd(ft, b_forget, name):
    nb, nh, s = ft.shape

    def body(f_ref, b_ref, o_ref):
        z = f_ref[0] + b_ref[...]
        o_ref[0] = -_scan_lanes(_log_sigmoid(z), False)

    spec = pl.BlockSpec((1, nh, s), lambda b: (b, 0, 0))
    return _pcall(
        body, name=name, grid=(nb,), in_specs=[spec, pl.BlockSpec((nh, 1), lambda b: (0, 0))], out_specs=spec,
        out_shape=jax.ShapeDtypeStruct((nb, nh, s), F32), compiler_params=_params(("arbitrary",)),
    )(ft, b_forget)


def fox_gate_bwd(dcb, drow, ft, b_forget, name):
    nb, nh, s = ft.shape

    def body(d_ref, r_ref, f_ref, b_ref, dz_ref, db_ref):
        b = pl.program_id(0)
        z = f_ref[0] + b_ref[...]
        dlf = _scan_lanes(r_ref[0] - d_ref[0], True)
        dz = dlf * _sigmoid(-z)
        dz_ref[0] = dz
        db = jnp.sum(dz, axis=1, keepdims=True)

        @pl.when(b == 0)
        def _():
            db_ref[...] = db

        @pl.when(b > 0)
        def _():
            db_ref[...] += db

    spec = pl.BlockSpec((1, nh, s), lambda b: (b, 0, 0))
    vec = pl.BlockSpec((nh, 1), lambda b: (0, 0))
    return _pcall(
        body, name=name, grid=(nb,), in_specs=[spec, spec, spec, vec], out_specs=[spec, vec],
        out_shape=[jax.ShapeDtypeStruct((nb, nh, s), F32), jax.ShapeDtypeStruct((nh, 1), F32)],
        compiler_params=_params(("arbitrary",)),
    )(dcb, drow, ft, b_forget)


ATTN_TQ = 512
ATTN_TK = 512
ONES_ROWS = 16


def _rows_to_cols(rows):
    tile = jnp.concatenate([jnp.broadcast_to(rw, (HEAD_DIM, rw.shape[1])) for rw in rows], axis=0)
    return tile.T


def _block_delta(s, tq, tk):
    off = jnp.arange(s // tk) - (tq // tk - 1)
    return off[:, None, None] * tk + jnp.arange(tq)[None, None, :] - jnp.arange(tk)[None, :, None]


def dilated_table(s, tq, tk):
    delta = _block_delta(s, tq, tk)
    count = jnp.zeros(delta.shape, F32)
    for window, dil in DILATED_PATTERNS:
        count = count + ((delta >= 0) & (delta <= window) & (delta % dil == 0)).astype(F32)
    return jnp.where(count > 0, jnp.log(jnp.maximum(count, 1.0)), NEG)


def causal_table(s, tq, tk):
    return jnp.where(_block_delta(s, tq, tk) >= 0, 0.0, NEG).astype(F32)


def attn_fwd(q_arr, q_off, k_arr, k_off, v_arr, v_off, table, colbias, nb, name, side=None, off_diag_bias=True):
    t = q_arr.shape[0]
    s = t // nb
    tk, tq = table.shape[1:]
    assert tq == tk, "the diagonal handling below is written for square tiles"
    nq, nk = s // tq, s // tk
    npairs = WIDTH_A // LANES
    use_cb = colbias is not None

    def body(*refs):
        refs = list(refs)
        q_ref, k_ref, v_ref, tab_ref = refs[:4]
        cb_ref = refs[4] if use_cb else None
        tail = refs[-(HEADS_PER_STEP + int(use_cb)):]
        acc_s = tail[:HEADS_PER_STEP]
        cbc_s = tail[-1] if use_cb else None
        o_ref, lse_ref, vt_s = refs[-3 - len(tail):-len(tail)]
        qi = pl.program_id(2)

        heads = [slice(h * HEAD_DIM, (h + 1) * HEAD_DIM) for h in range(HEADS_PER_STEP)]

        @pl.when(qi == 0)
        def _():
            for cblk in range(nk):
                vt = v_ref[cblk * tk:(cblk + 1) * tk, :].astype(F32).T.astype(BF16)
                for h, hs in enumerate(heads):
                    vt_s[cblk, h, 0:HEAD_DIM, :] = vt[hs, :]
                    vt_s[cblk, h, HEAD_DIM:, :] = jnp.ones((ONES_ROWS, tk), BF16)
                if use_cb:
                    cbc_s[cblk] = _rows_to_cols([cb_ref[0, h, cblk] for h in range(HEADS_PER_STEP)])

        qt_all = (q_ref[...].astype(F32) * ATTN_SCALE).T.astype(BF16)
        qts = [qt_all[hs, :] for hs in heads]
        for a in acc_s:
            a[...] = jnp.zeros_like(a)

        def tile(kb, tab, k0, klen, q0, carry):
            ks = pl.multiple_of(kb * tk + k0, klen)
            sts, out = [], []
            for h, hs in enumerate(heads):
                st = jnp.dot(k_ref[pl.ds(ks, klen), hs], qts[h][:, q0:], preferred_element_type=F32)
                if tab is not None:
                    st = st + tab
                if use_cb:
                    st = st + cbc_s[kb, k0:k0 + klen, h * HEAD_DIM:h * HEAD_DIM + 1]
                sts.append(st)
            m_old = [carry[h][:, q0:] for h in range(HEADS_PER_STEP)]
            m_new = [jnp.maximum(m_old[h], jnp.max(sts[h], axis=0, keepdims=True)) for h in range(HEADS_PER_STEP)]
            for h in range(HEADS_PER_STEP):
                pt = jnp.exp(sts[h] - m_new[h]).astype(BF16)
                acc_s[h][:, q0:] = (jnp.exp(m_old[h] - m_new[h]) * acc_s[h][:, q0:]
                                    + jnp.dot(vt_s[kb, h, :, k0:k0 + klen], pt, preferred_element_type=F32))
                out.append(m_new[h] if q0 == 0 else jnp.concatenate([carry[h][:, :q0], m_new[h]], axis=1))
            return tuple(out)

        fin = lax.fori_loop(0, qi, lambda kb, c: tile(kb, tab_ref[qi - kb] if off_diag_bias else None, 0, tk, 0, c),
                            tuple(jnp.full((1, tq), NEG, F32) for _ in heads))
        half = tk // 2
        fin = tile(qi, tab_ref[0, 0:half, :], 0, half, 0, fin)
        fin = tile(qi, tab_ref[0, half:, half:], half, half, half, fin)
        outs = []
        for h in range(HEADS_PER_STEP):
            l = acc_s[h][HEAD_DIM:HEAD_DIM + 1, :]
            outs.append(acc_s[h][0:HEAD_DIM, :] / l)
            lse_ref[0, h, 0] = fin[h] + jnp.log(l)
        o_ref[...] = jnp.concatenate(outs, axis=0).T

    def seq_spec(off):
        return pl.BlockSpec((s, LANES), lambda b, j, i: (b, off + j))

    in_specs = [pl.BlockSpec((tq, LANES), lambda b, j, i: (b * nq + i, q_off + j)), seq_spec(k_off), seq_spec(v_off),
                pl.BlockSpec(table.shape, lambda b, j, i: (0, 0, 0))]
    args = [q_arr, k_arr, v_arr, table]
    if use_cb:
        in_specs.append(pl.BlockSpec((1, HEADS_PER_STEP, nk, 1, tk), lambda b, j, i: (b, j, 0, 0, 0)))
        args.append(colbias)
    n_heads = npairs * HEADS_PER_STEP
    return _pcall(
        body, side=side, name=name, grid=(nb, npairs, nq), in_specs=in_specs,
        out_specs=[pl.BlockSpec((tq, LANES), lambda b, j, i: (b * nq + i, j)),
                   pl.BlockSpec((1, HEADS_PER_STEP, 1, 1, tq), lambda b, j, i: (b, j, i, 0, 0))],
        out_shape=[jax.ShapeDtypeStruct((t, npairs * LANES), F32), jax.ShapeDtypeStruct((nb, n_heads, nq, 1, tq), F32)],
        scratch_shapes=[pltpu.VMEM((nk, HEADS_PER_STEP, HEAD_DIM + ONES_ROWS, tk), BF16)]
        + [pltpu.VMEM((HEAD_DIM + ONES_ROWS, tq), F32)] * HEADS_PER_STEP
        + ([pltpu.VMEM((nk, tk, LANES), F32)] if use_cb else []),
        compiler_params=_params(("arbitrary", "arbitrary", "arbitrary")),
    )(*args)


def attn_bwd(q_arr, q_off, k_arr, k_off, v_arr, v_off, o_arr, lse_arr, do_arr, table, colbias, nb, name, side=None,
             rope_tabs=None, off_diag_bias=True):
    t = q_arr.shape[0]
    s = t // nb
    tk, tq = table.shape[1:]
    assert tq == tk, "the diagonal handling below is written for square tiles"
    nq, nk = s // tq, s // tk
    npairs = WIDTH_A // LANES
    use_cb = colbias is not None

    def body(*refs):
        refs = list(refs)
        q_ref, k_ref, v_ref, o_ref, lse_ref, do_ref, tab_ref = refs[:7]
        pos = 7
        cb_ref = None
        if use_cb:
            cb_ref = refs[pos]
            pos += 1
        rope_refs = None
        if rope_tabs is not None:
            rope_refs = refs[pos:pos + 3]
            pos += 3
        dq_ref, dk_ref, dv_ref = refs[pos:pos + 3]
        pos += 3
        dcb_ref = drow_ref = None
        if use_cb:
            dcb_ref, drow_ref = refs[pos:pos + 2]
            pos += 2
        kt_s, dkt_s, dvt_s = refs[pos:pos + 3]
        dqt_s = refs[pos + 3:pos + 3 + HEADS_PER_STEP]
        dcb_s, cbc_s = refs[pos + 3 + HEADS_PER_STEP:pos + 5 + HEADS_PER_STEP] if use_cb else (None, None)

        heads = [slice(h * HEAD_DIM, (h + 1) * HEAD_DIM) for h in range(HEADS_PER_STEP)]
        for cblk in range(nk):
            kt_s[cblk] = k_ref[cblk * tk:(cblk + 1) * tk, :].astype(F32).T.astype(BF16)
        dkt_s[...] = jnp.zeros_like(dkt_s)
        dvt_s[...] = jnp.zeros_like(dvt_s)
        if use_cb:
            dcb_s[...] = jnp.zeros_like(dcb_s)
            for cblk in range(nk):
                cbc_s[cblk] = _rows_to_cols([cb_ref[0, h, cblk] for h in range(HEADS_PER_STEP)])
        ones = jnp.ones((8, HEAD_DIM), BF16)

        def q_loop(qi, carry):
            qs = pl.multiple_of(qi * tq, tq)
            q_all = (q_ref[pl.ds(qs, tq), :].astype(F32) * ATTN_SCALE)
            do_all = do_ref[pl.ds(qs, tq), :]
            qt_all = q_all.T.astype(BF16)
            dot_all = do_all.T.astype(BF16)
            qt, dot, lse, dsum = [], [], [], []
            for h, hs in enumerate(heads):
                qt.append(qt_all[hs, :])
                dot.append(dot_all[hs, :])
                lse.append(lse_ref[0, h, qi])
                prod = do_all[:, hs] * o_ref[pl.ds(qs, tq), hs]
                hi = prod.astype(BF16)
                lo = (prod - hi.astype(F32)).astype(BF16)
                dsum.append((lax.dot_general(ones, hi, NT_DIMS, preferred_element_type=F32)
                             + lax.dot_general(ones, lo, NT_DIMS, preferred_element_type=F32))[0:1, :])
            for a in dqt_s:
                a[...] = jnp.zeros_like(a)

            def tile(kb, tab, k0, klen, q0, drow):
                ks = pl.multiple_of(kb * tk + k0, klen)
                keys = slice(k0, k0 + klen)
                sts, dpts, out = [], [], []
                for h, hs in enumerate(heads):
                    st = jnp.dot(k_ref[pl.ds(ks, klen), hs], qt[h][:, q0:], preferred_element_type=F32)
                    if tab is not None:
                        st = st + tab
                    if use_cb:
                        st = st + cbc_s[kb, keys, h * HEAD_DIM:h * HEAD_DIM + 1]
                    sts.append(st)
                    dpts.append(jnp.dot(v_ref[pl.ds(ks, klen), hs], dot[h][:, q0:], preferred_element_type=F32))
                for h, hs in enumerate(heads):
                    pt = jnp.exp(sts[h] - lse[h][:, q0:])
                    dst = pt * (dpts[h] - dsum[h][:, q0:])
                    dst_b = dst.astype(BF16)
                    dvt_s[h, kb, :, keys] += lax.dot_general(dot[h][:, q0:], pt.astype(BF16), NT_DIMS,
                                                             preferred_element_type=F32)
                    dkt_s[h, kb, :, keys] += lax.dot_general(qt[h][:, q0:], dst_b, NT_DIMS, preferred_element_type=F32)
                    dqt_s[h][:, q0:] += jnp.dot(kt_s[kb, hs, keys], dst_b, preferred_element_type=F32)
                    if use_cb:
                        dcb_s[h, pl.ds(ks, klen), :] += jnp.sum(dst, axis=1, keepdims=True)
                        dr = drow[h][:, q0:] + jnp.sum(dst, axis=0, keepdims=True)
                        out.append(dr if q0 == 0 else jnp.concatenate([drow[h][:, :q0], dr], axis=1))
                    else:
                        out.append(drow[h])
                return tuple(out)

            drow = lax.fori_loop(0, qi, lambda kb, c: tile(kb, tab_ref[qi - kb] if off_diag_bias else None, 0, tk, 0, c),
                                 tuple(jnp.zeros((1, tq), F32) for _ in heads))
            half = tk // 2
            drow = tile(qi, tab_ref[0, 0:half, :], 0, half, 0, drow)
            drow = tile(qi, tab_ref[0, half:, half:], half, half, half, drow)
            dq = (jnp.concatenate([a[...] for a in dqt_s], axis=0) * ATTN_SCALE).T
            if rope_refs is not None:
                dq = _rotate(dq, *[coef[pl.ds(qs, tq), :] for coef in rope_refs], True)
            dq_ref[pl.ds(qs, tq), :] = dq.astype(dq_ref.dtype)
            if use_cb:
                for h in range(HEADS_PER_STEP):
                    drow_ref[0, h, qi] = drow[h]
            return carry

        lax.fori_loop(0, nq, q_loop, 0)
        for cblk in range(nk):
            rows = slice(cblk * tk, (cblk + 1) * tk)
            dk = jnp.concatenate([dkt_s[h, cblk] for h in range(HEADS_PER_STEP)], axis=0).T
            if rope_refs is not None:
                dk = _rotate(dk, *[coef[rows, :] for coef in rope_refs], True)
            dk_ref[rows, :] = dk.astype(dk_ref.dtype)
            dv_ref[rows, :] = jnp.concatenate([dvt_s[h, cblk] for h in range(HEADS_PER_STEP)], axis=0).T.astype(dv_ref.dtype)
            if use_cb:
                for h in range(HEADS_PER_STEP):
                    dcb_ref[0, h, cblk] = jnp.broadcast_to(dcb_s[h, rows, :], (tk, LANES)).T[0:1, :]

    def seq_spec(off):
        return pl.BlockSpec((s, LANES), lambda b, j: (b, off + j))

    row_spec = pl.BlockSpec((1, HEADS_PER_STEP, nq, 1, tq), lambda b, j: (b, j, 0, 0, 0))
    in_specs = [seq_spec(q_off), seq_spec(k_off), seq_spec(v_off), seq_spec(0), row_spec, seq_spec(0),
                pl.BlockSpec(table.shape, lambda b, j: (0, 0, 0))]
    args = [q_arr, k_arr, v_arr, o_arr, lse_arr, do_arr, table]
    width = npairs * LANES
    out_specs = [seq_spec(0)] * 3
    out_shape = [jax.ShapeDtypeStruct((t, width), BF16)] * 3
    scratch = [pltpu.VMEM((nk, LANES, tk), BF16), pltpu.VMEM((HEADS_PER_STEP, nk, HEAD_DIM, tk), F32),
               pltpu.VMEM((HEADS_PER_STEP, nk, HEAD_DIM, tk), F32)] + [pltpu.VMEM((HEAD_DIM, tq), F32)] * HEADS_PER_STEP
    if use_cb:
        cb_spec = pl.BlockSpec((1, HEADS_PER_STEP, nk, 1, tk), lambda b, j: (b, j, 0, 0, 0))
        in_specs.append(cb_spec)
        args.append(colbias)
    if rope_tabs is not None:
        in_specs += [pl.BlockSpec((s, LANES), lambda b, j: (b, 0))] * 3
        args += list(rope_tabs)
    if use_cb:
        out_specs += [cb_spec, row_spec]
        out_shape += [jax.ShapeDtypeStruct(colbias.shape, F32), jax.ShapeDtypeStruct(lse_arr.shape, F32)]
        scratch += [pltpu.VMEM((HEADS_PER_STEP, s, 1), F32), pltpu.VMEM((nk, tk, LANES), F32)]
    return _pcall(
        body, side=side, name=name, grid=(nb, npairs), in_specs=in_specs, out_specs=out_specs, out_shape=out_shape,
        scratch_shapes=scratch, compiler_params=_params(("arbitrary", "arbitrary")),
    )(*args)


def ada_fwd(c_all, w_ada, b_cols, name):
    def body(c_ref, w_ref, b_ref, o_ref):
        cv = c_ref[...]
        sc = (cv * _sigmoid(cv)).astype(BF16)
        o_ref[...] = jnp.dot(sc, w_ref[...].astype(BF16), preferred_element_type=F32) + b_ref[...]

    return _pcall(body, name=name, out_shape=jax.ShapeDtypeStruct((c_all.shape[0], w_ada.shape[1]), F32),
                  compiler_params=_params())(c_all, w_ada, b_cols)


def ada_bwd(c_all, dmod_cols, name):
    def body(c_ref, d_ref, o_ref):
        cv = c_ref[...]
        sc = (cv * _sigmoid(cv)).astype(BF16)
        o_ref[...] = lax.dot_general(sc, d_ref[...].astype(BF16), TN_DIMS, preferred_element_type=F32)

    return _pcall(body, name=name, out_shape=jax.ShapeDtypeStruct((c_all.shape[1], dmod_cols.shape[1]), F32),
                  compiler_params=_params())(c_all, dmod_cols)


def adamw(parts, group, w, m, v, name, tr=None):
    n = parts.shape[0]
    r, c = w.shape
    tr = r if tr is None else tr
    c1 = 1.0 - ADAM_B1 ** ADAM_STEP
    c2 = 1.0 - ADAM_B2 ** ADAM_STEP

    def body(p_ref, w_ref, m_ref, v_ref, g_ref, d_ref, nm_ref, nv_ref):
        g = p_ref[0, 0].astype(F32)
        for i in range(1, n):
            g = g + p_ref[i, 0].astype(F32)
        wv = w_ref[...]
        nm = ADAM_B1 * m_ref[...] + (1.0 - ADAM_B1) * g
        nv = ADAM_B2 * v_ref[...] + (1.0 - ADAM_B2) * (g * g)
        g_ref[...] = g
        nm_ref[...] = nm
        nv_ref[...] = nv
        d_ref[...] = -ADAM_LR * ((nm / c1) / (jnp.sqrt(nv / c2) + ADAM_EPS) + ADAM_WD * wv)

    spec = pl.BlockSpec((tr, c), lambda i: (i, 0))
    shape = jax.ShapeDtypeStruct((r, c), F32)
    return _pcall(
        body, name=name, grid=(r // tr,),
        in_specs=[pl.BlockSpec((n, 1, tr, c), lambda i: (0, group, i, 0)), spec, spec, spec],
        out_specs=[spec] * 4, out_shape=[shape] * 4, compiler_params=_params(("arbitrary",)),
    )(parts, w, m, v)


def all_gather(arrs, name):
    n = len(arrs)
    hbm = pl.BlockSpec(memory_space=pl.ANY)

    def body(*refs):
        ins, outs = refs[:n], refs[n:2 * n]
        send_sems, recv_sems, local_sems = refs[2 * n:]
        x, y, c = _place()
        me, sibling = (x, y, c), (x, y, 1 - c)
        chips = [(1 - x, y), (x, 1 - y), (1 - x, 1 - y)]

        def copy(a, k, block, to, src=None):
            dst = outs[a].at[_slot(block)]
            return pltpu.make_async_remote_copy(
                src_ref=dst if src is None else src, dst_ref=dst, send_sem=send_sems.at[a * 7 + k],
                recv_sem=recv_sems.at[a * 7 + k], device_id=to, device_id_type=MESH)

        mine = [pltpu.make_async_copy(ins[a], outs[a].at[_slot(me)], local_sems.at[a]) for a in range(n)]
        for cp in mine:
            cp.start()
        first = []
        for a in range(n):
            first.append(copy(a, 0, me, sibling, src=ins[a]))
            first += [copy(a, 1 + j, me, (*chip, c), src=ins[a]) for j, chip in enumerate(chips)]
        for cp in first:
            cp.start()
        passed = []
        for a in range(n):
            for j, chip in enumerate(chips):
                copy(a, 1 + j, (*chip, c), me).wait_recv()
                cp = copy(a, 4 + j, (*chip, c), sibling)
                cp.start()
                passed.append(cp)
        for a in range(n):
            copy(a, 0, sibling, me).wait_recv()
            for j, chip in enumerate(chips):
                copy(a, 4 + j, (*chip, 1 - c), me).wait_recv()
        for cp in first + passed:
            cp.wait_send()
        for cp in mine:
            cp.wait()

    return _pcall(
        body, name=name, in_specs=[hbm] * n, out_specs=[hbm] * n,
        out_shape=[jax.ShapeDtypeStruct((N_DEV,) + a.shape, a.dtype) for a in arrs],
        scratch_shapes=[pltpu.SemaphoreType.DMA((7 * n,)), pltpu.SemaphoreType.DMA((7 * n,)),
                        pltpu.SemaphoreType.DMA((n,))],
        compiler_params=pltpu.CompilerParams(has_side_effects=True),
    )(*arrs)


def _t(w):
    return jnp.swapaxes(w, -1, -2)


def _rows_from_blocks(blocks, pad_to=None):
    full = blocks.reshape(-1, blocks.shape[2])
    if pad_to is not None and pad_to > full.shape[0]:
        full = jnp.pad(full, ((0, pad_to - full.shape[0]), (0, 0)))
    return full


def _rows_to_blocks(full, nrows):
    return full[:nrows].reshape(N_DEV, nrows // N_DEV, full.shape[1])


SMALL_ORDER = ("g_pre_ff1", "g_post_ff1", "g_pre_mix", "g_post_mix", "g_out_a", "g_out_b", "g_pre_ff2", "g_post_ff2",
               "b_forget")


def _pack_small(vals):
    rows = []
    for name in SMALL_ORDER:
        v = vals[name].reshape(1, -1)
        if v.shape[1] % LANES:
            v = jnp.pad(v, ((0, 0), (0, LANES - v.shape[1] % LANES)))
        rows.append(v)
    return jnp.concatenate(rows, axis=1)


def _unpack_small(row, sizes):
    out, pos = {}, 0
    for name in SMALL_ORDER:
        n = sizes[name]
        out[name] = row[:, pos:pos + n]
        pos += -(-n // LANES) * LANES
    return out


def _ffn_forward(x, mod, g_pre, g_post, wg, wu, wd, i0, nb, tag, target=None, side=None, side_down=None):
    h = prenorm_fwd(x, g_pre, mod, i0, i0 + 1, nb, f"{tag}_prenorm")
    res, side_out = ffn_up(h, wg, wu, f"{tag}_up", side=side), None
    if side is not None:
        res, side_out = res
    gate, up, act = res
    if callable(wd):
        wd = wd(side_out)
    res, side_down_out = postnorm_fwd(x, [(act, wd)], g_post, mod, i0 + 2, 0.5, nb, f"{tag}_down_postnorm",
                                      target=target, side=side_down), None
    if side_down is not None:
        res, side_down_out = res
    out, y0 = (res[0] if target is None else tuple(res[:2])), res[-1]
    return out, (x, h, gate, up, act, y0), wd, side_out, side_down_out


def _ffn_backward(dxo, saved, mod, g_pre, g_post, wg, wu, wd, i0, nb, tag, side=None, chain=False):
    x, h, gate, up, act, y0 = saved
    dy0, dg_post, dgate_mod = postnorm_bwd(dxo, y0, g_post, mod, i0 + 2, 0.5, nb, f"{tag}_postnorm_bwd")
    dwd = mm_tn(act, dy0, BF16, f"{tag}_dwd", rows=D_FF)
    res, side_out = ffn_down_bwd(dy0, wd, gate, up, f"{tag}_down_bwd", side=side), None
    if side is not None:
        res, side_out = res
    dgate, dup = res
    dh_pairs = [(dgate, wg), (dup, wu)]
    if chain:
        dwg, (dwd,) = mm_tn(dgate, h, BF16, f"{tag}_dwg", rows=D_FF, side=([_rows_to_blocks(dwd, D_FF)[:, None]], False))
        dwu, (dwg,) = mm_tn(dup, h, BF16, f"{tag}_dwu", rows=D_FF, side=([_rows_to_blocks(dwg, D_FF)[:, None]], False))
        (dx, dg_pre, dsc, dsh), (dwu,) = prenorm_bwd(dh_pairs, x, g_pre, mod, i0 + 1, dxo, nb, f"{tag}_dh_prenorm_bwd",
                                                     ts=DH_ROWS, side=([_rows_to_blocks(dwu, D_FF)[:, None]], False))
    else:
        dwg = mm_tn(dgate, h, BF16, f"{tag}_dwg", rows=D_FF)
        dwu = mm_tn(dup, h, BF16, f"{tag}_dwu", rows=D_FF)
        dx, dg_pre, dsc, dsh = prenorm_bwd(dh_pairs, x, g_pre, mod, i0 + 1, dxo, nb, f"{tag}_dh_prenorm_bwd", ts=DH_ROWS)
    return dx, dict(g_pre=dg_pre, g_post=dg_post, wg=dwg, wu=dwu, wd=dwd, mod=(dsh, dsc, dgate_mod)), side_out


def kernel(x, c, positions, w_ada, b_ada, g_pre_ff1, g_post_ff1, w_ff1_gate, w_ff1_up, w_ff1_down, g_pre_mix, g_post_mix, w_in, b_forget, g_out_a, g_out_b, w_out, g_pre_ff2, g_post_ff2, w_ff2_gate, w_ff2_up, w_ff2_down, loss_target, m_w_ada, m_b_ada, m_g_pre_ff1, m_g_post_ff1, m_w_ff1_gate, m_w_ff1_up, m_w_ff1_down, m_g_pre_mix, m_g_post_mix, m_w_in, m_b_forget, m_g_out_a, m_g_out_b, m_w_out, m_g_pre_ff2, m_g_post_ff2, m_w_ff2_gate, m_w_ff2_up, m_w_ff2_down, v_w_ada, v_b_ada, v_g_pre_ff1, v_g_post_ff1, v_w_ff1_gate, v_w_ff1_up, v_w_ff1_down, v_g_pre_mix, v_g_post_mix, v_w_in, v_b_forget, v_g_out_a, v_g_out_b, v_w_out, v_g_pre_ff2, v_g_post_ff2, v_w_ff2_gate, v_w_ff2_up, v_w_ff2_down):
    weights = dict(w_ada=w_ada, b_ada=b_ada, g_pre_ff1=g_pre_ff1, g_post_ff1=g_post_ff1, w_ff1_gate=w_ff1_gate,
                   w_ff1_up=w_ff1_up, w_ff1_down=w_ff1_down, g_pre_mix=g_pre_mix, g_post_mix=g_post_mix, w_in=w_in,
                   b_forget=b_forget, g_out_a=g_out_a, g_out_b=g_out_b, w_out=w_out, g_pre_ff2=g_pre_ff2,
                   g_post_ff2=g_post_ff2, w_ff2_gate=w_ff2_gate, w_ff2_up=w_ff2_up, w_ff2_down=w_ff2_down)
    mom_m = dict(w_ada=m_w_ada, b_ada=m_b_ada, g_pre_ff1=m_g_pre_ff1, g_post_ff1=m_g_post_ff1, w_ff1_gate=m_w_ff1_gate,
                 w_ff1_up=m_w_ff1_up, w_ff1_down=m_w_ff1_down, g_pre_mix=m_g_pre_mix, g_post_mix=m_g_post_mix,
                 w_in=m_w_in, b_forget=m_b_forget, g_out_a=m_g_out_a, g_out_b=m_g_out_b, w_out=m_w_out,
                 g_pre_ff2=m_g_pre_ff2, g_post_ff2=m_g_post_ff2, w_ff2_gate=m_w_ff2_gate, w_ff2_up=m_w_ff2_up,
                 w_ff2_down=m_w_ff2_down)
    mom_v = dict(w_ada=v_w_ada, b_ada=v_b_ada, g_pre_ff1=v_g_pre_ff1, g_post_ff1=v_g_post_ff1, w_ff1_gate=v_w_ff1_gate,
                 w_ff1_up=v_w_ff1_up, w_ff1_down=v_w_ff1_down, g_pre_mix=v_g_pre_mix, g_post_mix=v_g_post_mix,
                 w_in=v_w_in, b_forget=v_b_forget, g_out_a=v_g_out_a, g_out_b=v_g_out_b, w_out=v_w_out,
                 g_pre_ff2=v_g_pre_ff2, g_post_ff2=v_g_post_ff2, w_ff2_gate=v_w_ff2_gate, w_ff2_up=v_w_ff2_up,
                 w_ff2_down=v_w_ff2_down)
    order = list(weights)

    nb, s, d = x.shape
    t = nb * s
    me = _slot(_place())
    nbg = nb * N_DEV
    ada_cols = w_ada.shape[2]

    bf = lambda w: w[0].astype(BF16)
    bft = lambda w: _t(w)[0].astype(BF16)
    c_all, wg1, wu1 = all_gather([c, bft(w_ff1_gate), bft(w_ff1_up)], "gather_ff1")
    c_all = c_all.reshape(nbg, d)
    wg1, wu1 = (_rows_from_blocks(w, D_FF_PAD) for w in (wg1, wu1))

    b_cols = lax.dynamic_slice(b_ada, (0, me * ada_cols), (1, ada_cols))
    mod_cols = ada_fwd(c_all, w_ada[0], b_cols, "ada_fwd")
    (mod_all,) = all_gather([mod_cols], "gather_mod")
    mod = lax.dynamic_slice(mod_all, (0, me * nb, 0), (N_DEV, nb, ada_cols))
    mod = mod.transpose(1, 0, 2).reshape(nb, N_MOD, d)

    xf = x.reshape(t, d)
    target = loss_target.reshape(t, d)

    x1, saved1, wd1, (_, w_out_all), (w_in_all,) = _ffn_forward(
        xf, mod, g_pre_ff1, g_post_ff1, wg1, wu1, lambda got: _rows_from_blocks(got[0], D_FF_PAD), 0, nb, "ff1",
        side=([bf(w_ff1_down), bf(w_out)], True), side_down=([bft(w_in)], True))
    w_in_t = _rows_from_blocks(w_in_all)
    n_qkv = 3 * (WIDTH_A + WIDTH_B)
    w_qkv_t = w_in_t[:n_qkv]
    w_f_t = jnp.pad(w_in_t[n_qkv:], ((0, LANES - N_HEADS_B), (0, 0)))
    w_o = _rows_from_blocks(w_out_all)
    w_o_a, w_o_b = w_o[:WIDTH_A], w_o[WIDTH_A:]

    h2 = prenorm_fwd(x1, g_pre_mix, mod, 3, 4, nb, "mix_prenorm")
    tables = rope_tables(positions)
    proj = mm_rows([(h2, w_qkv_t)], True, BF16, "mix_proj", rope=(tables, 2 * WIDTH_A))
    f_logit = mm_rows([(h2, w_f_t)], True, F32, "mix_forget")
    tab_a = dilated_table(s, ATTN_TQ, ATTN_TK)
    tab_b = causal_table(s, ATTN_TQ, ATTN_TK)
    ft = f_logit[:, :N_HEADS_B].reshape(nb, s, N_HEADS_B).transpose(0, 2, 1)
    bf_col = b_forget.reshape(N_HEADS_B, 1)
    colbias = fox_gate_fwd(ft, bf_col, "fox_gate").reshape(nb, N_HEADS_B, s // ATTN_TK, 1, ATTN_TK)
    pa = WIDTH_A // LANES
    (o_a, lse_a), ff2_all = attn_fwd(
        proj, 0, proj, pa, proj, 2 * pa, tab_a, None, nb, "attn_a",
        side=([bft(w_ff2_gate), bft(w_ff2_up), bf(w_ff2_down)], True))
    wg2, wu2, wd2 = (_rows_from_blocks(w, D_FF_PAD) for w in ff2_all)
    o_b, lse_b = attn_fwd(proj, 3 * pa, proj, 4 * pa, proj, 5 * pa, tab_b, colbias, nb, "attn_b", off_diag_bias=False)
    m_a = prenorm_fwd(o_a, g_out_a, None, None, None, nb, "out_norm_a")
    m_b = prenorm_fwd(o_b, g_out_b, None, None, None, nb, "out_norm_b")
    x2, y0m = postnorm_fwd(x1, [(m_a, w_o_a), (m_b, w_o_b)], g_post_mix, mod, 5, 1.0, nb, "mix_out_postnorm")

    (dx3, loss_part), saved2 = _ffn_forward(x2, mod, g_pre_ff2, g_post_ff2, wg2, wu2, wd2, 6, nb, "ff2", target=target)[:2]
    loss = lax.psum(loss_part[0, 0], ("x", "y", "c"))

    dx2, gr2, _ = _ffn_backward(dx3, saved2, mod, g_pre_ff2, g_post_ff2, wg2, wu2, wd2, 6, nb, "ff2")
    ff2_blocks = [_rows_to_blocks(gr2[k], D_FF)[:, None] for k in ("wg", "wu", "wd")]

    dy0m, dg_post_mix, dgate_mix = postnorm_bwd(dx2, y0m, g_post_mix, mod, 5, 1.0, nb, "mix_postnorm_bwd")
    dw_o = mm_tn_stack([m_a, m_b], dy0m, [WIDTH_A, WIDTH_B], BF16, "mix_dwo")
    do_a, dg_out_a = prenorm_bwd([(dy0m, w_o_a.T)], o_a, g_out_a, None, None, None, nb, "out_norm_a_bwd")
    do_b, dg_out_b = prenorm_bwd([(dy0m, w_o_b.T)], o_b, g_out_b, None, None, None, nb, "out_norm_b_bwd")
    (dq_a, dk_a, dv_a), g_ff2 = attn_bwd(proj, 0, proj, pa, proj, 2 * pa, o_a, lse_a, do_a, tab_a, None, nb,
                                            "attn_a_bwd", side=(ff2_blocks, False), rope_tabs=tables)
    dq_b, dk_b, dv_b, dcb, drow = attn_bwd(proj, 3 * pa, proj, 4 * pa, proj, 5 * pa, o_b, lse_b, do_b, tab_b, colbias, nb,
                                           "attn_b_bwd", off_diag_bias=False)
    dz_t, db_forget = fox_gate_bwd(dcb.reshape(nb, N_HEADS_B, s), drow.reshape(nb, N_HEADS_B, s), ft, bf_col,
                                   "fox_gate_bwd")
    dz = jnp.pad(dz_t.transpose(0, 2, 1).reshape(t, N_HEADS_B), ((0, 0), (0, LANES - N_HEADS_B))).astype(BF16)
    pieces = [dq_a, dk_a, dv_a, dq_b, dk_b, dv_b]
    w_pieces = [w_qkv_t[i * WIDTH_A:(i + 1) * WIDTH_A] for i in range(6)]
    dh2_pairs = list(zip(pieces, w_pieces)) + [(dz, w_f_t)]
    dw_in_t = mm_tn_stack(pieces + [dz], h2, [WIDTH_A] * 6 + [N_HEADS_B], BF16, "mix_dwin")
    dx1, dg_pre_mix, dsc_mix, dsh_mix = prenorm_bwd(dh2_pairs, x1, g_pre_mix, mod, 4, dx2, nb, "mix_dh_prenorm_bwd",
                                                    ts=DH_ROWS)

    g_in = _rows_to_blocks(dw_in_t, dw_in_t.shape[0])[:, None]
    g_out = _rows_to_blocks(dw_o, d)[:, None]
    dx0, gr1, (g_in, g_out) = _ffn_backward(dx1, saved1, mod, g_pre_ff1, g_post_ff1, wg1, wu1, wd1, 0, nb, "ff1",
                                            side=([g_in, g_out], False), chain=True)
    grad_x = dx0.reshape(nb, s, d)

    dmod =jnp.concatenate(list(gr1["mod"]) + [dsh_mix, dsc_mix, dgate_mix] + list(gr2["mod"]), axis=1)
    small = _pack_small(dict(g_pre_ff1=gr1["g_pre"], g_post_ff1=gr1["g_post"], g_pre_mix=dg_pre_mix,
                             g_post_mix=dg_post_mix, g_out_a=dg_out_a, g_out_b=dg_out_b, g_pre_ff2=gr2["g_pre"],
                             g_post_ff2=gr2["g_post"], b_forget=db_forget))
    dmod_all, small_all = all_gather([dmod.reshape(nb, N_MOD * d), small], "gather_small_grads")
    dmod_all = dmod_all.reshape(nbg, N_MOD * d)

    res = {}
    def adamw_t(parts, group, n):
        return tuple(_t(r) for r in adamw(parts, group, _t(weights[n])[0], _t(mom_m[n])[0], _t(mom_v[n])[0], f"adamw_{n}"))

    res["w_ff1_gate"] = adamw_t(gr1["wg"], 0, "w_ff1_gate")
    res["w_ff1_up"] = adamw_t(gr1["wu"], 0, "w_ff1_up")
    res["w_ff2_gate"] = adamw_t(g_ff2[0], 0, "w_ff2_gate")
    res["w_ff2_up"] = adamw_t(g_ff2[1], 0, "w_ff2_up")
    res["w_ff1_down"] = adamw(gr1["wd"], 0, w_ff1_down[0], m_w_ff1_down[0], v_w_ff1_down[0], "adamw_ff1_down")
    res["w_ff2_down"] = adamw(g_ff2[2], 0, w_ff2_down[0], m_w_ff2_down[0], v_w_ff2_down[0], "adamw_ff2_down")
    res["w_in"] = adamw_t(g_in, 0, "w_in")
    res["w_out"] = adamw(g_out, 0, w_out[0], m_w_out[0], v_w_out[0], "adamw_out")
    dmod_cols = lax.dynamic_slice(dmod_all, (0, me * ada_cols), (nbg, ada_cols))
    dw_ada = ada_bwd(c_all, dmod_cols, "ada_bwd")
    res["w_ada"] = adamw(dw_ada[None, None], 0, w_ada[0], m_w_ada[0], v_w_ada[0], "adamw_ada", tr=256)
    res["b_ada"] = adamw(dmod_all[:, None, None], 0, b_ada, m_b_ada, v_b_ada, "adamw_b_ada")
    sizes = {n: weights[n].shape[1] for n in SMALL_ORDER}
    small_res = adamw(small_all[:, None], 0, _pack_small(weights), _pack_small(mom_m), _pack_small(mom_v), "adamw_small")
    small_res = [_unpack_small(r, sizes) for r in small_res]
    for n in SMALL_ORDER:
        res[n] = tuple(r[n] for r in small_res)

    outs = [loss, grad_x]
    for kind in range(4):
        for n in order:
            a = res[n][kind]
            outs.append(a.reshape(weights[n].shape))
    return tuple(outs)
```

```python
import functools

import jax
import jax.numpy as jnp
from jax import lax
from jax.experimental import pallas as pl
from jax.experimental.pallas import tpu as pltpu

F32 = jnp.float32
BF16 = jnp.bfloat16

D_MODEL = 1024
HEAD_DIM = 64
N_HEADS_A = 8
N_HEADS_B = 8
WIDTH_A = N_HEADS_A * HEAD_DIM
WIDTH_B = N_HEADS_B * HEAD_DIM
DILATED_PATTERNS = ((128, 1), (512, 4), (2048, 16))
ROT_DIM = HEAD_DIM // 4
ROPE_THETA = 500000.0
D_FF = 2752
D_FF_PAD = 2816
N_MOD = 9
EPS = 1e-6
ATTN_SCALE = HEAD_DIM ** -0.5
NEG = -1e30
N_DEV = 8
LANES = 128
HEADS_PER_STEP = LANES // HEAD_DIM

ADAM_LR = 0.001
ADAM_B1 = 0.9
ADAM_B2 = 0.999
ADAM_EPS = 1e-08
ADAM_WD = 0.01
ADAM_STEP = 10

VMEM_LIMIT = 56 * 1024 * 1024
MESH = pl.DeviceIdType.MESH

NT_DIMS = (((1,), (1,)), ((), ()))
TN_DIMS = (((0,), (0,)), ((), ()))
NN_DIMS = (((1,), (0,)), ((), ()))


def _place():
    return lax.axis_index("x"), lax.axis_index("y"), lax.axis_index("c")


def _slot(p):
    return 4 * p[0] + 2 * p[1] + p[2]


def _direct_copies(ins, outs, send_sems, recv_sems, local_sems, gather):
    x, y, c = _place()
    me = (x, y, c)
    flip = lambda v, bit: 1 - v if bit else v
    peers = [(flip(x, k & 4), flip(y, k & 2), flip(c, k & 1)) for k in range(1, N_DEV)]
    local, sends, recvs = [], [], []
    for a in range(len(ins)):
        mine = ins[a] if gather else ins[a].at[_slot(me)]
        local.append(pltpu.make_async_copy(mine, outs[a].at[_slot(me)], local_sems.at[a]))
        for k, peer in enumerate(peers):
            sems = dict(send_sem=send_sems.at[a * 7 + k], recv_sem=recv_sems.at[a * 7 + k], device_id=peer,
                        device_id_type=MESH)
            sends.append(pltpu.make_async_remote_copy(
                src_ref=ins[a] if gather else ins[a].at[_slot(peer)], dst_ref=outs[a].at[_slot(me)], **sems))
            recvs.append(pltpu.make_async_remote_copy(src_ref=mine, dst_ref=outs[a].at[_slot(peer)], **sems))
    return local, sends, recvs


def _comm_scratch(n):
    return [pltpu.SemaphoreType.DMA((7 * n,)), pltpu.SemaphoreType.DMA((7 * n,)), pltpu.SemaphoreType.DMA((n,))]


def _pcall(body, side=None, **kw):
    if side is None:
        return pl.pallas_call(body, **kw)
    arrs, gather = side
    n = len(arrs)
    grid = kw["grid"]
    in_specs = list(kw["in_specs"])
    single = not isinstance(kw["out_specs"], (list, tuple))
    out_specs = [kw["out_specs"]] if single else list(kw["out_specs"])
    out_shape = [kw["out_shape"]] if single else list(kw["out_shape"])
    scratch = list(kw.get("scratch_shapes", []))
    n_in, n_out, n_scr = len(in_specs), len(out_specs), len(scratch)
    hbm = pl.BlockSpec(memory_space=pl.ANY)

    def hosted(*refs):
        pos = [0]

        def take(k):
            pos[0] += k
            return refs[pos[0] - k:pos[0]]

        ins, s_ins, outs, s_outs, scr, sems = take(n_in), take(n), take(n_out), take(n), take(n_scr), take(3)
        ids = [pl.program_id(i) for i in range(len(grid))]
        first = functools.reduce(jnp.logical_and, [i == 0 for i in ids])
        last = functools.reduce(jnp.logical_and, [i == g - 1 for i, g in zip(ids, grid)])

        @pl.when(first)
        def _():
            local, sends, _ = _direct_copies(s_ins, s_outs, *sems, gather)
            for cp in local + sends:
                cp.start()

        body(*ins, *outs, *scr)

        @pl.when(last)
        def _():
            local, sends, recvs = _direct_copies(s_ins, s_outs, *sems, gather)
            for cp in recvs:
                cp.wait_recv()
            for cp in sends:
                cp.wait_send()
            for cp in local:
                cp.wait()

    kw.update(in_specs=in_specs + [hbm] * n, out_specs=out_specs + [hbm] * n,
              out_shape=out_shape + [jax.ShapeDtypeStruct(((N_DEV,) + a.shape) if gather else a.shape, a.dtype)
                                     for a in arrs],
              scratch_shapes=scratch + _comm_scratch(n))
    call = pl.pallas_call(hosted, **kw)

    def run(*args):
        res = call(*args, *arrs)
        main = res[0] if single else list(res[:n_out])
        return main, list(res[n_out:])

    return run


def _params(sem=None, **kw):
    if sem is not None:
        kw["dimension_semantics"] = sem
    return pltpu.CompilerParams(vmem_limit_bytes=VMEM_LIMIT, **kw)


def _rotate(xv, c, sp, sm, transpose):
    width = xv.shape[1]
    half = ROT_DIM // 2
    if transpose:
        return xv * c + pltpu.roll(xv * sp, width - half, 1) + pltpu.roll(xv * sm, half, 1)
    return xv * c + pltpu.roll(xv, half, 1) * sp + pltpu.roll(xv, width - half, 1) * sm


def mm_rows(pairs, trans_b, out_dtype, name, tm=512, side=None, rope=None):
    n = len(pairs)
    m = pairs[0][0].shape[0]
    n_out = pairs[0][1].shape[0 if trans_b else 1]
    dims = NT_DIMS if trans_b else NN_DIMS

    def body(*refs):
        o_ref = refs[-1]
        acc = None
        for a_ref, b_ref in zip(refs[:n], refs[n:2 * n]):
            d = lax.dot_general(a_ref[...], b_ref[...], dims, preferred_element_type=F32)
            acc = d if acc is None else acc + d
        if rope is None:
            o_ref[...] = acc.astype(o_ref.dtype)
        else:
            width = rope[1]
            c, sp, sm = (jnp.concatenate([r[...]] * (width // LANES), axis=1) for r in refs[2 * n:2 * n + 3])
            o_ref[:, :width] = _rotate(acc[:, :width], c, sp, sm, False).astype(o_ref.dtype)
            o_ref[:, width:] = acc[:, width:].astype(o_ref.dtype)

    in_specs = [pl.BlockSpec((tm, a.shape[1]), lambda i: (i, 0)) for a, _ in pairs]
    in_specs += [pl.BlockSpec(b.shape, lambda i: (0, 0)) for _, b in pairs]
    args = [a for a, _ in pairs] + [b for _, b in pairs]
    if rope is not None:
        in_specs += [pl.BlockSpec((tm, LANES), lambda i: (i, 0))] * 3
        args += list(rope[0])
    return _pcall(
        body, side=side, name=name, grid=(m // tm,), in_specs=in_specs,
        out_specs=pl.BlockSpec((tm, n_out), lambda i: (i, 0)),
        out_shape=jax.ShapeDtypeStruct((m, n_out), out_dtype),
        compiler_params=_params(("arbitrary",)),
    )(*args)


DH_ROWS = 256
TN_TOKENS = 2048
TN_OUT_ELEMS = 2 * 1024 * 1024


def mm_tn(a, b, out_dtype, name, side=None, rows=None):
    t, ka = a.shape
    n_out = b.shape[1]
    tk = min(TN_TOKENS, t)
    tka = ka // 2 if ka * n_out > TN_OUT_ELEMS else ka
    tn = n_out
    steps = t // tk

    def body(a_ref, b_ref, o_ref, acc_ref):
        k = pl.program_id(2)
        d = lax.dot_general(a_ref[...], b_ref[...], TN_DIMS, preferred_element_type=F32)

        @pl.when(k == 0)
        def _():
            acc_ref[...] = d

        @pl.when(k > 0)
        def _():
            acc_ref[...] += d

        @pl.when(k == steps - 1)
        def _():
            o_ref[...] = acc_ref[...].astype(o_ref.dtype)

    return _pcall(
        body, side=side, name=name, grid=(ka // tka, n_out // tn, steps),
        in_specs=[pl.BlockSpec((tk, tka), lambda i, j, k: (k, i)), pl.BlockSpec((tk, tn), lambda i, j, k: (k, j))],
        out_specs=pl.BlockSpec((tka, tn), lambda i, j, k: (i, j)),
        out_shape=jax.ShapeDtypeStruct((ka if rows is None else rows, n_out), out_dtype),
        scratch_shapes=[pltpu.VMEM((tka, tn), F32)],
        compiler_params=_params(("arbitrary", "arbitrary", "arbitrary")),
    )(a, b)


def mm_tn_stack(a_list, b, rows, out_dtype, name, tk=1024):
    t, n_out = b.shape
    tk = min(tk, t)
    steps = t // tk
    n = len(a_list)
    offs = [sum(rows[:i]) for i in range(n)]

    def body(*refs):
        a_refs, b_ref, o_ref, acc_refs = refs[:n], refs[n], refs[n + 1], refs[n + 2:]
        k = pl.program_id(0)
        bv = b_ref[...]
        for a_ref, acc_ref in zip(a_refs, acc_refs):
            d = lax.dot_general(a_ref[...], bv, TN_DIMS, preferred_element_type=F32)

            @pl.when(k == 0)
            def _(acc_ref=acc_ref, d=d):
                acc_ref[...] = d

            @pl.when(k > 0)
            def _(acc_ref=acc_ref, d=d):
                acc_ref[...] += d

        @pl.when(k == steps - 1)
        def _():
            for acc_ref, off, r in zip(acc_refs, offs, rows):
                o_ref[off:off + r, :] = acc_ref[0:r, :].astype(o_ref.dtype)

    return _pcall(
        body, name=name, grid=(steps,),
        in_specs=[pl.BlockSpec((tk, a.shape[1]), lambda k: (k, 0)) for a in a_list]
        + [pl.BlockSpec((tk, n_out), lambda k: (k, 0))],
        out_specs=pl.BlockSpec((sum(rows), n_out), lambda k: (0, 0)),
        out_shape=jax.ShapeDtypeStruct((sum(rows), n_out), out_dtype),
        scratch_shapes=[pltpu.VMEM((a.shape[1], n_out), F32) for a in a_list],
        compiler_params=_params(("arbitrary",)),
    )(*a_list, b)


def _col_chunks(width, chunk=512):
    return [slice(c, min(c + chunk, width)) for c in range(0, width, chunk)]


def _sigmoid(x):
    return 1.0 / (1.0 + jnp.exp(-x))


def ffn_up(h, wgt, wut, name, tm=256, tn=D_FF_PAD, side=None):
    t, d = h.shape
    fp = wgt.shape[0]

    def body(h_ref, wg_ref, wu_ref, g_ref, u_ref, a_ref):
        hv = h_ref[...]

        def finish(cols, g, u):
            g_ref[:, cols] = g.astype(BF16)
            u_ref[:, cols] = u.astype(BF16)
            a_ref[:, cols] = (g * _sigmoid(g) * u).astype(BF16)

        pending = None
        for cols in _col_chunks(tn):
            g = lax.dot_general(hv, wg_ref[cols, :], NT_DIMS, preferred_element_type=F32)
            u = lax.dot_general(hv, wu_ref[cols, :], NT_DIMS, preferred_element_type=F32)
            if pending is not None:
                finish(*pending)
            pending = (cols, g, u)
        finish(*pending)

    w_spec = pl.BlockSpec((tn, d), lambda j, i: (j, 0))
    o_spec = pl.BlockSpec((tm, tn), lambda j, i: (i, j))
    o_shape = jax.ShapeDtypeStruct((t, fp), BF16)
    return _pcall(
        body, side=side, name=name, grid=(fp // tn, t // tm),
        in_specs=[pl.BlockSpec((tm, d), lambda j, i: (i, 0)), w_spec, w_spec],
        out_specs=[o_spec, o_spec, o_spec], out_shape=[o_shape, o_shape, o_shape],
        compiler_params=_params(("arbitrary", "arbitrary")),
    )(h, wgt, wut)


def ffn_down_bwd(dy0, wd, gate, up, name, tm=256, tn=D_FF_PAD, side=None):
    t, d = dy0.shape
    fp = wd.shape[0]

    def body(dy_ref, wd_ref, g_ref, u_ref, dg_ref, du_ref):
        dyv = dy_ref[...]

        def finish(cols, dact):
            g = g_ref[:, cols].astype(F32)
            u = u_ref[:, cols].astype(F32)
            sg = _sigmoid(g)
            silu = g * sg
            du_ref[:, cols] = (dact * silu).astype(BF16)
            dg_ref[:, cols] = ((dact * u) * (sg + silu * (1.0 - sg))).astype(BF16)

        pending = None
        for cols in _col_chunks(tn):
            dact = lax.dot_general(dyv, wd_ref[cols, :], NT_DIMS, preferred_element_type=F32)
            if pending is not None:
                finish(*pending)
            pending = (cols, dact)
        finish(*pending)

    t_spec = pl.BlockSpec((tm, tn), lambda j, i: (i, j))
    o_shape = jax.ShapeDtypeStruct((t, fp), BF16)
    return _pcall(
        body, side=side, name=name, grid=(fp // tn, t // tm),
        in_specs=[pl.BlockSpec((tm, d), lambda j, i: (i, 0)), pl.BlockSpec((tn, d), lambda j, i: (j, 0)), t_spec, t_spec],
        out_specs=[t_spec, t_spec], out_shape=[o_shape, o_shape],
        compiler_params=_params(("arbitrary", "arbitrary")),
    )(dy0, wd, gate, up)


def _row_specs(dx, ts, ns):
    return pl.BlockSpec((ts, dx), lambda b, s: (b * ns + s, 0))


def _mod_spec():
    return pl.BlockSpec((1, N_MOD, D_MODEL), lambda b, s: (b, 0, 0))


def _vec_spec(dx):
    return pl.BlockSpec((1, dx), lambda b, s: (0, 0))


def prenorm_fwd(x, g, mod, i_shift, i_scale, nb, name, ts=1024):
    t, dx = x.shape
    ts = min(ts, t // nb)
    ns = t // nb // ts

    def body(*refs):
        if mod is None:
            x_ref, g_ref, h_ref = refs
        else:
            x_ref, g_ref, mod_ref, h_ref = refs
        xv = x_ref[...]
        r = lax.rsqrt(jnp.mean(xv * xv, axis=-1, keepdims=True) + EPS)
        h = xv * r * g_ref[...]
        if mod is not None:
            h = h * (1.0 + mod_ref[0, i_scale:i_scale + 1, :]) + mod_ref[0, i_shift:i_shift + 1, :]
        h_ref[...] = h.astype(BF16)

    in_specs = [_row_specs(dx, ts, ns), _vec_spec(dx)]
    args = [x, g]
    if mod is not None:
        in_specs.append(_mod_spec())
        args.append(mod)
    return _pcall(
        body, name=name, grid=(nb, ns), in_specs=in_specs, out_specs=_row_specs(dx, ts, ns),
        out_shape=jax.ShapeDtypeStruct((t, dx), BF16), compiler_params=_params(("arbitrary", "arbitrary")),
    )(*args)


def prenorm_bwd(dh, x, g, mod, i_scale, dres, nb, name, ts=512, side=None):
    t, dx = x.shape
    ts = min(ts, t // nb)
    ns = t // nb // ts
    has_mod = mod is not None
    has_res = dres is not None
    pairs = dh if isinstance(dh, list) else None
    n_mm = 0 if pairs is None else len(pairs)

    def body(*refs):
        refs = list(refs)
        if pairs is None:
            dhv = refs[0][...].astype(F32)
            refs = refs[1:]
        else:
            dhv = None
            for a_ref, b_ref in zip(refs[:n_mm], refs[n_mm:2 * n_mm]):
                d = jnp.dot(a_ref[...], b_ref[...], preferred_element_type=F32)
                dhv = d if dhv is None else dhv + d
            refs = refs[2 * n_mm:]
        x_ref, g_ref = refs[:2]
        pos = 2
        mod_ref = dres_ref = None
        if has_mod:
            mod_ref = refs[pos]
            pos += 1
        if has_res:
            dres_ref = refs[pos]
            pos += 1
        dx_ref, dg_ref = refs[pos], refs[pos + 1]
        b, s = pl.program_id(0), pl.program_id(1)
        xv = x_ref[...]
        gv = g_ref[...]
        r = lax.rsqrt(jnp.mean(xv * xv, axis=-1, keepdims=True) + EPS)
        xhat = xv * r
        dn = dhv
        if has_mod:
            dsc_ref, dsh_ref = refs[pos + 2], refs[pos + 3]
            dn = dhv * (1.0 + mod_ref[0, i_scale:i_scale + 1, :])
            dsc = jnp.sum(dhv * xhat * gv, axis=0, keepdims=True)[None]
            dsh = jnp.sum(dhv, axis=0, keepdims=True)[None]

            @pl.when(s == 0)
            def _():
                dsc_ref[...] = dsc
                dsh_ref[...] = dsh

            @pl.when(s > 0)
            def _():
                dsc_ref[...] += dsc
                dsh_ref[...] += dsh

        dg = jnp.sum(dn * xhat, axis=0, keepdims=True)
        first = jnp.logical_and(b == 0, s == 0)

        @pl.when(first)
        def _():
            dg_ref[...] = dg

        @pl.when(jnp.logical_not(first))
        def _():
            dg_ref[...] += dg

        dxhat = dn * gv
        dxv = r * (dxhat - xhat * jnp.mean(dxhat * xhat, axis=-1, keepdims=True))
        if has_res:
            dxv = dxv + dres_ref[...]
        dx_ref[...] = dxv

    row = _row_specs(dx, ts, ns)
    if pairs is None:
        in_specs, args = [row], [dh]
    else:
        in_specs = [_row_specs(a.shape[1], ts, ns) for a, _ in pairs]
        in_specs += [pl.BlockSpec(b.shape, lambda b_, s_: (0, 0)) for _, b in pairs]
        args = [a for a, _ in pairs] + [b for _, b in pairs]
    in_specs += [row, _vec_spec(dx)]
    args += [x, g]
    if has_mod:
        in_specs.append(_mod_spec())
        args.append(mod)
    if has_res:
        in_specs.append(row)
        args.append(dres)
    out_specs = [row, _vec_spec(dx)]
    out_shape = [jax.ShapeDtypeStruct((t, dx), F32), jax.ShapeDtypeStruct((1, dx), F32)]
    if has_mod:
        bspec = pl.BlockSpec((1, 1, dx), lambda b, s: (b, 0, 0))
        out_specs += [bspec, bspec]
        out_shape += [jax.ShapeDtypeStruct((nb, 1, dx), F32)] * 2
    return _pcall(
        body, side=side, name=name, grid=(nb, ns), in_specs=in_specs, out_specs=out_specs, out_shape=out_shape,
        compiler_params=_params(("arbitrary", "arbitrary")),
    )(*args)


def postnorm_fwd(x, pairs, g, mod, i_gate, coef, nb, name, target=None, ts=512, side=None):
    t, dx = x.shape
    with_loss = target is not None
    ts = min(ts, t // nb)
    ns = t // nb // ts
    n_mm = len(pairs)

    def body(*refs):
        yv = None
        for a_ref, b_ref in zip(refs[:n_mm], refs[n_mm:2 * n_mm]):
            d = jnp.dot(a_ref[...], b_ref[...], preferred_element_type=F32)
            yv = d if yv is None else yv + d
        refs = refs[2 * n_mm:]
        x_ref, g_ref, mod_ref = refs[:3]
        refs[-1][...] = yv
        r = lax.rsqrt(jnp.mean(yv * yv, axis=-1, keepdims=True) + EPS)
        out = x_ref[...] + (coef * mod_ref[0, i_gate:i_gate + 1, :]) * (yv * r * g_ref[...])
        if not with_loss:
            refs[3][...] = out
            return
        t_ref, dx_ref, loss_ref = refs[3:6]
        b, s = pl.program_id(0), pl.program_id(1)
        err = out - t_ref[...]
        dx_ref[...] = err * (1.0 / dx)
        part = (0.5 / dx) * jnp.sum(jnp.sum(err * err, axis=1, keepdims=True), axis=0, keepdims=True)
        first = jnp.logical_and(b == 0, s == 0)

        @pl.when(first)
        def _():
            loss_ref[...] = part

        @pl.when(jnp.logical_not(first))
        def _():
            loss_ref[...] += part

    row = _row_specs(dx, ts, ns)
    in_specs = [_row_specs(a.shape[1], ts, ns) for a, _ in pairs]
    in_specs += [pl.BlockSpec(b.shape, lambda b_, s_: (0, 0)) for _, b in pairs]
    in_specs += [row, _vec_spec(dx), _mod_spec()]
    args = [a for a, _ in pairs] + [b for _, b in pairs] + [x, g, mod]
    row_shape = jax.ShapeDtypeStruct((t, dx), F32)
    out_specs, out_shape = [row, row], [row_shape, row_shape]
    if with_loss:
        in_specs.append(row)
        args.append(target)
        out_specs = [row, pl.BlockSpec((1, 1), lambda b, s: (0, 0)), row]
        out_shape = [row_shape, jax.ShapeDtypeStruct((1, 1), F32), row_shape]
    return _pcall(
        body, side=side, name=name, grid=(nb, ns), in_specs=in_specs, out_specs=out_specs, out_shape=out_shape,
        compiler_params=_params(("arbitrary", "arbitrary")),
    )(*args)


def postnorm_bwd(dxo, y0, g, mod, i_gate, coef, nb, name, ts=1024):
    t, dx = y0.shape
    ts = min(ts, t // nb)
    ns = t // nb // ts

    def body(d_ref, y_ref, g_ref, mod_ref, dy_ref, dg_ref, dgate_ref):
        b, s = pl.program_id(0), pl.program_id(1)
        yv = y_ref[...]
        dv = d_ref[...]
        gv = g_ref[...]
        r = lax.rsqrt(jnp.mean(yv * yv, axis=-1, keepdims=True) + EPS)
        yhat = yv * r
        dgate = jnp.sum(dv * (coef * (yhat * gv)), axis=0, keepdims=True)[None]
        dyn = dv * (coef * mod_ref[0, i_gate:i_gate + 1, :])
        dg = jnp.sum(dyn * yhat, axis=0, keepdims=True)
        dyhat = dyn * gv
        dy_ref[...] = (r * (dyhat - yhat * jnp.mean(dyhat * yhat, axis=-1, keepdims=True))).astype(BF16)

        @pl.when(s == 0)
        def _():
            dgate_ref[...] = dgate

        @pl.when(s > 0)
        def _():
            dgate_ref[...] += dgate

        first = jnp.logical_and(b == 0, s == 0)

        @pl.when(first)
        def _():
            dg_ref[...] = dg

        @pl.when(jnp.logical_not(first))
        def _():
            dg_ref[...] += dg

    row = _row_specs(dx, ts, ns)
    return _pcall(
        body, name=name, grid=(nb, ns), in_specs=[row, row, _vec_spec(dx), _mod_spec()],
        out_specs=[row, _vec_spec(dx), pl.BlockSpec((1, 1, dx), lambda b, s: (b, 0, 0))],
        out_shape=[jax.ShapeDtypeStruct((t, dx), BF16), jax.ShapeDtypeStruct((1, dx), F32),
                   jax.ShapeDtypeStruct((nb, 1, dx), F32)],
        compiler_params=_params(("arbitrary", "arbitrary")),
    )(dxo, y0, g, mod)


def rope_tables(positions):
    inv_freq = ROPE_THETA ** (-jnp.arange(0, ROT_DIM, 2, dtype=F32) / ROT_DIM)
    ang = positions.astype(F32).reshape(-1, 1) * inv_freq
    cos, sin = jnp.cos(ang), jnp.sin(ang)
    half = ROT_DIM // 2
    z = lambda n: jnp.zeros((ang.shape[0], n), F32)
    c = jnp.concatenate([cos, cos, jnp.ones((ang.shape[0], HEAD_DIM - ROT_DIM), F32)], axis=1)
    sp = jnp.concatenate([z(half), sin, z(HEAD_DIM - ROT_DIM)], axis=1)
    sm = jnp.concatenate([-sin, z(HEAD_DIM - half)], axis=1)
    return tuple(jnp.tile(a, (1, HEADS_PER_STEP)) for a in (c, sp, sm))


def _scan_lanes(x, reverse):
    n = x.shape[-1]
    lane = lax.broadcasted_iota(jnp.int32, x.shape, x.ndim - 1)
    k = 1
    while k < n:
        if reverse:
            x = x + jnp.where(lane < n - k, pltpu.roll(x, n - k, x.ndim - 1), 0.0)
        else:
            x = x + jnp.where(lane >= k, pltpu.roll(x, k, x.ndim - 1), 0.0)
        k *= 2
    return x


def _log_sigmoid(z):
    return jnp.minimum(z, 0.0) - jnp.log(1.0 + jnp.exp(-jnp.abs(z)))


def fox_gate_fwd(ft, b_forget, name):
    nb, nh, s = ft.shape

    def body(f_ref, b_ref, o_ref):
        z = f_ref[0] + b_ref[...]
        o_ref[0] = -_scan_lanes(_log_sigmoid(z), False)

    spec = pl.BlockSpec((1, nh, s), lambda b: (b, 0, 0))
    return _pcall(
        body, name=name, grid=(nb,), in_specs=[spec, pl.BlockSpec((nh, 1), lambda b: (0, 0))], out_specs=spec,
        out_shape=jax.ShapeDtypeStruct((nb, nh, s), F32), compiler_params=_params(("arbitrary",)),
    )(ft, b_forget)


def fox_gate_bwd(dcb, drow, ft, b_forget, name):
    nb, nh, s = ft.shape

    def body(d_ref, r_ref, f_ref, b_ref, dz_ref, db_ref):
        b = pl.program_id(0)
        z = f_ref[0] + b_ref[...]
        dlf = _scan_lanes(r_ref[0] - d_ref[0], True)
        dz = dlf * _sigmoid(-z)
        dz_ref[0] = dz
        db = jnp.sum(dz, axis=1, keepdims=True)

        @pl.when(b == 0)
        def _():
            db_ref[...] = db

        @pl.when(b > 0)
        def _():
            db_ref[...] += db

    spec = pl.BlockSpec((1, nh, s), lambda b: (b, 0, 0))
    vec = pl.BlockSpec((nh, 1), lambda b: (0, 0))
    return _pcall(
        body, name=name, grid=(nb,), in_specs=[spec, spec, spec, vec], out_specs=[spec, vec],
        out_shape=[jax.ShapeDtypeStruct((nb, nh, s), F32), jax.ShapeDtypeStruct((nh, 1), F32)],
        compiler_params=_params(("arbitrary",)),
    )(dcb, drow, ft, b_forget)


ATTN_TQ = 512
ATTN_TK = 512
ONES_ROWS = 16


def _rows_to_cols(rows):
    tile = jnp.concatenate([jnp.broadcast_to(rw, (HEAD_DIM, rw.shape[1])) for rw in rows], axis=0)
    return tile.T


def _block_delta(s, tq, tk):
    off = jnp.arange(s // tk) - (tq // tk - 1)
    return off[:, None, None] * tk + jnp.arange(tq)[None, None, :] - jnp.arange(tk)[None, :, None]


def dilated_table(s, tq, tk):
    delta = _block_delta(s, tq, tk)
    count = jnp.zeros(delta.shape, F32)
    for window, dil in DILATED_PATTERNS:
        count = count + ((delta >= 0) & (delta <= window) & (delta % dil == 0)).astype(F32)
    return jnp.where(count > 0, jnp.log(jnp.maximum(count, 1.0)), NEG)


def causal_table(s, tq, tk):
    return jnp.where(_block_delta(s, tq, tk) >= 0, 0.0, NEG).astype(F32)


def attn_fwd(q_arr, q_off, k_arr, k_off, v_arr, v_off, table, colbias, nb, name, side=None, off_diag_bias=True):
    t = q_arr.shape[0]
    s = t // nb
    tk, tq = table.shape[1:]
    assert tq == tk, "the diagonal handling below is written for square tiles"
    nq, nk = s // tq, s // tk
    npairs = WIDTH_A // LANES
    use_cb = colbias is not None

    def body(*refs):
        refs = list(refs)
        q_ref, k_ref, v_ref, tab_ref = refs[:4]
        cb_ref = refs[4] if use_cb else None
        tail = refs[-(HEADS_PER_STEP + int(use_cb)):]
        acc_s = tail[:HEADS_PER_STEP]
        cbc_s = tail[-1] if use_cb else None
        o_ref, lse_ref, vt_s = refs[-3 - len(tail):-len(tail)]
        heads = [slice(h * HEAD_DIM, (h + 1) * HEAD_DIM) for h in range(HEADS_PER_STEP)]

        for cblk in range(nk):
            vt = v_ref[cblk * tk:(cblk + 1) * tk, :].astype(F32).T.astype(BF16)
            for h, hs in enumerate(heads):
                vt_s[cblk, h, 0:HEAD_DIM, :] = vt[hs, :]
                vt_s[cblk, h, HEAD_DIM:, :] = jnp.ones((ONES_ROWS, tk), BF16)
            if use_cb:
                cbc_s[cblk] = _rows_to_cols([cb_ref[0, h, cblk] for h in range(HEADS_PER_STEP)])

        def q_block(qi, carry):
            qs = pl.multiple_of(qi * tq, tq)
            qt_all = (q_ref[pl.ds(qs, tq), :].astype(F32) * ATTN_SCALE).T.astype(BF16)
            qts = [qt_all[hs, :] for hs in heads]
            for a in acc_s:
                a[...] = jnp.zeros_like(a)

            def tile(kb, tab, k0, klen, q0, carry):
                ks = pl.multiple_of(kb * tk + k0, klen)
                sts, out = [], []
                for h, hs in enumerate(heads):
                    st = jnp.dot(k_ref[pl.ds(ks, klen), hs], qts[h][:, q0:], preferred_element_type=F32)
                    if tab is not None:
                        st = st + tab
                    if use_cb:
                        st = st + cbc_s[kb, k0:k0 + klen, h * HEAD_DIM:h * HEAD_DIM + 1]
                    sts.append(st)
                m_old = [carry[h][:, q0:] for h in range(HEADS_PER_STEP)]
                m_new = [jnp.maximum(m_old[h], jnp.max(sts[h], axis=0, keepdims=True)) for h in range(HEADS_PER_STEP)]
                for h in range(HEADS_PER_STEP):
                    pt = jnp.exp(sts[h] - m_new[h]).astype(BF16)
                    acc_s[h][:, q0:] = (jnp.exp(m_old[h] - m_new[h]) * acc_s[h][:, q0:]
                                        + jnp.dot(vt_s[kb, h, :, k0:k0 + klen], pt, preferred_element_type=F32))
                    out.append(m_new[h] if q0 == 0 else jnp.concatenate([carry[h][:, :q0], m_new[h]], axis=1))
                return tuple(out)

            fin = lax.fori_loop(0, qi, lambda kb, c: tile(kb, tab_ref[qi - kb] if off_diag_bias else None, 0, tk, 0, c),
                                tuple(jnp.full((1, tq), NEG, F32) for _ in heads))
            half = tk // 2
            fin = tile(qi, tab_ref[0, 0:half, :], 0, half, 0, fin)
            fin = tile(qi, tab_ref[0, half:, half:], half, half, half, fin)
            outs = []
            for h in range(HEADS_PER_STEP):
                l = acc_s[h][HEAD_DIM:HEAD_DIM + 1, :]
                outs.append(acc_s[h][0:HEAD_DIM, :] / l)
                lse_ref[0, h, qi] = fin[h] + jnp.log(l)
            o_ref[pl.ds(qs, tq), :] = jnp.concatenate(outs, axis=0).T
            return carry

        lax.fori_loop(0, nq, q_block, 0)

    def seq_spec(off):
        return pl.BlockSpec((s, LANES), lambda b, j: (b, off + j))

    in_specs = [seq_spec(q_off), seq_spec(k_off), seq_spec(v_off), pl.BlockSpec(table.shape, lambda b, j: (0, 0, 0))]
    args = [q_arr, k_arr, v_arr, table]
    if use_cb:
        in_specs.append(pl.BlockSpec((1, HEADS_PER_STEP, nk, 1, tk), lambda b, j: (b, j, 0, 0, 0)))
        args.append(colbias)
    n_heads = npairs * HEADS_PER_STEP
    return _pcall(
        body, side=side, name=name, grid=(nb, npairs), in_specs=in_specs,
        out_specs=[seq_spec(0), pl.BlockSpec((1, HEADS_PER_STEP, nq, 1, tq), lambda b, j: (b, j, 0, 0, 0))],
        out_shape=[jax.ShapeDtypeStruct((t, npairs * LANES), F32), jax.ShapeDtypeStruct((nb, n_heads, nq, 1, tq), F32)],
        scratch_shapes=[pltpu.VMEM((nk, HEADS_PER_STEP, HEAD_DIM + ONES_ROWS, tk), BF16)]
        + [pltpu.VMEM((HEAD_DIM + ONES_ROWS, tq), F32)] * HEADS_PER_STEP
        + ([pltpu.VMEM((nk, tk, LANES), F32)] if use_cb else []),
        compiler_params=_params(("arbitrary", "arbitrary")),
    )(*args)


def attn_bwd(q_arr, q_off, k_arr, k_off, v_arr, v_off, o_arr, lse_arr, do_arr, table, colbias, nb, name, side=None,
             rope_tabs=None, off_diag_bias=True):
    t = q_arr.shape[0]
    s = t // nb
    tk, tq = table.shape[1:]
    assert tq == tk, "the diagonal handling below is written for square tiles"
    nq, nk = s // tq, s // tk
    npairs = WIDTH_A // LANES
    use_cb = colbias is not None

    def body(*refs):
        refs = list(refs)
        q_ref, k_ref, v_ref, o_ref, lse_ref, do_ref, tab_ref = refs[:7]
        pos = 7
        cb_ref = None
        if use_cb:
            cb_ref = refs[pos]
            pos += 1
        rope_refs = None
        if rope_tabs is not None:
            rope_refs = refs[pos:pos + 3]
            pos += 3
        dq_ref, dk_ref, dv_ref = refs[pos:pos + 3]
        pos += 3
        dcb_ref = drow_ref = None
        if use_cb:
            dcb_ref, drow_ref = refs[pos:pos + 2]
            pos += 2
        kt_s, dkt_s, dvt_s = refs[pos:pos + 3]
        dqt_s = refs[pos + 3:pos + 3 + HEADS_PER_STEP]
        dcb_s, cbc_s = refs[pos + 3 + HEADS_PER_STEP:pos + 5 + HEADS_PER_STEP] if use_cb else (None, None)

        heads = [slice(h * HEAD_DIM, (h + 1) * HEAD_DIM) for h in range(HEADS_PER_STEP)]
        for cblk in range(nk):
            kt_s[cblk] = k_ref[cblk * tk:(cblk + 1) * tk, :].astype(F32).T.astype(BF16)
        dkt_s[...] = jnp.zeros_like(dkt_s)
        dvt_s[...] = jnp.zeros_like(dvt_s)
        if use_cb:
            dcb_s[...] = jnp.zeros_like(dcb_s)
            for cblk in range(nk):
                cbc_s[cblk] = _rows_to_cols([cb_ref[0, h, cblk] for h in range(HEADS_PER_STEP)])
        ones = jnp.ones((8, HEAD_DIM), BF16)

        def q_loop(qi, carry):
            qs = pl.multiple_of(qi * tq, tq)
            q_all = (q_ref[pl.ds(qs, tq), :].astype(F32) * ATTN_SCALE)
            do_all = do_ref[pl.ds(qs, tq), :]
            qt_all = q_all.T.astype(BF16)
            dot_all = do_all.T.astype(BF16)
            qt, dot, lse, dsum = [], [], [], []
            for h, hs in enumerate(heads):
                qt.append(qt_all[hs, :])
                dot.append(dot_all[hs, :])
                lse.append(lse_ref[0, h, qi])
                prod = do_all[:, hs] * o_ref[pl.ds(qs, tq), hs]
                hi = prod.astype(BF16)
                lo = (prod - hi.astype(F32)).astype(BF16)
                dsum.append((lax.dot_general(ones, hi, NT_DIMS, preferred_element_type=F32)
                             + lax.dot_general(ones, lo, NT_DIMS, preferred_element_type=F32))[0:1, :])
            for a in dqt_s:
                a[...] = jnp.zeros_like(a)

            def tile(kb, tab, k0, klen, q0, drow):
                ks = pl.multiple_of(kb * tk + k0, klen)
                keys = slice(k0, k0 + klen)
                sts, dpts, out = [], [], []
                for h, hs in enumerate(heads):
                    st = jnp.dot(k_ref[pl.ds(ks, klen), hs], qt[h][:, q0:], preferred_element_type=F32)
                    if tab is not None:
                        st = st + tab
                    if use_cb:
                        st = st + cbc_s[kb, keys, h * HEAD_DIM:h * HEAD_DIM + 1]
                    sts.append(st)
                    dpts.append(jnp.dot(v_ref[pl.ds(ks, klen), hs], dot[h][:, q0:], preferred_element_type=F32))
                for h, hs in enumerate(heads):
                    pt = jnp.exp(sts[h] - lse[h][:, q0:])
                    dst = pt * (dpts[h] - dsum[h][:, q0:])
                    dst_b = dst.astype(BF16)
                    dvt_s[h, kb, :, keys] += lax.dot_general(dot[h][:, q0:], pt.astype(BF16), NT_DIMS,
                                                             preferred_element_type=F32)
                    dkt_s[h, kb, :, keys] += lax.dot_general(qt[h][:, q0:], dst_b, NT_DIMS, preferred_element_type=F32)
                    dqt_s[h][:, q0:] += jnp.dot(kt_s[kb, hs, keys], dst_b, preferred_element_type=F32)
                    if use_cb:
                        dcb_s[h, pl.ds(ks, klen), :] += jnp.sum(dst, axis=1, keepdims=True)
                        dr = drow[h][:, q0:] + jnp.sum(dst, axis=0, keepdims=True)
                        out.append(dr if q0 == 0 else jnp.concatenate([drow[h][:, :q0], dr], axis=1))
                    else:
                        out.append(drow[h])
                return tuple(out)

            drow = lax.fori_loop(0, qi, lambda kb, c: tile(kb, tab_ref[qi - kb] if off_diag_bias else None, 0, tk, 0, c),
                                 tuple(jnp.zeros((1, tq), F32) for _ in heads))
            half = tk // 2
            drow = tile(qi, tab_ref[0, 0:half, :], 0, half, 0, drow)
            drow = tile(qi, tab_ref[0, half:, half:], half, half, half, drow)
            dq = (jnp.concatenate([a[...] for a in dqt_s], axis=0) * ATTN_SCALE).T
            if rope_refs is not None:
                dq = _rotate(dq, *[coef[pl.ds(qs, tq), :] for coef in rope_refs], True)
            dq_ref[pl.ds(qs, tq), :] = dq.astype(dq_ref.dtype)
            if use_cb:
                for h in range(HEADS_PER_STEP):
                    drow_ref[0, h, qi] = drow[h]
            return carry

        lax.fori_loop(0, nq, q_loop, 0)
        for cblk in range(nk):
            rows = slice(cblk * tk, (cblk + 1) * tk)
            dk = jnp.concatenate([dkt_s[h, cblk] for h in range(HEADS_PER_STEP)], axis=0).T
            if rope_refs is not None:
                dk = _rotate(dk, *[coef[rows, :] for coef in rope_refs], True)
            dk_ref[rows, :] = dk.astype(dk_ref.dtype)
            dv_ref[rows, :] = jnp.concatenate([dvt_s[h, cblk] for h in range(HEADS_PER_STEP)], axis=0).T.astype(dv_ref.dtype)
            if use_cb:
                for h in range(HEADS_PER_STEP):
                    dcb_ref[0, h, cblk] = jnp.broadcast_to(dcb_s[h, rows, :], (tk, LANES)).T[0:1, :]

    def seq_spec(off):
        return pl.BlockSpec((s, LANES), lambda b, j: (b, off + j))

    row_spec = pl.BlockSpec((1, HEADS_PER_STEP, nq, 1, tq), lambda b, j: (b, j, 0, 0, 0))
    in_specs = [seq_spec(q_off), seq_spec(k_off), seq_spec(v_off), seq_spec(0), row_spec, seq_spec(0),
                pl.BlockSpec(table.shape, lambda b, j: (0, 0, 0))]
    args = [q_arr, k_arr, v_arr, o_arr, lse_arr, do_arr, table]
    width = npairs * LANES
    out_specs = [seq_spec(0)] * 3
    out_shape = [jax.ShapeDtypeStruct((t, width), BF16)] * 3
    scratch = [pltpu.VMEM((nk, LANES, tk), BF16), pltpu.VMEM((HEADS_PER_STEP, nk, HEAD_DIM, tk), F32),
               pltpu.VMEM((HEADS_PER_STEP, nk, HEAD_DIM, tk), F32)] + [pltpu.VMEM((HEAD_DIM, tq), F32)] * HEADS_PER_STEP
    if use_cb:
        cb_spec = pl.BlockSpec((1, HEADS_PER_STEP, nk, 1, tk), lambda b, j: (b, j, 0, 0, 0))
        in_specs.append(cb_spec)
        args.append(colbias)
    if rope_tabs is not None:
        in_specs += [pl.BlockSpec((s, LANES), lambda b, j: (b, 0))] * 3
        args += list(rope_tabs)
    if use_cb:
        out_specs += [cb_spec, row_spec]
        out_shape += [jax.ShapeDtypeStruct(colbias.shape, F32), jax.ShapeDtypeStruct(lse_arr.shape, F32)]
        scratch += [pltpu.VMEM((HEADS_PER_STEP, s, 1), F32), pltpu.VMEM((nk, tk, LANES), F32)]
    return _pcall(
        body, side=side, name=name, grid=(nb, npairs), in_specs=in_specs, out_specs=out_specs, out_shape=out_shape,
        scratch_shapes=scratch, compiler_params=_params(("arbitrary", "arbitrary")),
    )(*args)


def ada_fwd(c_all, w_ada, b_cols, name):
    def body(c_ref, w_ref, b_ref, o_ref):
        cv = c_ref[...]
        sc = (cv * _sigmoid(cv)).astype(BF16)
        o_ref[...] = jnp.dot(sc, w_ref[...].astype(BF16), preferred_element_type=F32) + b_ref[...]

    return _pcall(body, name=name, out_shape=jax.ShapeDtypeStruct((c_all.shape[0], w_ada.shape[1]), F32),
                  compiler_params=_params())(c_all, w_ada, b_cols)


def ada_bwd(c_all, dmod_cols, name):
    def body(c_ref, d_ref, o_ref):
        cv = c_ref[...]
        sc = (cv * _sigmoid(cv)).astype(BF16)
        o_ref[...] = lax.dot_general(sc, d_ref[...].astype(BF16), TN_DIMS, preferred_element_type=F32)

    return _pcall(body, name=name, out_shape=jax.ShapeDtypeStruct((c_all.shape[1], dmod_cols.shape[1]), F32),
                  compiler_params=_params())(c_all, dmod_cols)


def adamw(parts, group, w, m, v, name, tr=None):
    n = parts.shape[0]
    r, c = w.shape
    tr = r if tr is None else tr
    c1 = 1.0 - ADAM_B1 ** ADAM_STEP
    c2 = 1.0 - ADAM_B2 ** ADAM_STEP

    def body(p_ref, w_ref, m_ref, v_ref, g_ref, d_ref, nm_ref, nv_ref):
        g = p_ref[0, 0].astype(F32)
        for i in range(1, n):
            g = g + p_ref[i, 0].astype(F32)
        wv = w_ref[...]
        nm = ADAM_B1 * m_ref[...] + (1.0 - ADAM_B1) * g
        nv = ADAM_B2 * v_ref[...] + (1.0 - ADAM_B2) * (g * g)
        g_ref[...] = g
        nm_ref[...] = nm
        nv_ref[...] = nv
        d_ref[...] = -ADAM_LR * ((nm / c1) / (jnp.sqrt(nv / c2) + ADAM_EPS) + ADAM_WD * wv)

    spec = pl.BlockSpec((tr, c), lambda i: (i, 0))
    shape = jax.ShapeDtypeStruct((r, c), F32)
    return _pcall(
        body, name=name, grid=(r // tr,),
        in_specs=[pl.BlockSpec((n, 1, tr, c), lambda i: (0, group, i, 0)), spec, spec, spec],
        out_specs=[spec] * 4, out_shape=[shape] * 4, compiler_params=_params(("arbitrary",)),
    )(parts, w, m, v)


def all_gather(arrs, name):
    n = len(arrs)
    hbm = pl.BlockSpec(memory_space=pl.ANY)

    def body(*refs):
        ins, outs = refs[:n], refs[n:2 * n]
        send_sems, recv_sems, local_sems = refs[2 * n:]
        x, y, c = _place()
        me, sibling = (x, y, c), (x, y, 1 - c)
        chips = [(1 - x, y), (x, 1 - y), (1 - x, 1 - y)]

        def copy(a, k, block, to, src=None):
            dst = outs[a].at[_slot(block)]
            return pltpu.make_async_remote_copy(
                src_ref=dst if src is None else src, dst_ref=dst, send_sem=send_sems.at[a * 7 + k],
                recv_sem=recv_sems.at[a * 7 + k], device_id=to, device_id_type=MESH)

        mine = [pltpu.make_async_copy(ins[a], outs[a].at[_slot(me)], local_sems.at[a]) for a in range(n)]
        for cp in mine:
            cp.start()
        first = []
        for a in range(n):
            first.append(copy(a, 0, me, sibling, src=ins[a]))
            first += [copy(a, 1 + j, me, (*chip, c), src=ins[a]) for j, chip in enumerate(chips)]
        for cp in first:
            cp.start()
        passed = []
        for a in range(n):
            for j, chip in enumerate(chips):
                copy(a, 1 + j, (*chip, c), me).wait_recv()
                cp = copy(a, 4 + j, (*chip, c), sibling)
                cp.start()
                passed.append(cp)
        for a in range(n):
            copy(a, 0, sibling, me).wait_recv()
            for j, chip in enumerate(chips):
                copy(a, 4 + j, (*chip, 1 - c), me).wait_recv()
        for cp in first + passed:
            cp.wait_send()
        for cp in mine:
            cp.wait()

    return _pcall(
        body, name=name, in_specs=[hbm] * n, out_specs=[hbm] * n,
        out_shape=[jax.ShapeDtypeStruct((N_DEV,) + a.shape, a.dtype) for a in arrs],
        scratch_shapes=[pltpu.SemaphoreType.DMA((7 * n,)), pltpu.SemaphoreType.DMA((7 * n,)),
                        pltpu.SemaphoreType.DMA((n,))],
        compiler_params=pltpu.CompilerParams(has_side_effects=True),
    )(*arrs)


def _t(w):
    return jnp.swapaxes(w, -1, -2)


def _rows_from_blocks(blocks, pad_to=None):
    full = blocks.reshape(-1, blocks.shape[2])
    if pad_to is not None and pad_to > full.shape[0]:
        full = jnp.pad(full, ((0, pad_to - full.shape[0]), (0, 0)))
    return full


def _rows_to_blocks(full, nrows):
    return full[:nrows].reshape(N_DEV, nrows // N_DEV, full.shape[1])


SMALL_ORDER = ("g_pre_ff1", "g_post_ff1", "g_pre_mix", "g_post_mix", "g_out_a", "g_out_b", "g_pre_ff2", "g_post_ff2",
               "b_forget")


def _pack_small(vals):
    rows = []
    for name in SMALL_ORDER:
        v = vals[name].reshape(1, -1)
        if v.shape[1] % LANES:
            v = jnp.pad(v, ((0, 0), (0, LANES - v.shape[1] % LANES)))
        rows.append(v)
    return jnp.concatenate(rows, axis=1)


def _unpack_small(row, sizes):
    out, pos = {}, 0
    for name in SMALL_ORDER:
        n = sizes[name]
        out[name] = row[:, pos:pos + n]
        pos += -(-n // LANES) * LANES
    return out


def _ffn_forward(x, mod, g_pre, g_post, wg, wu, wd, i0, nb, tag, target=None, side=None, side_down=None):
    h = prenorm_fwd(x, g_pre, mod, i0, i0 + 1, nb, f"{tag}_prenorm")
    res, side_out = ffn_up(h, wg, wu, f"{tag}_up", side=side), None
    if side is not None:
        res, side_out = res
    gate, up, act = res
    if callable(wd):
        wd = wd(side_out)
    res, side_down_out = postnorm_fwd(x, [(act, wd)], g_post, mod, i0 + 2, 0.5, nb, f"{tag}_down_postnorm",
                                      target=target, side=side_down), None
    if side_down is not None:
        res, side_down_out = res
    out, y0 = (res[0] if target is None else tuple(res[:2])), res[-1]
    return out, (x, h, gate, up, act, y0), wd, side_out, side_down_out


def _ffn_backward(dxo, saved, mod, g_pre, g_post, wg, wu, wd, i0, nb, tag, side=None, chain=False):
    x, h, gate, up, act, y0 = saved
    dy0, dg_post, dgate_mod = postnorm_bwd(dxo, y0, g_post, mod, i0 + 2, 0.5, nb, f"{tag}_postnorm_bwd")
    dwd = mm_tn(act, dy0, BF16, f"{tag}_dwd", rows=D_FF)
    res, side_out = ffn_down_bwd(dy0, wd, gate, up, f"{tag}_down_bwd", side=side), None
    if side is not None:
        res, side_out = res
    dgate, dup = res
    dh_pairs = [(dgate, wg), (dup, wu)]
    if chain:
        dwg, (dwd,) = mm_tn(dgate, h, BF16, f"{tag}_dwg", rows=D_FF, side=([_rows_to_blocks(dwd, D_FF)[:, None]], False))
        dwu, (dwg,) = mm_tn(dup, h, BF16, f"{tag}_dwu", rows=D_FF, side=([_rows_to_blocks(dwg, D_FF)[:, None]], False))
        (dx, dg_pre, dsc, dsh), (dwu,) = prenorm_bwd(dh_pairs, x, g_pre, mod, i0 + 1, dxo, nb, f"{tag}_dh_prenorm_bwd",
                                                     ts=DH_ROWS, side=([_rows_to_blocks(dwu, D_FF)[:, None]], False))
    else:
        dwg = mm_tn(dgate, h, BF16, f"{tag}_dwg", rows=D_FF)
        dwu = mm_tn(dup, h, BF16, f"{tag}_dwu", rows=D_FF)
        dx, dg_pre, dsc, dsh = prenorm_bwd(dh_pairs, x, g_pre, mod, i0 + 1, dxo, nb, f"{tag}_dh_prenorm_bwd", ts=DH_ROWS)
    return dx, dict(g_pre=dg_pre, g_post=dg_post, wg=dwg, wu=dwu, wd=dwd, mod=(dsh, dsc, dgate_mod)), side_out


def kernel(x, c, positions, w_ada, b_ada, g_pre_ff1, g_post_ff1, w_ff1_gate, w_ff1_up, w_ff1_down, g_pre_mix, g_post_mix, w_in, b_forget, g_out_a, g_out_b, w_out, g_pre_ff2, g_post_ff2, w_ff2_gate, w_ff2_up, w_ff2_down, loss_target, m_w_ada, m_b_ada, m_g_pre_ff1, m_g_post_ff1, m_w_ff1_gate, m_w_ff1_up, m_w_ff1_down, m_g_pre_mix, m_g_post_mix, m_w_in, m_b_forget, m_g_out_a, m_g_out_b, m_w_out, m_g_pre_ff2, m_g_post_ff2, m_w_ff2_gate, m_w_ff2_up, m_w_ff2_down, v_w_ada, v_b_ada, v_g_pre_ff1, v_g_post_ff1, v_w_ff1_gate, v_w_ff1_up, v_w_ff1_down, v_g_pre_mix, v_g_post_mix, v_w_in, v_b_forget, v_g_out_a, v_g_out_b, v_w_out, v_g_pre_ff2, v_g_post_ff2, v_w_ff2_gate, v_w_ff2_up, v_w_ff2_down):
    weights = dict(w_ada=w_ada, b_ada=b_ada, g_pre_ff1=g_pre_ff1, g_post_ff1=g_post_ff1, w_ff1_gate=w_ff1_gate,
                   w_ff1_up=w_ff1_up, w_ff1_down=w_ff1_down, g_pre_mix=g_pre_mix, g_post_mix=g_post_mix, w_in=w_in,
                   b_forget=b_forget, g_out_a=g_out_a, g_out_b=g_out_b, w_out=w_out, g_pre_ff2=g_pre_ff2,
                   g_post_ff2=g_post_ff2, w_ff2_gate=w_ff2_gate, w_ff2_up=w_ff2_up, w_ff2_down=w_ff2_down)
    mom_m = dict(w_ada=m_w_ada, b_ada=m_b_ada, g_pre_ff1=m_g_pre_ff1, g_post_ff1=m_g_post_ff1, w_ff1_gate=m_w_ff1_gate,
                 w_ff1_up=m_w_ff1_up, w_ff1_down=m_w_ff1_down, g_pre_mix=m_g_pre_mix, g_post_mix=m_g_post_mix,
                 w_in=m_w_in, b_forget=m_b_forget, g_out_a=m_g_out_a, g_out_b=m_g_out_b, w_out=m_w_out,
                 g_pre_ff2=m_g_pre_ff2, g_post_ff2=m_g_post_ff2, w_ff2_gate=m_w_ff2_gate, w_ff2_up=m_w_ff2_up,
                 w_ff2_down=m_w_ff2_down)
    mom_v = dict(w_ada=v_w_ada, b_ada=v_b_ada, g_pre_ff1=v_g_pre_ff1, g_post_ff1=v_g_post_ff1, w_ff1_gate=v_w_ff1_gate,
                 w_ff1_up=v_w_ff1_up, w_ff1_down=v_w_ff1_down, g_pre_mix=v_g_pre_mix, g_post_mix=v_g_post_mix,
                 w_in=v_w_in, b_forget=v_b_forget, g_out_a=v_g_out_a, g_out_b=v_g_out_b, w_out=v_w_out,
                 g_pre_ff2=v_g_pre_ff2, g_post_ff2=v_g_post_ff2, w_ff2_gate=v_w_ff2_gate, w_ff2_up=v_w_ff2_up,
                 w_ff2_down=v_w_ff2_down)
    order = list(weights)

    nb, s, d = x.shape
    t = nb * s
    me = _slot(_place())
    nbg = nb * N_DEV
    ada_cols = w_ada.shape[2]

    bf = lambda w: w[0].astype(BF16)
    bft = lambda w: _t(w)[0].astype(BF16)
    c_all, wg1, wu1 = all_gather([c, bft(w_ff1_gate), bft(w_ff1_up)], "gather_ff1")
    c_all = c_all.reshape(nbg, d)
    wg1, wu1 = (_rows_from_blocks(w, D_FF_PAD) for w in (wg1, wu1))

    b_cols = lax.dynamic_slice(b_ada, (0, me * ada_cols), (1, ada_cols))
    mod_cols = ada_fwd(c_all, w_ada[0], b_cols, "ada_fwd")
    (mod_all,) = all_gather([mod_cols], "gather_mod")
    mod = lax.dynamic_slice(mod_all, (0, me * nb, 0), (N_DEV, nb, ada_cols))
    mod = mod.transpose(1, 0, 2).reshape(nb, N_MOD, d)

    xf = x.reshape(t, d)
    target = loss_target.reshape(t, d)

    x1, saved1, wd1, (_, w_out_all), (w_in_all,) = _ffn_forward(
        xf, mod, g_pre_ff1, g_post_ff1, wg1, wu1, lambda got: _rows_from_blocks(got[0], D_FF_PAD), 0, nb, "ff1",
        side=([bf(w_ff1_down), bf(w_out)], True), side_down=([bft(w_in)], True))
    w_in_t = _rows_from_blocks(w_in_all)
    n_qkv = 3 * (WIDTH_A + WIDTH_B)
    w_qkv_t = w_in_t[:n_qkv]
    w_f_t = jnp.pad(w_in_t[n_qkv:], ((0, LANES - N_HEADS_B), (0, 0)))
    w_o = _rows_from_blocks(w_out_all)
    w_o_a, w_o_b = w_o[:WIDTH_A], w_o[WIDTH_A:]

    h2 = prenorm_fwd(x1, g_pre_mix, mod, 3, 4, nb, "mix_prenorm")
    tables = rope_tables(positions)
    proj = mm_rows([(h2, w_qkv_t)], True, BF16, "mix_proj", rope=(tables, 2 * WIDTH_A))
    f_logit = mm_rows([(h2, w_f_t)], True, F32, "mix_forget")
    tab_a = dilated_table(s, ATTN_TQ, ATTN_TK)
    tab_b = causal_table(s, ATTN_TQ, ATTN_TK)
    ft = f_logit[:, :N_HEADS_B].reshape(nb, s, N_HEADS_B).transpose(0, 2, 1)
    bf_col = b_forget.reshape(N_HEADS_B, 1)
    colbias = fox_gate_fwd(ft, bf_col, "fox_gate").reshape(nb, N_HEADS_B, s // ATTN_TK, 1, ATTN_TK)
    pa = WIDTH_A // LANES
    (o_a, lse_a), ff2_all = attn_fwd(
        proj, 0, proj, pa, proj, 2 * pa, tab_a, None, nb, "attn_a",
        side=([bft(w_ff2_gate), bft(w_ff2_up), bf(w_ff2_down)], True))
    wg2, wu2, wd2 = (_rows_from_blocks(w, D_FF_PAD) for w in ff2_all)
    o_b, lse_b = attn_fwd(proj, 3 * pa, proj, 4 * pa, proj, 5 * pa, tab_b, colbias, nb, "attn_b", off_diag_bias=False)
    m_a = prenorm_fwd(o_a, g_out_a, None, None, None, nb, "out_norm_a")
    m_b = prenorm_fwd(o_b, g_out_b, None, None, None, nb, "out_norm_b")
    x2, y0m = postnorm_fwd(x1, [(m_a, w_o_a), (m_b, w_o_b)], g_post_mix, mod, 5, 1.0, nb, "mix_out_postnorm")

    (dx3, loss_part), saved2 = _ffn_forward(x2, mod, g_pre_ff2, g_post_ff2, wg2, wu2, wd2, 6, nb, "ff2", target=target)[:2]
    loss = lax.psum(loss_part[0, 0], ("x", "y", "c"))

    dx2, gr2, _ = _ffn_backward(dx3, saved2, mod, g_pre_ff2, g_post_ff2, wg2, wu2, wd2, 6, nb, "ff2")
    ff2_blocks = [_rows_to_blocks(gr2[k], D_FF)[:, None] for k in ("wg", "wu", "wd")]

    dy0m, dg_post_mix, dgate_mix = postnorm_bwd(dx2, y0m, g_post_mix, mod, 5, 1.0, nb, "mix_postnorm_bwd")
    dw_o = mm_tn_stack([m_a, m_b], dy0m, [WIDTH_A, WIDTH_B], BF16, "mix_dwo")
    do_a, dg_out_a = prenorm_bwd([(dy0m, w_o_a.T)], o_a, g_out_a, None, None, None, nb, "out_norm_a_bwd")
    do_b, dg_out_b = prenorm_bwd([(dy0m, w_o_b.T)], o_b, g_out_b, None, None, None, nb, "out_norm_b_bwd")
    (dq_a, dk_a, dv_a), g_ff2 = attn_bwd(proj, 0, proj, pa, proj, 2 * pa, o_a, lse_a, do_a, tab_a, None, nb,
                                            "attn_a_bwd", side=(ff2_blocks, False), rope_tabs=tables)
    dq_b, dk_b, dv_b, dcb, drow = attn_bwd(proj, 3 * pa, proj, 4 * pa, proj, 5 * pa, o_b, lse_b, do_b, tab_b, colbias, nb,
                                           "attn_b_bwd", off_diag_bias=False)
    dz_t, db_forget = fox_gate_bwd(dcb.reshape(nb, N_HEADS_B, s), drow.reshape(nb, N_HEADS_B, s), ft, bf_col,
                                   "fox_gate_bwd")
    dz = jnp.pad(dz_t.transpose(0, 2, 1).reshape(t, N_HEADS_B), ((0, 0), (0, LANES - N_HEADS_B))).astype(BF16)
    pieces = [dq_a, dk_a, dv_a, dq_b, dk_b, dv_b]
    w_pieces = [w_qkv_t[i * WIDTH_A:(i + 1) * WIDTH_A] for i in range(6)]
    dh2_pairs = list(zip(pieces, w_pieces)) + [(dz, w_f_t)]
    dw_in_t = mm_tn_stack(pieces + [dz], h2, [WIDTH_A] * 6 + [N_HEADS_B], BF16, "mix_dwin")
    dx1, dg_pre_mix, dsc_mix, dsh_mix = prenorm_bwd(dh2_pairs, x1, g_pre_mix, mod, 4, dx2, nb, "mix_dh_prenorm_bwd",
                                                    ts=DH_ROWS)

    g_in = _rows_to_blocks(dw_in_t, dw_in_t.shape[0])[:, None]
    g_out = _rows_to_blocks(dw_o, d)[:, None]
    dx0, gr1, (g_in, g_out) = _ffn_backward(dx1, saved1, mod, g_pre_ff1, g_post_ff1, wg1, wu1, wd1, 0, nb, "ff1",
                                            side=([g_in, g_out], False), chain=True)
    grad_x = dx0.reshape(nb, s, d)

    dmod =jnp.concatenate(list(gr1["mod"]) + [dsh_mix, dsc_mix, dgate_mix] + list(gr2["mod"]), axis=1)
    small = _pack_small(dict(g_pre_ff1=gr1["g_pre"], g_post_ff1=gr1["g_post"], g_pre_mix=dg_pre_mix,
                             g_post_mix=dg_post_mix, g_out_a=dg_out_a, g_out_b=dg_out_b, g_pre_ff2=gr2["g_pre"],
                             g_post_ff2=gr2["g_post"], b_forget=db_forget))
    dmod_all, small_all = all_gather([dmod.reshape(nb, N_MOD * d), small], "gather_small_grads")
    dmod_all = dmod_all.reshape(nbg, N_MOD * d)

    res = {}
    def adamw_t(parts, group, n):
        return tuple(_t(r) for r in adamw(parts, group, _t(weights[n])[0], _t(mom_m[n])[0], _t(mom_v[n])[0], f"adamw_{n}"))

    res["w_ff1_gate"] = adamw_t(gr1["wg"], 0, "w_ff1_gate")
    res["w_ff1_up"] = adamw_t(gr1["wu"], 0, "w_ff1_up")
    res["w_ff2_gate"] = adamw_t(g_ff2[0], 0, "w_ff2_gate")
    res["w_ff2_up"] = adamw_t(g_ff2[1], 0, "w_ff2_up")
    res["w_ff1_down"] = adamw(gr1["wd"], 0, w_ff1_down[0], m_w_ff1_down[0], v_w_ff1_down[0], "adamw_ff1_down")
    res["w_ff2_down"] = adamw(g_ff2[2], 0, w_ff2_down[0], m_w_ff2_down[0], v_w_ff2_down[0], "adamw_ff2_down")
    res["w_in"] = adamw_t(g_in, 0, "w_in")
    res["w_out"] = adamw(g_out, 0, w_out[0], m_w_out[0], v_w_out[0], "adamw_out")
    dmod_cols = lax.dynamic_slice(dmod_all, (0, me * ada_cols), (nbg, ada_cols))
    dw_ada = ada_bwd(c_all, dmod_cols, "ada_bwd")
    res["w_ada"] = adamw(dw_ada[None, None], 0, w_ada[0], m_w_ada[0], v_w_ada[0], "adamw_ada", tr=256)
    res["b_ada"] = adamw(dmod_all[:, None, None], 0, b_ada, m_b_ada, v_b_ada, "adamw_b_ada")
    sizes = {n: weights[n].shape[1] for n in SMALL_ORDER}
    small_res = adamw(small_all[:, None], 0, _pack_small(weights), _pack_small(mom_m), _pack_small(mom_v), "adamw_small")
    small_res = [_unpack_small(r, sizes) for r in small_res]
    for n in SMALL_ORDER:
        res[n] = tuple(r[n] for r in small_res)

    outs = [loss, grad_x]
    for kind in range(4):
        for n in order:
            a = res[n][kind]
            outs.append(a.reshape(weights[n].shape))
    return tuple(outs)
```

```python
import functools

import jax
import jax.numpy as jnp
from jax import lax
from jax.experimental import pallas as pl
from jax.experimental.pallas import tpu as pltpu

F32 = jnp.float32
BF16 = jnp.bfloat16

D_MODEL = 1024
HEAD_DIM = 64
N_HEADS_A = 8
N_HEADS_B = 8
WIDTH_A = N_HEADS_A * HEAD_DIM
WIDTH_B = N_HEADS_B * HEAD_DIM
DILATED_PATTERNS = ((128, 1), (512, 4), (2048, 16))
ROT_DIM = HEAD_DIM // 4
ROPE_THETA = 500000.0
D_FF = 2752
D_FF_PAD = 2816
N_MOD = 9
EPS = 1e-6
ATTN_SCALE = HEAD_DIM ** -0.5
NEG = -1e30
N_DEV = 8
LANES = 128
HEADS_PER_STEP = LANES // HEAD_DIM

ADAM_LR = 0.001
ADAM_B1 = 0.9
ADAM_B2 = 0.999
ADAM_EPS = 1e-08
ADAM_WD = 0.01
ADAM_STEP = 10

VMEM_LIMIT = 56 * 1024 * 1024
MESH = pl.DeviceIdType.MESH

NT_DIMS = (((1,), (1,)), ((), ()))
TN_DIMS = (((0,), (0,)), ((), ()))
NN_DIMS = (((1,), (0,)), ((), ()))


def _place():
    return lax.axis_index("x"), lax.axis_index("y"), lax.axis_index("c")


def _slot(p):
    return 4 * p[0] + 2 * p[1] + p[2]


def _direct_copies(ins, outs, send_sems, recv_sems, local_sems, gather):
    x, y, c = _place()
    me = (x, y, c)
    flip = lambda v, bit: 1 - v if bit else v
    peers = [(flip(x, k & 4), flip(y, k & 2), flip(c, k & 1)) for k in range(1, N_DEV)]
    local, sends, recvs = [], [], []
    for a in range(len(ins)):
        mine = ins[a] if gather else ins[a].at[_slot(me)]
        local.append(pltpu.make_async_copy(mine, outs[a].at[_slot(me)], local_sems.at[a]))
        for k, peer in enumerate(peers):
            sems = dict(send_sem=send_sems.at[a * 7 + k], recv_sem=recv_sems.at[a * 7 + k], device_id=peer,
                        device_id_type=MESH)
            sends.append(pltpu.make_async_remote_copy(
                src_ref=ins[a] if gather else ins[a].at[_slot(peer)], dst_ref=outs[a].at[_slot(me)], **sems))
            recvs.append(pltpu.make_async_remote_copy(src_ref=mine, dst_ref=outs[a].at[_slot(peer)], **sems))
    return local, sends, recvs


def _comm_scratch(n):
    return [pltpu.SemaphoreType.DMA((7 * n,)), pltpu.SemaphoreType.DMA((7 * n,)), pltpu.SemaphoreType.DMA((n,))]


def _pcall(body, side=None, **kw):
    if side is None:
        return pl.pallas_call(body, **kw)
    arrs, gather = side
    n = len(arrs)
    grid = kw["grid"]
    in_specs = list(kw["in_specs"])
    single = not isinstance(kw["out_specs"], (list, tuple))
    out_specs = [kw["out_specs"]] if single else list(kw["out_specs"])
    out_shape = [kw["out_shape"]] if single else list(kw["out_shape"])
    scratch = list(kw.get("scratch_shapes", []))
    n_in, n_out, n_scr = len(in_specs), len(out_specs), len(scratch)
    hbm = pl.BlockSpec(memory_space=pl.ANY)

    def hosted(*refs):
        pos = [0]

        def take(k):
            pos[0] += k
            return refs[pos[0] - k:pos[0]]

        ins, s_ins, outs, s_outs, scr, sems = take(n_in), take(n), take(n_out), take(n), take(n_scr), take(3)
        ids = [pl.program_id(i) for i in range(len(grid))]
        first = functools.reduce(jnp.logical_and, [i == 0 for i in ids])
        last = functools.reduce(jnp.logical_and, [i == g - 1 for i, g in zip(ids, grid)])

        @pl.when(first)
        def _():
            local, sends, _ = _direct_copies(s_ins, s_outs, *sems, gather)
            for cp in local + sends:
                cp.start()

        body(*ins, *outs, *scr)

        @pl.when(last)
        def _():
            local, sends, recvs = _direct_copies(s_ins, s_outs, *sems, gather)
            for cp in recvs:
                cp.wait_recv()
            for cp in sends:
                cp.wait_send()
            for cp in local:
                cp.wait()

    kw.update(in_specs=in_specs + [hbm] * n, out_specs=out_specs + [hbm] * n,
              out_shape=out_shape + [jax.ShapeDtypeStruct(((N_DEV,) + a.shape) if gather else a.shape, a.dtype)
                                     for a in arrs],
              scratch_shapes=scratch + _comm_scratch(n))
    call = pl.pallas_call(hosted, **kw)

    def run(*args):
        res = call(*args, *arrs)
        main = res[0] if single else list(res[:n_out])
        return main, list(res[n_out:])

    return run


def _params(sem=None, **kw):
    if sem is not None:
        kw["dimension_semantics"] = sem
    return pltpu.CompilerParams(vmem_limit_bytes=VMEM_LIMIT, **kw)


def _rotate(xv, c, sp, sm, transpose):
    width = xv.shape[1]
    half = ROT_DIM // 2
    if transpose:
        return xv * c + pltpu.roll(xv * sp, width - half, 1) + pltpu.roll(xv * sm, half, 1)
    return xv * c + pltpu.roll(xv, half, 1) * sp + pltpu.roll(xv, width - half, 1) * sm


def mm_rows(pairs, trans_b, out_dtype, name, tm=512, side=None, rope=None):
    n = len(pairs)
    m = pairs[0][0].shape[0]
    n_out = pairs[0][1].shape[0 if trans_b else 1]
    dims = NT_DIMS if trans_b else NN_DIMS

    def body(*refs):
        o_ref = refs[-1]
        acc = None
        for a_ref, b_ref in zip(refs[:n], refs[n:2 * n]):
            d = lax.dot_general(a_ref[...], b_ref[...], dims, preferred_element_type=F32)
            acc = d if acc is None else acc + d
        if rope is None:
            o_ref[...] = acc.astype(o_ref.dtype)
        else:
            width = rope[1]
            c, sp, sm = (jnp.concatenate([r[...]] * (width // LANES), axis=1) for r in refs[2 * n:2 * n + 3])
            o_ref[:, :width] = _rotate(acc[:, :width], c, sp, sm, False).astype(o_ref.dtype)
            o_ref[:, width:] = acc[:, width:].astype(o_ref.dtype)

    in_specs = [pl.BlockSpec((tm, a.shape[1]), lambda i: (i, 0)) for a, _ in pairs]
    in_specs += [pl.BlockSpec(b.shape, lambda i: (0, 0)) for _, b in pairs]
    args = [a for a, _ in pairs] + [b for _, b in pairs]
    if rope is not None:
        in_specs += [pl.BlockSpec((tm, LANES), lambda i: (i, 0))] * 3
        args += list(rope[0])
    return _pcall(
        body, side=side, name=name, grid=(m // tm,), in_specs=in_specs,
        out_specs=pl.BlockSpec((tm, n_out), lambda i: (i, 0)),
        out_shape=jax.ShapeDtypeStruct((m, n_out), out_dtype),
        compiler_params=_params(("arbitrary",)),
    )(*args)


DH_ROWS = 256
TN_TOKENS = 2048
TN_OUT_ELEMS = 2 * 1024 * 1024


def mm_tn(a, b, out_dtype, name, side=None, rows=None):
    t, ka = a.shape
    n_out = b.shape[1]
    tk = min(TN_TOKENS, t)
    tka = ka // 2 if ka * n_out > TN_OUT_ELEMS else ka
    tn = n_out
    steps = t // tk

    def body(a_ref, b_ref, o_ref, acc_ref):
        k = pl.program_id(2)
        d = lax.dot_general(a_ref[...], b_ref[...], TN_DIMS, preferred_element_type=F32)

        @pl.when(k == 0)
        def _():
            acc_ref[...] = d

        @pl.when(k > 0)
        def _():
            acc_ref[...] += d

        @pl.when(k == steps - 1)
        def _():
            o_ref[...] = acc_ref[...].astype(o_ref.dtype)

    return _pcall(
        body, side=side, name=name, grid=(ka // tka, n_out // tn, steps),
        in_specs=[pl.BlockSpec((tk, tka), lambda i, j, k: (k, i)), pl.BlockSpec((tk, tn), lambda i, j, k: (k, j))],
        out_specs=pl.BlockSpec((tka, tn), lambda i, j, k: (i, j)),
        out_shape=jax.ShapeDtypeStruct((ka if rows is None else rows, n_out), out_dtype),
        scratch_shapes=[pltpu.VMEM((tka, tn), F32)],
        compiler_params=_params(("arbitrary", "arbitrary", "arbitrary")),
    )(a, b)


def mm_tn_stack(a_list, b, rows, out_dtype, name, tk=1024):
    t, n_out = b.shape
    tk = min(tk, t)
    steps = t // tk
    n = len(a_list)
    offs = [sum(rows[:i]) for i in range(n)]

    def body(*refs):
        a_refs, b_ref, o_ref, acc_refs = refs[:n], refs[n], refs[n + 1], refs[n + 2:]
        k = pl.program_id(0)
        bv = b_ref[...]
        for a_ref, acc_ref in zip(a_refs, acc_refs):
            d = lax.dot_general(a_ref[...], bv, TN_DIMS, preferred_element_type=F32)

            @pl.when(k == 0)
            def _(acc_ref=acc_ref, d=d):
                acc_ref[...] = d

            @pl.when(k > 0)
            def _(acc_ref=acc_ref, d=d):
                acc_ref[...] += d

        @pl.when(k == steps - 1)
        def _():
            for acc_ref, off, r in zip(acc_refs, offs, rows):
                o_ref[off:off + r, :] = acc_ref[0:r, :].astype(o_ref.dtype)

    return _pcall(
        body, name=name, grid=(steps,),
        in_specs=[pl.BlockSpec((tk, a.shape[1]), lambda k: (k, 0)) for a in a_list]
        + [pl.BlockSpec((tk, n_out), lambda k: (k, 0))],
        out_specs=pl.BlockSpec((sum(rows), n_out), lambda k: (0, 0)),
        out_shape=jax.ShapeDtypeStruct((sum(rows), n_out), out_dtype),
        scratch_shapes=[pltpu.VMEM((a.shape[1], n_out), F32) for a in a_list],
        compiler_params=_params(("arbitrary",)),
    )(*a_list, b)


def _col_chunks(width, chunk=512):
    return [slice(c, min(c + chunk, width)) for c in range(0, width, chunk)]


def _sigmoid(x):
    return 1.0 / (1.0 + jnp.exp(-x))


def ffn_up(h, wgt, wut, name, tm=256, tn=D_FF_PAD, side=None):
    t, d = h.shape
    fp = wgt.shape[0]

    def body(h_ref, wg_ref, wu_ref, g_ref, u_ref, a_ref):
        hv = h_ref[...]

        def finish(cols, g, u):
            g_ref[:, cols] = g.astype(BF16)
            u_ref[:, cols] = u.astype(BF16)
            a_ref[:, cols] = (g * _sigmoid(g) * u).astype(BF16)

        pending = None
        for cols in _col_chunks(tn):
            g = lax.dot_general(hv, wg_ref[cols, :], NT_DIMS, preferred_element_type=F32)
            u = lax.dot_general(hv, wu_ref[cols, :], NT_DIMS, preferred_element_type=F32)
            if pending is not None:
                finish(*pending)
            pending = (cols, g, u)
        finish(*pending)

    w_spec = pl.BlockSpec((tn, d), lambda j, i: (j, 0))
    o_spec = pl.BlockSpec((tm, tn), lambda j, i: (i, j))
    o_shape = jax.ShapeDtypeStruct((t, fp), BF16)
    return _pcall(
        body, side=side, name=name, grid=(fp // tn, t // tm),
        in_specs=[pl.BlockSpec((tm, d), lambda j, i: (i, 0)), w_spec, w_spec],
        out_specs=[o_spec, o_spec, o_spec], out_shape=[o_shape, o_shape, o_shape],
        compiler_params=_params(("arbitrary", "arbitrary")),
    )(h, wgt, wut)


def ffn_down_bwd(dy0, wd, gate, up, name, tm=256, tn=D_FF_PAD, side=None):
    t, d = dy0.shape
    fp = wd.shape[0]

    def body(dy_ref, wd_ref, g_ref, u_ref, dg_ref, du_ref):
        dyv = dy_ref[...]

        def finish(cols, dact):
            g = g_ref[:, cols].astype(F32)
            u = u_ref[:, cols].astype(F32)
            sg = _sigmoid(g)
            silu = g * sg
            du_ref[:, cols] = (dact * silu).astype(BF16)
            dg_ref[:, cols] = ((dact * u) * (sg + silu * (1.0 - sg))).astype(BF16)

        pending = None
        for cols in _col_chunks(tn):
            dact = lax.dot_general(dyv, wd_ref[cols, :], NT_DIMS, preferred_element_type=F32)
            if pending is not None:
                finish(*pending)
            pending = (cols, dact)
        finish(*pending)

    t_spec = pl.BlockSpec((tm, tn), lambda j, i: (i, j))
    o_shape = jax.ShapeDtypeStruct((t, fp), BF16)
    return _pcall(
        body, side=side, name=name, grid=(fp // tn, t // tm),
        in_specs=[pl.BlockSpec((tm, d), lambda j, i: (i, 0)), pl.BlockSpec((tn, d), lambda j, i: (j, 0)), t_spec, t_spec],
        out_specs=[t_spec, t_spec], out_shape=[o_shape, o_shape],
        compiler_params=_params(("arbitrary", "arbitrary")),
    )(dy0, wd, gate, up)


def _row_specs(dx, ts, ns):
    return pl.BlockSpec((ts, dx), lambda b, s: (b * ns + s, 0))


def _mod_spec():
    return pl.BlockSpec((1, N_MOD, D_MODEL), lambda b, s: (b, 0, 0))


def _vec_spec(dx):
    return pl.BlockSpec((1, dx), lambda b, s: (0, 0))


def prenorm_fwd(x, g, mod, i_shift, i_scale, nb, name, ts=1024):
    t, dx = x.shape
    ts = min(ts, t // nb)
    ns = t // nb // ts

    def body(*refs):
        if mod is None:
            x_ref, g_ref, h_ref = refs
        else:
            x_ref, g_ref, mod_ref, h_ref = refs
        xv = x_ref[...]
        r = lax.rsqrt(jnp.mean(xv * xv, axis=-1, keepdims=True) + EPS)
        h = xv * r * g_ref[...]
        if mod is not None:
            h = h * (1.0 + mod_ref[0, i_scale:i_scale + 1, :]) + mod_ref[0, i_shift:i_shift + 1, :]
        h_ref[...] = h.astype(BF16)

    in_specs = [_row_specs(dx, ts, ns), _vec_spec(dx)]
    args = [x, g]
    if mod is not None:
        in_specs.append(_mod_spec())
        args.append(mod)
    return _pcall(
        body, name=name, grid=(nb, ns), in_specs=in_specs, out_specs=_row_specs(dx, ts, ns),
        out_shape=jax.ShapeDtypeStruct((t, dx), BF16), compiler_params=_params(("arbitrary", "arbitrary")),
    )(*args)


def prenorm_bwd(dh, x, g, mod, i_scale, dres, nb, name, ts=512, side=None):
    t, dx = x.shape
    ts = min(ts, t // nb)
    ns = t // nb // ts
    has_mod = mod is not None
    has_res = dres is not None
    pairs = dh if isinstance(dh, list) else None
    n_mm = 0 if pairs is None else len(pairs)

    def body(*refs):
        refs = list(refs)
        if pairs is None:
            dhv = refs[0][...].astype(F32)
            refs = refs[1:]
        else:
            dhv = None
            for a_ref, b_ref in zip(refs[:n_mm], refs[n_mm:2 * n_mm]):
                d = jnp.dot(a_ref[...], b_ref[...], preferred_element_type=F32)
                dhv = d if dhv is None else dhv + d
            refs = refs[2 * n_mm:]
        x_ref, g_ref = refs[:2]
        pos = 2
        mod_ref = dres_ref = None
        if has_mod:
            mod_ref = refs[pos]
            pos += 1
        if has_res:
            dres_ref = refs[pos]
            pos += 1
        dx_ref, dg_ref = refs[pos], refs[pos + 1]
        b, s = pl.program_id(0), pl.program_id(1)
        xv = x_ref[...]
        gv = g_ref[...]
        r = lax.rsqrt(jnp.mean(xv * xv, axis=-1, keepdims=True) + EPS)
        xhat = xv * r
        dn = dhv
        if has_mod:
            dsc_ref, dsh_ref = refs[pos + 2], refs[pos + 3]
            dn = dhv * (1.0 + mod_ref[0, i_scale:i_scale + 1, :])
            dsc = jnp.sum(dhv * xhat * gv, axis=0, keepdims=True)[None]
            dsh = jnp.sum(dhv, axis=0, keepdims=True)[None]

            @pl.when(s == 0)
            def _():
                dsc_ref[...] = dsc
                dsh_ref[...] = dsh

            @pl.when(s > 0)
            def _():
                dsc_ref[...] += dsc
                dsh_ref[...] += dsh

        dg = jnp.sum(dn * xhat, axis=0, keepdims=True)
        first = jnp.logical_and(b == 0, s == 0)

        @pl.when(first)
        def _():
            dg_ref[...] = dg

        @pl.when(jnp.logical_not(first))
        def _():
            dg_ref[...] += dg

        dxhat = dn * gv
        dxv = r * (dxhat - xhat * jnp.mean(dxhat * xhat, axis=-1, keepdims=True))
        if has_res:
            dxv = dxv + dres_ref[...]
        dx_ref[...] = dxv

    row = _row_specs(dx, ts, ns)
    if pairs is None:
        in_specs, args = [row], [dh]
    else:
        in_specs = [_row_specs(a.shape[1], ts, ns) for a, _ in pairs]
        in_specs += [pl.BlockSpec(b.shape, lambda b_, s_: (0, 0)) for _, b in pairs]
        args = [a for a, _ in pairs] + [b for _, b in pairs]
    in_specs += [row, _vec_spec(dx)]
    args += [x, g]
    if has_mod:
        in_specs.append(_mod_spec())
        args.append(mod)
    if has_res:
        in_specs.append(row)
        args.append(dres)
    out_specs = [row, _vec_spec(dx)]
    out_shape = [jax.ShapeDtypeStruct((t, dx), F32), jax.ShapeDtypeStruct((1, dx), F32)]
    if has_mod:
        bspec = pl.BlockSpec((1, 1, dx), lambda b, s: (b, 0, 0))
        out_specs += [bspec, bspec]
        out_shape += [jax.ShapeDtypeStruct((nb, 1, dx), F32)] * 2
    return _pcall(
        body, side=side, name=name, grid=(nb, ns), in_specs=in_specs, out_specs=out_specs, out_shape=out_shape,
        compiler_params=_params(("arbitrary", "arbitrary")),
    )(*args)


def postnorm_fwd(x, pairs, g, mod, i_gate, coef, nb, name, target=None, ts=512, side=None):
    t, dx = x.shape
    with_loss = target is not None
    ts = min(ts, t // nb)
    ns = t // nb // ts
    n_mm = len(pairs)

    def body(*refs):
        yv = None
        for a_ref, b_ref in zip(refs[:n_mm], refs[n_mm:2 * n_mm]):
            d = jnp.dot(a_ref[...], b_ref[...], preferred_element_type=F32)
            yv = d if yv is None else yv + d
        refs = refs[2 * n_mm:]
        x_ref, g_ref, mod_ref = refs[:3]
        refs[-1][...] = yv
        r = lax.rsqrt(jnp.mean(yv * yv, axis=-1, keepdims=True) + EPS)
        out = x_ref[...] + (coef * mod_ref[0, i_gate:i_gate + 1, :]) * (yv * r * g_ref[...])
        if not with_loss:
            refs[3][...] = out
            return
        t_ref, dx_ref, loss_ref = refs[3:6]
        b, s = pl.program_id(0), pl.program_id(1)
        err = out - t_ref[...]
        dx_ref[...] = err * (1.0 / dx)
        part = (0.5 / dx) * jnp.sum(jnp.sum(err * err, axis=1, keepdims=True), axis=0, keepdims=True)
        first = jnp.logical_and(b == 0, s == 0)

        @pl.when(first)
        def _():
            loss_ref[...] = part

        @pl.when(jnp.logical_not(first))
        def _():
            loss_ref[...] += part

    row = _row_specs(dx, ts, ns)
    in_specs = [_row_specs(a.shape[1], ts, ns) for a, _ in pairs]
    in_specs += [pl.BlockSpec(b.shape, lambda b_, s_: (0, 0)) for _, b in pairs]
    in_specs += [row, _vec_spec(dx), _mod_spec()]
    args = [a for a, _ in pairs] + [b for _, b in pairs] + [x, g, mod]
    row_shape = jax.ShapeDtypeStruct((t, dx), F32)
    out_specs, out_shape = [row, row], [row_shape, row_shape]
    if with_loss:
        in_specs.append(row)
        args.append(target)
        out_specs = [row, pl.BlockSpec((1, 1), lambda b, s: (0, 0)), row]
        out_shape = [row_shape, jax.ShapeDtypeStruct((1, 1), F32), row_shape]
    return _pcall(
        body, side=side, name=name, grid=(nb, ns), in_specs=in_specs, out_specs=out_specs, out_shape=out_shape,
        compiler_params=_params(("arbitrary", "arbitrary")),
    )(*args)


def postnorm_bwd(dxo, y0, g, mod, i_gate, coef, nb, name, ts=1024):
    t, dx = y0.shape
    ts = min(ts, t // nb)
    ns = t // nb // ts

    def body(d_ref, y_ref, g_ref, mod_ref, dy_ref, dg_ref, dgate_ref):
        b, s = pl.program_id(0), pl.program_id(1)
        yv = y_ref[...]
        dv = d_ref[...]
        gv = g_ref[...]
        r = lax.rsqrt(jnp.mean(yv * yv, axis=-1, keepdims=True) + EPS)
        yhat = yv * r
        dgate = jnp.sum(dv * (coef * (yhat * gv)), axis=0, keepdims=True)[None]
        dyn = dv * (coef * mod_ref[0, i_gate:i_gate + 1, :])
        dg = jnp.sum(dyn * yhat, axis=0, keepdims=True)
        dyhat = dyn * gv
        dy_ref[...] = (r * (dyhat - yhat * jnp.mean(dyhat * yhat, axis=-1, keepdims=True))).astype(BF16)

        @pl.when(s == 0)
        def _():
            dgate_ref[...] = dgate

        @pl.when(s > 0)
        def _():
            dgate_ref[...] += dgate

        first = jnp.logical_and(b == 0, s == 0)

        @pl.when(first)
        def _():
            dg_ref[...] = dg

        @pl.when(jnp.logical_not(first))
        def _():
            dg_ref[...] += dg

    row = _row_specs(dx, ts, ns)
    return _pcall(
        body, name=name, grid=(nb, ns), in_specs=[row, row, _vec_spec(dx), _mod_spec()],
        out_specs=[row, _vec_spec(dx), pl.BlockSpec((1, 1, dx), lambda b, s: (b, 0, 0))],
        out_shape=[jax.ShapeDtypeStruct((t, dx), BF16), jax.ShapeDtypeStruct((1, dx), F32),
                   jax.ShapeDtypeStruct((nb, 1, dx), F32)],
        compiler_params=_params(("arbitrary", "arbitrary")),
    )(dxo, y0, g, mod)


def rope_tables(positions):
    inv_freq = ROPE_THETA ** (-jnp.arange(0, ROT_DIM, 2, dtype=F32) / ROT_DIM)
    ang = positions.astype(F32).reshape(-1, 1) * inv_freq
    cos, sin = jnp.cos(ang), jnp.sin(ang)
    half = ROT_DIM // 2
    z = lambda n: jnp.zeros((ang.shape[0], n), F32)
    c = jnp.concatenate([cos, cos, jnp.ones((ang.shape[0], HEAD_DIM - ROT_DIM), F32)], axis=1)
    sp = jnp.concatenate([z(half), sin, z(HEAD_DIM - ROT_DIM)], axis=1)
    sm = jnp.concatenate([-sin, z(HEAD_DIM - half)], axis=1)
    return tuple(jnp.tile(a, (1, HEADS_PER_STEP)) for a in (c, sp, sm))


def _scan_lanes(x, reverse):
    n = x.shape[-1]
    lane = lax.broadcasted_iota(jnp.int32, x.shape, x.ndim - 1)
    k = 1
    while k < n:
        if reverse:
            x = x + jnp.where(lane < n - k, pltpu.roll(x, n - k, x.ndim - 1), 0.0)
        else:
            x = x + jnp.where(lane >= k, pltpu.roll(x, k, x.ndim - 1), 0.0)
        k *= 2
    return x


def _log_sigmoid(z):
    return jnp.minimum(z, 0.0) - jnp.log(1.0 + jnp.exp(-jnp.abs(z)))


def fox_gate_fwd(ft, b_forget, name):
    nb, nh, s = ft.shape

    def body(f_ref, b_ref, o_ref):
        z = f_ref[0] + b_ref[...]
        o_ref[0] = -_scan_lanes(_log_sigmoid(z), False)

    spec = pl.BlockSpec((1, nh, s), lambda b: (b, 0, 0))
    return _pcall(
        body, name=name, grid=(nb,), in_specs=[spec, pl.BlockSpec((nh, 1), lambda b: (0, 0))], out_specs=spec,
        out_shape=jax.ShapeDtypeStruct((nb, nh, s), F32), compiler_params=_params(("arbitrary",)),
    )(ft, b_forget)


def fox_gate_bwd(dcb, drow, ft, b_forget, name):
    nb, nh, s = ft.shape

    def body(d_ref, r_ref, f_ref, b_ref, dz_ref, db_ref):
        b = pl.program_id(0)
        z = f_ref[0] + b_ref[...]
        dlf = _scan_lanes(r_ref[0] - d_ref[0], True)
        dz = dlf * _sigmoid(-z)
        dz_ref[0] = dz
        db = jnp.sum(dz, axis=1, keepdims=True)

        @pl.when(b == 0)
        def _():
            db_ref[...] = db

        @pl.when(b > 0)
        def _():
            db_ref[...] += db

    spec = pl.BlockSpec((1, nh, s), lambda b: (b, 0, 0))
    vec = pl.BlockSpec((nh, 1), lambda b: (0, 0))
    return _pcall(
        body, name=name, grid=(nb,), in_specs=[spec, spec, spec, vec], out_specs=[spec, vec],
        out_shape=[jax.ShapeDtypeStruct((nb, nh, s), F32), jax.ShapeDtypeStruct((nh, 1), F32)],
        compiler_params=_params(("arbitrary",)),
    )(dcb, drow, ft, b_forget)


ATTN_TQ = 512
ATTN_TK = 512
ONES_ROWS = 16


def _rows_to_cols(rows):
    tile = jnp.concatenate([jnp.broadcast_to(rw, (HEAD_DIM, rw.shape[1])) for rw in rows], axis=0)
    return tile.T


def _block_delta(s, tq, tk):
    off = jnp.arange(s // tk) - (tq // tk - 1)
    return off[:, None, None] * tk + jnp.arange(tq)[None, None, :] - jnp.arange(tk)[None, :, None]


def dilated_table(s, tq, tk):
    delta = _block_delta(s, tq, tk)
    count = jnp.zeros(delta.shape, F32)
    for window, dil in DILATED_PATTERNS:
        count = count + ((delta >= 0) & (delta <= window) & (delta % dil == 0)).astype(F32)
    return jnp.where(count > 0, jnp.log(jnp.maximum(count, 1.0)), NEG)


def causal_table(s, tq, tk):
    return jnp.where(_block_delta(s, tq, tk) >= 0, 0.0, NEG).astype(F32)


def attn_fwd(q_arr, q_off, k_arr, k_off, v_arr, v_off, table, colbias, nb, name, side=None, off_diag_bias=True):
    t = q_arr.shape[0]
    s = t // nb
    tk, tq = table.shape[1:]
    assert tq == tk, "the diagonal handling below is written for square tiles"
    nq, nk = s // tq, s // tk
    npairs = WIDTH_A // LANES
    use_cb = colbias is not None

    def body(*refs):
        refs = list(refs)
        q_ref, k_ref, v_ref, tab_ref = refs[:4]
        cb_ref = refs[4] if use_cb else None
        tail = refs[-(HEADS_PER_STEP + int(use_cb)):]
        acc_s = tail[:HEADS_PER_STEP]
        cbc_s = tail[-1] if use_cb else None
        o_ref, lse_ref, vt_s = refs[-3 - len(tail):-len(tail)]
        qi = pl.program_id(2)

        heads = [slice(h * HEAD_DIM, (h + 1) * HEAD_DIM) for h in range(HEADS_PER_STEP)]

        @pl.when(qi == 0)
        def _():
            for cblk in range(nk):
                vt = v_ref[cblk * tk:(cblk + 1) * tk, :].astype(F32).T.astype(BF16)
                for h, hs in enumerate(heads):
                    vt_s[cblk, h, 0:HEAD_DIM, :] = vt[hs, :]
                    vt_s[cblk, h, HEAD_DIM:, :] = jnp.ones((ONES_ROWS, tk), BF16)
                if use_cb:
                    cbc_s[cblk] = _rows_to_cols([cb_ref[0, h, cblk] for h in range(HEADS_PER_STEP)])

        qt_all = (q_ref[...].astype(F32) * ATTN_SCALE).T.astype(BF16)
        qts = [qt_all[hs, :] for hs in heads]
        for a in acc_s:
            a[...] = jnp.zeros_like(a)

        def tile(kb, tab, k0, klen, q0, carry):
            ks = pl.multiple_of(kb * tk + k0, klen)
            sts, out = [], []
            for h, hs in enumerate(heads):
                st = jnp.dot(k_ref[pl.ds(ks, klen), hs], qts[h][:, q0:], preferred_element_type=F32)
                if tab is not None:
                    st = st + tab
                if use_cb:
                    st = st + cbc_s[kb, k0:k0 + klen, h * HEAD_DIM:h * HEAD_DIM + 1]
                sts.append(st)
            m_old = [carry[h][:, q0:] for h in range(HEADS_PER_STEP)]
            m_new = [jnp.maximum(m_old[h], jnp.max(sts[h], axis=0, keepdims=True)) for h in range(HEADS_PER_STEP)]
            for h in range(HEADS_PER_STEP):
                pt = jnp.exp(sts[h] - m_new[h]).astype(BF16)
                acc_s[h][:, q0:] = (jnp.exp(m_old[h] - m_new[h]) * acc_s[h][:, q0:]
                                    + jnp.dot(vt_s[kb, h, :, k0:k0 + klen], pt, preferred_element_type=F32))
                out.append(m_new[h] if q0 == 0 else jnp.concatenate([carry[h][:, :q0], m_new[h]], axis=1))
            return tuple(out)

        fin = lax.fori_loop(0, qi, lambda kb, c: tile(kb, tab_ref[qi - kb] if off_diag_bias else None, 0, tk, 0, c),
                            tuple(jnp.full((1, tq), NEG, F32) for _ in heads))
        half = tk // 2
        fin = tile(qi, tab_ref[0, 0:half, :], 0, half, 0, fin)
        fin = tile(qi, tab_ref[0, half:, half:], half, half, half, fin)
        outs = []
        for h in range(HEADS_PER_STEP):
            l = acc_s[h][HEAD_DIM:HEAD_DIM + 1, :]
            outs.append(acc_s[h][0:HEAD_DIM, :] / l)
            lse_ref[0, h, 0] = fin[h] + jnp.log(l)
        o_ref[...] = jnp.concatenate(outs, axis=0).T

    def seq_spec(off):
        return pl.BlockSpec((s, LANES), lambda b, j, i: (b, off + j))

    in_specs = [pl.BlockSpec((tq, LANES), lambda b, j, i: (b * nq + i, q_off + j)), seq_spec(k_off), seq_spec(v_off),
                pl.BlockSpec(table.shape, lambda b, j, i: (0, 0, 0))]
    args = [q_arr, k_arr, v_arr, table]
    if use_cb:
        in_specs.append(pl.BlockSpec((1, HEADS_PER_STEP, nk, 1, tk), lambda b, j, i: (b, j, 0, 0, 0)))
        args.append(colbias)
    n_heads = npairs * HEADS_PER_STEP
    return _pcall(
        body, side=side, name=name, grid=(nb, npairs, nq), in_specs=in_specs,
        out_specs=[pl.BlockSpec((tq, LANES), lambda b, j, i: (b * nq + i, j)),
                   pl.BlockSpec((1, HEADS_PER_STEP, 1, 1, tq), lambda b, j, i: (b, j, i, 0, 0))],
        out_shape=[jax.ShapeDtypeStruct((t, npairs * LANES), F32), jax.ShapeDtypeStruct((nb, n_heads, nq, 1, tq), F32)],
        scratch_shapes=[pltpu.VMEM((nk, HEADS_PER_STEP, HEAD_DIM + ONES_ROWS, tk), BF16)]
        + [pltpu.VMEM((HEAD_DIM + ONES_ROWS, tq), F32)] * HEADS_PER_STEP
        + ([pltpu.VMEM((nk, tk, LANES), F32)] if use_cb else []),
        compiler_params=_params(("arbitrary", "arbitrary", "arbitrary")),
    )(*args)


def attn_bwd(q_arr, q_off, k_arr, k_off, v_arr, v_off, o_arr, lse_arr, do_arr, table, colbias, nb, name, side=None,
             rope_tabs=None, off_diag_bias=True):
    t = q_arr.shape[0]
    s = t // nb
    tk, tq = table.shape[1:]
    assert tq == tk, "the diagonal handling below is written for square tiles"
    nq, nk = s // tq, s // tk
    npairs = WIDTH_A // LANES
    use_cb = colbias is not None

    def body(*refs):
        refs = list(refs)
        q_ref, k_ref, v_ref, o_ref, lse_ref, do_ref, tab_ref = refs[:7]
        pos = 7
        cb_ref = None
        if use_cb:
            cb_ref = refs[pos]
            pos += 1
        rope_refs = None
        if rope_tabs is not None:
            rope_refs = refs[pos:pos + 3]
            pos += 3
        dq_ref, dk_ref, dv_ref = refs[pos:pos + 3]
        pos += 3
        dcb_ref = drow_ref = None
        if use_cb:
            dcb_ref, drow_ref = refs[pos:pos + 2]
            pos += 2
        kt_s, dkt_s, dvt_s = refs[pos:pos + 3]
        dqt_s = refs[pos + 3:pos + 3 + HEADS_PER_STEP]
        dcb_s, cbc_s = refs[pos + 3 + HEADS_PER_STEP:pos + 5 + HEADS_PER_STEP] if use_cb else (None, None)

        heads = [slice(h * HEAD_DIM, (h + 1) * HEAD_DIM) for h in range(HEADS_PER_STEP)]
        for cblk in range(nk):
            kt_s[cblk] = k_ref[cblk * tk:(cblk + 1) * tk, :].astype(F32).T.astype(BF16)
        dkt_s[...] = jnp.zeros_like(dkt_s)
        dvt_s[...] = jnp.zeros_like(dvt_s)
        if use_cb:
            dcb_s[...] = jnp.zeros_like(dcb_s)
            for cblk in range(nk):
                cbc_s[cblk] = _rows_to_cols([cb_ref[0, h, cblk] for h in range(HEADS_PER_STEP)])
        ones = jnp.ones((8, HEAD_DIM), BF16)

        def q_loop(qi, carry):
            qs = pl.multiple_of(qi * tq, tq)
            q_all = (q_ref[pl.ds(qs, tq), :].astype(F32) * ATTN_SCALE)
            do_all = do_ref[pl.ds(qs, tq), :]
            qt_all = q_all.T.astype(BF16)
            dot_all = do_all.T.astype(BF16)
            qt, dot, lse, dsum = [], [], [], []
            for h, hs in enumerate(heads):
                qt.append(qt_all[hs, :])
                dot.append(dot_all[hs, :])
                lse.append(lse_ref[0, h, qi])
                prod = do_all[:, hs] * o_ref[pl.ds(qs, tq), hs]
                hi = prod.astype(BF16)
                lo = (prod - hi.astype(F32)).astype(BF16)
                dsum.append((lax.dot_general(ones, hi, NT_DIMS, preferred_element_type=F32)
                             + lax.dot_general(ones, lo, NT_DIMS, preferred_element_type=F32))[0:1, :])
            for a in dqt_s:
                a[...] = jnp.zeros_like(a)

            def tile(kb, tab, k0, klen, q0, drow):
                ks = pl.multiple_of(kb * tk + k0, klen)
                keys = slice(k0, k0 + klen)
                sts, dpts, out = [], [], []
                for h, hs in enumerate(heads):
                    st = jnp.dot(k_ref[pl.ds(ks, klen), hs], qt[h][:, q0:], preferred_element_type=F32)
                    if tab is not None:
                        st = st + tab
                    if use_cb:
                        st = st + cbc_s[kb, keys, h * HEAD_DIM:h * HEAD_DIM + 1]
                    sts.append(st)
                    dpts.append(jnp.dot(v_ref[pl.ds(ks, klen), hs], dot[h][:, q0:], preferred_element_type=F32))
                for h, hs in enumerate(heads):
                    pt = jnp.exp(sts[h] - lse[h][:, q0:])
                    dst = pt * (dpts[h] - dsum[h][:, q0:])
                    dst_b = dst.astype(BF16)
                    dvt_s[h, kb, :, keys] += lax.dot_general(dot[h][:, q0:], pt.astype(BF16), NT_DIMS,
                                                             preferred_element_type=F32)
                    dkt_s[h, kb, :, keys] += lax.dot_general(qt[h][:, q0:], dst_b, NT_DIMS, preferred_element_type=F32)
                    dqt_s[h][:, q0:] += jnp.dot(kt_s[kb, hs, keys], dst_b, preferred_element_type=F32)
                    if use_cb:
                        dcb_s[h, pl.ds(ks, klen), :] += jnp.sum(dst, axis=1, keepdims=True)
                        dr = drow[h][:, q0:] + jnp.sum(dst, axis=0, keepdims=True)
                        out.append(dr if q0 == 0 else jnp.concatenate([drow[h][:, :q0], dr], axis=1))
                    else:
                        out.append(drow[h])
                return tuple(out)

            drow = lax.fori_loop(0, qi, lambda kb, c: tile(kb, tab_ref[qi - kb] if off_diag_bias else None, 0, tk, 0, c),
                                 tuple(jnp.zeros((1, tq), F32) for _ in heads))
            half = tk // 2
            drow = tile(qi, tab_ref[0, 0:half, :], 0, half, 0, drow)
            drow = tile(qi, tab_ref[0, half:, half:], half, half, half, drow)
            dq = (jnp.concatenate([a[...] for a in dqt_s], axis=0) * ATTN_SCALE).T
            if rope_refs is not None:
                dq = _rotate(dq, *[coef[pl.ds(qs, tq), :] for coef in rope_refs], True)
            dq_ref[pl.ds(qs, tq), :] = dq.astype(dq_ref.dtype)
            if use_cb:
                for h in range(HEADS_PER_STEP):
                    drow_ref[0, h, qi] = drow[h]
            return carry

        lax.fori_loop(0, nq, q_loop, 0)
        for cblk in range(nk):
            rows = slice(cblk * tk, (cblk + 1) * tk)
            dk = jnp.concatenate([dkt_s[h, cblk] for h in range(HEADS_PER_STEP)], axis=0).T
            if rope_refs is not None:
                dk = _rotate(dk, *[coef[rows, :] for coef in rope_refs], True)
            dk_ref[rows, :] = dk.astype(dk_ref.dtype)
            dv_ref[rows, :] = jnp.concatenate([dvt_s[h, cblk] for h in range(HEADS_PER_STEP)], axis=0).T.astype(dv_ref.dtype)
            if use_cb:
                for h in range(HEADS_PER_STEP):
                    dcb_ref[0, h, cblk] = jnp.broadcast_to(dcb_s[h, rows, :], (tk, LANES)).T[0:1, :]

    def seq_spec(off):
        return pl.BlockSpec((s, LANES), lambda b, j: (b, off + j))

    row_spec = pl.BlockSpec((1, HEADS_PER_STEP, nq, 1, tq), lambda b, j: (b, j, 0, 0, 0))
    in_specs = [seq_spec(q_off), seq_spec(k_off), seq_spec(v_off), seq_spec(0), row_spec, seq_spec(0),
                pl.BlockSpec(table.shape, lambda b, j: (0, 0, 0))]
    args = [q_arr, k_arr, v_arr, o_arr, lse_arr, do_arr, table]
    width = npairs * LANES
    out_specs = [seq_spec(0)] * 3
    out_shape = [jax.ShapeDtypeStruct((t, width), BF16)] * 3
    scratch = [pltpu.VMEM((nk, LANES, tk), BF16), pltpu.VMEM((HEADS_PER_STEP, nk, HEAD_DIM, tk), F32),
               pltpu.VMEM((HEADS_PER_STEP, nk, HEAD_DIM, tk), F32)] + [pltpu.VMEM((HEAD_DIM, tq), F32)] * HEADS_PER_STEP
    if use_cb:
        cb_spec = pl.BlockSpec((1, HEADS_PER_STEP, nk, 1, tk), lambda b, j: (b, j, 0, 0, 0))
        in_specs.append(cb_spec)
        args.append(colbias)
    if rope_tabs is not None:
        in_specs += [pl.BlockSpec((s, LANES), lambda b, j: (b, 0))] * 3
        args += list(rope_tabs)
    if use_cb:
        out_specs += [cb_spec, row_spec]
        out_shape += [jax.ShapeDtypeStruct(colbias.shape, F32), jax.ShapeDtypeStruct(lse_arr.shape, F32)]
        scratch += [pltpu.VMEM((HEADS_PER_STEP, s, 1), F32), pltpu.VMEM((nk, tk, LANES), F32)]
    return _pcall(
        body, side=side, name=name, grid=(nb, npairs), in_specs=in_specs, out_specs=out_specs, out_shape=out_shape,
        scratch_shapes=scratch, compiler_params=_params(("arbitrary", "arbitrary")),
    )(*args)


def ada_fwd(c_all, w_ada, b_cols, name):
    def body(c_ref, w_ref, b_ref, o_ref):
        cv = c_ref[...]
        sc = (cv * _sigmoid(cv)).astype(BF16)
        o_ref[...] = jnp.dot(sc, w_ref[...].astype(BF16), preferred_element_type=F32) + b_ref[...]

    return _pcall(body, name=name, out_shape=jax.ShapeDtypeStruct((c_all.shape[0], w_ada.shape[1]), F32),
                  compiler_params=_params())(c_all, w_ada, b_cols)


def ada_bwd(c_all, dmod_cols, name):
    def body(c_ref, d_ref, o_ref):
        cv = c_ref[...]
        sc = (cv * _sigmoid(cv)).astype(BF16)
        o_ref[...] = lax.dot_general(sc, d_ref[...].astype(BF16), TN_DIMS, preferred_element_type=F32)

    return _pcall(body, name=name, out_shape=jax.ShapeDtypeStruct((c_all.shape[1], dmod_cols.shape[1]), F32),
                  compiler_params=_params())(c_all, dmod_cols)


def adamw(parts, group, w, m, v, name, tr=None):
    n = parts.shape[0]
    r, c = w.shape
    tr = r if tr is None else tr
    c1 = 1.0 - ADAM_B1 ** ADAM_STEP
    c2 = 1.0 - ADAM_B2 ** ADAM_STEP

    def body(p_ref, w_ref, m_ref, v_ref, g_ref, d_ref, nm_ref, nv_ref):
        g = p_ref[0, 0].astype(F32)
        for i in range(1, n):
            g = g + p_ref[i, 0].astype(F32)
        wv = w_ref[...]
        nm = ADAM_B1 * m_ref[...] + (1.0 - ADAM_B1) * g
        nv = ADAM_B2 * v_ref[...] + (1.0 - ADAM_B2) * (g * g)
        g_ref[...] = g
        nm_ref[...] = nm
        nv_ref[...] = nv
        d_ref[...] = -ADAM_LR * ((nm / c1) / (jnp.sqrt(nv / c2) + ADAM_EPS) + ADAM_WD * wv)

    spec = pl.BlockSpec((tr, c), lambda i: (i, 0))
    shape = jax.ShapeDtypeStruct((r, c), F32)
    return _pcall(
        body, name=name, grid=(r // tr,),
        in_specs=[pl.BlockSpec((n, 1, tr, c), lambda i: (0, group, i, 0)), spec, spec, spec],
        out_specs=[spec] * 4, out_shape=[shape] * 4, compiler_params=_params(("arbitrary",)),
    )(parts, w, m, v)


def all_gather(arrs, name):
    n = len(arrs)
    hbm = pl.BlockSpec(memory_space=pl.ANY)

    def body(*refs):
        ins, outs = refs[:n], refs[n:2 * n]
        send_sems, recv_sems, local_sems = refs[2 * n:]
        x, y, c = _place()
        me, sibling = (x, y, c), (x, y, 1 - c)
        chips = [(1 - x, y), (x, 1 - y), (1 - x, 1 - y)]

        def copy(a, k, block, to, src=None):
            dst = outs[a].at[_slot(block)]
            return pltpu.make_async_remote_copy(
                src_ref=dst if src is None else src, dst_ref=dst, send_sem=send_sems.at[a * 7 + k],
                recv_sem=recv_sems.at[a * 7 + k], device_id=to, device_id_type=MESH)

        mine = [pltpu.make_async_copy(ins[a], outs[a].at[_slot(me)], local_sems.at[a]) for a in range(n)]
        for cp in mine:
            cp.start()
        first = []
        for a in range(n):
            first.append(copy(a, 0, me, sibling, src=ins[a]))
            first += [copy(a, 1 + j, me, (*chip, c), src=ins[a]) for j, chip in enumerate(chips)]
        for cp in first:
            cp.start()
        passed = []
        for a in range(n):
            for j, chip in enumerate(chips):
                copy(a, 1 + j, (*chip, c), me).wait_recv()
                cp = copy(a, 4 + j, (*chip, c), sibling)
                cp.start()
                passed.append(cp)
        for a in range(n):
            copy(a, 0, sibling, me).wait_recv()
            for j, chip in enumerate(chips):
                copy(a, 4 + j, (*chip, 1 - c), me).wait_recv()
        for cp in first + passed:
            cp.wait_send()
        for cp in mine:
            cp.wait()

    return _pcall(
        body, name=name, in_specs=[hbm] * n, out_specs=[hbm] * n,
        out_shape=[jax.ShapeDtypeStruct((N_DEV,) + a.shape, a.dtype) for a in arrs],
        scratch_shapes=[pltpu.SemaphoreType.DMA((7 * n,)), pltpu.SemaphoreType.DMA((7 * n,)),
                        pltpu.SemaphoreType.DMA((n,))],
        compiler_params=pltpu.CompilerParams(has_side_effects=True),
    )(*arrs)


def _t(w):
    return jnp.swapaxes(w, -1, -2)


def _rows_from_blocks(blocks, pad_to=None):
    full = blocks.reshape(-1, blocks.shape[2])
    if pad_to is not None and pad_to > full.shape[0]:
        full = jnp.pad(full, ((0, pad_to - full.shape[0]), (0, 0)))
    return full


def _rows_to_blocks(full, nrows):
    return full[:nrows].reshape(N_DEV, nrows // N_DEV, full.shape[1])


SMALL_ORDER = ("g_pre_ff1", "g_post_ff1", "g_pre_mix", "g_post_mix", "g_out_a", "g_out_b", "g_pre_ff2", "g_post_ff2",
               "b_forget")


def _pack_small(vals):
    rows = []
    for name in SMALL_ORDER:
        v = vals[name].reshape(1, -1)
        if v.shape[1] % LANES:
            v = jnp.pad(v, ((0, 0), (0, LANES - v.shape[1] % LANES)))
        rows.append(v)
    return jnp.concatenate(rows, axis=1)


def _unpack_small(row, sizes):
    out, pos = {}, 0
    for name in SMALL_ORDER:
        n = sizes[name]
        out[name] = row[:, pos:pos + n]
        pos += -(-n // LANES) * LANES
    return out


def _ffn_forward(x, mod, g_pre, g_post, wg, wu, wd, i0, nb, tag, target=None, side=None, side_down=None):
    h = prenorm_fwd(x, g_pre, mod, i0, i0 + 1, nb, f"{tag}_prenorm")
    res, side_out = ffn_up(h, wg, wu, f"{tag}_up", side=side), None
    if side is not None:
        res, side_out = res
    gate, up, act = res
    if callable(wd):
        wd = wd(side_out)
    res, side_down_out = postnorm_fwd(x, [(act, wd)], g_post, mod, i0 + 2, 0.5, nb, f"{tag}_down_postnorm",
                                      target=target, side=side_down), None
    if side_down is not None:
        res, side_down_out = res
    out, y0 = (res[0] if target is None else tuple(res[:2])), res[-1]
    return out, (x, h, gate, up, act, y0), wd, side_out, side_down_out


def _ffn_backward(dxo, saved, mod, g_pre, g_post, wg, wu, wd, i0, nb, tag, exchange=False):
    x, h, gate, up, act, y0 = saved
    blocks = lambda g: _rows_to_blocks(g, D_FF)[:, None]
    dy0, dg_post, dgate_mod = postnorm_bwd(dxo, y0, g_post, mod, i0 + 2, 0.5, nb, f"{tag}_postnorm_bwd")
    dwd = mm_tn(act, dy0, BF16, f"{tag}_dwd", rows=D_FF)
    res = ffn_down_bwd(dy0, wd, gate, up, f"{tag}_down_bwd", side=([blocks(dwd)], False) if exchange else None)
    if exchange:
        res, (dwd,) = res
    dgate, dup = res
    dwg = mm_tn(dgate, h, BF16, f"{tag}_dwg", rows=D_FF)
    dwu = mm_tn(dup, h, BF16, f"{tag}_dwu", rows=D_FF)
    res = prenorm_bwd([(dgate, wg), (dup, wu)], x, g_pre, mod, i0 + 1, dxo, nb, f"{tag}_dh_prenorm_bwd", ts=DH_ROWS,
                      side=([blocks(dwg), blocks(dwu)], False) if exchange else None)
    if exchange:
        res, (dwg, dwu) = res
    dx, dg_pre, dsc, dsh = res
    return dx, dict(g_pre=dg_pre, g_post=dg_post, wg=dwg, wu=dwu, wd=dwd, mod=(dsh, dsc, dgate_mod))


def kernel(x, c, positions, w_ada, b_ada, g_pre_ff1, g_post_ff1, w_ff1_gate, w_ff1_up, w_ff1_down, g_pre_mix, g_post_mix, w_in, b_forget, g_out_a, g_out_b, w_out, g_pre_ff2, g_post_ff2, w_ff2_gate, w_ff2_up, w_ff2_down, loss_target, m_w_ada, m_b_ada, m_g_pre_ff1, m_g_post_ff1, m_w_ff1_gate, m_w_ff1_up, m_w_ff1_down, m_g_pre_mix, m_g_post_mix, m_w_in, m_b_forget, m_g_out_a, m_g_out_b, m_w_out, m_g_pre_ff2, m_g_post_ff2, m_w_ff2_gate, m_w_ff2_up, m_w_ff2_down, v_w_ada, v_b_ada, v_g_pre_ff1, v_g_post_ff1, v_w_ff1_gate, v_w_ff1_up, v_w_ff1_down, v_g_pre_mix, v_g_post_mix, v_w_in, v_b_forget, v_g_out_a, v_g_out_b, v_w_out, v_g_pre_ff2, v_g_post_ff2, v_w_ff2_gate, v_w_ff2_up, v_w_ff2_down):
    weights = dict(w_ada=w_ada, b_ada=b_ada, g_pre_ff1=g_pre_ff1, g_post_ff1=g_post_ff1, w_ff1_gate=w_ff1_gate,
                   w_ff1_up=w_ff1_up, w_ff1_down=w_ff1_down, g_pre_mix=g_pre_mix, g_post_mix=g_post_mix, w_in=w_in,
                   b_forget=b_forget, g_out_a=g_out_a, g_out_b=g_out_b, w_out=w_out, g_pre_ff2=g_pre_ff2,
                   g_post_ff2=g_post_ff2, w_ff2_gate=w_ff2_gate, w_ff2_up=w_ff2_up, w_ff2_down=w_ff2_down)
    mom_m = dict(w_ada=m_w_ada, b_ada=m_b_ada, g_pre_ff1=m_g_pre_ff1, g_post_ff1=m_g_post_ff1, w_ff1_gate=m_w_ff1_gate,
                 w_ff1_up=m_w_ff1_up, w_ff1_down=m_w_ff1_down, g_pre_mix=m_g_pre_mix, g_post_mix=m_g_post_mix,
                 w_in=m_w_in, b_forget=m_b_forget, g_out_a=m_g_out_a, g_out_b=m_g_out_b, w_out=m_w_out,
                 g_pre_ff2=m_g_pre_ff2, g_post_ff2=m_g_post_ff2, w_ff2_gate=m_w_ff2_gate, w_ff2_up=m_w_ff2_up,
                 w_ff2_down=m_w_ff2_down)
    mom_v = dict(w_ada=v_w_ada, b_ada=v_b_ada, g_pre_ff1=v_g_pre_ff1, g_post_ff1=v_g_post_ff1, w_ff1_gate=v_w_ff1_gate,
                 w_ff1_up=v_w_ff1_up, w_ff1_down=v_w_ff1_down, g_pre_mix=v_g_pre_mix, g_post_mix=v_g_post_mix,
                 w_in=v_w_in, b_forget=v_b_forget, g_out_a=v_g_out_a, g_out_b=v_g_out_b, w_out=v_w_out,
                 g_pre_ff2=v_g_pre_ff2, g_post_ff2=v_g_post_ff2, w_ff2_gate=v_w_ff2_gate, w_ff2_up=v_w_ff2_up,
                 w_ff2_down=v_w_ff2_down)
    order = list(weights)

    nb, s, d = x.shape
    t = nb * s
    me = _slot(_place())
    nbg = nb * N_DEV
    ada_cols = w_ada.shape[2]

    bf = lambda w: w[0].astype(BF16)
    bft = lambda w: _t(w)[0].astype(BF16)
    c_all, wg1, wu1 = all_gather([c, bft(w_ff1_gate), bft(w_ff1_up)], "gather_ff1")
    c_all = c_all.reshape(nbg, d)
    wg1, wu1 = (_rows_from_blocks(w, D_FF_PAD) for w in (wg1, wu1))

    b_cols = lax.dynamic_slice(b_ada, (0, me * ada_cols), (1, ada_cols))
    mod_cols = ada_fwd(c_all, w_ada[0], b_cols, "ada_fwd")
    (mod_all,) = all_gather([mod_cols], "gather_mod")
    mod = lax.dynamic_slice(mod_all, (0, me * nb, 0), (N_DEV, nb, ada_cols))
    mod = mod.transpose(1, 0, 2).reshape(nb, N_MOD, d)

    xf = x.reshape(t, d)
    target = loss_target.reshape(t, d)

    x1, saved1, wd1, (_, w_in_all), (w_out_all,) = _ffn_forward(
        xf, mod, g_pre_ff1, g_post_ff1, wg1, wu1, lambda got: _rows_from_blocks(got[0], D_FF_PAD), 0, nb, "ff1",
        side=([bf(w_ff1_down), bft(w_in)], True), side_down=([bf(w_out)], True))
    w_in_t = _rows_from_blocks(w_in_all)
    n_qkv = 3 * (WIDTH_A + WIDTH_B)
    w_qkv_t = w_in_t[:n_qkv]
    w_f_t = jnp.pad(w_in_t[n_qkv:], ((0, LANES - N_HEADS_B), (0, 0)))
    w_o = _rows_from_blocks(w_out_all)
    w_o_a, w_o_b = w_o[:WIDTH_A], w_o[WIDTH_A:]

    h2 = prenorm_fwd(x1, g_pre_mix, mod, 3, 4, nb, "mix_prenorm")
    tables = rope_tables(positions)
    proj = mm_rows([(h2, w_qkv_t)], True, BF16, "mix_proj", rope=(tables, 2 * WIDTH_A))
    f_logit = mm_rows([(h2, w_f_t)], True, F32, "mix_forget")
    tab_a = dilated_table(s, ATTN_TQ, ATTN_TK)
    tab_b = causal_table(s, ATTN_TQ, ATTN_TK)
    ft = f_logit[:, :N_HEADS_B].reshape(nb, s, N_HEADS_B).transpose(0, 2, 1)
    bf_col = b_forget.reshape(N_HEADS_B, 1)
    colbias = fox_gate_fwd(ft, bf_col, "fox_gate").reshape(nb, N_HEADS_B, s // ATTN_TK, 1, ATTN_TK)
    pa = WIDTH_A // LANES
    (o_a, lse_a), ff2_all = attn_fwd(
        proj, 0, proj, pa, proj, 2 * pa, tab_a, None, nb, "attn_a",
        side=([bft(w_ff2_gate), bft(w_ff2_up), bf(w_ff2_down)], True))
    wg2, wu2, wd2 = (_rows_from_blocks(w, D_FF_PAD) for w in ff2_all)
    o_b, lse_b = attn_fwd(proj, 3 * pa, proj, 4 * pa, proj, 5 * pa, tab_b, colbias, nb, "attn_b", off_diag_bias=False)
    m_a = prenorm_fwd(o_a, g_out_a, None, None, None, nb, "out_norm_a")
    m_b = prenorm_fwd(o_b, g_out_b, None, None, None, nb, "out_norm_b")
    x2, y0m = postnorm_fwd(x1, [(m_a, w_o_a), (m_b, w_o_b)], g_post_mix, mod, 5, 1.0, nb, "mix_out_postnorm")

    (dx3, loss_part), saved2 = _ffn_forward(x2, mod, g_pre_ff2, g_post_ff2, wg2, wu2, wd2, 6, nb, "ff2", target=target)[:2]
    loss = lax.psum(loss_part[0, 0], ("x", "y", "c"))

    dx2, gr2 = _ffn_backward(dx3, saved2, mod, g_pre_ff2, g_post_ff2, wg2, wu2, wd2, 6, nb, "ff2")
    ff2_blocks = [_rows_to_blocks(gr2[k], D_FF)[:, None] for k in ("wg", "wu", "wd")]

    dy0m, dg_post_mix, dgate_mix = postnorm_bwd(dx2, y0m, g_post_mix, mod, 5, 1.0, nb, "mix_postnorm_bwd")
    dw_o = mm_tn_stack([m_a, m_b], dy0m, [WIDTH_A, WIDTH_B], BF16, "mix_dwo")
    do_a, dg_out_a = prenorm_bwd([(dy0m, w_o_a.T)], o_a, g_out_a, None, None, None, nb, "out_norm_a_bwd")
    do_b, dg_out_b = prenorm_bwd([(dy0m, w_o_b.T)], o_b, g_out_b, None, None, None, nb, "out_norm_b_bwd")
    (dq_a, dk_a, dv_a), g_ff2 = attn_bwd(proj, 0, proj, pa, proj, 2 * pa, o_a, lse_a, do_a, tab_a, None, nb,
                                            "attn_a_bwd", side=(ff2_blocks, False), rope_tabs=tables)
    dq_b, dk_b, dv_b, dcb, drow = attn_bwd(proj, 3 * pa, proj, 4 * pa, proj, 5 * pa, o_b, lse_b, do_b, tab_b, colbias, nb,
                                           "attn_b_bwd", off_diag_bias=False)
    dz_t, db_forget = fox_gate_bwd(dcb.reshape(nb, N_HEADS_B, s), drow.reshape(nb, N_HEADS_B, s), ft, bf_col,
                                   "fox_gate_bwd")
    dz = jnp.pad(dz_t.transpose(0, 2, 1).reshape(t, N_HEADS_B), ((0, 0), (0, LANES - N_HEADS_B))).astype(BF16)
    pieces = [dq_a, dk_a, dv_a, dq_b, dk_b, dv_b]
    w_pieces = [w_qkv_t[i * WIDTH_A:(i + 1) * WIDTH_A] for i in range(6)]
    dh2_pairs = list(zip(pieces, w_pieces)) + [(dz, w_f_t)]
    dw_in_t = mm_tn_stack(pieces + [dz], h2, [WIDTH_A] * 6 + [N_HEADS_B], BF16, "mix_dwin")
    g_in = _rows_to_blocks(dw_in_t, dw_in_t.shape[0])[:, None]
    g_out = _rows_to_blocks(dw_o, d)[:, None]
    (dx1, dg_pre_mix, dsc_mix, dsh_mix), (g_in, g_out) = prenorm_bwd(
        dh2_pairs, x1, g_pre_mix, mod, 4, dx2, nb, "mix_dh_prenorm_bwd", ts=DH_ROWS, side=([g_in, g_out], False))
    dx0, gr1 = _ffn_backward(dx1, saved1, mod, g_pre_ff1, g_post_ff1, wg1, wu1, wd1, 0, nb, "ff1", exchange=True)
    grad_x = dx0.reshape(nb, s, d)

    dmod =jnp.concatenate(list(gr1["mod"]) + [dsh_mix, dsc_mix, dgate_mix] + list(gr2["mod"]), axis=1)
    small = _pack_small(dict(g_pre_ff1=gr1["g_pre"], g_post_ff1=gr1["g_post"], g_pre_mix=dg_pre_mix,
                             g_post_mix=dg_post_mix, g_out_a=dg_out_a, g_out_b=dg_out_b, g_pre_ff2=gr2["g_pre"],
                             g_post_ff2=gr2["g_post"], b_forget=db_forget))
    dmod_all, small_all = all_gather([dmod.reshape(nb, N_MOD * d), small], "gather_small_grads")
    dmod_all = dmod_all.reshape(nbg, N_MOD * d)

    res = {}
    def adamw_t(parts, group, n):
        return tuple(_t(r) for r in adamw(parts, group, _t(weights[n])[0], _t(mom_m[n])[0], _t(mom_v[n])[0], f"adamw_{n}"))

    res["w_ff1_gate"] = adamw_t(gr1["wg"], 0, "w_ff1_gate")
    res["w_ff1_up"] = adamw_t(gr1["wu"], 0, "w_ff1_up")
    res["w_ff2_gate"] = adamw_t(g_ff2[0], 0, "w_ff2_gate")
    res["w_ff2_up"] = adamw_t(g_ff2[1], 0, "w_ff2_up")
    res["w_ff1_down"] = adamw(gr1["wd"], 0, w_ff1_down[0], m_w_ff1_down[0], v_w_ff1_down[0], "adamw_ff1_down")
    res["w_ff2_down"] = adamw(g_ff2[2], 0, w_ff2_down[0], m_w_ff2_down[0], v_w_ff2_down[0], "adamw_ff2_down")
    res["w_in"] = adamw_t(g_in, 0, "w_in")
    res["w_out"] = adamw(g_out, 0, w_out[0], m_w_out[0], v_w_out[0], "adamw_out")
    dmod_cols = lax.dynamic_slice(dmod_all, (0, me * ada_cols), (nbg, ada_cols))
    dw_ada = ada_bwd(c_all, dmod_cols, "ada_bwd")
    res["w_ada"] = adamw(dw_ada[None, None], 0, w_ada[0], m_w_ada[0], v_w_ada[0], "adamw_ada", tr=256)
    res["b_ada"] = adamw(dmod_all[:, None, None], 0, b_ada, m_b_ada, v_b_ada, "adamw_b_ada")
    sizes = {n: weights[n].shape[1] for n in SMALL_ORDER}
    small_res = adamw(small_all[:, None], 0, _pack_small(weights), _pack_small(mom_m), _pack_small(mom_v), "adamw_small")
    small_res = [_unpack_small(r, sizes) for r in small_res]
    for n in SMALL_ORDER:
        res[n] = tuple(r[n] for r in small_res)

    outs = [loss, grad_x]
    for kind in range(4):
        for n in order:
            a = res[n][kind]
            outs.append(a.reshape(weights[n].shape))
    return tuple(outs)
```

```python
import functools

import jax
import jax.numpy as jnp
from jax import lax
from jax.experimental import pallas as pl
from jax.experimental.pallas import tpu as pltpu

F32 = jnp.float32
BF16 = jnp.bfloat16

D_MODEL = 1024
HEAD_DIM = 64
N_HEADS_A = 8
N_HEADS_B = 8
WIDTH_A = N_HEADS_A * HEAD_DIM
WIDTH_B = N_HEADS_B * HEAD_DIM
DILATED_PATTERNS = ((128, 1), (512, 4), (2048, 16))
ROT_DIM = HEAD_DIM // 4
ROPE_THETA = 500000.0
D_FF = 2752
D_FF_PAD = 2816
N_MOD = 9
EPS = 1e-6
ATTN_SCALE = HEAD_DIM ** -0.5
NEG = -1e30
N_DEV = 8
LANES = 128
HEADS_PER_STEP = LANES // HEAD_DIM

ADAM_LR = 0.001
ADAM_B1 = 0.9
ADAM_B2 = 0.999
ADAM_EPS = 1e-08
ADAM_WD = 0.01
ADAM_STEP = 10

VMEM_LIMIT = 56 * 1024 * 1024
MESH = pl.DeviceIdType.MESH

NT_DIMS = (((1,), (1,)), ((), ()))
TN_DIMS = (((0,), (0,)), ((), ()))
NN_DIMS = (((1,), (0,)), ((), ()))


def _place():
    return lax.axis_index("x"), lax.axis_index("y"), lax.axis_index("c")


def _slot(p):
    return 4 * p[0] + 2 * p[1] + p[2]


def _direct_copies(ins, outs, send_sems, recv_sems, local_sems, gather):
    x, y, c = _place()
    me = (x, y, c)
    flip = lambda v, bit: 1 - v if bit else v
    peers = [(flip(x, k & 4), flip(y, k & 2), flip(c, k & 1)) for k in range(1, N_DEV)]
    local, sends, recvs = [], [], []
    for a in range(len(ins)):
        mine = ins[a] if gather else ins[a].at[_slot(me)]
        local.append(pltpu.make_async_copy(mine, outs[a].at[_slot(me)], local_sems.at[a]))
        for k, peer in enumerate(peers):
            sems = dict(send_sem=send_sems.at[a * 7 + k], recv_sem=recv_sems.at[a * 7 + k], device_id=peer,
                        device_id_type=MESH)
            sends.append(pltpu.make_async_remote_copy(
                src_ref=ins[a] if gather else ins[a].at[_slot(peer)], dst_ref=outs[a].at[_slot(me)], **sems))
            recvs.append(pltpu.make_async_remote_copy(src_ref=mine, dst_ref=outs[a].at[_slot(peer)], **sems))
    return local, sends, recvs


def _comm_scratch(n):
    return [pltpu.SemaphoreType.DMA((7 * n,)), pltpu.SemaphoreType.DMA((7 * n,)), pltpu.SemaphoreType.DMA((n,))]


def _pcall(body, side=None, **kw):
    if side is None:
        return pl.pallas_call(body, **kw)
    arrs, gather = side
    n = len(arrs)
    grid = kw["grid"]
    in_specs = list(kw["in_specs"])
    single = not isinstance(kw["out_specs"], (list, tuple))
    out_specs = [kw["out_specs"]] if single else list(kw["out_specs"])
    out_shape = [kw["out_shape"]] if single else list(kw["out_shape"])
    scratch = list(kw.get("scratch_shapes", []))
    n_in, n_out, n_scr = len(in_specs), len(out_specs), len(scratch)
    hbm = pl.BlockSpec(memory_space=pl.ANY)

    def hosted(*refs):
        pos = [0]

        def take(k):
            pos[0] += k
            return refs[pos[0] - k:pos[0]]

        ins, s_ins, outs, s_outs, scr, sems = take(n_in), take(n), take(n_out), take(n), take(n_scr), take(3)
        ids = [pl.program_id(i) for i in range(len(grid))]
        first = functools.reduce(jnp.logical_and, [i == 0 for i in ids])
        last = functools.reduce(jnp.logical_and, [i == g - 1 for i, g in zip(ids, grid)])

        @pl.when(first)
        def _():
            local, sends, _ = _direct_copies(s_ins, s_outs, *sems, gather)
            for cp in local + sends:
                cp.start()

        body(*ins, *outs, *scr)

        @pl.when(last)
        def _():
            local, sends, recvs = _direct_copies(s_ins, s_outs, *sems, gather)
            for cp in recvs:
                cp.wait_recv()
            for cp in sends:
                cp.wait_send()
            for cp in local:
                cp.wait()

    kw.update(in_specs=in_specs + [hbm] * n, out_specs=out_specs + [hbm] * n,
              out_shape=out_shape + [jax.ShapeDtypeStruct(((N_DEV,) + a.shape) if gather else a.shape, a.dtype)
                                     for a in arrs],
              scratch_shapes=scratch + _comm_scratch(n))
    call = pl.pallas_call(hosted, **kw)

    def run(*args):
        res = call(*args, *arrs)
        main = res[0] if single else list(res[:n_out])
        return main, list(res[n_out:])

    return run


def _params(sem=None, **kw):
    if sem is not None:
        kw["dimension_semantics"] = sem
    return pltpu.CompilerParams(vmem_limit_bytes=VMEM_LIMIT, **kw)


def _rotate(xv, c, sp, sm, transpose):
    width = xv.shape[1]
    half = ROT_DIM // 2
    if transpose:
        return xv * c + pltpu.roll(xv * sp, width - half, 1) + pltpu.roll(xv * sm, half, 1)
    return xv * c + pltpu.roll(xv, half, 1) * sp + pltpu.roll(xv, width - half, 1) * sm


def mm_rows(pairs, trans_b, out_dtype, name, tm=512, side=None, rope=None):
    n = len(pairs)
    m = pairs[0][0].shape[0]
    n_out = pairs[0][1].shape[0 if trans_b else 1]
    dims = NT_DIMS if trans_b else NN_DIMS

    def body(*refs):
        o_ref = refs[-1]
        acc = None
        for a_ref, b_ref in zip(refs[:n], refs[n:2 * n]):
            d = lax.dot_general(a_ref[...], b_ref[...], dims, preferred_element_type=F32)
            acc = d if acc is None else acc + d
        if rope is None:
            o_ref[...] = acc.astype(o_ref.dtype)
        else:
            width = rope[1]
            c, sp, sm = (jnp.concatenate([r[...]] * (width // LANES), axis=1) for r in refs[2 * n:2 * n + 3])
            o_ref[:, :width] = _rotate(acc[:, :width], c, sp, sm, False).astype(o_ref.dtype)
            o_ref[:, width:] = acc[:, width:].astype(o_ref.dtype)

    in_specs = [pl.BlockSpec((tm, a.shape[1]), lambda i: (i, 0)) for a, _ in pairs]
    in_specs += [pl.BlockSpec(b.shape, lambda i: (0, 0)) for _, b in pairs]
    args = [a for a, _ in pairs] + [b for _, b in pairs]
    if rope is not None:
        in_specs += [pl.BlockSpec((tm, LANES), lambda i: (i, 0))] * 3
        args += list(rope[0])
    return _pcall(
        body, side=side, name=name, grid=(m // tm,), in_specs=in_specs,
        out_specs=pl.BlockSpec((tm, n_out), lambda i: (i, 0)),
        out_shape=jax.ShapeDtypeStruct((m, n_out), out_dtype),
        compiler_params=_params(("arbitrary",)),
    )(*args)


DH_ROWS = 256
TN_TOKENS = 2048
TN_OUT_ELEMS = 2 * 1024 * 1024


def mm_tn(a, b, out_dtype, name, side=None, rows=None):
    t, ka = a.shape
    n_out = b.shape[1]
    tk = min(TN_TOKENS, t)
    tka = ka // 2 if ka * n_out > TN_OUT_ELEMS else ka
    tn = n_out
    steps = t // tk

    def body(a_ref, b_ref, o_ref, acc_ref):
        k = pl.program_id(2)
        d = lax.dot_general(a_ref[...], b_ref[...], TN_DIMS, preferred_element_type=F32)

        @pl.when(k == 0)
        def _():
            acc_ref[...] = d

        @pl.when(k > 0)
        def _():
            acc_ref[...] += d

        @pl.when(k == steps - 1)
        def _():
            o_ref[...] = acc_ref[...].astype(o_ref.dtype)

    return _pcall(
        body, side=side, name=name, grid=(ka // tka, n_out // tn, steps),
        in_specs=[pl.BlockSpec((tk, tka), lambda i, j, k: (k, i)), pl.BlockSpec((tk, tn), lambda i, j, k: (k, j))],
        out_specs=pl.BlockSpec((tka, tn), lambda i, j, k: (i, j)),
        out_shape=jax.ShapeDtypeStruct((ka if rows is None else rows, n_out), out_dtype),
        scratch_shapes=[pltpu.VMEM((tka, tn), F32)],
        compiler_params=_params(("arbitrary", "arbitrary", "arbitrary")),
    )(a, b)


def mm_tn_stack(a_list, b, rows, out_dtype, name, tk=1024):
    t, n_out = b.shape
    tk = min(tk, t)
    steps = t // tk
    n = len(a_list)
    offs = [sum(rows[:i]) for i in range(n)]

    def body(*refs):
        a_refs, b_ref, o_ref, acc_refs = refs[:n], refs[n], refs[n + 1], refs[n + 2:]
        k = pl.program_id(0)
        bv = b_ref[...]
        for a_ref, acc_ref in zip(a_refs, acc_refs):
            d = lax.dot_general(a_ref[...], bv, TN_DIMS, preferred_element_type=F32)

            @pl.when(k == 0)
            def _(acc_ref=acc_ref, d=d):
                acc_ref[...] = d

            @pl.when(k > 0)
            def _(acc_ref=acc_ref, d=d):
                acc_ref[...] += d

        @pl.when(k == steps - 1)
        def _():
            for acc_ref, off, r in zip(acc_refs, offs, rows):
                o_ref[off:off + r, :] = acc_ref[0:r, :].astype(o_ref.dtype)

    return _pcall(
        body, name=name, grid=(steps,),
        in_specs=[pl.BlockSpec((tk, a.shape[1]), lambda k: (k, 0)) for a in a_list]
        + [pl.BlockSpec((tk, n_out), lambda k: (k, 0))],
        out_specs=pl.BlockSpec((sum(rows), n_out), lambda k: (0, 0)),
        out_shape=jax.ShapeDtypeStruct((sum(rows), n_out), out_dtype),
        scratch_shapes=[pltpu.VMEM((a.shape[1], n_out), F32) for a in a_list],
        compiler_params=_params(("arbitrary",)),
    )(*a_list, b)


def _col_chunks(width, chunk=512):
    return [slice(c, min(c + chunk, width)) for c in range(0, width, chunk)]


def _sigmoid(x):
    return 1.0 / (1.0 + jnp.exp(-x))


def ffn_up(h, wgt, wut, name, tm=256, tn=D_FF_PAD, side=None):
    t, d = h.shape
    fp = wgt.shape[0]

    def body(h_ref, wg_ref, wu_ref, g_ref, u_ref, a_ref):
        hv = h_ref[...]

        def finish(cols, g, u):
            g_ref[:, cols] = g.astype(BF16)
            u_ref[:, cols] = u.astype(BF16)
            a_ref[:, cols] = (g * _sigmoid(g) * u).astype(BF16)

        pending = None
        for cols in _col_chunks(tn):
            g = lax.dot_general(hv, wg_ref[cols, :], NT_DIMS, preferred_element_type=F32)
            u = lax.dot_general(hv, wu_ref[cols, :], NT_DIMS, preferred_element_type=F32)
            if pending is not None:
                finish(*pending)
            pending = (cols, g, u)
        finish(*pending)

    w_spec = pl.BlockSpec((tn, d), lambda j, i: (j, 0))
    o_spec = pl.BlockSpec((tm, tn), lambda j, i: (i, j))
    o_shape = jax.ShapeDtypeStruct((t, fp), BF16)
    return _pcall(
        body, side=side, name=name, grid=(fp // tn, t // tm),
        in_specs=[pl.BlockSpec((tm, d), lambda j, i: (i, 0)), w_spec, w_spec],
        out_specs=[o_spec, o_spec, o_spec], out_shape=[o_shape, o_shape, o_shape],
        compiler_params=_params(("arbitrary", "arbitrary")),
    )(h, wgt, wut)


def ffn_down_bwd(dy0, wd, gate, up, name, tm=256, tn=D_FF_PAD, side=None):
    t, d = dy0.shape
    fp = wd.shape[0]

    def body(dy_ref, wd_ref, g_ref, u_ref, dg_ref, du_ref):
        dyv = dy_ref[...]

        def finish(cols, dact):
            g = g_ref[:, cols].astype(F32)
            u = u_ref[:, cols].astype(F32)
            sg = _sigmoid(g)
            silu = g * sg
            du_ref[:, cols] = (dact * silu).astype(BF16)
            dg_ref[:, cols] = ((dact * u) * (sg + silu * (1.0 - sg))).astype(BF16)

        pending = None
        for cols in _col_chunks(tn):
            dact = lax.dot_general(dyv, wd_ref[cols, :], NT_DIMS, preferred_element_type=F32)
            if pending is not None:
                finish(*pending)
            pending = (cols, dact)
        finish(*pending)

    t_spec = pl.BlockSpec((tm, tn), lambda j, i: (i, j))
    o_shape = jax.ShapeDtypeStruct((t, fp), BF16)
    return _pcall(
        body, side=side, name=name, grid=(fp // tn, t // tm),
        in_specs=[pl.BlockSpec((tm, d), lambda j, i: (i, 0)), pl.BlockSpec((tn, d), lambda j, i: (j, 0)), t_spec, t_spec],
        out_specs=[t_spec, t_spec], out_shape=[o_shape, o_shape],
        compiler_params=_params(("arbitrary", "arbitrary")),
    )(dy0, wd, gate, up)


def _row_specs(dx, ts, ns):
    return pl.BlockSpec((ts, dx), lambda b, s: (b * ns + s, 0))


def _mod_spec():
    return pl.BlockSpec((1, N_MOD, D_MODEL), lambda b, s: (b, 0, 0))


def _vec_spec(dx):
    return pl.BlockSpec((1, dx), lambda b, s: (0, 0))


def prenorm_fwd(x, g, mod, i_shift, i_scale, nb, name, ts=1024):
    t, dx = x.shape
    ts = min(ts, t // nb)
    ns = t // nb // ts

    def body(*refs):
        if mod is None:
            x_ref, g_ref, h_ref = refs
        else:
            x_ref, g_ref, mod_ref, h_ref = refs
        xv = x_ref[...]
        r = lax.rsqrt(jnp.mean(xv * xv, axis=-1, keepdims=True) + EPS)
        h = xv * r * g_ref[...]
        if mod is not None:
            h = h * (1.0 + mod_ref[0, i_scale:i_scale + 1, :]) + mod_ref[0, i_shift:i_shift + 1, :]
        h_ref[...] = h.astype(BF16)

    in_specs = [_row_specs(dx, ts, ns), _vec_spec(dx)]
    args = [x, g]
    if mod is not None:
        in_specs.append(_mod_spec())
        args.append(mod)
    return _pcall(
        body, name=name, grid=(nb, ns), in_specs=in_specs, out_specs=_row_specs(dx, ts, ns),
        out_shape=jax.ShapeDtypeStruct((t, dx), BF16), compiler_params=_params(("arbitrary", "arbitrary")),
    )(*args)


def prenorm_bwd(dh, x, g, mod, i_scale, dres, nb, name, ts=512, side=None):
    t, dx = x.shape
    ts = min(ts, t // nb)
    ns = t // nb // ts
    has_mod = mod is not None
    has_res = dres is not None
    pairs = dh if isinstance(dh, list) else None
    n_mm = 0 if pairs is None else len(pairs)

    def body(*refs):
        refs = list(refs)
        if pairs is None:
            dhv = refs[0][...].astype(F32)
            refs = refs[1:]
        else:
            dhv = None
            for a_ref, b_ref in zip(refs[:n_mm], refs[n_mm:2 * n_mm]):
                d = jnp.dot(a_ref[...], b_ref[...], preferred_element_type=F32)
                dhv = d if dhv is None else dhv + d
            refs = refs[2 * n_mm:]
        x_ref, g_ref = refs[:2]
        pos = 2
        mod_ref = dres_ref = None
        if has_mod:
            mod_ref = refs[pos]
            pos += 1
        if has_res:
            dres_ref = refs[pos]
            pos += 1
        dx_ref, dg_ref = refs[pos], refs[pos + 1]
        b, s = pl.program_id(0), pl.program_id(1)
        xv = x_ref[...]
        gv = g_ref[...]
        r = lax.rsqrt(jnp.mean(xv * xv, axis=-1, keepdims=True) + EPS)
        xhat = xv * r
        dn = dhv
        if has_mod:
            dsc_ref, dsh_ref = refs[pos + 2], refs[pos + 3]
            dn = dhv * (1.0 + mod_ref[0, i_scale:i_scale + 1, :])
            dsc = jnp.sum(dhv * xhat * gv, axis=0, keepdims=True)[None]
            dsh = jnp.sum(dhv, axis=0, keepdims=True)[None]

            @pl.when(s == 0)
            def _():
                dsc_ref[...] = dsc
                dsh_ref[...] = dsh

            @pl.when(s > 0)
            def _():
                dsc_ref[...] += dsc
                dsh_ref[...] += dsh

        dg = jnp.sum(dn * xhat, axis=0, keepdims=True)
        first = jnp.logical_and(b == 0, s == 0)

        @pl.when(first)
        def _():
            dg_ref[...] = dg

        @pl.when(jnp.logical_not(first))
        def _():
            dg_ref[...] += dg

        dxhat = dn * gv
        dxv = r * (dxhat - xhat * jnp.mean(dxhat * xhat, axis=-1, keepdims=True))
        if has_res:
            dxv = dxv + dres_ref[...]
        dx_ref[...] = dxv

    row = _row_specs(dx, ts, ns)
    if pairs is None:
        in_specs, args = [row], [dh]
    else:
        in_specs = [_row_specs(a.shape[1], ts, ns) for a, _ in pairs]
        in_specs += [pl.BlockSpec(b.shape, lambda b_, s_: (0, 0)) for _, b in pairs]
        args = [a for a, _ in pairs] + [b for _, b in pairs]
    in_specs += [row, _vec_spec(dx)]
    args += [x, g]
    if has_mod:
        in_specs.append(_mod_spec())
        args.append(mod)
    if has_res:
        in_specs.append(row)
        args.append(dres)
    out_specs = [row, _vec_spec(dx)]
    out_shape = [jax.ShapeDtypeStruct((t, dx), F32), jax.ShapeDtypeStruct((1, dx), F32)]
    if has_mod:
        bspec = pl.BlockSpec((1, 1, dx), lambda b, s: (b, 0, 0))
        out_specs += [bspec, bspec]
        out_shape += [jax.ShapeDtypeStruct((nb, 1, dx), F32)] * 2
    return _pcall(
        body, side=side, name=name, grid=(nb, ns), in_specs=in_specs, out_specs=out_specs, out_shape=out_shape,
        compiler_params=_params(("arbitrary", "arbitrary")),
    )(*args)


def postnorm_fwd(x, pairs, g, mod, i_gate, coef, nb, name, target=None, ts=512, side=None):
    t, dx = x.shape
    with_loss = target is not None
    ts = min(ts, t // nb)
    ns = t // nb // ts
    n_mm = len(pairs)

    def body(*refs):
        yv = None
        for a_ref, b_ref in zip(refs[:n_mm], refs[n_mm:2 * n_mm]):
            d = jnp.dot(a_ref[...], b_ref[...], preferred_element_type=F32)
            yv = d if yv is None else yv + d
        refs = refs[2 * n_mm:]
        x_ref, g_ref, mod_ref = refs[:3]
        refs[-1][...] = yv.astype(BF16)
        r = lax.rsqrt(jnp.mean(yv * yv, axis=-1, keepdims=True) + EPS)
        out = x_ref[...] + (coef * mod_ref[0, i_gate:i_gate + 1, :]) * (yv * r * g_ref[...])
        if not with_loss:
            refs[3][...] = out
            return
        t_ref, dx_ref, loss_ref = refs[3:6]
        b, s = pl.program_id(0), pl.program_id(1)
        err = out - t_ref[...]
        dx_ref[...] = err * (1.0 / dx)
        part = (0.5 / dx) * jnp.sum(jnp.sum(err * err, axis=1, keepdims=True), axis=0, keepdims=True)
        first = jnp.logical_and(b == 0, s == 0)

        @pl.when(first)
        def _():
            loss_ref[...] = part

        @pl.when(jnp.logical_not(first))
        def _():
            loss_ref[...] += part

    row = _row_specs(dx, ts, ns)
    in_specs = [_row_specs(a.shape[1], ts, ns) for a, _ in pairs]
    in_specs += [pl.BlockSpec(b.shape, lambda b_, s_: (0, 0)) for _, b in pairs]
    in_specs += [row, _vec_spec(dx), _mod_spec()]
    args = [a for a, _ in pairs] + [b for _, b in pairs] + [x, g, mod]
    row_shape = jax.ShapeDtypeStruct((t, dx), F32)
    y0_shape = jax.ShapeDtypeStruct((t, dx), BF16)
    out_specs, out_shape = [row, row], [row_shape, y0_shape]
    if with_loss:
        in_specs.append(row)
        args.append(target)
        out_specs = [row, pl.BlockSpec((1, 1), lambda b, s: (0, 0)), row]
        out_shape = [row_shape, jax.ShapeDtypeStruct((1, 1), F32), y0_shape]
    return _pcall(
        body, side=side, name=name, grid=(nb, ns), in_specs=in_specs, out_specs=out_specs, out_shape=out_shape,
        compiler_params=_params(("arbitrary", "arbitrary")),
    )(*args)


def postnorm_bwd(dxo, y0, g, mod, i_gate, coef, nb, name, ts=1024):
    t, dx = y0.shape
    ts = min(ts, t // nb)
    ns = t // nb // ts

    def body(d_ref, y_ref, g_ref, mod_ref, dy_ref, dg_ref, dgate_ref):
        b, s = pl.program_id(0), pl.program_id(1)
        yv = y_ref[...].astype(F32)
        dv = d_ref[...]
        gv = g_ref[...]
        r = lax.rsqrt(jnp.mean(yv * yv, axis=-1, keepdims=True) + EPS)
        yhat = yv * r
        dgate = jnp.sum(dv * (coef * (yhat * gv)), axis=0, keepdims=True)[None]
        dyn = dv * (coef * mod_ref[0, i_gate:i_gate + 1, :])
        dg = jnp.sum(dyn * yhat, axis=0, keepdims=True)
        dyhat = dyn * gv
        dy_ref[...] = (r * (dyhat - yhat * jnp.mean(dyhat * yhat, axis=-1, keepdims=True))).astype(BF16)

        @pl.when(s == 0)
        def _():
            dgate_ref[...] = dgate

        @pl.when(s > 0)
        def _():
            dgate_ref[...] += dgate

        first = jnp.logical_and(b == 0, s == 0)

        @pl.when(first)
        def _():
            dg_ref[...] = dg

        @pl.when(jnp.logical_not(first))
        def _():
            dg_ref[...] += dg

    row = _row_specs(dx, ts, ns)
    return _pcall(
        body, name=name, grid=(nb, ns), in_specs=[row, row, _vec_spec(dx), _mod_spec()],
        out_specs=[row, _vec_spec(dx), pl.BlockSpec((1, 1, dx), lambda b, s: (b, 0, 0))],
        out_shape=[jax.ShapeDtypeStruct((t, dx), BF16), jax.ShapeDtypeStruct((1, dx), F32),
                   jax.ShapeDtypeStruct((nb, 1, dx), F32)],
        compiler_params=_params(("arbitrary", "arbitrary")),
    )(dxo, y0, g, mod)


def rope_tables(positions):
    inv_freq = ROPE_THETA ** (-jnp.arange(0, ROT_DIM, 2, dtype=F32) / ROT_DIM)
    ang = positions.astype(F32).reshape(-1, 1) * inv_freq
    cos, sin = jnp.cos(ang), jnp.sin(ang)
    half = ROT_DIM // 2
    z = lambda n: jnp.zeros((ang.shape[0], n), F32)
    c = jnp.concatenate([cos, cos, jnp.ones((ang.shape[0], HEAD_DIM - ROT_DIM), F32)], axis=1)
    sp = jnp.concatenate([z(half), sin, z(HEAD_DIM - ROT_DIM)], axis=1)
    sm = jnp.concatenate([-sin, z(HEAD_DIM - half)], axis=1)
    return tuple(jnp.tile(a, (1, HEADS_PER_STEP)) for a in (c, sp, sm))


def _scan_lanes(x, reverse):
    n = x.shape[-1]
    lane = lax.broadcasted_iota(jnp.int32, x.shape, x.ndim - 1)
    k = 1
    while k < n:
        if reverse:
            x = x + jnp.where(lane < n - k, pltpu.roll(x, n - k, x.ndim - 1), 0.0)
        else:
            x = x + jnp.where(lane >= k, pltpu.roll(x, k, x.ndim - 1), 0.0)
        k *= 2
    return x


def _log_sigmoid(z):
    return jnp.minimum(z, 0.0) - jnp.log(1.0 + jnp.exp(-jnp.abs(z)))


def fox_gate_fwd(ft, b_forget, name):
    nb, nh, s = ft.shape

    def body(f_ref, b_ref, o_ref):
        z = f_ref[0] + b_ref[...]
        o_ref[0] = -_scan_lanes(_log_sigmoid(z), False)

    spec = pl.BlockSpec((1, nh, s), lambda b: (b, 0, 0))
    return _pcall(
        body, name=name, grid=(nb,), in_specs=[spec, pl.BlockSpec((nh, 1), lambda b: (0, 0))], out_specs=spec,
        out_shape=jax.ShapeDtypeStruct((nb, nh, s), F32), compiler_params=_params(("arbitrary",)),
    )(ft, b_forget)


def fox_gate_bwd(dcb, drow, ft, b_forget, name):
    nb, nh, s = ft.shape

    def body(d_ref, r_ref, f_ref, b_ref, dz_ref, db_ref):
        b = pl.program_id(0)
        z = f_ref[0] + b_ref[...]
        dlf = _scan_lanes(r_ref[0] - d_ref[0], True)
        dz = dlf * _sigmoid(-z)
        dz_ref[0] = dz
        db = jnp.sum(dz, axis=1, keepdims=True)

        @pl.when(b == 0)
        def _():
            db_ref[...] = db

        @pl.when(b > 0)
        def _():
            db_ref[...] += db

    spec = pl.BlockSpec((1, nh, s), lambda b: (b, 0, 0))
    vec = pl.BlockSpec((nh, 1), lambda b: (0, 0))
    return _pcall(
        body, name=name, grid=(nb,), in_specs=[spec, spec, spec, vec], out_specs=[spec, vec],
        out_shape=[jax.ShapeDtypeStruct((nb, nh, s), F32), jax.ShapeDtypeStruct((nh, 1), F32)],
        compiler_params=_params(("arbitrary",)),
    )(dcb, drow, ft, b_forget)


ATTN_TQ = 512
ATTN_TK = 512
ONES_ROWS = 16


def _rows_to_cols(rows):
    tile = jnp.concatenate([jnp.broadcast_to(rw, (HEAD_DIM, rw.shape[1])) for rw in rows], axis=0)
    return tile.T


def _block_delta(s, tq, tk):
    off = jnp.arange(s // tk) - (tq // tk - 1)
    return off[:, None, None] * tk + jnp.arange(tq)[None, None, :] - jnp.arange(tk)[None, :, None]


def dilated_table(s, tq, tk):
    delta = _block_delta(s, tq, tk)
    count = jnp.zeros(delta.shape, F32)
    for window, dil in DILATED_PATTERNS:
        count = count + ((delta >= 0) & (delta <= window) & (delta % dil == 0)).astype(F32)
    return jnp.where(count > 0, jnp.log(jnp.maximum(count, 1.0)), NEG)


def causal_table(s, tq, tk):
    return jnp.where(_block_delta(s, tq, tk) >= 0, 0.0, NEG).astype(F32)


def attn_fwd(q_arr, q_off, k_arr, k_off, v_arr, v_off, table, colbias, nb, name, side=None, off_diag_bias=True):
    t = q_arr.shape[0]
    s = t // nb
    tk, tq = table.shape[1:]
    assert tq == tk, "the diagonal handling below is written for square tiles"
    nq, nk = s // tq, s // tk
    npairs = WIDTH_A // LANES
    use_cb = colbias is not None

    def body(*refs):
        refs = list(refs)
        q_ref, k_ref, v_ref, tab_ref = refs[:4]
        cb_ref = refs[4] if use_cb else None
        tail = refs[-(HEADS_PER_STEP + int(use_cb)):]
        acc_s = tail[:HEADS_PER_STEP]
        cbc_s = tail[-1] if use_cb else None
        o_ref, lse_ref, vt_s = refs[-3 - len(tail):-len(tail)]
        qi = pl.program_id(2)

        heads = [slice(h * HEAD_DIM, (h + 1) * HEAD_DIM) for h in range(HEADS_PER_STEP)]

        @pl.when(qi == 0)
        def _():
            for cblk in range(nk):
                vt = v_ref[cblk * tk:(cblk + 1) * tk, :].astype(F32).T.astype(BF16)
                for h, hs in enumerate(heads):
                    vt_s[cblk, h, 0:HEAD_DIM, :] = vt[hs, :]
                    vt_s[cblk, h, HEAD_DIM:, :] = jnp.ones((ONES_ROWS, tk), BF16)
                if use_cb:
                    cbc_s[cblk] = _rows_to_cols([cb_ref[0, h, cblk] for h in range(HEADS_PER_STEP)])

        qt_all = (q_ref[...].astype(F32) * ATTN_SCALE).T.astype(BF16)
        qts = [qt_all[hs, :] for hs in heads]
        for a in acc_s:
            a[...] = jnp.zeros_like(a)

        def tile(kb, tab, k0, klen, q0, carry):
            ks = pl.multiple_of(kb * tk + k0, klen)
            sts, out = [], []
            for h, hs in enumerate(heads):
                st = jnp.dot(k_ref[pl.ds(ks, klen), hs], qts[h][:, q0:], preferred_element_type=F32)
                if tab is not None:
                    st = st + tab
                if use_cb:
                    st = st + cbc_s[kb, k0:k0 + klen, h * HEAD_DIM:h * HEAD_DIM + 1]
                sts.append(st)
            m_old = [carry[h][:, q0:] for h in range(HEADS_PER_STEP)]
            m_new = [jnp.maximum(m_old[h], jnp.max(sts[h], axis=0, keepdims=True)) for h in range(HEADS_PER_STEP)]
            for h in range(HEADS_PER_STEP):
                pt = jnp.exp(sts[h] - m_new[h]).astype(BF16)
                acc_s[h][:, q0:] = (jnp.exp(m_old[h] - m_new[h]) * acc_s[h][:, q0:]
                                    + jnp.dot(vt_s[kb, h, :, k0:k0 + klen], pt, preferred_element_type=F32))
                out.append(m_new[h] if q0 == 0 else jnp.concatenate([carry[h][:, :q0], m_new[h]], axis=1))
            return tuple(out)

        fin = lax.fori_loop(0, qi, lambda kb, c: tile(kb, tab_ref[qi - kb] if off_diag_bias else None, 0, tk, 0, c),
                            tuple(jnp.full((1, tq), NEG, F32) for _ in heads))
        half = tk // 2
        fin = tile(qi, tab_ref[0, 0:half, :], 0, half, 0, fin)
        fin = tile(qi, tab_ref[0, half:, half:], half, half, half, fin)
        outs = []
        for h in range(HEADS_PER_STEP):
            l = acc_s[h][HEAD_DIM:HEAD_DIM + 1, :]
            outs.append(acc_s[h][0:HEAD_DIM, :] / l)
            lse_ref[0, h, 0] = fin[h] + jnp.log(l)
        o_ref[...] = jnp.concatenate(outs, axis=0).T

    def seq_spec(off):
        return pl.BlockSpec((s, LANES), lambda b, j, i: (b, off + j))

    in_specs = [pl.BlockSpec((tq, LANES), lambda b, j, i: (b * nq + i, q_off + j)), seq_spec(k_off), seq_spec(v_off),
                pl.BlockSpec(table.shape, lambda b, j, i: (0, 0, 0))]
    args = [q_arr, k_arr, v_arr, table]
    if use_cb:
        in_specs.append(pl.BlockSpec((1, HEADS_PER_STEP, nk, 1, tk), lambda b, j, i: (b, j, 0, 0, 0)))
        args.append(colbias)
    n_heads = npairs * HEADS_PER_STEP
    return _pcall(
        body, side=side, name=name, grid=(nb, npairs, nq), in_specs=in_specs,
        out_specs=[pl.BlockSpec((tq, LANES), lambda b, j, i: (b * nq + i, j)),
                   pl.BlockSpec((1, HEADS_PER_STEP, 1, 1, tq), lambda b, j, i: (b, j, i, 0, 0))],
        out_shape=[jax.ShapeDtypeStruct((t, npairs * LANES), F32), jax.ShapeDtypeStruct((nb, n_heads, nq, 1, tq), F32)],
        scratch_shapes=[pltpu.VMEM((nk, HEADS_PER_STEP, HEAD_DIM + ONES_ROWS, tk), BF16)]
        + [pltpu.VMEM((HEAD_DIM + ONES_ROWS, tq), F32)] * HEADS_PER_STEP
        + ([pltpu.VMEM((nk, tk, LANES), F32)] if use_cb else []),
        compiler_params=_params(("arbitrary", "arbitrary", "arbitrary")),
    )(*args)


def attn_bwd(q_arr, q_off, k_arr, k_off, v_arr, v_off, o_arr, lse_arr, do_arr, table, colbias, nb, name, side=None,
             rope_tabs=None, off_diag_bias=True):
    t = q_arr.shape[0]
    s = t // nb
    tk, tq = table.shape[1:]
    assert tq == tk, "the diagonal handling below is written for square tiles"
    nq, nk = s // tq, s // tk
    npairs = WIDTH_A // LANES
    use_cb = colbias is not None

    def body(*refs):
        refs = list(refs)
        q_ref, k_ref, v_ref, o_ref, lse_ref, do_ref, tab_ref = refs[:7]
        pos = 7
        cb_ref = None
        if use_cb:
            cb_ref = refs[pos]
            pos += 1
        rope_refs = None
        if rope_tabs is not None:
            rope_refs = refs[pos:pos + 3]
            pos += 3
        dq_ref, dk_ref, dv_ref = refs[pos:pos + 3]
        pos += 3
        dcb_ref = drow_ref = None
        if use_cb:
            dcb_ref, drow_ref = refs[pos:pos + 2]
            pos += 2
        kt_s, dkt_s, dvt_s = refs[pos:pos + 3]
        dqt_s = refs[pos + 3:pos + 3 + HEADS_PER_STEP]
        dcb_s, cbc_s = refs[pos + 3 + HEADS_PER_STEP:pos + 5 + HEADS_PER_STEP] if use_cb else (None, None)

        heads = [slice(h * HEAD_DIM, (h + 1) * HEAD_DIM) for h in range(HEADS_PER_STEP)]
        for cblk in range(nk):
            kt_s[cblk] = k_ref[cblk * tk:(cblk + 1) * tk, :].astype(F32).T.astype(BF16)
        dkt_s[...] = jnp.zeros_like(dkt_s)
        dvt_s[...] = jnp.zeros_like(dvt_s)
        if use_cb:
            dcb_s[...] = jnp.zeros_like(dcb_s)
            for cblk in range(nk):
                cbc_s[cblk] = _rows_to_cols([cb_ref[0, h, cblk] for h in range(HEADS_PER_STEP)])
        ones = jnp.ones((8, HEAD_DIM), BF16)

        def q_loop(qi, carry):
            qs = pl.multiple_of(qi * tq, tq)
            q_all = (q_ref[pl.ds(qs, tq), :].astype(F32) * ATTN_SCALE)
            do_all = do_ref[pl.ds(qs, tq), :]
            qt_all = q_all.T.astype(BF16)
            dot_all = do_all.T.astype(BF16)
            qt, dot, lse, dsum = [], [], [], []
            for h, hs in enumerate(heads):
                qt.append(qt_all[hs, :])
                dot.append(dot_all[hs, :])
                lse.append(lse_ref[0, h, qi])
                prod = do_all[:, hs] * o_ref[pl.ds(qs, tq), hs]
                hi = prod.astype(BF16)
                lo = (prod - hi.astype(F32)).astype(BF16)
                dsum.append((lax.dot_general(ones, hi, NT_DIMS, preferred_element_type=F32)
                             + lax.dot_general(ones, lo, NT_DIMS, preferred_element_type=F32))[0:1, :])
            for a in dqt_s:
                a[...] = jnp.zeros_like(a)

            def tile(kb, tab, k0, klen, q0, drow):
                ks = pl.multiple_of(kb * tk + k0, klen)
                keys = slice(k0, k0 + klen)
                sts, dpts, out = [], [], []
                for h, hs in enumerate(heads):
                    st = jnp.dot(k_ref[pl.ds(ks, klen), hs], qt[h][:, q0:], preferred_element_type=F32)
                    if tab is not None:
                        st = st + tab
                    if use_cb:
                        st = st + cbc_s[kb, keys, h * HEAD_DIM:h * HEAD_DIM + 1]
                    sts.append(st)
                    dpts.append(jnp.dot(v_ref[pl.ds(ks, klen), hs], dot[h][:, q0:], preferred_element_type=F32))
                for h, hs in enumerate(heads):
                    pt = jnp.exp(sts[h] - lse[h][:, q0:])
                    dst = pt * (dpts[h] - dsum[h][:, q0:])
                    dst_b = dst.astype(BF16)
                    dvt_s[h, kb, :, keys] += lax.dot_general(dot[h][:, q0:], pt.astype(BF16), NT_DIMS,
                                                             preferred_element_type=F32)
                    dkt_s[h, kb, :, keys] += lax.dot_general(qt[h][:, q0:], dst_b, NT_DIMS, preferred_element_type=F32)
                    dqt_s[h][:, q0:] += jnp.dot(kt_s[kb, hs, keys], dst_b, preferred_element_type=F32)
                    if use_cb:
                        dcb_s[h, pl.ds(ks, klen), :] += jnp.sum(dst, axis=1, keepdims=True)
                        dr = drow[h][:, q0:] + jnp.sum(dst, axis=0, keepdims=True)
                        out.append(dr if q0 == 0 else jnp.concatenate([drow[h][:, :q0], dr], axis=1))
                    else:
                        out.append(drow[h])
                return tuple(out)

            drow = lax.fori_loop(0, qi, lambda kb, c: tile(kb, tab_ref[qi - kb] if off_diag_bias else None, 0, tk, 0, c),
                                 tuple(jnp.zeros((1, tq), F32) for _ in heads))
            half = tk // 2
            drow = tile(qi, tab_ref[0, 0:half, :], 0, half, 0, drow)
            drow = tile(qi, tab_ref[0, half:, half:], half, half, half, drow)
            dq = (jnp.concatenate([a[...] for a in dqt_s], axis=0) * ATTN_SCALE).T
            if rope_refs is not None:
                dq = _rotate(dq, *[coef[pl.ds(qs, tq), :] for coef in rope_refs], True)
            dq_ref[pl.ds(qs, tq), :] = dq.astype(dq_ref.dtype)
            if use_cb:
                for h in range(HEADS_PER_STEP):
                    drow_ref[0, h, qi] = drow[h]
            return carry

        lax.fori_loop(0, nq, q_loop, 0)
        for cblk in range(nk):
            rows = slice(cblk * tk, (cblk + 1) * tk)
            dk = jnp.concatenate([dkt_s[h, cblk] for h in range(HEADS_PER_STEP)], axis=0).T
            if rope_refs is not None:
                dk = _rotate(dk, *[coef[rows, :] for coef in rope_refs], True)
            dk_ref[rows, :] = dk.astype(dk_ref.dtype)
            dv_ref[rows, :] = jnp.concatenate([dvt_s[h, cblk] for h in range(HEADS_PER_STEP)], axis=0).T.astype(dv_ref.dtype)
            if use_cb:
                for h in range(HEADS_PER_STEP):
                    dcb_ref[0, h, cblk] = jnp.broadcast_to(dcb_s[h, rows, :], (tk, LANES)).T[0:1, :]

    def seq_spec(off):
        return pl.BlockSpec((s, LANES), lambda b, j: (b, off + j))

    row_spec = pl.BlockSpec((1, HEADS_PER_STEP, nq, 1, tq), lambda b, j: (b, j, 0, 0, 0))
    in_specs = [seq_spec(q_off), seq_spec(k_off), seq_spec(v_off), seq_spec(0), row_spec, seq_spec(0),
                pl.BlockSpec(table.shape, lambda b, j: (0, 0, 0))]
    args = [q_arr, k_arr, v_arr, o_arr, lse_arr, do_arr, table]
    width = npairs * LANES
    out_specs = [seq_spec(0)] * 3
    out_shape = [jax.ShapeDtypeStruct((t, width), BF16)] * 3
    scratch = [pltpu.VMEM((nk, LANES, tk), BF16), pltpu.VMEM((HEADS_PER_STEP, nk, HEAD_DIM, tk), F32),
               pltpu.VMEM((HEADS_PER_STEP, nk, HEAD_DIM, tk), F32)] + [pltpu.VMEM((HEAD_DIM, tq), F32)] * HEADS_PER_STEP
    if use_cb:
        cb_spec = pl.BlockSpec((1, HEADS_PER_STEP, nk, 1, tk), lambda b, j: (b, j, 0, 0, 0))
        in_specs.append(cb_spec)
        args.append(colbias)
    if rope_tabs is not None:
        in_specs += [pl.BlockSpec((s, LANES), lambda b, j: (b, 0))] * 3
        args += list(rope_tabs)
    if use_cb:
        out_specs += [cb_spec, row_spec]
        out_shape += [jax.ShapeDtypeStruct(colbias.shape, F32), jax.ShapeDtypeStruct(lse_arr.shape, F32)]
        scratch += [pltpu.VMEM((HEADS_PER_STEP, s, 1), F32), pltpu.VMEM((nk, tk, LANES), F32)]
    return _pcall(
        body, side=side, name=name, grid=(nb, npairs), in_specs=in_specs, out_specs=out_specs, out_shape=out_shape,
        scratch_shapes=scratch, compiler_params=_params(("arbitrary", "arbitrary")),
    )(*args)


def ada_fwd(c_all, w_ada, b_cols, name):
    def body(c_ref, w_ref, b_ref, o_ref):
        cv = c_ref[...]
        sc = (cv * _sigmoid(cv)).astype(BF16)
        o_ref[...] = jnp.dot(sc, w_ref[...].astype(BF16), preferred_element_type=F32) + b_ref[...]

    return _pcall(body, name=name, out_shape=jax.ShapeDtypeStruct((c_all.shape[0], w_ada.shape[1]), F32),
                  compiler_params=_params())(c_all, w_ada, b_cols)


def ada_bwd(c_all, dmod_cols, name):
    def body(c_ref, d_ref, o_ref):
        cv = c_ref[...]
        sc = (cv * _sigmoid(cv)).astype(BF16)
        o_ref[...] = lax.dot_general(sc, d_ref[...].astype(BF16), TN_DIMS, preferred_element_type=F32)

    return _pcall(body, name=name, out_shape=jax.ShapeDtypeStruct((c_all.shape[1], dmod_cols.shape[1]), F32),
                  compiler_params=_params())(c_all, dmod_cols)


def adamw(parts, group, w, m, v, name, tr=None):
    n = parts.shape[0]
    r, c = w.shape
    tr = r if tr is None else tr
    c1 = 1.0 - ADAM_B1 ** ADAM_STEP
    c2 = 1.0 - ADAM_B2 ** ADAM_STEP

    def body(p_ref, w_ref, m_ref, v_ref, g_ref, d_ref, nm_ref, nv_ref):
        g = p_ref[0, 0].astype(F32)
        for i in range(1, n):
            g = g + p_ref[i, 0].astype(F32)
        wv = w_ref[...]
        nm = ADAM_B1 * m_ref[...] + (1.0 - ADAM_B1) * g
        nv = ADAM_B2 * v_ref[...] + (1.0 - ADAM_B2) * (g * g)
        g_ref[...] = g
        nm_ref[...] = nm
        nv_ref[...] = nv
        d_ref[...] = -ADAM_LR * ((nm / c1) / (jnp.sqrt(nv / c2) + ADAM_EPS) + ADAM_WD * wv)

    spec = pl.BlockSpec((tr, c), lambda i: (i, 0))
    shape = jax.ShapeDtypeStruct((r, c), F32)
    return _pcall(
        body, name=name, grid=(r // tr,),
        in_specs=[pl.BlockSpec((n, 1, tr, c), lambda i: (0, group, i, 0)), spec, spec, spec],
        out_specs=[spec] * 4, out_shape=[shape] * 4, compiler_params=_params(("arbitrary",)),
    )(parts, w, m, v)


def all_gather(arrs, name):
    n = len(arrs)
    hbm = pl.BlockSpec(memory_space=pl.ANY)

    def body(*refs):
        ins, outs = refs[:n], refs[n:2 * n]
        send_sems, recv_sems, local_sems = refs[2 * n:]
        x, y, c = _place()
        me, sibling = (x, y, c), (x, y, 1 - c)
        chips = [(1 - x, y), (x, 1 - y), (1 - x, 1 - y)]

        def copy(a, k, block, to, src=None):
            dst = outs[a].at[_slot(block)]
            return pltpu.make_async_remote_copy(
                src_ref=dst if src is None else src, dst_ref=dst, send_sem=send_sems.at[a * 7 + k],
                recv_sem=recv_sems.at[a * 7 + k], device_id=to, device_id_type=MESH)

        mine = [pltpu.make_async_copy(ins[a], outs[a].at[_slot(me)], local_sems.at[a]) for a in range(n)]
        for cp in mine:
            cp.start()
        first = []
        for a in range(n):
            first.append(copy(a, 0, me, sibling, src=ins[a]))
            first += [copy(a, 1 + j, me, (*chip, c), src=ins[a]) for j, chip in enumerate(chips)]
        for cp in first:
            cp.start()
        passed = []
        for a in range(n):
            for j, chip in enumerate(chips):
                copy(a, 1 + j, (*chip, c), me).wait_recv()
                cp = copy(a, 4 + j, (*chip, c), sibling)
                cp.start()
                passed.append(cp)
        for a in range(n):
            copy(a, 0, sibling, me).wait_recv()
            for j, chip in enumerate(chips):
                copy(a, 4 + j, (*chip, 1 - c), me).wait_recv()
        for cp in first + passed:
            cp.wait_send()
        for cp in mine:
            cp.wait()

    return _pcall(
        body, name=name, in_specs=[hbm] * n, out_specs=[hbm] * n,
        out_shape=[jax.ShapeDtypeStruct((N_DEV,) + a.shape, a.dtype) for a in arrs],
        scratch_shapes=[pltpu.SemaphoreType.DMA((7 * n,)), pltpu.SemaphoreType.DMA((7 * n,)),
                        pltpu.SemaphoreType.DMA((n,))],
        compiler_params=pltpu.CompilerParams(has_side_effects=True),
    )(*arrs)


def _t(w):
    return jnp.swapaxes(w, -1, -2)


def _rows_from_blocks(blocks, pad_to=None):
    full = blocks.reshape(-1, blocks.shape[2])
    if pad_to is not None and pad_to > full.shape[0]:
        full = jnp.pad(full, ((0, pad_to - full.shape[0]), (0, 0)))
    return full


def _rows_to_blocks(full, nrows):
    return full[:nrows].reshape(N_DEV, nrows // N_DEV, full.shape[1])


SMALL_ORDER = ("g_pre_ff1", "g_post_ff1", "g_pre_mix", "g_post_mix", "g_out_a", "g_out_b", "g_pre_ff2", "g_post_ff2",
               "b_forget")


def _pack_small(vals):
    rows = []
    for name in SMALL_ORDER:
        v = vals[name].reshape(1, -1)
        if v.shape[1] % LANES:
            v = jnp.pad(v, ((0, 0), (0, LANES - v.shape[1] % LANES)))
        rows.append(v)
    return jnp.concatenate(rows, axis=1)


def _unpack_small(row, sizes):
    out, pos = {}, 0
    for name in SMALL_ORDER:
        n = sizes[name]
        out[name] = row[:, pos:pos + n]
        pos += -(-n // LANES) * LANES
    return out


def _ffn_forward(x, mod, g_pre, g_post, wg, wu, wd, i0, nb, tag, target=None, side=None, side_down=None):
    h = prenorm_fwd(x, g_pre, mod, i0, i0 + 1, nb, f"{tag}_prenorm")
    res, side_out = ffn_up(h, wg, wu, f"{tag}_up", side=side), None
    if side is not None:
        res, side_out = res
    gate, up, act = res
    if callable(wd):
        wd = wd(side_out)
    res, side_down_out = postnorm_fwd(x, [(act, wd)], g_post, mod, i0 + 2, 0.5, nb, f"{tag}_down_postnorm",
                                      target=target, side=side_down), None
    if side_down is not None:
        res, side_down_out = res
    out, y0 = (res[0] if target is None else tuple(res[:2])), res[-1]
    return out, (x, h, gate, up, act, y0), wd, side_out, side_down_out


def _ffn_backward(dxo, saved, mod, g_pre, g_post, wg, wu, wd, i0, nb, tag, side=None, chain=False):
    x, h, gate, up, act, y0 = saved
    dy0, dg_post, dgate_mod = postnorm_bwd(dxo, y0, g_post, mod, i0 + 2, 0.5, nb, f"{tag}_postnorm_bwd")
    dwd = mm_tn(act, dy0, BF16, f"{tag}_dwd", rows=D_FF)
    res, side_out = ffn_down_bwd(dy0, wd, gate, up, f"{tag}_down_bwd", side=side), None
    if side is not None:
        res, side_out = res
    dgate, dup = res
    dh_pairs = [(dgate, wg), (dup, wu)]
    if chain:
        dwg, (dwd,) = mm_tn(dgate, h, BF16, f"{tag}_dwg", rows=D_FF, side=([_rows_to_blocks(dwd, D_FF)[:, None]], False))
        dwu, (dwg,) = mm_tn(dup, h, BF16, f"{tag}_dwu", rows=D_FF, side=([_rows_to_blocks(dwg, D_FF)[:, None]], False))
        (dx, dg_pre, dsc, dsh), (dwu,) = prenorm_bwd(dh_pairs, x, g_pre, mod, i0 + 1, dxo, nb, f"{tag}_dh_prenorm_bwd",
                                                     ts=DH_ROWS, side=([_rows_to_blocks(dwu, D_FF)[:, None]], False))
    else:
        dwg = mm_tn(dgate, h, BF16, f"{tag}_dwg", rows=D_FF)
        dwu = mm_tn(dup, h, BF16, f"{tag}_dwu", rows=D_FF)
        dx, dg_pre, dsc, dsh = prenorm_bwd(dh_pairs, x, g_pre, mod, i0 + 1, dxo, nb, f"{tag}_dh_prenorm_bwd", ts=DH_ROWS)
    return dx, dict(g_pre=dg_pre, g_post=dg_post, wg=dwg, wu=dwu, wd=dwd, mod=(dsh, dsc, dgate_mod)), side_out


def kernel(x, c, positions, w_ada, b_ada, g_pre_ff1, g_post_ff1, w_ff1_gate, w_ff1_up, w_ff1_down, g_pre_mix, g_post_mix, w_in, b_forget, g_out_a, g_out_b, w_out, g_pre_ff2, g_post_ff2, w_ff2_gate, w_ff2_up, w_ff2_down, loss_target, m_w_ada, m_b_ada, m_g_pre_ff1, m_g_post_ff1, m_w_ff1_gate, m_w_ff1_up, m_w_ff1_down, m_g_pre_mix, m_g_post_mix, m_w_in, m_b_forget, m_g_out_a, m_g_out_b, m_w_out, m_g_pre_ff2, m_g_post_ff2, m_w_ff2_gate, m_w_ff2_up, m_w_ff2_down, v_w_ada, v_b_ada, v_g_pre_ff1, v_g_post_ff1, v_w_ff1_gate, v_w_ff1_up, v_w_ff1_down, v_g_pre_mix, v_g_post_mix, v_w_in, v_b_forget, v_g_out_a, v_g_out_b, v_w_out, v_g_pre_ff2, v_g_post_ff2, v_w_ff2_gate, v_w_ff2_up, v_w_ff2_down):
    weights = dict(w_ada=w_ada, b_ada=b_ada, g_pre_ff1=g_pre_ff1, g_post_ff1=g_post_ff1, w_ff1_gate=w_ff1_gate,
                   w_ff1_up=w_ff1_up, w_ff1_down=w_ff1_down, g_pre_mix=g_pre_mix, g_post_mix=g_post_mix, w_in=w_in,
                   b_forget=b_forget, g_out_a=g_out_a, g_out_b=g_out_b, w_out=w_out, g_pre_ff2=g_pre_ff2,
                   g_post_ff2=g_post_ff2, w_ff2_gate=w_ff2_gate, w_ff2_up=w_ff2_up, w_ff2_down=w_ff2_down)
    mom_m = dict(w_ada=m_w_ada, b_ada=m_b_ada, g_pre_ff1=m_g_pre_ff1, g_post_ff1=m_g_post_ff1, w_ff1_gate=m_w_ff1_gate,
                 w_ff1_up=m_w_ff1_up, w_ff1_down=m_w_ff1_down, g_pre_mix=m_g_pre_mix, g_post_mix=m_g_post_mix,
                 w_in=m_w_in, b_forget=m_b_forget, g_out_a=m_g_out_a, g_out_b=m_g_out_b, w_out=m_w_out,
                 g_pre_ff2=m_g_pre_ff2, g_post_ff2=m_g_post_ff2, w_ff2_gate=m_w_ff2_gate, w_ff2_up=m_w_ff2_up,
                 w_ff2_down=m_w_ff2_down)
    mom_v = dict(w_ada=v_w_ada, b_ada=v_b_ada, g_pre_ff1=v_g_pre_ff1, g_post_ff1=v_g_post_ff1, w_ff1_gate=v_w_ff1_gate,
                 w_ff1_up=v_w_ff1_up, w_ff1_down=v_w_ff1_down, g_pre_mix=v_g_pre_mix, g_post_mix=v_g_post_mix,
                 w_in=v_w_in, b_forget=v_b_forget, g_out_a=v_g_out_a, g_out_b=v_g_out_b, w_out=v_w_out,
                 g_pre_ff2=v_g_pre_ff2, g_post_ff2=v_g_post_ff2, w_ff2_gate=v_w_ff2_gate, w_ff2_up=v_w_ff2_up,
                 w_ff2_down=v_w_ff2_down)
    order = list(weights)

    nb, s, d = x.shape
    t = nb * s
    me = _slot(_place())
    nbg = nb * N_DEV
    ada_cols = w_ada.shape[2]

    bf = lambda w: w[0].astype(BF16)
    bft = lambda w: _t(w)[0].astype(BF16)
    c_all, wg1, wu1 = all_gather([c, bft(w_ff1_gate), bft(w_ff1_up)], "gather_ff1")
    c_all = c_all.reshape(nbg, d)
    wg1, wu1 = (_rows_from_blocks(w, D_FF_PAD) for w in (wg1, wu1))

    b_cols = lax.dynamic_slice(b_ada, (0, me * ada_cols), (1, ada_cols))
    mod_cols = ada_fwd(c_all, w_ada[0], b_cols, "ada_fwd")
    (mod_all,) = all_gather([mod_cols], "gather_mod")
    mod = lax.dynamic_slice(mod_all, (0, me * nb, 0), (N_DEV, nb, ada_cols))
    mod = mod.transpose(1, 0, 2).reshape(nb, N_MOD, d)

    xf = x.reshape(t, d)
    target = loss_target.reshape(t, d)

    x1, saved1, wd1, (_, w_out_all), (w_in_all,) = _ffn_forward(
        xf, mod, g_pre_ff1, g_post_ff1, wg1, wu1, lambda got: _rows_from_blocks(got[0], D_FF_PAD), 0, nb, "ff1",
        side=([bf(w_ff1_down), bf(w_out)], True), side_down=([bft(w_in)], True))
    w_in_t = _rows_from_blocks(w_in_all)
    n_qkv = 3 * (WIDTH_A + WIDTH_B)
    w_qkv_t = w_in_t[:n_qkv]
    w_f_t = jnp.pad(w_in_t[n_qkv:], ((0, LANES - N_HEADS_B), (0, 0)))
    w_o = _rows_from_blocks(w_out_all)
    w_o_a, w_o_b = w_o[:WIDTH_A], w_o[WIDTH_A:]

    h2 = prenorm_fwd(x1, g_pre_mix, mod, 3, 4, nb, "mix_prenorm")
    tables = rope_tables(positions)
    proj = mm_rows([(h2, w_qkv_t)], True, BF16, "mix_proj", rope=(tables, 2 * WIDTH_A))
    f_logit = mm_rows([(h2, w_f_t)], True, F32, "mix_forget")
    tab_a = dilated_table(s, ATTN_TQ, ATTN_TK)
    tab_b = causal_table(s, ATTN_TQ, ATTN_TK)
    ft = f_logit[:, :N_HEADS_B].reshape(nb, s, N_HEADS_B).transpose(0, 2, 1)
    bf_col = b_forget.reshape(N_HEADS_B, 1)
    colbias = fox_gate_fwd(ft, bf_col, "fox_gate").reshape(nb, N_HEADS_B, s // ATTN_TK, 1, ATTN_TK)
    pa = WIDTH_A // LANES
    (o_a, lse_a), ff2_all = attn_fwd(
        proj, 0, proj, pa, proj, 2 * pa, tab_a, None, nb, "attn_a",
        side=([bft(w_ff2_gate), bft(w_ff2_up), bf(w_ff2_down)], True))
    wg2, wu2, wd2 = (_rows_from_blocks(w, D_FF_PAD) for w in ff2_all)
    o_b, lse_b = attn_fwd(proj, 3 * pa, proj, 4 * pa, proj, 5 * pa, tab_b, colbias, nb, "attn_b", off_diag_bias=False)
    m_a = prenorm_fwd(o_a, g_out_a, None, None, None, nb, "out_norm_a")
    m_b = prenorm_fwd(o_b, g_out_b, None, None, None, nb, "out_norm_b")
    x2, y0m = postnorm_fwd(x1, [(m_a, w_o_a), (m_b, w_o_b)], g_post_mix, mod, 5, 1.0, nb, "mix_out_postnorm")

    (dx3, loss_part), saved2 = _ffn_forward(x2, mod, g_pre_ff2, g_post_ff2, wg2, wu2, wd2, 6, nb, "ff2", target=target)[:2]
    loss = lax.psum(loss_part[0, 0], ("x", "y", "c"))

    dx2, gr2, _ = _ffn_backward(dx3, saved2, mod, g_pre_ff2, g_post_ff2, wg2, wu2, wd2, 6, nb, "ff2")
    ff2_blocks = [_rows_to_blocks(gr2[k], D_FF)[:, None] for k in ("wg", "wu", "wd")]

    dy0m, dg_post_mix, dgate_mix = postnorm_bwd(dx2, y0m, g_post_mix, mod, 5, 1.0, nb, "mix_postnorm_bwd")
    dw_o = mm_tn_stack([m_a, m_b], dy0m, [WIDTH_A, WIDTH_B], BF16, "mix_dwo")
    do_a, dg_out_a = prenorm_bwd([(dy0m, w_o_a.T)], o_a, g_out_a, None, None, None, nb, "out_norm_a_bwd")
    do_b, dg_out_b = prenorm_bwd([(dy0m, w_o_b.T)], o_b, g_out_b, None, None, None, nb, "out_norm_b_bwd")
    (dq_a, dk_a, dv_a), g_ff2 = attn_bwd(proj, 0, proj, pa, proj, 2 * pa, o_a, lse_a, do_a, tab_a, None, nb,
                                            "attn_a_bwd", side=(ff2_blocks, False), rope_tabs=tables)
    dq_b, dk_b, dv_b, dcb, drow = attn_bwd(proj, 3 * pa, proj, 4 * pa, proj, 5 * pa, o_b, lse_b, do_b, tab_b, colbias, nb,
                                           "attn_b_bwd", off_diag_bias=False)
    dz_t, db_forget = fox_gate_bwd(dcb.reshape(nb, N_HEADS_B, s), drow.reshape(nb, N_HEADS_B, s), ft, bf_col,
                                   "fox_gate_bwd")
    dz = jnp.pad(dz_t.transpose(0, 2, 1).reshape(t, N_HEADS_B), ((0, 0), (0, LANES - N_HEADS_B))).astype(BF16)
    pieces = [dq_a, dk_a, dv_a, dq_b, dk_b, dv_b]
    w_pieces = [w_qkv_t[i * WIDTH_A:(i + 1) * WIDTH_A] for i in range(6)]
    dh2_pairs = list(zip(pieces, w_pieces)) + [(dz, w_f_t)]
    dw_in_t = mm_tn_stack(pieces + [dz], h2, [WIDTH_A] * 6 + [N_HEADS_B], BF16, "mix_dwin")
    dx1, dg_pre_mix, dsc_mix, dsh_mix = prenorm_bwd(dh2_pairs, x1, g_pre_mix, mod, 4, dx2, nb, "mix_dh_prenorm_bwd",
                                                    ts=DH_ROWS)

    g_in = _rows_to_blocks(dw_in_t, dw_in_t.shape[0])[:, None]
    g_out = _rows_to_blocks(dw_o, d)[:, None]
    dx0, gr1, (g_in, g_out) = _ffn_backward(dx1, saved1, mod, g_pre_ff1, g_post_ff1, wg1, wu1, wd1, 0, nb, "ff1",
                                            side=([g_in, g_out], False), chain=True)
    grad_x = dx0.reshape(nb, s, d)

    dmod =jnp.concatenate(list(gr1["mod"]) + [dsh_mix, dsc_mix, dgate_mix] + list(gr2["mod"]), axis=1)
    small = _pack_small(dict(g_pre_ff1=gr1["g_pre"], g_post_ff1=gr1["g_post"], g_pre_mix=dg_pre_mix,
                             g_post_mix=dg_post_mix, g_out_a=dg_out_a, g_out_b=dg_out_b, g_pre_ff2=gr2["g_pre"],
                             g_post_ff2=gr2["g_post"], b_forget=db_forget))
    dmod_all, small_all = all_gather([dmod.reshape(nb, N_MOD * d), small], "gather_small_grads")
    dmod_all = dmod_all.reshape(nbg, N_MOD * d)

    res = {}
    def adamw_t(parts, group, n):
        return tuple(_t(r) for r in adamw(parts, group, _t(weights[n])[0], _t(mom_m[n])[0], _t(mom_v[n])[0], f"adamw_{n}"))

    res["w_ff1_gate"] = adamw_t(gr1["wg"], 0, "w_ff1_gate")
    res["w_ff1_up"] = adamw_t(gr1["wu"], 0, "w_ff1_up")
    res["w_ff2_gate"] = adamw_t(g_ff2[0], 0, "w_ff2_gate")
    res["w_ff2_up"] = adamw_t(g_ff2[1], 0, "w_ff2_up")
    res["w_ff1_down"] = adamw(gr1["wd"], 0, w_ff1_down[0], m_w_ff1_down[0], v_w_ff1_down[0], "adamw_ff1_down")
    res["w_ff2_down"] = adamw(g_ff2[2], 0, w_ff2_down[0], m_w_ff2_down[0], v_w_ff2_down[0], "adamw_ff2_down")
    res["w_in"] = adamw_t(g_in, 0, "w_in")
    res["w_out"] = adamw(g_out, 0, w_out[0], m_w_out[0], v_w_out[0], "adamw_out")
    dmod_cols = lax.dynamic_slice(dmod_all, (0, me * ada_cols), (nbg, ada_cols))
    dw_ada = ada_bwd(c_all, dmod_cols, "ada_bwd")
    res["w_ada"] = adamw(dw_ada[None, None], 0, w_ada[0], m_w_ada[0], v_w_ada[0], "adamw_ada", tr=256)
    res["b_ada"] = adamw(dmod_all[:, None, None], 0, b_ada, m_b_ada, v_b_ada, "adamw_b_ada")
    sizes = {n: weights[n].shape[1] for n in SMALL_ORDER}
    small_res = adamw(small_all[:, None], 0, _pack_small(weights), _pack_small(mom_m), _pack_small(mom_v), "adamw_small")
    small_res = [_unpack_small(r, sizes) for r in small_res]
    for n in SMALL_ORDER:
        res[n] = tuple(r[n] for r in small_res)

    outs = [loss, grad_x]
    for kind in range(4):
        for n in order:
            a = res[n][kind]
            outs.append(a.reshape(weights[n].shape))
    return tuple(outs)
```

```python
import functools

import jax
import jax.numpy as jnp
from jax import lax
from jax.experimental import pallas as pl
from jax.experimental.pallas import tpu as pltpu

F32 = jnp.float32
BF16 = jnp.bfloat16

D_MODEL = 1024
HEAD_DIM = 64
N_HEADS_A = 8
N_HEADS_B = 8
WIDTH_A = N_HEADS_A * HEAD_DIM
WIDTH_B = N_HEADS_B * HEAD_DIM
DILATED_PATTERNS = ((128, 1), (512, 4), (2048, 16))
ROT_DIM = HEAD_DIM // 4
ROPE_THETA = 500000.0
D_FF = 2752
D_FF_PAD = 2816
N_MOD = 9
EPS = 1e-6
ATTN_SCALE = HEAD_DIM ** -0.5
NEG = -1e30
N_DEV = 8
LANES = 128
HEADS_PER_STEP = LANES // HEAD_DIM

ADAM_LR = 0.001
ADAM_B1 = 0.9
ADAM_B2 = 0.999
ADAM_EPS = 1e-08
ADAM_WD = 0.01
ADAM_STEP = 10

VMEM_LIMIT = 56 * 1024 * 1024
MESH = pl.DeviceIdType.MESH

NT_DIMS = (((1,), (1,)), ((), ()))
TN_DIMS = (((0,), (0,)), ((), ()))
NN_DIMS = (((1,), (0,)), ((), ()))


def _place():
    return lax.axis_index("x"), lax.axis_index("y"), lax.axis_index("c")


def _slot(p):
    return 4 * p[0] + 2 * p[1] + p[2]


def _direct_copies(ins, outs, send_sems, recv_sems, local_sems, gather):
    x, y, c = _place()
    me = (x, y, c)
    flip = lambda v, bit: 1 - v if bit else v
    peers = [(flip(x, k & 4), flip(y, k & 2), flip(c, k & 1)) for k in range(1, N_DEV)]
    local, sends, recvs = [], [], []
    for a in range(len(ins)):
        mine = ins[a] if gather else ins[a].at[_slot(me)]
        local.append(pltpu.make_async_copy(mine, outs[a].at[_slot(me)], local_sems.at[a]))
        for k, peer in enumerate(peers):
            sems = dict(send_sem=send_sems.at[a * 7 + k], recv_sem=recv_sems.at[a * 7 + k], device_id=peer,
                        device_id_type=MESH)
            sends.append(pltpu.make_async_remote_copy(
                src_ref=ins[a] if gather else ins[a].at[_slot(peer)], dst_ref=outs[a].at[_slot(me)], **sems))
            recvs.append(pltpu.make_async_remote_copy(src_ref=mine, dst_ref=outs[a].at[_slot(peer)], **sems))
    return local, sends, recvs


def _two_level_copies(ins, outs, send_sems, recv_sems, local_sems):
    x, y, c = _place()
    me, sibling = (x, y, c), (x, y, 1 - c)
    chips = [(1 - x, y), (x, 1 - y), (1 - x, 1 - y)]
    local, first, landed, forwards, late = [], [], [], [], []
    for a in range(len(ins)):
        def copy(k, block, to, src=None, a=a):
            dst = outs[a].at[_slot(block)]
            return pltpu.make_async_remote_copy(
                src_ref=dst if src is None else src, dst_ref=dst, send_sem=send_sems.at[a * 7 + k],
                recv_sem=recv_sems.at[a * 7 + k], device_id=to, device_id_type=MESH)

        local.append(pltpu.make_async_copy(ins[a], outs[a].at[_slot(me)], local_sems.at[a]))
        first.append(copy(0, me, sibling, src=ins[a]))
        late.append(copy(0, sibling, me))
        for j, chip in enumerate(chips):
            first.append(copy(1 + j, me, (*chip, c), src=ins[a]))
            landed.append(copy(1 + j, (*chip, c), me))
            forwards.append(copy(4 + j, (*chip, c), sibling))
            late.append(copy(4 + j, (*chip, 1 - c), me))
    return local, first, landed, forwards, late


def _comm_scratch(n):
    return [pltpu.SemaphoreType.DMA((7 * n,)), pltpu.SemaphoreType.DMA((7 * n,)), pltpu.SemaphoreType.DMA((n,))]


def _pcall(body, side=None, **kw):
    if side is None:
        return pl.pallas_call(body, **kw)
    arrs, gather = side
    n = len(arrs)
    grid = kw["grid"]
    in_specs = list(kw["in_specs"])
    single = not isinstance(kw["out_specs"], (list, tuple))
    out_specs = [kw["out_specs"]] if single else list(kw["out_specs"])
    out_shape = [kw["out_shape"]] if single else list(kw["out_shape"])
    scratch = list(kw.get("scratch_shapes", []))
    n_in, n_out, n_scr = len(in_specs), len(out_specs), len(scratch)
    hbm = pl.BlockSpec(memory_space=pl.ANY)

    def hosted(*refs):
        pos = [0]

        def take(k):
            pos[0] += k
            return refs[pos[0] - k:pos[0]]

        ins, s_ins, outs, s_outs, scr, sems = take(n_in), take(n), take(n_out), take(n), take(n_scr), take(3)
        ids = [pl.program_id(i) for i in range(len(grid))]
        first = functools.reduce(jnp.logical_and, [i == 0 for i in ids])
        last = functools.reduce(jnp.logical_and, [i == g - 1 for i, g in zip(ids, grid)])
        if gather == "two_level":
            axis = max(range(len(grid)), key=lambda i: grid[i])
            assert grid[axis] >= 2
            middle = functools.reduce(jnp.logical_and, [i == (grid[axis] // 2 if k == axis else 0)
                                                        for k, i in enumerate(ids)])

            @pl.when(first)
            def _():
                local, sends, _, _, _ = _two_level_copies(s_ins, s_outs, *sems)
                for cp in local + sends:
                    cp.start()

            @pl.when(middle)
            def _():
                _, _, landed, forwards, _ = _two_level_copies(s_ins, s_outs, *sems)
                for cp_in, cp_out in zip(landed, forwards):
                    cp_in.wait_recv()
                    cp_out.start()

            body(*ins, *outs, *scr)

            @pl.when(last)
            def _():
                local, sends, _, forwards, late = _two_level_copies(s_ins, s_outs, *sems)
                for cp in late:
                    cp.wait_recv()
                for cp in sends + forwards:
                    cp.wait_send()
                for cp in local:
                    cp.wait()
            return

        @pl.when(first)
        def _():
            local, sends, _ = _direct_copies(s_ins, s_outs, *sems, gather)
            for cp in local + sends:
                cp.start()

        body(*ins, *outs, *scr)

        @pl.when(last)
        def _():
            local, sends, recvs = _direct_copies(s_ins, s_outs, *sems, gather)
            for cp in recvs:
                cp.wait_recv()
            for cp in sends:
                cp.wait_send()
            for cp in local:
                cp.wait()

    kw.update(in_specs=in_specs + [hbm] * n, out_specs=out_specs + [hbm] * n,
              out_shape=out_shape + [jax.ShapeDtypeStruct(((N_DEV,) + a.shape) if gather else a.shape, a.dtype)
                                     for a in arrs],
              scratch_shapes=scratch + _comm_scratch(n))
    call = pl.pallas_call(hosted, **kw)

    def run(*args):
        res = call(*args, *arrs)
        main = res[0] if single else list(res[:n_out])
        return main, list(res[n_out:])

    return run


def _params(sem=None, **kw):
    if sem is not None:
        kw["dimension_semantics"] = sem
    return pltpu.CompilerParams(vmem_limit_bytes=VMEM_LIMIT, **kw)


def _rotate(xv, c, sp, sm, transpose):
    width = xv.shape[1]
    half = ROT_DIM // 2
    if transpose:
        return xv * c + pltpu.roll(xv * sp, width - half, 1) + pltpu.roll(xv * sm, half, 1)
    return xv * c + pltpu.roll(xv, half, 1) * sp + pltpu.roll(xv, width - half, 1) * sm


def mm_rows(pairs, trans_b, out_dtype, name, tm=512, side=None, rope=None):
    n = len(pairs)
    m = pairs[0][0].shape[0]
    n_out = pairs[0][1].shape[0 if trans_b else 1]
    dims = NT_DIMS if trans_b else NN_DIMS

    def body(*refs):
        o_ref = refs[-1]
        acc = None
        for a_ref, b_ref in zip(refs[:n], refs[n:2 * n]):
            d = lax.dot_general(a_ref[...], b_ref[...], dims, preferred_element_type=F32)
            acc = d if acc is None else acc + d
        if rope is None:
            o_ref[...] = acc.astype(o_ref.dtype)
        else:
            width = rope[1]
            c, sp, sm = (jnp.concatenate([r[...]] * (width // LANES), axis=1) for r in refs[2 * n:2 * n + 3])
            o_ref[:, :width] = _rotate(acc[:, :width], c, sp, sm, False).astype(o_ref.dtype)
            o_ref[:, width:] = acc[:, width:].astype(o_ref.dtype)

    in_specs = [pl.BlockSpec((tm, a.shape[1]), lambda i: (i, 0)) for a, _ in pairs]
    in_specs += [pl.BlockSpec(b.shape, lambda i: (0, 0)) for _, b in pairs]
    args = [a for a, _ in pairs] + [b for _, b in pairs]
    if rope is not None:
        in_specs += [pl.BlockSpec((tm, LANES), lambda i: (i, 0))] * 3
        args += list(rope[0])
    return _pcall(
        body, side=side, name=name, grid=(m // tm,), in_specs=in_specs,
        out_specs=pl.BlockSpec((tm, n_out), lambda i: (i, 0)),
        out_shape=jax.ShapeDtypeStruct((m, n_out), out_dtype),
        compiler_params=_params(("arbitrary",)),
    )(*args)


DH_ROWS = 256
TN_TOKENS = 2048
TN_OUT_ELEMS = 2 * 1024 * 1024


def mm_tn(a, b, out_dtype, name, side=None, rows=None):
    t, ka = a.shape
    n_out = b.shape[1]
    tk = min(TN_TOKENS, t)
    tka = ka // 2 if ka * n_out > TN_OUT_ELEMS else ka
    tn = n_out
    steps = t // tk

    def body(a_ref, b_ref, o_ref, acc_ref):
        k = pl.program_id(2)
        d = lax.dot_general(a_ref[...], b_ref[...], TN_DIMS, preferred_element_type=F32)

        @pl.when(k == 0)
        def _():
            acc_ref[...] = d

        @pl.when(k > 0)
        def _():
            acc_ref[...] += d

        @pl.when(k == steps - 1)
        def _():
            o_ref[...] = acc_ref[...].astype(o_ref.dtype)

    return _pcall(
        body, side=side, name=name, grid=(ka // tka, n_out // tn, steps),
        in_specs=[pl.BlockSpec((tk, tka), lambda i, j, k: (k, i)), pl.BlockSpec((tk, tn), lambda i, j, k: (k, j))],
        out_specs=pl.BlockSpec((tka, tn), lambda i, j, k: (i, j)),
        out_shape=jax.ShapeDtypeStruct((ka if rows is None else rows, n_out), out_dtype),
        scratch_shapes=[pltpu.VMEM((tka, tn), F32)],
        compiler_params=_params(("arbitrary", "arbitrary", "arbitrary")),
    )(a, b)


def mm_tn_stack(a_list, b, rows, out_dtype, name, tk=1024):
    t, n_out = b.shape
    tk = min(tk, t)
    steps = t // tk
    n = len(a_list)
    offs = [sum(rows[:i]) for i in range(n)]

    def body(*refs):
        a_refs, b_ref, o_ref, acc_refs = refs[:n], refs[n], refs[n + 1], refs[n + 2:]
        k = pl.program_id(0)
        bv = b_ref[...]
        for a_ref, acc_ref in zip(a_refs, acc_refs):
            d = lax.dot_general(a_ref[...], bv, TN_DIMS, preferred_element_type=F32)

            @pl.when(k == 0)
            def _(acc_ref=acc_ref, d=d):
                acc_ref[...] = d

            @pl.when(k > 0)
            def _(acc_ref=acc_ref, d=d):
                acc_ref[...] += d

        @pl.when(k == steps - 1)
        def _():
            for acc_ref, off, r in zip(acc_refs, offs, rows):
                o_ref[off:off + r, :] = acc_ref[0:r, :].astype(o_ref.dtype)

    return _pcall(
        body, name=name, grid=(steps,),
        in_specs=[pl.BlockSpec((tk, a.shape[1]), lambda k: (k, 0)) for a in a_list]
        + [pl.BlockSpec((tk, n_out), lambda k: (k, 0))],
        out_specs=pl.BlockSpec((sum(rows), n_out), lambda k: (0, 0)),
        out_shape=jax.ShapeDtypeStruct((sum(rows), n_out), out_dtype),
        scratch_shapes=[pltpu.VMEM((a.shape[1], n_out), F32) for a in a_list],
        compiler_params=_params(("arbitrary",)),
    )(*a_list, b)


def _col_chunks(width, chunk=512):
    return [slice(c, min(c + chunk, width)) for c in range(0, width, chunk)]


def _sigmoid(x):
    return 1.0 / (1.0 + jnp.exp(-x))


def ffn_up(h, wgt, wut, name, tm=256, tn=D_FF_PAD, side=None):
    t, d = h.shape
    fp = wgt.shape[0]

    def body(h_ref, wg_ref, wu_ref, g_ref, u_ref, a_ref):
        hv = h_ref[...]

        def finish(cols, g, u):
            g_ref[:, cols] = g.astype(BF16)
            u_ref[:, cols] = u.astype(BF16)
            a_ref[:, cols] = (g * _sigmoid(g) * u).astype(BF16)

        pending = None
        for cols in _col_chunks(tn):
            g = lax.dot_general(hv, wg_ref[cols, :], NT_DIMS, preferred_element_type=F32)
            u = lax.dot_general(hv, wu_ref[cols, :], NT_DIMS, preferred_element_type=F32)
            if pending is not None:
                finish(*pending)
            pending = (cols, g, u)
        finish(*pending)

    w_spec = pl.BlockSpec((tn, d), lambda j, i: (j, 0))
    o_spec = pl.BlockSpec((tm, tn), lambda j, i: (i, j))
    o_shape = jax.ShapeDtypeStruct((t, fp), BF16)
    return _pcall(
        body, side=side, name=name, grid=(fp // tn, t // tm),
        in_specs=[pl.BlockSpec((tm, d), lambda j, i: (i, 0)), w_spec, w_spec],
        out_specs=[o_spec, o_spec, o_spec], out_shape=[o_shape, o_shape, o_shape],
        compiler_params=_params(("arbitrary", "arbitrary")),
    )(h, wgt, wut)


def ffn_down_bwd(dy0, wd, gate, up, name, tm=256, tn=D_FF_PAD, side=None):
    t, d = dy0.shape
    fp = wd.shape[0]

    def body(dy_ref, wd_ref, g_ref, u_ref, dg_ref, du_ref):
        dyv = dy_ref[...]

        def finish(cols, dact):
            g = g_ref[:, cols].astype(F32)
            u = u_ref[:, cols].astype(F32)
            sg = _sigmoid(g)
            silu = g * sg
            du_ref[:, cols] = (dact * silu).astype(BF16)
            dg_ref[:, cols] = ((dact * u) * (sg + silu * (1.0 - sg))).astype(BF16)

        pending = None
        for cols in _col_chunks(tn):
            dact = lax.dot_general(dyv, wd_ref[cols, :], NT_DIMS, preferred_element_type=F32)
            if pending is not None:
                finish(*pending)
            pending = (cols, dact)
        finish(*pending)

    t_spec = pl.BlockSpec((tm, tn), lambda j, i: (i, j))
    o_shape = jax.ShapeDtypeStruct((t, fp), BF16)
    return _pcall(
        body, side=side, name=name, grid=(fp // tn, t // tm),
        in_specs=[pl.BlockSpec((tm, d), lambda j, i: (i, 0)), pl.BlockSpec((tn, d), lambda j, i: (j, 0)), t_spec, t_spec],
        out_specs=[t_spec, t_spec], out_shape=[o_shape, o_shape],
        compiler_params=_params(("arbitrary", "arbitrary")),
    )(dy0, wd, gate, up)


def _row_specs(dx, ts, ns):
    return pl.BlockSpec((ts, dx), lambda b, s: (b * ns + s, 0))


def _mod_spec():
    return pl.BlockSpec((1, N_MOD, D_MODEL), lambda b, s: (b, 0, 0))


def _vec_spec(dx):
    return pl.BlockSpec((1, dx), lambda b, s: (0, 0))


def prenorm_fwd(x, g, mod, i_shift, i_scale, nb, name, ts=1024):
    t, dx = x.shape
    ts = min(ts, t // nb)
    ns = t // nb // ts

    def body(*refs):
        if mod is None:
            x_ref, g_ref, h_ref = refs
        else:
            x_ref, g_ref, mod_ref, h_ref = refs
        xv = x_ref[...]
        r = lax.rsqrt(jnp.mean(xv * xv, axis=-1, keepdims=True) + EPS)
        h = xv * r * g_ref[...]
        if mod is not None:
            h = h * (1.0 + mod_ref[0, i_scale:i_scale + 1, :]) + mod_ref[0, i_shift:i_shift + 1, :]
        h_ref[...] = h.astype(BF16)

    in_specs = [_row_specs(dx, ts, ns), _vec_spec(dx)]
    args = [x, g]
    if mod is not None:
        in_specs.append(_mod_spec())
        args.append(mod)
    return _pcall(
        body, name=name, grid=(nb, ns), in_specs=in_specs, out_specs=_row_specs(dx, ts, ns),
        out_shape=jax.ShapeDtypeStruct((t, dx), BF16), compiler_params=_params(("arbitrary", "arbitrary")),
    )(*args)


def prenorm_bwd(dh, x, g, mod, i_scale, dres, nb, name, ts=512, side=None):
    t, dx = x.shape
    ts = min(ts, t // nb)
    ns = t // nb // ts
    has_mod = mod is not None
    has_res = dres is not None
    pairs = dh if isinstance(dh, list) else None
    n_mm = 0 if pairs is None else len(pairs)

    def body(*refs):
        refs = list(refs)
        if pairs is None:
            dhv = refs[0][...].astype(F32)
            refs = refs[1:]
        else:
            dhv = None
            for a_ref, b_ref in zip(refs[:n_mm], refs[n_mm:2 * n_mm]):
                d = jnp.dot(a_ref[...], b_ref[...], preferred_element_type=F32)
                dhv = d if dhv is None else dhv + d
            refs = refs[2 * n_mm:]
        x_ref, g_ref = refs[:2]
        pos = 2
        mod_ref = dres_ref = None
        if has_mod:
            mod_ref = refs[pos]
            pos += 1
        if has_res:
            dres_ref = refs[pos]
            pos += 1
        dx_ref, dg_ref = refs[pos], refs[pos + 1]
        b, s = pl.program_id(0), pl.program_id(1)
        xv = x_ref[...]
        gv = g_ref[...]
        r = lax.rsqrt(jnp.mean(xv * xv, axis=-1, keepdims=True) + EPS)
        xhat = xv * r
        dn = dhv
        if has_mod:
            dsc_ref, dsh_ref = refs[pos + 2], refs[pos + 3]
            dn = dhv * (1.0 + mod_ref[0, i_scale:i_scale + 1, :])
            dsc = jnp.sum(dhv * xhat * gv, axis=0, keepdims=True)[None]
            dsh = jnp.sum(dhv, axis=0, keepdims=True)[None]

            @pl.when(s == 0)
            def _():
                dsc_ref[...] = dsc
                dsh_ref[...] = dsh

            @pl.when(s > 0)
            def _():
                dsc_ref[...] += dsc
                dsh_ref[...] += dsh

        dg = jnp.sum(dn * xhat, axis=0, keepdims=True)
        first = jnp.logical_and(b == 0, s == 0)

        @pl.when(first)
        def _():
            dg_ref[...] = dg

        @pl.when(jnp.logical_not(first))
        def _():
            dg_ref[...] += dg

        dxhat = dn * gv
        dxv = r * (dxhat - xhat * jnp.mean(dxhat * xhat, axis=-1, keepdims=True))
        if has_res:
            dxv = dxv + dres_ref[...]
        dx_ref[...] = dxv

    row = _row_specs(dx, ts, ns)
    if pairs is None:
        in_specs, args = [row], [dh]
    else:
        in_specs = [_row_specs(a.shape[1], ts, ns) for a, _ in pairs]
        in_specs += [pl.BlockSpec(b.shape, lambda b_, s_: (0, 0)) for _, b in pairs]
        args = [a for a, _ in pairs] + [b for _, b in pairs]
    in_specs += [row, _vec_spec(dx)]
    args += [x, g]
    if has_mod:
        in_specs.append(_mod_spec())
        args.append(mod)
    if has_res:
        in_specs.append(row)
        args.append(dres)
    out_specs = [row, _vec_spec(dx)]
    out_shape = [jax.ShapeDtypeStruct((t, dx), F32), jax.ShapeDtypeStruct((1, dx), F32)]
    if has_mod:
        bspec = pl.BlockSpec((1, 1, dx), lambda b, s: (b, 0, 0))
        out_specs += [bspec, bspec]
        out_shape += [jax.ShapeDtypeStruct((nb, 1, dx), F32)] * 2
    return _pcall(
        body, side=side, name=name, grid=(nb, ns), in_specs=in_specs, out_specs=out_specs, out_shape=out_shape,
        compiler_params=_params(("arbitrary", "arbitrary")),
    )(*args)


def postnorm_fwd(x, pairs, g, mod, i_gate, coef, nb, name, target=None, ts=512, side=None):
    t, dx = x.shape
    with_loss = target is not None
    ts = min(ts, t // nb)
    ns = t // nb // ts
    n_mm = len(pairs)

    def body(*refs):
        yv = None
        for a_ref, b_ref in zip(refs[:n_mm], refs[n_mm:2 * n_mm]):
            d = jnp.dot(a_ref[...], b_ref[...], preferred_element_type=F32)
            yv = d if yv is None else yv + d
        refs = refs[2 * n_mm:]
        x_ref, g_ref, mod_ref = refs[:3]
        refs[-1][...] = yv.astype(BF16)
        r = lax.rsqrt(jnp.mean(yv * yv, axis=-1, keepdims=True) + EPS)
        out = x_ref[...] + (coef * mod_ref[0, i_gate:i_gate + 1, :]) * (yv * r * g_ref[...])
        if not with_loss:
            refs[3][...] = out
            return
        t_ref, dx_ref, loss_ref = refs[3:6]
        b, s = pl.program_id(0), pl.program_id(1)
        err = out - t_ref[...]
        dx_ref[...] = err * (1.0 / dx)
        part = (0.5 / dx) * jnp.sum(jnp.sum(err * err, axis=1, keepdims=True), axis=0, keepdims=True)
        first = jnp.logical_and(b == 0, s == 0)

        @pl.when(first)
        def _():
            loss_ref[...] = part

        @pl.when(jnp.logical_not(first))
        def _():
            loss_ref[...] += part

    row = _row_specs(dx, ts, ns)
    in_specs = [_row_specs(a.shape[1], ts, ns) for a, _ in pairs]
    in_specs += [pl.BlockSpec(b.shape, lambda b_, s_: (0, 0)) for _, b in pairs]
    in_specs += [row, _vec_spec(dx), _mod_spec()]
    args = [a for a, _ in pairs] + [b for _, b in pairs] + [x, g, mod]
    row_shape = jax.ShapeDtypeStruct((t, dx), F32)
    y0_shape = jax.ShapeDtypeStruct((t, dx), BF16)
    out_specs, out_shape = [row, row], [row_shape, y0_shape]
    if with_loss:
        in_specs.append(row)
        args.append(target)
        out_specs = [row, pl.BlockSpec((1, 1), lambda b, s: (0, 0)), row]
        out_shape = [row_shape, jax.ShapeDtypeStruct((1, 1), F32), y0_shape]
    return _pcall(
        body, side=side, name=name, grid=(nb, ns), in_specs=in_specs, out_specs=out_specs, out_shape=out_shape,
        compiler_params=_params(("arbitrary", "arbitrary")),
    )(*args)


def postnorm_bwd(dxo, y0, g, mod, i_gate, coef, nb, name, ts=1024):
    t, dx = y0.shape
    ts = min(ts, t // nb)
    ns = t // nb // ts

    def body(d_ref, y_ref, g_ref, mod_ref, dy_ref, dg_ref, dgate_ref):
        b, s = pl.program_id(0), pl.program_id(1)
        yv = y_ref[...].astype(F32)
        dv = d_ref[...]
        gv = g_ref[...]
        r = lax.rsqrt(jnp.mean(yv * yv, axis=-1, keepdims=True) + EPS)
        yhat = yv * r
        dgate = jnp.sum(dv * (coef * (yhat * gv)), axis=0, keepdims=True)[None]
        dyn = dv * (coef * mod_ref[0, i_gate:i_gate + 1, :])
        dg = jnp.sum(dyn * yhat, axis=0, keepdims=True)
        dyhat = dyn * gv
        dy_ref[...] = (r * (dyhat - yhat * jnp.mean(dyhat * yhat, axis=-1, keepdims=True))).astype(BF16)

        @pl.when(s == 0)
        def _():
            dgate_ref[...] = dgate

        @pl.when(s > 0)
        def _():
            dgate_ref[...] += dgate

        first = jnp.logical_and(b == 0, s == 0)

        @pl.when(first)
        def _():
            dg_ref[...] = dg

        @pl.when(jnp.logical_not(first))
        def _():
            dg_ref[...] += dg

    row = _row_specs(dx, ts, ns)
    return _pcall(
        body, name=name, grid=(nb, ns), in_specs=[row, row, _vec_spec(dx), _mod_spec()],
        out_specs=[row, _vec_spec(dx), pl.BlockSpec((1, 1, dx), lambda b, s: (b, 0, 0))],
        out_shape=[jax.ShapeDtypeStruct((t, dx), BF16), jax.ShapeDtypeStruct((1, dx), F32),
                   jax.ShapeDtypeStruct((nb, 1, dx), F32)],
        compiler_params=_params(("arbitrary", "arbitrary")),
    )(dxo, y0, g, mod)


def rope_tables(positions):
    inv_freq = ROPE_THETA ** (-jnp.arange(0, ROT_DIM, 2, dtype=F32) / ROT_DIM)
    ang = positions.astype(F32).reshape(-1, 1) * inv_freq
    cos, sin = jnp.cos(ang), jnp.sin(ang)
    half = ROT_DIM // 2
    z = lambda n: jnp.zeros((ang.shape[0], n), F32)
    c = jnp.concatenate([cos, cos, jnp.ones((ang.shape[0], HEAD_DIM - ROT_DIM), F32)], axis=1)
    sp = jnp.concatenate([z(half), sin, z(HEAD_DIM - ROT_DIM)], axis=1)
    sm = jnp.concatenate([-sin, z(HEAD_DIM - half)], axis=1)
    return tuple(jnp.tile(a, (1, HEADS_PER_STEP)) for a in (c, sp, sm))


def _scan_lanes(x, reverse):
    n = x.shape[-1]
    lane = lax.broadcasted_iota(jnp.int32, x.shape, x.ndim - 1)
    k = 1
    while k < n:
        if reverse:
            x = x + jnp.where(lane < n - k, pltpu.roll(x, n - k, x.ndim - 1), 0.0)
        else:
            x = x + jnp.where(lane >= k, pltpu.roll(x, k, x.ndim - 1), 0.0)
        k *= 2
    return x


def _log_sigmoid(z):
    return jnp.minimum(z, 0.0) - jnp.log(1.0 + jnp.exp(-jnp.abs(z)))


def fox_gate_fwd(ft, b_forget, name):
    nb, nh, s = ft.shape

    def body(f_ref, b_ref, o_ref):
        z = f_ref[0] + b_ref[...]
        o_ref[0] = -_scan_lanes(_log_sigmoid(z), False)

    spec = pl.BlockSpec((1, nh, s), lambda b: (b, 0, 0))
    return _pcall(
        body, name=name, grid=(nb,), in_specs=[spec, pl.BlockSpec((nh, 1), lambda b: (0, 0))], out_specs=spec,
        out_shape=jax.ShapeDtypeStruct((nb, nh, s), F32), compiler_params=_params(("arbitrary",)),
    )(ft, b_forget)


def fox_gate_bwd(dcb, drow, ft, b_forget, name):
    nb, nh, s = ft.shape

    def body(d_ref, r_ref, f_ref, b_ref, dz_ref, db_ref):
        b = pl.program_id(0)
        z = f_ref[0] + b_ref[...]
        dlf = _scan_lanes(r_ref[0] - d_ref[0], True)
        dz = dlf * _sigmoid(-z)
        dz_ref[0] = dz
        db = jnp.sum(dz, axis=1, keepdims=True)

        @pl.when(b == 0)
        def _():
            db_ref[...] = db

        @pl.when(b > 0)
        def _():
            db_ref[...] += db

    spec = pl.BlockSpec((1, nh, s), lambda b: (b, 0, 0))
    vec = pl.BlockSpec((nh, 1), lambda b: (0, 0))
    return _pcall(
        body, name=name, grid=(nb,), in_specs=[spec, spec, spec, vec], out_specs=[spec, vec],
        out_shape=[jax.ShapeDtypeStruct((nb, nh, s), F32), jax.ShapeDtypeStruct((nh, 1), F32)],
        compiler_params=_params(("arbitrary",)),
    )(dcb, drow, ft, b_forget)


ATTN_TQ = 512
ATTN_TK = 512
ONES_ROWS = 16


def _rows_to_cols(rows):
    tile = jnp.concatenate([jnp.broadcast_to(rw, (HEAD_DIM, rw.shape[1])) for rw in rows], axis=0)
    return tile.T


def _block_delta(s, tq, tk):
    off = jnp.arange(s // tk) - (tq // tk - 1)
    return off[:, None, None] * tk + jnp.arange(tq)[None, None, :] - jnp.arange(tk)[None, :, None]


def dilated_table(s, tq, tk):
    delta = _block_delta(s, tq, tk)
    count = jnp.zeros(delta.shape, F32)
    for window, dil in DILATED_PATTERNS:
        count = count + ((delta >= 0) & (delta <= window) & (delta % dil == 0)).astype(F32)
    return jnp.where(count > 0, jnp.log(jnp.maximum(count, 1.0)), NEG)


def causal_table(s, tq, tk):
    return jnp.where(_block_delta(s, tq, tk) >= 0, 0.0, NEG).astype(F32)


def attn_fwd(q_arr, q_off, k_arr, k_off, v_arr, v_off, table, colbias, nb, name, side=None, off_diag_bias=True):
    t = q_arr.shape[0]
    s = t // nb
    tk, tq = table.shape[1:]
    assert tq == tk, "the diagonal handling below is written for square tiles"
    nq, nk = s // tq, s // tk
    npairs = WIDTH_A // LANES
    use_cb = colbias is not None

    def body(*refs):
        refs = list(refs)
        q_ref, k_ref, v_ref, tab_ref = refs[:4]
        cb_ref = refs[4] if use_cb else None
        tail = refs[-(HEADS_PER_STEP + int(use_cb)):]
        acc_s = tail[:HEADS_PER_STEP]
        cbc_s = tail[-1] if use_cb else None
        o_ref, lse_ref, vt_s = refs[-3 - len(tail):-len(tail)]
        qi = pl.program_id(2)

        heads = [slice(h * HEAD_DIM, (h + 1) * HEAD_DIM) for h in range(HEADS_PER_STEP)]

        @pl.when(qi == 0)
        def _():
            for cblk in range(nk):
                vt = v_ref[cblk * tk:(cblk + 1) * tk, :].astype(F32).T.astype(BF16)
                for h, hs in enumerate(heads):
                    vt_s[cblk, h, 0:HEAD_DIM, :] = vt[hs, :]
                    vt_s[cblk, h, HEAD_DIM:, :] = jnp.ones((ONES_ROWS, tk), BF16)
                if use_cb:
                    cbc_s[cblk] = _rows_to_cols([cb_ref[0, h, cblk] for h in range(HEADS_PER_STEP)])

        qt_all = (q_ref[...].astype(F32) * ATTN_SCALE).T.astype(BF16)
        qts = [qt_all[hs, :] for hs in heads]
        for a in acc_s:
            a[...] = jnp.zeros_like(a)

        def tile(kb, tab, k0, klen, q0, carry):
            ks = pl.multiple_of(kb * tk + k0, klen)
            sts, out = [], []
            for h, hs in enumerate(heads):
                st = jnp.dot(k_ref[pl.ds(ks, klen), hs], qts[h][:, q0:], preferred_element_type=F32)
                if tab is not None:
                    st = st + tab
                if use_cb:
                    st = st + cbc_s[kb, k0:k0 + klen, h * HEAD_DIM:h * HEAD_DIM + 1]
                sts.append(st)
            m_old = [carry[h][:, q0:] for h in range(HEADS_PER_STEP)]
            m_new = [jnp.maximum(m_old[h], jnp.max(sts[h], axis=0, keepdims=True)) for h in range(HEADS_PER_STEP)]
            for h in range(HEADS_PER_STEP):
                pt = jnp.exp(sts[h] - m_new[h]).astype(BF16)
                acc_s[h][:, q0:] = (jnp.exp(m_old[h] - m_new[h]) * acc_s[h][:, q0:]
                                    + jnp.dot(vt_s[kb, h, :, k0:k0 + klen], pt, preferred_element_type=F32))
                out.append(m_new[h] if q0 == 0 else jnp.concatenate([carry[h][:, :q0], m_new[h]], axis=1))
            return tuple(out)

        fin = lax.fori_loop(0, qi, lambda kb, c: tile(kb, tab_ref[qi - kb] if off_diag_bias else None, 0, tk, 0, c),
                            tuple(jnp.full((1, tq), NEG, F32) for _ in heads))
        half = tk // 2
        fin = tile(qi, tab_ref[0, 0:half, :], 0, half, 0, fin)
        fin = tile(qi, tab_ref[0, half:, half:], half, half, half, fin)
        outs = []
        for h in range(HEADS_PER_STEP):
            l = acc_s[h][HEAD_DIM:HEAD_DIM + 1, :]
            outs.append(acc_s[h][0:HEAD_DIM, :] / l)
            lse_ref[0, h, 0] = fin[h] + jnp.log(l)
        o_ref[...] = jnp.concatenate(outs, axis=0).T

    def seq_spec(off):
        return pl.BlockSpec((s, LANES), lambda b, j, i: (b, off + j))

    in_specs = [pl.BlockSpec((tq, LANES), lambda b, j, i: (b * nq + i, q_off + j)), seq_spec(k_off), seq_spec(v_off),
                pl.BlockSpec(table.shape, lambda b, j, i: (0, 0, 0))]
    args = [q_arr, k_arr, v_arr, table]
    if use_cb:
        in_specs.append(pl.BlockSpec((1, HEADS_PER_STEP, nk, 1, tk), lambda b, j, i: (b, j, 0, 0, 0)))
        args.append(colbias)
    n_heads = npairs * HEADS_PER_STEP
    return _pcall(
        body, side=side, name=name, grid=(nb, npairs, nq), in_specs=in_specs,
        out_specs=[pl.BlockSpec((tq, LANES), lambda b, j, i: (b * nq + i, j)),
                   pl.BlockSpec((1, HEADS_PER_STEP, 1, 1, tq), lambda b, j, i: (b, j, i, 0, 0))],
        out_shape=[jax.ShapeDtypeStruct((t, npairs * LANES), F32), jax.ShapeDtypeStruct((nb, n_heads, nq, 1, tq), F32)],
        scratch_shapes=[pltpu.VMEM((nk, HEADS_PER_STEP, HEAD_DIM + ONES_ROWS, tk), BF16)]
        + [pltpu.VMEM((HEAD_DIM + ONES_ROWS, tq), F32)] * HEADS_PER_STEP
        + ([pltpu.VMEM((nk, tk, LANES), F32)] if use_cb else []),
        compiler_params=_params(("arbitrary", "arbitrary", "arbitrary")),
    )(*args)


def attn_bwd(q_arr, q_off, k_arr, k_off, v_arr, v_off, o_arr, lse_arr, do_arr, table, colbias, nb, name, side=None,
             rope_tabs=None, off_diag_bias=True):
    t = q_arr.shape[0]
    s = t // nb
    tk, tq = table.shape[1:]
    assert tq == tk, "the diagonal handling below is written for square tiles"
    nq, nk = s // tq, s // tk
    npairs = WIDTH_A // LANES
    use_cb = colbias is not None

    def body(*refs):
        refs = list(refs)
        q_ref, k_ref, v_ref, o_ref, lse_ref, do_ref, tab_ref = refs[:7]
        pos = 7
        cb_ref = None
        if use_cb:
            cb_ref = refs[pos]
            pos += 1
        rope_refs = None
        if rope_tabs is not None:
            rope_refs = refs[pos:pos + 3]
            pos += 3
        dq_ref, dk_ref, dv_ref = refs[pos:pos + 3]
        pos += 3
        dcb_ref = drow_ref = None
        if use_cb:
            dcb_ref, drow_ref = refs[pos:pos + 2]
            pos += 2
        kt_s, dkt_s, dvt_s = refs[pos:pos + 3]
        dqt_s = refs[pos + 3:pos + 3 + HEADS_PER_STEP]
        dcb_s, cbc_s = refs[pos + 3 + HEADS_PER_STEP:pos + 5 + HEADS_PER_STEP] if use_cb else (None, None)

        heads = [slice(h * HEAD_DIM, (h + 1) * HEAD_DIM) for h in range(HEADS_PER_STEP)]
        for cblk in range(nk):
            kt_s[cblk] = k_ref[cblk * tk:(cblk + 1) * tk, :].astype(F32).T.astype(BF16)
        dkt_s[...] = jnp.zeros_like(dkt_s)
        dvt_s[...] = jnp.zeros_like(dvt_s)
        if use_cb:
            dcb_s[...] = jnp.zeros_like(dcb_s)
            for cblk in range(nk):
                cbc_s[cblk] = _rows_to_cols([cb_ref[0, h, cblk] for h in range(HEADS_PER_STEP)])
        ones = jnp.ones((8, HEAD_DIM), BF16)

        def q_loop(qi, carry):
            qs = pl.multiple_of(qi * tq, tq)
            q_all = (q_ref[pl.ds(qs, tq), :].astype(F32) * ATTN_SCALE)
            do_all = do_ref[pl.ds(qs, tq), :]
            qt_all = q_all.T.astype(BF16)
            dot_all = do_all.T.astype(BF16)
            qt, dot, lse, dsum = [], [], [], []
            for h, hs in enumerate(heads):
                qt.append(qt_all[hs, :])
                dot.append(dot_all[hs, :])
                lse.append(lse_ref[0, h, qi])
                prod = do_all[:, hs] * o_ref[pl.ds(qs, tq), hs]
                hi = prod.astype(BF16)
                lo = (prod - hi.astype(F32)).astype(BF16)
                dsum.append((lax.dot_general(ones, hi, NT_DIMS, preferred_element_type=F32)
                             + lax.dot_general(ones, lo, NT_DIMS, preferred_element_type=F32))[0:1, :])
            for a in dqt_s:
                a[...] = jnp.zeros_like(a)

            def tile(kb, tab, k0, klen, q0, drow):
                ks = pl.multiple_of(kb * tk + k0, klen)
                keys = slice(k0, k0 + klen)
                sts, dpts, out = [], [], []
                for h, hs in enumerate(heads):
                    st = jnp.dot(k_ref[pl.ds(ks, klen), hs], qt[h][:, q0:], preferred_element_type=F32)
                    if tab is not None:
                        st = st + tab
                    if use_cb:
                        st = st + cbc_s[kb, keys, h * HEAD_DIM:h * HEAD_DIM + 1]
                    sts.append(st)
                    dpts.append(jnp.dot(v_ref[pl.ds(ks, klen), hs], dot[h][:, q0:], preferred_element_type=F32))
                for h, hs in enumerate(heads):
                    pt = jnp.exp(sts[h] - lse[h][:, q0:])
                    dst = pt * (dpts[h] - dsum[h][:, q0:])
                    dst_b = dst.astype(BF16)
                    dvt_s[h, kb, :, keys] += lax.dot_general(dot[h][:, q0:], pt.astype(BF16), NT_DIMS,
                                                             preferred_element_type=F32)
                    dkt_s[h, kb, :, keys] += lax.dot_general(qt[h][:, q0:], dst_b, NT_DIMS, preferred_element_type=F32)
                    dqt_s[h][:, q0:] += jnp.dot(kt_s[kb, hs, keys], dst_b, preferred_element_type=F32)
                    if use_cb:
                        dcb_s[h, pl.ds(ks, klen), :] += jnp.sum(dst, axis=1, keepdims=True)
                        dr = drow[h][:, q0:] + jnp.sum(dst, axis=0, keepdims=True)
                        out.append(dr if q0 == 0 else jnp.concatenate([drow[h][:, :q0], dr], axis=1))
                    else:
                        out.append(drow[h])
                return tuple(out)

            drow = lax.fori_loop(0, qi, lambda kb, c: tile(kb, tab_ref[qi - kb] if off_diag_bias else None, 0, tk, 0, c),
                                 tuple(jnp.zeros((1, tq), F32) for _ in heads))
            half = tk // 2
            drow = tile(qi, tab_ref[0, 0:half, :], 0, half, 0, drow)
            drow = tile(qi, tab_ref[0, half:, half:], half, half, half, drow)
            dq = (jnp.concatenate([a[...] for a in dqt_s], axis=0) * ATTN_SCALE).T
            if rope_refs is not None:
                dq = _rotate(dq, *[coef[pl.ds(qs, tq), :] for coef in rope_refs], True)
            dq_ref[pl.ds(qs, tq), :] = dq.astype(dq_ref.dtype)
            if use_cb:
                for h in range(HEADS_PER_STEP):
                    drow_ref[0, h, qi] = drow[h]
            return carry

        lax.fori_loop(0, nq, q_loop, 0)
        for cblk in range(nk):
            rows = slice(cblk * tk, (cblk + 1) * tk)
            dk = jnp.concatenate([dkt_s[h, cblk] for h in range(HEADS_PER_STEP)], axis=0).T
            if rope_refs is not None:
                dk = _rotate(dk, *[coef[rows, :] for coef in rope_refs], True)
            dk_ref[rows, :] = dk.astype(dk_ref.dtype)
            dv_ref[rows, :] = jnp.concatenate([dvt_s[h, cblk] for h in range(HEADS_PER_STEP)], axis=0).T.astype(dv_ref.dtype)
            if use_cb:
                for h in range(HEADS_PER_STEP):
                    dcb_ref[0, h, cblk] = jnp.broadcast_to(dcb_s[h, rows, :], (tk, LANES)).T[0:1, :]

    def seq_spec(off):
        return pl.BlockSpec((s, LANES), lambda b, j: (b, off + j))

    row_spec = pl.BlockSpec((1, HEADS_PER_STEP, nq, 1, tq), lambda b, j: (b, j, 0, 0, 0))
    in_specs = [seq_spec(q_off), seq_spec(k_off), seq_spec(v_off), seq_spec(0), row_spec, seq_spec(0),
                pl.BlockSpec(table.shape, lambda b, j: (0, 0, 0))]
    args = [q_arr, k_arr, v_arr, o_arr, lse_arr, do_arr, table]
    width = npairs * LANES
    out_specs = [seq_spec(0)] * 3
    out_shape = [jax.ShapeDtypeStruct((t, width), BF16)] * 3
    scratch = [pltpu.VMEM((nk, LANES, tk), BF16), pltpu.VMEM((HEADS_PER_STEP, nk, HEAD_DIM, tk), F32),
               pltpu.VMEM((HEADS_PER_STEP, nk, HEAD_DIM, tk), F32)] + [pltpu.VMEM((HEAD_DIM, tq), F32)] * HEADS_PER_STEP
    if use_cb:
        cb_spec = pl.BlockSpec((1, HEADS_PER_STEP, nk, 1, tk), lambda b, j: (b, j, 0, 0, 0))
        in_specs.append(cb_spec)
        args.append(colbias)
    if rope_tabs is not None:
        in_specs += [pl.BlockSpec((s, LANES), lambda b, j: (b, 0))] * 3
        args += list(rope_tabs)
    if use_cb:
        out_specs += [cb_spec, row_spec]
        out_shape += [jax.ShapeDtypeStruct(colbias.shape, F32), jax.ShapeDtypeStruct(lse_arr.shape, F32)]
        scratch += [pltpu.VMEM((HEADS_PER_STEP, s, 1), F32), pltpu.VMEM((nk, tk, LANES), F32)]
    return _pcall(
        body, side=side, name=name, grid=(nb, npairs), in_specs=in_specs, out_specs=out_specs, out_shape=out_shape,
        scratch_shapes=scratch, compiler_params=_params(("arbitrary", "arbitrary")),
    )(*args)


def ada_fwd(c_all, w_ada, b_cols, name):
    def body(c_ref, w_ref, b_ref, o_ref):
        cv = c_ref[...]
        sc = (cv * _sigmoid(cv)).astype(BF16)
        o_ref[...] = jnp.dot(sc, w_ref[...].astype(BF16), preferred_element_type=F32) + b_ref[...]

    return _pcall(body, name=name, out_shape=jax.ShapeDtypeStruct((c_all.shape[0], w_ada.shape[1]), F32),
                  compiler_params=_params())(c_all, w_ada, b_cols)


def ada_bwd(c_all, dmod_cols, name):
    def body(c_ref, d_ref, o_ref):
        cv = c_ref[...]
        sc = (cv * _sigmoid(cv)).astype(BF16)
        o_ref[...] = lax.dot_general(sc, d_ref[...].astype(BF16), TN_DIMS, preferred_element_type=F32)

    return _pcall(body, name=name, out_shape=jax.ShapeDtypeStruct((c_all.shape[1], dmod_cols.shape[1]), F32),
                  compiler_params=_params())(c_all, dmod_cols)


def adamw(parts, group, w, m, v, name, tr=None):
    n = parts.shape[0]
    r, c = w.shape
    tr = r if tr is None else tr
    c1 = 1.0 - ADAM_B1 ** ADAM_STEP
    c2 = 1.0 - ADAM_B2 ** ADAM_STEP

    def body(p_ref, w_ref, m_ref, v_ref, g_ref, d_ref, nm_ref, nv_ref):
        g = p_ref[0, 0].astype(F32)
        for i in range(1, n):
            g = g + p_ref[i, 0].astype(F32)
        wv = w_ref[...]
        nm = ADAM_B1 * m_ref[...] + (1.0 - ADAM_B1) * g
        nv = ADAM_B2 * v_ref[...] + (1.0 - ADAM_B2) * (g * g)
        g_ref[...] = g
        nm_ref[...] = nm
        nv_ref[...] = nv
        d_ref[...] = -ADAM_LR * ((nm / c1) / (jnp.sqrt(nv / c2) + ADAM_EPS) + ADAM_WD * wv)

    spec = pl.BlockSpec((tr, c), lambda i: (i, 0))
    shape = jax.ShapeDtypeStruct((r, c), F32)
    return _pcall(
        body, name=name, grid=(r // tr,),
        in_specs=[pl.BlockSpec((n, 1, tr, c), lambda i: (0, group, i, 0)), spec, spec, spec],
        out_specs=[spec] * 4, out_shape=[shape] * 4, compiler_params=_params(("arbitrary",)),
    )(parts, w, m, v)


def all_gather(arrs, name):
    n = len(arrs)
    hbm = pl.BlockSpec(memory_space=pl.ANY)

    def body(*refs):
        ins, outs = refs[:n], refs[n:2 * n]
        send_sems, recv_sems, local_sems = refs[2 * n:]
        x, y, c = _place()
        me, sibling = (x, y, c), (x, y, 1 - c)
        chips = [(1 - x, y), (x, 1 - y), (1 - x, 1 - y)]

        def copy(a, k, block, to, src=None):
            dst = outs[a].at[_slot(block)]
            return pltpu.make_async_remote_copy(
                src_ref=dst if src is None else src, dst_ref=dst, send_sem=send_sems.at[a * 7 + k],
                recv_sem=recv_sems.at[a * 7 + k], device_id=to, device_id_type=MESH)

        mine = [pltpu.make_async_copy(ins[a], outs[a].at[_slot(me)], local_sems.at[a]) for a in range(n)]
        for cp in mine:
            cp.start()
        first = []
        for a in range(n):
            first.append(copy(a, 0, me, sibling, src=ins[a]))
            first += [copy(a, 1 + j, me, (*chip, c), src=ins[a]) for j, chip in enumerate(chips)]
        for cp in first:
            cp.start()
        passed = []
        for a in range(n):
            for j, chip in enumerate(chips):
                copy(a, 1 + j, (*chip, c), me).wait_recv()
                cp = copy(a, 4 + j, (*chip, c), sibling)
                cp.start()
                passed.append(cp)
        for a in range(n):
            copy(a, 0, sibling, me).wait_recv()
            for j, chip in enumerate(chips):
                copy(a, 4 + j, (*chip, 1 - c), me).wait_recv()
        for cp in first + passed:
            cp.wait_send()
        for cp in mine:
            cp.wait()

    return _pcall(
        body, name=name, in_specs=[hbm] * n, out_specs=[hbm] * n,
        out_shape=[jax.ShapeDtypeStruct((N_DEV,) + a.shape, a.dtype) for a in arrs],
        scratch_shapes=[pltpu.SemaphoreType.DMA((7 * n,)), pltpu.SemaphoreType.DMA((7 * n,)),
                        pltpu.SemaphoreType.DMA((n,))],
        compiler_params=pltpu.CompilerParams(has_side_effects=True),
    )(*arrs)


def _t(w):
    return jnp.swapaxes(w, -1, -2)


def _rows_from_blocks(blocks, pad_to=None):
    full = blocks.reshape(-1, blocks.shape[2])
    if pad_to is not None and pad_to > full.shape[0]:
        full = jnp.pad(full, ((0, pad_to - full.shape[0]), (0, 0)))
    return full


def _rows_to_blocks(full, nrows):
    return full[:nrows].reshape(N_DEV, nrows // N_DEV, full.shape[1])


SMALL_ORDER = ("g_pre_ff1", "g_post_ff1", "g_pre_mix", "g_post_mix", "g_out_a", "g_out_b", "g_pre_ff2", "g_post_ff2",
               "b_forget")


def _pack_small(vals):
    rows = []
    for name in SMALL_ORDER:
        v = vals[name].reshape(1, -1)
        if v.shape[1] % LANES:
            v = jnp.pad(v, ((0, 0), (0, LANES - v.shape[1] % LANES)))
        rows.append(v)
    return jnp.concatenate(rows, axis=1)


def _unpack_small(row, sizes):
    out, pos = {}, 0
    for name in SMALL_ORDER:
        n = sizes[name]
        out[name] = row[:, pos:pos + n]
        pos += -(-n // LANES) * LANES
    return out


def _ffn_forward(x, mod, g_pre, g_post, wg, wu, wd, i0, nb, tag, target=None, side=None, side_down=None):
    h = prenorm_fwd(x, g_pre, mod, i0, i0 + 1, nb, f"{tag}_prenorm")
    res, side_out = ffn_up(h, wg, wu, f"{tag}_up", side=side), None
    if side is not None:
        res, side_out = res
    gate, up, act = res
    if callable(wd):
        wd = wd(side_out)
    res, side_down_out = postnorm_fwd(x, [(act, wd)], g_post, mod, i0 + 2, 0.5, nb, f"{tag}_down_postnorm",
                                      target=target, side=side_down), None
    if side_down is not None:
        res, side_down_out = res
    out, y0 = (res[0] if target is None else tuple(res[:2])), res[-1]
    return out, (x, h, gate, up, act, y0), wd, side_out, side_down_out


def _ffn_backward(dxo, saved, mod, g_pre, g_post, wg, wu, wd, i0, nb, tag, side=None, chain=False):
    x, h, gate, up, act, y0 = saved
    dy0, dg_post, dgate_mod = postnorm_bwd(dxo, y0, g_post, mod, i0 + 2, 0.5, nb, f"{tag}_postnorm_bwd")
    dwd = mm_tn(act, dy0, BF16, f"{tag}_dwd", rows=D_FF)
    res, side_out = ffn_down_bwd(dy0, wd, gate, up, f"{tag}_down_bwd", side=side), None
    if side is not None:
        res, side_out = res
    dgate, dup = res
    dh_pairs = [(dgate, wg), (dup, wu)]
    if chain:
        dwg, (dwd,) = mm_tn(dgate, h, BF16, f"{tag}_dwg", rows=D_FF, side=([_rows_to_blocks(dwd, D_FF)[:, None]], False))
        dwu, (dwg,) = mm_tn(dup, h, BF16, f"{tag}_dwu", rows=D_FF, side=([_rows_to_blocks(dwg, D_FF)[:, None]], False))
        (dx, dg_pre, dsc, dsh), (dwu,) = prenorm_bwd(dh_pairs, x, g_pre, mod, i0 + 1, dxo, nb, f"{tag}_dh_prenorm_bwd",
                                                     ts=DH_ROWS, side=([_rows_to_blocks(dwu, D_FF)[:, None]], False))
    else:
        dwg = mm_tn(dgate, h, BF16, f"{tag}_dwg", rows=D_FF)
        dwu = mm_tn(dup, h, BF16, f"{tag}_dwu", rows=D_FF)
        dx, dg_pre, dsc, dsh = prenorm_bwd(dh_pairs, x, g_pre, mod, i0 + 1, dxo, nb, f"{tag}_dh_prenorm_bwd", ts=DH_ROWS)
    return dx, dict(g_pre=dg_pre, g_post=dg_post, wg=dwg, wu=dwu, wd=dwd, mod=(dsh, dsc, dgate_mod)), side_out


def kernel(x, c, positions, w_ada, b_ada, g_pre_ff1, g_post_ff1, w_ff1_gate, w_ff1_up, w_ff1_down, g_pre_mix, g_post_mix, w_in, b_forget, g_out_a, g_out_b, w_out, g_pre_ff2, g_post_ff2, w_ff2_gate, w_ff2_up, w_ff2_down, loss_target, m_w_ada, m_b_ada, m_g_pre_ff1, m_g_post_ff1, m_w_ff1_gate, m_w_ff1_up, m_w_ff1_down, m_g_pre_mix, m_g_post_mix, m_w_in, m_b_forget, m_g_out_a, m_g_out_b, m_w_out, m_g_pre_ff2, m_g_post_ff2, m_w_ff2_gate, m_w_ff2_up, m_w_ff2_down, v_w_ada, v_b_ada, v_g_pre_ff1, v_g_post_ff1, v_w_ff1_gate, v_w_ff1_up, v_w_ff1_down, v_g_pre_mix, v_g_post_mix, v_w_in, v_b_forget, v_g_out_a, v_g_out_b, v_w_out, v_g_pre_ff2, v_g_post_ff2, v_w_ff2_gate, v_w_ff2_up, v_w_ff2_down):
    weights = dict(w_ada=w_ada, b_ada=b_ada, g_pre_ff1=g_pre_ff1, g_post_ff1=g_post_ff1, w_ff1_gate=w_ff1_gate,
                   w_ff1_up=w_ff1_up, w_ff1_down=w_ff1_down, g_pre_mix=g_pre_mix, g_post_mix=g_post_mix, w_in=w_in,
                   b_forget=b_forget, g_out_a=g_out_a, g_out_b=g_out_b, w_out=w_out, g_pre_ff2=g_pre_ff2,
                   g_post_ff2=g_post_ff2, w_ff2_gate=w_ff2_gate, w_ff2_up=w_ff2_up, w_ff2_down=w_ff2_down)
    mom_m = dict(w_ada=m_w_ada, b_ada=m_b_ada, g_pre_ff1=m_g_pre_ff1, g_post_ff1=m_g_post_ff1, w_ff1_gate=m_w_ff1_gate,
                 w_ff1_up=m_w_ff1_up, w_ff1_down=m_w_ff1_down, g_pre_mix=m_g_pre_mix, g_post_mix=m_g_post_mix,
                 w_in=m_w_in, b_forget=m_b_forget, g_out_a=m_g_out_a, g_out_b=m_g_out_b, w_out=m_w_out,
                 g_pre_ff2=m_g_pre_ff2, g_post_ff2=m_g_post_ff2, w_ff2_gate=m_w_ff2_gate, w_ff2_up=m_w_ff2_up,
                 w_ff2_down=m_w_ff2_down)
    mom_v = dict(w_ada=v_w_ada, b_ada=v_b_ada, g_pre_ff1=v_g_pre_ff1, g_post_ff1=v_g_post_ff1, w_ff1_gate=v_w_ff1_gate,
                 w_ff1_up=v_w_ff1_up, w_ff1_down=v_w_ff1_down, g_pre_mix=v_g_pre_mix, g_post_mix=v_g_post_mix,
                 w_in=v_w_in, b_forget=v_b_forget, g_out_a=v_g_out_a, g_out_b=v_g_out_b, w_out=v_w_out,
                 g_pre_ff2=v_g_pre_ff2, g_post_ff2=v_g_post_ff2, w_ff2_gate=v_w_ff2_gate, w_ff2_up=v_w_ff2_up,
                 w_ff2_down=v_w_ff2_down)
    order = list(weights)

    nb, s, d = x.shape
    t = nb * s
    me = _slot(_place())
    nbg = nb * N_DEV
    ada_cols = w_ada.shape[2]

    bf = lambda w: w[0].astype(BF16)
    bft = lambda w: _t(w)[0].astype(BF16)
    c_all, wg1, wu1 = all_gather([c, bft(w_ff1_gate), bft(w_ff1_up)], "gather_ff1")
    c_all = c_all.reshape(nbg, d)
    wg1, wu1 = (_rows_from_blocks(w, D_FF_PAD) for w in (wg1, wu1))

    b_cols = lax.dynamic_slice(b_ada, (0, me * ada_cols), (1, ada_cols))
    mod_cols = ada_fwd(c_all, w_ada[0], b_cols, "ada_fwd")
    (mod_all,) = all_gather([mod_cols], "gather_mod")
    mod = lax.dynamic_slice(mod_all, (0, me * nb, 0), (N_DEV, nb, ada_cols))
    mod = mod.transpose(1, 0, 2).reshape(nb, N_MOD, d)

    xf = x.reshape(t, d)
    target = loss_target.reshape(t, d)

    x1, saved1, wd1, (_, w_out_all), (w_in_all,) = _ffn_forward(
        xf, mod, g_pre_ff1, g_post_ff1, wg1, wu1, lambda got: _rows_from_blocks(got[0], D_FF_PAD), 0, nb, "ff1",
        side=([bf(w_ff1_down), bf(w_out)], "two_level"), side_down=([bft(w_in)], "two_level"))
    w_in_t = _rows_from_blocks(w_in_all)
    n_qkv = 3 * (WIDTH_A + WIDTH_B)
    w_qkv_t = w_in_t[:n_qkv]
    w_f_t = jnp.pad(w_in_t[n_qkv:], ((0, LANES - N_HEADS_B), (0, 0)))
    w_o = _rows_from_blocks(w_out_all)
    w_o_a, w_o_b = w_o[:WIDTH_A], w_o[WIDTH_A:]

    h2 = prenorm_fwd(x1, g_pre_mix, mod, 3, 4, nb, "mix_prenorm")
    tables = rope_tables(positions)
    proj = mm_rows([(h2, w_qkv_t)], True, BF16, "mix_proj", rope=(tables, 2 * WIDTH_A))
    f_logit = mm_rows([(h2, w_f_t)], True, F32, "mix_forget")
    tab_a = dilated_table(s, ATTN_TQ, ATTN_TK)
    tab_b = causal_table(s, ATTN_TQ, ATTN_TK)
    ft = f_logit[:, :N_HEADS_B].reshape(nb, s, N_HEADS_B).transpose(0, 2, 1)
    bf_col = b_forget.reshape(N_HEADS_B, 1)
    colbias = fox_gate_fwd(ft, bf_col, "fox_gate").reshape(nb, N_HEADS_B, s // ATTN_TK, 1, ATTN_TK)
    pa = WIDTH_A // LANES
    (o_a, lse_a), ff2_all = attn_fwd(
        proj, 0, proj, pa, proj, 2 * pa, tab_a, None, nb, "attn_a",
        side=([bft(w_ff2_gate), bft(w_ff2_up), bf(w_ff2_down)], "two_level"))
    wg2, wu2, wd2 = (_rows_from_blocks(w, D_FF_PAD) for w in ff2_all)
    o_b, lse_b = attn_fwd(proj, 3 * pa, proj, 4 * pa, proj, 5 * pa, tab_b, colbias, nb, "attn_b", off_diag_bias=False)
    m_a = prenorm_fwd(o_a, g_out_a, None, None, None, nb, "out_norm_a")
    m_b = prenorm_fwd(o_b, g_out_b, None, None, None, nb, "out_norm_b")
    x2, y0m = postnorm_fwd(x1, [(m_a, w_o_a), (m_b, w_o_b)], g_post_mix, mod, 5, 1.0, nb, "mix_out_postnorm")

    (dx3, loss_part), saved2 = _ffn_forward(x2, mod, g_pre_ff2, g_post_ff2, wg2, wu2, wd2, 6, nb, "ff2", target=target)[:2]
    loss = lax.psum(loss_part[0, 0], ("x", "y", "c"))

    dx2, gr2, _ = _ffn_backward(dx3, saved2, mod, g_pre_ff2, g_post_ff2, wg2, wu2, wd2, 6, nb, "ff2")
    ff2_blocks = [_rows_to_blocks(gr2[k], D_FF)[:, None] for k in ("wg", "wu", "wd")]

    dy0m, dg_post_mix, dgate_mix = postnorm_bwd(dx2, y0m, g_post_mix, mod, 5, 1.0, nb, "mix_postnorm_bwd")
    dw_o = mm_tn_stack([m_a, m_b], dy0m, [WIDTH_A, WIDTH_B], BF16, "mix_dwo")
    do_a, dg_out_a = prenorm_bwd([(dy0m, w_o_a.T)], o_a, g_out_a, None, None, None, nb, "out_norm_a_bwd")
    do_b, dg_out_b = prenorm_bwd([(dy0m, w_o_b.T)], o_b, g_out_b, None, None, None, nb, "out_norm_b_bwd")
    (dq_a, dk_a, dv_a), g_ff2 = attn_bwd(proj, 0, proj, pa, proj, 2 * pa, o_a, lse_a, do_a, tab_a, None, nb,
                                            "attn_a_bwd", side=(ff2_blocks, False), rope_tabs=tables)
    dq_b, dk_b, dv_b, dcb, drow = attn_bwd(proj, 3 * pa, proj, 4 * pa, proj, 5 * pa, o_b, lse_b, do_b, tab_b, colbias, nb,
                                           "attn_b_bwd", off_diag_bias=False)
    dz_t, db_forget = fox_gate_bwd(dcb.reshape(nb, N_HEADS_B, s), drow.reshape(nb, N_HEADS_B, s), ft, bf_col,
                                   "fox_gate_bwd")
    dz = jnp.pad(dz_t.transpose(0, 2, 1).reshape(t, N_HEADS_B), ((0, 0), (0, LANES - N_HEADS_B))).astype(BF16)
    pieces = [dq_a, dk_a, dv_a, dq_b, dk_b, dv_b]
    w_pieces = [w_qkv_t[i * WIDTH_A:(i + 1) * WIDTH_A] for i in range(6)]
    dh2_pairs = list(zip(pieces, w_pieces)) + [(dz, w_f_t)]
    dw_in_t = mm_tn_stack(pieces + [dz], h2, [WIDTH_A] * 6 + [N_HEADS_B], BF16, "mix_dwin")
    dx1, dg_pre_mix, dsc_mix, dsh_mix = prenorm_bwd(dh2_pairs, x1, g_pre_mix, mod, 4, dx2, nb, "mix_dh_prenorm_bwd",
                                                    ts=DH_ROWS)

    g_in = _rows_to_blocks(dw_in_t, dw_in_t.shape[0])[:, None]
    g_out = _rows_to_blocks(dw_o, d)[:, None]
    dx0, gr1, (g_in, g_out) = _ffn_backward(dx1, saved1, mod, g_pre_ff1, g_post_ff1, wg1, wu1, wd1, 0, nb, "ff1",
                                            side=([g_in, g_out], False), chain=True)
    grad_x = dx0.reshape(nb, s, d)

    dmod =jnp.concatenate(list(gr1["mod"]) + [dsh_mix, dsc_mix, dgate_mix] + list(gr2["mod"]), axis=1)
    small = _pack_small(dict(g_pre_ff1=gr1["g_pre"], g_post_ff1=gr1["g_post"], g_pre_mix=dg_pre_mix,
                             g_post_mix=dg_post_mix, g_out_a=dg_out_a, g_out_b=dg_out_b, g_pre_ff2=gr2["g_pre"],
                             g_post_ff2=gr2["g_post"], b_forget=db_forget))
    dmod_all, small_all = all_gather([dmod.reshape(nb, N_MOD * d), small], "gather_small_grads")
    dmod_all = dmod_all.reshape(nbg, N_MOD * d)

    res = {}
    def adamw_t(parts, group, n):
        return tuple(_t(r) for r in adamw(parts, group, _t(weights[n])[0], _t(mom_m[n])[0], _t(mom_v[n])[0], f"adamw_{n}"))

    res["w_ff1_gate"] = adamw_t(gr1["wg"], 0, "w_ff1_gate")
    res["w_ff1_up"] = adamw_t(gr1["wu"], 0, "w_ff1_up")
    res["w_ff2_gate"] = adamw_t(g_ff2[0], 0, "w_ff2_gate")
    res["w_ff2_up"] = adamw_t(g_ff2[1], 0, "w_ff2_up")
    res["w_ff1_down"] = adamw(gr1["wd"], 0, w_ff1_down[0], m_w_ff1_down[0], v_w_ff1_down[0], "adamw_ff1_down")
    res["w_ff2_down"] = adamw(g_ff2[2], 0, w_ff2_down[0], m_w_ff2_down[0], v_w_ff2_down[0], "adamw_ff2_down")
    res["w_in"] = adamw_t(g_in, 0, "w_in")
    res["w_out"] = adamw(g_out, 0, w_out[0], m_w_out[0], v_w_out[0], "adamw_out")
    dmod_cols = lax.dynamic_slice(dmod_all, (0, me * ada_cols), (nbg, ada_cols))
    dw_ada = ada_bwd(c_all, dmod_cols, "ada_bwd")
    res["w_ada"] = adamw(dw_ada[None, None], 0, w_ada[0], m_w_ada[0], v_w_ada[0], "adamw_ada", tr=256)
    res["b_ada"] = adamw(dmod_all[:, None, None], 0, b_ada, m_b_ada, v_b_ada, "adamw_b_ada")
    sizes = {n: weights[n].shape[1] for n in SMALL_ORDER}
    small_res = adamw(small_all[:, None], 0, _pack_small(weights), _pack_small(mom_m), _pack_small(mom_v), "adamw_small")
    small_res = [_unpack_small(r, sizes) for r in small_res]
    for n in SMALL_ORDER:
        res[n] = tuple(r[n] for r in small_res)

    outs = [loss, grad_x]
    for kind in range(4):
        for n in order:
            a = res[n][kind]
            outs.append(a.reshape(weights[n].shape))
    return tuple(outs)
```

```python
import functools

import jax
import jax.numpy as jnp
from jax import lax
from jax.experimental import pallas as pl
from jax.experimental.pallas import tpu as pltpu

F32 = jnp.float32
BF16 = jnp.bfloat16

D_MODEL = 1024
HEAD_DIM = 64
N_HEADS_A = 8
N_HEADS_B = 8
WIDTH_A = N_HEADS_A * HEAD_DIM
WIDTH_B = N_HEADS_B * HEAD_DIM
DILATED_PATTERNS = ((128, 1), (512, 4), (2048, 16))
ROT_DIM = HEAD_DIM // 4
ROPE_THETA = 500000.0
D_FF = 2752
D_FF_PAD = 2816
N_MOD = 9
EPS = 1e-6
ATTN_SCALE = HEAD_DIM ** -0.5
NEG = -1e30
N_DEV = 8
LANES = 128
HEADS_PER_STEP = LANES // HEAD_DIM

ADAM_LR = 0.001
ADAM_B1 = 0.9
ADAM_B2 = 0.999
ADAM_EPS = 1e-08
ADAM_WD = 0.01
ADAM_STEP = 10

VMEM_LIMIT = 56 * 1024 * 1024
MESH = pl.DeviceIdType.MESH

NT_DIMS = (((1,), (1,)), ((), ()))
TN_DIMS = (((0,), (0,)), ((), ()))
NN_DIMS = (((1,), (0,)), ((), ()))


def _place():
    return lax.axis_index("x"), lax.axis_index("y"), lax.axis_index("c")


def _slot(p):
    return 4 * p[0] + 2 * p[1] + p[2]


def _direct_copies(ins, outs, send_sems, recv_sems, local_sems, gather):
    x, y, c = _place()
    me = (x, y, c)
    flip = lambda v, bit: 1 - v if bit else v
    peers = [(flip(x, k & 4), flip(y, k & 2), flip(c, k & 1)) for k in range(1, N_DEV)]
    local, sends, recvs = [], [], []
    for a in range(len(ins)):
        mine = ins[a] if gather else ins[a].at[_slot(me)]
        local.append(pltpu.make_async_copy(mine, outs[a].at[_slot(me)], local_sems.at[a]))
        for k, peer in enumerate(peers):
            sems = dict(send_sem=send_sems.at[a * 7 + k], recv_sem=recv_sems.at[a * 7 + k], device_id=peer,
                        device_id_type=MESH)
            sends.append(pltpu.make_async_remote_copy(
                src_ref=ins[a] if gather else ins[a].at[_slot(peer)], dst_ref=outs[a].at[_slot(me)], **sems))
            recvs.append(pltpu.make_async_remote_copy(src_ref=mine, dst_ref=outs[a].at[_slot(peer)], **sems))
    return local, sends, recvs


def _two_level_copies(ins, outs, send_sems, recv_sems, local_sems):
    x, y, c = _place()
    me, sibling = (x, y, c), (x, y, 1 - c)
    chips = [(1 - x, y), (x, 1 - y), (1 - x, 1 - y)]
    local, first, landed, forwards, late = [], [], [], [], []
    for a in range(len(ins)):
        def copy(k, block, to, src=None, a=a):
            dst = outs[a].at[_slot(block)]
            return pltpu.make_async_remote_copy(
                src_ref=dst if src is None else src, dst_ref=dst, send_sem=send_sems.at[a * 7 + k],
                recv_sem=recv_sems.at[a * 7 + k], device_id=to, device_id_type=MESH)

        local.append(pltpu.make_async_copy(ins[a], outs[a].at[_slot(me)], local_sems.at[a]))
        first.append(copy(0, me, sibling, src=ins[a]))
        late.append(copy(0, sibling, me))
        for j, chip in enumerate(chips):
            first.append(copy(1 + j, me, (*chip, c), src=ins[a]))
            landed.append(copy(1 + j, (*chip, c), me))
            forwards.append(copy(4 + j, (*chip, c), sibling))
            late.append(copy(4 + j, (*chip, 1 - c), me))
    return local, first, landed, forwards, late


def _comm_scratch(n):
    return [pltpu.SemaphoreType.DMA((7 * n,)), pltpu.SemaphoreType.DMA((7 * n,)), pltpu.SemaphoreType.DMA((n,))]


def _pcall(body, side=None, **kw):
    if side is None:
        return pl.pallas_call(body, **kw)
    arrs, gather = side
    n = len(arrs)
    grid = kw["grid"]
    in_specs = list(kw["in_specs"])
    single = not isinstance(kw["out_specs"], (list, tuple))
    out_specs = [kw["out_specs"]] if single else list(kw["out_specs"])
    out_shape = [kw["out_shape"]] if single else list(kw["out_shape"])
    scratch = list(kw.get("scratch_shapes", []))
    n_in, n_out, n_scr = len(in_specs), len(out_specs), len(scratch)
    hbm = pl.BlockSpec(memory_space=pl.ANY)

    def hosted(*refs):
        pos = [0]

        def take(k):
            pos[0] += k
            return refs[pos[0] - k:pos[0]]

        ins, s_ins, outs, s_outs, scr, sems = take(n_in), take(n), take(n_out), take(n), take(n_scr), take(3)
        ids = [pl.program_id(i) for i in range(len(grid))]
        first = functools.reduce(jnp.logical_and, [i == 0 for i in ids])
        last = functools.reduce(jnp.logical_and, [i == g - 1 for i, g in zip(ids, grid)])
        if gather == "two_level":
            axis = max(range(len(grid)), key=lambda i: grid[i])
            assert grid[axis] >= 2
            middle = functools.reduce(jnp.logical_and, [i == (grid[axis] // 2 if k == axis else 0)
                                                        for k, i in enumerate(ids)])

            @pl.when(first)
            def _():
                local, sends, _, _, _ = _two_level_copies(s_ins, s_outs, *sems)
                for cp in local + sends:
                    cp.start()

            @pl.when(middle)
            def _():
                _, _, landed, forwards, _ = _two_level_copies(s_ins, s_outs, *sems)
                for cp_in, cp_out in zip(landed, forwards):
                    cp_in.wait_recv()
                    cp_out.start()

            body(*ins, *outs, *scr)

            @pl.when(last)
            def _():
                local, sends, _, forwards, late = _two_level_copies(s_ins, s_outs, *sems)
                for cp in late:
                    cp.wait_recv()
                for cp in sends + forwards:
                    cp.wait_send()
                for cp in local:
                    cp.wait()
            return

        @pl.when(first)
        def _():
            local, sends, _ = _direct_copies(s_ins, s_outs, *sems, gather)
            for cp in local + sends:
                cp.start()

        body(*ins, *outs, *scr)

        @pl.when(last)
        def _():
            local, sends, recvs = _direct_copies(s_ins, s_outs, *sems, gather)
            for cp in recvs:
                cp.wait_recv()
            for cp in sends:
                cp.wait_send()
            for cp in local:
                cp.wait()

    kw.update(in_specs=in_specs + [hbm] * n, out_specs=out_specs + [hbm] * n,
              out_shape=out_shape + [jax.ShapeDtypeStruct(((N_DEV,) + a.shape) if gather else a.shape, a.dtype)
                                     for a in arrs],
              scratch_shapes=scratch + _comm_scratch(n))
    call = pl.pallas_call(hosted, **kw)

    def run(*args):
        res = call(*args, *arrs)
        main = res[0] if single else list(res[:n_out])
        return main, list(res[n_out:])

    return run


def _params(sem=None, **kw):
    if sem is not None:
        kw["dimension_semantics"] = sem
    return pltpu.CompilerParams(vmem_limit_bytes=VMEM_LIMIT, **kw)


def _rotate(xv, c, sp, sm, transpose):
    width = xv.shape[1]
    half = ROT_DIM // 2
    if transpose:
        return xv * c + pltpu.roll(xv * sp, width - half, 1) + pltpu.roll(xv * sm, half, 1)
    return xv * c + pltpu.roll(xv, half, 1) * sp + pltpu.roll(xv, width - half, 1) * sm


def mm_rows(pairs, trans_b, out_dtype, name, tm=512, side=None, rope=None):
    n = len(pairs)
    m = pairs[0][0].shape[0]
    n_out = pairs[0][1].shape[0 if trans_b else 1]
    dims = NT_DIMS if trans_b else NN_DIMS

    def body(*refs):
        o_ref = refs[-1]
        acc = None
        for a_ref, b_ref in zip(refs[:n], refs[n:2 * n]):
            d = lax.dot_general(a_ref[...], b_ref[...], dims, preferred_element_type=F32)
            acc = d if acc is None else acc + d
        if rope is None:
            o_ref[...] = acc.astype(o_ref.dtype)
        else:
            width = rope[1]
            c, sp, sm = (jnp.concatenate([r[...]] * (width // LANES), axis=1) for r in refs[2 * n:2 * n + 3])
            o_ref[:, :width] = _rotate(acc[:, :width], c, sp, sm, False).astype(o_ref.dtype)
            o_ref[:, width:] = acc[:, width:].astype(o_ref.dtype)

    in_specs = [pl.BlockSpec((tm, a.shape[1]), lambda i: (i, 0)) for a, _ in pairs]
    in_specs += [pl.BlockSpec(b.shape, lambda i: (0, 0)) for _, b in pairs]
    args = [a for a, _ in pairs] + [b for _, b in pairs]
    if rope is not None:
        in_specs += [pl.BlockSpec((tm, LANES), lambda i: (i, 0))] * 3
        args += list(rope[0])
    return _pcall(
        body, side=side, name=name, grid=(m // tm,), in_specs=in_specs,
        out_specs=pl.BlockSpec((tm, n_out), lambda i: (i, 0)),
        out_shape=jax.ShapeDtypeStruct((m, n_out), out_dtype),
        compiler_params=_params(("arbitrary",)),
    )(*args)


DH_ROWS = 256
TN_TOKENS = 2048
TN_OUT_ELEMS = 2 * 1024 * 1024


def mm_tn(a, b, out_dtype, name, side=None, rows=None):
    t, ka = a.shape
    n_out = b.shape[1]
    tk = min(TN_TOKENS, t)
    tka = ka // 2 if ka * n_out > TN_OUT_ELEMS else ka
    tn = n_out
    steps = t // tk

    def body(a_ref, b_ref, o_ref, acc_ref):
        k = pl.program_id(2)
        d = lax.dot_general(a_ref[...], b_ref[...], TN_DIMS, preferred_element_type=F32)

        @pl.when(k == 0)
        def _():
            acc_ref[...] = d

        @pl.when(k > 0)
        def _():
            acc_ref[...] += d

        @pl.when(k == steps - 1)
        def _():
            o_ref[...] = acc_ref[...].astype(o_ref.dtype)

    return _pcall(
        body, side=side, name=name, grid=(ka // tka, n_out // tn, steps),
        in_specs=[pl.BlockSpec((tk, tka), lambda i, j, k: (k, i)), pl.BlockSpec((tk, tn), lambda i, j, k: (k, j))],
        out_specs=pl.BlockSpec((tka, tn), lambda i, j, k: (i, j)),
        out_shape=jax.ShapeDtypeStruct((ka if rows is None else rows, n_out), out_dtype),
        scratch_shapes=[pltpu.VMEM((tka, tn), F32)],
        compiler_params=_params(("arbitrary", "arbitrary", "arbitrary")),
    )(a, b)


def mm_tn_stack(a_list, b, rows, out_dtype, name, tk=1024):
    t, n_out = b.shape
    tk = min(tk, t)
    steps = t // tk
    n = len(a_list)
    offs = [sum(rows[:i]) for i in range(n)]

    def body(*refs):
        a_refs, b_ref, o_ref, acc_refs = refs[:n], refs[n], refs[n + 1], refs[n + 2:]
        k = pl.program_id(0)
        bv = b_ref[...]
        for a_ref, acc_ref in zip(a_refs, acc_refs):
            d = lax.dot_general(a_ref[...], bv, TN_DIMS, preferred_element_type=F32)

            @pl.when(k == 0)
            def _(acc_ref=acc_ref, d=d):
                acc_ref[...] = d

            @pl.when(k > 0)
            def _(acc_ref=acc_ref, d=d):
                acc_ref[...] += d

        @pl.when(k == steps - 1)
        def _():
            for acc_ref, off, r in zip(acc_refs, offs, rows):
                o_ref[off:off + r, :] = acc_ref[0:r, :].astype(o_ref.dtype)

    return _pcall(
        body, name=name, grid=(steps,),
        in_specs=[pl.BlockSpec((tk, a.shape[1]), lambda k: (k, 0)) for a in a_list]
        + [pl.BlockSpec((tk, n_out), lambda k: (k, 0))],
        out_specs=pl.BlockSpec((sum(rows), n_out), lambda k: (0, 0)),
        out_shape=jax.ShapeDtypeStruct((sum(rows), n_out), out_dtype),
        scratch_shapes=[pltpu.VMEM((a.shape[1], n_out), F32) for a in a_list],
        compiler_params=_params(("arbitrary",)),
    )(*a_list, b)


def _col_chunks(width, chunk=512):
    return [slice(c, min(c + chunk, width)) for c in range(0, width, chunk)]


def _sigmoid(x):
    return 1.0 / (1.0 + jnp.exp(-x))


def ffn_up(h, wgt, wut, name, tm=256, tn=D_FF_PAD, side=None):
    t, d = h.shape
    fp = wgt.shape[0]

    def body(h_ref, wg_ref, wu_ref, g_ref, u_ref, a_ref):
        hv = h_ref[...]

        def finish(cols, g, u):
            g_ref[:, cols] = g.astype(BF16)
            u_ref[:, cols] = u.astype(BF16)
            a_ref[:, cols] = (g * _sigmoid(g) * u).astype(BF16)

        pending = None
        for cols in _col_chunks(tn):
            g = lax.dot_general(hv, wg_ref[cols, :], NT_DIMS, preferred_element_type=F32)
            u = lax.dot_general(hv, wu_ref[cols, :], NT_DIMS, preferred_element_type=F32)
            if pending is not None:
                finish(*pending)
            pending = (cols, g, u)
        finish(*pending)

    w_spec = pl.BlockSpec((tn, d), lambda j, i: (j, 0))
    o_spec = pl.BlockSpec((tm, tn), lambda j, i: (i, j))
    o_shape = jax.ShapeDtypeStruct((t, fp), BF16)
    return _pcall(
        body, side=side, name=name, grid=(fp // tn, t // tm),
        in_specs=[pl.BlockSpec((tm, d), lambda j, i: (i, 0)), w_spec, w_spec],
        out_specs=[o_spec, o_spec, o_spec], out_shape=[o_shape, o_shape, o_shape],
        compiler_params=_params(("arbitrary", "arbitrary")),
    )(h, wgt, wut)


def ffn_down_bwd(dy0, wd, gate, up, name, tm=256, tn=D_FF_PAD, side=None):
    t, d = dy0.shape
    fp = wd.shape[0]

    def body(dy_ref, wd_ref, g_ref, u_ref, dg_ref, du_ref):
        dyv = dy_ref[...]

        def finish(cols, dact):
            g = g_ref[:, cols].astype(F32)
            u = u_ref[:, cols].astype(F32)
            sg = _sigmoid(g)
            silu = g * sg
            du_ref[:, cols] = (dact * silu).astype(BF16)
            dg_ref[:, cols] = ((dact * u) * (sg + silu * (1.0 - sg))).astype(BF16)

        pending = None
        for cols in _col_chunks(tn):
            dact = lax.dot_general(dyv, wd_ref[cols, :], NT_DIMS, preferred_element_type=F32)
            if pending is not None:
                finish(*pending)
            pending = (cols, dact)
        finish(*pending)

    t_spec = pl.BlockSpec((tm, tn), lambda j, i: (i, j))
    o_shape = jax.ShapeDtypeStruct((t, fp), BF16)
    return _pcall(
        body, side=side, name=name, grid=(fp // tn, t // tm),
        in_specs=[pl.BlockSpec((tm, d), lambda j, i: (i, 0)), pl.BlockSpec((tn, d), lambda j, i: (j, 0)), t_spec, t_spec],
        out_specs=[t_spec, t_spec], out_shape=[o_shape, o_shape],
        compiler_params=_params(("arbitrary", "arbitrary")),
    )(dy0, wd, gate, up)


def _row_specs(dx, ts, ns):
    return pl.BlockSpec((ts, dx), lambda b, s: (b * ns + s, 0))


def _mod_spec():
    return pl.BlockSpec((1, N_MOD, D_MODEL), lambda b, s: (b, 0, 0))


def _vec_spec(dx):
    return pl.BlockSpec((1, dx), lambda b, s: (0, 0))


def prenorm_fwd(x, g, mod, i_shift, i_scale, nb, name, ts=1024):
    t, dx = x.shape
    ts = min(ts, t // nb)
    ns = t // nb // ts

    def body(*refs):
        if mod is None:
            x_ref, g_ref, h_ref = refs
        else:
            x_ref, g_ref, mod_ref, h_ref = refs
        xv = x_ref[...]
        r = lax.rsqrt(jnp.mean(xv * xv, axis=-1, keepdims=True) + EPS)
        h = xv * r * g_ref[...]
        if mod is not None:
            h = h * (1.0 + mod_ref[0, i_scale:i_scale + 1, :]) + mod_ref[0, i_shift:i_shift + 1, :]
        h_ref[...] = h.astype(BF16)

    in_specs = [_row_specs(dx, ts, ns), _vec_spec(dx)]
    args = [x, g]
    if mod is not None:
        in_specs.append(_mod_spec())
        args.append(mod)
    return _pcall(
        body, name=name, grid=(nb, ns), in_specs=in_specs, out_specs=_row_specs(dx, ts, ns),
        out_shape=jax.ShapeDtypeStruct((t, dx), BF16), compiler_params=_params(("arbitrary", "arbitrary")),
    )(*args)


def prenorm_bwd(dh, x, g, mod, i_scale, dres, nb, name, ts=512, side=None):
    t, dx = x.shape
    ts = min(ts, t // nb)
    ns = t // nb // ts
    has_mod = mod is not None
    has_res = dres is not None
    pairs = dh if isinstance(dh, list) else None
    n_mm = 0 if pairs is None else len(pairs)

    def body(*refs):
        refs = list(refs)
        if pairs is None:
            dhv = refs[0][...].astype(F32)
            refs = refs[1:]
        else:
            dhv = None
            for a_ref, b_ref in zip(refs[:n_mm], refs[n_mm:2 * n_mm]):
                d = jnp.dot(a_ref[...], b_ref[...], preferred_element_type=F32)
                dhv = d if dhv is None else dhv + d
            refs = refs[2 * n_mm:]
        x_ref, g_ref = refs[:2]
        pos = 2
        mod_ref = dres_ref = None
        if has_mod:
            mod_ref = refs[pos]
            pos += 1
        if has_res:
            dres_ref = refs[pos]
            pos += 1
        dx_ref, dg_ref = refs[pos], refs[pos + 1]
        b, s = pl.program_id(0), pl.program_id(1)
        xv = x_ref[...]
        gv = g_ref[...]
        r = lax.rsqrt(jnp.mean(xv * xv, axis=-1, keepdims=True) + EPS)
        xhat = xv * r
        dn = dhv
        if has_mod:
            dsc_ref, dsh_ref = refs[pos + 2], refs[pos + 3]
            dn = dhv * (1.0 + mod_ref[0, i_scale:i_scale + 1, :])
            dsc = jnp.sum(dhv * xhat * gv, axis=0, keepdims=True)[None]
            dsh = jnp.sum(dhv, axis=0, keepdims=True)[None]

            @pl.when(s == 0)
            def _():
                dsc_ref[...] = dsc
                dsh_ref[...] = dsh

            @pl.when(s > 0)
            def _():
                dsc_ref[...] += dsc
                dsh_ref[...] += dsh

        dg = jnp.sum(dn * xhat, axis=0, keepdims=True)
        first = jnp.logical_and(b == 0, s == 0)

        @pl.when(first)
        def _():
            dg_ref[...] = dg

        @pl.when(jnp.logical_not(first))
        def _():
            dg_ref[...] += dg

        dxhat = dn * gv
        dxv = r * (dxhat - xhat * jnp.mean(dxhat * xhat, axis=-1, keepdims=True))
        if has_res:
            dxv = dxv + dres_ref[...]
        dx_ref[...] = dxv

    row = _row_specs(dx, ts, ns)
    if pairs is None:
        in_specs, args = [row], [dh]
    else:
        in_specs = [_row_specs(a.shape[1], ts, ns) for a, _ in pairs]
        in_specs += [pl.BlockSpec(b.shape, lambda b_, s_: (0, 0)) for _, b in pairs]
        args = [a for a, _ in pairs] + [b for _, b in pairs]
    in_specs += [row, _vec_spec(dx)]
    args += [x, g]
    if has_mod:
        in_specs.append(_mod_spec())
        args.append(mod)
    if has_res:
        in_specs.append(row)
        args.append(dres)
    out_specs = [row, _vec_spec(dx)]
    out_shape = [jax.ShapeDtypeStruct((t, dx), F32), jax.ShapeDtypeStruct((1, dx), F32)]
    if has_mod:
        bspec = pl.BlockSpec((1, 1, dx), lambda b, s: (b, 0, 0))
        out_specs += [bspec, bspec]
        out_shape += [jax.ShapeDtypeStruct((nb, 1, dx), F32)] * 2
    return _pcall(
        body, side=side, name=name, grid=(nb, ns), in_specs=in_specs, out_specs=out_specs, out_shape=out_shape,
        compiler_params=_params(("arbitrary", "arbitrary")),
    )(*args)


def postnorm_fwd(x, pairs, g, mod, i_gate, coef, nb, name, target=None, ts=512, side=None):
    t, dx = x.shape
    with_loss = target is not None
    ts = min(ts, t // nb)
    ns = t // nb // ts
    n_mm = len(pairs)

    def body(*refs):
        yv = None
        for a_ref, b_ref in zip(refs[:n_mm], refs[n_mm:2 * n_mm]):
            d = jnp.dot(a_ref[...], b_ref[...], preferred_element_type=F32)
            yv = d if yv is None else yv + d
        refs = refs[2 * n_mm:]
        x_ref, g_ref, mod_ref = refs[:3]
        refs[-1][...] = yv.astype(BF16)
        r = lax.rsqrt(jnp.mean(yv * yv, axis=-1, keepdims=True) + EPS)
        out = x_ref[...] + (coef * mod_ref[0, i_gate:i_gate + 1, :]) * (yv * r * g_ref[...])
        if not with_loss:
            refs[3][...] = out
            return
        t_ref, dx_ref, loss_ref = refs[3:6]
        b, s = pl.program_id(0), pl.program_id(1)
        err = out - t_ref[...]
        dx_ref[...] = err * (1.0 / dx)
        part = (0.5 / dx) * jnp.sum(jnp.sum(err * err, axis=1, keepdims=True), axis=0, keepdims=True)
        first = jnp.logical_and(b == 0, s == 0)

        @pl.when(first)
        def _():
            loss_ref[...] = part

        @pl.when(jnp.logical_not(first))
        def _():
            loss_ref[...] += part

    row = _row_specs(dx, ts, ns)
    in_specs = [_row_specs(a.shape[1], ts, ns) for a, _ in pairs]
    in_specs += [pl.BlockSpec(b.shape, lambda b_, s_: (0, 0)) for _, b in pairs]
    in_specs += [row, _vec_spec(dx), _mod_spec()]
    args = [a for a, _ in pairs] + [b for _, b in pairs] + [x, g, mod]
    row_shape = jax.ShapeDtypeStruct((t, dx), F32)
    y0_shape = jax.ShapeDtypeStruct((t, dx), BF16)
    out_specs, out_shape = [row, row], [row_shape, y0_shape]
    if with_loss:
        in_specs.append(row)
        args.append(target)
        out_specs = [row, pl.BlockSpec((1, 1), lambda b, s: (0, 0)), row]
        out_shape = [row_shape, jax.ShapeDtypeStruct((1, 1), F32), y0_shape]
    return _pcall(
        body, side=side, name=name, grid=(nb, ns), in_specs=in_specs, out_specs=out_specs, out_shape=out_shape,
        compiler_params=_params(("arbitrary", "arbitrary")),
    )(*args)


def postnorm_bwd(dxo, y0, g, mod, i_gate, coef, nb, name, ts=1024):
    t, dx = y0.shape
    ts = min(ts, t // nb)
    ns = t // nb // ts

    def body(d_ref, y_ref, g_ref, mod_ref, dy_ref, dg_ref, dgate_ref):
        b, s = pl.program_id(0), pl.program_id(1)
        yv = y_ref[...].astype(F32)
        dv = d_ref[...]
        gv = g_ref[...]
        r = lax.rsqrt(jnp.mean(yv * yv, axis=-1, keepdims=True) + EPS)
        yhat = yv * r
        dgate = jnp.sum(dv * (coef * (yhat * gv)), axis=0, keepdims=True)[None]
        dyn = dv * (coef * mod_ref[0, i_gate:i_gate + 1, :])
        dg = jnp.sum(dyn * yhat, axis=0, keepdims=True)
        dyhat = dyn * gv
        dy_ref[...] = (r * (dyhat - yhat * jnp.mean(dyhat * yhat, axis=-1, keepdims=True))).astype(BF16)

        @pl.when(s == 0)
        def _():
            dgate_ref[...] = dgate

        @pl.when(s > 0)
        def _():
            dgate_ref[...] += dgate

        first = jnp.logical_and(b == 0, s == 0)

        @pl.when(first)
        def _():
            dg_ref[...] = dg

        @pl.when(jnp.logical_not(first))
        def _():
            dg_ref[...] += dg

    row = _row_specs(dx, ts, ns)
    return _pcall(
        body, name=name, grid=(nb, ns), in_specs=[row, row, _vec_spec(dx), _mod_spec()],
        out_specs=[row, _vec_spec(dx), pl.BlockSpec((1, 1, dx), lambda b, s: (b, 0, 0))],
        out_shape=[jax.ShapeDtypeStruct((t, dx), BF16), jax.ShapeDtypeStruct((1, dx), F32),
                   jax.ShapeDtypeStruct((nb, 1, dx), F32)],
        compiler_params=_params(("arbitrary", "arbitrary")),
    )(dxo, y0, g, mod)


def rope_tables(positions):
    inv_freq = ROPE_THETA ** (-jnp.arange(0, ROT_DIM, 2, dtype=F32) / ROT_DIM)
    ang = positions.astype(F32).reshape(-1, 1) * inv_freq
    cos, sin = jnp.cos(ang), jnp.sin(ang)
    half = ROT_DIM // 2
    z = lambda n: jnp.zeros((ang.shape[0], n), F32)
    c = jnp.concatenate([cos, cos, jnp.ones((ang.shape[0], HEAD_DIM - ROT_DIM), F32)], axis=1)
    sp = jnp.concatenate([z(half), sin, z(HEAD_DIM - ROT_DIM)], axis=1)
    sm = jnp.concatenate([-sin, z(HEAD_DIM - half)], axis=1)
    return tuple(jnp.tile(a, (1, HEADS_PER_STEP)) for a in (c, sp, sm))


def _scan_lanes(x, reverse):
    n = x.shape[-1]
    lane = lax.broadcasted_iota(jnp.int32, x.shape, x.ndim - 1)
    k = 1
    while k < n:
        if reverse:
            x = x + jnp.where(lane < n - k, pltpu.roll(x, n - k, x.ndim - 1), 0.0)
        else:
            x = x + jnp.where(lane >= k, pltpu.roll(x, k, x.ndim - 1), 0.0)
        k *= 2
    return x


def _log_sigmoid(z):
    return jnp.minimum(z, 0.0) - jnp.log(1.0 + jnp.exp(-jnp.abs(z)))


def fox_gate_fwd(ft, b_forget, name):
    nb, nh, s = ft.shape

    def body(f_ref, b_ref, o_ref):
        z = f_ref[0] + b_ref[...]
        o_ref[0] = -_scan_lanes(_log_sigmoid(z), False)

    spec = pl.BlockSpec((1, nh, s), lambda b: (b, 0, 0))
    return _pcall(
        body, name=name, grid=(nb,), in_specs=[spec, pl.BlockSpec((nh, 1), lambda b: (0, 0))], out_specs=spec,
        out_shape=jax.ShapeDtypeStruct((nb, nh, s), F32), compiler_params=_params(("arbitrary",)),
    )(ft, b_forget)


def fox_gate_bwd(dcb, drow, ft, b_forget, name):
    nb, nh, s = ft.shape

    def body(d_ref, r_ref, f_ref, b_ref, dz_ref, db_ref):
        b = pl.program_id(0)
        z = f_ref[0] + b_ref[...]
        dlf = _scan_lanes(r_ref[0] - d_ref[0], True)
        dz = dlf * _sigmoid(-z)
        dz_ref[0] = dz
        db = jnp.sum(dz, axis=1, keepdims=True)

        @pl.when(b == 0)
        def _():
            db_ref[...] = db

        @pl.when(b > 0)
        def _():
            db_ref[...] += db

    spec = pl.BlockSpec((1, nh, s), lambda b: (b, 0, 0))
    vec = pl.BlockSpec((nh, 1), lambda b: (0, 0))
    return _pcall(
        body, name=name, grid=(nb,), in_specs=[spec, spec, spec, vec], out_specs=[spec, vec],
        out_shape=[jax.ShapeDtypeStruct((nb, nh, s), F32), jax.ShapeDtypeStruct((nh, 1), F32)],
        compiler_params=_params(("arbitrary",)),
    )(dcb, drow, ft, b_forget)


ATTN_TQ = 512
ATTN_TK = 512
ONES_ROWS = 16


def _rows_to_cols(rows):
    tile = jnp.concatenate([jnp.broadcast_to(rw, (HEAD_DIM, rw.shape[1])) for rw in rows], axis=0)
    return tile.T


def _block_delta(s, tq, tk):
    off = jnp.arange(s // tk) - (tq // tk - 1)
    return off[:, None, None] * tk + jnp.arange(tq)[None, None, :] - jnp.arange(tk)[None, :, None]


def dilated_table(s, tq, tk):
    delta = _block_delta(s, tq, tk)
    count = jnp.zeros(delta.shape, F32)
    for window, dil in DILATED_PATTERNS:
        count = count + ((delta >= 0) & (delta <= window) & (delta % dil == 0)).astype(F32)
    return jnp.where(count > 0, jnp.log(jnp.maximum(count, 1.0)), NEG)


def causal_table(s, tq, tk):
    return jnp.where(_block_delta(s, tq, tk) >= 0, 0.0, NEG).astype(F32)


def attn_fwd(q_arr, q_off, k_arr, k_off, v_arr, v_off, table, colbias, nb, name, side=None, off_diag_bias=True):
    t = q_arr.shape[0]
    s = t // nb
    tk, tq = table.shape[1:]
    assert tq == tk, "the diagonal handling below is written for square tiles"
    nq, nk = s // tq, s // tk
    npairs = WIDTH_A // LANES
    use_cb = colbias is not None

    def body(*refs):
        refs = list(refs)
        q_ref, k_ref, v_ref, tab_ref = refs[:4]
        cb_ref = refs[4] if use_cb else None
        tail = refs[-(HEADS_PER_STEP + int(use_cb)):]
        acc_s = tail[:HEADS_PER_STEP]
        cbc_s = tail[-1] if use_cb else None
        o_ref, lse_ref, vt_s = refs[-3 - len(tail):-len(tail)]
        qi = pl.program_id(2)

        heads = [slice(h * HEAD_DIM, (h + 1) * HEAD_DIM) for h in range(HEADS_PER_STEP)]

        @pl.when(qi == 0)
        def _():
            for cblk in range(nk):
                vt = v_ref[cblk * tk:(cblk + 1) * tk, :].astype(F32).T.astype(BF16)
                for h, hs in enumerate(heads):
                    vt_s[cblk, h, 0:HEAD_DIM, :] = vt[hs, :]
                    vt_s[cblk, h, HEAD_DIM:, :] = jnp.ones((ONES_ROWS, tk), BF16)
                if use_cb:
                    cbc_s[cblk] = _rows_to_cols([cb_ref[0, h, cblk] for h in range(HEADS_PER_STEP)])

        qt_all = (q_ref[...].astype(F32) * ATTN_SCALE).T.astype(BF16)
        qts = [qt_all[hs, :] for hs in heads]
        for a in acc_s:
            a[...] = jnp.zeros_like(a)

        def tile(kb, tab, k0, klen, q0, carry):
            ks = pl.multiple_of(kb * tk + k0, klen)
            sts, out = [], []
            for h, hs in enumerate(heads):
                st = jnp.dot(k_ref[pl.ds(ks, klen), hs], qts[h][:, q0:], preferred_element_type=F32)
                if tab is not None:
                    st = st + tab
                if use_cb:
                    st = st + cbc_s[kb, k0:k0 + klen, h * HEAD_DIM:h * HEAD_DIM + 1]
                sts.append(st)
            m_old = [carry[h][:, q0:] for h in range(HEADS_PER_STEP)]
            m_new = [jnp.maximum(m_old[h], jnp.max(sts[h], axis=0, keepdims=True)) for h in range(HEADS_PER_STEP)]
            for h in range(HEADS_PER_STEP):
                pt = jnp.exp(sts[h] - m_new[h]).astype(BF16)
                acc_s[h][:, q0:] = (jnp.exp(m_old[h] - m_new[h]) * acc_s[h][:, q0:]
                                    + jnp.dot(vt_s[kb, h, :, k0:k0 + klen], pt, preferred_element_type=F32))
                out.append(m_new[h] if q0 == 0 else jnp.concatenate([carry[h][:, :q0], m_new[h]], axis=1))
            return tuple(out)

        fin = lax.fori_loop(0, qi, lambda kb, c: tile(kb, tab_ref[qi - kb] if off_diag_bias else None, 0, tk, 0, c),
                            tuple(jnp.full((1, tq), NEG, F32) for _ in heads))
        half = tk // 2
        fin = tile(qi, tab_ref[0, 0:half, :], 0, half, 0, fin)
        fin = tile(qi, tab_ref[0, half:, half:], half, half, half, fin)
        outs = []
        for h in range(HEADS_PER_STEP):
            l = acc_s[h][HEAD_DIM:HEAD_DIM + 1, :]
            outs.append(acc_s[h][0:HEAD_DIM, :] / l)
            lse_ref[0, h, 0] = fin[h] + jnp.log(l)
        o_ref[...] = jnp.concatenate(outs, axis=0).T

    def seq_spec(off):
        return pl.BlockSpec((s, LANES), lambda b, j, i: (b, off + j))

    in_specs = [pl.BlockSpec((tq, LANES), lambda b, j, i: (b * nq + i, q_off + j)), seq_spec(k_off), seq_spec(v_off),
                pl.BlockSpec(table.shape, lambda b, j, i: (0, 0, 0))]
    args = [q_arr, k_arr, v_arr, table]
    if use_cb:
        in_specs.append(pl.BlockSpec((1, HEADS_PER_STEP, nk, 1, tk), lambda b, j, i: (b, j, 0, 0, 0)))
        args.append(colbias)
    n_heads = npairs * HEADS_PER_STEP
    return _pcall(
        body, side=side, name=name, grid=(nb, npairs, nq), in_specs=in_specs,
        out_specs=[pl.BlockSpec((tq, LANES), lambda b, j, i: (b * nq + i, j)),
                   pl.BlockSpec((1, HEADS_PER_STEP, 1, 1, tq), lambda b, j, i: (b, j, i, 0, 0))],
        out_shape=[jax.ShapeDtypeStruct((t, npairs * LANES), F32), jax.ShapeDtypeStruct((nb, n_heads, nq, 1, tq), F32)],
        scratch_shapes=[pltpu.VMEM((nk, HEADS_PER_STEP, HEAD_DIM + ONES_ROWS, tk), BF16)]
        + [pltpu.VMEM((HEAD_DIM + ONES_ROWS, tq), F32)] * HEADS_PER_STEP
        + ([pltpu.VMEM((nk, tk, LANES), F32)] if use_cb else []),
        compiler_params=_params(("arbitrary", "arbitrary", "arbitrary")),
    )(*args)


def attn_bwd(q_arr, q_off, k_arr, k_off, v_arr, v_off, o_arr, lse_arr, do_arr, table, colbias, nb, name, side=None,
             rope_tabs=None, off_diag_bias=True):
    t = q_arr.shape[0]
    s = t // nb
    tk, tq = table.shape[1:]
    assert tq == tk, "the diagonal handling below is written for square tiles"
    nq, nk = s // tq, s // tk
    npairs = WIDTH_A // LANES
    use_cb = colbias is not None

    def body(*refs):
        refs = list(refs)
        q_ref, k_ref, v_ref, o_ref, lse_ref, do_ref, tab_ref = refs[:7]
        pos = 7
        cb_ref = None
        if use_cb:
            cb_ref = refs[pos]
            pos += 1
        rope_refs = None
        if rope_tabs is not None:
            rope_refs = refs[pos:pos + 3]
            pos += 3
        dq_ref, dk_ref, dv_ref = refs[pos:pos + 3]
        pos += 3
        dcb_ref = drow_ref = None
        if use_cb:
            dcb_ref, drow_ref = refs[pos:pos + 2]
            pos += 2
        kt_s, dkt_s, dvt_s = refs[pos:pos + 3]
        dqt_s = refs[pos + 3:pos + 3 + HEADS_PER_STEP]
        dcb_s, cbc_s = refs[pos + 3 + HEADS_PER_STEP:pos + 5 + HEADS_PER_STEP] if use_cb else (None, None)

        heads = [slice(h * HEAD_DIM, (h + 1) * HEAD_DIM) for h in range(HEADS_PER_STEP)]
        for cblk in range(nk):
            kt_s[cblk] = k_ref[cblk * tk:(cblk + 1) * tk, :].astype(F32).T.astype(BF16)
        dkt_s[...] = jnp.zeros_like(dkt_s)
        dvt_s[...] = jnp.zeros_like(dvt_s)
        if use_cb:
            dcb_s[...] = jnp.zeros_like(dcb_s)
            for cblk in range(nk):
                cbc_s[cblk] = _rows_to_cols([cb_ref[0, h, cblk] for h in range(HEADS_PER_STEP)])
        ones = jnp.ones((8, HEAD_DIM), BF16)

        def q_loop(qi, carry):
            qs = pl.multiple_of(qi * tq, tq)
            q_all = (q_ref[pl.ds(qs, tq), :].astype(F32) * ATTN_SCALE)
            do_all = do_ref[pl.ds(qs, tq), :]
            qt_all = q_all.T.astype(BF16)
            dot_all = do_all.T.astype(BF16)
            qt, dot, lse, dsum = [], [], [], []
            for h, hs in enumerate(heads):
                qt.append(qt_all[hs, :])
                dot.append(dot_all[hs, :])
                lse.append(lse_ref[0, h, qi])
                prod = do_all[:, hs] * o_ref[pl.ds(qs, tq), hs]
                hi = prod.astype(BF16)
                lo = (prod - hi.astype(F32)).astype(BF16)
                dsum.append((lax.dot_general(ones, hi, NT_DIMS, preferred_element_type=F32)
                             + lax.dot_general(ones, lo, NT_DIMS, preferred_element_type=F32))[0:1, :])
            for a in dqt_s:
                a[...] = jnp.zeros_like(a)

            def tile(kb, tab, k0, klen, q0, drow):
                ks = pl.multiple_of(kb * tk + k0, klen)
                keys = slice(k0, k0 + klen)
                sts, dpts, out = [], [], []
                for h, hs in enumerate(heads):
                    st = jnp.dot(k_ref[pl.ds(ks, klen), hs], qt[h][:, q0:], preferred_element_type=F32)
                    if tab is not None:
                        st = st + tab
                    if use_cb:
                        st = st + cbc_s[kb, keys, h * HEAD_DIM:h * HEAD_DIM + 1]
                    sts.append(st)
                    dpts.append(jnp.dot(v_ref[pl.ds(ks, klen), hs], dot[h][:, q0:], preferred_element_type=F32))
                for h, hs in enumerate(heads):
                    pt = jnp.exp(sts[h] - lse[h][:, q0:])
                    dst = pt * (dpts[h] - dsum[h][:, q0:])
                    dst_b = dst.astype(BF16)
                    dvt_s[h, kb, :, keys] += lax.dot_general(dot[h][:, q0:], pt.astype(BF16), NT_DIMS,
                                                             preferred_element_type=F32)
                    dkt_s[h, kb, :, keys] += lax.dot_general(qt[h][:, q0:], dst_b, NT_DIMS, preferred_element_type=F32)
                    dqt_s[h][:, q0:] += jnp.dot(kt_s[kb, hs, keys], dst_b, preferred_element_type=F32)
                    if use_cb:
                        dcb_s[h, pl.ds(ks, klen), :] += jnp.sum(dst, axis=1, keepdims=True)
                        dr = drow[h][:, q0:] + jnp.sum(dst, axis=0, keepdims=True)
                        out.append(dr if q0 == 0 else jnp.concatenate([drow[h][:, :q0], dr], axis=1))
                    else:
                        out.append(drow[h])
                return tuple(out)

            drow = lax.fori_loop(0, qi, lambda kb, c: tile(kb, tab_ref[qi - kb] if off_diag_bias else None, 0, tk, 0, c),
                                 tuple(jnp.zeros((1, tq), F32) for _ in heads))
            half = tk // 2
            drow = tile(qi, tab_ref[0, 0:half, :], 0, half, 0, drow)
            drow = tile(qi, tab_ref[0, half:, half:], half, half, half, drow)
            dq = (jnp.concatenate([a[...] for a in dqt_s], axis=0) * ATTN_SCALE).T
            if rope_refs is not None:
                dq = _rotate(dq, *[coef[pl.ds(qs, tq), :] for coef in rope_refs], True)
            dq_ref[pl.ds(qs, tq), :] = dq.astype(dq_ref.dtype)
            if use_cb:
                for h in range(HEADS_PER_STEP):
                    drow_ref[0, h, qi] = drow[h]
            return carry

        lax.fori_loop(0, nq, q_loop, 0)
        for cblk in range(nk):
            rows = slice(cblk * tk, (cblk + 1) * tk)
            dk = jnp.concatenate([dkt_s[h, cblk] for h in range(HEADS_PER_STEP)], axis=0).T
            if rope_refs is not None:
                dk = _rotate(dk, *[coef[rows, :] for coef in rope_refs], True)
            dk_ref[rows, :] = dk.astype(dk_ref.dtype)
            dv_ref[rows, :] = jnp.concatenate([dvt_s[h, cblk] for h in range(HEADS_PER_STEP)], axis=0).T.astype(dv_ref.dtype)
            if use_cb:
                for h in range(HEADS_PER_STEP):
                    dcb_ref[0, h, cblk] = jnp.broadcast_to(dcb_s[h, rows, :], (tk, LANES)).T[0:1, :]

    def seq_spec(off):
        return pl.BlockSpec((s, LANES), lambda b, j: (b, off + j))

    row_spec = pl.BlockSpec((1, HEADS_PER_STEP, nq, 1, tq), lambda b, j: (b, j, 0, 0, 0))
    in_specs = [seq_spec(q_off), seq_spec(k_off), seq_spec(v_off), seq_spec(0), row_spec, seq_spec(0),
                pl.BlockSpec(table.shape, lambda b, j: (0, 0, 0))]
    args = [q_arr, k_arr, v_arr, o_arr, lse_arr, do_arr, table]
    width = npairs * LANES
    out_specs = [seq_spec(0)] * 3
    out_shape = [jax.ShapeDtypeStruct((t, width), BF16)] * 3
    scratch = [pltpu.VMEM((nk, LANES, tk), BF16), pltpu.VMEM((HEADS_PER_STEP, nk, HEAD_DIM, tk), F32),
               pltpu.VMEM((HEADS_PER_STEP, nk, HEAD_DIM, tk), F32)] + [pltpu.VMEM((HEAD_DIM, tq), F32)] * HEADS_PER_STEP
    if use_cb:
        cb_spec = pl.BlockSpec((1, HEADS_PER_STEP, nk, 1, tk), lambda b, j: (b, j, 0, 0, 0))
        in_specs.append(cb_spec)
        args.append(colbias)
    if rope_tabs is not None:
        in_specs += [pl.BlockSpec((s, LANES), lambda b, j: (b, 0))] * 3
        args += list(rope_tabs)
    if use_cb:
        out_specs += [cb_spec, row_spec]
        out_shape += [jax.ShapeDtypeStruct(colbias.shape, F32), jax.ShapeDtypeStruct(lse_arr.shape, F32)]
        scratch += [pltpu.VMEM((HEADS_PER_STEP, s, 1), F32), pltpu.VMEM((nk, tk, LANES), F32)]
    return _pcall(
        body, side=side, name=name, grid=(nb, npairs), in_specs=in_specs, out_specs=out_specs, out_shape=out_shape,
        scratch_shapes=scratch, compiler_params=_params(("arbitrary", "arbitrary")),
    )(*args)


def ada_fwd(c_all, w_ada, b_cols, name):
    def body(c_ref, w_ref, b_ref, o_ref):
        cv = c_ref[...]
        sc = (cv * _sigmoid(cv)).astype(BF16)
        o_ref[...] = jnp.dot(sc, w_ref[...].astype(BF16), preferred_element_type=F32) + b_ref[...]

    return _pcall(body, name=name, out_shape=jax.ShapeDtypeStruct((c_all.shape[0], w_ada.shape[1]), F32),
                  compiler_params=_params())(c_all, w_ada, b_cols)


def ada_bwd(c_all, dmod_cols, name):
    def body(c_ref, d_ref, o_ref):
        cv = c_ref[...]
        sc = (cv * _sigmoid(cv)).astype(BF16)
        o_ref[...] = lax.dot_general(sc, d_ref[...].astype(BF16), TN_DIMS, preferred_element_type=F32)

    return _pcall(body, name=name, out_shape=jax.ShapeDtypeStruct((c_all.shape[1], dmod_cols.shape[1]), F32),
                  compiler_params=_params())(c_all, dmod_cols)


def adamw(parts, group, w, m, v, name, tr=None):
    n = parts.shape[0]
    r, c = w.shape
    tr = r if tr is None else tr
    c1 = 1.0 - ADAM_B1 ** ADAM_STEP
    c2 = 1.0 - ADAM_B2 ** ADAM_STEP

    def body(p_ref, w_ref, m_ref, v_ref, g_ref, d_ref, nm_ref, nv_ref):
        g = p_ref[0, 0].astype(F32)
        for i in range(1, n):
            g = g + p_ref[i, 0].astype(F32)
        wv = w_ref[...]
        nm = ADAM_B1 * m_ref[...] + (1.0 - ADAM_B1) * g
        nv = ADAM_B2 * v_ref[...] + (1.0 - ADAM_B2) * (g * g)
        g_ref[...] = g
        nm_ref[...] = nm
        nv_ref[...] = nv
        d_ref[...] = -ADAM_LR * ((nm / c1) / (jnp.sqrt(nv / c2) + ADAM_EPS) + ADAM_WD * wv)

    spec = pl.BlockSpec((tr, c), lambda i: (i, 0))
    shape = jax.ShapeDtypeStruct((r, c), F32)
    return _pcall(
        body, name=name, grid=(r // tr,),
        in_specs=[pl.BlockSpec((n, 1, tr, c), lambda i: (0, group, i, 0)), spec, spec, spec],
        out_specs=[spec] * 4, out_shape=[shape] * 4, compiler_params=_params(("arbitrary",)),
    )(parts, w, m, v)


def all_gather(arrs, name):
    n = len(arrs)
    hbm = pl.BlockSpec(memory_space=pl.ANY)

    def body(*refs):
        ins, outs = refs[:n], refs[n:2 * n]
        send_sems, recv_sems, local_sems = refs[2 * n:]
        x, y, c = _place()
        me, sibling = (x, y, c), (x, y, 1 - c)
        chips = [(1 - x, y), (x, 1 - y), (1 - x, 1 - y)]

        def copy(a, k, block, to, src=None):
            dst = outs[a].at[_slot(block)]
            return pltpu.make_async_remote_copy(
                src_ref=dst if src is None else src, dst_ref=dst, send_sem=send_sems.at[a * 7 + k],
                recv_sem=recv_sems.at[a * 7 + k], device_id=to, device_id_type=MESH)

        mine = [pltpu.make_async_copy(ins[a], outs[a].at[_slot(me)], local_sems.at[a]) for a in range(n)]
        for cp in mine:
            cp.start()
        first = []
        for a in range(n):
            first.append(copy(a, 0, me, sibling, src=ins[a]))
            first += [copy(a, 1 + j, me, (*chip, c), src=ins[a]) for j, chip in enumerate(chips)]
        for cp in first:
            cp.start()
        passed = []
        for a in range(n):
            for j, chip in enumerate(chips):
                copy(a, 1 + j, (*chip, c), me).wait_recv()
                cp = copy(a, 4 + j, (*chip, c), sibling)
                cp.start()
                passed.append(cp)
        for a in range(n):
            copy(a, 0, sibling, me).wait_recv()
            for j, chip in enumerate(chips):
                copy(a, 4 + j, (*chip, 1 - c), me).wait_recv()
        for cp in first + passed:
            cp.wait_send()
        for cp in mine:
            cp.wait()

    return _pcall(
        body, name=name, in_specs=[hbm] * n, out_specs=[hbm] * n,
        out_shape=[jax.ShapeDtypeStruct((N_DEV,) + a.shape, a.dtype) for a in arrs],
        scratch_shapes=[pltpu.SemaphoreType.DMA((7 * n,)), pltpu.SemaphoreType.DMA((7 * n,)),
                        pltpu.SemaphoreType.DMA((n,))],
        compiler_params=pltpu.CompilerParams(has_side_effects=True),
    )(*arrs)


def _t(w):
    return jnp.swapaxes(w, -1, -2)


def _rows_from_blocks(blocks, pad_to=None):
    full = blocks.reshape(-1, blocks.shape[2])
    if pad_to is not None and pad_to > full.shape[0]:
        full = jnp.pad(full, ((0, pad_to - full.shape[0]), (0, 0)))
    return full


def _rows_to_blocks(full, nrows):
    return full[:nrows].reshape(N_DEV, nrows // N_DEV, full.shape[1])


SMALL_ORDER = ("g_pre_ff1", "g_post_ff1", "g_pre_mix", "g_post_mix", "g_out_a", "g_out_b", "g_pre_ff2", "g_post_ff2",
               "b_forget")


def _pack_small(vals):
    rows = []
    for name in SMALL_ORDER:
        v = vals[name].reshape(1, -1)
        if v.shape[1] % LANES:
            v = jnp.pad(v, ((0, 0), (0, LANES - v.shape[1] % LANES)))
        rows.append(v)
    return jnp.concatenate(rows, axis=1)


def _unpack_small(row, sizes):
    out, pos = {}, 0
    for name in SMALL_ORDER:
        n = sizes[name]
        out[name] = row[:, pos:pos + n]
        pos += -(-n // LANES) * LANES
    return out


def _ffn_forward(x, mod, g_pre, g_post, wg, wu, wd, i0, nb, tag, target=None, side=None, side_down=None):
    h = prenorm_fwd(x, g_pre, mod, i0, i0 + 1, nb, f"{tag}_prenorm")
    res, side_out = ffn_up(h, wg, wu, f"{tag}_up", side=side), None
    if side is not None:
        res, side_out = res
    gate, up, act = res
    if callable(wd):
        wd = wd(side_out)
    res, side_down_out = postnorm_fwd(x, [(act, wd)], g_post, mod, i0 + 2, 0.5, nb, f"{tag}_down_postnorm",
                                      target=target, side=side_down), None
    if side_down is not None:
        res, side_down_out = res
    out, y0 = (res[0] if target is None else tuple(res[:2])), res[-1]
    return out, (x, h, gate, up, act, y0), wd, side_out, side_down_out


def _ffn_backward(dxo, saved, mod, g_pre, g_post, wg, wu, wd, i0, nb, tag, side=None, chain=False):
    x, h, gate, up, act, y0 = saved
    dy0, dg_post, dgate_mod = postnorm_bwd(dxo, y0, g_post, mod, i0 + 2, 0.5, nb, f"{tag}_postnorm_bwd")
    dwd = mm_tn(act, dy0, BF16, f"{tag}_dwd", rows=D_FF)
    res, side_out = ffn_down_bwd(dy0, wd, gate, up, f"{tag}_down_bwd", side=side), None
    if side is not None:
        res, side_out = res
    dgate, dup = res
    dh_pairs = [(dgate, wg), (dup, wu)]
    if chain:
        dwg, (dwd,) = mm_tn(dgate, h, BF16, f"{tag}_dwg", rows=D_FF, side=([_rows_to_blocks(dwd, D_FF)[:, None]], False))
        dwu, (dwg,) = mm_tn(dup, h, BF16, f"{tag}_dwu", rows=D_FF, side=([_rows_to_blocks(dwg, D_FF)[:, None]], False))
        (dx, dg_pre, dsc, dsh), (dwu,) = prenorm_bwd(dh_pairs, x, g_pre, mod, i0 + 1, dxo, nb, f"{tag}_dh_prenorm_bwd",
                                                     ts=DH_ROWS, side=([_rows_to_blocks(dwu, D_FF)[:, None]], False))
    else:
        dwg = mm_tn(dgate, h, BF16, f"{tag}_dwg", rows=D_FF)
        dwu = mm_tn(dup, h, BF16, f"{tag}_dwu", rows=D_FF)
        dx, dg_pre, dsc, dsh = prenorm_bwd(dh_pairs, x, g_pre, mod, i0 + 1, dxo, nb, f"{tag}_dh_prenorm_bwd", ts=DH_ROWS)
    return dx, dict(g_pre=dg_pre, g_post=dg_post, wg=dwg, wu=dwu, wd=dwd, mod=(dsh, dsc, dgate_mod)), side_out


def kernel(x, c, positions, w_ada, b_ada, g_pre_ff1, g_post_ff1, w_ff1_gate, w_ff1_up, w_ff1_down, g_pre_mix, g_post_mix, w_in, b_forget, g_out_a, g_out_b, w_out, g_pre_ff2, g_post_ff2, w_ff2_gate, w_ff2_up, w_ff2_down, loss_target, m_w_ada, m_b_ada, m_g_pre_ff1, m_g_post_ff1, m_w_ff1_gate, m_w_ff1_up, m_w_ff1_down, m_g_pre_mix, m_g_post_mix, m_w_in, m_b_forget, m_g_out_a, m_g_out_b, m_w_out, m_g_pre_ff2, m_g_post_ff2, m_w_ff2_gate, m_w_ff2_up, m_w_ff2_down, v_w_ada, v_b_ada, v_g_pre_ff1, v_g_post_ff1, v_w_ff1_gate, v_w_ff1_up, v_w_ff1_down, v_g_pre_mix, v_g_post_mix, v_w_in, v_b_forget, v_g_out_a, v_g_out_b, v_w_out, v_g_pre_ff2, v_g_post_ff2, v_w_ff2_gate, v_w_ff2_up, v_w_ff2_down):
    weights = dict(w_ada=w_ada, b_ada=b_ada, g_pre_ff1=g_pre_ff1, g_post_ff1=g_post_ff1, w_ff1_gate=w_ff1_gate,
                   w_ff1_up=w_ff1_up, w_ff1_down=w_ff1_down, g_pre_mix=g_pre_mix, g_post_mix=g_post_mix, w_in=w_in,
                   b_forget=b_forget, g_out_a=g_out_a, g_out_b=g_out_b, w_out=w_out, g_pre_ff2=g_pre_ff2,
                   g_post_ff2=g_post_ff2, w_ff2_gate=w_ff2_gate, w_ff2_up=w_ff2_up, w_ff2_down=w_ff2_down)
    mom_m = dict(w_ada=m_w_ada, b_ada=m_b_ada, g_pre_ff1=m_g_pre_ff1, g_post_ff1=m_g_post_ff1, w_ff1_gate=m_w_ff1_gate,
                 w_ff1_up=m_w_ff1_up, w_ff1_down=m_w_ff1_down, g_pre_mix=m_g_pre_mix, g_post_mix=m_g_post_mix,
                 w_in=m_w_in, b_forget=m_b_forget, g_out_a=m_g_out_a, g_out_b=m_g_out_b, w_out=m_w_out,
                 g_pre_ff2=m_g_pre_ff2, g_post_ff2=m_g_post_ff2, w_ff2_gate=m_w_ff2_gate, w_ff2_up=m_w_ff2_up,
                 w_ff2_down=m_w_ff2_down)
    mom_v = dict(w_ada=v_w_ada, b_ada=v_b_ada, g_pre_ff1=v_g_pre_ff1, g_post_ff1=v_g_post_ff1, w_ff1_gate=v_w_ff1_gate,
                 w_ff1_up=v_w_ff1_up, w_ff1_down=v_w_ff1_down, g_pre_mix=v_g_pre_mix, g_post_mix=v_g_post_mix,
                 w_in=v_w_in, b_forget=v_b_forget, g_out_a=v_g_out_a, g_out_b=v_g_out_b, w_out=v_w_out,
                 g_pre_ff2=v_g_pre_ff2, g_post_ff2=v_g_post_ff2, w_ff2_gate=v_w_ff2_gate, w_ff2_up=v_w_ff2_up,
                 w_ff2_down=v_w_ff2_down)
    order = list(weights)

    nb, s, d = x.shape
    t = nb * s
    me = _slot(_place())
    nbg = nb * N_DEV
    ada_cols = w_ada.shape[2]

    bf = lambda w: w[0].astype(BF16)
    bft = lambda w: _t(w)[0].astype(BF16)
    c_all, wg1, wu1 = all_gather([c, bft(w_ff1_gate), bft(w_ff1_up)], "gather_ff1")
    c_all = c_all.reshape(nbg, d)
    wg1, wu1 = (_rows_from_blocks(w, D_FF_PAD) for w in (wg1, wu1))

    b_cols = lax.dynamic_slice(b_ada, (0, me * ada_cols), (1, ada_cols))
    mod_cols = ada_fwd(c_all, w_ada[0], b_cols, "ada_fwd")
    (mod_all,) = all_gather([mod_cols], "gather_mod")
    mod = lax.dynamic_slice(mod_all, (0, me * nb, 0), (N_DEV, nb, ada_cols))
    mod = mod.transpose(1, 0, 2).reshape(nb, N_MOD, d)

    xf = x.reshape(t, d)
    target = loss_target.reshape(t, d)

    x1, saved1, wd1, (_, w_in_all, w_out_all) = _ffn_forward(
        xf, mod, g_pre_ff1, g_post_ff1, wg1, wu1, lambda got: _rows_from_blocks(got[0], D_FF_PAD), 0, nb, "ff1",
        side=([bf(w_ff1_down), bft(w_in), bf(w_out)], "two_level"))[:4]
    w_in_t = _rows_from_blocks(w_in_all)
    n_qkv = 3 * (WIDTH_A + WIDTH_B)
    w_qkv_t = w_in_t[:n_qkv]
    w_f_t = jnp.pad(w_in_t[n_qkv:], ((0, LANES - N_HEADS_B), (0, 0)))
    w_o = _rows_from_blocks(w_out_all)
    w_o_a, w_o_b = w_o[:WIDTH_A], w_o[WIDTH_A:]

    h2 = prenorm_fwd(x1, g_pre_mix, mod, 3, 4, nb, "mix_prenorm")
    tables = rope_tables(positions)
    proj = mm_rows([(h2, w_qkv_t)], True, BF16, "mix_proj", rope=(tables, 2 * WIDTH_A))
    f_logit = mm_rows([(h2, w_f_t)], True, F32, "mix_forget")
    tab_a = dilated_table(s, ATTN_TQ, ATTN_TK)
    tab_b = causal_table(s, ATTN_TQ, ATTN_TK)
    ft = f_logit[:, :N_HEADS_B].reshape(nb, s, N_HEADS_B).transpose(0, 2, 1)
    bf_col = b_forget.reshape(N_HEADS_B, 1)
    colbias = fox_gate_fwd(ft, bf_col, "fox_gate").reshape(nb, N_HEADS_B, s // ATTN_TK, 1, ATTN_TK)
    pa = WIDTH_A // LANES
    (o_a, lse_a), ff2_all = attn_fwd(
        proj, 0, proj, pa, proj, 2 * pa, tab_a, None, nb, "attn_a",
        side=([bft(w_ff2_gate), bft(w_ff2_up), bf(w_ff2_down)], "two_level"))
    wg2, wu2, wd2 = (_rows_from_blocks(w, D_FF_PAD) for w in ff2_all)
    o_b, lse_b = attn_fwd(proj, 3 * pa, proj, 4 * pa, proj, 5 * pa, tab_b, colbias, nb, "attn_b", off_diag_bias=False)
    m_a = prenorm_fwd(o_a, g_out_a, None, None, None, nb, "out_norm_a")
    m_b = prenorm_fwd(o_b, g_out_b, None, None, None, nb, "out_norm_b")
    x2, y0m = postnorm_fwd(x1, [(m_a, w_o_a), (m_b, w_o_b)], g_post_mix, mod, 5, 1.0, nb, "mix_out_postnorm")

    (dx3, loss_part), saved2 = _ffn_forward(x2, mod, g_pre_ff2, g_post_ff2, wg2, wu2, wd2, 6, nb, "ff2", target=target)[:2]
    loss = lax.psum(loss_part[0, 0], ("x", "y", "c"))

    dx2, gr2, _ = _ffn_backward(dx3, saved2, mod, g_pre_ff2, g_post_ff2, wg2, wu2, wd2, 6, nb, "ff2")
    ff2_blocks = [_rows_to_blocks(gr2[k], D_FF)[:, None] for k in ("wg", "wu", "wd")]

    dy0m, dg_post_mix, dgate_mix = postnorm_bwd(dx2, y0m, g_post_mix, mod, 5, 1.0, nb, "mix_postnorm_bwd")
    dw_o = mm_tn_stack([m_a, m_b], dy0m, [WIDTH_A, WIDTH_B], BF16, "mix_dwo")
    do_a, dg_out_a = prenorm_bwd([(dy0m, w_o_a.T)], o_a, g_out_a, None, None, None, nb, "out_norm_a_bwd")
    do_b, dg_out_b = prenorm_bwd([(dy0m, w_o_b.T)], o_b, g_out_b, None, None, None, nb, "out_norm_b_bwd")
    (dq_a, dk_a, dv_a), g_ff2 = attn_bwd(proj, 0, proj, pa, proj, 2 * pa, o_a, lse_a, do_a, tab_a, None, nb,
                                            "attn_a_bwd", side=(ff2_blocks, False), rope_tabs=tables)
    dq_b, dk_b, dv_b, dcb, drow = attn_bwd(proj, 3 * pa, proj, 4 * pa, proj, 5 * pa, o_b, lse_b, do_b, tab_b, colbias, nb,
                                           "attn_b_bwd", off_diag_bias=False)
    dz_t, db_forget = fox_gate_bwd(dcb.reshape(nb, N_HEADS_B, s), drow.reshape(nb, N_HEADS_B, s), ft, bf_col,
                                   "fox_gate_bwd")
    dz = jnp.pad(dz_t.transpose(0, 2, 1).reshape(t, N_HEADS_B), ((0, 0), (0, LANES - N_HEADS_B))).astype(BF16)
    pieces = [dq_a, dk_a, dv_a, dq_b, dk_b, dv_b]
    w_pieces = [w_qkv_t[i * WIDTH_A:(i + 1) * WIDTH_A] for i in range(6)]
    dh2_pairs = list(zip(pieces, w_pieces)) + [(dz, w_f_t)]
    dw_in_t = mm_tn_stack(pieces + [dz], h2, [WIDTH_A] * 6 + [N_HEADS_B], BF16, "mix_dwin")
    dx1, dg_pre_mix, dsc_mix, dsh_mix = prenorm_bwd(dh2_pairs, x1, g_pre_mix, mod, 4, dx2, nb, "mix_dh_prenorm_bwd",
                                                    ts=DH_ROWS)

    g_in = _rows_to_blocks(dw_in_t, dw_in_t.shape[0])[:, None]
    g_out = _rows_to_blocks(dw_o, d)[:, None]
    dx0, gr1, (g_in, g_out) = _ffn_backward(dx1, saved1, mod, g_pre_ff1, g_post_ff1, wg1, wu1, wd1, 0, nb, "ff1",
                                            side=([g_in, g_out], False), chain=True)
    grad_x = dx0.reshape(nb, s, d)

    dmod =jnp.concatenate(list(gr1["mod"]) + [dsh_mix, dsc_mix, dgate_mix] + list(gr2["mod"]), axis=1)
    small = _pack_small(dict(g_pre_ff1=gr1["g_pre"], g_post_ff1=gr1["g_post"], g_pre_mix=dg_pre_mix,
                             g_post_mix=dg_post_mix, g_out_a=dg_out_a, g_out_b=dg_out_b, g_pre_ff2=gr2["g_pre"],
                             g_post_ff2=gr2["g_post"], b_forget=db_forget))
    dmod_all, small_all = all_gather([dmod.reshape(nb, N_MOD * d), small], "gather_small_grads")
    dmod_all = dmod_all.reshape(nbg, N_MOD * d)

    res = {}
    def adamw_t(parts, group, n):
        return tuple(_t(r) for r in adamw(parts, group, _t(weights[n])[0], _t(mom_m[n])[0], _t(mom_v[n])[0], f"adamw_{n}"))

    res["w_ff1_gate"] = adamw_t(gr1["wg"], 0, "w_ff1_gate")
    res["w_ff1_up"] = adamw_t(gr1["wu"], 0, "w_ff1_up")
    res["w_ff2_gate"] = adamw_t(g_ff2[0], 0, "w_ff2_gate")
    res["w_ff2_up"] = adamw_t(g_ff2[1], 0, "w_ff2_up")
    res["w_ff1_down"] = adamw(gr1["wd"], 0, w_ff1_down[0], m_w_ff1_down[0], v_w_ff1_down[0], "adamw_ff1_down")
    res["w_ff2_down"] = adamw(g_ff2[2], 0, w_ff2_down[0], m_w_ff2_down[0], v_w_ff2_down[0], "adamw_ff2_down")
    res["w_in"] = adamw_t(g_in, 0, "w_in")
    res["w_out"] = adamw(g_out, 0, w_out[0], m_w_out[0], v_w_out[0], "adamw_out")
    dmod_cols = lax.dynamic_slice(dmod_all, (0, me * ada_cols), (nbg, ada_cols))
    dw_ada = ada_bwd(c_all, dmod_cols, "ada_bwd")
    res["w_ada"] = adamw(dw_ada[None, None], 0, w_ada[0], m_w_ada[0], v_w_ada[0], "adamw_ada", tr=256)
    res["b_ada"] = adamw(dmod_all[:, None, None], 0, b_ada, m_b_ada, v_b_ada, "adamw_b_ada")
    sizes = {n: weights[n].shape[1] for n in SMALL_ORDER}
    small_res = adamw(small_all[:, None], 0, _pack_small(weights), _pack_small(mom_m), _pack_small(mom_v), "adamw_small")
    small_res = [_unpack_small(r, sizes) for r in small_res]
    for n in SMALL_ORDER:
        res[n] = tuple(r[n] for r in small_res)

    outs = [loss, grad_x]
    for kind in range(4):
        for n in order:
            a = res[n][kind]
            outs.append(a.reshape(weights[n].shape))
    return tuple(outs)
```

```python
import functools

import jax
import jax.numpy as jnp
from jax import lax
from jax.experimental import pallas as pl
from jax.experimental.pallas import tpu as pltpu

F32 = jnp.float32
BF16 = jnp.bfloat16

D_MODEL = 1024
HEAD_DIM = 64
N_HEADS_A = 8
N_HEADS_B = 8
WIDTH_A = N_HEADS_A * HEAD_DIM
WIDTH_B = N_HEADS_B * HEAD_DIM
DILATED_PATTERNS = ((128, 1), (512, 4), (2048, 16))
ROT_DIM = HEAD_DIM // 4
ROPE_THETA = 500000.0
D_FF = 2752
D_FF_PAD = 2816
N_MOD = 9
EPS = 1e-6
ATTN_SCALE = HEAD_DIM ** -0.5
NEG = -1e30
N_DEV = 8
LANES = 128
HEADS_PER_STEP = LANES // HEAD_DIM

ADAM_LR = 0.001
ADAM_B1 = 0.9
ADAM_B2 = 0.999
ADAM_EPS = 1e-08
ADAM_WD = 0.01
ADAM_STEP = 10

VMEM_LIMIT = 56 * 1024 * 1024
MESH = pl.DeviceIdType.MESH

NT_DIMS = (((1,), (1,)), ((), ()))
TN_DIMS = (((0,), (0,)), ((), ()))
NN_DIMS = (((1,), (0,)), ((), ()))


def _place():
    return lax.axis_index("x"), lax.axis_index("y"), lax.axis_index("c")


def _slot(p):
    return 4 * p[0] + 2 * p[1] + p[2]


def _direct_copies(ins, outs, send_sems, recv_sems, local_sems, gather):
    x, y, c = _place()
    me = (x, y, c)
    flip = lambda v, bit: 1 - v if bit else v
    peers = [(flip(x, k & 4), flip(y, k & 2), flip(c, k & 1)) for k in range(1, N_DEV)]
    local, sends, recvs = [], [], []
    for a in range(len(ins)):
        mine = ins[a] if gather else ins[a].at[_slot(me)]
        local.append(pltpu.make_async_copy(mine, outs[a].at[_slot(me)], local_sems.at[a]))
        for k, peer in enumerate(peers):
            sems = dict(send_sem=send_sems.at[a * 7 + k], recv_sem=recv_sems.at[a * 7 + k], device_id=peer,
                        device_id_type=MESH)
            sends.append(pltpu.make_async_remote_copy(
                src_ref=ins[a] if gather else ins[a].at[_slot(peer)], dst_ref=outs[a].at[_slot(me)], **sems))
            recvs.append(pltpu.make_async_remote_copy(src_ref=mine, dst_ref=outs[a].at[_slot(peer)], **sems))
    return local, sends, recvs


def _two_level_copies(ins, outs, send_sems, recv_sems, local_sems):
    x, y, c = _place()
    me, sibling = (x, y, c), (x, y, 1 - c)
    chips = [(1 - x, y), (x, 1 - y), (1 - x, 1 - y)]
    local, first, landed, forwards, late = [], [], [], [], []
    for a in range(len(ins)):
        def copy(k, block, to, src=None, a=a):
            dst = outs[a].at[_slot(block)]
            return pltpu.make_async_remote_copy(
                src_ref=dst if src is None else src, dst_ref=dst, send_sem=send_sems.at[a * 7 + k],
                recv_sem=recv_sems.at[a * 7 + k], device_id=to, device_id_type=MESH)

        local.append(pltpu.make_async_copy(ins[a], outs[a].at[_slot(me)], local_sems.at[a]))
        first.append(copy(0, me, sibling, src=ins[a]))
        late.append(copy(0, sibling, me))
        for j, chip in enumerate(chips):
            first.append(copy(1 + j, me, (*chip, c), src=ins[a]))
            landed.append(copy(1 + j, (*chip, c), me))
            forwards.append(copy(4 + j, (*chip, c), sibling))
            late.append(copy(4 + j, (*chip, 1 - c), me))
    return local, first, landed, forwards, late


def _comm_scratch(n):
    return [pltpu.SemaphoreType.DMA((7 * n,)), pltpu.SemaphoreType.DMA((7 * n,)), pltpu.SemaphoreType.DMA((n,))]


def _pcall(body, side=None, **kw):
    if side is None:
        return pl.pallas_call(body, **kw)
    arrs, gather = side
    n = len(arrs)
    grid = kw["grid"]
    in_specs = list(kw["in_specs"])
    single = not isinstance(kw["out_specs"], (list, tuple))
    out_specs = [kw["out_specs"]] if single else list(kw["out_specs"])
    out_shape = [kw["out_shape"]] if single else list(kw["out_shape"])
    scratch = list(kw.get("scratch_shapes", []))
    n_in, n_out, n_scr = len(in_specs), len(out_specs), len(scratch)
    hbm = pl.BlockSpec(memory_space=pl.ANY)

    def hosted(*refs):
        pos = [0]

        def take(k):
            pos[0] += k
            return refs[pos[0] - k:pos[0]]

        ins, s_ins, outs, s_outs, scr, sems = take(n_in), take(n), take(n_out), take(n), take(n_scr), take(3)
        ids = [pl.program_id(i) for i in range(len(grid))]
        first = functools.reduce(jnp.logical_and, [i == 0 for i in ids])
        last = functools.reduce(jnp.logical_and, [i == g - 1 for i, g in zip(ids, grid)])
        if gather == "two_level":
            axis = max(range(len(grid)), key=lambda i: grid[i])
            assert grid[axis] >= 2
            middle = functools.reduce(jnp.logical_and, [i == ((3 * grid[axis]) // 4 if k == axis else 0)
                                                        for k, i in enumerate(ids)])

            @pl.when(first)
            def _():
                local, sends, _, _, _ = _two_level_copies(s_ins, s_outs, *sems)
                for cp in local + sends:
                    cp.start()

            @pl.when(middle)
            def _():
                _, _, landed, forwards, _ = _two_level_copies(s_ins, s_outs, *sems)
                for cp_in, cp_out in zip(landed, forwards):
                    cp_in.wait_recv()
                    cp_out.start()

            body(*ins, *outs, *scr)

            @pl.when(last)
            def _():
                local, sends, _, forwards, late = _two_level_copies(s_ins, s_outs, *sems)
                for cp in late:
                    cp.wait_recv()
                for cp in sends + forwards:
                    cp.wait_send()
                for cp in local:
                    cp.wait()
            return

        @pl.when(first)
        def _():
            local, sends, _ = _direct_copies(s_ins, s_outs, *sems, gather)
            for cp in local + sends:
                cp.start()

        body(*ins, *outs, *scr)

        @pl.when(last)
        def _():
            local, sends, recvs = _direct_copies(s_ins, s_outs, *sems, gather)
            for cp in recvs:
                cp.wait_recv()
            for cp in sends:
                cp.wait_send()
            for cp in local:
                cp.wait()

    kw.update(in_specs=in_specs + [hbm] * n, out_specs=out_specs + [hbm] * n,
              out_shape=out_shape + [jax.ShapeDtypeStruct(((N_DEV,) + a.shape) if gather else a.shape, a.dtype)
                                     for a in arrs],
              scratch_shapes=scratch + _comm_scratch(n))
    call = pl.pallas_call(hosted, **kw)

    def run(*args):
        res = call(*args, *arrs)
        main = res[0] if single else list(res[:n_out])
        return main, list(res[n_out:])

    return run


def _params(sem=None, **kw):
    if sem is not None:
        kw["dimension_semantics"] = sem
    return pltpu.CompilerParams(vmem_limit_bytes=VMEM_LIMIT, **kw)


def _rotate(xv, c, sp, sm, transpose):
    width = xv.shape[1]
    half = ROT_DIM // 2
    if transpose:
        return xv * c + pltpu.roll(xv * sp, width - half, 1) + pltpu.roll(xv * sm, half, 1)
    return xv * c + pltpu.roll(xv, half, 1) * sp + pltpu.roll(xv, width - half, 1) * sm


def mm_rows(pairs, trans_b, out_dtype, name, tm=512, side=None, rope=None):
    n = len(pairs)
    m = pairs[0][0].shape[0]
    n_out = pairs[0][1].shape[0 if trans_b else 1]
    dims = NT_DIMS if trans_b else NN_DIMS

    def body(*refs):
        o_ref = refs[-1]
        acc = None
        for a_ref, b_ref in zip(refs[:n], refs[n:2 * n]):
            d = lax.dot_general(a_ref[...], b_ref[...], dims, preferred_element_type=F32)
            acc = d if acc is None else acc + d
        if rope is None:
            o_ref[...] = acc.astype(o_ref.dtype)
        else:
            width = rope[1]
            c, sp, sm = (jnp.concatenate([r[...]] * (width // LANES), axis=1) for r in refs[2 * n:2 * n + 3])
            o_ref[:, :width] = _rotate(acc[:, :width], c, sp, sm, False).astype(o_ref.dtype)
            o_ref[:, width:] = acc[:, width:].astype(o_ref.dtype)

    in_specs = [pl.BlockSpec((tm, a.shape[1]), lambda i: (i, 0)) for a, _ in pairs]
    in_specs += [pl.BlockSpec(b.shape, lambda i: (0, 0)) for _, b in pairs]
    args = [a for a, _ in pairs] + [b for _, b in pairs]
    if rope is not None:
        in_specs += [pl.BlockSpec((tm, LANES), lambda i: (i, 0))] * 3
        args += list(rope[0])
    return _pcall(
        body, side=side, name=name, grid=(m // tm,), in_specs=in_specs,
        out_specs=pl.BlockSpec((tm, n_out), lambda i: (i, 0)),
        out_shape=jax.ShapeDtypeStruct((m, n_out), out_dtype),
        compiler_params=_params(("arbitrary",)),
    )(*args)


DH_ROWS = 256
TN_TOKENS = 2048
TN_OUT_ELEMS = 2 * 1024 * 1024


def mm_tn(a, b, out_dtype, name, side=None, rows=None):
    t, ka = a.shape
    n_out = b.shape[1]
    tk = min(TN_TOKENS, t)
    tka = ka // 2 if ka * n_out > TN_OUT_ELEMS else ka
    tn = n_out
    steps = t // tk

    def body(a_ref, b_ref, o_ref, acc_ref):
        k = pl.program_id(2)
        d = lax.dot_general(a_ref[...], b_ref[...], TN_DIMS, preferred_element_type=F32)

        @pl.when(k == 0)
        def _():
            acc_ref[...] = d

        @pl.when(k > 0)
        def _():
            acc_ref[...] += d

        @pl.when(k == steps - 1)
        def _():
            o_ref[...] = acc_ref[...].astype(o_ref.dtype)

    return _pcall(
        body, side=side, name=name, grid=(ka // tka, n_out // tn, steps),
        in_specs=[pl.BlockSpec((tk, tka), lambda i, j, k: (k, i)), pl.BlockSpec((tk, tn), lambda i, j, k: (k, j))],
        out_specs=pl.BlockSpec((tka, tn), lambda i, j, k: (i, j)),
        out_shape=jax.ShapeDtypeStruct((ka if rows is None else rows, n_out), out_dtype),
        scratch_shapes=[pltpu.VMEM((tka, tn), F32)],
        compiler_params=_params(("arbitrary", "arbitrary", "arbitrary")),
    )(a, b)


def mm_tn_stack(a_list, b, rows, out_dtype, name, tk=1024):
    t, n_out = b.shape
    tk = min(tk, t)
    steps = t // tk
    n = len(a_list)
    offs = [sum(rows[:i]) for i in range(n)]

    def body(*refs):
        a_refs, b_ref, o_ref, acc_refs = refs[:n], refs[n], refs[n + 1], refs[n + 2:]
        k = pl.program_id(0)
        bv = b_ref[...]
        for a_ref, acc_ref in zip(a_refs, acc_refs):
            d = lax.dot_general(a_ref[...], bv, TN_DIMS, preferred_element_type=F32)

            @pl.when(k == 0)
            def _(acc_ref=acc_ref, d=d):
                acc_ref[...] = d

            @pl.when(k > 0)
            def _(acc_ref=acc_ref, d=d):
                acc_ref[...] += d

        @pl.when(k == steps - 1)
        def _():
            for acc_ref, off, r in zip(acc_refs, offs, rows):
                o_ref[off:off + r, :] = acc_ref[0:r, :].astype(o_ref.dtype)

    return _pcall(
        body, name=name, grid=(steps,),
        in_specs=[pl.BlockSpec((tk, a.shape[1]), lambda k: (k, 0)) for a in a_list]
        + [pl.BlockSpec((tk, n_out), lambda k: (k, 0))],
        out_specs=pl.BlockSpec((sum(rows), n_out), lambda k: (0, 0)),
        out_shape=jax.ShapeDtypeStruct((sum(rows), n_out), out_dtype),
        scratch_shapes=[pltpu.VMEM((a.shape[1], n_out), F32) for a in a_list],
        compiler_params=_params(("arbitrary",)),
    )(*a_list, b)


def _col_chunks(width, chunk=512):
    return [slice(c, min(c + chunk, width)) for c in range(0, width, chunk)]


def _sigmoid(x):
    return 1.0 / (1.0 + jnp.exp(-x))


def ffn_up(h, wgt, wut, name, tm=256, tn=D_FF_PAD, side=None):
    t, d = h.shape
    fp = wgt.shape[0]

    def body(h_ref, wg_ref, wu_ref, g_ref, u_ref, a_ref):
        hv = h_ref[...]

        def finish(cols, g, u):
            g_ref[:, cols] = g.astype(BF16)
            u_ref[:, cols] = u.astype(BF16)
            a_ref[:, cols] = (g * _sigmoid(g) * u).astype(BF16)

        pending = None
        for cols in _col_chunks(tn):
            g = lax.dot_general(hv, wg_ref[cols, :], NT_DIMS, preferred_element_type=F32)
            u = lax.dot_general(hv, wu_ref[cols, :], NT_DIMS, preferred_element_type=F32)
            if pending is not None:
                finish(*pending)
            pending = (cols, g, u)
        finish(*pending)

    w_spec = pl.BlockSpec((tn, d), lambda j, i: (j, 0))
    o_spec = pl.BlockSpec((tm, tn), lambda j, i: (i, j))
    o_shape = jax.ShapeDtypeStruct((t, fp), BF16)
    return _pcall(
        body, side=side, name=name, grid=(fp // tn, t // tm),
        in_specs=[pl.BlockSpec((tm, d), lambda j, i: (i, 0)), w_spec, w_spec],
        out_specs=[o_spec, o_spec, o_spec], out_shape=[o_shape, o_shape, o_shape],
        compiler_params=_params(("arbitrary", "arbitrary")),
    )(h, wgt, wut)


def ffn_down_bwd(dy0, wd, gate, up, name, tm=256, tn=D_FF_PAD, side=None):
    t, d = dy0.shape
    fp = wd.shape[0]

    def body(dy_ref, wd_ref, g_ref, u_ref, dg_ref, du_ref):
        dyv = dy_ref[...]

        def finish(cols, dact):
            g = g_ref[:, cols].astype(F32)
            u = u_ref[:, cols].astype(F32)
            sg = _sigmoid(g)
            silu = g * sg
            du_ref[:, cols] = (dact * silu).astype(BF16)
            dg_ref[:, cols] = ((dact * u) * (sg + silu * (1.0 - sg))).astype(BF16)

        pending = None
        for cols in _col_chunks(tn):
            dact = lax.dot_general(dyv, wd_ref[cols, :], NT_DIMS, preferred_element_type=F32)
            if pending is not None:
                finish(*pending)
            pending = (cols, dact)
        finish(*pending)

    t_spec = pl.BlockSpec((tm, tn), lambda j, i: (i, j))
    o_shape = jax.ShapeDtypeStruct((t, fp), BF16)
    return _pcall(
        body, side=side, name=name, grid=(fp // tn, t // tm),
        in_specs=[pl.BlockSpec((tm, d), lambda j, i: (i, 0)), pl.BlockSpec((tn, d), lambda j, i: (j, 0)), t_spec, t_spec],
        out_specs=[t_spec, t_spec], out_shape=[o_shape, o_shape],
        compiler_params=_params(("arbitrary", "arbitrary")),
    )(dy0, wd, gate, up)


def _row_specs(dx, ts, ns):
    return pl.BlockSpec((ts, dx), lambda b, s: (b * ns + s, 0))


def _mod_spec():
    return pl.BlockSpec((1, N_MOD, D_MODEL), lambda b, s: (b, 0, 0))


def _vec_spec(dx):
    return pl.BlockSpec((1, dx), lambda b, s: (0, 0))


def prenorm_fwd(x, g, mod, i_shift, i_scale, nb, name, ts=1024):
    t, dx = x.shape
    ts = min(ts, t // nb)
    ns = t // nb // ts

    def body(*refs):
        if mod is None:
            x_ref, g_ref, h_ref = refs
        else:
            x_ref, g_ref, mod_ref, h_ref = refs
        xv = x_ref[...]
        r = lax.rsqrt(jnp.mean(xv * xv, axis=-1, keepdims=True) + EPS)
        h = xv * r * g_ref[...]
        if mod is not None:
            h = h * (1.0 + mod_ref[0, i_scale:i_scale + 1, :]) + mod_ref[0, i_shift:i_shift + 1, :]
        h_ref[...] = h.astype(BF16)

    in_specs = [_row_specs(dx, ts, ns), _vec_spec(dx)]
    args = [x, g]
    if mod is not None:
        in_specs.append(_mod_spec())
        args.append(mod)
    return _pcall(
        body, name=name, grid=(nb, ns), in_specs=in_specs, out_specs=_row_specs(dx, ts, ns),
        out_shape=jax.ShapeDtypeStruct((t, dx), BF16), compiler_params=_params(("arbitrary", "arbitrary")),
    )(*args)


def prenorm_bwd(dh, x, g, mod, i_scale, dres, nb, name, ts=512, side=None):
    t, dx = x.shape
    ts = min(ts, t // nb)
    ns = t // nb // ts
    has_mod = mod is not None
    has_res = dres is not None
    pairs = dh if isinstance(dh, list) else None
    n_mm = 0 if pairs is None else len(pairs)

    def body(*refs):
        refs = list(refs)
        if pairs is None:
            dhv = refs[0][...].astype(F32)
            refs = refs[1:]
        else:
            dhv = None
            for a_ref, b_ref in zip(refs[:n_mm], refs[n_mm:2 * n_mm]):
                d = jnp.dot(a_ref[...], b_ref[...], preferred_element_type=F32)
                dhv = d if dhv is None else dhv + d
            refs = refs[2 * n_mm:]
        x_ref, g_ref = refs[:2]
        pos = 2
        mod_ref = dres_ref = None
        if has_mod:
            mod_ref = refs[pos]
            pos += 1
        if has_res:
            dres_ref = refs[pos]
            pos += 1
        dx_ref, dg_ref = refs[pos], refs[pos + 1]
        b, s = pl.program_id(0), pl.program_id(1)
        xv = x_ref[...]
        gv = g_ref[...]
        r = lax.rsqrt(jnp.mean(xv * xv, axis=-1, keepdims=True) + EPS)
        xhat = xv * r
        dn = dhv
        if has_mod:
            dsc_ref, dsh_ref = refs[pos + 2], refs[pos + 3]
            dn = dhv * (1.0 + mod_ref[0, i_scale:i_scale + 1, :])
            dsc = jnp.sum(dhv * xhat * gv, axis=0, keepdims=True)[None]
            dsh = jnp.sum(dhv, axis=0, keepdims=True)[None]

            @pl.when(s == 0)
            def _():
                dsc_ref[...] = dsc
                dsh_ref[...] = dsh

            @pl.when(s > 0)
            def _():
                dsc_ref[...] += dsc
                dsh_ref[...] += dsh

        dg = jnp.sum(dn * xhat, axis=0, keepdims=True)
        first = jnp.logical_and(b == 0, s == 0)

        @pl.when(first)
        def _():
            dg_ref[...] = dg

        @pl.when(jnp.logical_not(first))
        def _():
            dg_ref[...] += dg

        dxhat = dn * gv
        dxv = r * (dxhat - xhat * jnp.mean(dxhat * xhat, axis=-1, keepdims=True))
        if has_res:
            dxv = dxv + dres_ref[...]
        dx_ref[...] = dxv

    row = _row_specs(dx, ts, ns)
    if pairs is None:
        in_specs, args = [row], [dh]
    else:
        in_specs = [_row_specs(a.shape[1], ts, ns) for a, _ in pairs]
        in_specs += [pl.BlockSpec(b.shape, lambda b_, s_: (0, 0)) for _, b in pairs]
        args = [a for a, _ in pairs] + [b for _, b in pairs]
    in_specs += [row, _vec_spec(dx)]
    args += [x, g]
    if has_mod:
        in_specs.append(_mod_spec())
        args.append(mod)
    if has_res:
        in_specs.append(row)
        args.append(dres)
    out_specs = [row, _vec_spec(dx)]
    out_shape = [jax.ShapeDtypeStruct((t, dx), F32), jax.ShapeDtypeStruct((1, dx), F32)]
    if has_mod:
        bspec = pl.BlockSpec((1, 1, dx), lambda b, s: (b, 0, 0))
        out_specs += [bspec, bspec]
        out_shape += [jax.ShapeDtypeStruct((nb, 1, dx), F32)] * 2
    return _pcall(
        body, side=side, name=name, grid=(nb, ns), in_specs=in_specs, out_specs=out_specs, out_shape=out_shape,
        compiler_params=_params(("arbitrary", "arbitrary")),
    )(*args)


def postnorm_fwd(x, pairs, g, mod, i_gate, coef, nb, name, target=None, ts=512, side=None):
    t, dx = x.shape
    with_loss = target is not None
    ts = min(ts, t // nb)
    ns = t // nb // ts
    n_mm = len(pairs)

    def body(*refs):
        yv = None
        for a_ref, b_ref in zip(refs[:n_mm], refs[n_mm:2 * n_mm]):
            d = jnp.dot(a_ref[...], b_ref[...], preferred_element_type=F32)
            yv = d if yv is None else yv + d
        refs = refs[2 * n_mm:]
        x_ref, g_ref, mod_ref = refs[:3]
        refs[-1][...] = yv.astype(BF16)
        r = lax.rsqrt(jnp.mean(yv * yv, axis=-1, keepdims=True) + EPS)
        out = x_ref[...] + (coef * mod_ref[0, i_gate:i_gate + 1, :]) * (yv * r * g_ref[...])
        if not with_loss:
            refs[3][...] = out
            return
        t_ref, dx_ref, loss_ref = refs[3:6]
        b, s = pl.program_id(0), pl.program_id(1)
        err = out - t_ref[...]
        dx_ref[...] = err * (1.0 / dx)
        part = (0.5 / dx) * jnp.sum(jnp.sum(err * err, axis=1, keepdims=True), axis=0, keepdims=True)
        first = jnp.logical_and(b == 0, s == 0)

        @pl.when(first)
        def _():
            loss_ref[...] = part

        @pl.when(jnp.logical_not(first))
        def _():
            loss_ref[...] += part

    row = _row_specs(dx, ts, ns)
    in_specs = [_row_specs(a.shape[1], ts, ns) for a, _ in pairs]
    in_specs += [pl.BlockSpec(b.shape, lambda b_, s_: (0, 0)) for _, b in pairs]
    in_specs += [row, _vec_spec(dx), _mod_spec()]
    args = [a for a, _ in pairs] + [b for _, b in pairs] + [x, g, mod]
    row_shape = jax.ShapeDtypeStruct((t, dx), F32)
    y0_shape = jax.ShapeDtypeStruct((t, dx), BF16)
    out_specs, out_shape = [row, row], [row_shape, y0_shape]
    if with_loss:
        in_specs.append(row)
        args.append(target)
        out_specs = [row, pl.BlockSpec((1, 1), lambda b, s: (0, 0)), row]
        out_shape = [row_shape, jax.ShapeDtypeStruct((1, 1), F32), y0_shape]
    return _pcall(
        body, side=side, name=name, grid=(nb, ns), in_specs=in_specs, out_specs=out_specs, out_shape=out_shape,
        compiler_params=_params(("arbitrary", "arbitrary")),
    )(*args)


def postnorm_bwd(dxo, y0, g, mod, i_gate, coef, nb, name, ts=1024):
    t, dx = y0.shape
    ts = min(ts, t // nb)
    ns = t // nb // ts

    def body(d_ref, y_ref, g_ref, mod_ref, dy_ref, dg_ref, dgate_ref):
        b, s = pl.program_id(0), pl.program_id(1)
        yv = y_ref[...].astype(F32)
        dv = d_ref[...]
        gv = g_ref[...]
        r = lax.rsqrt(jnp.mean(yv * yv, axis=-1, keepdims=True) + EPS)
        yhat = yv * r
        dgate = jnp.sum(dv * (coef * (yhat * gv)), axis=0, keepdims=True)[None]
        dyn = dv * (coef * mod_ref[0, i_gate:i_gate + 1, :])
        dg = jnp.sum(dyn * yhat, axis=0, keepdims=True)
        dyhat = dyn * gv
        dy_ref[...] = (r * (dyhat - yhat * jnp.mean(dyhat * yhat, axis=-1, keepdims=True))).astype(BF16)

        @pl.when(s == 0)
        def _():
            dgate_ref[...] = dgate

        @pl.when(s > 0)
        def _():
            dgate_ref[...] += dgate

        first = jnp.logical_and(b == 0, s == 0)

        @pl.when(first)
        def _():
            dg_ref[...] = dg

        @pl.when(jnp.logical_not(first))
        def _():
            dg_ref[...] += dg

    row = _row_specs(dx, ts, ns)
    return _pcall(
        body, name=name, grid=(nb, ns), in_specs=[row, row, _vec_spec(dx), _mod_spec()],
        out_specs=[row, _vec_spec(dx), pl.BlockSpec((1, 1, dx), lambda b, s: (b, 0, 0))],
        out_shape=[jax.ShapeDtypeStruct((t, dx), BF16), jax.ShapeDtypeStruct((1, dx), F32),
                   jax.ShapeDtypeStruct((nb, 1, dx), F32)],
        compiler_params=_params(("arbitrary", "arbitrary")),
    )(dxo, y0, g, mod)


def rope_tables(positions):
    inv_freq = ROPE_THETA ** (-jnp.arange(0, ROT_DIM, 2, dtype=F32) / ROT_DIM)
    ang = positions.astype(F32).reshape(-1, 1) * inv_freq
    cos, sin = jnp.cos(ang), jnp.sin(ang)
    half = ROT_DIM // 2
    z = lambda n: jnp.zeros((ang.shape[0], n), F32)
    c = jnp.concatenate([cos, cos, jnp.ones((ang.shape[0], HEAD_DIM - ROT_DIM), F32)], axis=1)
    sp = jnp.concatenate([z(half), sin, z(HEAD_DIM - ROT_DIM)], axis=1)
    sm = jnp.concatenate([-sin, z(HEAD_DIM - half)], axis=1)
    return tuple(jnp.tile(a, (1, HEADS_PER_STEP)) for a in (c, sp, sm))


def _scan_lanes(x, reverse):
    n = x.shape[-1]
    lane = lax.broadcasted_iota(jnp.int32, x.shape, x.ndim - 1)
    k = 1
    while k < n:
        if reverse:
            x = x + jnp.where(lane < n - k, pltpu.roll(x, n - k, x.ndim - 1), 0.0)
        else:
            x = x + jnp.where(lane >= k, pltpu.roll(x, k, x.ndim - 1), 0.0)
        k *= 2
    return x


def _log_sigmoid(z):
    return jnp.minimum(z, 0.0) - jnp.log(1.0 + jnp.exp(-jnp.abs(z)))


def fox_gate_fwd(ft, b_forget, name):
    nb, nh, s = ft.shape

    def body(f_ref, b_ref, o_ref):
        z = f_ref[0] + b_ref[...]
        o_ref[0] = -_scan_lanes(_log_sigmoid(z), False)

    spec = pl.BlockSpec((1, nh, s), lambda b: (b, 0, 0))
    return _pcall(
        body, name=name, grid=(nb,), in_specs=[spec, pl.BlockSpec((nh, 1), lambda b: (0, 0))], out_specs=spec,
        out_shape=jax.ShapeDtypeStruct((nb, nh, s), F32), compiler_params=_params(("arbitrary",)),
    )(ft, b_forget)


def fox_gate_bwd(dcb, drow, ft, b_forget, name):
    nb, nh, s = ft.shape

    def body(d_ref, r_ref, f_ref, b_ref, dz_ref, db_ref):
        b = pl.program_id(0)
        z = f_ref[0] + b_ref[...]
        dlf = _scan_lanes(r_ref[0] - d_ref[0], True)
        dz = dlf * _sigmoid(-z)
        dz_ref[0] = dz
        db = jnp.sum(dz, axis=1, keepdims=True)

        @pl.when(b == 0)
        def _():
            db_ref[...] = db

        @pl.when(b > 0)
        def _():
            db_ref[...] += db

    spec = pl.BlockSpec((1, nh, s), lambda b: (b, 0, 0))
    vec = pl.BlockSpec((nh, 1), lambda b: (0, 0))
    return _pcall(
        body, name=name, grid=(nb,), in_specs=[spec, spec, spec, vec], out_specs=[spec, vec],
        out_shape=[jax.ShapeDtypeStruct((nb, nh, s), F32), jax.ShapeDtypeStruct((nh, 1), F32)],
        compiler_params=_params(("arbitrary",)),
    )(dcb, drow, ft, b_forget)


ATTN_TQ = 512
ATTN_TK = 512
ONES_ROWS = 16


def _rows_to_cols(rows):
    tile = jnp.concatenate([jnp.broadcast_to(rw, (HEAD_DIM, rw.shape[1])) for rw in rows], axis=0)
    return tile.T


def _block_delta(s, tq, tk):
    off = jnp.arange(s // tk) - (tq // tk - 1)
    return off[:, None, None] * tk + jnp.arange(tq)[None, None, :] - jnp.arange(tk)[None, :, None]


def dilated_table(s, tq, tk):
    delta = _block_delta(s, tq, tk)
    count = jnp.zeros(delta.shape, F32)
    for window, dil in DILATED_PATTERNS:
        count = count + ((delta >= 0) & (delta <= window) & (delta % dil == 0)).astype(F32)
    return jnp.where(count > 0, jnp.log(jnp.maximum(count, 1.0)), NEG)


def causal_table(s, tq, tk):
    return jnp.where(_block_delta(s, tq, tk) >= 0, 0.0, NEG).astype(F32)


def attn_fwd(q_arr, q_off, k_arr, k_off, v_arr, v_off, table, colbias, nb, name, side=None, off_diag_bias=True):
    t = q_arr.shape[0]
    s = t // nb
    tk, tq = table.shape[1:]
    assert tq == tk, "the diagonal handling below is written for square tiles"
    nq, nk = s // tq, s // tk
    npairs = WIDTH_A // LANES
    use_cb = colbias is not None

    def body(*refs):
        refs = list(refs)
        q_ref, k_ref, v_ref, tab_ref = refs[:4]
        cb_ref = refs[4] if use_cb else None
        tail = refs[-(HEADS_PER_STEP + int(use_cb)):]
        acc_s = tail[:HEADS_PER_STEP]
        cbc_s = tail[-1] if use_cb else None
        o_ref, lse_ref, vt_s = refs[-3 - len(tail):-len(tail)]
        qi = pl.program_id(2)

        heads = [slice(h * HEAD_DIM, (h + 1) * HEAD_DIM) for h in range(HEADS_PER_STEP)]

        @pl.when(qi == 0)
        def _():
            for cblk in range(nk):
                vt = v_ref[cblk * tk:(cblk + 1) * tk, :].astype(F32).T.astype(BF16)
                for h, hs in enumerate(heads):
                    vt_s[cblk, h, 0:HEAD_DIM, :] = vt[hs, :]
                    vt_s[cblk, h, HEAD_DIM:, :] = jnp.ones((ONES_ROWS, tk), BF16)
                if use_cb:
                    cbc_s[cblk] = _rows_to_cols([cb_ref[0, h, cblk] for h in range(HEADS_PER_STEP)])

        qt_all = (q_ref[...].astype(F32) * ATTN_SCALE).T.astype(BF16)
        qts = [qt_all[hs, :] for hs in heads]
        for a in acc_s:
            a[...] = jnp.zeros_like(a)

        def tile(kb, tab, k0, klen, q0, carry):
            ks = pl.multiple_of(kb * tk + k0, klen)
            sts, out = [], []
            for h, hs in enumerate(heads):
                st = jnp.dot(k_ref[pl.ds(ks, klen), hs], qts[h][:, q0:], preferred_element_type=F32)
                if tab is not None:
                    st = st + tab
                if use_cb:
                    st = st + cbc_s[kb, k0:k0 + klen, h * HEAD_DIM:h * HEAD_DIM + 1]
                sts.append(st)
            m_old = [carry[h][:, q0:] for h in range(HEADS_PER_STEP)]
            m_new = [jnp.maximum(m_old[h], jnp.max(sts[h], axis=0, keepdims=True)) for h in range(HEADS_PER_STEP)]
            for h in range(HEADS_PER_STEP):
                pt = jnp.exp(sts[h] - m_new[h]).astype(BF16)
                acc_s[h][:, q0:] = (jnp.exp(m_old[h] - m_new[h]) * acc_s[h][:, q0:]
                                    + jnp.dot(vt_s[kb, h, :, k0:k0 + klen], pt, preferred_element_type=F32))
                out.append(m_new[h] if q0 == 0 else jnp.concatenate([carry[h][:, :q0], m_new[h]], axis=1))
            return tuple(out)

        fin = lax.fori_loop(0, qi, lambda kb, c: tile(kb, tab_ref[qi - kb] if off_diag_bias else None, 0, tk, 0, c),
                            tuple(jnp.full((1, tq), NEG, F32) for _ in heads))
        half = tk // 2
        fin = tile(qi, tab_ref[0, 0:half, :], 0, half, 0, fin)
        fin = tile(qi, tab_ref[0, half:, half:], half, half, half, fin)
        outs = []
        for h in range(HEADS_PER_STEP):
            l = acc_s[h][HEAD_DIM:HEAD_DIM + 1, :]
            outs.append(acc_s[h][0:HEAD_DIM, :] / l)
            lse_ref[0, h, 0] = fin[h] + jnp.log(l)
        o_ref[...] = jnp.concatenate(outs, axis=0).T

    def seq_spec(off):
        return pl.BlockSpec((s, LANES), lambda b, j, i: (b, off + j))

    in_specs = [pl.BlockSpec((tq, LANES), lambda b, j, i: (b * nq + i, q_off + j)), seq_spec(k_off), seq_spec(v_off),
                pl.BlockSpec(table.shape, lambda b, j, i: (0, 0, 0))]
    args = [q_arr, k_arr, v_arr, table]
    if use_cb:
        in_specs.append(pl.BlockSpec((1, HEADS_PER_STEP, nk, 1, tk), lambda b, j, i: (b, j, 0, 0, 0)))
        args.append(colbias)
    n_heads = npairs * HEADS_PER_STEP
    return _pcall(
        body, side=side, name=name, grid=(nb, npairs, nq), in_specs=in_specs,
        out_specs=[pl.BlockSpec((tq, LANES), lambda b, j, i: (b * nq + i, j)),
                   pl.BlockSpec((1, HEADS_PER_STEP, 1, 1, tq), lambda b, j, i: (b, j, i, 0, 0))],
        out_shape=[jax.ShapeDtypeStruct((t, npairs * LANES), F32), jax.ShapeDtypeStruct((nb, n_heads, nq, 1, tq), F32)],
        scratch_shapes=[pltpu.VMEM((nk, HEADS_PER_STEP, HEAD_DIM + ONES_ROWS, tk), BF16)]
        + [pltpu.VMEM((HEAD_DIM + ONES_ROWS, tq), F32)] * HEADS_PER_STEP
        + ([pltpu.VMEM((nk, tk, LANES), F32)] if use_cb else []),
        compiler_params=_params(("arbitrary", "arbitrary", "arbitrary")),
    )(*args)


def attn_bwd(q_arr, q_off, k_arr, k_off, v_arr, v_off, o_arr, lse_arr, do_arr, table, colbias, nb, name, side=None,
             rope_tabs=None, off_diag_bias=True):
    t = q_arr.shape[0]
    s = t // nb
    tk, tq = table.shape[1:]
    assert tq == tk, "the diagonal handling below is written for square tiles"
    nq, nk = s // tq, s // tk
    npairs = WIDTH_A // LANES
    use_cb = colbias is not None

    def body(*refs):
        refs = list(refs)
        q_ref, k_ref, v_ref, o_ref, lse_ref, do_ref, tab_ref = refs[:7]
        pos = 7
        cb_ref = None
        if use_cb:
            cb_ref = refs[pos]
            pos += 1
        rope_refs = None
        if rope_tabs is not None:
            rope_refs = refs[pos:pos + 3]
            pos += 3
        dq_ref, dk_ref, dv_ref = refs[pos:pos + 3]
        pos += 3
        dcb_ref = drow_ref = None
        if use_cb:
            dcb_ref, drow_ref = refs[pos:pos + 2]
            pos += 2
        kt_s, dkt_s, dvt_s = refs[pos:pos + 3]
        dqt_s = refs[pos + 3:pos + 3 + HEADS_PER_STEP]
        dcb_s, cbc_s = refs[pos + 3 + HEADS_PER_STEP:pos + 5 + HEADS_PER_STEP] if use_cb else (None, None)

        heads = [slice(h * HEAD_DIM, (h + 1) * HEAD_DIM) for h in range(HEADS_PER_STEP)]
        for cblk in range(nk):
            kt_s[cblk] = k_ref[cblk * tk:(cblk + 1) * tk, :].astype(F32).T.astype(BF16)
        dkt_s[...] = jnp.zeros_like(dkt_s)
        dvt_s[...] = jnp.zeros_like(dvt_s)
        if use_cb:
            dcb_s[...] = jnp.zeros_like(dcb_s)
            for cblk in range(nk):
                cbc_s[cblk] = _rows_to_cols([cb_ref[0, h, cblk] for h in range(HEADS_PER_STEP)])
        ones = jnp.ones((8, HEAD_DIM), BF16)

        def q_loop(qi, carry):
            qs = pl.multiple_of(qi * tq, tq)
            q_all = (q_ref[pl.ds(qs, tq), :].astype(F32) * ATTN_SCALE)
            do_all = do_ref[pl.ds(qs, tq), :]
            qt_all = q_all.T.astype(BF16)
            dot_all = do_all.T.astype(BF16)
            qt, dot, lse, dsum = [], [], [], []
            for h, hs in enumerate(heads):
                qt.append(qt_all[hs, :])
                dot.append(dot_all[hs, :])
                lse.append(lse_ref[0, h, qi])
                prod = do_all[:, hs] * o_ref[pl.ds(qs, tq), hs]
                hi = prod.astype(BF16)
                lo = (prod - hi.astype(F32)).astype(BF16)
                dsum.append((lax.dot_general(ones, hi, NT_DIMS, preferred_element_type=F32)
                             + lax.dot_general(ones, lo, NT_DIMS, preferred_element_type=F32))[0:1, :])
            for a in dqt_s:
                a[...] = jnp.zeros_like(a)

            def tile(kb, tab, k0, klen, q0, drow):
                ks = pl.multiple_of(kb * tk + k0, klen)
                keys = slice(k0, k0 + klen)
                sts, dpts, out = [], [], []
                for h, hs in enumerate(heads):
                    st = jnp.dot(k_ref[pl.ds(ks, klen), hs], qt[h][:, q0:], preferred_element_type=F32)
                    if tab is not None:
                        st = st + tab
                    if use_cb:
                        st = st + cbc_s[kb, keys, h * HEAD_DIM:h * HEAD_DIM + 1]
                    sts.append(st)
                    dpts.append(jnp.dot(v_ref[pl.ds(ks, klen), hs], dot[h][:, q0:], preferred_element_type=F32))
                for h, hs in enumerate(heads):
                    pt = jnp.exp(sts[h] - lse[h][:, q0:])
                    dst = pt * (dpts[h] - dsum[h][:, q0:])
                    dst_b = dst.astype(BF16)
                    dvt_s[h, kb, :, keys] += lax.dot_general(dot[h][:, q0:], pt.astype(BF16), NT_DIMS,
                                                             preferred_element_type=F32)
                    dkt_s[h, kb, :, keys] += lax.dot_general(qt[h][:, q0:], dst_b, NT_DIMS, preferred_element_type=F32)
                    dqt_s[h][:, q0:] += jnp.dot(kt_s[kb, hs, keys], dst_b, preferred_element_type=F32)
                    if use_cb:
                        dcb_s[h, pl.ds(ks, klen), :] += jnp.sum(dst, axis=1, keepdims=True)
                        dr = drow[h][:, q0:] + jnp.sum(dst, axis=0, keepdims=True)
                        out.append(dr if q0 == 0 else jnp.concatenate([drow[h][:, :q0], dr], axis=1))
                    else:
                        out.append(drow[h])
                return tuple(out)

            drow = lax.fori_loop(0, qi, lambda kb, c: tile(kb, tab_ref[qi - kb] if off_diag_bias else None, 0, tk, 0, c),
                                 tuple(jnp.zeros((1, tq), F32) for _ in heads))
            half = tk // 2
            drow = tile(qi, tab_ref[0, 0:half, :], 0, half, 0, drow)
            drow = tile(qi, tab_ref[0, half:, half:], half, half, half, drow)
            dq = (jnp.concatenate([a[...] for a in dqt_s], axis=0) * ATTN_SCALE).T
            if rope_refs is not None:
                dq = _rotate(dq, *[coef[pl.ds(qs, tq), :] for coef in rope_refs], True)
            dq_ref[pl.ds(qs, tq), :] = dq.astype(dq_ref.dtype)
            if use_cb:
                for h in range(HEADS_PER_STEP):
                    drow_ref[0, h, qi] = drow[h]
            return carry

        lax.fori_loop(0, nq, q_loop, 0)
        for cblk in range(nk):
            rows = slice(cblk * tk, (cblk + 1) * tk)
            dk = jnp.concatenate([dkt_s[h, cblk] for h in range(HEADS_PER_STEP)], axis=0).T
            if rope_refs is not None:
                dk = _rotate(dk, *[coef[rows, :] for coef in rope_refs], True)
            dk_ref[rows, :] = dk.astype(dk_ref.dtype)
            dv_ref[rows, :] = jnp.concatenate([dvt_s[h, cblk] for h in range(HEADS_PER_STEP)], axis=0).T.astype(dv_ref.dtype)
            if use_cb:
                for h in range(HEADS_PER_STEP):
                    dcb_ref[0, h, cblk] = jnp.broadcast_to(dcb_s[h, rows, :], (tk, LANES)).T[0:1, :]

    def seq_spec(off):
        return pl.BlockSpec((s, LANES), lambda b, j: (b, off + j))

    row_spec = pl.BlockSpec((1, HEADS_PER_STEP, nq, 1, tq), lambda b, j: (b, j, 0, 0, 0))
    in_specs = [seq_spec(q_off), seq_spec(k_off), seq_spec(v_off), seq_spec(0), row_spec, seq_spec(0),
                pl.BlockSpec(table.shape, lambda b, j: (0, 0, 0))]
    args = [q_arr, k_arr, v_arr, o_arr, lse_arr, do_arr, table]
    width = npairs * LANES
    out_specs = [seq_spec(0)] * 3
    out_shape = [jax.ShapeDtypeStruct((t, width), BF16)] * 3
    scratch = [pltpu.VMEM((nk, LANES, tk), BF16), pltpu.VMEM((HEADS_PER_STEP, nk, HEAD_DIM, tk), F32),
               pltpu.VMEM((HEADS_PER_STEP, nk, HEAD_DIM, tk), F32)] + [pltpu.VMEM((HEAD_DIM, tq), F32)] * HEADS_PER_STEP
    if use_cb:
        cb_spec = pl.BlockSpec((1, HEADS_PER_STEP, nk, 1, tk), lambda b, j: (b, j, 0, 0, 0))
        in_specs.append(cb_spec)
        args.append(colbias)
    if rope_tabs is not None:
        in_specs += [pl.BlockSpec((s, LANES), lambda b, j: (b, 0))] * 3
        args += list(rope_tabs)
    if use_cb:
        out_specs += [cb_spec, row_spec]
        out_shape += [jax.ShapeDtypeStruct(colbias.shape, F32), jax.ShapeDtypeStruct(lse_arr.shape, F32)]
        scratch += [pltpu.VMEM((HEADS_PER_STEP, s, 1), F32), pltpu.VMEM((nk, tk, LANES), F32)]
    return _pcall(
        body, side=side, name=name, grid=(nb, npairs), in_specs=in_specs, out_specs=out_specs, out_shape=out_shape,
        scratch_shapes=scratch, compiler_params=_params(("arbitrary", "arbitrary")),
    )(*args)


def ada_fwd(c_all, w_ada, b_cols, name):
    def body(c_ref, w_ref, b_ref, o_ref):
        cv = c_ref[...]
        sc = (cv * _sigmoid(cv)).astype(BF16)
        o_ref[...] = jnp.dot(sc, w_ref[...].astype(BF16), preferred_element_type=F32) + b_ref[...]

    return _pcall(body, name=name, out_shape=jax.ShapeDtypeStruct((c_all.shape[0], w_ada.shape[1]), F32),
                  compiler_params=_params())(c_all, w_ada, b_cols)


def ada_bwd(c_all, dmod_cols, name):
    def body(c_ref, d_ref, o_ref):
        cv = c_ref[...]
        sc = (cv * _sigmoid(cv)).astype(BF16)
        o_ref[...] = lax.dot_general(sc, d_ref[...].astype(BF16), TN_DIMS, preferred_element_type=F32)

    return _pcall(body, name=name, out_shape=jax.ShapeDtypeStruct((c_all.shape[1], dmod_cols.shape[1]), F32),
                  compiler_params=_params())(c_all, dmod_cols)


def adamw(parts, group, w, m, v, name, tr=None):
    n = parts.shape[0]
    r, c = w.shape
    tr = r if tr is None else tr
    c1 = 1.0 - ADAM_B1 ** ADAM_STEP
    c2 = 1.0 - ADAM_B2 ** ADAM_STEP

    def body(p_ref, w_ref, m_ref, v_ref, g_ref, d_ref, nm_ref, nv_ref):
        g = p_ref[0, 0].astype(F32)
        for i in range(1, n):
            g = g + p_ref[i, 0].astype(F32)
        wv = w_ref[...]
        nm = ADAM_B1 * m_ref[...] + (1.0 - ADAM_B1) * g
        nv = ADAM_B2 * v_ref[...] + (1.0 - ADAM_B2) * (g * g)
        g_ref[...] = g
        nm_ref[...] = nm
        nv_ref[...] = nv
        d_ref[...] = -ADAM_LR * ((nm / c1) / (jnp.sqrt(nv / c2) + ADAM_EPS) + ADAM_WD * wv)

    spec = pl.BlockSpec((tr, c), lambda i: (i, 0))
    shape = jax.ShapeDtypeStruct((r, c), F32)
    return _pcall(
        body, name=name, grid=(r // tr,),
        in_specs=[pl.BlockSpec((n, 1, tr, c), lambda i: (0, group, i, 0)), spec, spec, spec],
        out_specs=[spec] * 4, out_shape=[shape] * 4, compiler_params=_params(("arbitrary",)),
    )(parts, w, m, v)


def all_gather(arrs, name):
    n = len(arrs)
    hbm = pl.BlockSpec(memory_space=pl.ANY)

    def body(*refs):
        ins, outs = refs[:n], refs[n:2 * n]
        send_sems, recv_sems, local_sems = refs[2 * n:]
        x, y, c = _place()
        me, sibling = (x, y, c), (x, y, 1 - c)
        chips = [(1 - x, y), (x, 1 - y), (1 - x, 1 - y)]

        def copy(a, k, block, to, src=None):
            dst = outs[a].at[_slot(block)]
            return pltpu.make_async_remote_copy(
                src_ref=dst if src is None else src, dst_ref=dst, send_sem=send_sems.at[a * 7 + k],
                recv_sem=recv_sems.at[a * 7 + k], device_id=to, device_id_type=MESH)

        mine = [pltpu.make_async_copy(ins[a], outs[a].at[_slot(me)], local_sems.at[a]) for a in range(n)]
        for cp in mine:
            cp.start()
        first = []
        for a in range(n):
            first.append(copy(a, 0, me, sibling, src=ins[a]))
            first += [copy(a, 1 + j, me, (*chip, c), src=ins[a]) for j, chip in enumerate(chips)]
        for cp in first:
            cp.start()
        passed = []
        for a in range(n):
            for j, chip in enumerate(chips):
                copy(a, 1 + j, (*chip, c), me).wait_recv()
                cp = copy(a, 4 + j, (*chip, c), sibling)
                cp.start()
                passed.append(cp)
        for a in range(n):
            copy(a, 0, sibling, me).wait_recv()
            for j, chip in enumerate(chips):
                copy(a, 4 + j, (*chip, 1 - c), me).wait_recv()
        for cp in first + passed:
            cp.wait_send()
        for cp in mine:
            cp.wait()

    return _pcall(
        body, name=name, in_specs=[hbm] * n, out_specs=[hbm] * n,
        out_shape=[jax.ShapeDtypeStruct((N_DEV,) + a.shape, a.dtype) for a in arrs],
        scratch_shapes=[pltpu.SemaphoreType.DMA((7 * n,)), pltpu.SemaphoreType.DMA((7 * n,)),
                        pltpu.SemaphoreType.DMA((n,))],
        compiler_params=pltpu.CompilerParams(has_side_effects=True),
    )(*arrs)


def _t(w):
    return jnp.swapaxes(w, -1, -2)


def _rows_from_blocks(blocks, pad_to=None):
    full = blocks.reshape(-1, blocks.shape[2])
    if pad_to is not None and pad_to > full.shape[0]:
        full = jnp.pad(full, ((0, pad_to - full.shape[0]), (0, 0)))
    return full


def _rows_to_blocks(full, nrows):
    return full[:nrows].reshape(N_DEV, nrows // N_DEV, full.shape[1])


SMALL_ORDER = ("g_pre_ff1", "g_post_ff1", "g_pre_mix", "g_post_mix", "g_out_a", "g_out_b", "g_pre_ff2", "g_post_ff2",
               "b_forget")


def _pack_small(vals):
    rows = []
    for name in SMALL_ORDER:
        v = vals[name].reshape(1, -1)
        if v.shape[1] % LANES:
            v = jnp.pad(v, ((0, 0), (0, LANES - v.shape[1] % LANES)))
        rows.append(v)
    return jnp.concatenate(rows, axis=1)


def _unpack_small(row, sizes):
    out, pos = {}, 0
    for name in SMALL_ORDER:
        n = sizes[name]
        out[name] = row[:, pos:pos + n]
        pos += -(-n // LANES) * LANES
    return out


def _ffn_forward(x, mod, g_pre, g_post, wg, wu, wd, i0, nb, tag, target=None, side=None, side_down=None):
    h = prenorm_fwd(x, g_pre, mod, i0, i0 + 1, nb, f"{tag}_prenorm")
    res, side_out = ffn_up(h, wg, wu, f"{tag}_up", side=side), None
    if side is not None:
        res, side_out = res
    gate, up, act = res
    if callable(wd):
        wd = wd(side_out)
    res, side_down_out = postnorm_fwd(x, [(act, wd)], g_post, mod, i0 + 2, 0.5, nb, f"{tag}_down_postnorm",
                                      target=target, side=side_down), None
    if side_down is not None:
        res, side_down_out = res
    out, y0 = (res[0] if target is None else tuple(res[:2])), res[-1]
    return out, (x, h, gate, up, act, y0), wd, side_out, side_down_out


def _ffn_backward(dxo, saved, mod, g_pre, g_post, wg, wu, wd, i0, nb, tag, side=None, chain=False):
    x, h, gate, up, act, y0 = saved
    dy0, dg_post, dgate_mod = postnorm_bwd(dxo, y0, g_post, mod, i0 + 2, 0.5, nb, f"{tag}_postnorm_bwd")
    dwd = mm_tn(act, dy0, BF16, f"{tag}_dwd", rows=D_FF)
    res, side_out = ffn_down_bwd(dy0, wd, gate, up, f"{tag}_down_bwd", side=side), None
    if side is not None:
        res, side_out = res
    dgate, dup = res
    dh_pairs = [(dgate, wg), (dup, wu)]
    if chain:
        dwg, (dwd,) = mm_tn(dgate, h, BF16, f"{tag}_dwg", rows=D_FF, side=([_rows_to_blocks(dwd, D_FF)[:, None]], False))
        dwu, (dwg,) = mm_tn(dup, h, BF16, f"{tag}_dwu", rows=D_FF, side=([_rows_to_blocks(dwg, D_FF)[:, None]], False))
        (dx, dg_pre, dsc, dsh), (dwu,) = prenorm_bwd(dh_pairs, x, g_pre, mod, i0 + 1, dxo, nb, f"{tag}_dh_prenorm_bwd",
                                                     ts=DH_ROWS, side=([_rows_to_blocks(dwu, D_FF)[:, None]], False))
    else:
        dwg = mm_tn(dgate, h, BF16, f"{tag}_dwg", rows=D_FF)
        dwu = mm_tn(dup, h, BF16, f"{tag}_dwu", rows=D_FF)
        dx, dg_pre, dsc, dsh = prenorm_bwd(dh_pairs, x, g_pre, mod, i0 + 1, dxo, nb, f"{tag}_dh_prenorm_bwd", ts=DH_ROWS)
    return dx, dict(g_pre=dg_pre, g_post=dg_post, wg=dwg, wu=dwu, wd=dwd, mod=(dsh, dsc, dgate_mod)), side_out


def kernel(x, c, positions, w_ada, b_ada, g_pre_ff1, g_post_ff1, w_ff1_gate, w_ff1_up, w_ff1_down, g_pre_mix, g_post_mix, w_in, b_forget, g_out_a, g_out_b, w_out, g_pre_ff2, g_post_ff2, w_ff2_gate, w_ff2_up, w_ff2_down, loss_target, m_w_ada, m_b_ada, m_g_pre_ff1, m_g_post_ff1, m_w_ff1_gate, m_w_ff1_up, m_w_ff1_down, m_g_pre_mix, m_g_post_mix, m_w_in, m_b_forget, m_g_out_a, m_g_out_b, m_w_out, m_g_pre_ff2, m_g_post_ff2, m_w_ff2_gate, m_w_ff2_up, m_w_ff2_down, v_w_ada, v_b_ada, v_g_pre_ff1, v_g_post_ff1, v_w_ff1_gate, v_w_ff1_up, v_w_ff1_down, v_g_pre_mix, v_g_post_mix, v_w_in, v_b_forget, v_g_out_a, v_g_out_b, v_w_out, v_g_pre_ff2, v_g_post_ff2, v_w_ff2_gate, v_w_ff2_up, v_w_ff2_down):
    weights = dict(w_ada=w_ada, b_ada=b_ada, g_pre_ff1=g_pre_ff1, g_post_ff1=g_post_ff1, w_ff1_gate=w_ff1_gate,
                   w_ff1_up=w_ff1_up, w_ff1_down=w_ff1_down, g_pre_mix=g_pre_mix, g_post_mix=g_post_mix, w_in=w_in,
                   b_forget=b_forget, g_out_a=g_out_a, g_out_b=g_out_b, w_out=w_out, g_pre_ff2=g_pre_ff2,
                   g_post_ff2=g_post_ff2, w_ff2_gate=w_ff2_gate, w_ff2_up=w_ff2_up, w_ff2_down=w_ff2_down)
    mom_m = dict(w_ada=m_w_ada, b_ada=m_b_ada, g_pre_ff1=m_g_pre_ff1, g_post_ff1=m_g_post_ff1, w_ff1_gate=m_w_ff1_gate,
                 w_ff1_up=m_w_ff1_up, w_ff1_down=m_w_ff1_down, g_pre_mix=m_g_pre_mix, g_post_mix=m_g_post_mix,
                 w_in=m_w_in, b_forget=m_b_forget, g_out_a=m_g_out_a, g_out_b=m_g_out_b, w_out=m_w_out,
                 g_pre_ff2=m_g_pre_ff2, g_post_ff2=m_g_post_ff2, w_ff2_gate=m_w_ff2_gate, w_ff2_up=m_w_ff2_up,
                 w_ff2_down=m_w_ff2_down)
    mom_v = dict(w_ada=v_w_ada, b_ada=v_b_ada, g_pre_ff1=v_g_pre_ff1, g_post_ff1=v_g_post_ff1, w_ff1_gate=v_w_ff1_gate,
                 w_ff1_up=v_w_ff1_up, w_ff1_down=v_w_ff1_down, g_pre_mix=v_g_pre_mix, g_post_mix=v_g_post_mix,
                 w_in=v_w_in, b_forget=v_b_forget, g_out_a=v_g_out_a, g_out_b=v_g_out_b, w_out=v_w_out,
                 g_pre_ff2=v_g_pre_ff2, g_post_ff2=v_g_post_ff2, w_ff2_gate=v_w_ff2_gate, w_ff2_up=v_w_ff2_up,
                 w_ff2_down=v_w_ff2_down)
    order = list(weights)

    nb, s, d = x.shape
    t = nb * s
    me = _slot(_place())
    nbg = nb * N_DEV
    ada_cols = w_ada.shape[2]

    bf = lambda w: w[0].astype(BF16)
    bft = lambda w: _t(w)[0].astype(BF16)
    c_all, wg1, wu1 = all_gather([c, bft(w_ff1_gate), bft(w_ff1_up)], "gather_ff1")
    c_all = c_all.reshape(nbg, d)
    wg1, wu1 = (_rows_from_blocks(w, D_FF_PAD) for w in (wg1, wu1))

    b_cols = lax.dynamic_slice(b_ada, (0, me * ada_cols), (1, ada_cols))
    mod_cols = ada_fwd(c_all, w_ada[0], b_cols, "ada_fwd")
    (mod_all,) = all_gather([mod_cols], "gather_mod")
    mod = lax.dynamic_slice(mod_all, (0, me * nb, 0), (N_DEV, nb, ada_cols))
    mod = mod.transpose(1, 0, 2).reshape(nb, N_MOD, d)

    xf = x.reshape(t, d)
    target = loss_target.reshape(t, d)

    x1, saved1, wd1, (_, w_in_all, w_out_all) = _ffn_forward(
        xf, mod, g_pre_ff1, g_post_ff1, wg1, wu1, lambda got: _rows_from_blocks(got[0], D_FF_PAD), 0, nb, "ff1",
        side=([bf(w_ff1_down), bft(w_in), bf(w_out)], "two_level"))[:4]
    w_in_t = _rows_from_blocks(w_in_all)
    n_qkv = 3 * (WIDTH_A + WIDTH_B)
    w_qkv_t = w_in_t[:n_qkv]
    w_f_t = jnp.pad(w_in_t[n_qkv:], ((0, LANES - N_HEADS_B), (0, 0)))
    w_o = _rows_from_blocks(w_out_all)
    w_o_a, w_o_b = w_o[:WIDTH_A], w_o[WIDTH_A:]

    h2 = prenorm_fwd(x1, g_pre_mix, mod, 3, 4, nb, "mix_prenorm")
    tables = rope_tables(positions)
    proj = mm_rows([(h2, w_qkv_t)], True, BF16, "mix_proj", rope=(tables, 2 * WIDTH_A))
    f_logit = mm_rows([(h2, w_f_t)], True, F32, "mix_forget")
    tab_a = dilated_table(s, ATTN_TQ, ATTN_TK)
    tab_b = causal_table(s, ATTN_TQ, ATTN_TK)
    ft = f_logit[:, :N_HEADS_B].reshape(nb, s, N_HEADS_B).transpose(0, 2, 1)
    bf_col = b_forget.reshape(N_HEADS_B, 1)
    colbias = fox_gate_fwd(ft, bf_col, "fox_gate").reshape(nb, N_HEADS_B, s // ATTN_TK, 1, ATTN_TK)
    pa = WIDTH_A // LANES
    (o_a, lse_a), ff2_all = attn_fwd(
        proj, 0, proj, pa, proj, 2 * pa, tab_a, None, nb, "attn_a",
        side=([bft(w_ff2_gate), bft(w_ff2_up), bf(w_ff2_down)], "two_level"))
    wg2, wu2, wd2 = (_rows_from_blocks(w, D_FF_PAD) for w in ff2_all)
    o_b, lse_b = attn_fwd(proj, 3 * pa, proj, 4 * pa, proj, 5 * pa, tab_b, colbias, nb, "attn_b", off_diag_bias=False)
    m_a = prenorm_fwd(o_a, g_out_a, None, None, None, nb, "out_norm_a")
    m_b = prenorm_fwd(o_b, g_out_b, None, None, None, nb, "out_norm_b")
    x2, y0m = postnorm_fwd(x1, [(m_a, w_o_a), (m_b, w_o_b)], g_post_mix, mod, 5, 1.0, nb, "mix_out_postnorm")

    (dx3, loss_part), saved2 = _ffn_forward(x2, mod, g_pre_ff2, g_post_ff2, wg2, wu2, wd2, 6, nb, "ff2", target=target)[:2]
    loss = lax.psum(loss_part[0, 0], ("x", "y", "c"))

    dx2, gr2, _ = _ffn_backward(dx3, saved2, mod, g_pre_ff2, g_post_ff2, wg2, wu2, wd2, 6, nb, "ff2")
    ff2_blocks = [_rows_to_blocks(gr2[k], D_FF)[:, None] for k in ("wg", "wu", "wd")]

    dy0m, dg_post_mix, dgate_mix = postnorm_bwd(dx2, y0m, g_post_mix, mod, 5, 1.0, nb, "mix_postnorm_bwd")
    dw_o = mm_tn_stack([m_a, m_b], dy0m, [WIDTH_A, WIDTH_B], BF16, "mix_dwo")
    do_a, dg_out_a = prenorm_bwd([(dy0m, w_o_a.T)], o_a, g_out_a, None, None, None, nb, "out_norm_a_bwd")
    do_b, dg_out_b = prenorm_bwd([(dy0m, w_o_b.T)], o_b, g_out_b, None, None, None, nb, "out_norm_b_bwd")
    (dq_a, dk_a, dv_a), g_ff2 = attn_bwd(proj, 0, proj, pa, proj, 2 * pa, o_a, lse_a, do_a, tab_a, None, nb,
                                            "attn_a_bwd", side=(ff2_blocks, False), rope_tabs=tables)
    dq_b, dk_b, dv_b, dcb, drow = attn_bwd(proj, 3 * pa, proj, 4 * pa, proj, 5 * pa, o_b, lse_b, do_b, tab_b, colbias, nb,
                                           "attn_b_bwd", off_diag_bias=False)
    dz_t, db_forget = fox_gate_bwd(dcb.reshape(nb, N_HEADS_B, s), drow.reshape(nb, N_HEADS_B, s), ft, bf_col,
                                   "fox_gate_bwd")
    dz = jnp.pad(dz_t.transpose(0, 2, 1).reshape(t, N_HEADS_B), ((0, 0), (0, LANES - N_HEADS_B))).astype(BF16)
    pieces = [dq_a, dk_a, dv_a, dq_b, dk_b, dv_b]
    w_pieces = [w_qkv_t[i * WIDTH_A:(i + 1) * WIDTH_A] for i in range(6)]
    dh2_pairs = list(zip(pieces, w_pieces)) + [(dz, w_f_t)]
    dw_in_t = mm_tn_stack(pieces + [dz], h2, [WIDTH_A] * 6 + [N_HEADS_B], BF16, "mix_dwin")
    dx1, dg_pre_mix, dsc_mix, dsh_mix = prenorm_bwd(dh2_pairs, x1, g_pre_mix, mod, 4, dx2, nb, "mix_dh_prenorm_bwd",
                                                    ts=DH_ROWS)

    g_in = _rows_to_blocks(dw_in_t, dw_in_t.shape[0])[:, None]
    g_out = _rows_to_blocks(dw_o, d)[:, None]
    dx0, gr1, (g_in, g_out) = _ffn_backward(dx1, saved1, mod, g_pre_ff1, g_post_ff1, wg1, wu1, wd1, 0, nb, "ff1",
                                            side=([g_in, g_out], False), chain=True)
    grad_x = dx0.reshape(nb, s, d)

    dmod =jnp.concatenate(list(gr1["mod"]) + [dsh_mix, dsc_mix, dgate_mix] + list(gr2["mod"]), axis=1)
    small = _pack_small(dict(g_pre_ff1=gr1["g_pre"], g_post_ff1=gr1["g_post"], g_pre_mix=dg_pre_mix,
                             g_post_mix=dg_post_mix, g_out_a=dg_out_a, g_out_b=dg_out_b, g_pre_ff2=gr2["g_pre"],
                             g_post_ff2=gr2["g_post"], b_forget=db_forget))
    dmod_all, small_all = all_gather([dmod.reshape(nb, N_MOD * d), small], "gather_small_grads")
    dmod_all = dmod_all.reshape(nbg, N_MOD * d)

    res = {}
    def adamw_t(parts, group, n):
        return tuple(_t(r) for r in adamw(parts, group, _t(weights[n])[0], _t(mom_m[n])[0], _t(mom_v[n])[0], f"adamw_{n}"))

    res["w_ff1_gate"] = adamw_t(gr1["wg"], 0, "w_ff1_gate")
    res["w_ff1_up"] = adamw_t(gr1["wu"], 0, "w_ff1_up")
    res["w_ff2_gate"] = adamw_t(g_ff2[0], 0, "w_ff2_gate")
    res["w_ff2_up"] = adamw_t(g_ff2[1], 0, "w_ff2_up")
    res["w_ff1_down"] = adamw(gr1["wd"], 0, w_ff1_down[0], m_w_ff1_down[0], v_w_ff1_down[0], "adamw_ff1_down")
    res["w_ff2_down"] = adamw(g_ff2[2], 0, w_ff2_down[0], m_w_ff2_down[0], v_w_ff2_down[0], "adamw_ff2_down")
    res["w_in"] = adamw_t(g_in, 0, "w_in")
    res["w_out"] = adamw(g_out, 0, w_out[0], m_w_out[0], v_w_out[0], "adamw_out")
    dmod_cols = lax.dynamic_slice(dmod_all, (0, me * ada_cols), (nbg, ada_cols))
    dw_ada = ada_bwd(c_all, dmod_cols, "ada_bwd")
    res["w_ada"] = adamw(dw_ada[None, None], 0, w_ada[0], m_w_ada[0], v_w_ada[0], "adamw_ada", tr=256)
    res["b_ada"] = adamw(dmod_all[:, None, None], 0, b_ada, m_b_ada, v_b_ada, "adamw_b_ada")
    sizes = {n: weights[n].shape[1] for n in SMALL_ORDER}
    small_res = adamw(small_all[:, None], 0, _pack_small(weights), _pack_small(mom_m), _pack_small(mom_v), "adamw_small")
    small_res = [_unpack_small(r, sizes) for r in small_res]
    for n in SMALL_ORDER:
        res[n] = tuple(r[n] for r in small_res)

    outs = [loss, grad_x]
    for kind in range(4):
        for n in order:
            a = res[n][kind]
            outs.append(a.reshape(weights[n].shape))
    return tuple(outs)
```

```python
import functools

import jax
import jax.numpy as jnp
from jax import lax
from jax.experimental import pallas as pl
from jax.experimental.pallas import tpu as pltpu

F32 = jnp.float32
BF16 = jnp.bfloat16

D_MODEL = 1024
HEAD_DIM = 64
N_HEADS_A = 8
N_HEADS_B = 8
WIDTH_A = N_HEADS_A * HEAD_DIM
WIDTH_B = N_HEADS_B * HEAD_DIM
DILATED_PATTERNS = ((128, 1), (512, 4), (2048, 16))
ROT_DIM = HEAD_DIM // 4
ROPE_THETA = 500000.0
D_FF = 2752
D_FF_PAD = 2816
N_MOD = 9
EPS = 1e-6
ATTN_SCALE = HEAD_DIM ** -0.5
NEG = -1e30
N_DEV = 8
LANES = 128
HEADS_PER_STEP = LANES // HEAD_DIM

ADAM_LR = 0.001
ADAM_B1 = 0.9
ADAM_B2 = 0.999
ADAM_EPS = 1e-08
ADAM_WD = 0.01
ADAM_STEP = 10

VMEM_LIMIT = 56 * 1024 * 1024
MESH = pl.DeviceIdType.MESH

NT_DIMS = (((1,), (1,)), ((), ()))
TN_DIMS = (((0,), (0,)), ((), ()))
NN_DIMS = (((1,), (0,)), ((), ()))


def _place():
    return lax.axis_index("x"), lax.axis_index("y"), lax.axis_index("c")


def _slot(p):
    return 4 * p[0] + 2 * p[1] + p[2]


def _direct_copies(ins, outs, send_sems, recv_sems, local_sems, gather):
    x, y, c = _place()
    me = (x, y, c)
    flip = lambda v, bit: 1 - v if bit else v
    peers = [(flip(x, k & 4), flip(y, k & 2), flip(c, k & 1)) for k in range(1, N_DEV)]
    local, sends, recvs = [], [], []
    for a in range(len(ins)):
        mine = ins[a] if gather else ins[a].at[_slot(me)]
        local.append(pltpu.make_async_copy(mine, outs[a].at[_slot(me)], local_sems.at[a]))
        for k, peer in enumerate(peers):
            sems = dict(send_sem=send_sems.at[a * 7 + k], recv_sem=recv_sems.at[a * 7 + k], device_id=peer,
                        device_id_type=MESH)
            sends.append(pltpu.make_async_remote_copy(
                src_ref=ins[a] if gather else ins[a].at[_slot(peer)], dst_ref=outs[a].at[_slot(me)], **sems))
            recvs.append(pltpu.make_async_remote_copy(src_ref=mine, dst_ref=outs[a].at[_slot(peer)], **sems))
    return local, sends, recvs


def _two_level_copies(ins, outs, send_sems, recv_sems, local_sems):
    x, y, c = _place()
    me, sibling = (x, y, c), (x, y, 1 - c)
    chips = [(1 - x, y), (x, 1 - y), (1 - x, 1 - y)]
    local, first, landed, forwards, late = [], [], [], [], []
    for a in range(len(ins)):
        def copy(k, block, to, src=None, a=a):
            dst = outs[a].at[_slot(block)]
            return pltpu.make_async_remote_copy(
                src_ref=dst if src is None else src, dst_ref=dst, send_sem=send_sems.at[a * 7 + k],
                recv_sem=recv_sems.at[a * 7 + k], device_id=to, device_id_type=MESH)

        local.append(pltpu.make_async_copy(ins[a], outs[a].at[_slot(me)], local_sems.at[a]))
        first.append(copy(0, me, sibling, src=ins[a]))
        late.append(copy(0, sibling, me))
        for j, chip in enumerate(chips):
            first.append(copy(1 + j, me, (*chip, c), src=ins[a]))
            landed.append(copy(1 + j, (*chip, c), me))
            forwards.append(copy(4 + j, (*chip, c), sibling))
            late.append(copy(4 + j, (*chip, 1 - c), me))
    return local, first, landed, forwards, late


def _comm_scratch(n):
    return [pltpu.SemaphoreType.DMA((7 * n,)), pltpu.SemaphoreType.DMA((7 * n,)), pltpu.SemaphoreType.DMA((n,))]


def _pcall(body, side=None, **kw):
    if side is None:
        return pl.pallas_call(body, **kw)
    arrs, gather = side
    n = len(arrs)
    grid = kw["grid"]
    in_specs = list(kw["in_specs"])
    single = not isinstance(kw["out_specs"], (list, tuple))
    out_specs = [kw["out_specs"]] if single else list(kw["out_specs"])
    out_shape = [kw["out_shape"]] if single else list(kw["out_shape"])
    scratch = list(kw.get("scratch_shapes", []))
    n_in, n_out, n_scr = len(in_specs), len(out_specs), len(scratch)
    hbm = pl.BlockSpec(memory_space=pl.ANY)

    def hosted(*refs):
        pos = [0]

        def take(k):
            pos[0] += k
            return refs[pos[0] - k:pos[0]]

        ins, s_ins, outs, s_outs, scr, sems = take(n_in), take(n), take(n_out), take(n), take(n_scr), take(3)
        ids = [pl.program_id(i) for i in range(len(grid))]
        first = functools.reduce(jnp.logical_and, [i == 0 for i in ids])
        last = functools.reduce(jnp.logical_and, [i == g - 1 for i, g in zip(ids, grid)])
        if gather == "two_level":
            axis = max(range(len(grid)), key=lambda i: grid[i])
            assert grid[axis] >= 2
            middle = functools.reduce(jnp.logical_and, [i == ((3 * grid[axis]) // 4 if k == axis else 0)
                                                        for k, i in enumerate(ids)])

            @pl.when(first)
            def _():
                local, sends, _, _, _ = _two_level_copies(s_ins, s_outs, *sems)
                for cp in local + sends:
                    cp.start()

            @pl.when(middle)
            def _():
                _, _, landed, forwards, _ = _two_level_copies(s_ins, s_outs, *sems)
                for cp_in, cp_out in zip(landed, forwards):
                    cp_in.wait_recv()
                    cp_out.start()

            body(*ins, *outs, *scr)

            @pl.when(last)
            def _():
                local, sends, _, forwards, late = _two_level_copies(s_ins, s_outs, *sems)
                for cp in late:
                    cp.wait_recv()
                for cp in sends + forwards:
                    cp.wait_send()
                for cp in local:
                    cp.wait()
            return

        @pl.when(first)
        def _():
            local, sends, _ = _direct_copies(s_ins, s_outs, *sems, gather)
            for cp in local + sends:
                cp.start()

        body(*ins, *outs, *scr)

        @pl.when(last)
        def _():
            local, sends, recvs = _direct_copies(s_ins, s_outs, *sems, gather)
            for cp in recvs:
                cp.wait_recv()
            for cp in sends:
                cp.wait_send()
            for cp in local:
                cp.wait()

    kw.update(in_specs=in_specs + [hbm] * n, out_specs=out_specs + [hbm] * n,
              out_shape=out_shape + [jax.ShapeDtypeStruct(((N_DEV,) + a.shape) if gather else a.shape, a.dtype)
                                     for a in arrs],
              scratch_shapes=scratch + _comm_scratch(n))
    call = pl.pallas_call(hosted, **kw)

    def run(*args):
        res = call(*args, *arrs)
        main = res[0] if single else list(res[:n_out])
        return main, list(res[n_out:])

    return run


def _params(sem=None, **kw):
    if sem is not None:
        kw["dimension_semantics"] = sem
    return pltpu.CompilerParams(vmem_limit_bytes=VMEM_LIMIT, **kw)


def _rotate(xv, c, sp, sm, transpose):
    width = xv.shape[1]
    half = ROT_DIM // 2
    if transpose:
        return xv * c + pltpu.roll(xv * sp, width - half, 1) + pltpu.roll(xv * sm, half, 1)
    return xv * c + pltpu.roll(xv, half, 1) * sp + pltpu.roll(xv, width - half, 1) * sm


def mm_rows(pairs, trans_b, out_dtype, name, tm=512, side=None, rope=None):
    n = len(pairs)
    m = pairs[0][0].shape[0]
    n_out = pairs[0][1].shape[0 if trans_b else 1]
    dims = NT_DIMS if trans_b else NN_DIMS

    def body(*refs):
        o_ref = refs[-1]
        acc = None
        for a_ref, b_ref in zip(refs[:n], refs[n:2 * n]):
            d = lax.dot_general(a_ref[...], b_ref[...], dims, preferred_element_type=F32)
            acc = d if acc is None else acc + d
        if rope is None:
            o_ref[...] = acc.astype(o_ref.dtype)
        else:
            width = rope[1]
            c, sp, sm = (jnp.concatenate([r[...]] * (width // LANES), axis=1) for r in refs[2 * n:2 * n + 3])
            o_ref[:, :width] = _rotate(acc[:, :width], c, sp, sm, False).astype(o_ref.dtype)
            o_ref[:, width:] = acc[:, width:].astype(o_ref.dtype)

    in_specs = [pl.BlockSpec((tm, a.shape[1]), lambda i: (i, 0)) for a, _ in pairs]
    in_specs += [pl.BlockSpec(b.shape, lambda i: (0, 0)) for _, b in pairs]
    args = [a for a, _ in pairs] + [b for _, b in pairs]
    if rope is not None:
        in_specs += [pl.BlockSpec((tm, LANES), lambda i: (i, 0))] * 3
        args += list(rope[0])
    return _pcall(
        body, side=side, name=name, grid=(m // tm,), in_specs=in_specs,
        out_specs=pl.BlockSpec((tm, n_out), lambda i: (i, 0)),
        out_shape=jax.ShapeDtypeStruct((m, n_out), out_dtype),
        compiler_params=_params(("arbitrary",)),
    )(*args)


DH_ROWS = 256
TN_TOKENS = 2048
TN_OUT_ELEMS = 2 * 1024 * 1024


def mm_tn(a, b, out_dtype, name, side=None, rows=None):
    t, ka = a.shape
    n_out = b.shape[1]
    tk = min(TN_TOKENS, t)
    tka = ka // 2 if ka * n_out > TN_OUT_ELEMS else ka
    tn = n_out
    steps = t // tk

    def body(a_ref, b_ref, o_ref, acc_ref):
        k = pl.program_id(2)
        d = lax.dot_general(a_ref[...], b_ref[...], TN_DIMS, preferred_element_type=F32)

        @pl.when(k == 0)
        def _():
            acc_ref[...] = d

        @pl.when(k > 0)
        def _():
            acc_ref[...] += d

        @pl.when(k == steps - 1)
        def _():
            o_ref[...] = acc_ref[...].astype(o_ref.dtype)

    return _pcall(
        body, side=side, name=name, grid=(ka // tka, n_out // tn, steps),
        in_specs=[pl.BlockSpec((tk, tka), lambda i, j, k: (k, i)), pl.BlockSpec((tk, tn), lambda i, j, k: (k, j))],
        out_specs=pl.BlockSpec((tka, tn), lambda i, j, k: (i, j)),
        out_shape=jax.ShapeDtypeStruct((ka if rows is None else rows, n_out), out_dtype),
        scratch_shapes=[pltpu.VMEM((tka, tn), F32)],
        compiler_params=_params(("arbitrary", "arbitrary", "arbitrary")),
    )(a, b)


def mm_tn_stack(a_list, b, rows, out_dtype, name, tk=1024):
    t, n_out = b.shape
    tk = min(tk, t)
    steps = t // tk
    n = len(a_list)
    offs = [sum(rows[:i]) for i in range(n)]

    def body(*refs):
        a_refs, b_ref, o_ref, acc_refs = refs[:n], refs[n], refs[n + 1], refs[n + 2:]
        k = pl.program_id(0)
        bv = b_ref[...]
        for a_ref, acc_ref in zip(a_refs, acc_refs):
            d = lax.dot_general(a_ref[...], bv, TN_DIMS, preferred_element_type=F32)

            @pl.when(k == 0)
            def _(acc_ref=acc_ref, d=d):
                acc_ref[...] = d

            @pl.when(k > 0)
            def _(acc_ref=acc_ref, d=d):
                acc_ref[...] += d

        @pl.when(k == steps - 1)
        def _():
            for acc_ref, off, r in zip(acc_refs, offs, rows):
                o_ref[off:off + r, :] = acc_ref[0:r, :].astype(o_ref.dtype)

    return _pcall(
        body, name=name, grid=(steps,),
        in_specs=[pl.BlockSpec((tk, a.shape[1]), lambda k: (k, 0)) for a in a_list]
        + [pl.BlockSpec((tk, n_out), lambda k: (k, 0))],
        out_specs=pl.BlockSpec((sum(rows), n_out), lambda k: (0, 0)),
        out_shape=jax.ShapeDtypeStruct((sum(rows), n_out), out_dtype),
        scratch_shapes=[pltpu.VMEM((a.shape[1], n_out), F32) for a in a_list],
        compiler_params=_params(("arbitrary",)),
    )(*a_list, b)


def _col_chunks(width, chunk=512):
    return [slice(c, min(c + chunk, width)) for c in range(0, width, chunk)]


def _sigmoid(x):
    return 1.0 / (1.0 + jnp.exp(-x))


def ffn_up(h, wgt, wut, name, tm=256, tn=D_FF_PAD, side=None):
    t, d = h.shape
    fp = wgt.shape[0]

    def body(h_ref, wg_ref, wu_ref, g_ref, u_ref, a_ref):
        hv = h_ref[...]

        def finish(cols, g, u):
            g_ref[:, cols] = g.astype(BF16)
            u_ref[:, cols] = u.astype(BF16)
            a_ref[:, cols] = (g * _sigmoid(g) * u).astype(BF16)

        pending = None
        for cols in _col_chunks(tn):
            g = lax.dot_general(hv, wg_ref[cols, :], NT_DIMS, preferred_element_type=F32)
            u = lax.dot_general(hv, wu_ref[cols, :], NT_DIMS, preferred_element_type=F32)
            if pending is not None:
                finish(*pending)
            pending = (cols, g, u)
        finish(*pending)

    w_spec = pl.BlockSpec((tn, d), lambda j, i: (j, 0))
    o_spec = pl.BlockSpec((tm, tn), lambda j, i: (i, j))
    o_shape = jax.ShapeDtypeStruct((t, fp), BF16)
    return _pcall(
        body, side=side, name=name, grid=(fp // tn, t // tm),
        in_specs=[pl.BlockSpec((tm, d), lambda j, i: (i, 0)), w_spec, w_spec],
        out_specs=[o_spec, o_spec, o_spec], out_shape=[o_shape, o_shape, o_shape],
        compiler_params=_params(("arbitrary", "arbitrary")),
    )(h, wgt, wut)


def ffn_down_bwd(dy0, wd, gate, up, name, tm=256, tn=D_FF_PAD, side=None):
    t, d = dy0.shape
    fp = wd.shape[0]

    def body(dy_ref, wd_ref, g_ref, u_ref, dg_ref, du_ref):
        dyv = dy_ref[...]

        def finish(cols, dact):
            g = g_ref[:, cols].astype(F32)
            u = u_ref[:, cols].astype(F32)
            sg = _sigmoid(g)
            silu = g * sg
            du_ref[:, cols] = (dact * silu).astype(BF16)
            dg_ref[:, cols] = ((dact * u) * (sg + silu * (1.0 - sg))).astype(BF16)

        pending = None
        for cols in _col_chunks(tn):
            dact = lax.dot_general(dyv, wd_ref[cols, :], NT_DIMS, preferred_element_type=F32)
            if pending is not None:
                finish(*pending)
            pending = (cols, dact)
        finish(*pending)

    t_spec = pl.BlockSpec((tm, tn), lambda j, i: (i, j))
    o_shape = jax.ShapeDtypeStruct((t, fp), BF16)
    return _pcall(
        body, side=side, name=name, grid=(fp // tn, t // tm),
        in_specs=[pl.BlockSpec((tm, d), lambda j, i: (i, 0)), pl.BlockSpec((tn, d), lambda j, i: (j, 0)), t_spec, t_spec],
        out_specs=[t_spec, t_spec], out_shape=[o_shape, o_shape],
        compiler_params=_params(("arbitrary", "arbitrary")),
    )(dy0, wd, gate, up)


def _row_specs(dx, ts, ns):
    return pl.BlockSpec((ts, dx), lambda b, s: (b * ns + s, 0))


def _mod_spec():
    return pl.BlockSpec((1, N_MOD, D_MODEL), lambda b, s: (b, 0, 0))


def _vec_spec(dx):
    return pl.BlockSpec((1, dx), lambda b, s: (0, 0))


def prenorm_fwd(x, g, mod, i_shift, i_scale, nb, name, ts=1024):
    t, dx = x.shape
    ts = min(ts, t // nb)
    ns = t // nb // ts

    def body(*refs):
        if mod is None:
            x_ref, g_ref, h_ref = refs
        else:
            x_ref, g_ref, mod_ref, h_ref = refs
        xv = x_ref[...]
        r = lax.rsqrt(jnp.mean(xv * xv, axis=-1, keepdims=True) + EPS)
        h = xv * r * g_ref[...]
        if mod is not None:
            h = h * (1.0 + mod_ref[0, i_scale:i_scale + 1, :]) + mod_ref[0, i_shift:i_shift + 1, :]
        h_ref[...] = h.astype(BF16)

    in_specs = [_row_specs(dx, ts, ns), _vec_spec(dx)]
    args = [x, g]
    if mod is not None:
        in_specs.append(_mod_spec())
        args.append(mod)
    return _pcall(
        body, name=name, grid=(nb, ns), in_specs=in_specs, out_specs=_row_specs(dx, ts, ns),
        out_shape=jax.ShapeDtypeStruct((t, dx), BF16), compiler_params=_params(("arbitrary", "arbitrary")),
    )(*args)


def prenorm_bwd(dh, x, g, mod, i_scale, dres, nb, name, ts=512, side=None):
    t, dx = x.shape
    ts = min(ts, t // nb)
    ns = t // nb // ts
    has_mod = mod is not None
    has_res = dres is not None
    pairs = dh if isinstance(dh, list) else None
    n_mm = 0 if pairs is None else len(pairs)

    def body(*refs):
        refs = list(refs)
        if pairs is None:
            dhv = refs[0][...].astype(F32)
            refs = refs[1:]
        else:
            dhv = None
            for a_ref, b_ref in zip(refs[:n_mm], refs[n_mm:2 * n_mm]):
                d = jnp.dot(a_ref[...], b_ref[...], preferred_element_type=F32)
                dhv = d if dhv is None else dhv + d
            refs = refs[2 * n_mm:]
        x_ref, g_ref = refs[:2]
        pos = 2
        mod_ref = dres_ref = None
        if has_mod:
            mod_ref = refs[pos]
            pos += 1
        if has_res:
            dres_ref = refs[pos]
            pos += 1
        dx_ref, dg_ref = refs[pos], refs[pos + 1]
        b, s = pl.program_id(0), pl.program_id(1)
        xv = x_ref[...]
        gv = g_ref[...]
        r = lax.rsqrt(jnp.mean(xv * xv, axis=-1, keepdims=True) + EPS)
        xhat = xv * r
        dn = dhv
        if has_mod:
            dsc_ref, dsh_ref = refs[pos + 2], refs[pos + 3]
            dn = dhv * (1.0 + mod_ref[0, i_scale:i_scale + 1, :])
            dsc = jnp.sum(dhv * xhat * gv, axis=0, keepdims=True)[None]
            dsh = jnp.sum(dhv, axis=0, keepdims=True)[None]

            @pl.when(s == 0)
            def _():
                dsc_ref[...] = dsc
                dsh_ref[...] = dsh

            @pl.when(s > 0)
            def _():
                dsc_ref[...] += dsc
                dsh_ref[...] += dsh

        dg = jnp.sum(dn * xhat, axis=0, keepdims=True)
        first = jnp.logical_and(b == 0, s == 0)

        @pl.when(first)
        def _():
            dg_ref[...] = dg

        @pl.when(jnp.logical_not(first))
        def _():
            dg_ref[...] += dg

        dxhat = dn * gv
        dxv = r * (dxhat - xhat * jnp.mean(dxhat * xhat, axis=-1, keepdims=True))
        if has_res:
            dxv = dxv + dres_ref[...]
        dx_ref[...] = dxv

    row = _row_specs(dx, ts, ns)
    if pairs is None:
        in_specs, args = [row], [dh]
    else:
        in_specs = [_row_specs(a.shape[1], ts, ns) for a, _ in pairs]
        in_specs += [pl.BlockSpec(b.shape, lambda b_, s_: (0, 0)) for _, b in pairs]
        args = [a for a, _ in pairs] + [b for _, b in pairs]
    in_specs += [row, _vec_spec(dx)]
    args += [x, g]
    if has_mod:
        in_specs.append(_mod_spec())
        args.append(mod)
    if has_res:
        in_specs.append(row)
        args.append(dres)
    out_specs = [row, _vec_spec(dx)]
    out_shape = [jax.ShapeDtypeStruct((t, dx), F32), jax.ShapeDtypeStruct((1, dx), F32)]
    if has_mod:
        bspec = pl.BlockSpec((1, 1, dx), lambda b, s: (b, 0, 0))
        out_specs += [bspec, bspec]
        out_shape += [jax.ShapeDtypeStruct((nb, 1, dx), F32)] * 2
    return _pcall(
        body, side=side, name=name, grid=(nb, ns), in_specs=in_specs, out_specs=out_specs, out_shape=out_shape,
        compiler_params=_params(("arbitrary", "arbitrary")),
    )(*args)


def postnorm_fwd(x, pairs, g, mod, i_gate, coef, nb, name, target=None, ts=512, side=None):
    t, dx = x.shape
    with_loss = target is not None
    ts = min(ts, t // nb)
    ns = t // nb // ts
    n_mm = len(pairs)

    def body(*refs):
        yv = None
        for a_ref, b_ref in zip(refs[:n_mm], refs[n_mm:2 * n_mm]):
            d = jnp.dot(a_ref[...], b_ref[...], preferred_element_type=F32)
            yv = d if yv is None else yv + d
        refs = refs[2 * n_mm:]
        x_ref, g_ref, mod_ref = refs[:3]
        refs[-1][...] = yv.astype(BF16)
        r = lax.rsqrt(jnp.mean(yv * yv, axis=-1, keepdims=True) + EPS)
        out = x_ref[...] + (coef * mod_ref[0, i_gate:i_gate + 1, :]) * (yv * r * g_ref[...])
        if not with_loss:
            refs[3][...] = out
            return
        t_ref, dx_ref, loss_ref = refs[3:6]
        b, s = pl.program_id(0), pl.program_id(1)
        err = out - t_ref[...]
        dx_ref[...] = err * (1.0 / dx)
        part = (0.5 / dx) * jnp.sum(jnp.sum(err * err, axis=1, keepdims=True), axis=0, keepdims=True)
        first = jnp.logical_and(b == 0, s == 0)

        @pl.when(first)
        def _():
            loss_ref[...] = part

        @pl.when(jnp.logical_not(first))
        def _():
            loss_ref[...] += part

    row = _row_specs(dx, ts, ns)
    in_specs = [_row_specs(a.shape[1], ts, ns) for a, _ in pairs]
    in_specs += [pl.BlockSpec(b.shape, lambda b_, s_: (0, 0)) for _, b in pairs]
    in_specs += [row, _vec_spec(dx), _mod_spec()]
    args = [a for a, _ in pairs] + [b for _, b in pairs] + [x, g, mod]
    row_shape = jax.ShapeDtypeStruct((t, dx), F32)
    y0_shape = jax.ShapeDtypeStruct((t, dx), BF16)
    out_specs, out_shape = [row, row], [row_shape, y0_shape]
    if with_loss:
        in_specs.append(row)
        args.append(target)
        out_specs = [row, pl.BlockSpec((1, 1), lambda b, s: (0, 0)), row]
        out_shape = [row_shape, jax.ShapeDtypeStruct((1, 1), F32), y0_shape]
    return _pcall(
        body, side=side, name=name, grid=(nb, ns), in_specs=in_specs, out_specs=out_specs, out_shape=out_shape,
        compiler_params=_params(("arbitrary", "arbitrary")),
    )(*args)


def postnorm_bwd(dxo, y0, g, mod, i_gate, coef, nb, name, ts=1024):
    t, dx = y0.shape
    ts = min(ts, t // nb)
    ns = t // nb // ts

    def body(d_ref, y_ref, g_ref, mod_ref, dy_ref, dg_ref, dgate_ref):
        b, s = pl.program_id(0), pl.program_id(1)
        yv = y_ref[...].astype(F32)
        dv = d_ref[...]
        gv = g_ref[...]
        r = lax.rsqrt(jnp.mean(yv * yv, axis=-1, keepdims=True) + EPS)
        yhat = yv * r
        dgate = jnp.sum(dv * (coef * (yhat * gv)), axis=0, keepdims=True)[None]
        dyn = dv * (coef * mod_ref[0, i_gate:i_gate + 1, :])
        dg = jnp.sum(dyn * yhat, axis=0, keepdims=True)
        dyhat = dyn * gv
        dy_ref[...] = (r * (dyhat - yhat * jnp.mean(dyhat * yhat, axis=-1, keepdims=True))).astype(BF16)

        @pl.when(s == 0)
        def _():
            dgate_ref[...] = dgate

        @pl.when(s > 0)
        def _():
            dgate_ref[...] += dgate

        first = jnp.logical_and(b == 0, s == 0)

        @pl.when(first)
        def _():
            dg_ref[...] = dg

        @pl.when(jnp.logical_not(first))
        def _():
            dg_ref[...] += dg

    row = _row_specs(dx, ts, ns)
    return _pcall(
        body, name=name, grid=(nb, ns), in_specs=[row, row, _vec_spec(dx), _mod_spec()],
        out_specs=[row, _vec_spec(dx), pl.BlockSpec((1, 1, dx), lambda b, s: (b, 0, 0))],
        out_shape=[jax.ShapeDtypeStruct((t, dx), BF16), jax.ShapeDtypeStruct((1, dx), F32),
                   jax.ShapeDtypeStruct((nb, 1, dx), F32)],
        compiler_params=_params(("arbitrary", "arbitrary")),
    )(dxo, y0, g, mod)


def rope_tables(positions):
    inv_freq = ROPE_THETA ** (-jnp.arange(0, ROT_DIM, 2, dtype=F32) / ROT_DIM)
    ang = positions.astype(F32).reshape(-1, 1) * inv_freq
    cos, sin = jnp.cos(ang), jnp.sin(ang)
    half = ROT_DIM // 2
    z = lambda n: jnp.zeros((ang.shape[0], n), F32)
    c = jnp.concatenate([cos, cos, jnp.ones((ang.shape[0], HEAD_DIM - ROT_DIM), F32)], axis=1)
    sp = jnp.concatenate([z(half), sin, z(HEAD_DIM - ROT_DIM)], axis=1)
    sm = jnp.concatenate([-sin, z(HEAD_DIM - half)], axis=1)
    return tuple(jnp.tile(a, (1, HEADS_PER_STEP)) for a in (c, sp, sm))


def _scan_lanes(x, reverse):
    n = x.shape[-1]
    lane = lax.broadcasted_iota(jnp.int32, x.shape, x.ndim - 1)
    k = 1
    while k < n:
        if reverse:
            x = x + jnp.where(lane < n - k, pltpu.roll(x, n - k, x.ndim - 1), 0.0)
        else:
            x = x + jnp.where(lane >= k, pltpu.roll(x, k, x.ndim - 1), 0.0)
        k *= 2
    return x


def _log_sigmoid(z):
    return jnp.minimum(z, 0.0) - jnp.log(1.0 + jnp.exp(-jnp.abs(z)))


def fox_gate_fwd(ft, b_forget, name):
    nb, nh, s = ft.shape

    def body(f_ref, b_ref, o_ref):
        z = f_ref[0] + b_ref[...]
        o_ref[0] = -_scan_lanes(_log_sigmoid(z), False)

    spec = pl.BlockSpec((1, nh, s), lambda b: (b, 0, 0))
    return _pcall(
        body, name=name, grid=(nb,), in_specs=[spec, pl.BlockSpec((nh, 1), lambda b: (0, 0))], out_specs=spec,
        out_shape=jax.ShapeDtypeStruct((nb, nh, s), F32), compiler_params=_params(("arbitrary",)),
    )(ft, b_forget)


def fox_gate_bwd(dcb, drow, ft, b_forget, name):
    nb, nh, s = ft.shape

    def body(d_ref, r_ref, f_ref, b_ref, dz_ref, db_ref):
        b = pl.program_id(0)
        z = f_ref[0] + b_ref[...]
        dlf = _scan_lanes(r_ref[0] - d_ref[0], True)
        dz = dlf * _sigmoid(-z)
        dz_ref[0] = dz
        db = jnp.sum(dz, axis=1, keepdims=True)

        @pl.when(b == 0)
        def _():
            db_ref[...] = db

        @pl.when(b > 0)
        def _():
            db_ref[...] += db

    spec = pl.BlockSpec((1, nh, s), lambda b: (b, 0, 0))
    vec = pl.BlockSpec((nh, 1), lambda b: (0, 0))
    return _pcall(
        body, name=name, grid=(nb,), in_specs=[spec, spec, spec, vec], out_specs=[spec, vec],
        out_shape=[jax.ShapeDtypeStruct((nb, nh, s), F32), jax.ShapeDtypeStruct((nh, 1), F32)],
        compiler_params=_params(("arbitrary",)),
    )(dcb, drow, ft, b_forget)


ATTN_TQ = 512
ATTN_TK = 512
ONES_ROWS = 16


def _rows_to_cols(rows):
    tile = jnp.concatenate([jnp.broadcast_to(rw, (HEAD_DIM, rw.shape[1])) for rw in rows], axis=0)
    return tile.T


def _block_delta(s, tq, tk):
    off = jnp.arange(s // tk) - (tq // tk - 1)
    return off[:, None, None] * tk + jnp.arange(tq)[None, None, :] - jnp.arange(tk)[None, :, None]


def dilated_table(s, tq, tk):
    delta = _block_delta(s, tq, tk)
    count = jnp.zeros(delta.shape, F32)
    for window, dil in DILATED_PATTERNS:
        count = count + ((delta >= 0) & (delta <= window) & (delta % dil == 0)).astype(F32)
    return jnp.where(count > 0, jnp.log(jnp.maximum(count, 1.0)), NEG)


def causal_table(s, tq, tk):
    return jnp.where(_block_delta(s, tq, tk) >= 0, 0.0, NEG).astype(F32)


def attn_fwd(q_arr, q_off, k_arr, k_off, v_arr, v_off, table, colbias, nb, name, side=None, off_diag_bias=True):
    t = q_arr.shape[0]
    s = t // nb
    tk, tq = table.shape[1:]
    assert tq == tk, "the diagonal handling below is written for square tiles"
    nq, nk = s // tq, s // tk
    npairs = WIDTH_A // LANES
    use_cb = colbias is not None

    def body(*refs):
        refs = list(refs)
        q_ref, k_ref, v_ref, tab_ref = refs[:4]
        cb_ref = refs[4] if use_cb else None
        tail = refs[-(HEADS_PER_STEP + int(use_cb)):]
        acc_s = tail[:HEADS_PER_STEP]
        cbc_s = tail[-1] if use_cb else None
        o_ref, lse_ref, vt_s = refs[-3 - len(tail):-len(tail)]
        qi = pl.program_id(2)

        heads = [slice(h * HEAD_DIM, (h + 1) * HEAD_DIM) for h in range(HEADS_PER_STEP)]

        @pl.when(qi == 0)
        def _():
            for cblk in range(nk):
                vt = v_ref[cblk * tk:(cblk + 1) * tk, :].astype(F32).T.astype(BF16)
                for h, hs in enumerate(heads):
                    vt_s[cblk, h, 0:HEAD_DIM, :] = vt[hs, :]
                    vt_s[cblk, h, HEAD_DIM:, :] = jnp.ones((ONES_ROWS, tk), BF16)
                if use_cb:
                    cbc_s[cblk] = _rows_to_cols([cb_ref[0, h, cblk] for h in range(HEADS_PER_STEP)])

        qt_all = (q_ref[...].astype(F32) * ATTN_SCALE).T.astype(BF16)
        qts = [qt_all[hs, :] for hs in heads]
        for a in acc_s:
            a[...] = jnp.zeros_like(a)

        def tile(kb, tab, k0, klen, q0, carry):
            ks = pl.multiple_of(kb * tk + k0, klen)
            sts, out = [], []
            for h, hs in enumerate(heads):
                st = jnp.dot(k_ref[pl.ds(ks, klen), hs], qts[h][:, q0:], preferred_element_type=F32)
                if tab is not None:
                    st = st + tab
                if use_cb:
                    st = st + cbc_s[kb, k0:k0 + klen, h * HEAD_DIM:h * HEAD_DIM + 1]
                sts.append(st)
            m_old = [carry[h][:, q0:] for h in range(HEADS_PER_STEP)]
            m_new = [jnp.maximum(m_old[h], jnp.max(sts[h], axis=0, keepdims=True)) for h in range(HEADS_PER_STEP)]
            for h in range(HEADS_PER_STEP):
                pt = jnp.exp(sts[h] - m_new[h]).astype(BF16)
                acc_s[h][:, q0:] = (jnp.exp(m_old[h] - m_new[h]) * acc_s[h][:, q0:]
                                    + jnp.dot(vt_s[kb, h, :, k0:k0 + klen], pt, preferred_element_type=F32))
                out.append(m_new[h] if q0 == 0 else jnp.concatenate([carry[h][:, :q0], m_new[h]], axis=1))
            return tuple(out)

        fin = lax.fori_loop(0, qi, lambda kb, c: tile(kb, tab_ref[qi - kb] if off_diag_bias else None, 0, tk, 0, c),
                            tuple(jnp.full((1, tq), NEG, F32) for _ in heads))
        half = tk // 2
        fin = tile(qi, tab_ref[0, 0:half, :], 0, half, 0, fin)
        fin = tile(qi, tab_ref[0, half:, half:], half, half, half, fin)
        outs = []
        for h in range(HEADS_PER_STEP):
            l = acc_s[h][HEAD_DIM:HEAD_DIM + 1, :]
            outs.append(acc_s[h][0:HEAD_DIM, :] / l)
            lse_ref[0, h, 0] = fin[h] + jnp.log(l)
        o_ref[...] = jnp.concatenate(outs, axis=0).T

    def seq_spec(off):
        return pl.BlockSpec((s, LANES), lambda b, j, i: (b, off + j))

    in_specs = [pl.BlockSpec((tq, LANES), lambda b, j, i: (b * nq + i, q_off + j)), seq_spec(k_off), seq_spec(v_off),
                pl.BlockSpec(table.shape, lambda b, j, i: (0, 0, 0))]
    args = [q_arr, k_arr, v_arr, table]
    if use_cb:
        in_specs.append(pl.BlockSpec((1, HEADS_PER_STEP, nk, 1, tk), lambda b, j, i: (b, j, 0, 0, 0)))
        args.append(colbias)
    n_heads = npairs * HEADS_PER_STEP
    return _pcall(
        body, side=side, name=name, grid=(nb, npairs, nq), in_specs=in_specs,
        out_specs=[pl.BlockSpec((tq, LANES), lambda b, j, i: (b * nq + i, j)),
                   pl.BlockSpec((1, HEADS_PER_STEP, 1, 1, tq), lambda b, j, i: (b, j, i, 0, 0))],
        out_shape=[jax.ShapeDtypeStruct((t, npairs * LANES), F32), jax.ShapeDtypeStruct((nb, n_heads, nq, 1, tq), F32)],
        scratch_shapes=[pltpu.VMEM((nk, HEADS_PER_STEP, HEAD_DIM + ONES_ROWS, tk), BF16)]
        + [pltpu.VMEM((HEAD_DIM + ONES_ROWS, tq), F32)] * HEADS_PER_STEP
        + ([pltpu.VMEM((nk, tk, LANES), F32)] if use_cb else []),
        compiler_params=_params(("arbitrary", "arbitrary", "arbitrary")),
    )(*args)


def attn_bwd(q_arr, q_off, k_arr, k_off, v_arr, v_off, o_arr, lse_arr, do_arr, table, colbias, nb, name, side=None,
             rope_tabs=None, off_diag_bias=True):
    t = q_arr.shape[0]
    s = t // nb
    tk, tq = table.shape[1:]
    assert tq == tk, "the diagonal handling below is written for square tiles"
    nq, nk = s // tq, s // tk
    npairs = WIDTH_A // LANES
    use_cb = colbias is not None

    def body(*refs):
        refs = list(refs)
        q_ref, k_ref, v_ref, o_ref, lse_ref, do_ref, tab_ref = refs[:7]
        pos = 7
        cb_ref = None
        if use_cb:
            cb_ref = refs[pos]
            pos += 1
        rope_refs = None
        if rope_tabs is not None:
            rope_refs = refs[pos:pos + 3]
            pos += 3
        dq_ref, dk_ref, dv_ref = refs[pos:pos + 3]
        pos += 3
        dcb_ref = drow_ref = None
        if use_cb:
            dcb_ref, drow_ref = refs[pos:pos + 2]
            pos += 2
        kt_s, dkt_s, dvt_s = refs[pos:pos + 3]
        dqt_s = refs[pos + 3:pos + 3 + HEADS_PER_STEP]
        dcb_s, cbc_s = refs[pos + 3 + HEADS_PER_STEP:pos + 5 + HEADS_PER_STEP] if use_cb else (None, None)

        heads = [slice(h * HEAD_DIM, (h + 1) * HEAD_DIM) for h in range(HEADS_PER_STEP)]
        for cblk in range(nk):
            kt_s[cblk] = k_ref[cblk * tk:(cblk + 1) * tk, :].astype(F32).T.astype(BF16)
        dkt_s[...] = jnp.zeros_like(dkt_s)
        dvt_s[...] = jnp.zeros_like(dvt_s)
        if use_cb:
            dcb_s[...] = jnp.zeros_like(dcb_s)
            for cblk in range(nk):
                cbc_s[cblk] = _rows_to_cols([cb_ref[0, h, cblk] for h in range(HEADS_PER_STEP)])
        ones = jnp.ones((8, HEAD_DIM), BF16)

        def q_loop(qi, carry):
            qs = pl.multiple_of(qi * tq, tq)
            q_all = (q_ref[pl.ds(qs, tq), :].astype(F32) * ATTN_SCALE)
            do_all = do_ref[pl.ds(qs, tq), :]
            qt_all = q_all.T.astype(BF16)
            dot_all = do_all.T.astype(BF16)
            qt, dot, lse, dsum = [], [], [], []
            for h, hs in enumerate(heads):
                qt.append(qt_all[hs, :])
                dot.append(dot_all[hs, :])
                lse.append(lse_ref[0, h, qi])
                prod = do_all[:, hs] * o_ref[pl.ds(qs, tq), hs]
                hi = prod.astype(BF16)
                lo = (prod - hi.astype(F32)).astype(BF16)
                dsum.append((lax.dot_general(ones, hi, NT_DIMS, preferred_element_type=F32)
                             + lax.dot_general(ones, lo, NT_DIMS, preferred_element_type=F32))[0:1, :])
            for a in dqt_s:
                a[...] = jnp.zeros_like(a)

            def tile(kb, tab, k0, klen, q0, drow):
                ks = pl.multiple_of(kb * tk + k0, klen)
                keys = slice(k0, k0 + klen)
                sts, dpts, out = [], [], []
                for h, hs in enumerate(heads):
                    st = jnp.dot(k_ref[pl.ds(ks, klen), hs], qt[h][:, q0:], preferred_element_type=F32)
                    if tab is not None:
                        st = st + tab
                    if use_cb:
                        st = st + cbc_s[kb, keys, h * HEAD_DIM:h * HEAD_DIM + 1]
                    sts.append(st)
                    dpts.append(jnp.dot(v_ref[pl.ds(ks, klen), hs], dot[h][:, q0:], preferred_element_type=F32))
                for h, hs in enumerate(heads):
                    pt = jnp.exp(sts[h] - lse[h][:, q0:])
                    dst = pt * (dpts[h] - dsum[h][:, q0:])
                    dst_b = dst.astype(BF16)
                    dvt_s[h, kb, :, keys] += lax.dot_general(dot[h][:, q0:], pt.astype(BF16), NT_DIMS,
                                                             preferred_element_type=F32)
                    dkt_s[h, kb, :, keys] += lax.dot_general(qt[h][:, q0:], dst_b, NT_DIMS, preferred_element_type=F32)
                    dqt_s[h][:, q0:] += jnp.dot(kt_s[kb, hs, keys], dst_b, preferred_element_type=F32)
                    if use_cb:
                        dcb_s[h, pl.ds(ks, klen), :] += jnp.sum(dst, axis=1, keepdims=True)
                        dr = drow[h][:, q0:] + jnp.sum(dst, axis=0, keepdims=True)
                        out.append(dr if q0 == 0 else jnp.concatenate([drow[h][:, :q0], dr], axis=1))
                    else:
                        out.append(drow[h])
                return tuple(out)

            drow = lax.fori_loop(0, qi, lambda kb, c: tile(kb, tab_ref[qi - kb] if off_diag_bias else None, 0, tk, 0, c),
                                 tuple(jnp.zeros((1, tq), F32) for _ in heads))
            half = tk // 2
            drow = tile(qi, tab_ref[0, 0:half, :], 0, half, 0, drow)
            drow = tile(qi, tab_ref[0, half:, half:], half, half, half, drow)
            dq = (jnp.concatenate([a[...] for a in dqt_s], axis=0) * ATTN_SCALE).T
            if rope_refs is not None:
                dq = _rotate(dq, *[coef[pl.ds(qs, tq), :] for coef in rope_refs], True)
            dq_ref[pl.ds(qs, tq), :] = dq.astype(dq_ref.dtype)
            if use_cb:
                for h in range(HEADS_PER_STEP):
                    drow_ref[0, h, qi] = drow[h]
            return carry

        lax.fori_loop(0, nq, q_loop, 0)
        for cblk in range(nk):
            rows = slice(cblk * tk, (cblk + 1) * tk)
            dk = jnp.concatenate([dkt_s[h, cblk] for h in range(HEADS_PER_STEP)], axis=0).T
            if rope_refs is not None:
                dk = _rotate(dk, *[coef[rows, :] for coef in rope_refs], True)
            dk_ref[rows, :] = dk.astype(dk_ref.dtype)
            dv_ref[rows, :] = jnp.concatenate([dvt_s[h, cblk] for h in range(HEADS_PER_STEP)], axis=0).T.astype(dv_ref.dtype)
            if use_cb:
                for h in range(HEADS_PER_STEP):
                    dcb_ref[0, h, cblk] = jnp.broadcast_to(dcb_s[h, rows, :], (tk, LANES)).T[0:1, :]

    def seq_spec(off):
        return pl.BlockSpec((s, LANES), lambda b, j: (b, off + j))

    row_spec = pl.BlockSpec((1, HEADS_PER_STEP, nq, 1, tq), lambda b, j: (b, j, 0, 0, 0))
    in_specs = [seq_spec(q_off), seq_spec(k_off), seq_spec(v_off), seq_spec(0), row_spec, seq_spec(0),
                pl.BlockSpec(table.shape, lambda b, j: (0, 0, 0))]
    args = [q_arr, k_arr, v_arr, o_arr, lse_arr, do_arr, table]
    width = npairs * LANES
    out_specs = [seq_spec(0)] * 3
    out_shape = [jax.ShapeDtypeStruct((t, width), BF16)] * 3
    scratch = [pltpu.VMEM((nk, LANES, tk), BF16), pltpu.VMEM((HEADS_PER_STEP, nk, HEAD_DIM, tk), F32),
               pltpu.VMEM((HEADS_PER_STEP, nk, HEAD_DIM, tk), F32)] + [pltpu.VMEM((HEAD_DIM, tq), F32)] * HEADS_PER_STEP
    if use_cb:
        cb_spec = pl.BlockSpec((1, HEADS_PER_STEP, nk, 1, tk), lambda b, j: (b, j, 0, 0, 0))
        in_specs.append(cb_spec)
        args.append(colbias)
    if rope_tabs is not None:
        in_specs += [pl.BlockSpec((s, LANES), lambda b, j: (b, 0))] * 3
        args += list(rope_tabs)
    if use_cb:
        out_specs += [cb_spec, row_spec]
        out_shape += [jax.ShapeDtypeStruct(colbias.shape, F32), jax.ShapeDtypeStruct(lse_arr.shape, F32)]
        scratch += [pltpu.VMEM((HEADS_PER_STEP, s, 1), F32), pltpu.VMEM((nk, tk, LANES), F32)]
    return _pcall(
        body, side=side, name=name, grid=(nb, npairs), in_specs=in_specs, out_specs=out_specs, out_shape=out_shape,
        scratch_shapes=scratch, compiler_params=_params(("arbitrary", "arbitrary")),
    )(*args)


def ada_fwd(c_all, w_ada, b_cols, name):
    def body(c_ref, w_ref, b_ref, o_ref):
        cv = c_ref[...]
        sc = (cv * _sigmoid(cv)).astype(BF16)
        o_ref[...] = jnp.dot(sc, w_ref[...].astype(BF16), preferred_element_type=F32) + b_ref[...]

    return _pcall(body, name=name, out_shape=jax.ShapeDtypeStruct((c_all.shape[0], w_ada.shape[1]), F32),
                  compiler_params=_params())(c_all, w_ada, b_cols)


def ada_bwd(c_all, dmod_cols, name):
    def body(c_ref, d_ref, o_ref):
        cv = c_ref[...]
        sc = (cv * _sigmoid(cv)).astype(BF16)
        o_ref[...] = lax.dot_general(sc, d_ref[...].astype(BF16), TN_DIMS, preferred_element_type=F32)

    return _pcall(body, name=name, out_shape=jax.ShapeDtypeStruct((c_all.shape[1], dmod_cols.shape[1]), F32),
                  compiler_params=_params())(c_all, dmod_cols)


def adamw(parts, group, w, m, v, name, tr=None):
    n = parts.shape[0]
    r, c = w.shape
    tr = r if tr is None else tr
    c1 = 1.0 - ADAM_B1 ** ADAM_STEP
    c2 = 1.0 - ADAM_B2 ** ADAM_STEP

    def body(p_ref, w_ref, m_ref, v_ref, g_ref, d_ref, nm_ref, nv_ref):
        g = p_ref[0, 0].astype(F32)
        for i in range(1, n):
            g = g + p_ref[i, 0].astype(F32)
        wv = w_ref[...]
        nm = ADAM_B1 * m_ref[...] + (1.0 - ADAM_B1) * g
        nv = ADAM_B2 * v_ref[...] + (1.0 - ADAM_B2) * (g * g)
        g_ref[...] = g
        nm_ref[...] = nm
        nv_ref[...] = nv
        d_ref[...] = -ADAM_LR * ((nm / c1) / (jnp.sqrt(nv / c2) + ADAM_EPS) + ADAM_WD * wv)

    spec = pl.BlockSpec((tr, c), lambda i: (i, 0))
    shape = jax.ShapeDtypeStruct((r, c), F32)
    return _pcall(
        body, name=name, grid=(r // tr,),
        in_specs=[pl.BlockSpec((n, 1, tr, c), lambda i: (0, group, i, 0)), spec, spec, spec],
        out_specs=[spec] * 4, out_shape=[shape] * 4, compiler_params=_params(("arbitrary",)),
    )(parts, w, m, v)


def all_gather(arrs, name):
    n = len(arrs)
    hbm = pl.BlockSpec(memory_space=pl.ANY)

    def body(*refs):
        ins, outs = refs[:n], refs[n:2 * n]
        send_sems, recv_sems, local_sems = refs[2 * n:]
        x, y, c = _place()
        me, sibling = (x, y, c), (x, y, 1 - c)
        chips = [(1 - x, y), (x, 1 - y), (1 - x, 1 - y)]

        def copy(a, k, block, to, src=None):
            dst = outs[a].at[_slot(block)]
            return pltpu.make_async_remote_copy(
                src_ref=dst if src is None else src, dst_ref=dst, send_sem=send_sems.at[a * 7 + k],
                recv_sem=recv_sems.at[a * 7 + k], device_id=to, device_id_type=MESH)

        mine = [pltpu.make_async_copy(ins[a], outs[a].at[_slot(me)], local_sems.at[a]) for a in range(n)]
        for cp in mine:
            cp.start()
        first = []
        for a in range(n):
            first.append(copy(a, 0, me, sibling, src=ins[a]))
            first += [copy(a, 1 + j, me, (*chip, c), src=ins[a]) for j, chip in enumerate(chips)]
        for cp in first:
            cp.start()
        passed = []
        for a in range(n):
            for j, chip in enumerate(chips):
                copy(a, 1 + j, (*chip, c), me).wait_recv()
                cp = copy(a, 4 + j, (*chip, c), sibling)
                cp.start()
                passed.append(cp)
        for a in range(n):
            copy(a, 0, sibling, me).wait_recv()
            for j, chip in enumerate(chips):
                copy(a, 4 + j, (*chip, 1 - c), me).wait_recv()
        for cp in first + passed:
            cp.wait_send()
        for cp in mine:
            cp.wait()

    return _pcall(
        body, name=name, in_specs=[hbm] * n, out_specs=[hbm] * n,
        out_shape=[jax.ShapeDtypeStruct((N_DEV,) + a.shape, a.dtype) for a in arrs],
        scratch_shapes=[pltpu.SemaphoreType.DMA((7 * n,)), pltpu.SemaphoreType.DMA((7 * n,)),
                        pltpu.SemaphoreType.DMA((n,))],
        compiler_params=pltpu.CompilerParams(has_side_effects=True),
    )(*arrs)


def _t(w):
    return jnp.swapaxes(w, -1, -2)


def _rows_from_blocks(blocks, pad_to=None):
    full = blocks.reshape(-1, blocks.shape[2])
    if pad_to is not None and pad_to > full.shape[0]:
        full = jnp.pad(full, ((0, pad_to - full.shape[0]), (0, 0)))
    return full


def _rows_to_blocks(full, nrows):
    return full[:nrows].reshape(N_DEV, nrows // N_DEV, full.shape[1])


SMALL_ORDER = ("g_pre_ff1", "g_post_ff1", "g_pre_mix", "g_post_mix", "g_out_a", "g_out_b", "g_pre_ff2", "g_post_ff2",
               "b_forget")


def _pack_small(vals):
    rows = []
    for name in SMALL_ORDER:
        v = vals[name].reshape(1, -1)
        if v.shape[1] % LANES:
            v = jnp.pad(v, ((0, 0), (0, LANES - v.shape[1] % LANES)))
        rows.append(v)
    return jnp.concatenate(rows, axis=1)


def _unpack_small(row, sizes):
    out, pos = {}, 0
    for name in SMALL_ORDER:
        n = sizes[name]
        out[name] = row[:, pos:pos + n]
        pos += -(-n // LANES) * LANES
    return out


def _ffn_forward(x, mod, g_pre, g_post, wg, wu, wd, i0, nb, tag, target=None, side=None, side_down=None):
    h = prenorm_fwd(x, g_pre, mod, i0, i0 + 1, nb, f"{tag}_prenorm")
    res, side_out = ffn_up(h, wg, wu, f"{tag}_up", side=side), None
    if side is not None:
        res, side_out = res
    gate, up, act = res
    if callable(wd):
        wd = wd(side_out)
    res, side_down_out = postnorm_fwd(x, [(act, wd)], g_post, mod, i0 + 2, 0.5, nb, f"{tag}_down_postnorm",
                                      target=target, side=side_down), None
    if side_down is not None:
        res, side_down_out = res
    out, y0 = (res[0] if target is None else tuple(res[:2])), res[-1]
    return out, (x, h, gate, up, act, y0), wd, side_out, side_down_out


def _ffn_backward(dxo, saved, mod, g_pre, g_post, wg, wu, wd, i0, nb, tag, exchange=False, also=()):
    x, h, gate, up, act, y0 = saved
    blocks = lambda g: _rows_to_blocks(g, D_FF)[:, None]
    dy0, dg_post, dgate_mod = postnorm_bwd(dxo, y0, g_post, mod, i0 + 2, 0.5, nb, f"{tag}_postnorm_bwd")
    dwd = mm_tn(act, dy0, BF16, f"{tag}_dwd", rows=D_FF)
    res, extra = ffn_down_bwd(dy0, wd, gate, up, f"{tag}_down_bwd",
                              side=([blocks(dwd)] + list(also), False) if exchange else None), []
    if exchange:
        res, (dwd, *extra) = res
    dgate, dup = res
    dwg = mm_tn(dgate, h, BF16, f"{tag}_dwg", rows=D_FF)
    dwu = mm_tn(dup, h, BF16, f"{tag}_dwu", rows=D_FF)
    res = prenorm_bwd([(dgate, wg), (dup, wu)], x, g_pre, mod, i0 + 1, dxo, nb, f"{tag}_dh_prenorm_bwd", ts=DH_ROWS,
                      side=([blocks(dwg), blocks(dwu)], False) if exchange else None)
    if exchange:
        res, (dwg, dwu) = res
    dx, dg_pre, dsc, dsh = res
    return dx, dict(g_pre=dg_pre, g_post=dg_post, wg=dwg, wu=dwu, wd=dwd, mod=(dsh, dsc, dgate_mod)), extra


def kernel(x, c, positions, w_ada, b_ada, g_pre_ff1, g_post_ff1, w_ff1_gate, w_ff1_up, w_ff1_down, g_pre_mix, g_post_mix, w_in, b_forget, g_out_a, g_out_b, w_out, g_pre_ff2, g_post_ff2, w_ff2_gate, w_ff2_up, w_ff2_down, loss_target, m_w_ada, m_b_ada, m_g_pre_ff1, m_g_post_ff1, m_w_ff1_gate, m_w_ff1_up, m_w_ff1_down, m_g_pre_mix, m_g_post_mix, m_w_in, m_b_forget, m_g_out_a, m_g_out_b, m_w_out, m_g_pre_ff2, m_g_post_ff2, m_w_ff2_gate, m_w_ff2_up, m_w_ff2_down, v_w_ada, v_b_ada, v_g_pre_ff1, v_g_post_ff1, v_w_ff1_gate, v_w_ff1_up, v_w_ff1_down, v_g_pre_mix, v_g_post_mix, v_w_in, v_b_forget, v_g_out_a, v_g_out_b, v_w_out, v_g_pre_ff2, v_g_post_ff2, v_w_ff2_gate, v_w_ff2_up, v_w_ff2_down):
    weights = dict(w_ada=w_ada, b_ada=b_ada, g_pre_ff1=g_pre_ff1, g_post_ff1=g_post_ff1, w_ff1_gate=w_ff1_gate,
                   w_ff1_up=w_ff1_up, w_ff1_down=w_ff1_down, g_pre_mix=g_pre_mix, g_post_mix=g_post_mix, w_in=w_in,
                   b_forget=b_forget, g_out_a=g_out_a, g_out_b=g_out_b, w_out=w_out, g_pre_ff2=g_pre_ff2,
                   g_post_ff2=g_post_ff2, w_ff2_gate=w_ff2_gate, w_ff2_up=w_ff2_up, w_ff2_down=w_ff2_down)
    mom_m = dict(w_ada=m_w_ada, b_ada=m_b_ada, g_pre_ff1=m_g_pre_ff1, g_post_ff1=m_g_post_ff1, w_ff1_gate=m_w_ff1_gate,
                 w_ff1_up=m_w_ff1_up, w_ff1_down=m_w_ff1_down, g_pre_mix=m_g_pre_mix, g_post_mix=m_g_post_mix,
                 w_in=m_w_in, b_forget=m_b_forget, g_out_a=m_g_out_a, g_out_b=m_g_out_b, w_out=m_w_out,
                 g_pre_ff2=m_g_pre_ff2, g_post_ff2=m_g_post_ff2, w_ff2_gate=m_w_ff2_gate, w_ff2_up=m_w_ff2_up,
                 w_ff2_down=m_w_ff2_down)
    mom_v = dict(w_ada=v_w_ada, b_ada=v_b_ada, g_pre_ff1=v_g_pre_ff1, g_post_ff1=v_g_post_ff1, w_ff1_gate=v_w_ff1_gate,
                 w_ff1_up=v_w_ff1_up, w_ff1_down=v_w_ff1_down, g_pre_mix=v_g_pre_mix, g_post_mix=v_g_post_mix,
                 w_in=v_w_in, b_forget=v_b_forget, g_out_a=v_g_out_a, g_out_b=v_g_out_b, w_out=v_w_out,
                 g_pre_ff2=v_g_pre_ff2, g_post_ff2=v_g_post_ff2, w_ff2_gate=v_w_ff2_gate, w_ff2_up=v_w_ff2_up,
                 w_ff2_down=v_w_ff2_down)
    order = list(weights)

    nb, s, d = x.shape
    t = nb * s
    me = _slot(_place())
    nbg = nb * N_DEV
    ada_cols = w_ada.shape[2]

    bf = lambda w: w[0].astype(BF16)
    bft = lambda w: _t(w)[0].astype(BF16)
    c_all, wg1, wu1 = all_gather([c, bft(w_ff1_gate), bft(w_ff1_up)], "gather_ff1")
    c_all = c_all.reshape(nbg, d)
    wg1, wu1 = (_rows_from_blocks(w, D_FF_PAD) for w in (wg1, wu1))

    b_cols = lax.dynamic_slice(b_ada, (0, me * ada_cols), (1, ada_cols))
    mod_cols = ada_fwd(c_all, w_ada[0], b_cols, "ada_fwd")
    (mod_all,) = all_gather([mod_cols], "gather_mod")
    mod = lax.dynamic_slice(mod_all, (0, me * nb, 0), (N_DEV, nb, ada_cols))
    mod = mod.transpose(1, 0, 2).reshape(nb, N_MOD, d)

    xf = x.reshape(t, d)
    target = loss_target.reshape(t, d)

    x1, saved1, wd1, (_, w_in_all, w_out_all) = _ffn_forward(
        xf, mod, g_pre_ff1, g_post_ff1, wg1, wu1, lambda got: _rows_from_blocks(got[0], D_FF_PAD), 0, nb, "ff1",
        side=([bf(w_ff1_down), bft(w_in), bf(w_out)], "two_level"))[:4]
    w_in_t = _rows_from_blocks(w_in_all)
    n_qkv = 3 * (WIDTH_A + WIDTH_B)
    w_qkv_t = w_in_t[:n_qkv]
    w_f_t = jnp.pad(w_in_t[n_qkv:], ((0, LANES - N_HEADS_B), (0, 0)))
    w_o = _rows_from_blocks(w_out_all)
    w_o_a, w_o_b = w_o[:WIDTH_A], w_o[WIDTH_A:]

    h2 = prenorm_fwd(x1, g_pre_mix, mod, 3, 4, nb, "mix_prenorm")
    tables = rope_tables(positions)
    proj = mm_rows([(h2, w_qkv_t)], True, BF16, "mix_proj", rope=(tables, 2 * WIDTH_A))
    f_logit = mm_rows([(h2, w_f_t)], True, F32, "mix_forget")
    tab_a = dilated_table(s, ATTN_TQ, ATTN_TK)
    tab_b = causal_table(s, ATTN_TQ, ATTN_TK)
    ft = f_logit[:, :N_HEADS_B].reshape(nb, s, N_HEADS_B).transpose(0, 2, 1)
    bf_col = b_forget.reshape(N_HEADS_B, 1)
    colbias = fox_gate_fwd(ft, bf_col, "fox_gate").reshape(nb, N_HEADS_B, s // ATTN_TK, 1, ATTN_TK)
    pa = WIDTH_A // LANES
    (o_a, lse_a), ff2_all = attn_fwd(
        proj, 0, proj, pa, proj, 2 * pa, tab_a, None, nb, "attn_a",
        side=([bft(w_ff2_gate), bft(w_ff2_up), bf(w_ff2_down)], "two_level"))
    wg2, wu2, wd2 = (_rows_from_blocks(w, D_FF_PAD) for w in ff2_all)
    o_b, lse_b = attn_fwd(proj, 3 * pa, proj, 4 * pa, proj, 5 * pa, tab_b, colbias, nb, "attn_b", off_diag_bias=False)
    m_a = prenorm_fwd(o_a, g_out_a, None, None, None, nb, "out_norm_a")
    m_b = prenorm_fwd(o_b, g_out_b, None, None, None, nb, "out_norm_b")
    x2, y0m = postnorm_fwd(x1, [(m_a, w_o_a), (m_b, w_o_b)], g_post_mix, mod, 5, 1.0, nb, "mix_out_postnorm")

    (dx3, loss_part), saved2 = _ffn_forward(x2, mod, g_pre_ff2, g_post_ff2, wg2, wu2, wd2, 6, nb, "ff2", target=target)[:2]
    loss = lax.psum(loss_part[0, 0], ("x", "y", "c"))

    dx2, gr2, _ = _ffn_backward(dx3, saved2, mod, g_pre_ff2, g_post_ff2, wg2, wu2, wd2, 6, nb, "ff2")
    ff2_blocks = [_rows_to_blocks(gr2[k], D_FF)[:, None] for k in ("wg", "wu", "wd")]

    dy0m, dg_post_mix, dgate_mix = postnorm_bwd(dx2, y0m, g_post_mix, mod, 5, 1.0, nb, "mix_postnorm_bwd")
    dw_o = mm_tn_stack([m_a, m_b], dy0m, [WIDTH_A, WIDTH_B], BF16, "mix_dwo")
    do_a, dg_out_a = prenorm_bwd([(dy0m, w_o_a.T)], o_a, g_out_a, None, None, None, nb, "out_norm_a_bwd")
    do_b, dg_out_b = prenorm_bwd([(dy0m, w_o_b.T)], o_b, g_out_b, None, None, None, nb, "out_norm_b_bwd")
    (dq_a, dk_a, dv_a), g_ff2 = attn_bwd(proj, 0, proj, pa, proj, 2 * pa, o_a, lse_a, do_a, tab_a, None, nb,
                                            "attn_a_bwd", side=(ff2_blocks, False), rope_tabs=tables)
    dq_b, dk_b, dv_b, dcb, drow = attn_bwd(proj, 3 * pa, proj, 4 * pa, proj, 5 * pa, o_b, lse_b, do_b, tab_b, colbias, nb,
                                           "attn_b_bwd", off_diag_bias=False)
    dz_t, db_forget = fox_gate_bwd(dcb.reshape(nb, N_HEADS_B, s), drow.reshape(nb, N_HEADS_B, s), ft, bf_col,
                                   "fox_gate_bwd")
    dz = jnp.pad(dz_t.transpose(0, 2, 1).reshape(t, N_HEADS_B), ((0, 0), (0, LANES - N_HEADS_B))).astype(BF16)
    pieces = [dq_a, dk_a, dv_a, dq_b, dk_b, dv_b]
    w_pieces = [w_qkv_t[i * WIDTH_A:(i + 1) * WIDTH_A] for i in range(6)]
    dh2_pairs = list(zip(pieces, w_pieces)) + [(dz, w_f_t)]
    dw_in_t = mm_tn_stack(pieces + [dz], h2, [WIDTH_A] * 6 + [N_HEADS_B], BF16, "mix_dwin")
    g_in = _rows_to_blocks(dw_in_t, dw_in_t.shape[0])[:, None]
    g_out = _rows_to_blocks(dw_o, d)[:, None]
    (dx1, dg_pre_mix, dsc_mix, dsh_mix), (g_in,) = prenorm_bwd(
        dh2_pairs, x1, g_pre_mix, mod, 4, dx2, nb, "mix_dh_prenorm_bwd", ts=DH_ROWS, side=([g_in], False))
    dx0, gr1, (g_out,) = _ffn_backward(dx1, saved1, mod, g_pre_ff1, g_post_ff1, wg1, wu1, wd1, 0, nb, "ff1",
                                       exchange=True, also=[g_out])
    grad_x = dx0.reshape(nb, s, d)

    dmod =jnp.concatenate(list(gr1["mod"]) + [dsh_mix, dsc_mix, dgate_mix] + list(gr2["mod"]), axis=1)
    small = _pack_small(dict(g_pre_ff1=gr1["g_pre"], g_post_ff1=gr1["g_post"], g_pre_mix=dg_pre_mix,
                             g_post_mix=dg_post_mix, g_out_a=dg_out_a, g_out_b=dg_out_b, g_pre_ff2=gr2["g_pre"],
                             g_post_ff2=gr2["g_post"], b_forget=db_forget))
    dmod_all, small_all = all_gather([dmod.reshape(nb, N_MOD * d), small], "gather_small_grads")
    dmod_all = dmod_all.reshape(nbg, N_MOD * d)

    res = {}
    def adamw_t(parts, group, n):
        return tuple(_t(r) for r in adamw(parts, group, _t(weights[n])[0], _t(mom_m[n])[0], _t(mom_v[n])[0], f"adamw_{n}"))

    res["w_ff1_gate"] = adamw_t(gr1["wg"], 0, "w_ff1_gate")
    res["w_ff1_up"] = adamw_t(gr1["wu"], 0, "w_ff1_up")
    res["w_ff2_gate"] = adamw_t(g_ff2[0], 0, "w_ff2_gate")
    res["w_ff2_up"] = adamw_t(g_ff2[1], 0, "w_ff2_up")
    res["w_ff1_down"] = adamw(gr1["wd"], 0, w_ff1_down[0], m_w_ff1_down[0], v_w_ff1_down[0], "adamw_ff1_down")
    res["w_ff2_down"] = adamw(g_ff2[2], 0, w_ff2_down[0], m_w_ff2_down[0], v_w_ff2_down[0], "adamw_ff2_down")
    res["w_in"] = adamw_t(g_in, 0, "w_in")
    res["w_out"] = adamw(g_out, 0, w_out[0], m_w_out[0], v_w_out[0], "adamw_out")
    dmod_cols = lax.dynamic_slice(dmod_all, (0, me * ada_cols), (nbg, ada_cols))
    dw_ada = ada_bwd(c_all, dmod_cols, "ada_bwd")
    res["w_ada"] = adamw(dw_ada[None, None], 0, w_ada[0], m_w_ada[0], v_w_ada[0], "adamw_ada", tr=256)
    res["b_ada"] = adamw(dmod_all[:, None, None], 0, b_ada, m_b_ada, v_b_ada, "adamw_b_ada")
    sizes = {n: weights[n].shape[1] for n in SMALL_ORDER}
    small_res = adamw(small_all[:, None], 0, _pack_small(weights), _pack_small(mom_m), _pack_small(mom_v), "adamw_small")
    small_res = [_unpack_small(r, sizes) for r in small_res]
    for n in SMALL_ORDER:
        res[n] = tuple(r[n] for r in small_res)

    outs = [loss, grad_x]
    for kind in range(4):
        for n in order:
            a = res[n][kind]
            outs.append(a.reshape(weights[n].shape))
    return tuple(outs)
```

```python
import functools

import jax
import jax.numpy as jnp
from jax import lax
from jax.experimental import pallas as pl
from jax.experimental.pallas import tpu as pltpu

F32 = jnp.float32
BF16 = jnp.bfloat16

D_MODEL = 1024
HEAD_DIM = 64
N_HEADS_A = 8
N_HEADS_B = 8
WIDTH_A = N_HEADS_A * HEAD_DIM
WIDTH_B = N_HEADS_B * HEAD_DIM
DILATED_PATTERNS = ((128, 1), (512, 4), (2048, 16))
ROT_DIM = HEAD_DIM // 4
ROPE_THETA = 500000.0
D_FF = 2752
D_FF_PAD = 2816
N_MOD = 9
EPS = 1e-6
ATTN_SCALE = HEAD_DIM ** -0.5
NEG = -1e30
N_DEV = 8
LANES = 128
HEADS_PER_STEP = LANES // HEAD_DIM

ADAM_LR = 0.001
ADAM_B1 = 0.9
ADAM_B2 = 0.999
ADAM_EPS = 1e-08
ADAM_WD = 0.01
ADAM_STEP = 10

VMEM_LIMIT = 56 * 1024 * 1024
MESH = pl.DeviceIdType.MESH

NT_DIMS = (((1,), (1,)), ((), ()))
TN_DIMS = (((0,), (0,)), ((), ()))
NN_DIMS = (((1,), (0,)), ((), ()))


def _place():
    return lax.axis_index("x"), lax.axis_index("y"), lax.axis_index("c")


def _slot(p):
    return 4 * p[0] + 2 * p[1] + p[2]


def _direct_copies(ins, outs, send_sems, recv_sems, local_sems, gather):
    x, y, c = _place()
    me = (x, y, c)
    flip = lambda v, bit: 1 - v if bit else v
    peers = [(flip(x, k & 4), flip(y, k & 2), flip(c, k & 1)) for k in range(1, N_DEV)]
    local, sends, recvs = [], [], []
    for a in range(len(ins)):
        mine = ins[a] if gather else ins[a].at[_slot(me)]
        local.append(pltpu.make_async_copy(mine, outs[a].at[_slot(me)], local_sems.at[a]))
        for k, peer in enumerate(peers):
            sems = dict(send_sem=send_sems.at[a * 7 + k], recv_sem=recv_sems.at[a * 7 + k], device_id=peer,
                        device_id_type=MESH)
            sends.append(pltpu.make_async_remote_copy(
                src_ref=ins[a] if gather else ins[a].at[_slot(peer)], dst_ref=outs[a].at[_slot(me)], **sems))
            recvs.append(pltpu.make_async_remote_copy(src_ref=mine, dst_ref=outs[a].at[_slot(peer)], **sems))
    return local, sends, recvs


def _two_level_copies(ins, outs, send_sems, recv_sems, local_sems):
    x, y, c = _place()
    me, sibling = (x, y, c), (x, y, 1 - c)
    chips = [(1 - x, y), (x, 1 - y), (1 - x, 1 - y)]
    local, first, landed, forwards, late = [], [], [], [], []
    for a in range(len(ins)):
        def copy(k, block, to, src=None, a=a):
            dst = outs[a].at[_slot(block)]
            return pltpu.make_async_remote_copy(
                src_ref=dst if src is None else src, dst_ref=dst, send_sem=send_sems.at[a * 7 + k],
                recv_sem=recv_sems.at[a * 7 + k], device_id=to, device_id_type=MESH)

        local.append(pltpu.make_async_copy(ins[a], outs[a].at[_slot(me)], local_sems.at[a]))
        first.append(copy(0, me, sibling, src=ins[a]))
        late.append(copy(0, sibling, me))
        for j, chip in enumerate(chips):
            first.append(copy(1 + j, me, (*chip, c), src=ins[a]))
            landed.append(copy(1 + j, (*chip, c), me))
            forwards.append(copy(4 + j, (*chip, c), sibling))
            late.append(copy(4 + j, (*chip, 1 - c), me))
    return local, first, landed, forwards, late


def _comm_scratch(n):
    return [pltpu.SemaphoreType.DMA((7 * n,)), pltpu.SemaphoreType.DMA((7 * n,)), pltpu.SemaphoreType.DMA((n,))]


def _pcall(body, side=None, **kw):
    if side is None:
        return pl.pallas_call(body, **kw)
    arrs, gather = side
    n = len(arrs)
    grid = kw["grid"]
    in_specs = list(kw["in_specs"])
    single = not isinstance(kw["out_specs"], (list, tuple))
    out_specs = [kw["out_specs"]] if single else list(kw["out_specs"])
    out_shape = [kw["out_shape"]] if single else list(kw["out_shape"])
    scratch = list(kw.get("scratch_shapes", []))
    n_in, n_out, n_scr = len(in_specs), len(out_specs), len(scratch)
    hbm = pl.BlockSpec(memory_space=pl.ANY)

    def hosted(*refs):
        pos = [0]

        def take(k):
            pos[0] += k
            return refs[pos[0] - k:pos[0]]

        ins, s_ins, outs, s_outs, scr, sems = take(n_in), take(n), take(n_out), take(n), take(n_scr), take(3)
        ids = [pl.program_id(i) for i in range(len(grid))]
        first = functools.reduce(jnp.logical_and, [i == 0 for i in ids])
        last = functools.reduce(jnp.logical_and, [i == g - 1 for i, g in zip(ids, grid)])
        if gather == "two_level":
            axis = max(range(len(grid)), key=lambda i: grid[i])
            assert grid[axis] >= 2
            middle = functools.reduce(jnp.logical_and, [i == ((3 * grid[axis]) // 4 if k == axis else 0)
                                                        for k, i in enumerate(ids)])

            @pl.when(first)
            def _():
                local, sends, _, _, _ = _two_level_copies(s_ins, s_outs, *sems)
                for cp in local + sends:
                    cp.start()

            @pl.when(middle)
            def _():
                _, _, landed, forwards, _ = _two_level_copies(s_ins, s_outs, *sems)
                for cp_in, cp_out in zip(landed, forwards):
                    cp_in.wait_recv()
                    cp_out.start()

            body(*ins, *outs, *scr)

            @pl.when(last)
            def _():
                local, sends, _, forwards, late = _two_level_copies(s_ins, s_outs, *sems)
                for cp in late:
                    cp.wait_recv()
                for cp in sends + forwards:
                    cp.wait_send()
                for cp in local:
                    cp.wait()
            return

        @pl.when(first)
        def _():
            local, sends, _ = _direct_copies(s_ins, s_outs, *sems, gather)
            for cp in local + sends:
                cp.start()

        body(*ins, *outs, *scr)

        @pl.when(last)
        def _():
            local, sends, recvs = _direct_copies(s_ins, s_outs, *sems, gather)
            for cp in recvs:
                cp.wait_recv()
            for cp in sends:
                cp.wait_send()
            for cp in local:
                cp.wait()

    kw.update(in_specs=in_specs + [hbm] * n, out_specs=out_specs + [hbm] * n,
              out_shape=out_shape + [jax.ShapeDtypeStruct(((N_DEV,) + a.shape) if gather else a.shape, a.dtype)
                                     for a in arrs],
              scratch_shapes=scratch + _comm_scratch(n))
    call = pl.pallas_call(hosted, **kw)

    def run(*args):
        res = call(*args, *arrs)
        main = res[0] if single else list(res[:n_out])
        return main, list(res[n_out:])

    return run


def _params(sem=None, **kw):
    if sem is not None:
        kw["dimension_semantics"] = sem
    return pltpu.CompilerParams(vmem_limit_bytes=VMEM_LIMIT, **kw)


def _rotate(xv, c, sp, sm, transpose):
    width = xv.shape[1]
    half = ROT_DIM // 2
    if transpose:
        return xv * c + pltpu.roll(xv * sp, width - half, 1) + pltpu.roll(xv * sm, half, 1)
    return xv * c + pltpu.roll(xv, half, 1) * sp + pltpu.roll(xv, width - half, 1) * sm


def mm_rows(pairs, trans_b, out_dtype, name, tm=512, side=None, rope=None):
    n = len(pairs)
    m = pairs[0][0].shape[0]
    n_out = pairs[0][1].shape[0 if trans_b else 1]
    dims = NT_DIMS if trans_b else NN_DIMS

    def body(*refs):
        o_ref = refs[-1]
        acc = None
        for a_ref, b_ref in zip(refs[:n], refs[n:2 * n]):
            d = lax.dot_general(a_ref[...], b_ref[...], dims, preferred_element_type=F32)
            acc = d if acc is None else acc + d
        if rope is None:
            o_ref[...] = acc.astype(o_ref.dtype)
        else:
            width = rope[1]
            c, sp, sm = (jnp.concatenate([r[...]] * (width // LANES), axis=1) for r in refs[2 * n:2 * n + 3])
            o_ref[:, :width] = _rotate(acc[:, :width], c, sp, sm, False).astype(o_ref.dtype)
            o_ref[:, width:] = acc[:, width:].astype(o_ref.dtype)

    in_specs = [pl.BlockSpec((tm, a.shape[1]), lambda i: (i, 0)) for a, _ in pairs]
    in_specs += [pl.BlockSpec(b.shape, lambda i: (0, 0)) for _, b in pairs]
    args = [a for a, _ in pairs] + [b for _, b in pairs]
    if rope is not None:
        in_specs += [pl.BlockSpec((tm, LANES), lambda i: (i, 0))] * 3
        args += list(rope[0])
    return _pcall(
        body, side=side, name=name, grid=(m // tm,), in_specs=in_specs,
        out_specs=pl.BlockSpec((tm, n_out), lambda i: (i, 0)),
        out_shape=jax.ShapeDtypeStruct((m, n_out), out_dtype),
        compiler_params=_params(("arbitrary",)),
    )(*args)


DH_ROWS = 256
TN_TOKENS = 2048
TN_OUT_ELEMS = 2 * 1024 * 1024


def mm_tn(a, b, out_dtype, name, side=None, rows=None):
    t, ka = a.shape
    n_out = b.shape[1]
    tk = min(TN_TOKENS, t)
    tka = ka // 2 if ka * n_out > TN_OUT_ELEMS else ka
    tn = n_out
    steps = t // tk

    def body(a_ref, b_ref, o_ref, acc_ref):
        k = pl.program_id(2)
        d = lax.dot_general(a_ref[...], b_ref[...], TN_DIMS, preferred_element_type=F32)

        @pl.when(k == 0)
        def _():
            acc_ref[...] = d

        @pl.when(k > 0)
        def _():
            acc_ref[...] += d

        @pl.when(k == steps - 1)
        def _():
            o_ref[...] = acc_ref[...].astype(o_ref.dtype)

    return _pcall(
        body, side=side, name=name, grid=(ka // tka, n_out // tn, steps),
        in_specs=[pl.BlockSpec((tk, tka), lambda i, j, k: (k, i)), pl.BlockSpec((tk, tn), lambda i, j, k: (k, j))],
        out_specs=pl.BlockSpec((tka, tn), lambda i, j, k: (i, j)),
        out_shape=jax.ShapeDtypeStruct((ka if rows is None else rows, n_out), out_dtype),
        scratch_shapes=[pltpu.VMEM((tka, tn), F32)],
        compiler_params=_params(("arbitrary", "arbitrary", "arbitrary")),
    )(a, b)


def mm_tn_stack(a_list, b, rows, out_dtype, name, tk=1024):
    t, n_out = b.shape
    tk = min(tk, t)
    steps = t // tk
    n = len(a_list)
    offs = [sum(rows[:i]) for i in range(n)]

    def body(*refs):
        a_refs, b_ref, o_ref, acc_refs = refs[:n], refs[n], refs[n + 1], refs[n + 2:]
        k = pl.program_id(0)
        bv = b_ref[...]
        for a_ref, acc_ref in zip(a_refs, acc_refs):
            d = lax.dot_general(a_ref[...], bv, TN_DIMS, preferred_element_type=F32)

            @pl.when(k == 0)
            def _(acc_ref=acc_ref, d=d):
                acc_ref[...] = d

            @pl.when(k > 0)
            def _(acc_ref=acc_ref, d=d):
                acc_ref[...] += d

        @pl.when(k == steps - 1)
        def _():
            for acc_ref, off, r in zip(acc_refs, offs, rows):
                o_ref[off:off + r, :] = acc_ref[0:r, :].astype(o_ref.dtype)

    return _pcall(
        body, name=name, grid=(steps,),
        in_specs=[pl.BlockSpec((tk, a.shape[1]), lambda k: (k, 0)) for a in a_list]
        + [pl.BlockSpec((tk, n_out), lambda k: (k, 0))],
        out_specs=pl.BlockSpec((sum(rows), n_out), lambda k: (0, 0)),
        out_shape=jax.ShapeDtypeStruct((sum(rows), n_out), out_dtype),
        scratch_shapes=[pltpu.VMEM((a.shape[1], n_out), F32) for a in a_list],
        compiler_params=_params(("arbitrary",)),
    )(*a_list, b)


def _col_chunks(width, chunk=512):
    return [slice(c, min(c + chunk, width)) for c in range(0, width, chunk)]


def _sigmoid(x):
    return 1.0 / (1.0 + jnp.exp(-x))


def ffn_up(h, wgt, wut, name, tm=256, tn=D_FF_PAD, side=None):
    t, d = h.shape
    fp = wgt.shape[0]

    def body(h_ref, wg_ref, wu_ref, g_ref, u_ref, a_ref):
        hv = h_ref[...]

        def finish(cols, g, u):
            g_ref[:, cols] = g.astype(BF16)
            u_ref[:, cols] = u.astype(BF16)
            a_ref[:, cols] = (g * _sigmoid(g) * u).astype(BF16)

        pending = None
        for cols in _col_chunks(tn):
            g = lax.dot_general(hv, wg_ref[cols, :], NT_DIMS, preferred_element_type=F32)
            u = lax.dot_general(hv, wu_ref[cols, :], NT_DIMS, preferred_element_type=F32)
            if pending is not None:
                finish(*pending)
            pending = (cols, g, u)
        finish(*pending)

    w_spec = pl.BlockSpec((tn, d), lambda j, i: (j, 0))
    o_spec = pl.BlockSpec((tm, tn), lambda j, i: (i, j))
    o_shape = jax.ShapeDtypeStruct((t, fp), BF16)
    return _pcall(
        body, side=side, name=name, grid=(fp // tn, t // tm),
        in_specs=[pl.BlockSpec((tm, d), lambda j, i: (i, 0)), w_spec, w_spec],
        out_specs=[o_spec, o_spec, o_spec], out_shape=[o_shape, o_shape, o_shape],
        compiler_params=_params(("arbitrary", "arbitrary")),
    )(h, wgt, wut)


def ffn_down_bwd(dy0, wd, gate, up, name, tm=256, tn=D_FF_PAD, side=None):
    t, d = dy0.shape
    fp = wd.shape[0]

    def body(dy_ref, wd_ref, g_ref, u_ref, dg_ref, du_ref):
        dyv = dy_ref[...]

        def finish(cols, dact):
            g = g_ref[:, cols].astype(F32)
            u = u_ref[:, cols].astype(F32)
            sg = _sigmoid(g)
            silu = g * sg
            du_ref[:, cols] = (dact * silu).astype(BF16)
            dg_ref[:, cols] = ((dact * u) * (sg + silu * (1.0 - sg))).astype(BF16)

        pending = None
        for cols in _col_chunks(tn):
            dact = lax.dot_general(dyv, wd_ref[cols, :], NT_DIMS, preferred_element_type=F32)
            if pending is not None:
                finish(*pending)
            pending = (cols, dact)
        finish(*pending)

    t_spec = pl.BlockSpec((tm, tn), lambda j, i: (i, j))
    o_shape = jax.ShapeDtypeStruct((t, fp), BF16)
    return _pcall(
        body, side=side, name=name, grid=(fp // tn, t // tm),
        in_specs=[pl.BlockSpec((tm, d), lambda j, i: (i, 0)), pl.BlockSpec((tn, d), lambda j, i: (j, 0)), t_spec, t_spec],
        out_specs=[t_spec, t_spec], out_shape=[o_shape, o_shape],
        compiler_params=_params(("arbitrary", "arbitrary")),
    )(dy0, wd, gate, up)


def _row_specs(dx, ts, ns):
    return pl.BlockSpec((ts, dx), lambda b, s: (b * ns + s, 0))


def _mod_spec():
    return pl.BlockSpec((1, N_MOD, D_MODEL), lambda b, s: (b, 0, 0))


def _vec_spec(dx):
    return pl.BlockSpec((1, dx), lambda b, s: (0, 0))


def prenorm_fwd(x, g, mod, i_shift, i_scale, nb, name, ts=1024):
    t, dx = x.shape
    ts = min(ts, t // nb)
    ns = t // nb // ts

    def body(*refs):
        if mod is None:
            x_ref, g_ref, h_ref = refs
        else:
            x_ref, g_ref, mod_ref, h_ref = refs
        xv = x_ref[...]
        r = lax.rsqrt(jnp.mean(xv * xv, axis=-1, keepdims=True) + EPS)
        h = xv * r * g_ref[...]
        if mod is not None:
            h = h * (1.0 + mod_ref[0, i_scale:i_scale + 1, :]) + mod_ref[0, i_shift:i_shift + 1, :]
        h_ref[...] = h.astype(BF16)

    in_specs = [_row_specs(dx, ts, ns), _vec_spec(dx)]
    args = [x, g]
    if mod is not None:
        in_specs.append(_mod_spec())
        args.append(mod)
    return _pcall(
        body, name=name, grid=(nb, ns), in_specs=in_specs, out_specs=_row_specs(dx, ts, ns),
        out_shape=jax.ShapeDtypeStruct((t, dx), BF16), compiler_params=_params(("arbitrary", "arbitrary")),
    )(*args)


def prenorm_bwd(dh, x, g, mod, i_scale, dres, nb, name, ts=512, side=None):
    t, dx = x.shape
    ts = min(ts, t // nb)
    ns = t // nb // ts
    has_mod = mod is not None
    has_res = dres is not None
    pairs = dh if isinstance(dh, list) else None
    n_mm = 0 if pairs is None else len(pairs)

    def body(*refs):
        refs = list(refs)
        if pairs is None:
            dhv = refs[0][...].astype(F32)
            refs = refs[1:]
        else:
            dhv = None
            for a_ref, b_ref in zip(refs[:n_mm], refs[n_mm:2 * n_mm]):
                d = jnp.dot(a_ref[...], b_ref[...], preferred_element_type=F32)
                dhv = d if dhv is None else dhv + d
            refs = refs[2 * n_mm:]
        x_ref, g_ref = refs[:2]
        pos = 2
        mod_ref = dres_ref = None
        if has_mod:
            mod_ref = refs[pos]
            pos += 1
        if has_res:
            dres_ref = refs[pos]
            pos += 1
        dx_ref, dg_ref = refs[pos], refs[pos + 1]
        b, s = pl.program_id(0), pl.program_id(1)
        xv = x_ref[...]
        gv = g_ref[...]
        r = lax.rsqrt(jnp.mean(xv * xv, axis=-1, keepdims=True) + EPS)
        xhat = xv * r
        dn = dhv
        if has_mod:
            dsc_ref, dsh_ref = refs[pos + 2], refs[pos + 3]
            dn = dhv * (1.0 + mod_ref[0, i_scale:i_scale + 1, :])
            dsc = jnp.sum(dhv * xhat * gv, axis=0, keepdims=True)[None]
            dsh = jnp.sum(dhv, axis=0, keepdims=True)[None]

            @pl.when(s == 0)
            def _():
                dsc_ref[...] = dsc
                dsh_ref[...] = dsh

            @pl.when(s > 0)
            def _():
                dsc_ref[...] += dsc
                dsh_ref[...] += dsh

        dg = jnp.sum(dn * xhat, axis=0, keepdims=True)
        first = jnp.logical_and(b == 0, s == 0)

        @pl.when(first)
        def _():
            dg_ref[...] = dg

        @pl.when(jnp.logical_not(first))
        def _():
            dg_ref[...] += dg

        dxhat = dn * gv
        dxv = r * (dxhat - xhat * jnp.mean(dxhat * xhat, axis=-1, keepdims=True))
        if has_res:
            dxv = dxv + dres_ref[...]
        dx_ref[...] = dxv

    row = _row_specs(dx, ts, ns)
    if pairs is None:
        in_specs, args = [row], [dh]
    else:
        in_specs = [_row_specs(a.shape[1], ts, ns) for a, _ in pairs]
        in_specs += [pl.BlockSpec(b.shape, lambda b_, s_: (0, 0)) for _, b in pairs]
        args = [a for a, _ in pairs] + [b for _, b in pairs]
    in_specs += [row, _vec_spec(dx)]
    args += [x, g]
    if has_mod:
        in_specs.append(_mod_spec())
        args.append(mod)
    if has_res:
        in_specs.append(row)
        args.append(dres)
    out_specs = [row, _vec_spec(dx)]
    out_shape = [jax.ShapeDtypeStruct((t, dx), F32), jax.ShapeDtypeStruct((1, dx), F32)]
    if has_mod:
        bspec = pl.BlockSpec((1, 1, dx), lambda b, s: (b, 0, 0))
        out_specs += [bspec, bspec]
        out_shape += [jax.ShapeDtypeStruct((nb, 1, dx), F32)] * 2
    return _pcall(
        body, side=side, name=name, grid=(nb, ns), in_specs=in_specs, out_specs=out_specs, out_shape=out_shape,
        compiler_params=_params(("arbitrary", "arbitrary")),
    )(*args)


def postnorm_fwd(x, pairs, g, mod, i_gate, coef, nb, name, target=None, ts=512, side=None):
    t, dx = x.shape
    with_loss = target is not None
    ts = min(ts, t // nb)
    ns = t // nb // ts
    n_mm = len(pairs)

    def body(*refs):
        yv = None
        for a_ref, b_ref in zip(refs[:n_mm], refs[n_mm:2 * n_mm]):
            d = jnp.dot(a_ref[...], b_ref[...], preferred_element_type=F32)
            yv = d if yv is None else yv + d
        refs = refs[2 * n_mm:]
        x_ref, g_ref, mod_ref = refs[:3]
        refs[-1][...] = yv.astype(BF16)
        r = lax.rsqrt(jnp.mean(yv * yv, axis=-1, keepdims=True) + EPS)
        out = x_ref[...] + (coef * mod_ref[0, i_gate:i_gate + 1, :]) * (yv * r * g_ref[...])
        if not with_loss:
            refs[3][...] = out
            return
        t_ref, dx_ref, loss_ref = refs[3:6]
        b, s = pl.program_id(0), pl.program_id(1)
        err = out - t_ref[...]
        dx_ref[...] = err * (1.0 / dx)
        part = (0.5 / dx) * jnp.sum(jnp.sum(err * err, axis=1, keepdims=True), axis=0, keepdims=True)
        first = jnp.logical_and(b == 0, s == 0)

        @pl.when(first)
        def _():
            loss_ref[...] = part

        @pl.when(jnp.logical_not(first))
        def _():
            loss_ref[...] += part

    row = _row_specs(dx, ts, ns)
    in_specs = [_row_specs(a.shape[1], ts, ns) for a, _ in pairs]
    in_specs += [pl.BlockSpec(b.shape, lambda b_, s_: (0, 0)) for _, b in pairs]
    in_specs += [row, _vec_spec(dx), _mod_spec()]
    args = [a for a, _ in pairs] + [b for _, b in pairs] + [x, g, mod]
    row_shape = jax.ShapeDtypeStruct((t, dx), F32)
    y0_shape = jax.ShapeDtypeStruct((t, dx), BF16)
    out_specs, out_shape = [row, row], [row_shape, y0_shape]
    if with_loss:
        in_specs.append(row)
        args.append(target)
        out_specs = [row, pl.BlockSpec((1, 1), lambda b, s: (0, 0)), row]
        out_shape = [row_shape, jax.ShapeDtypeStruct((1, 1), F32), y0_shape]
    return _pcall(
        body, side=side, name=name, grid=(nb, ns), in_specs=in_specs, out_specs=out_specs, out_shape=out_shape,
        compiler_params=_params(("arbitrary", "arbitrary")),
    )(*args)


def postnorm_bwd(dxo, y0, g, mod, i_gate, coef, nb, name, ts=1024):
    t, dx = y0.shape
    ts = min(ts, t // nb)
    ns = t // nb // ts

    def body(d_ref, y_ref, g_ref, mod_ref, dy_ref, dg_ref, dgate_ref):
        b, s = pl.program_id(0), pl.program_id(1)
        yv = y_ref[...].astype(F32)
        dv = d_ref[...]
        gv = g_ref[...]
        r = lax.rsqrt(jnp.mean(yv * yv, axis=-1, keepdims=True) + EPS)
        yhat = yv * r
        dgate = jnp.sum(dv * (coef * (yhat * gv)), axis=0, keepdims=True)[None]
        dyn = dv * (coef * mod_ref[0, i_gate:i_gate + 1, :])
        dg = jnp.sum(dyn * yhat, axis=0, keepdims=True)
        dyhat = dyn * gv
        dy_ref[...] = (r * (dyhat - yhat * jnp.mean(dyhat * yhat, axis=-1, keepdims=True))).astype(BF16)

        @pl.when(s == 0)
        def _():
            dgate_ref[...] = dgate

        @pl.when(s > 0)
        def _():
            dgate_ref[...] += dgate

        first = jnp.logical_and(b == 0, s == 0)

        @pl.when(first)
        def _():
            dg_ref[...] = dg

        @pl.when(jnp.logical_not(first))
        def _():
            dg_ref[...] += dg

    row = _row_specs(dx, ts, ns)
    return _pcall(
        body, name=name, grid=(nb, ns), in_specs=[row, row, _vec_spec(dx), _mod_spec()],
        out_specs=[row, _vec_spec(dx), pl.BlockSpec((1, 1, dx), lambda b, s: (b, 0, 0))],
        out_shape=[jax.ShapeDtypeStruct((t, dx), BF16), jax.ShapeDtypeStruct((1, dx), F32),
                   jax.ShapeDtypeStruct((nb, 1, dx), F32)],
        compiler_params=_params(("arbitrary", "arbitrary")),
    )(dxo, y0, g, mod)


def rope_tables(positions):
    inv_freq = ROPE_THETA ** (-jnp.arange(0, ROT_DIM, 2, dtype=F32) / ROT_DIM)
    ang = positions.astype(F32).reshape(-1, 1) * inv_freq
    cos, sin = jnp.cos(ang), jnp.sin(ang)
    half = ROT_DIM // 2
    z = lambda n: jnp.zeros((ang.shape[0], n), F32)
    c = jnp.concatenate([cos, cos, jnp.ones((ang.shape[0], HEAD_DIM - ROT_DIM), F32)], axis=1)
    sp = jnp.concatenate([z(half), sin, z(HEAD_DIM - ROT_DIM)], axis=1)
    sm = jnp.concatenate([-sin, z(HEAD_DIM - half)], axis=1)
    return tuple(jnp.tile(a, (1, HEADS_PER_STEP)) for a in (c, sp, sm))


def _scan_lanes(x, reverse):
    n = x.shape[-1]
    lane = lax.broadcasted_iota(jnp.int32, x.shape, x.ndim - 1)
    k = 1
    while k < n:
        if reverse:
            x = x + jnp.where(lane < n - k, pltpu.roll(x, n - k, x.ndim - 1), 0.0)
        else:
            x = x + jnp.where(lane >= k, pltpu.roll(x, k, x.ndim - 1), 0.0)
        k *= 2
    return x


def _log_sigmoid(z):
    return jnp.minimum(z, 0.0) - jnp.log(1.0 + jnp.exp(-jnp.abs(z)))


def fox_gate_fwd(ft, b_forget, name):
    nb, nh, s = ft.shape

    def body(f_ref, b_ref, o_ref):
        z = f_ref[0] + b_ref[...]
        o_ref[0] = -_scan_lanes(_log_sigmoid(z), False)

    spec = pl.BlockSpec((1, nh, s), lambda b: (b, 0, 0))
    return _pcall(
        body, name=name, grid=(nb,), in_specs=[spec, pl.BlockSpec((nh, 1), lambda b: (0, 0))], out_specs=spec,
        out_shape=jax.ShapeDtypeStruct((nb, nh, s), F32), compiler_params=_params(("arbitrary",)),
    )(ft, b_forget)


def fox_gate_bwd(dcb, drow, ft, b_forget, name):
    nb, nh, s = ft.shape

    def body(d_ref, r_ref, f_ref, b_ref, dz_ref, db_ref):
        b = pl.program_id(0)
        z = f_ref[0] + b_ref[...]
        dlf = _scan_lanes(r_ref[0] - d_ref[0], True)
        dz = dlf * _sigmoid(-z)
        dz_ref[0] = dz
        db = jnp.sum(dz, axis=1, keepdims=True)

        @pl.when(b == 0)
        def _():
            db_ref[...] = db

        @pl.when(b > 0)
        def _():
            db_ref[...] += db

    spec = pl.BlockSpec((1, nh, s), lambda b: (b, 0, 0))
    vec = pl.BlockSpec((nh, 1), lambda b: (0, 0))
    return _pcall(
        body, name=name, grid=(nb,), in_specs=[spec, spec, spec, vec], out_specs=[spec, vec],
        out_shape=[jax.ShapeDtypeStruct((nb, nh, s), F32), jax.ShapeDtypeStruct((nh, 1), F32)],
        compiler_params=_params(("arbitrary",)),
    )(dcb, drow, ft, b_forget)


ATTN_TQ = 512
ATTN_TK = 512
ONES_ROWS = 16


def _rows_to_cols(rows):
    tile = jnp.concatenate([jnp.broadcast_to(rw, (HEAD_DIM, rw.shape[1])) for rw in rows], axis=0)
    return tile.T


def _block_delta(s, tq, tk):
    off = jnp.arange(s // tk) - (tq // tk - 1)
    return off[:, None, None] * tk + jnp.arange(tq)[None, None, :] - jnp.arange(tk)[None, :, None]


def dilated_table(s, tq, tk):
    delta = _block_delta(s, tq, tk)
    count = jnp.zeros(delta.shape, F32)
    for window, dil in DILATED_PATTERNS:
        count = count + ((delta >= 0) & (delta <= window) & (delta % dil == 0)).astype(F32)
    return jnp.where(count > 0, jnp.log(jnp.maximum(count, 1.0)), NEG)


def causal_table(s, tq, tk):
    return jnp.where(_block_delta(s, tq, tk) >= 0, 0.0, NEG).astype(F32)


def attn_fwd(q_arr, q_off, k_arr, k_off, v_arr, v_off, table, colbias, nb, name, side=None, off_diag_bias=True):
    t = q_arr.shape[0]
    s = t // nb
    tk, tq = table.shape[1:]
    assert tq == tk, "the diagonal handling below is written for square tiles"
    nq, nk = s // tq, s // tk
    npairs = WIDTH_A // LANES
    use_cb = colbias is not None

    def body(*refs):
        refs = list(refs)
        q_ref, k_ref, v_ref, tab_ref = refs[:4]
        cb_ref = refs[4] if use_cb else None
        tail = refs[-(HEADS_PER_STEP + int(use_cb)):]
        acc_s = tail[:HEADS_PER_STEP]
        cbc_s = tail[-1] if use_cb else None
        o_ref, lse_ref, vt_s = refs[-3 - len(tail):-len(tail)]
        qi = pl.program_id(2)

        heads = [slice(h * HEAD_DIM, (h + 1) * HEAD_DIM) for h in range(HEADS_PER_STEP)]

        @pl.when(qi == 0)
        def _():
            for cblk in range(nk):
                vt = v_ref[cblk * tk:(cblk + 1) * tk, :].astype(F32).T.astype(BF16)
                for h, hs in enumerate(heads):
                    vt_s[cblk, h, 0:HEAD_DIM, :] = vt[hs, :]
                    vt_s[cblk, h, HEAD_DIM:, :] = jnp.ones((ONES_ROWS, tk), BF16)
                if use_cb:
                    cbc_s[cblk] = _rows_to_cols([cb_ref[0, h, cblk] for h in range(HEADS_PER_STEP)])

        qt_all = (q_ref[...].astype(F32) * ATTN_SCALE).T.astype(BF16)
        qts = [qt_all[hs, :] for hs in heads]
        for a in acc_s:
            a[...] = jnp.zeros_like(a)

        def tile(kb, tab, k0, klen, q0, carry):
            ks = pl.multiple_of(kb * tk + k0, klen)
            sts, out = [], []
            for h, hs in enumerate(heads):
                st = jnp.dot(k_ref[pl.ds(ks, klen), hs], qts[h][:, q0:], preferred_element_type=F32)
                if tab is not None:
                    st = st + tab
                if use_cb:
                    st = st + cbc_s[kb, k0:k0 + klen, h * HEAD_DIM:h * HEAD_DIM + 1]
                sts.append(st)
            m_old = [carry[h][:, q0:] for h in range(HEADS_PER_STEP)]
            m_new = [jnp.maximum(m_old[h], jnp.max(sts[h], axis=0, keepdims=True)) for h in range(HEADS_PER_STEP)]
            for h in range(HEADS_PER_STEP):
                pt = jnp.exp(sts[h] - m_new[h]).astype(BF16)
                acc_s[h][:, q0:] = (jnp.exp(m_old[h] - m_new[h]) * acc_s[h][:, q0:]
                                    + jnp.dot(vt_s[kb, h, :, k0:k0 + klen], pt, preferred_element_type=F32))
                out.append(m_new[h] if q0 == 0 else jnp.concatenate([carry[h][:, :q0], m_new[h]], axis=1))
            return tuple(out)

        fin = lax.fori_loop(0, qi, lambda kb, c: tile(kb, tab_ref[qi - kb] if off_diag_bias else None, 0, tk, 0, c),
                            tuple(jnp.full((1, tq), NEG, F32) for _ in heads))
        half = tk // 2
        fin = tile(qi, tab_ref[0, 0:half, :], 0, half, 0, fin)
        fin = tile(qi, tab_ref[0, half:, half:], half, half, half, fin)
        outs = []
        for h in range(HEADS_PER_STEP):
            l = acc_s[h][HEAD_DIM:HEAD_DIM + 1, :]
            outs.append(acc_s[h][0:HEAD_DIM, :] / l)
            lse_ref[0, h, 0] = fin[h] + jnp.log(l)
        o_ref[...] = jnp.concatenate(outs, axis=0).T

    def seq_spec(off):
        return pl.BlockSpec((s, LANES), lambda b, j, i: (b, off + j))

    in_specs = [pl.BlockSpec((tq, LANES), lambda b, j, i: (b * nq + i, q_off + j)), seq_spec(k_off), seq_spec(v_off),
                pl.BlockSpec(table.shape, lambda b, j, i: (0, 0, 0))]
    args = [q_arr, k_arr, v_arr, table]
    if use_cb:
        in_specs.append(pl.BlockSpec((1, HEADS_PER_STEP, nk, 1, tk), lambda b, j, i: (b, j, 0, 0, 0)))
        args.append(colbias)
    n_heads = npairs * HEADS_PER_STEP
    return _pcall(
        body, side=side, name=name, grid=(nb, npairs, nq), in_specs=in_specs,
        out_specs=[pl.BlockSpec((tq, LANES), lambda b, j, i: (b * nq + i, j)),
                   pl.BlockSpec((1, HEADS_PER_STEP, 1, 1, tq), lambda b, j, i: (b, j, i, 0, 0))],
        out_shape=[jax.ShapeDtypeStruct((t, npairs * LANES), F32), jax.ShapeDtypeStruct((nb, n_heads, nq, 1, tq), F32)],
        scratch_shapes=[pltpu.VMEM((nk, HEADS_PER_STEP, HEAD_DIM + ONES_ROWS, tk), BF16)]
        + [pltpu.VMEM((HEAD_DIM + ONES_ROWS, tq), F32)] * HEADS_PER_STEP
        + ([pltpu.VMEM((nk, tk, LANES), F32)] if use_cb else []),
        compiler_params=_params(("arbitrary", "arbitrary", "arbitrary")),
    )(*args)


def attn_bwd(q_arr, q_off, k_arr, k_off, v_arr, v_off, o_arr, lse_arr, do_arr, table, colbias, nb, name, side=None,
             rope_tabs=None, off_diag_bias=True):
    t = q_arr.shape[0]
    s = t // nb
    tk, tq = table.shape[1:]
    assert tq == tk, "the diagonal handling below is written for square tiles"
    nq, nk = s // tq, s // tk
    npairs = WIDTH_A // LANES
    use_cb = colbias is not None

    def body(*refs):
        refs = list(refs)
        q_ref, k_ref, v_ref, o_ref, lse_ref, do_ref, tab_ref = refs[:7]
        pos = 7
        cb_ref = None
        if use_cb:
            cb_ref = refs[pos]
            pos += 1
        rope_refs = None
        if rope_tabs is not None:
            rope_refs = refs[pos:pos + 3]
            pos += 3
        dq_ref, dk_ref, dv_ref = refs[pos:pos + 3]
        pos += 3
        dcb_ref = drow_ref = None
        if use_cb:
            dcb_ref, drow_ref = refs[pos:pos + 2]
            pos += 2
        kt_s, dkt_s, dvt_s = refs[pos:pos + 3]
        dqt_s = refs[pos + 3:pos + 3 + HEADS_PER_STEP]
        dcb_s, cbc_s = refs[pos + 3 + HEADS_PER_STEP:pos + 5 + HEADS_PER_STEP] if use_cb else (None, None)

        heads = [slice(h * HEAD_DIM, (h + 1) * HEAD_DIM) for h in range(HEADS_PER_STEP)]
        for cblk in range(nk):
            kt_s[cblk] = k_ref[cblk * tk:(cblk + 1) * tk, :].astype(F32).T.astype(BF16)
        dkt_s[...] = jnp.zeros_like(dkt_s)
        dvt_s[...] = jnp.zeros_like(dvt_s)
        if use_cb:
            dcb_s[...] = jnp.zeros_like(dcb_s)
            for cblk in range(nk):
                cbc_s[cblk] = _rows_to_cols([cb_ref[0, h, cblk] for h in range(HEADS_PER_STEP)])
        ones = jnp.ones((8, HEAD_DIM), BF16)

        def q_loop(qi, carry):
            qs = pl.multiple_of(qi * tq, tq)
            q_all = (q_ref[pl.ds(qs, tq), :].astype(F32) * ATTN_SCALE)
            do_all = do_ref[pl.ds(qs, tq), :]
            qt_all = q_all.T.astype(BF16)
            dot_all = do_all.T.astype(BF16)
            qt, dot, lse, dsum = [], [], [], []
            for h, hs in enumerate(heads):
                qt.append(qt_all[hs, :])
                dot.append(dot_all[hs, :])
                lse.append(lse_ref[0, h, qi])
                prod = do_all[:, hs] * o_ref[pl.ds(qs, tq), hs]
                hi = prod.astype(BF16)
                lo = (prod - hi.astype(F32)).astype(BF16)
                dsum.append((lax.dot_general(ones, hi, NT_DIMS, preferred_element_type=F32)
                             + lax.dot_general(ones, lo, NT_DIMS, preferred_element_type=F32))[0:1, :])
            for a in dqt_s:
                a[...] = jnp.zeros_like(a)

            def tile(kb, tab, k0, klen, q0, drow):
                ks = pl.multiple_of(kb * tk + k0, klen)
                keys = slice(k0, k0 + klen)
                sts, dpts, out = [], [], []
                for h, hs in enumerate(heads):
                    st = jnp.dot(k_ref[pl.ds(ks, klen), hs], qt[h][:, q0:], preferred_element_type=F32)
                    if tab is not None:
                        st = st + tab
                    if use_cb:
                        st = st + cbc_s[kb, keys, h * HEAD_DIM:h * HEAD_DIM + 1]
                    sts.append(st)
                    dpts.append(jnp.dot(v_ref[pl.ds(ks, klen), hs], dot[h][:, q0:], preferred_element_type=F32))
                for h, hs in enumerate(heads):
                    pt = jnp.exp(sts[h] - lse[h][:, q0:])
                    dst = pt * (dpts[h] - dsum[h][:, q0:])
                    dst_b = dst.astype(BF16)
                    dvt_s[h, kb, :, keys] += lax.dot_general(dot[h][:, q0:], pt.astype(BF16), NT_DIMS,
                                                             preferred_element_type=F32)
                    dkt_s[h, kb, :, keys] += lax.dot_general(qt[h][:, q0:], dst_b, NT_DIMS, preferred_element_type=F32)
                    dqt_s[h][:, q0:] += jnp.dot(kt_s[kb, hs, keys], dst_b, preferred_element_type=F32)
                    if use_cb:
                        dcb_s[h, pl.ds(ks, klen), :] += jnp.sum(dst, axis=1, keepdims=True)
                        dr = drow[h][:, q0:] + jnp.sum(dst, axis=0, keepdims=True)
                        out.append(dr if q0 == 0 else jnp.concatenate([drow[h][:, :q0], dr], axis=1))
                    else:
                        out.append(drow[h])
                return tuple(out)

            drow = lax.fori_loop(0, qi, lambda kb, c: tile(kb, tab_ref[qi - kb] if off_diag_bias else None, 0, tk, 0, c),
                                 tuple(jnp.zeros((1, tq), F32) for _ in heads))
            half = tk // 2
            drow = tile(qi, tab_ref[0, 0:half, :], 0, half, 0, drow)
            drow = tile(qi, tab_ref[0, half:, half:], half, half, half, drow)
            dq = (jnp.concatenate([a[...] for a in dqt_s], axis=0) * ATTN_SCALE).T
            if rope_refs is not None:
                dq = _rotate(dq, *[coef[pl.ds(qs, tq), :] for coef in rope_refs], True)
            dq_ref[pl.ds(qs, tq), :] = dq.astype(dq_ref.dtype)
            if use_cb:
                for h in range(HEADS_PER_STEP):
                    drow_ref[0, h, qi] = drow[h]
            return carry

        lax.fori_loop(0, nq, q_loop, 0)
        for cblk in range(nk):
            rows = slice(cblk * tk, (cblk + 1) * tk)
            dk = jnp.concatenate([dkt_s[h, cblk] for h in range(HEADS_PER_STEP)], axis=0).T
            if rope_refs is not None:
                dk = _rotate(dk, *[coef[rows, :] for coef in rope_refs], True)
            dk_ref[rows, :] = dk.astype(dk_ref.dtype)
            dv_ref[rows, :] = jnp.concatenate([dvt_s[h, cblk] for h in range(HEADS_PER_STEP)], axis=0).T.astype(dv_ref.dtype)
            if use_cb:
                for h in range(HEADS_PER_STEP):
                    dcb_ref[0, h, cblk] = jnp.broadcast_to(dcb_s[h, rows, :], (tk, LANES)).T[0:1, :]

    def seq_spec(off):
        return pl.BlockSpec((s, LANES), lambda b, j: (b, off + j))

    row_spec = pl.BlockSpec((1, HEADS_PER_STEP, nq, 1, tq), lambda b, j: (b, j, 0, 0, 0))
    in_specs = [seq_spec(q_off), seq_spec(k_off), seq_spec(v_off), seq_spec(0), row_spec, seq_spec(0),
                pl.BlockSpec(table.shape, lambda b, j: (0, 0, 0))]
    args = [q_arr, k_arr, v_arr, o_arr, lse_arr, do_arr, table]
    width = npairs * LANES
    out_specs = [seq_spec(0)] * 3
    out_shape = [jax.ShapeDtypeStruct((t, width), BF16)] * 3
    scratch = [pltpu.VMEM((nk, LANES, tk), BF16), pltpu.VMEM((HEADS_PER_STEP, nk, HEAD_DIM, tk), F32),
               pltpu.VMEM((HEADS_PER_STEP, nk, HEAD_DIM, tk), F32)] + [pltpu.VMEM((HEAD_DIM, tq), F32)] * HEADS_PER_STEP
    if use_cb:
        cb_spec = pl.BlockSpec((1, HEADS_PER_STEP, nk, 1, tk), lambda b, j: (b, j, 0, 0, 0))
        in_specs.append(cb_spec)
        args.append(colbias)
    if rope_tabs is not None:
        in_specs += [pl.BlockSpec((s, LANES), lambda b, j: (b, 0))] * 3
        args += list(rope_tabs)
    if use_cb:
        out_specs += [cb_spec, row_spec]
        out_shape += [jax.ShapeDtypeStruct(colbias.shape, F32), jax.ShapeDtypeStruct(lse_arr.shape, F32)]
        scratch += [pltpu.VMEM((HEADS_PER_STEP, s, 1), F32), pltpu.VMEM((nk, tk, LANES), F32)]
    return _pcall(
        body, side=side, name=name, grid=(nb, npairs), in_specs=in_specs, out_specs=out_specs, out_shape=out_shape,
        scratch_shapes=scratch, compiler_params=_params(("arbitrary", "arbitrary")),
    )(*args)


def ada_fwd(c_all, w_ada, b_cols, name):
    def body(c_ref, w_ref, b_ref, o_ref):
        cv = c_ref[...]
        sc = (cv * _sigmoid(cv)).astype(BF16)
        o_ref[...] = jnp.dot(sc, w_ref[...].astype(BF16), preferred_element_type=F32) + b_ref[...]

    return _pcall(body, name=name, out_shape=jax.ShapeDtypeStruct((c_all.shape[0], w_ada.shape[1]), F32),
                  compiler_params=_params())(c_all, w_ada, b_cols)


def ada_bwd(c_all, dmod_cols, name):
    def body(c_ref, d_ref, o_ref):
        cv = c_ref[...]
        sc = (cv * _sigmoid(cv)).astype(BF16)
        o_ref[...] = lax.dot_general(sc, d_ref[...].astype(BF16), TN_DIMS, preferred_element_type=F32)

    return _pcall(body, name=name, out_shape=jax.ShapeDtypeStruct((c_all.shape[1], dmod_cols.shape[1]), F32),
                  compiler_params=_params())(c_all, dmod_cols)


def adamw(parts, group, w, m, v, name, tr=None):
    n = parts.shape[0]
    r, c = w.shape
    tr = r if tr is None else tr
    c1 = 1.0 - ADAM_B1 ** ADAM_STEP
    c2 = 1.0 - ADAM_B2 ** ADAM_STEP

    def body(p_ref, w_ref, m_ref, v_ref, g_ref, d_ref, nm_ref, nv_ref):
        g = p_ref[0, 0].astype(F32)
        for i in range(1, n):
            g = g + p_ref[i, 0].astype(F32)
        wv = w_ref[...]
        nm = ADAM_B1 * m_ref[...] + (1.0 - ADAM_B1) * g
        nv = ADAM_B2 * v_ref[...] + (1.0 - ADAM_B2) * (g * g)
        g_ref[...] = g
        nm_ref[...] = nm
        nv_ref[...] = nv
        d_ref[...] = -ADAM_LR * ((nm / c1) / (jnp.sqrt(nv / c2) + ADAM_EPS) + ADAM_WD * wv)

    spec = pl.BlockSpec((tr, c), lambda i: (i, 0))
    shape = jax.ShapeDtypeStruct((r, c), F32)
    return _pcall(
        body, name=name, grid=(r // tr,),
        in_specs=[pl.BlockSpec((n, 1, tr, c), lambda i: (0, group, i, 0)), spec, spec, spec],
        out_specs=[spec] * 4, out_shape=[shape] * 4, compiler_params=_params(("arbitrary",)),
    )(parts, w, m, v)


def all_gather(arrs, name):
    n = len(arrs)
    hbm = pl.BlockSpec(memory_space=pl.ANY)

    def body(*refs):
        ins, outs = refs[:n], refs[n:2 * n]
        send_sems, recv_sems, local_sems = refs[2 * n:]
        x, y, c = _place()
        me, sibling = (x, y, c), (x, y, 1 - c)
        chips = [(1 - x, y), (x, 1 - y), (1 - x, 1 - y)]

        def copy(a, k, block, to, src=None):
            dst = outs[a].at[_slot(block)]
            return pltpu.make_async_remote_copy(
                src_ref=dst if src is None else src, dst_ref=dst, send_sem=send_sems.at[a * 7 + k],
                recv_sem=recv_sems.at[a * 7 + k], device_id=to, device_id_type=MESH)

        mine = [pltpu.make_async_copy(ins[a], outs[a].at[_slot(me)], local_sems.at[a]) for a in range(n)]
        for cp in mine:
            cp.start()
        first = []
        for a in range(n):
            first.append(copy(a, 0, me, sibling, src=ins[a]))
            first += [copy(a, 1 + j, me, (*chip, c), src=ins[a]) for j, chip in enumerate(chips)]
        for cp in first:
            cp.start()
        passed = []
        for a in range(n):
            for j, chip in enumerate(chips):
                copy(a, 1 + j, (*chip, c), me).wait_recv()
                cp = copy(a, 4 + j, (*chip, c), sibling)
                cp.start()
                passed.append(cp)
        for a in range(n):
            copy(a, 0, sibling, me).wait_recv()
            for j, chip in enumerate(chips):
                copy(a, 4 + j, (*chip, 1 - c), me).wait_recv()
        for cp in first + passed:
            cp.wait_send()
        for cp in mine:
            cp.wait()

    return _pcall(
        body, name=name, in_specs=[hbm] * n, out_specs=[hbm] * n,
        out_shape=[jax.ShapeDtypeStruct((N_DEV,) + a.shape, a.dtype) for a in arrs],
        scratch_shapes=[pltpu.SemaphoreType.DMA((7 * n,)), pltpu.SemaphoreType.DMA((7 * n,)),
                        pltpu.SemaphoreType.DMA((n,))],
        compiler_params=pltpu.CompilerParams(has_side_effects=True),
    )(*arrs)


def _t(w):
    return jnp.swapaxes(w, -1, -2)


def _rows_from_blocks(blocks, pad_to=None):
    full = blocks.reshape(-1, blocks.shape[2])
    if pad_to is not None and pad_to > full.shape[0]:
        full = jnp.pad(full, ((0, pad_to - full.shape[0]), (0, 0)))
    return full


def _rows_to_blocks(full, nrows):
    return full[:nrows].reshape(N_DEV, nrows // N_DEV, full.shape[1])


SMALL_ORDER = ("g_pre_ff1", "g_post_ff1", "g_pre_mix", "g_post_mix", "g_out_a", "g_out_b", "g_pre_ff2", "g_post_ff2",
               "b_forget")


def _pack_small(vals):
    rows = []
    for name in SMALL_ORDER:
        v = vals[name].reshape(1, -1)
        if v.shape[1] % LANES:
            v = jnp.pad(v, ((0, 0), (0, LANES - v.shape[1] % LANES)))
        rows.append(v)
    return jnp.concatenate(rows, axis=1)


def _unpack_small(row, sizes):
    out, pos = {}, 0
    for name in SMALL_ORDER:
        n = sizes[name]
        out[name] = row[:, pos:pos + n]
        pos += -(-n // LANES) * LANES
    return out


def _ffn_forward(x, mod, g_pre, g_post, wg, wu, wd, i0, nb, tag, target=None, side=None, side_down=None):
    h = prenorm_fwd(x, g_pre, mod, i0, i0 + 1, nb, f"{tag}_prenorm")
    res, side_out = ffn_up(h, wg, wu, f"{tag}_up", side=side), None
    if side is not None:
        res, side_out = res
    gate, up, act = res
    if callable(wd):
        wd = wd(side_out)
    res, side_down_out = postnorm_fwd(x, [(act, wd)], g_post, mod, i0 + 2, 0.5, nb, f"{tag}_down_postnorm",
                                      target=target, side=side_down), None
    if side_down is not None:
        res, side_down_out = res
    out, y0 = (res[0] if target is None else tuple(res[:2])), res[-1]
    return out, (x, h, gate, up, act, y0), wd, side_out, side_down_out


def _ffn_backward(dxo, saved, mod, g_pre, g_post, wg, wu, wd, i0, nb, tag, exchange=False, also=()):
    x, h, gate, up, act, y0 = saved
    blocks = lambda g: _rows_to_blocks(g, D_FF)[:, None]
    dy0, dg_post, dgate_mod = postnorm_bwd(dxo, y0, g_post, mod, i0 + 2, 0.5, nb, f"{tag}_postnorm_bwd")
    dwd = mm_tn(act, dy0, BF16, f"{tag}_dwd", rows=D_FF)
    res, extra = ffn_down_bwd(dy0, wd, gate, up, f"{tag}_down_bwd",
                              side=([blocks(dwd)] + list(also), False) if exchange else None), []
    if exchange:
        res, (dwd, *extra) = res
    dgate, dup = res
    dwg = mm_tn(dgate, h, BF16, f"{tag}_dwg", rows=D_FF)
    dwu = mm_tn(dup, h, BF16, f"{tag}_dwu", rows=D_FF)
    res = prenorm_bwd([(dgate, wg), (dup, wu)], x, g_pre, mod, i0 + 1, dxo, nb, f"{tag}_dh_prenorm_bwd", ts=DH_ROWS,
                      side=([blocks(dwg), blocks(dwu)], False) if exchange else None)
    if exchange:
        res, (dwg, dwu) = res
    dx, dg_pre, dsc, dsh = res
    return dx, dict(g_pre=dg_pre, g_post=dg_post, wg=dwg, wu=dwu, wd=dwd, mod=(dsh, dsc, dgate_mod)), extra


def kernel(x, c, positions, w_ada, b_ada, g_pre_ff1, g_post_ff1, w_ff1_gate, w_ff1_up, w_ff1_down, g_pre_mix, g_post_mix, w_in, b_forget, g_out_a, g_out_b, w_out, g_pre_ff2, g_post_ff2, w_ff2_gate, w_ff2_up, w_ff2_down, loss_target, m_w_ada, m_b_ada, m_g_pre_ff1, m_g_post_ff1, m_w_ff1_gate, m_w_ff1_up, m_w_ff1_down, m_g_pre_mix, m_g_post_mix, m_w_in, m_b_forget, m_g_out_a, m_g_out_b, m_w_out, m_g_pre_ff2, m_g_post_ff2, m_w_ff2_gate, m_w_ff2_up, m_w_ff2_down, v_w_ada, v_b_ada, v_g_pre_ff1, v_g_post_ff1, v_w_ff1_gate, v_w_ff1_up, v_w_ff1_down, v_g_pre_mix, v_g_post_mix, v_w_in, v_b_forget, v_g_out_a, v_g_out_b, v_w_out, v_g_pre_ff2, v_g_post_ff2, v_w_ff2_gate, v_w_ff2_up, v_w_ff2_down):
    weights = dict(w_ada=w_ada, b_ada=b_ada, g_pre_ff1=g_pre_ff1, g_post_ff1=g_post_ff1, w_ff1_gate=w_ff1_gate,
                   w_ff1_up=w_ff1_up, w_ff1_down=w_ff1_down, g_pre_mix=g_pre_mix, g_post_mix=g_post_mix, w_in=w_in,
                   b_forget=b_forget, g_out_a=g_out_a, g_out_b=g_out_b, w_out=w_out, g_pre_ff2=g_pre_ff2,
                   g_post_ff2=g_post_ff2, w_ff2_gate=w_ff2_gate, w_ff2_up=w_ff2_up, w_ff2_down=w_ff2_down)
    mom_m = dict(w_ada=m_w_ada, b_ada=m_b_ada, g_pre_ff1=m_g_pre_ff1, g_post_ff1=m_g_post_ff1, w_ff1_gate=m_w_ff1_gate,
                 w_ff1_up=m_w_ff1_up, w_ff1_down=m_w_ff1_down, g_pre_mix=m_g_pre_mix, g_post_mix=m_g_post_mix,
                 w_in=m_w_in, b_forget=m_b_forget, g_out_a=m_g_out_a, g_out_b=m_g_out_b, w_out=m_w_out,
                 g_pre_ff2=m_g_pre_ff2, g_post_ff2=m_g_post_ff2, w_ff2_gate=m_w_ff2_gate, w_ff2_up=m_w_ff2_up,
                 w_ff2_down=m_w_ff2_down)
    mom_v = dict(w_ada=v_w_ada, b_ada=v_b_ada, g_pre_ff1=v_g_pre_ff1, g_post_ff1=v_g_post_ff1, w_ff1_gate=v_w_ff1_gate,
                 w_ff1_up=v_w_ff1_up, w_ff1_down=v_w_ff1_down, g_pre_mix=v_g_pre_mix, g_post_mix=v_g_post_mix,
                 w_in=v_w_in, b_forget=v_b_forget, g_out_a=v_g_out_a, g_out_b=v_g_out_b, w_out=v_w_out,
                 g_pre_ff2=v_g_pre_ff2, g_post_ff2=v_g_post_ff2, w_ff2_gate=v_w_ff2_gate, w_ff2_up=v_w_ff2_up,
                 w_ff2_down=v_w_ff2_down)
    order = list(weights)

    nb, s, d = x.shape
    t = nb * s
    me = _slot(_place())
    nbg = nb * N_DEV
    ada_cols = w_ada.shape[2]

    bf = lambda w: w[0].astype(BF16)
    bft = lambda w: _t(w)[0].astype(BF16)
    c_all, wg1, wu1 = all_gather([c, bft(w_ff1_gate), bft(w_ff1_up)], "gather_ff1")
    c_all = c_all.reshape(nbg, d)
    wg1, wu1 = (_rows_from_blocks(w, D_FF_PAD) for w in (wg1, wu1))

    b_cols = lax.dynamic_slice(b_ada, (0, me * ada_cols), (1, ada_cols))
    mod_cols = ada_fwd(c_all, w_ada[0], b_cols, "ada_fwd")
    (mod_all,) = all_gather([mod_cols], "gather_mod")
    mod = lax.dynamic_slice(mod_all, (0, me * nb, 0), (N_DEV, nb, ada_cols))
    mod = mod.transpose(1, 0, 2).reshape(nb, N_MOD, d)

    xf = x.reshape(t, d)
    target = loss_target.reshape(t, d)

    x1, saved1, wd1, (_, w_in_all, w_out_all) = _ffn_forward(
        xf, mod, g_pre_ff1, g_post_ff1, wg1, wu1, lambda got: _rows_from_blocks(got[0], D_FF_PAD), 0, nb, "ff1",
        side=([bf(w_ff1_down), bft(w_in), bf(w_out)], "two_level"))[:4]
    w_in_t = _rows_from_blocks(w_in_all)
    n_qkv = 3 * (WIDTH_A + WIDTH_B)
    w_qkv_t = w_in_t[:n_qkv]
    w_f_t = jnp.pad(w_in_t[n_qkv:], ((0, LANES - N_HEADS_B), (0, 0)))
    w_o = _rows_from_blocks(w_out_all)
    w_o_a, w_o_b = w_o[:WIDTH_A], w_o[WIDTH_A:]

    h2 = prenorm_fwd(x1, g_pre_mix, mod, 3, 4, nb, "mix_prenorm")
    tables = rope_tables(positions)
    proj = mm_rows([(h2, w_qkv_t)], True, BF16, "mix_proj", rope=(tables, 2 * WIDTH_A))
    f_logit = mm_rows([(h2, w_f_t)], True, F32, "mix_forget")
    tab_a = dilated_table(s, ATTN_TQ, ATTN_TK)
    tab_b = causal_table(s, ATTN_TQ, ATTN_TK)
    ft = f_logit[:, :N_HEADS_B].reshape(nb, s, N_HEADS_B).transpose(0, 2, 1)
    bf_col = b_forget.reshape(N_HEADS_B, 1)
    colbias = fox_gate_fwd(ft, bf_col, "fox_gate").reshape(nb, N_HEADS_B, s // ATTN_TK, 1, ATTN_TK)
    pa = WIDTH_A // LANES
    (o_a, lse_a), ff2_all = attn_fwd(
        proj, 0, proj, pa, proj, 2 * pa, tab_a, None, nb, "attn_a",
        side=([bft(w_ff2_gate), bft(w_ff2_up), bf(w_ff2_down)], "two_level"))
    wg2, wu2, wd2 = (_rows_from_blocks(w, D_FF_PAD) for w in ff2_all)
    o_b, lse_b = attn_fwd(proj, 3 * pa, proj, 4 * pa, proj, 5 * pa, tab_b, colbias, nb, "attn_b", off_diag_bias=False)
    m_a = prenorm_fwd(o_a, g_out_a, None, None, None, nb, "out_norm_a")
    m_b = prenorm_fwd(o_b, g_out_b, None, None, None, nb, "out_norm_b")
    x2, y0m = postnorm_fwd(x1, [(m_a, w_o_a), (m_b, w_o_b)], g_post_mix, mod, 5, 1.0, nb, "mix_out_postnorm")

    (dx3, loss_part), saved2 = _ffn_forward(x2, mod, g_pre_ff2, g_post_ff2, wg2, wu2, wd2, 6, nb, "ff2", target=target)[:2]
    loss = lax.psum(loss_part[0, 0], ("x", "y", "c"))

    dx2, gr2, _ = _ffn_backward(dx3, saved2, mod, g_pre_ff2, g_post_ff2, wg2, wu2, wd2, 6, nb, "ff2")
    ff2_blocks = [_rows_to_blocks(gr2[k], D_FF)[:, None] for k in ("wg", "wu", "wd")]

    dy0m, dg_post_mix, dgate_mix = postnorm_bwd(dx2, y0m, g_post_mix, mod, 5, 1.0, nb, "mix_postnorm_bwd")
    dw_o = mm_tn_stack([m_a, m_b], dy0m, [WIDTH_A, WIDTH_B], BF16, "mix_dwo")
    do_a, dg_out_a = prenorm_bwd([(dy0m, w_o_a.T)], o_a, g_out_a, None, None, None, nb, "out_norm_a_bwd")
    do_b, dg_out_b = prenorm_bwd([(dy0m, w_o_b.T)], o_b, g_out_b, None, None, None, nb, "out_norm_b_bwd")
    (dq_a, dk_a, dv_a), g_ff2 = attn_bwd(proj, 0, proj, pa, proj, 2 * pa, o_a, lse_a, do_a, tab_a, None, nb,
                                            "attn_a_bwd", side=(ff2_blocks, False), rope_tabs=tables)
    (dq_b, dk_b, dv_b, dcb, drow), (g_out,) = attn_bwd(
        proj, 3 * pa, proj, 4 * pa, proj, 5 * pa, o_b, lse_b, do_b, tab_b, colbias, nb, "attn_b_bwd",
        off_diag_bias=False, side=([_rows_to_blocks(dw_o, d)[:, None]], False))
    dz_t, db_forget = fox_gate_bwd(dcb.reshape(nb, N_HEADS_B, s), drow.reshape(nb, N_HEADS_B, s), ft, bf_col,
                                   "fox_gate_bwd")
    dz = jnp.pad(dz_t.transpose(0, 2, 1).reshape(t, N_HEADS_B), ((0, 0), (0, LANES - N_HEADS_B))).astype(BF16)
    pieces = [dq_a, dk_a, dv_a, dq_b, dk_b, dv_b]
    w_pieces = [w_qkv_t[i * WIDTH_A:(i + 1) * WIDTH_A] for i in range(6)]
    dh2_pairs = list(zip(pieces, w_pieces)) + [(dz, w_f_t)]
    dw_in_t = mm_tn_stack(pieces + [dz], h2, [WIDTH_A] * 6 + [N_HEADS_B], BF16, "mix_dwin")
    g_in = _rows_to_blocks(dw_in_t, dw_in_t.shape[0])[:, None]
    (dx1, dg_pre_mix, dsc_mix, dsh_mix), (g_in,) = prenorm_bwd(
        dh2_pairs, x1, g_pre_mix, mod, 4, dx2, nb, "mix_dh_prenorm_bwd", ts=DH_ROWS, side=([g_in], False))
    dx0, gr1, _ = _ffn_backward(dx1, saved1, mod, g_pre_ff1, g_post_ff1, wg1, wu1, wd1, 0, nb, "ff1", exchange=True)
    grad_x = dx0.reshape(nb, s, d)

    dmod =jnp.concatenate(list(gr1["mod"]) + [dsh_mix, dsc_mix, dgate_mix] + list(gr2["mod"]), axis=1)
    small = _pack_small(dict(g_pre_ff1=gr1["g_pre"], g_post_ff1=gr1["g_post"], g_pre_mix=dg_pre_mix,
                             g_post_mix=dg_post_mix, g_out_a=dg_out_a, g_out_b=dg_out_b, g_pre_ff2=gr2["g_pre"],
                             g_post_ff2=gr2["g_post"], b_forget=db_forget))
    dmod_all, small_all = all_gather([dmod.reshape(nb, N_MOD * d), small], "gather_small_grads")
    dmod_all = dmod_all.reshape(nbg, N_MOD * d)

    res = {}
    def adamw_t(parts, group, n):
        return tuple(_t(r) for r in adamw(parts, group, _t(weights[n])[0], _t(mom_m[n])[0], _t(mom_v[n])[0], f"adamw_{n}"))

    res["w_ff1_gate"] = adamw_t(gr1["wg"], 0, "w_ff1_gate")
    res["w_ff1_up"] = adamw_t(gr1["wu"], 0, "w_ff1_up")
    res["w_ff2_gate"] = adamw_t(g_ff2[0], 0, "w_ff2_gate")
    res["w_ff2_up"] = adamw_t(g_ff2[1], 0, "w_ff2_up")
    res["w_ff1_down"] = adamw(gr1["wd"], 0, w_ff1_down[0], m_w_ff1_down[0], v_w_ff1_down[0], "adamw_ff1_down")
    res["w_ff2_down"] = adamw(g_ff2[2], 0, w_ff2_down[0], m_w_ff2_down[0], v_w_ff2_down[0], "adamw_ff2_down")
    res["w_in"] = adamw_t(g_in, 0, "w_in")
    res["w_out"] = adamw(g_out, 0, w_out[0], m_w_out[0], v_w_out[0], "adamw_out")
    dmod_cols = lax.dynamic_slice(dmod_all, (0, me * ada_cols), (nbg, ada_cols))
    dw_ada = ada_bwd(c_all, dmod_cols, "ada_bwd")
    res["w_ada"] = adamw(dw_ada[None, None], 0, w_ada[0], m_w_ada[0], v_w_ada[0], "adamw_ada", tr=256)
    res["b_ada"] = adamw(dmod_all[:, None, None], 0, b_ada, m_b_ada, v_b_ada, "adamw_b_ada")
    sizes = {n: weights[n].shape[1] for n in SMALL_ORDER}
    small_res = adamw(small_all[:, None], 0, _pack_small(weights), _pack_small(mom_m), _pack_small(mom_v), "adamw_small")
    small_res = [_unpack_small(r, sizes) for r in small_res]
    for n in SMALL_ORDER:
        res[n] = tuple(r[n] for r in small_res)

    outs = [loss, grad_x]
    for kind in range(4):
        for n in order:
            a = res[n][kind]
            outs.append(a.reshape(weights[n].shape))
    return tuple(outs)
```
